```python
import math
import jax, jax.numpy as jnp
from jax import lax
import numpy as np

D_MODEL = 2048
BATCH = 16
SEQ = 2048
DEPTH = 2

HEAD_DIM = 64
N_MIXERS = 4
GROUP_WIDTH = D_MODEL // N_MIXERS
MIX_WIDTH = N_MIXERS * GROUP_WIDTH
SGU_GROUPS = GROUP_WIDTH // HEAD_DIM
SGU_CHUNK = 128
DIL_HEADS = GROUP_WIDTH // HEAD_DIM
DIL_PATTERNS = ((128, 1), (512, 4), (2048, 16))
CONV_CH = GROUP_WIDTH
CONV_WIDTH = 31
GQA_Q_HEADS = GROUP_WIDTH // HEAD_DIM
GQA_KV_HEADS = GQA_Q_HEADS // 4
KV_WIDTH = GQA_KV_HEADS * HEAD_DIM
Q_BLOCK = 128
GRID_W = 64
ROPE_THETA = 10000.0
REL_BUCKETS = 32
REL_MAX_DIST = 1024
FFN_HIDDEN = ((8 * D_MODEL + 3 * 256 - 1) // (3 * 256)) * 256
IN_SIZES = (GROUP_WIDTH, GROUP_WIDTH,
            GROUP_WIDTH, GROUP_WIDTH, GROUP_WIDTH,
            CONV_CH, CONV_CH,
            GROUP_WIDTH, KV_WIDTH, KV_WIDTH)
IN_WIDTH = sum(IN_SIZES)
RMS_EPS = 1e-6
LN_EPS = 1e-5

kernel_name = "hymba_style_hybrid_encoder_block"


def rms_norm(x, g):
    xf = x.astype(jnp.float32)
    y = xf * lax.rsqrt(jnp.mean(xf * xf, axis=-1, keepdims=True) + RMS_EPS)
    return (y * g.astype(jnp.float32)).astype(x.dtype)


def layer_norm_stats(x):
    xf = x.astype(jnp.float32)
    mu = jnp.mean(xf, axis=-1, keepdims=True)
    xc = xf - mu
    return xc * lax.rsqrt(jnp.mean(xc * xc, axis=-1, keepdims=True) + LN_EPS)


def split_heads(t):
    return t.reshape(t.shape[0], t.shape[1], -1, HEAD_DIM)


def t5_buckets(rel):
    nb = REL_BUCKETS // 2
    max_exact = nb // 2
    ret = jnp.where(rel > 0, nb, 0)
    n = jnp.abs(rel)
    nf = jnp.maximum(n, 1).astype(jnp.float32)
    large = max_exact + (jnp.log(nf / max_exact) / math.log(REL_MAX_DIST / max_exact)
                         * (nb - max_exact)).astype(jnp.int32)
    large = jnp.minimum(large, nb - 1)
    return ret + jnp.where(n < max_exact, n, large)


def sgu_branch(u, v, w_s, b_s):
    bn, s, w = u.shape
    nc = s // SGU_CHUNK
    u = jax.nn.gelu(u)
    v = jax.nn.gelu(v).reshape(bn, nc, SGU_CHUNK, SGU_GROUPS, HEAD_DIM)
    vn = layer_norm_stats(v).astype(u.dtype)
    mixed = jnp.einsum('gpq,bcqgd->bcpgd', w_s, vn) + b_s.T[None, None, :, :, None]
    return u * mixed.reshape(bn, s, w)


def dilated_pattern(q, k, v, rel_table, window, dil):
    bn, s, h, dh = q.shape
    half = window // (2 * dil)
    blk = half
    L = s // dil
    nb = -(-L // blk)
    lp = nb * blk

    def to_sub(t):
        return t.reshape(bn, L, dil, h, dh).transpose(0, 2, 1, 3, 4)

    qs = jnp.pad(to_sub(q), ((0, 0), (0, 0), (0, lp - L), (0, 0), (0, 0)))
    qs = qs.reshape(bn, dil, nb, blk, h, dh)
    pad_kv = ((0, 0), (0, 0), (blk, lp - L + blk), (0, 0), (0, 0))

    def band(t):
        t = jnp.pad(to_sub(t), pad_kv).reshape(bn, dil, nb + 2, blk, h, dh)
        return jnp.concatenate([t[:, :, :-2], t[:, :, 1:-1], t[:, :, 2:]], axis=3)

    kb, vb = band(k), band(v)
    sc = jnp.einsum('brnqhd,brnkhd->brnhqk', qs, kb, preferred_element_type=jnp.float32)
    off = jnp.arange(3 * blk)[None, :] - blk - jnp.arange(blk)[:, None]
    key_idx = jnp.arange(nb)[:, None] * blk - blk + jnp.arange(3 * blk)[None, :]
    valid = (jnp.abs(off) <= half)[None] & ((key_idx >= 0) & (key_idx < L))[:, None, :]
    bias = rel_table[t5_buckets(off * dil)].astype(jnp.float32).transpose(2, 0, 1)
    sc = sc + bias[None, None, None]
    sc = jnp.where(valid[None, None, :, None], sc, -1e30)
    lse = jax.nn.logsumexp(sc, axis=-1)
    p = jnp.exp(sc - lse[..., None])
    o = jnp.einsum('brnhqk,brnkhd->brnqhd', p.astype(v.dtype), vb)
    o = o.reshape(bn, dil, lp, h, dh)[:, :, :L].transpose(0, 2, 1, 3, 4).reshape(bn, s, h, dh)
    lse = lse.transpose(0, 1, 2, 4, 3).reshape(bn, dil, lp, h)[:, :, :L]
    lse = lse.transpose(0, 2, 1, 3).reshape(bn, s, h)
    return o, lse


def dilated_mixture(q, k, v, rel_table):
    outs, lses = [], []
    for window, dil in DIL_PATTERNS:
        o, lse = dilated_pattern(q, k, v, rel_table, window, dil)
        outs.append(o)
        lses.append(lse)
    w = jax.nn.softmax(jnp.stack(lses, axis=0), axis=0)
    return jnp.einsum('gbsh,gbshd->bshd', w.astype(v.dtype), jnp.stack(outs, axis=0))


def conv_branch(a, gate, w_dw, b_dw, ln_g, ln_b):
    hdn = a * jax.nn.sigmoid(gate)
    pad = CONV_WIDTH // 2
    hdn = lax.conv_general_dilated(hdn, w_dw[:, None, :], (1,), [(pad, pad)],
                                   dimension_numbers=('NWC', 'WIO', 'NWC'),
                                   feature_group_count=hdn.shape[-1]) + b_dw
    hdn = (layer_norm_stats(hdn) * ln_g.astype(jnp.float32) + ln_b.astype(jnp.float32)).astype(a.dtype)
    return jax.nn.silu(hdn)


def rope_axis(x, pos):
    half = x.shape[-1] // 2
    freqs = ROPE_THETA ** (-jnp.arange(half, dtype=jnp.float32) / half)
    ang = pos.astype(jnp.float32)[:, None] * freqs[None, :]
    cos = jnp.cos(ang)[:, None, :]
    sin = jnp.sin(ang)[:, None, :]
    xf = x.astype(jnp.float32)
    x1, x2 = xf[..., :half], xf[..., half:]
    return jnp.concatenate([x1 * cos - x2 * sin, x2 * cos + x1 * sin], axis=-1).astype(x.dtype)


def axial_rope(x, row, col):
    d2 = x.shape[-1] // 2
    return jnp.concatenate([rope_axis(x[..., :d2], row), rope_axis(x[..., d2:], col)], axis=-1)


def gqa_branch(q, k, v):
    bn, s, hq, dh = q.shape
    hkv = k.shape[2]
    g = hq // hkv
    nq = s // Q_BLOCK
    qb = q.reshape(bn, nq, Q_BLOCK, hkv, g, dh).transpose(1, 0, 2, 3, 4, 5)

    def block(qblk):
        sc = jnp.einsum('bqhgd,bkhd->bhgqk', qblk, k, preferred_element_type=jnp.float32)
        p = jax.nn.softmax(sc, axis=-1)
        return jnp.einsum('bhgqk,bkhd->bqhgd', p.astype(v.dtype), v)

    o = lax.map(block, qb)
    return o.transpose(1, 0, 2, 3, 4, 5).reshape(bn, s, hq, dh)


def _fwd_setup_inputs(seed: int = 0) -> dict:
    key = jax.random.key(seed)
    ks = jax.random.split(key, 20)
    f32 = jnp.float32
    nrm = lambda k, shape, scale: jax.random.normal(k, shape, f32) * scale
    gain = lambda k, shape: 1.0 + 0.02 * jax.random.normal(k, shape, f32)
    return {
        "x": jax.random.normal(ks[0], (BATCH, SEQ, D_MODEL), f32),
        "rel_bias": nrm(ks[1], (REL_BUCKETS, DIL_HEADS), 0.5),
        "norm1_g": gain(ks[2], (DEPTH, D_MODEL)),
        "w_in": nrm(ks[3], (DEPTH, D_MODEL, IN_WIDTH), D_MODEL ** -0.5),
        "sgu_w": nrm(ks[4], (DEPTH, SGU_GROUPS, SGU_CHUNK, SGU_CHUNK), SGU_CHUNK ** -0.5),
        "sgu_b": gain(ks[5], (DEPTH, SGU_GROUPS, SGU_CHUNK)),
        "dil_qn_g": gain(ks[6], (DEPTH, HEAD_DIM)),
        "dil_kn_g": gain(ks[7], (DEPTH, HEAD_DIM)),
        "conv_w": nrm(ks[8], (DEPTH, CONV_WIDTH, CONV_CH), CONV_WIDTH ** -0.5),
        "conv_b": nrm(ks[9], (DEPTH, CONV_CH), 0.02),
        "conv_ln_g": gain(ks[10], (DEPTH, CONV_CH)),
        "conv_ln_b": nrm(ks[11], (DEPTH, CONV_CH), 0.02),
        "gqa_qn_g": gain(ks[12], (DEPTH, HEAD_DIM)),
        "gqa_kn_g": gain(ks[13], (DEPTH, HEAD_DIM)),
        "mix_norm_g": gain(ks[14], (DEPTH, MIX_WIDTH)),
        "w_out": nrm(ks[15], (DEPTH, MIX_WIDTH, D_MODEL), MIX_WIDTH ** -0.5),
        "norm2_g": gain(ks[16], (DEPTH, D_MODEL)),
        "w_gate": nrm(ks[17], (DEPTH, D_MODEL, FFN_HIDDEN), D_MODEL ** -0.5),
        "w_up": nrm(ks[18], (DEPTH, D_MODEL, FFN_HIDDEN), D_MODEL ** -0.5),
        "w_down": nrm(ks[19], (DEPTH, FFN_HIDDEN, D_MODEL), FFN_HIDDEN ** -0.5),
    }


def _fwd_reference(x, rel_bias, norm1_g, w_in, sgu_w, sgu_b, dil_qn_g, dil_kn_g, conv_w, conv_b,
              conv_ln_g, conv_ln_b, gqa_qn_g, gqa_kn_g, mix_norm_g, w_out, norm2_g,
              w_gate, w_up, w_down):
    bn, s, _ = x.shape
    rows = s // GRID_W
    row = jnp.repeat(jnp.arange(rows), GRID_W)
    col = jnp.tile(jnp.arange(GRID_W), rows)
    split_at = np.cumsum(IN_SIZES)[:-1].tolist()
    scale = HEAD_DIM ** -0.5
    for l in range(DEPTH):
        h = rms_norm(x, norm1_g[l])
        z = h @ w_in[l]
        a_u, a_v, b_q, b_k, b_v, c_a, c_g, d_q, d_k, d_v = jnp.split(z, split_at, axis=-1)
        y_a = sgu_branch(a_u, a_v, sgu_w[l], sgu_b[l])
        qb = rms_norm(split_heads(b_q), dil_qn_g[l]) * scale
        kb = rms_norm(split_heads(b_k), dil_kn_g[l])
        y_b = dilated_mixture(qb, kb, split_heads(b_v), rel_bias).reshape(bn, s, GROUP_WIDTH)
        y_c = conv_branch(c_a, c_g, conv_w[l], conv_b[l], conv_ln_g[l], conv_ln_b[l])
        qd = axial_rope(rms_norm(split_heads(d_q), gqa_qn_g[l]), row, col) * scale
        kd = axial_rope(rms_norm(split_heads(d_k), gqa_kn_g[l]), row, col)
        y_d = gqa_branch(qd, kd, split_heads(d_v)).reshape(bn, s, GROUP_WIDTH)
        y = jnp.stack([y_a, y_b, y_c, y_d], axis=2)
        y = rms_norm(y, mix_norm_g[l].reshape(N_MIXERS, GROUP_WIDTH)).reshape(bn, s, MIX_WIDTH)
        x = x + y @ w_out[l]
        h = rms_norm(x, norm2_g[l])
        x = x + (jax.nn.silu(h @ w_gate[l]) * (h @ w_up[l])) @ w_down[l]
    return x


import jax as _jax
import jax.numpy as _jnp

TWIN_FORMAT = 'train_step'
FWD_PARAMS = ['x', 'rel_bias', 'norm1_g', 'w_in', 'sgu_w', 'sgu_b', 'dil_qn_g', 'dil_kn_g', 'conv_w', 'conv_b', 'conv_ln_g', 'conv_ln_b', 'gqa_qn_g', 'gqa_kn_g', 'mix_norm_g', 'w_out', 'norm2_g', 'w_gate', 'w_up', 'w_down']
TWIN_WEIGHTS = ['rel_bias', 'norm1_g', 'w_in', 'sgu_w', 'sgu_b', 'dil_qn_g', 'dil_kn_g', 'conv_w', 'conv_b', 'conv_ln_g', 'conv_ln_b', 'gqa_qn_g', 'gqa_kn_g', 'mix_norm_g', 'w_out', 'norm2_g', 'w_gate', 'w_up', 'w_down']
TWIN_DIFF_INPUT = 'x'
TWIN_INPUTS = ['x', 'rel_bias', 'norm1_g', 'w_in', 'sgu_w', 'sgu_b', 'dil_qn_g', 'dil_kn_g', 'conv_w', 'conv_b', 'conv_ln_g', 'conv_ln_b', 'gqa_qn_g', 'gqa_kn_g', 'mix_norm_g', 'w_out', 'norm2_g', 'w_gate', 'w_up', 'w_down', 'loss_target', 'm_rel_bias', 'm_norm1_g', 'm_w_in', 'm_sgu_w', 'm_sgu_b', 'm_dil_qn_g', 'm_dil_kn_g', 'm_conv_w', 'm_conv_b', 'm_conv_ln_g', 'm_conv_ln_b', 'm_gqa_qn_g', 'm_gqa_kn_g', 'm_mix_norm_g', 'm_w_out', 'm_norm2_g', 'm_w_gate', 'm_w_up', 'm_w_down', 'v_rel_bias', 'v_norm1_g', 'v_w_in', 'v_sgu_w', 'v_sgu_b', 'v_dil_qn_g', 'v_dil_kn_g', 'v_conv_w', 'v_conv_b', 'v_conv_ln_g', 'v_conv_ln_b', 'v_gqa_qn_g', 'v_gqa_kn_g', 'v_mix_norm_g', 'v_w_out', 'v_norm2_g', 'v_w_gate', 'v_w_up', 'v_w_down']
TWIN_OUTPUTS = ['loss', 'grad_x', 'grad_rel_bias', 'grad_norm1_g', 'grad_w_in', 'grad_sgu_w', 'grad_sgu_b', 'grad_dil_qn_g', 'grad_dil_kn_g', 'grad_conv_w', 'grad_conv_b', 'grad_conv_ln_g', 'grad_conv_ln_b', 'grad_gqa_qn_g', 'grad_gqa_kn_g', 'grad_mix_norm_g', 'grad_w_out', 'grad_norm2_g', 'grad_w_gate', 'grad_w_up', 'grad_w_down', 'delta_rel_bias', 'delta_norm1_g', 'delta_w_in', 'delta_sgu_w', 'delta_sgu_b', 'delta_dil_qn_g', 'delta_dil_kn_g', 'delta_conv_w', 'delta_conv_b', 'delta_conv_ln_g', 'delta_conv_ln_b', 'delta_gqa_qn_g', 'delta_gqa_kn_g', 'delta_mix_norm_g', 'delta_w_out', 'delta_norm2_g', 'delta_w_gate', 'delta_w_up', 'delta_w_down', 'new_m_rel_bias', 'new_m_norm1_g', 'new_m_w_in', 'new_m_sgu_w', 'new_m_sgu_b', 'new_m_dil_qn_g', 'new_m_dil_kn_g', 'new_m_conv_w', 'new_m_conv_b', 'new_m_conv_ln_g', 'new_m_conv_ln_b', 'new_m_gqa_qn_g', 'new_m_gqa_kn_g', 'new_m_mix_norm_g', 'new_m_w_out', 'new_m_norm2_g', 'new_m_w_gate', 'new_m_w_up', 'new_m_w_down', 'new_v_rel_bias', 'new_v_norm1_g', 'new_v_w_in', 'new_v_sgu_w', 'new_v_sgu_b', 'new_v_dil_qn_g', 'new_v_dil_kn_g', 'new_v_conv_w', 'new_v_conv_b', 'new_v_conv_ln_g', 'new_v_conv_ln_b', 'new_v_gqa_qn_g', 'new_v_gqa_kn_g', 'new_v_mix_norm_g', 'new_v_w_out', 'new_v_norm2_g', 'new_v_w_gate', 'new_v_w_up', 'new_v_w_down']
TWIN_LEAF_KINDS = {'loss': 'loss', 'grad_x': 'grad_x', 'grad_rel_bias': 'grad_w', 'grad_norm1_g': 'grad_w', 'grad_w_in': 'grad_w', 'grad_sgu_w': 'grad_w', 'grad_sgu_b': 'grad_w', 'grad_dil_qn_g': 'grad_w', 'grad_dil_kn_g': 'grad_w', 'grad_conv_w': 'grad_w', 'grad_conv_b': 'grad_w', 'grad_conv_ln_g': 'grad_w', 'grad_conv_ln_b': 'grad_w', 'grad_gqa_qn_g': 'grad_w', 'grad_gqa_kn_g': 'grad_w', 'grad_mix_norm_g': 'grad_w', 'grad_w_out': 'grad_w', 'grad_norm2_g': 'grad_w', 'grad_w_gate': 'grad_w', 'grad_w_up': 'grad_w', 'grad_w_down': 'grad_w', 'delta_rel_bias': 'delta_w', 'delta_norm1_g': 'delta_w', 'delta_w_in': 'delta_w', 'delta_sgu_w': 'delta_w', 'delta_sgu_b': 'delta_w', 'delta_dil_qn_g': 'delta_w', 'delta_dil_kn_g': 'delta_w', 'delta_conv_w': 'delta_w', 'delta_conv_b': 'delta_w', 'delta_conv_ln_g': 'delta_w', 'delta_conv_ln_b': 'delta_w', 'delta_gqa_qn_g': 'delta_w', 'delta_gqa_kn_g': 'delta_w', 'delta_mix_norm_g': 'delta_w', 'delta_w_out': 'delta_w', 'delta_norm2_g': 'delta_w', 'delta_w_gate': 'delta_w', 'delta_w_up': 'delta_w', 'delta_w_down': 'delta_w', 'new_m_rel_bias': 'new_m', 'new_m_norm1_g': 'new_m', 'new_m_w_in': 'new_m', 'new_m_sgu_w': 'new_m', 'new_m_sgu_b': 'new_m', 'new_m_dil_qn_g': 'new_m', 'new_m_dil_kn_g': 'new_m', 'new_m_conv_w': 'new_m', 'new_m_conv_b': 'new_m', 'new_m_conv_ln_g': 'new_m', 'new_m_conv_ln_b': 'new_m', 'new_m_gqa_qn_g': 'new_m', 'new_m_gqa_kn_g': 'new_m', 'new_m_mix_norm_g': 'new_m', 'new_m_w_out': 'new_m', 'new_m_norm2_g': 'new_m', 'new_m_w_gate': 'new_m', 'new_m_w_up': 'new_m', 'new_m_w_down': 'new_m', 'new_v_rel_bias': 'new_v', 'new_v_norm1_g': 'new_v', 'new_v_w_in': 'new_v', 'new_v_sgu_w': 'new_v', 'new_v_sgu_b': 'new_v', 'new_v_dil_qn_g': 'new_v', 'new_v_dil_kn_g': 'new_v', 'new_v_conv_w': 'new_v', 'new_v_conv_b': 'new_v', 'new_v_conv_ln_g': 'new_v', 'new_v_conv_ln_b': 'new_v', 'new_v_gqa_qn_g': 'new_v', 'new_v_gqa_kn_g': 'new_v', 'new_v_mix_norm_g': 'new_v', 'new_v_w_out': 'new_v', 'new_v_norm2_g': 'new_v', 'new_v_w_gate': 'new_v', 'new_v_w_up': 'new_v', 'new_v_w_down': 'new_v'}


def _forward(args):
    return _fwd_reference(*[args[k] for k in FWD_PARAMS])


def _output_shape():
    out = _jax.eval_shape(lambda: _forward(_fwd_setup_inputs(0)))
    return out.shape, out.dtype

N_MICROBATCH = 1
ADAM_LR = 0.001
ADAM_B1 = 0.9
ADAM_B2 = 0.999
ADAM_EPS = 1e-08
ADAM_WD = 0.01
ADAM_STEP = 10
PER_EXAMPLE_BATCH_AXIS = {'x': 0, 'loss_target': 0}
SHARED_INPUTS = []
_WEIGHT_DTYPES = {'rel_bias': _jnp.float32, 'norm1_g': _jnp.float32, 'w_in': _jnp.float32, 'sgu_w': _jnp.float32, 'sgu_b': _jnp.float32, 'dil_qn_g': _jnp.float32, 'dil_kn_g': _jnp.float32, 'conv_w': _jnp.float32, 'conv_b': _jnp.float32, 'conv_ln_g': _jnp.float32, 'conv_ln_b': _jnp.float32, 'gqa_qn_g': _jnp.float32, 'gqa_kn_g': _jnp.float32, 'mix_norm_g': _jnp.float32, 'w_out': _jnp.float32, 'norm2_g': _jnp.float32, 'w_gate': _jnp.float32, 'w_up': _jnp.float32, 'w_down': _jnp.float32}
MOMENT_SCALE = {'rel_bias': 5.254793e-01, 'norm1_g': 4.339029e+00, 'w_in': 3.040999e+00, 'sgu_w': 1.437898e-01, 'sgu_b': 2.048768e-01, 'dil_qn_g': 9.034708e-01, 'dil_kn_g': 9.078697e-01, 'conv_w': 1.915440e+00, 'conv_b': 2.380111e+01, 'conv_ln_g': 9.019726e+00, 'conv_ln_b': 1.253677e+01, 'gqa_qn_g': 1.749087e+00, 'gqa_kn_g': 2.009824e+00, 'mix_norm_g': 1.852889e+01, 'w_out': 6.197075e+00, 'norm2_g': 1.272728e+01, 'w_gate': 5.422146e-01, 'w_up': 5.887362e-01, 'w_down': 9.301355e-01}


def _to_microbatches(a, axis):
    t = _jnp.moveaxis(a, axis, 0)
    t = t.reshape((N_MICROBATCH, t.shape[0] // N_MICROBATCH) + t.shape[1:])
    return _jnp.moveaxis(t, 1, axis + 1)


def setup_inputs(seed: int = 0) -> dict:
    inp = _fwd_setup_inputs(seed)
    key = _jax.random.fold_in(_jax.random.key(seed), 7919)
    shape, _ = _output_shape()
    out = dict(inp)
    out["loss_target"] = _jax.random.normal(_jax.random.fold_in(key, 0), shape, _jnp.float32)
    for i, name in enumerate(TWIN_WEIGHTS):
        w = inp[name].astype(_jnp.float32)
        if MOMENT_SCALE is None:
            s = _jnp.sqrt(_jnp.mean(_jnp.square(w)) + 1e-30)
        else:
            s = MOMENT_SCALE[name]
        km, kv = _jax.random.split(_jax.random.fold_in(key, i + 1))
        out[name] = w
        out["m_" + name] = s * _jax.random.normal(km, w.shape, _jnp.float32)
        out["v_" + name] = (s * s) * _jax.random.uniform(kv, w.shape, _jnp.float32, 0.5, 1.5)
    if N_MICROBATCH > 1:
        for name, axis in PER_EXAMPLE_BATCH_AXIS.items():
            out[name] = _to_microbatches(out[name], axis)
    return {'x': out['x'], 'rel_bias': out['rel_bias'], 'norm1_g': out['norm1_g'], 'w_in': out['w_in'], 'sgu_w': out['sgu_w'], 'sgu_b': out['sgu_b'], 'dil_qn_g': out['dil_qn_g'], 'dil_kn_g': out['dil_kn_g'], 'conv_w': out['conv_w'], 'conv_b': out['conv_b'], 'conv_ln_g': out['conv_ln_g'], 'conv_ln_b': out['conv_ln_b'], 'gqa_qn_g': out['gqa_qn_g'], 'gqa_kn_g': out['gqa_kn_g'], 'mix_norm_g': out['mix_norm_g'], 'w_out': out['w_out'], 'norm2_g': out['norm2_g'], 'w_gate': out['w_gate'], 'w_up': out['w_up'], 'w_down': out['w_down'], 'loss_target': out['loss_target'], 'm_rel_bias': out['m_rel_bias'], 'm_norm1_g': out['m_norm1_g'], 'm_w_in': out['m_w_in'], 'm_sgu_w': out['m_sgu_w'], 'm_sgu_b': out['m_sgu_b'], 'm_dil_qn_g': out['m_dil_qn_g'], 'm_dil_kn_g': out['m_dil_kn_g'], 'm_conv_w': out['m_conv_w'], 'm_conv_b': out['m_conv_b'], 'm_conv_ln_g': out['m_conv_ln_g'], 'm_conv_ln_b': out['m_conv_ln_b'], 'm_gqa_qn_g': out['m_gqa_qn_g'], 'm_gqa_kn_g': out['m_gqa_kn_g'], 'm_mix_norm_g': out['m_mix_norm_g'], 'm_w_out': out['m_w_out'], 'm_norm2_g': out['m_norm2_g'], 'm_w_gate': out['m_w_gate'], 'm_w_up': out['m_w_up'], 'm_w_down': out['m_w_down'], 'v_rel_bias': out['v_rel_bias'], 'v_norm1_g': out['v_norm1_g'], 'v_w_in': out['v_w_in'], 'v_sgu_w': out['v_sgu_w'], 'v_sgu_b': out['v_sgu_b'], 'v_dil_qn_g': out['v_dil_qn_g'], 'v_dil_kn_g': out['v_dil_kn_g'], 'v_conv_w': out['v_conv_w'], 'v_conv_b': out['v_conv_b'], 'v_conv_ln_g': out['v_conv_ln_g'], 'v_conv_ln_b': out['v_conv_ln_b'], 'v_gqa_qn_g': out['v_gqa_qn_g'], 'v_gqa_kn_g': out['v_gqa_kn_g'], 'v_mix_norm_g': out['v_mix_norm_g'], 'v_w_out': out['v_w_out'], 'v_norm2_g': out['v_norm2_g'], 'v_w_gate': out['v_w_gate'], 'v_w_up': out['v_w_up'], 'v_w_down': out['v_w_down']}


def _loss(weights, diff, rest, loss_target):
    with _jax.named_scope("forward"):
        args = {**rest, TWIN_DIFF_INPUT: diff, **{k: w.astype(_WEIGHT_DTYPES[k]) for k, w in weights.items()}}
        y = _forward(args)
    with _jax.named_scope("loss_head"):
        err = _jnp.square(y.astype(_jnp.float32) - loss_target)
        return 0.5 * _jnp.sum(_jnp.mean(err, axis=-1)) if err.ndim else 0.5 * err


def _adamw(w, g, m, v):
    m = ADAM_B1 * m + (1.0 - ADAM_B1) * g
    v = ADAM_B2 * v + (1.0 - ADAM_B2) * _jnp.square(g)
    m_hat = m / (1.0 - ADAM_B1 ** ADAM_STEP)
    v_hat = v / (1.0 - ADAM_B2 ** ADAM_STEP)
    delta = -ADAM_LR * (m_hat / (_jnp.sqrt(v_hat) + ADAM_EPS) + ADAM_WD * w)
    return delta, m, v


def reference(x, rel_bias, norm1_g, w_in, sgu_w, sgu_b, dil_qn_g, dil_kn_g, conv_w, conv_b, conv_ln_g, conv_ln_b, gqa_qn_g, gqa_kn_g, mix_norm_g, w_out, norm2_g, w_gate, w_up, w_down, loss_target, m_rel_bias, m_norm1_g, m_w_in, m_sgu_w, m_sgu_b, m_dil_qn_g, m_dil_kn_g, m_conv_w, m_conv_b, m_conv_ln_g, m_conv_ln_b, m_gqa_qn_g, m_gqa_kn_g, m_mix_norm_g, m_w_out, m_norm2_g, m_w_gate, m_w_up, m_w_down, v_rel_bias, v_norm1_g, v_w_in, v_sgu_w, v_sgu_b, v_dil_qn_g, v_dil_kn_g, v_conv_w, v_conv_b, v_conv_ln_g, v_conv_ln_b, v_gqa_qn_g, v_gqa_kn_g, v_mix_norm_g, v_w_out, v_norm2_g, v_w_gate, v_w_up, v_w_down):
    given = dict(x=x, rel_bias=rel_bias, norm1_g=norm1_g, w_in=w_in, sgu_w=sgu_w, sgu_b=sgu_b, dil_qn_g=dil_qn_g, dil_kn_g=dil_kn_g, conv_w=conv_w, conv_b=conv_b, conv_ln_g=conv_ln_g, conv_ln_b=conv_ln_b, gqa_qn_g=gqa_qn_g, gqa_kn_g=gqa_kn_g, mix_norm_g=mix_norm_g, w_out=w_out, norm2_g=norm2_g, w_gate=w_gate, w_up=w_up, w_down=w_down, loss_target=loss_target, m_rel_bias=m_rel_bias, m_norm1_g=m_norm1_g, m_w_in=m_w_in, m_sgu_w=m_sgu_w, m_sgu_b=m_sgu_b, m_dil_qn_g=m_dil_qn_g, m_dil_kn_g=m_dil_kn_g, m_conv_w=m_conv_w, m_conv_b=m_conv_b, m_conv_ln_g=m_conv_ln_g, m_conv_ln_b=m_conv_ln_b, m_gqa_qn_g=m_gqa_qn_g, m_gqa_kn_g=m_gqa_kn_g, m_mix_norm_g=m_mix_norm_g, m_w_out=m_w_out, m_norm2_g=m_norm2_g, m_w_gate=m_w_gate, m_w_up=m_w_up, m_w_down=m_w_down, v_rel_bias=v_rel_bias, v_norm1_g=v_norm1_g, v_w_in=v_w_in, v_sgu_w=v_sgu_w, v_sgu_b=v_sgu_b, v_dil_qn_g=v_dil_qn_g, v_dil_kn_g=v_dil_kn_g, v_conv_w=v_conv_w, v_conv_b=v_conv_b, v_conv_ln_g=v_conv_ln_g, v_conv_ln_b=v_conv_ln_b, v_gqa_qn_g=v_gqa_qn_g, v_gqa_kn_g=v_gqa_kn_g, v_mix_norm_g=v_mix_norm_g, v_w_out=v_w_out, v_norm2_g=v_norm2_g, v_w_gate=v_w_gate, v_w_up=v_w_up, v_w_down=v_w_down)
    weights = {n: given[n] for n in TWIN_WEIGHTS}
    shared = {n: given[n] for n in SHARED_INPUTS}
    per_example = {n: given[n] for n in ['x']}
    grad_fn = _jax.value_and_grad(_loss, argnums=(0, 1))

    def one_microbatch(ex, loss_target):
        ex = dict(ex)
        diff = ex.pop(TWIN_DIFF_INPUT)
        return grad_fn(weights, diff, {**shared, **ex}, loss_target)

    if N_MICROBATCH == 1:
        loss, (grad_w, grad_x) = one_microbatch(per_example, given["loss_target"])
    else:
        def body(carry, xs):
            loss_sum, grad_sum = carry
            l_k, (gw_k, gx_k) = one_microbatch(xs[0], xs[1])
            with _jax.named_scope("update"):
                return (loss_sum + l_k, _jax.tree.map(_jnp.add, grad_sum, gw_k)), gx_k

        init = (_jnp.zeros((), _jnp.float32), _jax.tree.map(_jnp.zeros_like, weights))
        (loss, grad_w), grad_x = _jax.lax.scan(body, init, (per_example, given["loss_target"]))
    with _jax.named_scope("update"):
        delta_w, new_m, new_v = {}, {}, {}
        for n in TWIN_WEIGHTS:
            delta_w[n], new_m[n], new_v[n] = _adamw(weights[n], grad_w[n], given["m_" + n], given["v_" + n])
    return (loss, grad_x, *[grad_w[n] for n in TWIN_WEIGHTS], *[delta_w[n] for n in TWIN_WEIGHTS],
            *[new_m[n] for n in TWIN_WEIGHTS], *[new_v[n] for n in TWIN_WEIGHTS])
```

```python
import functools
import math

import numpy as np
import jax
import jax.numpy as jnp
from jax import lax
from jax.experimental import pallas as pl
from jax.experimental.pallas import tpu as pltpu

F32 = jnp.float32
BF16 = jnp.bfloat16

D_MODEL = 2048
SEQ = 2048
DEPTH = 2
HEAD_DIM = 64
GROUP_WIDTH = 512
N_HEADS = 8
SGU_CHUNK = 128
DIL_PATTERNS = ((128, 1), (512, 4), (2048, 16))
DIL_HALF = 64
CONV_WIDTH = 31
KV_WIDTH = 128
GRID_W = 64
ROPE_THETA = 10000.0
REL_BUCKETS = 32
REL_MAX_DIST = 1024
FFN_HIDDEN = 5632
IN_WIDTH = 4352
RMS_EPS = 1e-6
LN_EPS = 1e-5
ADAM_LR = 0.001
ADAM_B1 = 0.9
ADAM_B2 = 0.999
ADAM_EPS = 1e-08
ADAM_WD = 0.01
ADAM_STEP = 10
N_CHIPS = 4

V7X_VMEM_LIMIT = 56 * 1024 * 1024
HI = lax.Precision.HIGHEST
MESH = pl.DeviceIdType.MESH


def _cparams(sem=None):
    return pltpu.CompilerParams(dimension_semantics=sem, vmem_limit_bytes=V7X_VMEM_LIMIT)


def _pick(n, cands):
    for c in cands:
        if n % c == 0:
            return c
    raise ValueError(f"no tile for {n}")


_DIMS = {"nn": (((1,), (0,)), ((), ())), "nt": (((1,), (1,)), ((), ())), "tn": (((0,), (0,)), ((), ()))}


def _matmul(pairs, mode, out_dtype, name, residual=None, slabs=1):
    a0, b0 = pairs[0]
    b3 = b0.ndim == 3
    if mode == "nn":
        (M, K), N = a0.shape, b0.shape[1]
    elif mode == "nt":
        (M, K), N = a0.shape, b0.shape[0]
    else:
        (K, M) = a0.shape
        N = b0.shape[-1] if b3 else b0.shape[1] // slabs
    tm = _pick(M, (1024, 512, 256))
    tn = _pick(N, (1024, 2176, 1408, 1088, 512, 256))
    tk = _pick(K, (512, 2176, 256))
    nk = K // tk
    nj = N // tn
    npairs = len(pairs)

    if mode in ("nn", "nt"):
        a_spec = pl.BlockSpec((tm, tk), lambda s, i, j, k: (i, k))
    else:
        a_spec = pl.BlockSpec((tk, tm), lambda s, i, j, k: (k, i))
    if mode == "nt":
        b_spec = pl.BlockSpec((tn, tk), lambda s, i, j, k: (j, k))
    elif b3:
        b_spec = pl.BlockSpec((None, tk, tn), lambda s, i, j, k: (s, k, j))
    else:
        b_spec = pl.BlockSpec((tk, tn), lambda s, i, j, k: (k, s * nj + j))
    if slabs > 1:
        o_spec = pl.BlockSpec((None, tm, tn), lambda s, i, j, k: (s, i, j))
        o_shape = jax.ShapeDtypeStruct((slabs, M, N), out_dtype)
    else:
        o_spec = pl.BlockSpec((tm, tn), lambda s, i, j, k: (i, j))
        o_shape = jax.ShapeDtypeStruct((M, N), out_dtype)
    in_specs = [a_spec] * npairs + [b_spec] * npairs
    args = [a for a, _ in pairs] + [b for _, b in pairs]
    if residual is not None:
        in_specs.append(pl.BlockSpec((tm, tn), lambda s, i, j, k: (i, j)))
        args.append(residual)
    dims = _DIMS[mode]

    def body(*refs):
        a_refs, b_refs = refs[:npairs], refs[npairs:2 * npairs]
        res_ref = refs[2 * npairs] if residual is not None else None
        o_ref, acc_ref = refs[-2], refs[-1]
        k = pl.program_id(3)

        @pl.when(k == 0)
        def _():
            acc_ref[...] = jnp.zeros_like(acc_ref)

        part = None
        for a_ref, b_ref in zip(a_refs, b_refs):
            d = lax.dot_general(a_ref[...].astype(BF16), b_ref[...].astype(BF16), dims, preferred_element_type=F32)
            part = d if part is None else part + d
        acc_ref[...] += part

        @pl.when(k == nk - 1)
        def _():
            r = acc_ref[...]
            if res_ref is not None:
                r = r + res_ref[...]
            o_ref[...] = r.astype(out_dtype)

    return pl.pallas_call(
        body, name=name, grid=(slabs, M // tm, nj, nk), in_specs=in_specs, out_specs=o_spec, out_shape=o_shape,
        scratch_shapes=[pltpu.VMEM((tm, tn), F32)],
        compiler_params=_cparams(("parallel", "parallel", "parallel", "arbitrary")),
    )(*args)


def _rowmap(fn, rows, fulls, row_outs, acc_outs, name, tm, n_rows):
    nr, nf, nro = len(rows), len(fulls), len(row_outs)
    rows = [r if len(r) == 4 else (*r, n_rows // tm) for r in rows]
    in_specs = [pl.BlockSpec((tm, w), functools.partial(lambda i, cb, per: (i % per, cb), cb=cb, per=per)) for _, w, cb, per in rows]
    in_specs += [pl.BlockSpec(f.shape, lambda i: (0,) * f.ndim) for f in fulls]
    out_specs = [pl.BlockSpec((tm, w), lambda i: (i, 0)) for w, _ in row_outs]
    out_specs += [pl.BlockSpec(s, functools.partial(lambda i, n: (0,) * n, n=len(s))) for s in acc_outs]
    out_shape = [jax.ShapeDtypeStruct((n_rows, w), dt) for w, dt in row_outs]
    out_shape += [jax.ShapeDtypeStruct(s, F32) for s in acc_outs]

    def body(*refs):
        ins = [r[...] for r in refs[:nr + nf]]
        outs = fn(*ins)
        o_refs = refs[nr + nf:]
        for o_ref, val in zip(o_refs[:nro], outs[:nro]):
            o_ref[...] = val.astype(o_ref.dtype)
        if acc_outs:
            first = pl.program_id(0) == 0
            for o_ref, val in zip(o_refs[nro:], outs[nro:]):
                @pl.when(first)
                def _(o_ref=o_ref, val=val):
                    o_ref[...] = val

                @pl.when(jnp.logical_not(first))
                def _(o_ref=o_ref, val=val):
                    o_ref[...] += val

    res = pl.pallas_call(
        body, name=name, grid=(n_rows // tm,), in_specs=in_specs, out_specs=out_specs, out_shape=out_shape,
        compiler_params=_cparams(("arbitrary",) if acc_outs else ("parallel",)),
    )(*[r[0] for r in rows], *fulls)
    return res


def _rms(x, g):
    return x * lax.rsqrt(jnp.mean(x * x, axis=-1, keepdims=True) + RMS_EPS) * g


def _rmsnorm_fwd(x, g, name):
    t = x.shape[0]
    return _rowmap(lambda xv, gv: (_rms(xv, gv),), [(x, D_MODEL, 0)], [g], [(D_MODEL, BF16)], [], name, 512, t)[0]


def _rmsnorm_bwd(dh, x, g, dres, name):
    t = x.shape[0]

    def fn(dhv, xv, drv, gv):
        _, vjp = jax.vjp(_rms, xv, gv)
        dx, dg = vjp(dhv)
        return dx + drv, dg

    return _rowmap(fn, [(dh, D_MODEL, 0), (x, D_MODEL, 0), (dres, D_MODEL, 0)], [g], [(D_MODEL, F32)], [(1, D_MODEL)],
                   name, 256, t)


def _loss_fwd_bwd(y, target, name):
    t = y.shape[0]

    def fn(yv, tv):
        e = yv - tv
        return e * (1.0 / D_MODEL), (0.5 / D_MODEL) * jnp.sum(e * e, keepdims=True)

    return _rowmap(fn, [(y, D_MODEL, 0), (target, D_MODEL, 0)], [], [(D_MODEL, F32)], [(1, 1)], name, 512, t)


def _ffn_up(h, wg, wu, name):
    t, n = h.shape[0], wg.shape[1]
    tm, tn = 512, 512

    def body(h_ref, wg_ref, wu_ref, act_ref, g_ref, u_ref):
        hv = h_ref[...]
        g = jnp.dot(hv, wg_ref[...], preferred_element_type=F32)
        u = jnp.dot(hv, wu_ref[...], preferred_element_type=F32)
        act_ref[...] = (g * jax.nn.sigmoid(g) * u).astype(BF16)
        g_ref[...] = g.astype(BF16)
        u_ref[...] = u.astype(BF16)

    o_spec = pl.BlockSpec((tm, tn), lambda i, j: (i, j))
    o_shape = jax.ShapeDtypeStruct((t, n), BF16)
    return pl.pallas_call(
        body, name=name, grid=(t // tm, n // tn),
        in_specs=[pl.BlockSpec((tm, D_MODEL), lambda i, j: (i, 0)), pl.BlockSpec((D_MODEL, tn), lambda i, j: (0, j)),
                  pl.BlockSpec((D_MODEL, tn), lambda i, j: (0, j))],
        out_specs=[o_spec] * 3, out_shape=[o_shape] * 3, compiler_params=_cparams(("parallel", "parallel")),
    )(h, wg, wu)


def _ffn_down_bwd(dy, wd, g, u, name):
    t, n = dy.shape[0], wd.shape[0]
    tm, tn = 512, 512

    def body(dy_ref, wd_ref, g_ref, u_ref, dg_ref, du_ref):
        dact = lax.dot_general(dy_ref[...].astype(BF16), wd_ref[...], _DIMS["nt"], preferred_element_type=F32)
        gv = g_ref[...].astype(F32)
        uv = u_ref[...].astype(F32)
        sg = jax.nn.sigmoid(gv)
        silu = gv * sg
        du_ref[...] = (dact * silu).astype(BF16)
        dg_ref[...] = (dact * uv * (sg + silu * (1.0 - sg))).astype(BF16)

    o_spec = pl.BlockSpec((tm, tn), lambda i, j: (i, j))
    o_shape = jax.ShapeDtypeStruct((t, n), BF16)
    return pl.pallas_call(
        body, name=name, grid=(t // tm, n // tn),
        in_specs=[pl.BlockSpec((tm, D_MODEL), lambda i, j: (i, 0)), pl.BlockSpec((tn, D_MODEL), lambda i, j: (j, 0)),
                  o_spec, o_spec],
        out_specs=[o_spec] * 2, out_shape=[o_shape] * 2, compiler_params=_cparams(("parallel", "parallel")),
    )(dy, wd, g, u)


def _np_group_avg(width, group=HEAD_DIM):
    i = np.arange(width)
    return ((i[:, None] // group) == (i[None, :] // group)).astype(np.float32) / group


def _np_tile_fold(width, group=HEAD_DIM):
    return ((np.arange(width)[:, None] % group) == np.arange(group)[None, :]).astype(np.float32)


def _np_group_fold(width, group=HEAD_DIM, pad=128):
    return ((np.arange(width)[:, None] // group) == np.arange(pad)[None, :]).astype(np.float32)


def _np_rope_partner(width):
    i = np.arange(width)
    partner = np.where(i % 32 < 16, i + 16, i - 16)
    return (partner[:, None] == i[None, :]).astype(np.float32)


def _np_kv_expand():
    src = np.arange(KV_WIDTH)
    dst = np.arange(GROUP_WIDTH)
    return ((src[:, None] // HEAD_DIM == dst[None, :] // (4 * HEAD_DIM)) & (src[:, None] % HEAD_DIM == dst[None, :] % HEAD_DIM)).astype(np.float32)


def _np_rope_tables(n_heads):
    t = np.arange(SEQ)
    pos = {0: (t // GRID_W).astype(np.float32), 1: (t % GRID_W).astype(np.float32)}
    freqs = (ROPE_THETA ** (-np.arange(16, dtype=np.float32) / 16)).astype(np.float32)
    cos_parts, sin_parts = [], []
    for axis in (0, 1):
        ang = pos[axis][:, None] * freqs[None, :]
        c, s = np.cos(ang).astype(np.float32), np.sin(ang).astype(np.float32)
        cos_parts += [c, c]
        sin_parts += [-s, s]
    cos = np.concatenate(cos_parts, axis=1)
    sin = np.concatenate(sin_parts, axis=1)
    return np.tile(cos, (1, n_heads)), np.tile(sin, (1, n_heads))


def _np_t5_buckets(rel):
    nb = REL_BUCKETS // 2
    max_exact = nb // 2
    ret = np.where(rel > 0, nb, 0)
    n = np.abs(rel)
    nf = np.maximum(n, 1).astype(np.float32)
    large = max_exact + (np.log(nf / max_exact) / math.log(REL_MAX_DIST / max_exact) * (nb - max_exact)).astype(np.int32)
    large = np.minimum(large, nb - 1)
    return (ret + np.where(n < max_exact, n, large)).astype(np.int32)


DIL_QB = 128
DIL_WIN = 3 * DIL_QB


def _np_dil_buckets(dil):
    off = np.arange(DIL_WIN)[None, :] - DIL_QB - np.arange(DIL_QB)[:, None]
    return _np_t5_buckets(off * dil)


def _dil_live_buckets(dil):
    off = np.arange(-DIL_HALF, DIL_HALF + 1)
    return sorted(set(_np_t5_buckets(off * dil).tolist()))


def _head_stat(x, mavg):
    return jnp.dot(x, mavg, precision=HI, preferred_element_type=F32)


def _gelu(x):
    return 0.5 * x * (1.0 + jnp.tanh(math.sqrt(2.0 / math.pi) * (x + 0.044715 * (x * x * x))))


def _sgu_pre(u_pre, v_pre, mavg):
    v = _gelu(v_pre)
    xc = v - _head_stat(v, mavg)
    vn = xc * lax.rsqrt(_head_stat(xc * xc, mavg) + LN_EPS)
    return _gelu(u_pre), vn


def _sgu_mix(w_ref, vnb, bm):
    lane_group = lax.broadcasted_iota(jnp.int32, (1, GROUP_WIDTH), 1) // HEAD_DIM
    mixed = bm
    for g in range(N_HEADS):
        r = jnp.dot(w_ref[g], vnb, preferred_element_type=F32)
        mixed = mixed + jnp.where(lane_group == g, r, 0.0)
    return mixed


SGU_TM = 512


def _sgu_fwd(z, w_s, bm, name):
    t = z.shape[0]
    mavg = jnp.asarray(_np_group_avg(GROUP_WIDTH))

    def body(u_ref, v_ref, w_ref, bm_ref, mavg_ref, y_ref):
        for c in range(SGU_TM // SGU_CHUNK):
            rows = pl.ds(c * SGU_CHUNK, SGU_CHUNK)
            u, vn = _sgu_pre(u_ref[rows, :], v_ref[rows, :], mavg_ref[...])
            y_ref[rows, :] = u * _sgu_mix(w_ref, vn.astype(BF16), bm_ref[...])

    full = lambda a: pl.BlockSpec(a.shape, lambda i: (0,) * a.ndim)
    return pl.pallas_call(
        body, name=name, grid=(t // SGU_TM,),
        in_specs=[pl.BlockSpec((SGU_TM, GROUP_WIDTH), lambda i: (i, 0)), pl.BlockSpec((SGU_TM, GROUP_WIDTH), lambda i: (i, 1)),
                  full(w_s), full(bm), full(mavg)],
        out_specs=pl.BlockSpec((SGU_TM, GROUP_WIDTH), lambda i: (i, 0)),
        out_shape=jax.ShapeDtypeStruct((t, GROUP_WIDTH), F32), compiler_params=_cparams(("parallel",)),
    )(z, z, w_s, bm, mavg)


def _sgu_bwd(z, dy, w_s, w_s_t, bm, name):
    t = z.shape[0]
    mavg = jnp.asarray(_np_group_avg(GROUP_WIDTH))
    gfold = jnp.asarray(_np_group_fold(GROUP_WIDTH))

    def body(u_ref, v_ref, dy_ref, w_ref, wt_ref, bm_ref, mavg_ref, gfold_ref, du_ref, dv_ref, dw_ref, dbs_ref, dbm_ref):
        @pl.when(pl.program_id(0) == 0)
        def _():
            dw_ref[...] = jnp.zeros_like(dw_ref)
            dbm_ref[...] = jnp.zeros_like(dbm_ref)

        lane_group = lax.broadcasted_iota(jnp.int32, (1, GROUP_WIDTH), 1) // HEAD_DIM
        for c in range(SGU_TM // SGU_CHUNK):
            rows = pl.ds(c * SGU_CHUNK, SGU_CHUNK)
            (u, vn), pre_vjp = jax.vjp(functools.partial(_sgu_pre, mavg=mavg_ref[...]), u_ref[rows, :], v_ref[rows, :])
            vnb = vn.astype(BF16)
            mixed = _sgu_mix(w_ref, vnb, bm_ref[...])
            dyv = dy_ref[rows, :]
            dmixed = dyv * u
            dbm_ref[...] += dmixed
            dvn = jnp.zeros_like(vn)
            for g in range(N_HEADS):
                dm_g = jnp.where(lane_group == g, dmixed, 0.0).astype(BF16)
                dw_ref[g] += lax.dot_general(dm_g, vnb, _DIMS["nt"], preferred_element_type=F32)
                dvn = dvn + jnp.dot(wt_ref[g], dm_g, preferred_element_type=F32)
            du_pre, dv_pre = pre_vjp((dyv * mixed, dvn))
            du_ref[rows, :] = du_pre
            dv_ref[rows, :] = dv_pre

        @pl.when(pl.program_id(0) == t // SGU_TM - 1)
        def _():
            dbs_ref[...] = jnp.dot(dbm_ref[...], gfold_ref[...], precision=HI, preferred_element_type=F32)

    full = lambda a: pl.BlockSpec(a.shape, lambda i: (0,) * a.ndim)
    row = pl.BlockSpec((SGU_TM, GROUP_WIDTH), lambda i: (i, 0))
    return pl.pallas_call(
        body, name=name, grid=(t // SGU_TM,),
        in_specs=[row, pl.BlockSpec((SGU_TM, GROUP_WIDTH), lambda i: (i, 1)), row, full(w_s), full(w_s_t), full(bm), full(mavg),
                  full(gfold)],
        out_specs=[row, row, pl.BlockSpec((N_HEADS, SGU_CHUNK, SGU_CHUNK), lambda i: (0, 0, 0)),
                   pl.BlockSpec((SGU_CHUNK, 128), lambda i: (0, 0))],
        out_shape=[jax.ShapeDtypeStruct((t, GROUP_WIDTH), F32)] * 2 + [jax.ShapeDtypeStruct((N_HEADS, SGU_CHUNK, SGU_CHUNK), F32),
                                                                      jax.ShapeDtypeStruct((SGU_CHUNK, 128), F32)],
        scratch_shapes=[pltpu.VMEM((SGU_CHUNK, GROUP_WIDTH), F32)],
        compiler_params=_cparams(("arbitrary",)),
    )(z, z, dy, w_s, w_s_t, bm, mavg, gfold)


def _pair_softmax(q2, k2, hh, bias, valid):
    head = (lax.broadcasted_iota(jnp.int32, (1, 2 * HEAD_DIM), 1) // HEAD_DIM) == hh
    qm = jnp.where(head, q2, jnp.zeros_like(q2))
    s = lax.dot_general(qm, k2, _DIMS["nt"], preferred_element_type=F32)
    if bias is not None:
        s = s + bias
    if valid is not None:
        s = jnp.where(valid, s, -1e30)
    m = jnp.max(s, axis=-1, keepdims=True)
    e = jnp.exp(s - m)
    l = jnp.sum(e, axis=-1, keepdims=True)
    return head, qm, e / l, m + jnp.log(l)


def _attn_pair_fwd(q2, k2, v2, biases, valid):
    o2 = lse2 = None
    for hh in range(2):
        head, _, p, lse = _pair_softmax(q2, k2, hh, None if biases is None else biases[hh], valid)
        oh = jnp.dot(p.astype(BF16), v2, preferred_element_type=F32)
        o_h = jnp.where(head, oh, 0.0)
        l_h = jnp.where(head, lse, 0.0)
        o2 = o_h if o2 is None else o2 + o_h
        lse2 = l_h if lse2 is None else lse2 + l_h
    return o2, lse2


def _attn_pair_bwd(q2, k2, v2, biases, valid, do2, dlse2):
    dq2 = dk2 = dv2 = None
    ds_heads = []
    for hh in range(2):
        head, qm, p, _ = _pair_softmax(q2, k2, hh, None if biases is None else biases[hh], valid)
        dom = jnp.where(head, do2, 0.0).astype(BF16)
        dp = lax.dot_general(dom, v2, _DIMS["nt"], preferred_element_type=F32)
        delta = jnp.sum(dp * p, axis=-1, keepdims=True)
        if dlse2 is not None:
            delta = delta - jnp.sum(jnp.where(head, dlse2, 0.0), axis=-1, keepdims=True)
        ds = p * (dp - delta)
        dsb = ds.astype(BF16)
        dq_h = jnp.where(head, jnp.dot(dsb, k2, preferred_element_type=F32), 0.0)
        dk_h = lax.dot_general(dsb, qm, _DIMS["tn"], preferred_element_type=F32)
        dv_h = lax.dot_general(p.astype(BF16), dom, _DIMS["tn"], preferred_element_type=F32)
        dq2 = dq_h if dq2 is None else dq2 + dq_h
        dk2 = dk_h if dk2 is None else dk2 + dk_h
        dv2 = dv_h if dv2 is None else dv2 + dv_h
        ds_heads.append(ds)
    return dq2, dk2, dv2, ds_heads


def _dil_valid(r0, length):
    row = lax.broadcasted_iota(jnp.int32, (DIL_QB, DIL_WIN), 0)
    col = lax.broadcasted_iota(jnp.int32, (DIL_QB, DIL_WIN), 1)
    off = col - DIL_QB - row
    kpos = r0 - DIL_QB + col
    return (jnp.abs(off) <= DIL_HALF) & (kpos >= 0) & (kpos < length)


def _dil_build_bias(tab_ref, bkt_ref, bias_ref, dil):
    bkt = bkt_ref[...]
    for h in range(N_HEADS):
        acc = jnp.zeros((DIL_QB, DIL_WIN), F32)
        for b in _dil_live_buckets(dil):
            acc = jnp.where(bkt == b, tab_ref[b, h], acc)
        bias_ref[h] = acc


def _dil_fill_pad(pad_ref, src_ref, length):
    zeros = jnp.zeros((DIL_QB, GROUP_WIDTH), pad_ref.dtype)
    pad_ref[pl.ds(0, DIL_QB), :] = zeros
    pad_ref[pl.ds(DIL_QB + length, DIL_QB), :] = zeros
    pad_ref[pl.ds(DIL_QB, length), :] = src_ref[...]


def _dil_specs(bsz, length, dil):
    view = lambda a: a.reshape(bsz, length, dil * GROUP_WIDTH)
    blk = pl.BlockSpec((None, DIL_QB, GROUP_WIDTH), lambda b, rho, i: (b, i, rho))
    seq = pl.BlockSpec((None, length, GROUP_WIDTH), lambda b, rho, i: (b, 0, rho))
    return view, blk, seq


def _dil_fwd(qb, kb, vb, table, dil, name):
    bsz = qb.shape[0]
    length = SEQ // dil
    bkt = jnp.asarray(_np_dil_buckets(dil))
    view, blk, seq = _dil_specs(bsz, length, dil)

    def body(tab_ref, bkt_ref, q_ref, k_ref, v_ref, o_ref, lse_ref, kpad, vpad, bias_ref):
        i = pl.program_id(2)

        @pl.when((pl.program_id(0) == 0) & (pl.program_id(1) == 0) & (i == 0))
        def _():
            _dil_build_bias(tab_ref, bkt_ref, bias_ref, dil)

        @pl.when(i == 0)
        def _():
            _dil_fill_pad(kpad, k_ref, length)
            _dil_fill_pad(vpad, v_ref, length)

        r0 = pl.multiple_of(i * DIL_QB, DIL_QB)
        valid = _dil_valid(r0, length)
        for m in range(N_HEADS // 2):
            lanes = pl.ds(m * 128, 128)
            o2, lse2 = _attn_pair_fwd(q_ref[:, lanes], kpad[pl.ds(r0, DIL_WIN), lanes], vpad[pl.ds(r0, DIL_WIN), lanes],
                                      (bias_ref[2 * m], bias_ref[2 * m + 1]), valid)
            o_ref[:, lanes] = o2
            lse_ref[:, lanes] = lse2

    out = jax.ShapeDtypeStruct((bsz, length, dil * GROUP_WIDTH), F32)
    o, lse = pl.pallas_call(
        body, name=name, grid=(bsz, dil, length // DIL_QB),
        in_specs=[pl.BlockSpec(memory_space=pltpu.SMEM), pl.BlockSpec(bkt.shape, lambda b, rho, i: (0, 0)), blk, seq, seq],
        out_specs=[blk, blk], out_shape=[out, out],
        scratch_shapes=[pltpu.VMEM((length + 2 * DIL_QB, GROUP_WIDTH), BF16), pltpu.VMEM((length + 2 * DIL_QB, GROUP_WIDTH), BF16),
                        pltpu.VMEM((N_HEADS, DIL_QB, DIL_WIN), F32)],
        compiler_params=_cparams(("arbitrary", "arbitrary", "arbitrary")),
    )(table, bkt, view(qb), view(kb), view(vb))
    return o.reshape(bsz, SEQ, GROUP_WIDTH), lse.reshape(bsz, SEQ, GROUP_WIDTH)


def _dil_bwd(qb, kb, vb, do, dlse, table, dil, name):
    bsz = qb.shape[0]
    length = SEQ // dil
    nqb = length // DIL_QB
    bkt = jnp.asarray(_np_dil_buckets(dil))
    view, blk, seq = _dil_specs(bsz, length, dil)

    def body(tab_ref, bkt_ref, q_ref, k_ref, v_ref, do_ref, dlse_ref, dq_ref, dk_ref, dv_ref, dsc_ref, kpad, vpad, bias_ref):
        i = pl.program_id(2)

        @pl.when((pl.program_id(0) == 0) & (pl.program_id(1) == 0) & (i == 0))
        def _():
            _dil_build_bias(tab_ref, bkt_ref, bias_ref, dil)
            dsc_ref[...] = jnp.zeros_like(dsc_ref)

        @pl.when(i == 0)
        def _():
            _dil_fill_pad(kpad, k_ref, length)
            _dil_fill_pad(vpad, v_ref, length)
            dk_ref[...] = jnp.zeros_like(dk_ref)
            dv_ref[...] = jnp.zeros_like(dv_ref)

        r0 = pl.multiple_of(i * DIL_QB, DIL_QB)
        valid = _dil_valid(r0, length)
        for m in range(N_HEADS // 2):
            lanes = pl.ds(m * 128, 128)
            dq2, dk2, dv2, ds_heads = _attn_pair_bwd(
                q_ref[:, lanes], kpad[pl.ds(r0, DIL_WIN), lanes], vpad[pl.ds(r0, DIL_WIN), lanes],
                (bias_ref[2 * m], bias_ref[2 * m + 1]), valid, do_ref[:, lanes], dlse_ref[:, lanes])
            dq_ref[:, lanes] = dq2
            dsc_ref[2 * m] += ds_heads[0]
            dsc_ref[2 * m + 1] += ds_heads[1]
            for part, live in ((0, i >= 1), (1, None), (2, i <= nqb - 2)):
                def add(part=part, dk2=dk2, dv2=dv2, lanes=lanes):
                    rows = pl.ds(pl.multiple_of((i - 1 + part) * DIL_QB, DIL_QB), DIL_QB)
                    dk_ref[rows, lanes] += dk2[part * DIL_QB:(part + 1) * DIL_QB]
                    dv_ref[rows, lanes] += dv2[part * DIL_QB:(part + 1) * DIL_QB]
                if live is None:
                    add()
                else:
                    pl.when(live)(add)

    out = jax.ShapeDtypeStruct((bsz, length, dil * GROUP_WIDTH), F32)
    dsc_shape = (N_HEADS, DIL_QB, DIL_WIN)
    dq, dk, dv, dsc = pl.pallas_call(
        body, name=name, grid=(bsz, dil, nqb),
        in_specs=[pl.BlockSpec(memory_space=pltpu.SMEM), pl.BlockSpec(bkt.shape, lambda b, rho, i: (0, 0)), blk, seq, seq, blk, blk],
        out_specs=[blk, seq, seq, pl.BlockSpec(dsc_shape, lambda b, rho, i: (0, 0, 0))],
        out_shape=[out, out, out, jax.ShapeDtypeStruct(dsc_shape, F32)],
        scratch_shapes=[pltpu.VMEM((length + 2 * DIL_QB, GROUP_WIDTH), BF16), pltpu.VMEM((length + 2 * DIL_QB, GROUP_WIDTH), BF16),
                        pltpu.VMEM(dsc_shape, F32)],
        compiler_params=_cparams(("arbitrary", "arbitrary", "arbitrary")),
    )(table, bkt, view(qb), view(kb), view(vb), view(do), view(dlse))
    shp = (bsz, SEQ, GROUP_WIDTH)
    return dq.reshape(shp), dk.reshape(shp), dv.reshape(shp), dsc


def _headnorm(x, g, mavg):
    return x * lax.rsqrt(_head_stat(x * x, mavg) + RMS_EPS) * g


def _fold_gain(dg_full, fold):
    return jnp.dot(jnp.broadcast_to(dg_full, (8, dg_full.shape[1])), fold, precision=HI, preferred_element_type=F32)


def _bprep_fn(qp, kp, gq, gk, mavg):
    return _headnorm(qp, gq, mavg) * (HEAD_DIM ** -0.5), _headnorm(kp, gk, mavg)


def _bprep_fwd(z, gq, gk, name):
    mavg = jnp.asarray(_np_group_avg(GROUP_WIDTH))

    def fn(qp, kp, vp, gqv, gkv, mv):
        qb, kb = _bprep_fn(qp, kp, gqv, gkv, mv)
        return qb, kb, vp

    w = GROUP_WIDTH
    return _rowmap(fn, [(z, w, 2), (z, w, 3), (z, w, 4)], [gq, gk, mavg], [(w, BF16)] * 3, [], name, 512, z.shape[0])


def _bprep_bwd(z, dqs, dks, dvs, gq, gk, name):
    mavg = jnp.asarray(_np_group_avg(GROUP_WIDTH))
    fold = jnp.asarray(_np_tile_fold(GROUP_WIDTH))

    def fn(qp, kp, dq0, dq1, dq2, dk0, dk1, dk2, dv0, dv1, dv2, gqv, gkv, mv, fv):
        _, vjp = jax.vjp(functools.partial(_bprep_fn, mavg=mv), qp, kp, gqv, gkv)
        dqp, dkp, dgq, dgk = vjp((dq0 + dq1 + dq2, dk0 + dk1 + dk2))
        return dqp, dkp, dv0 + dv1 + dv2, _fold_gain(dgq, fv), _fold_gain(dgk, fv)

    w = GROUP_WIDTH
    rows = [(z, w, 2), (z, w, 3)] + [(a, w, 0) for a in (*dqs, *dks, *dvs)]
    return _rowmap(fn, rows, [gq, gk, mavg, fold], [(w, F32)] * 3, [(8, HEAD_DIM)] * 2, name, 256, z.shape[0])


def _mixture_fn(o0, o1, o2, l0, l1, l2):
    m = lax.stop_gradient(jnp.maximum(jnp.maximum(l0, l1), l2))
    e0, e1, e2 = jnp.exp(l0 - m), jnp.exp(l1 - m), jnp.exp(l2 - m)
    return (e0 * o0 + e1 * o1 + e2 * o2) / (e0 + e1 + e2)


def _mixture_fwd(os_, ls_, name):
    w = GROUP_WIDTH
    rows = [(a, w, 0) for a in (*os_, *ls_)]
    return _rowmap(lambda *v: (_mixture_fn(*v),), rows, [], [(w, F32)], [], name, 512, os_[0].shape[0])[0]


def _mixture_bwd(os_, ls_, dy, name):
    w = GROUP_WIDTH

    def fn(*v):
        _, vjp = jax.vjp(_mixture_fn, *v[:6])
        return vjp(v[6])

    rows = [(a, w, 0) for a in (*os_, *ls_, dy)]
    return _rowmap(fn, rows, [], [(w, F32)] * 6, [], name, 512, dy.shape[0])


def _relbias_fold(dscs, name):
    bkts = [jnp.asarray(_np_dil_buckets(dil)) for _, dil in DIL_PATTERNS]
    npat = len(DIL_PATTERNS)

    def body(*refs):
        bkt_refs, d_refs, o_ref = refs[:npat], refs[npat:-1], refs[-1]
        row = lax.broadcasted_iota(jnp.int32, (REL_BUCKETS, 128), 0)
        lane = lax.broadcasted_iota(jnp.int32, (REL_BUCKETS, 128), 1)
        out = jnp.zeros((REL_BUCKETS, 128), F32)
        for p, (_, dil) in enumerate(DIL_PATTERNS):
            bkt = bkt_refs[p][...]
            for h in range(N_HEADS):
                d = d_refs[2 * p][h] + d_refs[2 * p + 1][h]
                for b in _dil_live_buckets(dil):
                    val = jnp.sum(jnp.where(bkt == b, d, 0.0), keepdims=True)
                    out = out + jnp.where((row == b) & (lane == h), val, 0.0)
        o_ref[...] = out

    return pl.pallas_call(
        body, name=name, out_shape=jax.ShapeDtypeStruct((REL_BUCKETS, 128), F32), compiler_params=_cparams(),
    )(*bkts, *dscs)


DPREP_TM = 512


def _dprep_fn(qp, kp, vp, gq, gk, cq, sq, ck, sk, mavg_q, mavg_k, perm_q, perm_k, expand):
    rot = lambda x, perm: jnp.dot(x, perm, precision=HI, preferred_element_type=F32)
    qn = _headnorm(qp, gq, mavg_q)
    kn = _headnorm(kp, gk, mavg_k)
    qr = (qn * cq + rot(qn, perm_q) * sq) * (HEAD_DIM ** -0.5)
    kr = kn * ck + rot(kn, perm_k) * sk
    return qr, rot(kr, expand), rot(vp, expand)


def _dprep_consts():
    cq, sq = _np_rope_tables(N_HEADS)
    ck, sk = _np_rope_tables(KV_WIDTH // HEAD_DIM)
    tables = [jnp.asarray(a) for a in (cq, sq, ck, sk)]
    mats = [jnp.asarray(a) for a in (_np_group_avg(GROUP_WIDTH), _np_group_avg(KV_WIDTH), _np_rope_partner(GROUP_WIDTH),
                                      _np_rope_partner(KV_WIDTH), _np_kv_expand())]
    per = SEQ // DPREP_TM
    w, kw = GROUP_WIDTH, KV_WIDTH
    table_rows = [(tables[0], w, 0, per), (tables[1], w, 0, per), (tables[2], kw, 0, per), (tables[3], kw, 0, per)]
    return table_rows, mats


def _dprep_fwd(z, gq, gk, name):
    table_rows, mats = _dprep_consts()
    w, kw = GROUP_WIDTH, KV_WIDTH

    def fn(qp, kp, vp, cq, sq, ck, sk, gqv, gkv, *m):
        return _dprep_fn(qp, kp, vp, gqv, gkv, cq, sq, ck, sk, *m)

    return _rowmap(fn, [(z, w, 7), (z, kw, 32), (z, kw, 33)] + table_rows, [gq, gk] + mats, [(w, BF16)] * 3, [], name,
                   DPREP_TM, z.shape[0])


def _dprep_bwd(z, dq, dkx, dvx, gq, gk, name):
    table_rows, mats = _dprep_consts()
    fold_q = jnp.asarray(_np_tile_fold(GROUP_WIDTH))
    fold_k = jnp.asarray(_np_tile_fold(KV_WIDTH))
    w, kw = GROUP_WIDTH, KV_WIDTH

    def fn(qp, kp, vp, dqv, dkv, dvv, cq, sq, ck, sk, gqv, gkv, fq, fk, *m):
        f = lambda a, b, c, d, e: _dprep_fn(a, b, c, d, e, cq, sq, ck, sk, *m)
        _, vjp = jax.vjp(f, qp, kp, vp, gqv, gkv)
        dqp, dkp, dvp, dgq, dgk = vjp((dqv, dkv, dvv))
        return dqp, dkp, dvp, _fold_gain(dgq, fq), _fold_gain(dgk, fk)

    return _rowmap(fn, [(z, w, 7), (z, kw, 32), (z, kw, 33), (dq, w, 0), (dkx, w, 0), (dvx, w, 0)] + table_rows,
                   [gq, gk, fold_q, fold_k] + mats, [(w, F32), (kw, F32), (kw, F32)], [(8, HEAD_DIM)] * 2, name,
                   DPREP_TM, z.shape[0])


GQA_QB = 256


def _gqa_fwd(q, kx, vx, name):
    bsz = q.shape[0]
    blk = pl.BlockSpec((None, GQA_QB, GROUP_WIDTH), lambda b, i: (b, i, 0))
    seq = pl.BlockSpec((None, SEQ, GROUP_WIDTH), lambda b, i: (b, 0, 0))

    def body(q_ref, k_ref, v_ref, o_ref):
        for m in range(N_HEADS // 2):
            lanes = pl.ds(m * 128, 128)
            o_ref[:, lanes] = _attn_pair_fwd(q_ref[:, lanes], k_ref[:, lanes], v_ref[:, lanes], None, None)[0]

    return pl.pallas_call(
        body, name=name, grid=(bsz, SEQ // GQA_QB), in_specs=[blk, seq, seq], out_specs=blk,
        out_shape=jax.ShapeDtypeStruct((bsz, SEQ, GROUP_WIDTH), F32), compiler_params=_cparams(("parallel", "parallel")),
    )(q, kx, vx)


def _gqa_bwd(q, kx, vx, do, name):
    bsz = q.shape[0]
    blk = pl.BlockSpec((None, GQA_QB, GROUP_WIDTH), lambda b, i: (b, i, 0))
    seq = pl.BlockSpec((None, SEQ, GROUP_WIDTH), lambda b, i: (b, 0, 0))

    def body(q_ref, k_ref, v_ref, do_ref, dq_ref, dk_ref, dv_ref):
        @pl.when(pl.program_id(1) == 0)
        def _():
            dk_ref[...] = jnp.zeros_like(dk_ref)
            dv_ref[...] = jnp.zeros_like(dv_ref)

        for m in range(N_HEADS // 2):
            lanes = pl.ds(m * 128, 128)
            dq2, dk2, dv2, _ = _attn_pair_bwd(q_ref[:, lanes], k_ref[:, lanes], v_ref[:, lanes], None, None, do_ref[:, lanes], None)
            dq_ref[:, lanes] = dq2
            dk_ref[:, lanes] += dk2
            dv_ref[:, lanes] += dv2

    out = jax.ShapeDtypeStruct((bsz, SEQ, GROUP_WIDTH), F32)
    return pl.pallas_call(
        body, name=name, grid=(bsz, SEQ // GQA_QB), in_specs=[blk, seq, seq, blk], out_specs=[blk, seq, seq],
        out_shape=[out, out, out], compiler_params=_cparams(("parallel", "arbitrary")),
    )(q, kx, vx, do)


CONV_TILE = 64
CONV_LEAD = 16
CONV_WINDOW = CONV_TILE + 32


def _glu(a, g):
    return a * jax.nn.sigmoid(g)


def _conv_post(c, b, ln_g, ln_b):
    x = c + b
    xc = x - jnp.mean(x, axis=-1, keepdims=True)
    y = xc * lax.rsqrt(jnp.mean(xc * xc, axis=-1, keepdims=True) + LN_EPS) * ln_g + ln_b
    return y * jax.nn.sigmoid(y)


def _conv_shifted(win, offset):
    return pltpu.roll(win, CONV_WINDOW - offset, 0)[:CONV_TILE]


def _conv_fill(pad_ref, value_of_tile):
    zeros = jnp.zeros((CONV_LEAD, GROUP_WIDTH), F32)
    pad_ref[pl.ds(0, CONV_LEAD), :] = zeros
    pad_ref[pl.ds(CONV_LEAD + SEQ, CONV_LEAD), :] = zeros

    def step(t, carry):
        r0 = pl.multiple_of(t * CONV_TILE, CONV_TILE)
        pad_ref[pl.ds(CONV_LEAD + r0, CONV_TILE), :] = value_of_tile(r0)
        return carry

    lax.fori_loop(0, SEQ // CONV_TILE, step, 0)


def _conv_tile(pad_ref, w_ref, r0, flip):
    win = pad_ref[pl.ds(r0, CONV_WINDOW), :]
    acc = jnp.zeros((CONV_TILE, GROUP_WIDTH), F32)
    for k in range(CONV_WIDTH):
        offset = (CONV_WIDTH - k) if flip else (k + 1)
        acc = acc + w_ref[pl.ds(k, 1), :] * _conv_shifted(win, offset)
    return acc


def _conv_fwd(z3, w, b, ln_g, ln_b, name):
    bsz = z3.shape[0]
    seq = lambda cb: pl.BlockSpec((None, SEQ, GROUP_WIDTH), functools.partial(lambda i, cb: (i, 0, cb), cb=cb))
    full = lambda a: pl.BlockSpec(a.shape, lambda i: (0,) * a.ndim)

    def body(a_ref, g_ref, w_ref, b_ref, lg_ref, lb_ref, y_ref, pad_ref):
        _conv_fill(pad_ref, lambda r0: _glu(a_ref[pl.ds(r0, CONV_TILE), :], g_ref[pl.ds(r0, CONV_TILE), :]))

        def step(t, carry):
            r0 = pl.multiple_of(t * CONV_TILE, CONV_TILE)
            y_ref[pl.ds(r0, CONV_TILE), :] = _conv_post(_conv_tile(pad_ref, w_ref, r0, False), b_ref[...], lg_ref[...], lb_ref[...])
            return carry

        lax.fori_loop(0, SEQ // CONV_TILE, step, 0)

    return pl.pallas_call(
        body, name=name, grid=(bsz,), in_specs=[seq(5), seq(6), full(w), full(b), full(ln_g), full(ln_b)], out_specs=seq(0),
        out_shape=jax.ShapeDtypeStruct((bsz, SEQ, GROUP_WIDTH), F32),
        scratch_shapes=[pltpu.VMEM((SEQ + 2 * CONV_LEAD, GROUP_WIDTH), F32)], compiler_params=_cparams(("parallel",)),
    )(z3, z3, w, b, ln_g, ln_b)


def _conv_bwd(z3, dy, w, b, ln_g, ln_b, name):
    bsz = z3.shape[0]
    seq = lambda cb: pl.BlockSpec((None, SEQ, GROUP_WIDTH), functools.partial(lambda i, cb: (i, 0, cb), cb=cb))
    full = lambda a: pl.BlockSpec(a.shape, lambda i: (0,) * a.ndim)
    vec = pl.BlockSpec((1, GROUP_WIDTH), lambda i: (0, 0))

    def body(a_ref, g_ref, dy_ref, w_ref, b_ref, lg_ref, lb_ref, da_ref, dg_ref, dw_ref, db_ref, dlg_ref, dlb_ref, hpad, dpad, dw8):
        @pl.when(pl.program_id(0) == 0)
        def _():
            dw8[...] = jnp.zeros_like(dw8)
            db_ref[...] = jnp.zeros_like(db_ref)
            dlg_ref[...] = jnp.zeros_like(dlg_ref)
            dlb_ref[...] = jnp.zeros_like(dlb_ref)

        _conv_fill(hpad, lambda r0: _glu(a_ref[pl.ds(r0, CONV_TILE), :], g_ref[pl.ds(r0, CONV_TILE), :]))
        zeros = jnp.zeros((CONV_LEAD, GROUP_WIDTH), F32)
        dpad[pl.ds(0, CONV_LEAD), :] = zeros
        dpad[pl.ds(CONV_LEAD + SEQ, CONV_LEAD), :] = zeros

        def through_post(t, carry):
            r0 = pl.multiple_of(t * CONV_TILE, CONV_TILE)
            conv = _conv_tile(hpad, w_ref, r0, False)
            _, vjp = jax.vjp(_conv_post, conv, b_ref[...], lg_ref[...], lb_ref[...])
            dconv, db, dlg, dlb = vjp(dy_ref[pl.ds(r0, CONV_TILE), :])
            db_ref[...] += db
            dlg_ref[...] += dlg
            dlb_ref[...] += dlb
            dpad[pl.ds(CONV_LEAD + r0, CONV_TILE), :] = dconv
            win = hpad[pl.ds(r0, CONV_WINDOW), :]
            for k in range(CONV_WIDTH):
                prod = dconv * _conv_shifted(win, k + 1)
                part = prod[0:8]
                for j in range(1, CONV_TILE // 8):
                    part = part + prod[8 * j:8 * j + 8]
                dw8[k] += part
            return carry

        lax.fori_loop(0, SEQ // CONV_TILE, through_post, 0)

        def through_glu(t, carry):
            r0 = pl.multiple_of(t * CONV_TILE, CONV_TILE)
            dh = _conv_tile(dpad, w_ref, r0, True)
            rows = pl.ds(r0, CONV_TILE)
            _, vjp = jax.vjp(_glu, a_ref[rows, :], g_ref[rows, :])
            da, dg = vjp(dh)
            da_ref[rows, :] = da
            dg_ref[rows, :] = dg
            return carry

        lax.fori_loop(0, SEQ // CONV_TILE, through_glu, 0)
        dw_ref[...] = jnp.sum(dw8[...], axis=1)

    out = jax.ShapeDtypeStruct((bsz, SEQ, GROUP_WIDTH), F32)
    v = jax.ShapeDtypeStruct((1, GROUP_WIDTH), F32)
    return pl.pallas_call(
        body, name=name, grid=(bsz,), in_specs=[seq(5), seq(6), seq(0), full(w), full(b), full(ln_g), full(ln_b)],
        out_specs=[seq(0), seq(0), pl.BlockSpec((CONV_WIDTH, GROUP_WIDTH), lambda i: (0, 0)), vec, vec, vec],
        out_shape=[out, out, jax.ShapeDtypeStruct((CONV_WIDTH, GROUP_WIDTH), F32), v, v, v],
        scratch_shapes=[pltpu.VMEM((SEQ + 2 * CONV_LEAD, GROUP_WIDTH), F32), pltpu.VMEM((SEQ + 2 * CONV_LEAD, GROUP_WIDTH), F32),
                        pltpu.VMEM((CONV_WIDTH, 8, GROUP_WIDTH), F32)],
        compiler_params=_cparams(("arbitrary",)),
    )(z3, z3, dy, w, b, ln_g, ln_b)


def _mixnorm_fwd(ys, gains, name):
    w = GROUP_WIDTH

    def fn(*v):
        return (jnp.concatenate([_rms(v[i], v[4 + i]) for i in range(4)], axis=-1),)

    return _rowmap(fn, [(y, w, 0) for y in ys], list(gains), [(4 * w, BF16)], [], name, 512, ys[0].shape[0])[0]


def _mixnorm_bwd(dyn, ys, gains, name):
    w = GROUP_WIDTH

    def fn(*v):
        dys, dgs = [], []
        for i in range(4):
            _, vjp = jax.vjp(_rms, v[4 + i], v[8 + i])
            dy, dg = vjp(v[i])
            dys.append(dy)
            dgs.append(dg)
        return (*dys, *dgs)

    rows = [(dyn, w, i) for i in range(4)] + [(y, w, 0) for y in ys]
    return _rowmap(fn, rows, list(gains), [(w, F32)] * 4, [(1, w)] * 4, name, 512, dyn.shape[0])


def _adamw_fn(w, g, m, v):
    m = ADAM_B1 * m + (1.0 - ADAM_B1) * g
    v = ADAM_B2 * v + (1.0 - ADAM_B2) * (g * g)
    m_hat = m / (1.0 - ADAM_B1 ** ADAM_STEP)
    v_hat = v / (1.0 - ADAM_B2 ** ADAM_STEP)
    delta = -ADAM_LR * (m_hat / (jnp.sqrt(v_hat) + ADAM_EPS) + ADAM_WD * w)
    return delta, m, v


def _adamw(w, g, m, v, name):
    r, c = w.shape
    tm = _pick(r, (256, 128, 64, 32, 16, 8))
    return _rowmap(_adamw_fn, [(a, c, 0) for a in (w, g, m, v)], [], [(c, F32)] * 3, [], name, tm, r)


def _layer_params(l, small, big):
    tile_row = lambda g, n: jnp.tile(g, n)[None, :]
    row = lambda g: g[None, :]
    w_s = small["sgu_w"][l].astype(BF16)
    return dict(
        norm1_g=row(small["norm1_g"][l]), norm2_g=row(small["norm2_g"][l]),
        w_s=w_s, w_s_t=jnp.swapaxes(w_s, 1, 2), bm=jnp.repeat(small["sgu_b"][l].T, HEAD_DIM, axis=1),
        gq_dil=tile_row(small["dil_qn_g"][l], N_HEADS), gk_dil=tile_row(small["dil_kn_g"][l], N_HEADS),
        conv_w=small["conv_w"][l], conv_b=row(small["conv_b"][l]), conv_ln_g=row(small["conv_ln_g"][l]),
        conv_ln_b=row(small["conv_ln_b"][l]),
        gq_gqa=tile_row(small["gqa_qn_g"][l], N_HEADS), gk_gqa=tile_row(small["gqa_kn_g"][l], KV_WIDTH // HEAD_DIM),
        mix_g=[row(small["mix_norm_g"][l][i * GROUP_WIDTH:(i + 1) * GROUP_WIDTH]) for i in range(4)],
        **big,
    )


def _layer_fwd(x, p, table, bsz, tag):
    t = x.shape[0]
    seq3 = lambda a: a.reshape(bsz, SEQ, a.shape[-1])
    flat = lambda a: a.reshape(t, a.shape[-1])
    h1 = _rmsnorm_fwd(x, p["norm1_g"], tag + "rms1")
    z = _matmul([(h1, p["w_in"])], "nn", F32, tag + "mm_z")
    y_a = _sgu_fwd(z, p["w_s"], p["bm"], tag + "sgu_fwd")
    qb, kb, vb = _bprep_fwd(z, p["gq_dil"], p["gk_dil"], tag + "dil_prep")
    outs, lses = [], []
    for _, dil in DIL_PATTERNS:
        o, lse = _dil_fwd(seq3(qb), seq3(kb), seq3(vb), table, dil, f"{tag}dil{dil}_fwd")
        outs.append(flat(o))
        lses.append(flat(lse))
    y_b = _mixture_fwd(outs, lses, tag + "dil_mix")
    y_c = flat(_conv_fwd(seq3(z), p["conv_w"], p["conv_b"], p["conv_ln_g"], p["conv_ln_b"], tag + "conv_fwd"))
    qd, kx, vx = _dprep_fwd(z, p["gq_gqa"], p["gk_gqa"], tag + "gqa_prep")
    y_d = flat(_gqa_fwd(seq3(qd), seq3(kx), seq3(vx), tag + "gqa_fwd"))
    ys = [y_a, y_b, y_c, y_d]
    yn = _mixnorm_fwd(ys, p["mix_g"], tag + "mixnorm")
    x_mid = _matmul([(yn, p["w_out"])], "nn", F32, tag + "mm_out", residual=x)
    h2 = _rmsnorm_fwd(x_mid, p["norm2_g"], tag + "rms2")
    act, gate, up = _ffn_up(h2, p["w_gate"], p["w_up"], tag + "ffn_up")
    x_out = _matmul([(act, p["w_down"])], "nn", F32, tag + "mm_down", residual=x_mid)
    saved = dict(x=x, h1=h1, z=z, qb=qb, kb=kb, vb=vb, outs=outs, lses=lses, qd=qd, kx=kx, vx=vx, ys=ys, yn=yn, x_mid=x_mid,
                 h2=h2, act=act, gate=gate, up=up)
    return x_out, saved


def _layer_bwd(dx_out, s, p, table, bsz, tag):
    t = dx_out.shape[0]
    seq3 = lambda a: a.reshape(bsz, SEQ, a.shape[-1])
    flat = lambda a: a.reshape(t, a.shape[-1])
    z = s["z"]
    big, small = {}, {}
    big["w_down"] = _matmul([(s["act"], dx_out)], "tn", BF16, tag + "mm_dwdown").reshape(N_CHIPS, FFN_HIDDEN // N_CHIPS, D_MODEL)
    dgate, dup = _ffn_down_bwd(dx_out, p["w_down"], s["gate"], s["up"], tag + "ffn_dact")
    big["w_gate"] = _matmul([(s["h2"], dgate)], "tn", BF16, tag + "mm_dwgate", slabs=N_CHIPS)
    big["w_up"] = _matmul([(s["h2"], dup)], "tn", BF16, tag + "mm_dwup", slabs=N_CHIPS)
    dh2 = _matmul([(dgate, p["w_gate"]), (dup, p["w_up"])], "nt", F32, tag + "mm_dh2")
    dx_mid, dg2 = _rmsnorm_bwd(dh2, s["x_mid"], p["norm2_g"], dx_out, tag + "rms2_bwd")
    small["norm2_g"] = dg2[0]
    dyn = _matmul([(dx_mid, p["w_out"])], "nt", F32, tag + "mm_dyn")
    big["w_out"] = _matmul([(s["yn"], dx_mid)], "tn", BF16, tag + "mm_dwout").reshape(N_CHIPS, D_MODEL // N_CHIPS, D_MODEL)
    *dys, dga, dgb, dgc, dgd = _mixnorm_bwd(dyn, s["ys"], p["mix_g"], tag + "mixnorm_bwd")
    small["mix_norm_g"] = jnp.concatenate([dga[0], dgb[0], dgc[0], dgd[0]])
    du, dv, dws, dbs = _sgu_bwd(z, dys[0], p["w_s"], p["w_s_t"], p["bm"], tag + "sgu_bwd")
    small["sgu_w"] = dws
    small["sgu_b"] = dbs[:, :N_HEADS].T
    *douts, dl0, dl1, dl2 = _mixture_bwd(s["outs"], s["lses"], dys[1], tag + "dil_mix_bwd")
    dlses = [dl0, dl1, dl2]
    dqs, dks, dvs, dscs = [], [], [], []
    for i, (_, dil) in enumerate(DIL_PATTERNS):
        dq, dk, dvv, dsc = _dil_bwd(seq3(s["qb"]), seq3(s["kb"]), seq3(s["vb"]), seq3(douts[i]), seq3(dlses[i]), table, dil,
                                    f"{tag}dil{dil}_bwd")
        dqs.append(flat(dq))
        dks.append(flat(dk))
        dvs.append(flat(dvv))
        dscs.append(dsc)
    dbq, dbk, dbv, dgq, dgk = _bprep_bwd(z, dqs, dks, dvs, p["gq_dil"], p["gk_dil"], tag + "dil_prep_bwd")
    small["dil_qn_g"], small["dil_kn_g"] = dgq[0], dgk[0]
    dca, dcg, dcw, dcb, dclg, dclb = _conv_bwd(seq3(z), seq3(dys[2]), p["conv_w"], p["conv_b"], p["conv_ln_g"], p["conv_ln_b"],
                                               tag + "conv_bwd")
    small["conv_w"], small["conv_b"], small["conv_ln_g"], small["conv_ln_b"] = dcw, dcb[0], dclg[0], dclb[0]
    dqd, dkx, dvx = _gqa_bwd(seq3(s["qd"]), seq3(s["kx"]), seq3(s["vx"]), seq3(dys[3]), tag + "gqa_bwd")
    ddq, ddk, ddv, dgq, dgk = _dprep_bwd(z, flat(dqd), flat(dkx), flat(dvx), p["gq_gqa"], p["gk_gqa"], tag + "gqa_prep_bwd")
    small["gqa_qn_g"], small["gqa_kn_g"] = dgq[0], dgk[0]
    dz = jnp.concatenate([du, dv, dbq, dbk, dbv, flat(dca), flat(dcg), ddq, ddk, ddv], axis=1)
    dz4 = dz.reshape(t, N_CHIPS, IN_WIDTH // N_CHIPS).transpose(1, 0, 2)
    big["w_in"] = _matmul([(s["h1"], dz4)], "tn", BF16, tag + "mm_dwin", slabs=N_CHIPS)
    dh1 = _matmul([(dz, p["w_in"])], "nt", F32, tag + "mm_dh1")
    dx, dg1 = _rmsnorm_bwd(dh1, s["x"], p["norm1_g"], dx_mid, tag + "rms1_bwd")
    small["norm1_g"] = dg1[0]
    return dx, big, small, dscs


def _local_step(x, target, small, bigs, bsz):
    table = small["rel_bias"]
    params = [_layer_params(l, small, bigs[l]) for l in range(DEPTH)]
    saved = []
    h = x
    for l in range(DEPTH):
        h, sv = _layer_fwd(h, params[l], table, bsz, f"l{l}_")
        saved.append(sv)
    dh, loss = _loss_fwd_bwd(h, target, "loss")
    big_grads, small_grads, dscs = [None] * DEPTH, [None] * DEPTH, [None] * DEPTH
    for l in reversed(range(DEPTH)):
        dh, big_grads[l], small_grads[l], dscs[l] = _layer_bwd(dh, saved[l], params[l], table, bsz, f"l{l}_")
    fold_in = [dscs[l][i] for i in range(len(DIL_PATTERNS)) for l in range(DEPTH)]
    stacked = {k: jnp.stack([small_grads[l][k] for l in range(DEPTH)]) for k in small_grads[0]}
    stacked["rel_bias"] = _relbias_fold(fold_in, "relbias_fold")[:, :N_HEADS]
    return loss, dh, big_grads, stacked


def _mesh_pos():
    return lax.axis_index("x"), lax.axis_index("y"), lax.axis_index("c")


def _other_chips(x, y):
    return [(1 - x, y), (x, 1 - y), (1 - x, 1 - y)]


_ANY = pl.BlockSpec(memory_space=pl.ANY)


def _swap_sibling(arrs, name):
    n = len(arrs)

    def body(*refs):
        in_refs, out_refs, send_sems, recv_sems = refs[:n], refs[n:2 * n], refs[2 * n], refs[2 * n + 1]
        x, y, c = _mesh_pos()
        copies = [pltpu.make_async_remote_copy(src_ref=in_refs[k], dst_ref=out_refs[k], send_sem=send_sems.at[k],
                                               recv_sem=recv_sems.at[k], device_id=(x, y, 1 - c), device_id_type=MESH)
                  for k in range(n)]
        for cp in copies:
            cp.start()
        for cp in copies:
            cp.wait()

    return pl.pallas_call(
        body, name=name, in_specs=[_ANY] * n, out_specs=[_ANY] * n,
        out_shape=[jax.ShapeDtypeStruct(a.shape, a.dtype) for a in arrs],
        scratch_shapes=[pltpu.SemaphoreType.DMA((n,)), pltpu.SemaphoreType.DMA((n,))],
    )(*arrs)


def _exchange_chips(arrs, name):
    n = len(arrs)

    def body(*refs):
        in_refs, out_refs = refs[:n], refs[n:2 * n]
        send_sems, recv_sems, local_sems = refs[2 * n:]
        x, y, c = _mesh_pos()
        me = 2 * x + y
        chips = _other_chips(x, y)
        local, sends = [], []
        for k in range(n):
            cp = pltpu.make_async_copy(in_refs[k].at[me], out_refs[k].at[me], local_sems.at[k])
            cp.start()
            local.append(cp)
            for j, (cx, cy) in enumerate(chips):
                cp = pltpu.make_async_remote_copy(src_ref=in_refs[k].at[2 * cx + cy], dst_ref=out_refs[k].at[me],
                                                  send_sem=send_sems.at[3 * k + j], recv_sem=recv_sems.at[3 * k + j],
                                                  device_id=(cx, cy, c), device_id_type=MESH)
                cp.start()
                sends.append(cp)
        for k in range(n):
            for j, (cx, cy) in enumerate(chips):
                there = out_refs[k].at[2 * cx + cy]
                pltpu.make_async_remote_copy(src_ref=there, dst_ref=there, send_sem=send_sems.at[3 * k + j],
                                             recv_sem=recv_sems.at[3 * k + j], device_id=(cx, cy, c), device_id_type=MESH).wait_recv()
        for cp in sends:
            cp.wait_send()
        for cp in local:
            cp.wait()

    return pl.pallas_call(
        body, name=name, in_specs=[_ANY] * n, out_specs=[_ANY] * n,
        out_shape=[jax.ShapeDtypeStruct(a.shape, a.dtype) for a in arrs],
        scratch_shapes=[pltpu.SemaphoreType.DMA((3 * n,)), pltpu.SemaphoreType.DMA((3 * n,)), pltpu.SemaphoreType.DMA((n,))],
    )(*arrs)


def _allgather_chips(shards, name):
    n = len(shards)

    def body(*refs):
        in_refs, out_refs = refs[:n], refs[n:2 * n]
        send_sems, recv_sems, local_sems = refs[2 * n:]
        x, y, c = _mesh_pos()
        me = 2 * x + y
        chips = _other_chips(x, y)
        local, sends = [], []
        for k in range(n):
            half = shards[k].shape[0] // 2
            mine, theirs = pl.ds(c * half, half), pl.ds((1 - c) * half, half)
            cp = pltpu.make_async_copy(in_refs[k], out_refs[k].at[me], local_sems.at[k])
            cp.start()
            local.append(cp)
            for j, (cx, cy) in enumerate(chips):
                cp = pltpu.make_async_remote_copy(src_ref=in_refs[k].at[mine], dst_ref=out_refs[k].at[me, mine],
                                                  send_sem=send_sems.at[6 * k + j], recv_sem=recv_sems.at[6 * k + j],
                                                  device_id=(cx, cy, c), device_id_type=MESH)
                cp.start()
                sends.append(cp)
        for k in range(n):
            half = shards[k].shape[0] // 2
            mine = pl.ds(c * half, half)
            for j, (cx, cy) in enumerate(chips):
                landed = out_refs[k].at[2 * cx + cy, mine]
                pltpu.make_async_remote_copy(src_ref=landed, dst_ref=landed, send_sem=send_sems.at[6 * k + j],
                                             recv_sem=recv_sems.at[6 * k + j], device_id=(cx, cy, c), device_id_type=MESH).wait_recv()
                cp = pltpu.make_async_remote_copy(src_ref=landed, dst_ref=landed, send_sem=send_sems.at[6 * k + 3 + j],
                                                  recv_sem=recv_sems.at[6 * k + 3 + j], device_id=(x, y, 1 - c), device_id_type=MESH)
                cp.start()
                sends.append(cp)
        for k in range(n):
            half = shards[k].shape[0] // 2
            theirs = pl.ds((1 - c) * half, half)
            for j, (cx, cy) in enumerate(chips):
                passed = out_refs[k].at[2 * cx + cy, theirs]
                pltpu.make_async_remote_copy(src_ref=passed, dst_ref=passed, send_sem=send_sems.at[6 * k + 3 + j],
                                             recv_sem=recv_sems.at[6 * k + 3 + j], device_id=(x, y, 1 - c), device_id_type=MESH).wait_recv()
        for cp in sends:
            cp.wait_send()
        for cp in local:
            cp.wait()

    return pl.pallas_call(
        body, name=name, in_specs=[_ANY] * n, out_specs=[_ANY] * n,
        out_shape=[jax.ShapeDtypeStruct((N_CHIPS, *a.shape), a.dtype) for a in shards],
        scratch_shapes=[pltpu.SemaphoreType.DMA((6 * n,)), pltpu.SemaphoreType.DMA((6 * n,)), pltpu.SemaphoreType.DMA((n,))],
    )(*shards)


N_DEV = 8


def _allgather_sum_small(block, name):
    m_per, n = block.shape

    def body(x_ref, out_ref, sum_ref, send_sems, recv_sems, local_sem):
        x, y, c = _mesh_pos()
        me, sibling = (x, y, c), (x, y, 1 - c)
        chips = _other_chips(x, y)

        def rows(px, py, pc):
            return out_ref.at[pl.ds((4 * px + 2 * py + pc) * m_per, m_per), :]

        def copy(k, blk, to, src=None):
            return pltpu.make_async_remote_copy(src_ref=rows(*blk) if src is None else src, dst_ref=rows(*blk),
                                                send_sem=send_sems.at[k], recv_sem=recv_sems.at[k], device_id=to, device_id_type=MESH)

        mine = pltpu.make_async_copy(x_ref, rows(*me), local_sem)
        mine.start()
        first = [copy(0, me, sibling, src=x_ref)]
        first += [copy(1 + j, me, (*chip, c), src=x_ref) for j, chip in enumerate(chips)]
        for cp in first:
            cp.start()
        passed = [copy(4 + j, (*chip, c), sibling) for j, chip in enumerate(chips)]
        for j, chip in enumerate(chips):
            copy(1 + j, (*chip, c), me).wait_recv()
            passed[j].start()
        copy(0, sibling, me).wait_recv()
        for j, chip in enumerate(chips):
            copy(4 + j, (*chip, 1 - c), me).wait_recv()
        for cp in first + passed:
            cp.wait_send()
        mine.wait()
        total = out_ref[pl.ds(0, m_per), :]
        for d in range(1, N_DEV):
            total = total + out_ref[pl.ds(d * m_per, m_per), :]
        sum_ref[...] = total

    vmem = pl.BlockSpec(memory_space=pltpu.VMEM)
    return pl.pallas_call(
        body, name=name, in_specs=[vmem], out_specs=[vmem, vmem],
        out_shape=[jax.ShapeDtypeStruct((N_DEV * m_per, n), F32), jax.ShapeDtypeStruct((m_per, n), F32)],
        scratch_shapes=[pltpu.SemaphoreType.DMA((7,)), pltpu.SemaphoreType.DMA((7,)), pltpu.SemaphoreType.DMA],
        compiler_params=pltpu.CompilerParams(vmem_limit_bytes=V7X_VMEM_LIMIT),
    )(block)


WEIGHTS = ("rel_bias", "norm1_g", "w_in", "sgu_w", "sgu_b", "dil_qn_g", "dil_kn_g", "conv_w", "conv_b", "conv_ln_g", "conv_ln_b",
           "gqa_qn_g", "gqa_kn_g", "mix_norm_g", "w_out", "norm2_g", "w_gate", "w_up", "w_down")
SHARDED = ("w_in", "w_out", "w_gate", "w_up", "w_down")
COLUMN_SHARDED = ("w_in", "w_gate", "w_up")
REPLICATED = tuple(k for k in WEIGHTS if k not in SHARDED and k != "conv_w")


def _pack(parts):
    flat = jnp.concatenate([p.reshape(-1) for p in parts])
    pad = (-flat.shape[0]) % (8 * 128)
    return jnp.pad(flat, (0, pad)).reshape(-1, 128)


def _unpack(buf, shapes):
    flat = buf.reshape(-1)
    out, at = [], 0
    for s in shapes:
        size = math.prod(s)
        out.append(flat[at:at + size].reshape(s))
        at += size
    return out


def _add_pairs(a, b, out_dtype, name):
    r, c = a.shape
    return _rowmap(lambda u, v: (u.astype(F32) + v.astype(F32),), [(a, c, 0), (b, c, 0)], [], [(c, out_dtype)], [], name,
                   _pick(r, (256, 128, 64, 32, 16)), r)[0]


def _add_four(parts, name):
    r, c = parts[0].shape
    fn = lambda p0, p1, p2, p3: (((p0.astype(F32) + p1.astype(F32)) + p2.astype(F32)) + p3.astype(F32),)
    return _rowmap(fn, [(p, c, 0) for p in parts], [], [(c, F32)], [], name, _pick(r, (256, 128, 64, 32, 16)), r)[0]


def _reduce_scatter(grads, c, tag):
    halves = [g.shape[1] // 2 for g in grads]
    mine = [lax.dynamic_slice_in_dim(g, c * h, h, axis=1) for g, h in zip(grads, halves)]
    other = [lax.dynamic_slice_in_dim(g, (1 - c) * h, h, axis=1) for g, h in zip(grads, halves)]
    from_sibling = _swap_sibling(other, tag + "rs_pair")
    flat2 = lambda a: a.reshape(-1, a.shape[-1])
    chip_sums = [_add_pairs(flat2(a), flat2(b), BF16, f"{tag}rs_add2_{k}").reshape(a.shape)
                 for k, (a, b) in enumerate(zip(mine, from_sibling))]
    from_chips = _exchange_chips(chip_sums, tag + "rs_chips")
    totals = [_add_four([f[i] for i in range(N_CHIPS)], f"{tag}rs_add4_{k}") for k, f in enumerate(from_chips)]
    from_sibling = _swap_sibling(totals, tag + "rs_share")
    out = []
    for g, h, t_mine, t_sib in zip(grads, halves, totals, from_sibling):
        full = jnp.zeros((2 * h, g.shape[2]), F32)
        full = lax.dynamic_update_slice_in_dim(full, t_mine, c * h, axis=0)
        out.append(lax.dynamic_update_slice_in_dim(full, t_sib, (1 - c) * h, axis=0))
    return out


def kernel(x, rel_bias, norm1_g, w_in, sgu_w, sgu_b, dil_qn_g, dil_kn_g, conv_w, conv_b, conv_ln_g, conv_ln_b, gqa_qn_g, gqa_kn_g, mix_norm_g, w_out, norm2_g, w_gate, w_up, w_down, loss_target, m_rel_bias, m_norm1_g, m_w_in, m_sgu_w, m_sgu_b, m_dil_qn_g, m_dil_kn_g, m_conv_w, m_conv_b, m_conv_ln_g, m_conv_ln_b, m_gqa_qn_g, m_gqa_kn_g, m_mix_norm_g, m_w_out, m_norm2_g, m_w_gate, m_w_up, m_w_down, v_rel_bias, v_norm1_g, v_w_in, v_sgu_w, v_sgu_b, v_dil_qn_g, v_dil_kn_g, v_conv_w, v_conv_b, v_conv_ln_g, v_conv_ln_b, v_gqa_qn_g, v_gqa_kn_g, v_mix_norm_g, v_w_out, v_norm2_g, v_w_gate, v_w_up, v_w_down):
    w = dict(rel_bias=rel_bias, norm1_g=norm1_g, w_in=w_in, sgu_w=sgu_w, sgu_b=sgu_b, dil_qn_g=dil_qn_g, dil_kn_g=dil_kn_g,
             conv_w=conv_w, conv_b=conv_b, conv_ln_g=conv_ln_g, conv_ln_b=conv_ln_b, gqa_qn_g=gqa_qn_g, gqa_kn_g=gqa_kn_g,
             mix_norm_g=mix_norm_g, w_out=w_out, norm2_g=norm2_g, w_gate=w_gate, w_up=w_up, w_down=w_down)
    m = dict(rel_bias=m_rel_bias, norm1_g=m_norm1_g, w_in=m_w_in, sgu_w=m_sgu_w, sgu_b=m_sgu_b, dil_qn_g=m_dil_qn_g,
             dil_kn_g=m_dil_kn_g, conv_w=m_conv_w, conv_b=m_conv_b, conv_ln_g=m_conv_ln_g, conv_ln_b=m_conv_ln_b,
             gqa_qn_g=m_gqa_qn_g, gqa_kn_g=m_gqa_kn_g, mix_norm_g=m_mix_norm_g, w_out=m_w_out, norm2_g=m_norm2_g,
             w_gate=m_w_gate, w_up=m_w_up, w_down=m_w_down)
    v = dict(rel_bias=v_rel_bias, norm1_g=v_norm1_g, w_in=v_w_in, sgu_w=v_sgu_w, sgu_b=v_sgu_b, dil_qn_g=v_dil_qn_g,
             dil_kn_g=v_dil_kn_g, conv_w=v_conv_w, conv_b=v_conv_b, conv_ln_g=v_conv_ln_g, conv_ln_b=v_conv_ln_b,
             gqa_qn_g=v_gqa_qn_g, gqa_kn_g=v_gqa_kn_g, mix_norm_g=v_mix_norm_g, w_out=v_w_out, norm2_g=v_norm2_g,
             w_gate=v_w_gate, w_up=v_w_up, w_down=v_w_down)
    bsz = x.shape[0]
    t = bsz * SEQ
    xi, yi, ci = _mesh_pos()
    chip = 2 * xi + yi
    conv_cols = conv_w.shape[-1]

    conv_rows = DEPTH * CONV_WIDTH
    conv_block = jnp.pad(conv_w.reshape(conv_rows, conv_cols), ((0, (-conv_rows) % 8), (0, 0)))
    every, _ = _allgather_sum_small(conv_block, "conv_w_gather")
    every = every.reshape(N_DEV, conv_block.shape[0], conv_cols)
    conv_w_full = jnp.concatenate([every[2 * j, :conv_rows].reshape(DEPTH, CONV_WIDTH, conv_cols) for j in range(N_CHIPS)], axis=-1)

    stacked = [w[k].astype(BF16).reshape(-1, w[k].shape[-1]) for k in SHARDED]
    gathered = dict(zip(SHARDED, _allgather_chips(stacked, "weights_gather")))
    bigs = []
    for l in range(DEPTH):
        big = {}
        for k in SHARDED:
            rows, cols = w[k].shape[1:]
            g = gathered[k].reshape(N_CHIPS, DEPTH, rows, cols)[:, l]
            big[k] = g.transpose(1, 0, 2).reshape(rows, N_CHIPS * cols) if k in COLUMN_SHARDED else g.reshape(N_CHIPS * rows, cols)
        bigs.append(big)

    small = {k: w[k] for k in REPLICATED}
    small["conv_w"] = conv_w_full
    loss, dx, big_grads, small_grads = _local_step(x.reshape(t, D_MODEL), loss_target.reshape(t, D_MODEL), small, bigs, bsz)
    loss = lax.psum(loss[0, 0], ("x", "y", "c"))

    reduced = [_reduce_scatter([big_grads[l][k] for k in SHARDED], ci, f"l{l}_") for l in range(DEPTH)]
    grads, deltas, new_m, new_v = {}, {}, {}, {}
    for i, k in enumerate(SHARDED):
        g2 = jnp.concatenate([reduced[l][i] for l in range(DEPTH)], axis=0)
        d2, m2, v2 = _adamw(w[k].reshape(g2.shape), g2, m[k].reshape(g2.shape), v[k].reshape(g2.shape), "adamw_" + k)
        grads[k], deltas[k], new_m[k], new_v[k] = (a.reshape(w[k].shape) for a in (g2, d2, m2, v2))

    names = REPLICATED + ("conv_w",)
    shapes = [small_grads[k].shape for k in names]
    _, summed = _allgather_sum_small(_pack([small_grads[k] for k in names]), "small_grads_sum")
    summed_parts = dict(zip(names, _unpack(summed, shapes)))
    rep_shapes = [w[k].shape for k in REPLICATED]
    packed = [_pack([src[k] for k in REPLICATED]) for src in (w, {k: summed_parts[k] for k in REPLICATED}, m, v)]
    d_p, m_p, v_p = _adamw(*packed, "adamw_replicated")
    for k, gk, dk, mk, vk in zip(REPLICATED, _unpack(packed[1], rep_shapes), _unpack(d_p, rep_shapes), _unpack(m_p, rep_shapes),
                                 _unpack(v_p, rep_shapes)):
        grads[k], deltas[k], new_m[k], new_v[k] = gk, dk, mk, vk
    g_conv = lax.dynamic_slice_in_dim(summed_parts["conv_w"], chip * conv_cols, conv_cols, axis=2)
    packed = [_pack([a]) for a in (conv_w, g_conv, m["conv_w"], v["conv_w"])]
    d_p, m_p, v_p = _adamw(*packed, "adamw_conv_w")
    grads["conv_w"] = g_conv
    deltas["conv_w"], new_m["conv_w"], new_v["conv_w"] = (_unpack(a, [conv_w.shape])[0] for a in (d_p, m_p, v_p))

    return (loss, dx.reshape(x.shape), *[grads[k] for k in WEIGHTS], *[deltas[k] for k in WEIGHTS],
            *[new_m[k] for k in WEIGHTS], *[new_v[k] for k in WEIGHTS])
```

```python
import functools
import math

import numpy as np
import jax
import jax.numpy as jnp
from jax import lax
from jax.experimental import pallas as pl
from jax.experimental.pallas import tpu as pltpu

F32 = jnp.float32
BF16 = jnp.bfloat16

D_MODEL = 2048
SEQ = 2048
DEPTH = 2
HEAD_DIM = 64
GROUP_WIDTH = 512
N_HEADS = 8
SGU_CHUNK = 128
DIL_PATTERNS = ((128, 1), (512, 4), (2048, 16))
DIL_HALF = 64
CONV_WIDTH = 31
KV_WIDTH = 128
GRID_W = 64
ROPE_THETA = 10000.0
REL_BUCKETS = 32
REL_MAX_DIST = 1024
FFN_HIDDEN = 5632
IN_WIDTH = 4352
RMS_EPS = 1e-6
LN_EPS = 1e-5
ADAM_LR = 0.001
ADAM_B1 = 0.9
ADAM_B2 = 0.999
ADAM_EPS = 1e-08
ADAM_WD = 0.01
ADAM_STEP = 10
N_CHIPS = 4

V7X_VMEM_LIMIT = 56 * 1024 * 1024
HI = lax.Precision.HIGHEST
MESH = pl.DeviceIdType.MESH


def _cparams(sem=None):
    return pltpu.CompilerParams(dimension_semantics=sem, vmem_limit_bytes=V7X_VMEM_LIMIT)


def _pick(n, cands):
    for c in cands:
        if n % c == 0:
            return c
    raise ValueError(f"no tile for {n}")


_DIMS = {"nn": (((1,), (0,)), ((), ())), "nt": (((1,), (1,)), ((), ())), "tn": (((0,), (0,)), ((), ()))}


def _matmul(pairs, mode, out_dtype, name, residual=None, slabs=1):
    a0, b0 = pairs[0]
    b3 = b0.ndim == 3
    if mode == "nn":
        (M, K), N = a0.shape, b0.shape[1]
    elif mode == "nt":
        (M, K), N = a0.shape, b0.shape[0]
    else:
        (K, M) = a0.shape
        N = b0.shape[-1] if b3 else b0.shape[1] // slabs
    tm = _pick(M, (1024, 512, 256))
    tn = _pick(N, (1024, 2176, 1408, 1088, 512, 256))
    tk = _pick(K, (512, 2176, 256))
    nk = K // tk
    nj = N // tn
    npairs = len(pairs)

    if mode in ("nn", "nt"):
        a_spec = pl.BlockSpec((tm, tk), lambda s, i, j, k: (i, k))
    else:
        a_spec = pl.BlockSpec((tk, tm), lambda s, i, j, k: (k, i))
    if mode == "nt":
        b_spec = pl.BlockSpec((tn, tk), lambda s, i, j, k: (j, k))
    elif b3:
        b_spec = pl.BlockSpec((None, tk, tn), lambda s, i, j, k: (s, k, j))
    else:
        b_spec = pl.BlockSpec((tk, tn), lambda s, i, j, k: (k, s * nj + j))
    if slabs > 1:
        o_spec = pl.BlockSpec((None, tm, tn), lambda s, i, j, k: (s, i, j))
        o_shape = jax.ShapeDtypeStruct((slabs, M, N), out_dtype)
    else:
        o_spec = pl.BlockSpec((tm, tn), lambda s, i, j, k: (i, j))
        o_shape = jax.ShapeDtypeStruct((M, N), out_dtype)
    in_specs = [a_spec] * npairs + [b_spec] * npairs
    args = [a for a, _ in pairs] + [b for _, b in pairs]
    if residual is not None:
        in_specs.append(pl.BlockSpec((tm, tn), lambda s, i, j, k: (i, j)))
        args.append(residual)
    dims = _DIMS[mode]

    def body(*refs):
        a_refs, b_refs = refs[:npairs], refs[npairs:2 * npairs]
        res_ref = refs[2 * npairs] if residual is not None else None
        o_ref, acc_ref = refs[-2], refs[-1]
        k = pl.program_id(3)

        @pl.when(k == 0)
        def _():
            acc_ref[...] = jnp.zeros_like(acc_ref)

        part = None
        for a_ref, b_ref in zip(a_refs, b_refs):
            d = lax.dot_general(a_ref[...].astype(BF16), b_ref[...].astype(BF16), dims, preferred_element_type=F32)
            part = d if part is None else part + d
        acc_ref[...] += part

        @pl.when(k == nk - 1)
        def _():
            r = acc_ref[...]
            if res_ref is not None:
                r = r + res_ref[...]
            o_ref[...] = r.astype(out_dtype)

    return pl.pallas_call(
        body, name=name, grid=(slabs, M // tm, nj, nk), in_specs=in_specs, out_specs=o_spec, out_shape=o_shape,
        scratch_shapes=[pltpu.VMEM((tm, tn), F32)],
        compiler_params=_cparams(("parallel", "parallel", "parallel", "arbitrary")),
    )(*args)


def _rowmap(fn, rows, fulls, row_outs, acc_outs, name, tm, n_rows):
    nr, nf, nro = len(rows), len(fulls), len(row_outs)
    rows = [r if len(r) == 4 else (*r, n_rows // tm) for r in rows]
    in_specs = [pl.BlockSpec((tm, w), functools.partial(lambda i, cb, per: (i % per, cb), cb=cb, per=per)) for _, w, cb, per in rows]
    in_specs += [pl.BlockSpec(f.shape, lambda i: (0,) * f.ndim) for f in fulls]
    out_specs = [pl.BlockSpec((tm, w), lambda i: (i, 0)) for w, _ in row_outs]
    out_specs += [pl.BlockSpec(s, functools.partial(lambda i, n: (0,) * n, n=len(s))) for s in acc_outs]
    out_shape = [jax.ShapeDtypeStruct((n_rows, w), dt) for w, dt in row_outs]
    out_shape += [jax.ShapeDtypeStruct(s, F32) for s in acc_outs]

    def body(*refs):
        ins = [r[...] for r in refs[:nr + nf]]
        outs = fn(*ins)
        o_refs = refs[nr + nf:]
        for o_ref, val in zip(o_refs[:nro], outs[:nro]):
            o_ref[...] = val.astype(o_ref.dtype)
        if acc_outs:
            first = pl.program_id(0) == 0
            for o_ref, val in zip(o_refs[nro:], outs[nro:]):
                @pl.when(first)
                def _(o_ref=o_ref, val=val):
                    o_ref[...] = val

                @pl.when(jnp.logical_not(first))
                def _(o_ref=o_ref, val=val):
                    o_ref[...] += val

    res = pl.pallas_call(
        body, name=name, grid=(n_rows // tm,), in_specs=in_specs, out_specs=out_specs, out_shape=out_shape,
        compiler_params=_cparams(("arbitrary",) if acc_outs else ("parallel",)),
    )(*[r[0] for r in rows], *fulls)
    return res


def _rms(x, g):
    return x * lax.rsqrt(jnp.mean(x * x, axis=-1, keepdims=True) + RMS_EPS) * g


def _rmsnorm_fwd(x, g, name):
    t = x.shape[0]
    return _rowmap(lambda xv, gv: (_rms(xv, gv),), [(x, D_MODEL, 0)], [g], [(D_MODEL, BF16)], [], name, 512, t)[0]


def _rmsnorm_bwd(dh, x, g, dres, name, follow=None):
    t = x.shape[0]

    def fn(dhv, xv, drv, gv, *_):
        _, vjp = jax.vjp(_rms, xv, gv)
        dx, dg = vjp(dhv)
        return dx + drv, dg

    fulls = [g] if follow is None else [g, follow]
    return _rowmap(fn, [(dh, D_MODEL, 0), (x, D_MODEL, 0), (dres, D_MODEL, 0)], fulls, [(D_MODEL, F32)], [(1, D_MODEL)],
                   name, 256, t)


def _loss_fwd_bwd(y, target, name):
    t = y.shape[0]

    def fn(yv, tv):
        e = yv - tv
        return e * (1.0 / D_MODEL), (0.5 / D_MODEL) * jnp.sum(e * e, keepdims=True)

    return _rowmap(fn, [(y, D_MODEL, 0), (target, D_MODEL, 0)], [], [(D_MODEL, F32)], [(1, 1)], name, 512, t)


def _ffn_up(h, wg, wu, name):
    t, n = h.shape[0], wg.shape[1]
    tm, tn = 512, 512

    def body(h_ref, wg_ref, wu_ref, act_ref, g_ref, u_ref):
        hv = h_ref[...]
        g = jnp.dot(hv, wg_ref[...], preferred_element_type=F32)
        u = jnp.dot(hv, wu_ref[...], preferred_element_type=F32)
        act_ref[...] = (g * jax.nn.sigmoid(g) * u).astype(BF16)
        g_ref[...] = g.astype(BF16)
        u_ref[...] = u.astype(BF16)

    o_spec = pl.BlockSpec((tm, tn), lambda i, j: (i, j))
    o_shape = jax.ShapeDtypeStruct((t, n), BF16)
    return pl.pallas_call(
        body, name=name, grid=(t // tm, n // tn),
        in_specs=[pl.BlockSpec((tm, D_MODEL), lambda i, j: (i, 0)), pl.BlockSpec((D_MODEL, tn), lambda i, j: (0, j)),
                  pl.BlockSpec((D_MODEL, tn), lambda i, j: (0, j))],
        out_specs=[o_spec] * 3, out_shape=[o_shape] * 3, compiler_params=_cparams(("parallel", "parallel")),
    )(h, wg, wu)


def _ffn_down_bwd(dy, wd, g, u, name):
    t, n = dy.shape[0], wd.shape[0]
    tm, tn = 512, 512

    def body(dy_ref, wd_ref, g_ref, u_ref, dg_ref, du_ref):
        dact = lax.dot_general(dy_ref[...].astype(BF16), wd_ref[...], _DIMS["nt"], preferred_element_type=F32)
        gv = g_ref[...].astype(F32)
        uv = u_ref[...].astype(F32)
        sg = jax.nn.sigmoid(gv)
        silu = gv * sg
        du_ref[...] = (dact * silu).astype(BF16)
        dg_ref[...] = (dact * uv * (sg + silu * (1.0 - sg))).astype(BF16)

    o_spec = pl.BlockSpec((tm, tn), lambda i, j: (i, j))
    o_shape = jax.ShapeDtypeStruct((t, n), BF16)
    return pl.pallas_call(
        body, name=name, grid=(t // tm, n // tn),
        in_specs=[pl.BlockSpec((tm, D_MODEL), lambda i, j: (i, 0)), pl.BlockSpec((tn, D_MODEL), lambda i, j: (j, 0)),
                  o_spec, o_spec],
        out_specs=[o_spec] * 2, out_shape=[o_shape] * 2, compiler_params=_cparams(("parallel", "parallel")),
    )(dy, wd, g, u)


def _np_group_avg(width, group=HEAD_DIM):
    i = np.arange(width)
    return ((i[:, None] // group) == (i[None, :] // group)).astype(np.float32) / group


def _np_tile_fold(width, group=HEAD_DIM):
    return ((np.arange(width)[:, None] % group) == np.arange(group)[None, :]).astype(np.float32)


def _np_group_fold(width, group=HEAD_DIM, pad=128):
    return ((np.arange(width)[:, None] // group) == np.arange(pad)[None, :]).astype(np.float32)


def _np_rope_partner(width):
    i = np.arange(width)
    partner = np.where(i % 32 < 16, i + 16, i - 16)
    return (partner[:, None] == i[None, :]).astype(np.float32)


def _np_kv_expand():
    src = np.arange(KV_WIDTH)
    dst = np.arange(GROUP_WIDTH)
    return ((src[:, None] // HEAD_DIM == dst[None, :] // (4 * HEAD_DIM)) & (src[:, None] % HEAD_DIM == dst[None, :] % HEAD_DIM)).astype(np.float32)


def _np_rope_tables(n_heads):
    t = np.arange(SEQ)
    pos = {0: (t // GRID_W).astype(np.float32), 1: (t % GRID_W).astype(np.float32)}
    freqs = (ROPE_THETA ** (-np.arange(16, dtype=np.float32) / 16)).astype(np.float32)
    cos_parts, sin_parts = [], []
    for axis in (0, 1):
        ang = pos[axis][:, None] * freqs[None, :]
        c, s = np.cos(ang).astype(np.float32), np.sin(ang).astype(np.float32)
        cos_parts += [c, c]
        sin_parts += [-s, s]
    cos = np.concatenate(cos_parts, axis=1)
    sin = np.concatenate(sin_parts, axis=1)
    return np.tile(cos, (1, n_heads)), np.tile(sin, (1, n_heads))


def _np_t5_buckets(rel):
    nb = REL_BUCKETS // 2
    max_exact = nb // 2
    ret = np.where(rel > 0, nb, 0)
    n = np.abs(rel)
    nf = np.maximum(n, 1).astype(np.float32)
    large = max_exact + (np.log(nf / max_exact) / math.log(REL_MAX_DIST / max_exact) * (nb - max_exact)).astype(np.int32)
    large = np.minimum(large, nb - 1)
    return (ret + np.where(n < max_exact, n, large)).astype(np.int32)


DIL_QB = 128
DIL_WIN = DIL_QB + 2 * DIL_HALF


def _np_dil_buckets(dil):
    off = np.arange(DIL_WIN)[None, :] - DIL_HALF - np.arange(DIL_QB)[:, None]
    return _np_t5_buckets(off * dil)


def _dil_live_buckets(dil):
    off = np.arange(-DIL_HALF, DIL_HALF + 1)
    return sorted(set(_np_t5_buckets(off * dil).tolist()))


def _head_stat(x, mavg):
    return jnp.dot(x, mavg, precision=HI, preferred_element_type=F32)


def _gelu(x):
    return 0.5 * x * (1.0 + jnp.tanh(math.sqrt(2.0 / math.pi) * (x + 0.044715 * (x * x * x))))


def _sgu_pre(u_pre, v_pre, mavg):
    v = _gelu(v_pre)
    xc = v - _head_stat(v, mavg)
    vn = xc * lax.rsqrt(_head_stat(xc * xc, mavg) + LN_EPS)
    return _gelu(u_pre), vn


def _sgu_mix(w_ref, vnb, bm):
    lane_group = lax.broadcasted_iota(jnp.int32, (1, GROUP_WIDTH), 1) // HEAD_DIM
    mixed = bm
    for g in range(N_HEADS):
        r = jnp.dot(w_ref[g], vnb, preferred_element_type=F32)
        mixed = mixed + jnp.where(lane_group == g, r, 0.0)
    return mixed


SGU_TM = 512


def _sgu_fwd(z, w_s, bm, name):
    t = z.shape[0]
    mavg = jnp.asarray(_np_group_avg(GROUP_WIDTH))

    def body(u_ref, v_ref, w_ref, bm_ref, mavg_ref, y_ref):
        for c in range(SGU_TM // SGU_CHUNK):
            rows = pl.ds(c * SGU_CHUNK, SGU_CHUNK)
            u, vn = _sgu_pre(u_ref[rows, :], v_ref[rows, :], mavg_ref[...])
            y_ref[rows, :] = u * _sgu_mix(w_ref, vn.astype(BF16), bm_ref[...])

    full = lambda a: pl.BlockSpec(a.shape, lambda i: (0,) * a.ndim)
    return pl.pallas_call(
        body, name=name, grid=(t // SGU_TM,),
        in_specs=[pl.BlockSpec((SGU_TM, GROUP_WIDTH), lambda i: (i, 0)), pl.BlockSpec((SGU_TM, GROUP_WIDTH), lambda i: (i, 1)),
                  full(w_s), full(bm), full(mavg)],
        out_specs=pl.BlockSpec((SGU_TM, GROUP_WIDTH), lambda i: (i, 0)),
        out_shape=jax.ShapeDtypeStruct((t, GROUP_WIDTH), F32), compiler_params=_cparams(("parallel",)),
    )(z, z, w_s, bm, mavg)


def _sgu_bwd(z, dy, w_s, w_s_t, bm, name):
    t = z.shape[0]
    mavg = jnp.asarray(_np_group_avg(GROUP_WIDTH))
    gfold = jnp.asarray(_np_group_fold(GROUP_WIDTH))

    def body(u_ref, v_ref, dy_ref, w_ref, wt_ref, bm_ref, mavg_ref, gfold_ref, du_ref, dv_ref, dw_ref, dbs_ref, dbm_ref):
        @pl.when(pl.program_id(0) == 0)
        def _():
            dw_ref[...] = jnp.zeros_like(dw_ref)
            dbm_ref[...] = jnp.zeros_like(dbm_ref)

        lane_group = lax.broadcasted_iota(jnp.int32, (1, GROUP_WIDTH), 1) // HEAD_DIM
        for c in range(SGU_TM // SGU_CHUNK):
            rows = pl.ds(c * SGU_CHUNK, SGU_CHUNK)
            (u, vn), pre_vjp = jax.vjp(functools.partial(_sgu_pre, mavg=mavg_ref[...]), u_ref[rows, :], v_ref[rows, :])
            vnb = vn.astype(BF16)
            mixed = _sgu_mix(w_ref, vnb, bm_ref[...])
            dyv = dy_ref[rows, :]
            dmixed = dyv * u
            dbm_ref[...] += dmixed
            dvn = jnp.zeros_like(vn)
            for g in range(N_HEADS):
                dm_g = jnp.where(lane_group == g, dmixed, 0.0).astype(BF16)
                dw_ref[g] += lax.dot_general(dm_g, vnb, _DIMS["nt"], preferred_element_type=F32)
                dvn = dvn + jnp.dot(wt_ref[g], dm_g, preferred_element_type=F32)
            du_pre, dv_pre = pre_vjp((dyv * mixed, dvn))
            du_ref[rows, :] = du_pre
            dv_ref[rows, :] = dv_pre

        @pl.when(pl.program_id(0) == t // SGU_TM - 1)
        def _():
            dbs_ref[...] = jnp.dot(dbm_ref[...], gfold_ref[...], precision=HI, preferred_element_type=F32)

    full = lambda a: pl.BlockSpec(a.shape, lambda i: (0,) * a.ndim)
    row = pl.BlockSpec((SGU_TM, GROUP_WIDTH), lambda i: (i, 0))
    return pl.pallas_call(
        body, name=name, grid=(t // SGU_TM,),
        in_specs=[row, pl.BlockSpec((SGU_TM, GROUP_WIDTH), lambda i: (i, 1)), row, full(w_s), full(w_s_t), full(bm), full(mavg),
                  full(gfold)],
        out_specs=[row, row, pl.BlockSpec((N_HEADS, SGU_CHUNK, SGU_CHUNK), lambda i: (0, 0, 0)),
                   pl.BlockSpec((SGU_CHUNK, 128), lambda i: (0, 0))],
        out_shape=[jax.ShapeDtypeStruct((t, GROUP_WIDTH), F32)] * 2 + [jax.ShapeDtypeStruct((N_HEADS, SGU_CHUNK, SGU_CHUNK), F32),
                                                                      jax.ShapeDtypeStruct((SGU_CHUNK, 128), F32)],
        scratch_shapes=[pltpu.VMEM((SGU_CHUNK, GROUP_WIDTH), F32)],
        compiler_params=_cparams(("arbitrary",)),
    )(z, z, dy, w_s, w_s_t, bm, mavg, gfold)


def _pair_softmax(q2, k2, hh, bias, valid):
    head = (lax.broadcasted_iota(jnp.int32, (1, 2 * HEAD_DIM), 1) // HEAD_DIM) == hh
    qm = jnp.where(head, q2, jnp.zeros_like(q2))
    s = lax.dot_general(qm, k2, _DIMS["nt"], preferred_element_type=F32)
    if bias is not None:
        s = s + bias
    if valid is not None:
        s = jnp.where(valid, s, -1e30)
    m = jnp.max(s, axis=-1, keepdims=True)
    e = jnp.exp(s - m)
    l = jnp.sum(e, axis=-1, keepdims=True)
    return head, qm, e / l, m + jnp.log(l)


def _attn_pair_fwd(q2, k2, v2, biases, valid):
    o2 = lse2 = None
    for hh in range(2):
        head, _, p, lse = _pair_softmax(q2, k2, hh, None if biases is None else biases[hh], valid)
        oh = jnp.dot(p.astype(BF16), v2, preferred_element_type=F32)
        o_h = jnp.where(head, oh, 0.0)
        l_h = jnp.where(head, lse, 0.0)
        o2 = o_h if o2 is None else o2 + o_h
        lse2 = l_h if lse2 is None else lse2 + l_h
    return o2, lse2


def _attn_pair_bwd(q2, k2, v2, biases, valid, do2, dlse2):
    dq2 = dk2 = dv2 = None
    ds_heads = []
    for hh in range(2):
        head, qm, p, _ = _pair_softmax(q2, k2, hh, None if biases is None else biases[hh], valid)
        dom = jnp.where(head, do2, 0.0).astype(BF16)
        dp = lax.dot_general(dom, v2, _DIMS["nt"], preferred_element_type=F32)
        delta = jnp.sum(dp * p, axis=-1, keepdims=True)
        if dlse2 is not None:
            delta = delta - jnp.sum(jnp.where(head, dlse2, 0.0), axis=-1, keepdims=True)
        ds = p * (dp - delta)
        dsb = ds.astype(BF16)
        dq_h = jnp.where(head, jnp.dot(dsb, k2, preferred_element_type=F32), 0.0)
        dk_h = lax.dot_general(dsb, qm, _DIMS["tn"], preferred_element_type=F32)
        dv_h = lax.dot_general(p.astype(BF16), dom, _DIMS["tn"], preferred_element_type=F32)
        dq2 = dq_h if dq2 is None else dq2 + dq_h
        dk2 = dk_h if dk2 is None else dk2 + dk_h
        dv2 = dv_h if dv2 is None else dv2 + dv_h
        ds_heads.append(ds)
    return dq2, dk2, dv2, ds_heads


def _dil_valid(r0, length):
    row = lax.broadcasted_iota(jnp.int32, (DIL_QB, DIL_WIN), 0)
    col = lax.broadcasted_iota(jnp.int32, (DIL_QB, DIL_WIN), 1)
    off = col - DIL_HALF - row
    kpos = r0 - DIL_HALF + col
    return (jnp.abs(off) <= DIL_HALF) & (kpos >= 0) & (kpos < length)


def _dil_build_bias(tab_ref, bkt_ref, bias_ref, dil):
    bkt = bkt_ref[...]
    for h in range(N_HEADS):
        acc = jnp.zeros((DIL_QB, DIL_WIN), F32)
        for b in _dil_live_buckets(dil):
            acc = jnp.where(bkt == b, tab_ref[b, h], acc)
        bias_ref[h] = acc


def _dil_fill_pad(pad_ref, src_ref, length):
    zeros = jnp.zeros((DIL_HALF, GROUP_WIDTH), pad_ref.dtype)
    pad_ref[pl.ds(0, DIL_HALF), :] = zeros
    pad_ref[pl.ds(DIL_HALF + length, DIL_HALF), :] = zeros
    pad_ref[pl.ds(DIL_HALF, length), :] = src_ref[...]


def _dil_specs(bsz, length, dil):
    view = lambda a: a.reshape(bsz, length, dil * GROUP_WIDTH)
    blk = pl.BlockSpec((None, DIL_QB, GROUP_WIDTH), lambda b, rho, i: (b, i, rho))
    seq = pl.BlockSpec((None, length, GROUP_WIDTH), lambda b, rho, i: (b, 0, rho))
    return view, blk, seq


def _dil_fwd(qb, kb, vb, table, dil, name):
    bsz = qb.shape[0]
    length = SEQ // dil
    bkt = jnp.asarray(_np_dil_buckets(dil))
    view, blk, seq = _dil_specs(bsz, length, dil)

    def body(tab_ref, bkt_ref, q_ref, k_ref, v_ref, o_ref, lse_ref, kpad, vpad, bias_ref):
        i = pl.program_id(2)

        @pl.when((pl.program_id(0) == 0) & (pl.program_id(1) == 0) & (i == 0))
        def _():
            _dil_build_bias(tab_ref, bkt_ref, bias_ref, dil)

        @pl.when(i == 0)
        def _():
            _dil_fill_pad(kpad, k_ref, length)
            _dil_fill_pad(vpad, v_ref, length)

        r0 = pl.multiple_of(i * DIL_QB, DIL_QB)
        valid = _dil_valid(r0, length)
        for m in range(N_HEADS // 2):
            lanes = pl.ds(m * 128, 128)
            o2, lse2 = _attn_pair_fwd(q_ref[:, lanes], kpad[pl.ds(r0, DIL_WIN), lanes], vpad[pl.ds(r0, DIL_WIN), lanes],
                                      (bias_ref[2 * m], bias_ref[2 * m + 1]), valid)
            o_ref[:, lanes] = o2
            lse_ref[:, lanes] = lse2

    out = jax.ShapeDtypeStruct((bsz, length, dil * GROUP_WIDTH), F32)
    o, lse = pl.pallas_call(
        body, name=name, grid=(bsz, dil, length // DIL_QB),
        in_specs=[pl.BlockSpec(memory_space=pltpu.SMEM), pl.BlockSpec(bkt.shape, lambda b, rho, i: (0, 0)), blk, seq, seq],
        out_specs=[blk, blk], out_shape=[out, out],
        scratch_shapes=[pltpu.VMEM((length + 2 * DIL_HALF, GROUP_WIDTH), BF16), pltpu.VMEM((length + 2 * DIL_HALF, GROUP_WIDTH), BF16),
                        pltpu.VMEM((N_HEADS, DIL_QB, DIL_WIN), F32)],
        compiler_params=_cparams(("arbitrary", "arbitrary", "arbitrary")),
    )(table, bkt, view(qb), view(kb), view(vb))
    return o.reshape(bsz, SEQ, GROUP_WIDTH), lse.reshape(bsz, SEQ, GROUP_WIDTH)


def _dil_bwd(qb, kb, vb, do, dlse, table, dil, name):
    bsz = qb.shape[0]
    length = SEQ // dil
    nqb = length // DIL_QB
    bkt = jnp.asarray(_np_dil_buckets(dil))
    view, blk, seq = _dil_specs(bsz, length, dil)

    def body(tab_ref, bkt_ref, q_ref, k_ref, v_ref, do_ref, dlse_ref, dq_ref, dk_ref, dv_ref, dsc_ref, kpad, vpad, bias_ref):
        i = pl.program_id(2)

        @pl.when((pl.program_id(0) == 0) & (pl.program_id(1) == 0) & (i == 0))
        def _():
            _dil_build_bias(tab_ref, bkt_ref, bias_ref, dil)
            dsc_ref[...] = jnp.zeros_like(dsc_ref)

        @pl.when(i == 0)
        def _():
            _dil_fill_pad(kpad, k_ref, length)
            _dil_fill_pad(vpad, v_ref, length)
            dk_ref[...] = jnp.zeros_like(dk_ref)
            dv_ref[...] = jnp.zeros_like(dv_ref)

        r0 = pl.multiple_of(i * DIL_QB, DIL_QB)
        valid = _dil_valid(r0, length)
        for m in range(N_HEADS // 2):
            lanes = pl.ds(m * 128, 128)
            dq2, dk2, dv2, ds_heads = _attn_pair_bwd(
                q_ref[:, lanes], kpad[pl.ds(r0, DIL_WIN), lanes], vpad[pl.ds(r0, DIL_WIN), lanes],
                (bias_ref[2 * m], bias_ref[2 * m + 1]), valid, do_ref[:, lanes], dlse_ref[:, lanes])
            dq_ref[:, lanes] = dq2
            dsc_ref[2 * m] += ds_heads[0]
            dsc_ref[2 * m + 1] += ds_heads[1]
            for first, size, live in ((0, DIL_HALF, i >= 1), (DIL_HALF, DIL_QB, None), (DIL_HALF + DIL_QB, DIL_HALF, i <= nqb - 2)):
                def add(first=first, size=size, dk2=dk2, dv2=dv2, lanes=lanes):
                    rows = pl.ds(pl.multiple_of(r0 - DIL_HALF + first, DIL_HALF), size)
                    dk_ref[rows, lanes] += dk2[first:first + size]
                    dv_ref[rows, lanes] += dv2[first:first + size]
                if live is None:
                    add()
                else:
                    pl.when(live)(add)

    out = jax.ShapeDtypeStruct((bsz, length, dil * GROUP_WIDTH), F32)
    dsc_shape = (N_HEADS, DIL_QB, DIL_WIN)
    dq, dk, dv, dsc = pl.pallas_call(
        body, name=name, grid=(bsz, dil, nqb),
        in_specs=[pl.BlockSpec(memory_space=pltpu.SMEM), pl.BlockSpec(bkt.shape, lambda b, rho, i: (0, 0)), blk, seq, seq, blk, blk],
        out_specs=[blk, seq, seq, pl.BlockSpec(dsc_shape, lambda b, rho, i: (0, 0, 0))],
        out_shape=[out, out, out, jax.ShapeDtypeStruct(dsc_shape, F32)],
        scratch_shapes=[pltpu.VMEM((length + 2 * DIL_HALF, GROUP_WIDTH), BF16), pltpu.VMEM((length + 2 * DIL_HALF, GROUP_WIDTH), BF16),
                        pltpu.VMEM(dsc_shape, F32)],
        compiler_params=_cparams(("arbitrary", "arbitrary", "arbitrary")),
    )(table, bkt, view(qb), view(kb), view(vb), view(do), view(dlse))
    shp = (bsz, SEQ, GROUP_WIDTH)
    return dq.reshape(shp), dk.reshape(shp), dv.reshape(shp), dsc


def _headnorm(x, g, mavg):
    return x * lax.rsqrt(_head_stat(x * x, mavg) + RMS_EPS) * g


def _fold_gain(dg_full, fold):
    return jnp.dot(jnp.broadcast_to(dg_full, (8, dg_full.shape[1])), fold, precision=HI, preferred_element_type=F32)


def _bprep_fn(qp, kp, gq, gk, mavg):
    return _headnorm(qp, gq, mavg) * (HEAD_DIM ** -0.5), _headnorm(kp, gk, mavg)


def _bprep_fwd(z, gq, gk, name):
    mavg = jnp.asarray(_np_group_avg(GROUP_WIDTH))

    def fn(qp, kp, vp, gqv, gkv, mv):
        qb, kb = _bprep_fn(qp, kp, gqv, gkv, mv)
        return qb, kb, vp

    w = GROUP_WIDTH
    return _rowmap(fn, [(z, w, 2), (z, w, 3), (z, w, 4)], [gq, gk, mavg], [(w, BF16)] * 3, [], name, 512, z.shape[0])


def _bprep_bwd(z, dqs, dks, dvs, gq, gk, name):
    mavg = jnp.asarray(_np_group_avg(GROUP_WIDTH))
    fold = jnp.asarray(_np_tile_fold(GROUP_WIDTH))

    def fn(qp, kp, dq0, dq1, dq2, dk0, dk1, dk2, dv0, dv1, dv2, gqv, gkv, mv, fv):
        _, vjp = jax.vjp(functools.partial(_bprep_fn, mavg=mv), qp, kp, gqv, gkv)
        dqp, dkp, dgq, dgk = vjp((dq0 + dq1 + dq2, dk0 + dk1 + dk2))
        return dqp, dkp, dv0 + dv1 + dv2, _fold_gain(dgq, fv), _fold_gain(dgk, fv)

    w = GROUP_WIDTH
    rows = [(z, w, 2), (z, w, 3)] + [(a, w, 0) for a in (*dqs, *dks, *dvs)]
    return _rowmap(fn, rows, [gq, gk, mavg, fold], [(w, F32)] * 3, [(8, HEAD_DIM)] * 2, name, 256, z.shape[0])


def _mixture_fn(o0, o1, o2, l0, l1, l2):
    m = lax.stop_gradient(jnp.maximum(jnp.maximum(l0, l1), l2))
    e0, e1, e2 = jnp.exp(l0 - m), jnp.exp(l1 - m), jnp.exp(l2 - m)
    return (e0 * o0 + e1 * o1 + e2 * o2) / (e0 + e1 + e2)


def _mixture_fwd(os_, ls_, name):
    w = GROUP_WIDTH
    rows = [(a, w, 0) for a in (*os_, *ls_)]
    return _rowmap(lambda *v: (_mixture_fn(*v),), rows, [], [(w, F32)], [], name, 512, os_[0].shape[0])[0]


def _mixture_bwd(os_, ls_, dy, name):
    w = GROUP_WIDTH

    def fn(*v):
        _, vjp = jax.vjp(_mixture_fn, *v[:6])
        return vjp(v[6])

    rows = [(a, w, 0) for a in (*os_, *ls_, dy)]
    return _rowmap(fn, rows, [], [(w, F32)] * 6, [], name, 512, dy.shape[0])


def _relbias_fold(dscs, name):
    bkts = [jnp.asarray(_np_dil_buckets(dil)) for _, dil in DIL_PATTERNS]
    npat = len(DIL_PATTERNS)

    def body(*refs):
        bkt_refs, d_refs, o_ref = refs[:npat], refs[npat:-1], refs[-1]
        row = lax.broadcasted_iota(jnp.int32, (REL_BUCKETS, 128), 0)
        lane = lax.broadcasted_iota(jnp.int32, (REL_BUCKETS, 128), 1)
        out = jnp.zeros((REL_BUCKETS, 128), F32)
        for p, (_, dil) in enumerate(DIL_PATTERNS):
            bkt = bkt_refs[p][...]
            for h in range(N_HEADS):
                d = d_refs[2 * p][h] + d_refs[2 * p + 1][h]
                for b in _dil_live_buckets(dil):
                    val = jnp.sum(jnp.where(bkt == b, d, 0.0), keepdims=True)
                    out = out + jnp.where((row == b) & (lane == h), val, 0.0)
        o_ref[...] = out

    return pl.pallas_call(
        body, name=name, out_shape=jax.ShapeDtypeStruct((REL_BUCKETS, 128), F32), compiler_params=_cparams(),
    )(*bkts, *dscs)


DPREP_TM = 512


def _dprep_fn(qp, kp, vp, gq, gk, cq, sq, ck, sk, mavg_q, mavg_k, perm_q, perm_k, expand):
    rot = lambda x, perm: jnp.dot(x, perm, precision=HI, preferred_element_type=F32)
    qn = _headnorm(qp, gq, mavg_q)
    kn = _headnorm(kp, gk, mavg_k)
    qr = (qn * cq + rot(qn, perm_q) * sq) * (HEAD_DIM ** -0.5)
    kr = kn * ck + rot(kn, perm_k) * sk
    return qr, rot(kr, expand), rot(vp, expand)


def _dprep_consts():
    cq, sq = _np_rope_tables(N_HEADS)
    ck, sk = _np_rope_tables(KV_WIDTH // HEAD_DIM)
    tables = [jnp.asarray(a) for a in (cq, sq, ck, sk)]
    mats = [jnp.asarray(a) for a in (_np_group_avg(GROUP_WIDTH), _np_group_avg(KV_WIDTH), _np_rope_partner(GROUP_WIDTH),
                                      _np_rope_partner(KV_WIDTH), _np_kv_expand())]
    per = SEQ // DPREP_TM
    w, kw = GROUP_WIDTH, KV_WIDTH
    table_rows = [(tables[0], w, 0, per), (tables[1], w, 0, per), (tables[2], kw, 0, per), (tables[3], kw, 0, per)]
    return table_rows, mats


def _dprep_fwd(z, gq, gk, name):
    table_rows, mats = _dprep_consts()
    w, kw = GROUP_WIDTH, KV_WIDTH

    def fn(qp, kp, vp, cq, sq, ck, sk, gqv, gkv, *m):
        return _dprep_fn(qp, kp, vp, gqv, gkv, cq, sq, ck, sk, *m)

    return _rowmap(fn, [(z, w, 7), (z, kw, 32), (z, kw, 33)] + table_rows, [gq, gk] + mats, [(w, BF16)] * 3, [], name,
                   DPREP_TM, z.shape[0])


def _dprep_bwd(z, dq, dkx, dvx, gq, gk, name):
    table_rows, mats = _dprep_consts()
    fold_q = jnp.asarray(_np_tile_fold(GROUP_WIDTH))
    fold_k = jnp.asarray(_np_tile_fold(KV_WIDTH))
    w, kw = GROUP_WIDTH, KV_WIDTH

    def fn(qp, kp, vp, dqv, dkv, dvv, cq, sq, ck, sk, gqv, gkv, fq, fk, *m):
        f = lambda a, b, c, d, e: _dprep_fn(a, b, c, d, e, cq, sq, ck, sk, *m)
        _, vjp = jax.vjp(f, qp, kp, vp, gqv, gkv)
        dqp, dkp, dvp, dgq, dgk = vjp((dqv, dkv, dvv))
        return dqp, dkp, dvp, _fold_gain(dgq, fq), _fold_gain(dgk, fk)

    return _rowmap(fn, [(z, w, 7), (z, kw, 32), (z, kw, 33), (dq, w, 0), (dkx, w, 0), (dvx, w, 0)] + table_rows,
                   [gq, gk, fold_q, fold_k] + mats, [(w, F32), (kw, F32), (kw, F32)], [(8, HEAD_DIM)] * 2, name,
                   DPREP_TM, z.shape[0])


GQA_QB = 256


def _gqa_fwd(q, kx, vx, name):
    bsz = q.shape[0]
    blk = pl.BlockSpec((None, GQA_QB, GROUP_WIDTH), lambda b, i: (b, i, 0))
    seq = pl.BlockSpec((None, SEQ, GROUP_WIDTH), lambda b, i: (b, 0, 0))

    def body(q_ref, k_ref, v_ref, o_ref):
        for m in range(N_HEADS // 2):
            lanes = pl.ds(m * 128, 128)
            o_ref[:, lanes] = _attn_pair_fwd(q_ref[:, lanes], k_ref[:, lanes], v_ref[:, lanes], None, None)[0]

    return pl.pallas_call(
        body, name=name, grid=(bsz, SEQ // GQA_QB), in_specs=[blk, seq, seq], out_specs=blk,
        out_shape=jax.ShapeDtypeStruct((bsz, SEQ, GROUP_WIDTH), F32), compiler_params=_cparams(("parallel", "parallel")),
    )(q, kx, vx)


def _gqa_bwd(q, kx, vx, do, name):
    bsz = q.shape[0]
    blk = pl.BlockSpec((None, GQA_QB, GROUP_WIDTH), lambda b, i: (b, i, 0))
    seq = pl.BlockSpec((None, SEQ, GROUP_WIDTH), lambda b, i: (b, 0, 0))

    def body(q_ref, k_ref, v_ref, do_ref, dq_ref, dk_ref, dv_ref):
        @pl.when(pl.program_id(1) == 0)
        def _():
            dk_ref[...] = jnp.zeros_like(dk_ref)
            dv_ref[...] = jnp.zeros_like(dv_ref)

        for m in range(N_HEADS // 2):
            lanes = pl.ds(m * 128, 128)
            dq2, dk2, dv2, _ = _attn_pair_bwd(q_ref[:, lanes], k_ref[:, lanes], v_ref[:, lanes], None, None, do_ref[:, lanes], None)
            dq_ref[:, lanes] = dq2
            dk_ref[:, lanes] += dk2
            dv_ref[:, lanes] += dv2

    out = jax.ShapeDtypeStruct((bsz, SEQ, GROUP_WIDTH), F32)
    return pl.pallas_call(
        body, name=name, grid=(bsz, SEQ // GQA_QB), in_specs=[blk, seq, seq, blk], out_specs=[blk, seq, seq],
        out_shape=[out, out, out], compiler_params=_cparams(("parallel", "arbitrary")),
    )(q, kx, vx, do)


CONV_TILE = 64
CONV_LEAD = 16
CONV_WINDOW = CONV_TILE + 32


def _glu(a, g):
    return a * jax.nn.sigmoid(g)


def _conv_post(c, b, ln_g, ln_b):
    x = c + b
    xc = x - jnp.mean(x, axis=-1, keepdims=True)
    y = xc * lax.rsqrt(jnp.mean(xc * xc, axis=-1, keepdims=True) + LN_EPS) * ln_g + ln_b
    return y * jax.nn.sigmoid(y)


def _conv_shifted(win, offset):
    return pltpu.roll(win, CONV_WINDOW - offset, 0)[:CONV_TILE]


def _conv_fill(pad_ref, value_of_tile):
    zeros = jnp.zeros((CONV_LEAD, GROUP_WIDTH), F32)
    pad_ref[pl.ds(0, CONV_LEAD), :] = zeros
    pad_ref[pl.ds(CONV_LEAD + SEQ, CONV_LEAD), :] = zeros

    def step(t, carry):
        r0 = pl.multiple_of(t * CONV_TILE, CONV_TILE)
        pad_ref[pl.ds(CONV_LEAD + r0, CONV_TILE), :] = value_of_tile(r0)
        return carry

    lax.fori_loop(0, SEQ // CONV_TILE, step, 0)


def _conv_tile(pad_ref, w_ref, r0, flip):
    win = pad_ref[pl.ds(r0, CONV_WINDOW), :]
    acc = jnp.zeros((CONV_TILE, GROUP_WIDTH), F32)
    for k in range(CONV_WIDTH):
        offset = (CONV_WIDTH - k) if flip else (k + 1)
        acc = acc + w_ref[pl.ds(k, 1), :] * _conv_shifted(win, offset)
    return acc


def _conv_fwd(z3, w, b, ln_g, ln_b, name):
    bsz = z3.shape[0]
    seq = lambda cb: pl.BlockSpec((None, SEQ, GROUP_WIDTH), functools.partial(lambda i, cb: (i, 0, cb), cb=cb))
    full = lambda a: pl.BlockSpec(a.shape, lambda i: (0,) * a.ndim)

    def body(a_ref, g_ref, w_ref, b_ref, lg_ref, lb_ref, y_ref, pad_ref):
        _conv_fill(pad_ref, lambda r0: _glu(a_ref[pl.ds(r0, CONV_TILE), :], g_ref[pl.ds(r0, CONV_TILE), :]))

        def step(t, carry):
            r0 = pl.multiple_of(t * CONV_TILE, CONV_TILE)
            y_ref[pl.ds(r0, CONV_TILE), :] = _conv_post(_conv_tile(pad_ref, w_ref, r0, False), b_ref[...], lg_ref[...], lb_ref[...])
            return carry

        lax.fori_loop(0, SEQ // CONV_TILE, step, 0)

    return pl.pallas_call(
        body, name=name, grid=(bsz,), in_specs=[seq(5), seq(6), full(w), full(b), full(ln_g), full(ln_b)], out_specs=seq(0),
        out_shape=jax.ShapeDtypeStruct((bsz, SEQ, GROUP_WIDTH), F32),
        scratch_shapes=[pltpu.VMEM((SEQ + 2 * CONV_LEAD, GROUP_WIDTH), F32)], compiler_params=_cparams(("parallel",)),
    )(z3, z3, w, b, ln_g, ln_b)


def _conv_bwd(z3, dy, w, b, ln_g, ln_b, name):
    bsz = z3.shape[0]
    seq = lambda cb: pl.BlockSpec((None, SEQ, GROUP_WIDTH), functools.partial(lambda i, cb: (i, 0, cb), cb=cb))
    full = lambda a: pl.BlockSpec(a.shape, lambda i: (0,) * a.ndim)
    vec = pl.BlockSpec((1, GROUP_WIDTH), lambda i: (0, 0))

    def body(a_ref, g_ref, dy_ref, w_ref, b_ref, lg_ref, lb_ref, da_ref, dg_ref, dw_ref, db_ref, dlg_ref, dlb_ref, hpad, dpad, dw8):
        @pl.when(pl.program_id(0) == 0)
        def _():
            dw8[...] = jnp.zeros_like(dw8)
            db_ref[...] = jnp.zeros_like(db_ref)
            dlg_ref[...] = jnp.zeros_like(dlg_ref)
            dlb_ref[...] = jnp.zeros_like(dlb_ref)

        _conv_fill(hpad, lambda r0: _glu(a_ref[pl.ds(r0, CONV_TILE), :], g_ref[pl.ds(r0, CONV_TILE), :]))
        zeros = jnp.zeros((CONV_LEAD, GROUP_WIDTH), F32)
        dpad[pl.ds(0, CONV_LEAD), :] = zeros
        dpad[pl.ds(CONV_LEAD + SEQ, CONV_LEAD), :] = zeros

        def through_post(t, carry):
            r0 = pl.multiple_of(t * CONV_TILE, CONV_TILE)
            conv = _conv_tile(hpad, w_ref, r0, False)
            _, vjp = jax.vjp(_conv_post, conv, b_ref[...], lg_ref[...], lb_ref[...])
            dconv, db, dlg, dlb = vjp(dy_ref[pl.ds(r0, CONV_TILE), :])
            db_ref[...] += db
            dlg_ref[...] += dlg
            dlb_ref[...] += dlb
            dpad[pl.ds(CONV_LEAD + r0, CONV_TILE), :] = dconv
            win = hpad[pl.ds(r0, CONV_WINDOW), :]
            for k in range(CONV_WIDTH):
                prod = dconv * _conv_shifted(win, k + 1)
                part = prod[0:8]
                for j in range(1, CONV_TILE // 8):
                    part = part + prod[8 * j:8 * j + 8]
                dw8[k] += part
            return carry

        lax.fori_loop(0, SEQ // CONV_TILE, through_post, 0)

        def through_glu(t, carry):
            r0 = pl.multiple_of(t * CONV_TILE, CONV_TILE)
            dh = _conv_tile(dpad, w_ref, r0, True)
            rows = pl.ds(r0, CONV_TILE)
            _, vjp = jax.vjp(_glu, a_ref[rows, :], g_ref[rows, :])
            da, dg = vjp(dh)
            da_ref[rows, :] = da
            dg_ref[rows, :] = dg
            return carry

        lax.fori_loop(0, SEQ // CONV_TILE, through_glu, 0)
        dw_ref[...] = jnp.sum(dw8[...], axis=1)

    out = jax.ShapeDtypeStruct((bsz, SEQ, GROUP_WIDTH), F32)
    v = jax.ShapeDtypeStruct((1, GROUP_WIDTH), F32)
    return pl.pallas_call(
        body, name=name, grid=(bsz,), in_specs=[seq(5), seq(6), seq(0), full(w), full(b), full(ln_g), full(ln_b)],
        out_specs=[seq(0), seq(0), pl.BlockSpec((CONV_WIDTH, GROUP_WIDTH), lambda i: (0, 0)), vec, vec, vec],
        out_shape=[out, out, jax.ShapeDtypeStruct((CONV_WIDTH, GROUP_WIDTH), F32), v, v, v],
        scratch_shapes=[pltpu.VMEM((SEQ + 2 * CONV_LEAD, GROUP_WIDTH), F32), pltpu.VMEM((SEQ + 2 * CONV_LEAD, GROUP_WIDTH), F32),
                        pltpu.VMEM((CONV_WIDTH, 8, GROUP_WIDTH), F32)],
        compiler_params=_cparams(("arbitrary",)),
    )(z3, z3, dy, w, b, ln_g, ln_b)


def _mixnorm_fwd(ys, gains, name):
    w = GROUP_WIDTH

    def fn(*v):
        return (jnp.concatenate([_rms(v[i], v[4 + i]) for i in range(4)], axis=-1),)

    return _rowmap(fn, [(y, w, 0) for y in ys], list(gains), [(4 * w, BF16)], [], name, 512, ys[0].shape[0])[0]


def _mixnorm_bwd(dyn, ys, gains, name, follow=None):
    w = GROUP_WIDTH
    gains = list(gains) if follow is None else [*gains, follow]

    def fn(*v):
        dys, dgs = [], []
        for i in range(4):
            _, vjp = jax.vjp(_rms, v[4 + i], v[8 + i])
            dy, dg = vjp(v[i])
            dys.append(dy)
            dgs.append(dg)
        return (*dys, *dgs)

    rows = [(dyn, w, i) for i in range(4)] + [(y, w, 0) for y in ys]
    return _rowmap(fn, rows, list(gains), [(w, F32)] * 4, [(1, w)] * 4, name, 512, dyn.shape[0])


def _adamw_fn(w, g, m, v):
    m = ADAM_B1 * m + (1.0 - ADAM_B1) * g
    v = ADAM_B2 * v + (1.0 - ADAM_B2) * (g * g)
    m_hat = m / (1.0 - ADAM_B1 ** ADAM_STEP)
    v_hat = v / (1.0 - ADAM_B2 ** ADAM_STEP)
    delta = -ADAM_LR * (m_hat / (jnp.sqrt(v_hat) + ADAM_EPS) + ADAM_WD * w)
    return delta, m, v


def _adamw(w, g, m, v, name):
    r, c = w.shape
    tm = _pick(r, (256, 128, 64, 32, 16, 8))
    return _rowmap(_adamw_fn, [(a, c, 0) for a in (w, g, m, v)], [], [(c, F32)] * 3, [], name, tm, r)


def _layer_params(l, small, big):
    tile_row = lambda g, n: jnp.tile(g, n)[None, :]
    row = lambda g: g[None, :]
    w_s = small["sgu_w"][l].astype(BF16)
    return dict(
        norm1_g=row(small["norm1_g"][l]), norm2_g=row(small["norm2_g"][l]),
        w_s=w_s, w_s_t=jnp.swapaxes(w_s, 1, 2), bm=jnp.repeat(small["sgu_b"][l].T, HEAD_DIM, axis=1),
        gq_dil=tile_row(small["dil_qn_g"][l], N_HEADS), gk_dil=tile_row(small["dil_kn_g"][l], N_HEADS),
        conv_w=small["conv_w"][l], conv_b=row(small["conv_b"][l]), conv_ln_g=row(small["conv_ln_g"][l]),
        conv_ln_b=row(small["conv_ln_b"][l]),
        gq_gqa=tile_row(small["gqa_qn_g"][l], N_HEADS), gk_gqa=tile_row(small["gqa_kn_g"][l], KV_WIDTH // HEAD_DIM),
        mix_g=[row(small["mix_norm_g"][l][i * GROUP_WIDTH:(i + 1) * GROUP_WIDTH]) for i in range(4)],
        big=big,
    )


def _layer_fwd(x, p, table, bsz, tag):
    t = x.shape[0]
    seq3 = lambda a: a.reshape(bsz, SEQ, a.shape[-1])
    flat = lambda a: a.reshape(t, a.shape[-1])
    h1 = _rmsnorm_fwd(x, p["norm1_g"], tag + "rms1")
    z = _matmul([(h1, p["big"]("w_in", h1))], "nn", F32, tag + "mm_z")
    y_a = _sgu_fwd(z, p["w_s"], p["bm"], tag + "sgu_fwd")
    qb, kb, vb = _bprep_fwd(z, p["gq_dil"], p["gk_dil"], tag + "dil_prep")
    outs, lses = [], []
    for _, dil in DIL_PATTERNS:
        o, lse = _dil_fwd(seq3(qb), seq3(kb), seq3(vb), table, dil, f"{tag}dil{dil}_fwd")
        outs.append(flat(o))
        lses.append(flat(lse))
    y_b = _mixture_fwd(outs, lses, tag + "dil_mix")
    y_c = flat(_conv_fwd(seq3(z), p["conv_w"], p["conv_b"], p["conv_ln_g"], p["conv_ln_b"], tag + "conv_fwd"))
    qd, kx, vx = _dprep_fwd(z, p["gq_gqa"], p["gk_gqa"], tag + "gqa_prep")
    y_d = flat(_gqa_fwd(seq3(qd), seq3(kx), seq3(vx), tag + "gqa_fwd"))
    ys = [y_a, y_b, y_c, y_d]
    yn = _mixnorm_fwd(ys, p["mix_g"], tag + "mixnorm")
    x_mid = _matmul([(yn, p["big"]("w_out", yn))], "nn", F32, tag + "mm_out", residual=x)
    h2 = _rmsnorm_fwd(x_mid, p["norm2_g"], tag + "rms2")
    act, gate, up = _ffn_up(h2, p["big"]("w_gate", yn), p["big"]("w_up", yn), tag + "ffn_up")
    x_out = _matmul([(act, p["big"]("w_down", yn))], "nn", F32, tag + "mm_down", residual=x_mid)
    saved = dict(x=x, h1=h1, z=z, qb=qb, kb=kb, vb=vb, outs=outs, lses=lses, qd=qd, kx=kx, vx=vx, ys=ys, yn=yn, x_mid=x_mid,
                 h2=h2, act=act, gate=gate, up=up)
    return x_out, saved


def _layer_bwd(dx_out, s, p, table, bsz, tag, emit, mid_hook):
    t = dx_out.shape[0]
    seq3 = lambda a: a.reshape(bsz, SEQ, a.shape[-1])
    flat = lambda a: a.reshape(t, a.shape[-1])
    z = s["z"]
    small = {}
    weight = lambda name: p["big"](name, None)
    emit("w_down", _matmul([(s["act"], dx_out)], "tn", BF16, tag + "mm_dwdown").reshape(N_CHIPS, FFN_HIDDEN // N_CHIPS, D_MODEL))
    dgate, dup = _ffn_down_bwd(dx_out, weight("w_down"), s["gate"], s["up"], tag + "ffn_dact")
    emit("w_gate", _matmul([(s["h2"], dgate)], "tn", BF16, tag + "mm_dwgate", slabs=N_CHIPS))
    started = emit("w_up", _matmul([(s["h2"], dup)], "tn", BF16, tag + "mm_dwup", slabs=N_CHIPS))
    dh2 = _matmul([(dgate, weight("w_gate")), (dup, weight("w_up"))], "nt", F32, tag + "mm_dh2")
    dx_mid, dg2 = _rmsnorm_bwd(dh2, s["x_mid"], p["norm2_g"], dx_out, tag + "rms2_bwd", follow=started)
    small["norm2_g"] = dg2[0]
    mid_hook(dx_mid)
    dyn = _matmul([(dx_mid, weight("w_out"))], "nt", F32, tag + "mm_dyn")
    started = emit("w_out", _matmul([(s["yn"], dx_mid)], "tn", BF16, tag + "mm_dwout").reshape(N_CHIPS, D_MODEL // N_CHIPS, D_MODEL))
    *dys, dga, dgb, dgc, dgd = _mixnorm_bwd(dyn, s["ys"], p["mix_g"], tag + "mixnorm_bwd", follow=started)
    small["mix_norm_g"] = jnp.concatenate([dga[0], dgb[0], dgc[0], dgd[0]])
    du, dv, dws, dbs = _sgu_bwd(z, dys[0], p["w_s"], p["w_s_t"], p["bm"], tag + "sgu_bwd")
    small["sgu_w"] = dws
    small["sgu_b"] = dbs[:, :N_HEADS].T
    *douts, dl0, dl1, dl2 = _mixture_bwd(s["outs"], s["lses"], dys[1], tag + "dil_mix_bwd")
    dlses = [dl0, dl1, dl2]
    dqs, dks, dvs, dscs = [], [], [], []
    for i, (_, dil) in enumerate(DIL_PATTERNS):
        dq, dk, dvv, dsc = _dil_bwd(seq3(s["qb"]), seq3(s["kb"]), seq3(s["vb"]), seq3(douts[i]), seq3(dlses[i]), table, dil,
                                    f"{tag}dil{dil}_bwd")
        dqs.append(flat(dq))
        dks.append(flat(dk))
        dvs.append(flat(dvv))
        dscs.append(dsc)
    dbq, dbk, dbv, dgq, dgk = _bprep_bwd(z, dqs, dks, dvs, p["gq_dil"], p["gk_dil"], tag + "dil_prep_bwd")
    small["dil_qn_g"], small["dil_kn_g"] = dgq[0], dgk[0]
    dca, dcg, dcw, dcb, dclg, dclb = _conv_bwd(seq3(z), seq3(dys[2]), p["conv_w"], p["conv_b"], p["conv_ln_g"], p["conv_ln_b"],
                                               tag + "conv_bwd")
    small["conv_w"], small["conv_b"], small["conv_ln_g"], small["conv_ln_b"] = dcw, dcb[0], dclg[0], dclb[0]
    dqd, dkx, dvx = _gqa_bwd(seq3(s["qd"]), seq3(s["kx"]), seq3(s["vx"]), seq3(dys[3]), tag + "gqa_bwd")
    ddq, ddk, ddv, dgq, dgk = _dprep_bwd(z, flat(dqd), flat(dkx), flat(dvx), p["gq_gqa"], p["gk_gqa"], tag + "gqa_prep_bwd")
    small["gqa_qn_g"], small["gqa_kn_g"] = dgq[0], dgk[0]
    dz = jnp.concatenate([du, dv, dbq, dbk, dbv, flat(dca), flat(dcg), ddq, ddk, ddv], axis=1)
    dz4 = dz.reshape(t, N_CHIPS, IN_WIDTH // N_CHIPS).transpose(1, 0, 2)
    started = emit("w_in", _matmul([(s["h1"], dz4)], "tn", BF16, tag + "mm_dwin", slabs=N_CHIPS))
    dh1 = _matmul([(dz, weight("w_in"))], "nt", F32, tag + "mm_dh1")
    dx, dg1 = _rmsnorm_bwd(dh1, s["x"], p["norm1_g"], dx_mid, tag + "rms1_bwd", follow=started)
    small["norm1_g"] = dg1[0]
    return dx, small, dscs


def _local_step(x, target, small, big, emit, mid_hook, bsz):
    table = small["rel_bias"]
    params = [_layer_params(l, small, functools.partial(big, l)) for l in range(DEPTH)]
    saved = []
    h = x
    for l in range(DEPTH):
        h, sv = _layer_fwd(h, params[l], table, bsz, f"l{l}_")
        saved.append(sv)
    dh, loss = _loss_fwd_bwd(h, target, "loss")
    small_grads, dscs = [None] * DEPTH, [None] * DEPTH
    for l in reversed(range(DEPTH)):
        dh, small_grads[l], dscs[l] = _layer_bwd(dh, saved[l], params[l], table, bsz, f"l{l}_", functools.partial(emit, l),
                                                 functools.partial(mid_hook, l))
    fold_in = [dscs[l][i] for i in range(len(DIL_PATTERNS)) for l in range(DEPTH)]
    stacked = {k: jnp.stack([small_grads[l][k] for l in range(DEPTH)]) for k in small_grads[0]}
    stacked["rel_bias"] = _relbias_fold(fold_in, "relbias_fold")[:, :N_HEADS]
    return loss, dh, stacked


def _mesh_pos():
    return lax.axis_index("x"), lax.axis_index("y"), lax.axis_index("c")


def _other_chips(x, y):
    return [(1 - x, y), (x, 1 - y), (1 - x, 1 - y)]


_ANY = pl.BlockSpec(memory_space=pl.ANY)


def _swap_other_half(arrs, name):
    n = len(arrs)

    def body(*refs):
        in_refs, out_refs, send_sems, recv_sems = refs[:n], refs[n:2 * n], refs[2 * n], refs[2 * n + 1]
        x, y, c = _mesh_pos()
        copies = []
        for k in range(n):
            h = arrs[k].shape[1] // 2
            copies.append(pltpu.make_async_remote_copy(
                src_ref=in_refs[k].at[:, pl.ds((1 - c) * h, h)], dst_ref=out_refs[k], send_sem=send_sems.at[k],
                recv_sem=recv_sems.at[k], device_id=(x, y, 1 - c), device_id_type=MESH))
        for cp in copies:
            cp.start()
        for cp in copies:
            cp.wait()

    return pl.pallas_call(
        body, name=name, in_specs=[_ANY] * n, out_specs=[_ANY] * n,
        out_shape=[jax.ShapeDtypeStruct((a.shape[0], a.shape[1] // 2, a.shape[2]), a.dtype) for a in arrs],
        scratch_shapes=[pltpu.SemaphoreType.DMA((n,)), pltpu.SemaphoreType.DMA((n,))],
    )(*arrs)


def _gather_finish(own_halves, lands, name):
    n = len(own_halves)

    def body(*refs):
        own_refs, land_refs, out_refs = refs[:n], refs[n:2 * n], refs[2 * n:3 * n]
        send_sems, recv_sems, local_sems = refs[3 * n:]
        x, y, c = _mesh_pos()
        places = [2 * x + y] + [2 * cx + cy for cx, cy in _other_chips(x, y)]
        copies = []
        for k in range(n):
            h = own_halves[k].shape[0]
            for j, place in enumerate(places):
                src = own_refs[k] if j == 0 else land_refs[k].at[j - 1]
                dst = out_refs[k].at[place, pl.ds(c * h, h)]
                cp = pltpu.make_async_copy(src, dst, local_sems.at[4 * k + j])
                cp.start()
                copies.append(cp)
                cp = pltpu.make_async_remote_copy(src_ref=src, dst_ref=dst, send_sem=send_sems.at[4 * k + j],
                                                  recv_sem=recv_sems.at[4 * k + j], device_id=(x, y, 1 - c), device_id_type=MESH)
                cp.start()
                copies.append(cp)
        for cp in copies:
            cp.wait()

    return pl.pallas_call(
        body, name=name, in_specs=[_ANY] * (2 * n), out_specs=[_ANY] * n,
        out_shape=[jax.ShapeDtypeStruct((N_CHIPS, 2 * a.shape[0], a.shape[1]), a.dtype) for a in own_halves],
        scratch_shapes=[pltpu.SemaphoreType.DMA((4 * n,)), pltpu.SemaphoreType.DMA((4 * n,)), pltpu.SemaphoreType.DMA((4 * n,))],
    )(*own_halves, *lands)


def _share_halves(halves, name):
    n = len(halves)

    def body(*refs):
        in_refs, out_refs, send_sems, recv_sems, local_sems = refs[:n], refs[n:2 * n], refs[2 * n], refs[2 * n + 1], refs[2 * n + 2]
        x, y, c = _mesh_pos()
        copies = []
        for k in range(n):
            h = halves[k].shape[-2]
            lead = (slice(None),) * (halves[k].ndim - 2)
            mine = out_refs[k].at[(*lead, pl.ds(c * h, h))]
            cp = pltpu.make_async_copy(in_refs[k], mine, local_sems.at[k])
            cp.start()
            copies.append(cp)
            cp = pltpu.make_async_remote_copy(src_ref=in_refs[k], dst_ref=mine, send_sem=send_sems.at[k], recv_sem=recv_sems.at[k],
                                              device_id=(x, y, 1 - c), device_id_type=MESH)
            cp.start()
            copies.append(cp)
        for cp in copies:
            cp.wait()

    def whole(a):
        return jax.ShapeDtypeStruct((*a.shape[:-2], 2 * a.shape[-2], a.shape[-1]), a.dtype)

    return pl.pallas_call(
        body, name=name, in_specs=[_ANY] * n, out_specs=[_ANY] * n, out_shape=[whole(a) for a in halves],
        scratch_shapes=[pltpu.SemaphoreType.DMA((n,)), pltpu.SemaphoreType.DMA((n,)), pltpu.SemaphoreType.DMA((n,))],
    )(*halves)


_HBM = pl.BlockSpec(memory_space=pltpu.HBM)
_SEM = pl.BlockSpec(memory_space=pltpu.SEMAPHORE)
_DATAFLOW = pltpu.SideEffectType.DATAFLOW_SIDE_EFFECTING


def _chip_copies(src_refs, land_refs, send_sems, recv_sems, scatter):
    x, y, c = _mesh_pos()
    out = []
    for k, (src_ref, land_ref) in enumerate(zip(src_refs, land_refs)):
        for j, (cx, cy) in enumerate(_other_chips(x, y)):
            src = src_ref.at[2 * cx + cy] if scatter else src_ref
            out.append(pltpu.make_async_remote_copy(src_ref=src, dst_ref=land_ref.at[j], send_sem=send_sems.at[3 * k + j],
                                                    recv_sem=recv_sems.at[3 * k + j], device_id=(cx, cy, c), device_id_type=MESH))
    return out


def _chips_start(srcs, scatter, after, name):
    n = len(srcs)
    lands = [lax.empty((N_CHIPS - 1, *s.shape[-2:]), s.dtype) for s in srcs]

    def body(*refs):
        src_refs, land_refs = refs[:n], refs[n:2 * n]
        send_sems, recv_sems, token = refs[2 * n + 1], refs[2 * n + 2], refs[-1]
        for cp in _chip_copies(src_refs, land_refs, send_sems, recv_sems, scatter):
            cp.start()
        token[...] = jnp.zeros_like(token)

    hbm = lambda a: pltpu.HBM(a.shape, a.dtype)
    res = pl.pallas_call(
        body, name=name,
        in_specs=[_HBM] * (2 * n) + [_ANY],
        out_specs=[_SEM, _SEM] + [_HBM] * (2 * n) + [pl.BlockSpec(memory_space=pltpu.VMEM)],
        out_shape=[pltpu.SemaphoreType.DMA((3 * n,)), pltpu.SemaphoreType.DMA((3 * n,))] + [hbm(a) for a in srcs] + [hbm(a) for a in lands]
        + [jax.ShapeDtypeStruct((8, 128), F32)],
        input_output_aliases={i: 2 + i for i in range(2 * n)},
        compiler_params=pltpu.CompilerParams(has_side_effects=_DATAFLOW),
    )(*[pltpu.with_memory_space_constraint(a, pltpu.HBM) for a in (*srcs, *lands)], after)
    return (res[0], res[1], res[2:2 + n], res[2 + n:2 + 2 * n]), res[-1]


def _chips_wait(handle, scatter, after, name):
    send_sems, recv_sems, srcs, lands = handle
    n = len(srcs)

    def body(*refs):
        src_refs, land_refs = refs[:n], refs[n:2 * n]
        send_sems, recv_sems = refs[2 * n], refs[2 * n + 1]
        for cp in _chip_copies(src_refs, land_refs, send_sems, recv_sems, scatter):
            cp.wait_send()
            cp.wait_recv()

    hbm = lambda a: pltpu.HBM(a.shape, a.dtype)
    res = pl.pallas_call(
        body, name=name,
        in_specs=[_HBM] * (2 * n) + [_SEM, _SEM, _ANY], out_specs=[_HBM] * (2 * n),
        out_shape=[hbm(a) for a in srcs] + [hbm(a) for a in lands],
        input_output_aliases={i: i for i in range(2 * n)},
        compiler_params=pltpu.CompilerParams(has_side_effects=_DATAFLOW),
    )(*srcs, *lands, send_sems, recv_sems, after)
    return res[n:]


N_DEV = 8


def _allgather_sum_small(block, name):
    m_per, n = block.shape

    def body(x_ref, out_ref, sum_ref, send_sems, recv_sems, local_sem):
        x, y, c = _mesh_pos()
        me, sibling = (x, y, c), (x, y, 1 - c)
        chips = _other_chips(x, y)

        def rows(px, py, pc):
            return out_ref.at[pl.ds((4 * px + 2 * py + pc) * m_per, m_per), :]

        def copy(k, blk, to, src=None):
            return pltpu.make_async_remote_copy(src_ref=rows(*blk) if src is None else src, dst_ref=rows(*blk),
                                                send_sem=send_sems.at[k], recv_sem=recv_sems.at[k], device_id=to, device_id_type=MESH)

        mine = pltpu.make_async_copy(x_ref, rows(*me), local_sem)
        mine.start()
        first = [copy(0, me, sibling, src=x_ref)]
        first += [copy(1 + j, me, (*chip, c), src=x_ref) for j, chip in enumerate(chips)]
        for cp in first:
            cp.start()
        passed = [copy(4 + j, (*chip, c), sibling) for j, chip in enumerate(chips)]
        for j, chip in enumerate(chips):
            copy(1 + j, (*chip, c), me).wait_recv()
            passed[j].start()
        copy(0, sibling, me).wait_recv()
        for j, chip in enumerate(chips):
            copy(4 + j, (*chip, 1 - c), me).wait_recv()
        for cp in first + passed:
            cp.wait_send()
        mine.wait()
        total = out_ref[pl.ds(0, m_per), :]
        for d in range(1, N_DEV):
            total = total + out_ref[pl.ds(d * m_per, m_per), :]
        sum_ref[...] = total

    vmem = pl.BlockSpec(memory_space=pltpu.VMEM)
    return pl.pallas_call(
        body, name=name, in_specs=[vmem], out_specs=[vmem, vmem],
        out_shape=[jax.ShapeDtypeStruct((N_DEV * m_per, n), F32), jax.ShapeDtypeStruct((m_per, n), F32)],
        scratch_shapes=[pltpu.SemaphoreType.DMA((7,)), pltpu.SemaphoreType.DMA((7,)), pltpu.SemaphoreType.DMA],
        compiler_params=pltpu.CompilerParams(vmem_limit_bytes=V7X_VMEM_LIMIT),
    )(block)


WEIGHTS = ("rel_bias", "norm1_g", "w_in", "sgu_w", "sgu_b", "dil_qn_g", "dil_kn_g", "conv_w", "conv_b", "conv_ln_g", "conv_ln_b",
           "gqa_qn_g", "gqa_kn_g", "mix_norm_g", "w_out", "norm2_g", "w_gate", "w_up", "w_down")
SHARDED = ("w_in", "w_out", "w_gate", "w_up", "w_down")
COLUMN_SHARDED = ("w_in", "w_gate", "w_up")
REPLICATED = tuple(k for k in WEIGHTS if k not in SHARDED and k != "conv_w")


def _pack(parts):
    flat = jnp.concatenate([p.reshape(-1) for p in parts])
    pad = (-flat.shape[0]) % (8 * 128)
    return jnp.pad(flat, (0, pad)).reshape(-1, 128)


def _unpack(buf, shapes):
    flat = buf.reshape(-1)
    out, at = [], 0
    for s in shapes:
        size = math.prod(s)
        out.append(flat[at:at + size].reshape(s))
        at += size
    return out


RS_TM = (256, 128, 64, 32, 16)


def _add_halves(g, sib, c, name):
    slabs, h, cols = sib.shape
    tm = _pick(h, RS_TM)
    nb = h // tm

    def body(c_ref, g_ref, s_ref, o_ref):
        o_ref[...] = (g_ref[...].astype(F32) + s_ref[...].astype(F32)).astype(BF16)

    blk = pl.BlockSpec((None, tm, cols), lambda j, i, c_ref: (j, i, 0))
    grid_spec = pltpu.PrefetchScalarGridSpec(
        num_scalar_prefetch=1, grid=(slabs, nb),
        in_specs=[pl.BlockSpec((None, tm, cols), lambda j, i, c_ref: (j, c_ref[0] * nb + i, 0)), blk], out_specs=blk)
    return pl.pallas_call(body, name=name, grid_spec=grid_spec, out_shape=jax.ShapeDtypeStruct(sib.shape, BF16),
                          compiler_params=_cparams(("parallel", "parallel")))(c, g, sib)


def _add_own_three(own, land, chip, name):
    _, h, cols = own.shape
    tm = _pick(h, RS_TM)

    def body(chip_ref, own_ref, l0_ref, l1_ref, l2_ref, o_ref):
        o_ref[...] = ((own_ref[...].astype(F32) + l0_ref[...].astype(F32)) + l1_ref[...].astype(F32)) + l2_ref[...].astype(F32)

    slot = lambda j: pl.BlockSpec((None, tm, cols), functools.partial(lambda i, chip_ref, j: (j, i, 0), j=j))
    grid_spec = pltpu.PrefetchScalarGridSpec(
        num_scalar_prefetch=1, grid=(h // tm,),
        in_specs=[pl.BlockSpec((None, tm, cols), lambda i, chip_ref: (chip_ref[0], i, 0)), slot(0), slot(1), slot(2)],
        out_specs=pl.BlockSpec((tm, cols), lambda i, chip_ref: (i, 0)))
    return pl.pallas_call(body, name=name, grid_spec=grid_spec, out_shape=jax.ShapeDtypeStruct((h, cols), F32),
                          compiler_params=_cparams(("parallel",)))(chip, own, land, land, land)


def _reduce_start(grads, c1, after, tag):
    from_sibling = _swap_other_half(grads, tag + "pair")
    sums = [_add_halves(g, s, c1, f"{tag}add2_{k}") for k, (g, s) in enumerate(zip(grads, from_sibling))]
    handle, token = _chips_start(sums, True, after, tag + "start")
    return (handle, sums), token


def _reduce_finish(started, chip1, after, tag):
    handle, sums = started
    lands = _chips_wait(handle, True, after, tag + "wait")
    totals = [_add_own_three(s, land, chip1, f"{tag}add4_{k}") for k, (s, land) in enumerate(zip(sums, lands))]
    return _share_halves(totals, tag + "share")


GATHER_GROUPS = (("w_in",), ("w_out", "w_gate", "w_up", "w_down"))
REDUCE_GROUPS = (("w_down", "w_gate", "w_up"), ("w_out",), ("w_in",))


def kernel(x, rel_bias, norm1_g, w_in, sgu_w, sgu_b, dil_qn_g, dil_kn_g, conv_w, conv_b, conv_ln_g, conv_ln_b, gqa_qn_g, gqa_kn_g, mix_norm_g, w_out, norm2_g, w_gate, w_up, w_down, loss_target, m_rel_bias, m_norm1_g, m_w_in, m_sgu_w, m_sgu_b, m_dil_qn_g, m_dil_kn_g, m_conv_w, m_conv_b, m_conv_ln_g, m_conv_ln_b, m_gqa_qn_g, m_gqa_kn_g, m_mix_norm_g, m_w_out, m_norm2_g, m_w_gate, m_w_up, m_w_down, v_rel_bias, v_norm1_g, v_w_in, v_sgu_w, v_sgu_b, v_dil_qn_g, v_dil_kn_g, v_conv_w, v_conv_b, v_conv_ln_g, v_conv_ln_b, v_gqa_qn_g, v_gqa_kn_g, v_mix_norm_g, v_w_out, v_norm2_g, v_w_gate, v_w_up, v_w_down):
    w = dict(rel_bias=rel_bias, norm1_g=norm1_g, w_in=w_in, sgu_w=sgu_w, sgu_b=sgu_b, dil_qn_g=dil_qn_g, dil_kn_g=dil_kn_g,
             conv_w=conv_w, conv_b=conv_b, conv_ln_g=conv_ln_g, conv_ln_b=conv_ln_b, gqa_qn_g=gqa_qn_g, gqa_kn_g=gqa_kn_g,
             mix_norm_g=mix_norm_g, w_out=w_out, norm2_g=norm2_g, w_gate=w_gate, w_up=w_up, w_down=w_down)
    m = dict(rel_bias=m_rel_bias, norm1_g=m_norm1_g, w_in=m_w_in, sgu_w=m_sgu_w, sgu_b=m_sgu_b, dil_qn_g=m_dil_qn_g,
             dil_kn_g=m_dil_kn_g, conv_w=m_conv_w, conv_b=m_conv_b, conv_ln_g=m_conv_ln_g, conv_ln_b=m_conv_ln_b,
             gqa_qn_g=m_gqa_qn_g, gqa_kn_g=m_gqa_kn_g, mix_norm_g=m_mix_norm_g, w_out=m_w_out, norm2_g=m_norm2_g,
             w_gate=m_w_gate, w_up=m_w_up, w_down=m_w_down)
    v = dict(rel_bias=v_rel_bias, norm1_g=v_norm1_g, w_in=v_w_in, sgu_w=v_sgu_w, sgu_b=v_sgu_b, dil_qn_g=v_dil_qn_g,
             dil_kn_g=v_dil_kn_g, conv_w=v_conv_w, conv_b=v_conv_b, conv_ln_g=v_conv_ln_g, conv_ln_b=v_conv_ln_b,
             gqa_qn_g=v_gqa_qn_g, gqa_kn_g=v_gqa_kn_g, mix_norm_g=v_mix_norm_g, w_out=v_w_out, norm2_g=v_norm2_g,
             w_gate=v_w_gate, w_up=v_w_up, w_down=v_w_down)
    bsz = x.shape[0]
    t = bsz * SEQ
    xi, yi, ci = _mesh_pos()
    chip = 2 * xi + yi
    conv_cols = conv_w.shape[-1]

    conv_rows = DEPTH * CONV_WIDTH
    conv_block = jnp.pad(conv_w.reshape(conv_rows, conv_cols), ((0, (-conv_rows) % 8), (0, 0)))
    every, _ = _allgather_sum_small(conv_block, "conv_w_gather")
    every = every.reshape(N_DEV, conv_block.shape[0], conv_cols)
    conv_w_full = jnp.concatenate([every[2 * j, :conv_rows].reshape(DEPTH, CONV_WIDTH, conv_cols) for j in range(N_CHIPS)], axis=-1)

    c1 = jnp.reshape(ci, (1,)).astype(jnp.int32)
    chip1 = jnp.reshape(chip, (1,)).astype(jnp.int32)

    def own_half(k, l):
        a = w[k][l].astype(BF16)
        h = a.shape[0] // 2
        return lax.dynamic_slice_in_dim(a, ci * h, h, axis=0)

    fetches, token = {}, jnp.zeros((8, 128), F32)
    for l in range(DEPTH):
        for gi, group in enumerate(GATHER_GROUPS):
            halves = [own_half(k, l) for k in group]
            handle, token = _chips_start(halves, False, token, f"l{l}_gather{gi}_start")
            fetches[l, gi] = (handle, halves)
    all_started = token
    gathered = {}

    def big(l, name, after):
        if (l, name) not in gathered:
            gi = [name in group for group in GATHER_GROUPS].index(True)
            handle, halves = fetches[l, gi]
            lands = _chips_wait(handle, False, all_started if after is None else after, f"l{l}_gather{gi}_wait")
            for k, g in zip(GATHER_GROUPS[gi], _gather_finish(halves, lands, f"l{l}_gather{gi}_share")):
                rows, cols = g.shape[1:]
                gathered[l, k] = g.transpose(1, 0, 2).reshape(rows, N_CHIPS * cols) if k in COLUMN_SHARDED else g.reshape(N_CHIPS * rows, cols)
        return gathered[l, name]

    big(0, "w_in", None)

    pending, started, reduced = {}, {}, {}

    def emit(l, name, g):
        pending[l, name] = g
        for gi, group in enumerate(REDUCE_GROUPS):
            if name in group and all((l, k) in pending for k in group):
                started[l, gi], token = _reduce_start([pending[l, k] for k in group], c1, g, f"l{l}_reduce{gi}_")
                return token
        return None

    def finish(l, after):
        for gi, group in enumerate(REDUCE_GROUPS):
            for k, r in zip(group, _reduce_finish(started[l, gi], chip1, after, f"l{l}_reduce{gi}_")):
                reduced[l, k] = r
            after = reduced[l, group[0]]

    def mid_hook(l, a):
        if l + 1 < DEPTH:
            finish(l + 1, a)

    small = {k: w[k] for k in REPLICATED}
    small["conv_w"] = conv_w_full
    loss, dx, small_grads = _local_step(x.reshape(t, D_MODEL), loss_target.reshape(t, D_MODEL), small, big, emit, mid_hook, bsz)
    loss = lax.psum(loss[0, 0], ("x", "y", "c"))
    finish(0, dx)

    grads, deltas, new_m, new_v = {}, {}, {}, {}
    for i, k in enumerate(SHARDED):
        g2 = jnp.concatenate([reduced[l, k] for l in range(DEPTH)], axis=0)
        d2, m2, v2 = _adamw(w[k].reshape(g2.shape), g2, m[k].reshape(g2.shape), v[k].reshape(g2.shape), "adamw_" + k)
        grads[k], deltas[k], new_m[k], new_v[k] = (a.reshape(w[k].shape) for a in (g2, d2, m2, v2))

    names = REPLICATED + ("conv_w",)
    shapes = [small_grads[k].shape for k in names]
    _, summed = _allgather_sum_small(_pack([small_grads[k] for k in names]), "small_grads_sum")
    summed_parts = dict(zip(names, _unpack(summed, shapes)))
    rep_shapes = [w[k].shape for k in REPLICATED]
    packed = [_pack([src[k] for k in REPLICATED]) for src in (w, {k: summed_parts[k] for k in REPLICATED}, m, v)]
    d_p, m_p, v_p = _adamw(*packed, "adamw_replicated")
    for k, gk, dk, mk, vk in zip(REPLICATED, _unpack(packed[1], rep_shapes), _unpack(d_p, rep_shapes), _unpack(m_p, rep_shapes),
                                 _unpack(v_p, rep_shapes)):
        grads[k], deltas[k], new_m[k], new_v[k] = gk, dk, mk, vk
    g_conv = lax.dynamic_slice_in_dim(summed_parts["conv_w"], chip * conv_cols, conv_cols, axis=2)
    packed = [_pack([a]) for a in (conv_w, g_conv, m["conv_w"], v["conv_w"])]
    d_p, m_p, v_p = _adamw(*packed, "adamw_conv_w")
    grads["conv_w"] = g_conv
    deltas["conv_w"], new_m["conv_w"], new_v["conv_w"] = (_unpack(a, [conv_w.shape])[0] for a in (d_p, m_p, v_p))

    return (loss, dx.reshape(x.shape), *[grads[k] for k in WEIGHTS], *[deltas[k] for k in WEIGHTS],
            *[new_m[k] for k in WEIGHTS], *[new_v[k] for k in WEIGHTS])
```

```python
import functools
import math

import numpy as np
import jax
import jax.numpy as jnp
from jax import lax
from jax.experimental import pallas as pl
from jax.experimental.pallas import tpu as pltpu

F32 = jnp.float32
BF16 = jnp.bfloat16

D_MODEL = 2048
SEQ = 2048
DEPTH = 2
HEAD_DIM = 64
GROUP_WIDTH = 512
N_HEADS = 8
SGU_CHUNK = 128
DIL_PATTERNS = ((128, 1), (512, 4), (2048, 16))
DIL_HALF = 64
CONV_WIDTH = 31
KV_WIDTH = 128
GRID_W = 64
ROPE_THETA = 10000.0
REL_BUCKETS = 32
REL_MAX_DIST = 1024
FFN_HIDDEN = 5632
IN_WIDTH = 4352
RMS_EPS = 1e-6
LN_EPS = 1e-5
ADAM_LR = 0.001
ADAM_B1 = 0.9
ADAM_B2 = 0.999
ADAM_EPS = 1e-08
ADAM_WD = 0.01
ADAM_STEP = 10
N_CHIPS = 4

V7X_VMEM_LIMIT = 56 * 1024 * 1024
HI = lax.Precision.HIGHEST
MESH = pl.DeviceIdType.MESH


def _cparams(sem=None):
    return pltpu.CompilerParams(dimension_semantics=sem, vmem_limit_bytes=V7X_VMEM_LIMIT)


def _pick(n, cands):
    for c in cands:
        if n % c == 0:
            return c
    raise ValueError(f"no tile for {n}")


_DIMS = {"nn": (((1,), (0,)), ((), ())), "nt": (((1,), (1,)), ((), ())), "tn": (((0,), (0,)), ((), ()))}


def _matmul(pairs, mode, out_dtype, name, residual=None, slabs=1):
    a0, b0 = pairs[0]
    b3 = b0.ndim == 3
    if mode == "nn":
        (M, K), N = a0.shape, b0.shape[1]
    elif mode == "nt":
        (M, K), N = a0.shape, b0.shape[0]
    else:
        (K, M) = a0.shape
        N = b0.shape[-1] if b3 else b0.shape[1] // slabs
    tm = _pick(M, (1024, 512, 256))
    tn = _pick(N, (1024, 2176, 1408, 1088, 512, 256))
    tk = _pick(K, (512, 2176, 256))
    nk = K // tk
    nj = N // tn
    npairs = len(pairs)

    if mode in ("nn", "nt"):
        a_spec = pl.BlockSpec((tm, tk), lambda s, i, j, k: (i, k))
    else:
        a_spec = pl.BlockSpec((tk, tm), lambda s, i, j, k: (k, i))
    if mode == "nt":
        b_spec = pl.BlockSpec((tn, tk), lambda s, i, j, k: (j, k))
    elif b3:
        b_spec = pl.BlockSpec((None, tk, tn), lambda s, i, j, k: (s, k, j))
    else:
        b_spec = pl.BlockSpec((tk, tn), lambda s, i, j, k: (k, s * nj + j))
    if slabs > 1:
        o_spec = pl.BlockSpec((None, tm, tn), lambda s, i, j, k: (s, i, j))
        o_shape = jax.ShapeDtypeStruct((slabs, M, N), out_dtype)
    else:
        o_spec = pl.BlockSpec((tm, tn), lambda s, i, j, k: (i, j))
        o_shape = jax.ShapeDtypeStruct((M, N), out_dtype)
    in_specs = [a_spec] * npairs + [b_spec] * npairs
    args = [a for a, _ in pairs] + [b for _, b in pairs]
    if residual is not None:
        in_specs.append(pl.BlockSpec((tm, tn), lambda s, i, j, k: (i, j)))
        args.append(residual)
    dims = _DIMS[mode]

    def body(*refs):
        a_refs, b_refs = refs[:npairs], refs[npairs:2 * npairs]
        res_ref = refs[2 * npairs] if residual is not None else None
        o_ref, acc_ref = refs[-2], refs[-1]
        k = pl.program_id(3)

        @pl.when(k == 0)
        def _():
            acc_ref[...] = jnp.zeros_like(acc_ref)

        part = None
        for a_ref, b_ref in zip(a_refs, b_refs):
            d = lax.dot_general(a_ref[...].astype(BF16), b_ref[...].astype(BF16), dims, preferred_element_type=F32)
            part = d if part is None else part + d
        acc_ref[...] += part

        @pl.when(k == nk - 1)
        def _():
            r = acc_ref[...]
            if res_ref is not None:
                r = r + res_ref[...]
            o_ref[...] = r.astype(out_dtype)

    return pl.pallas_call(
        body, name=name, grid=(slabs, M // tm, nj, nk), in_specs=in_specs, out_specs=o_spec, out_shape=o_shape,
        scratch_shapes=[pltpu.VMEM((tm, tn), F32)],
        compiler_params=_cparams(("parallel", "parallel", "parallel", "arbitrary")),
    )(*args)


def _rowmap(fn, rows, fulls, row_outs, acc_outs, name, tm, n_rows):
    nr, nf, nro = len(rows), len(fulls), len(row_outs)
    rows = [r if len(r) == 4 else (*r, n_rows // tm) for r in rows]
    in_specs = [pl.BlockSpec((tm, w), functools.partial(lambda i, cb, per: (i % per, cb), cb=cb, per=per)) for _, w, cb, per in rows]
    in_specs += [pl.BlockSpec(f.shape, lambda i: (0,) * f.ndim) for f in fulls]
    out_specs = [pl.BlockSpec((tm, w), lambda i: (i, 0)) for w, _ in row_outs]
    out_specs += [pl.BlockSpec(s, functools.partial(lambda i, n: (0,) * n, n=len(s))) for s in acc_outs]
    out_shape = [jax.ShapeDtypeStruct((n_rows, w), dt) for w, dt in row_outs]
    out_shape += [jax.ShapeDtypeStruct(s, F32) for s in acc_outs]

    def body(*refs):
        ins = [r[...] for r in refs[:nr + nf]]
        outs = fn(*ins)
        o_refs = refs[nr + nf:]
        for o_ref, val in zip(o_refs[:nro], outs[:nro]):
            o_ref[...] = val.astype(o_ref.dtype)
        if acc_outs:
            first = pl.program_id(0) == 0
            for o_ref, val in zip(o_refs[nro:], outs[nro:]):
                @pl.when(first)
                def _(o_ref=o_ref, val=val):
                    o_ref[...] = val

                @pl.when(jnp.logical_not(first))
                def _(o_ref=o_ref, val=val):
                    o_ref[...] += val

    res = pl.pallas_call(
        body, name=name, grid=(n_rows // tm,), in_specs=in_specs, out_specs=out_specs, out_shape=out_shape,
        compiler_params=_cparams(("arbitrary",) if acc_outs else ("parallel",)),
    )(*[r[0] for r in rows], *fulls)
    return res


def _rms(x, g):
    return x * lax.rsqrt(jnp.mean(x * x, axis=-1, keepdims=True) + RMS_EPS) * g


def _rmsnorm_fwd(x, g, name):
    t = x.shape[0]
    return _rowmap(lambda xv, gv: (_rms(xv, gv),), [(x, D_MODEL, 0)], [g], [(D_MODEL, BF16)], [], name, 512, t)[0]


def _rmsnorm_bwd(dh, x, g, dres, name, follow=None):
    t = x.shape[0]

    def fn(dhv, xv, drv, gv, *_):
        _, vjp = jax.vjp(_rms, xv, gv)
        dx, dg = vjp(dhv)
        return dx + drv, dg

    fulls = [g] if follow is None else [g, follow]
    return _rowmap(fn, [(dh, D_MODEL, 0), (x, D_MODEL, 0), (dres, D_MODEL, 0)], fulls, [(D_MODEL, F32)], [(1, D_MODEL)],
                   name, 256, t)


def _loss_fwd_bwd(y, target, name):
    t = y.shape[0]

    def fn(yv, tv):
        e = yv - tv
        return e * (1.0 / D_MODEL), (0.5 / D_MODEL) * jnp.sum(e * e, keepdims=True)

    return _rowmap(fn, [(y, D_MODEL, 0), (target, D_MODEL, 0)], [], [(D_MODEL, F32)], [(1, 1)], name, 512, t)


def _ffn_up(h, wg, wu, name):
    t, n = h.shape[0], wg.shape[1]
    tm, tn = 512, 512

    def body(h_ref, wg_ref, wu_ref, act_ref, g_ref, u_ref):
        hv = h_ref[...]
        g = jnp.dot(hv, wg_ref[...], preferred_element_type=F32)
        u = jnp.dot(hv, wu_ref[...], preferred_element_type=F32)
        act_ref[...] = (g * jax.nn.sigmoid(g) * u).astype(BF16)
        g_ref[...] = g.astype(BF16)
        u_ref[...] = u.astype(BF16)

    o_spec = pl.BlockSpec((tm, tn), lambda i, j: (i, j))
    o_shape = jax.ShapeDtypeStruct((t, n), BF16)
    return pl.pallas_call(
        body, name=name, grid=(t // tm, n // tn),
        in_specs=[pl.BlockSpec((tm, D_MODEL), lambda i, j: (i, 0)), pl.BlockSpec((D_MODEL, tn), lambda i, j: (0, j)),
                  pl.BlockSpec((D_MODEL, tn), lambda i, j: (0, j))],
        out_specs=[o_spec] * 3, out_shape=[o_shape] * 3, compiler_params=_cparams(("parallel", "parallel")),
    )(h, wg, wu)


def _ffn_down_bwd(dy, wd, g, u, name):
    t, n = dy.shape[0], wd.shape[0]
    tm, tn = 512, 512

    def body(dy_ref, wd_ref, g_ref, u_ref, dg_ref, du_ref):
        dact = lax.dot_general(dy_ref[...].astype(BF16), wd_ref[...], _DIMS["nt"], preferred_element_type=F32)
        gv = g_ref[...].astype(F32)
        uv = u_ref[...].astype(F32)
        sg = jax.nn.sigmoid(gv)
        silu = gv * sg
        du_ref[...] = (dact * silu).astype(BF16)
        dg_ref[...] = (dact * uv * (sg + silu * (1.0 - sg))).astype(BF16)

    o_spec = pl.BlockSpec((tm, tn), lambda i, j: (i, j))
    o_shape = jax.ShapeDtypeStruct((t, n), BF16)
    return pl.pallas_call(
        body, name=name, grid=(t // tm, n // tn),
        in_specs=[pl.BlockSpec((tm, D_MODEL), lambda i, j: (i, 0)), pl.BlockSpec((tn, D_MODEL), lambda i, j: (j, 0)),
                  o_spec, o_spec],
        out_specs=[o_spec] * 2, out_shape=[o_shape] * 2, compiler_params=_cparams(("parallel", "parallel")),
    )(dy, wd, g, u)


def _np_group_avg(width, group=HEAD_DIM):
    i = np.arange(width)
    return ((i[:, None] // group) == (i[None, :] // group)).astype(np.float32) / group


def _np_tile_fold(width, group=HEAD_DIM):
    return ((np.arange(width)[:, None] % group) == np.arange(group)[None, :]).astype(np.float32)


def _np_group_fold(width, group=HEAD_DIM, pad=128):
    return ((np.arange(width)[:, None] // group) == np.arange(pad)[None, :]).astype(np.float32)


def _np_rope_partner(width):
    i = np.arange(width)
    partner = np.where(i % 32 < 16, i + 16, i - 16)
    return (partner[:, None] == i[None, :]).astype(np.float32)


def _np_kv_expand():
    src = np.arange(KV_WIDTH)
    dst = np.arange(GROUP_WIDTH)
    return ((src[:, None] // HEAD_DIM == dst[None, :] // (4 * HEAD_DIM)) & (src[:, None] % HEAD_DIM == dst[None, :] % HEAD_DIM)).astype(np.float32)


def _np_rope_tables(n_heads):
    t = np.arange(SEQ)
    pos = {0: (t // GRID_W).astype(np.float32), 1: (t % GRID_W).astype(np.float32)}
    freqs = (ROPE_THETA ** (-np.arange(16, dtype=np.float32) / 16)).astype(np.float32)
    cos_parts, sin_parts = [], []
    for axis in (0, 1):
        ang = pos[axis][:, None] * freqs[None, :]
        c, s = np.cos(ang).astype(np.float32), np.sin(ang).astype(np.float32)
        cos_parts += [c, c]
        sin_parts += [-s, s]
    cos = np.concatenate(cos_parts, axis=1)
    sin = np.concatenate(sin_parts, axis=1)
    return np.tile(cos, (1, n_heads)), np.tile(sin, (1, n_heads))


def _np_t5_buckets(rel):
    nb = REL_BUCKETS // 2
    max_exact = nb // 2
    ret = np.where(rel > 0, nb, 0)
    n = np.abs(rel)
    nf = np.maximum(n, 1).astype(np.float32)
    large = max_exact + (np.log(nf / max_exact) / math.log(REL_MAX_DIST / max_exact) * (nb - max_exact)).astype(np.int32)
    large = np.minimum(large, nb - 1)
    return (ret + np.where(n < max_exact, n, large)).astype(np.int32)


DIL_QB = 128
DIL_WIN = DIL_QB + 2 * DIL_HALF


def _np_dil_buckets(dil):
    off = np.arange(DIL_WIN)[None, :] - DIL_HALF - np.arange(DIL_QB)[:, None]
    return _np_t5_buckets(off * dil)


def _dil_live_buckets(dil):
    off = np.arange(-DIL_HALF, DIL_HALF + 1)
    return sorted(set(_np_t5_buckets(off * dil).tolist()))


def _head_stat(x, mavg):
    return jnp.dot(x, mavg, precision=HI, preferred_element_type=F32)


def _gelu(x):
    return 0.5 * x * (1.0 + jnp.tanh(math.sqrt(2.0 / math.pi) * (x + 0.044715 * (x * x * x))))


def _sgu_pre(u_pre, v_pre, mavg):
    v = _gelu(v_pre)
    xc = v - _head_stat(v, mavg)
    vn = xc * lax.rsqrt(_head_stat(xc * xc, mavg) + LN_EPS)
    return _gelu(u_pre), vn


def _sgu_mix(w_ref, vnb, bm):
    lane_group = lax.broadcasted_iota(jnp.int32, (1, GROUP_WIDTH), 1) // HEAD_DIM
    mixed = bm
    for g in range(N_HEADS):
        r = jnp.dot(w_ref[g], vnb, preferred_element_type=F32)
        mixed = mixed + jnp.where(lane_group == g, r, 0.0)
    return mixed


SGU_TM = 512


def _sgu_fwd(z, w_s, bm, name):
    t = z.shape[0]
    mavg = jnp.asarray(_np_group_avg(GROUP_WIDTH))

    def body(u_ref, v_ref, w_ref, bm_ref, mavg_ref, y_ref):
        for c in range(SGU_TM // SGU_CHUNK):
            rows = pl.ds(c * SGU_CHUNK, SGU_CHUNK)
            u, vn = _sgu_pre(u_ref[rows, :], v_ref[rows, :], mavg_ref[...])
            y_ref[rows, :] = u * _sgu_mix(w_ref, vn.astype(BF16), bm_ref[...])

    full = lambda a: pl.BlockSpec(a.shape, lambda i: (0,) * a.ndim)
    return pl.pallas_call(
        body, name=name, grid=(t // SGU_TM,),
        in_specs=[pl.BlockSpec((SGU_TM, GROUP_WIDTH), lambda i: (i, 0)), pl.BlockSpec((SGU_TM, GROUP_WIDTH), lambda i: (i, 1)),
                  full(w_s), full(bm), full(mavg)],
        out_specs=pl.BlockSpec((SGU_TM, GROUP_WIDTH), lambda i: (i, 0)),
        out_shape=jax.ShapeDtypeStruct((t, GROUP_WIDTH), F32), compiler_params=_cparams(("parallel",)),
    )(z, z, w_s, bm, mavg)


def _sgu_bwd(z, dy, w_s, w_s_t, bm, name):
    t = z.shape[0]
    mavg = jnp.asarray(_np_group_avg(GROUP_WIDTH))
    gfold = jnp.asarray(_np_group_fold(GROUP_WIDTH))

    def body(u_ref, v_ref, dy_ref, w_ref, wt_ref, bm_ref, mavg_ref, gfold_ref, du_ref, dv_ref, dw_ref, dbs_ref, dbm_ref):
        @pl.when(pl.program_id(0) == 0)
        def _():
            dw_ref[...] = jnp.zeros_like(dw_ref)
            dbm_ref[...] = jnp.zeros_like(dbm_ref)

        lane_group = lax.broadcasted_iota(jnp.int32, (1, GROUP_WIDTH), 1) // HEAD_DIM
        for c in range(SGU_TM // SGU_CHUNK):
            rows = pl.ds(c * SGU_CHUNK, SGU_CHUNK)
            (u, vn), pre_vjp = jax.vjp(functools.partial(_sgu_pre, mavg=mavg_ref[...]), u_ref[rows, :], v_ref[rows, :])
            vnb = vn.astype(BF16)
            mixed = _sgu_mix(w_ref, vnb, bm_ref[...])
            dyv = dy_ref[rows, :]
            dmixed = dyv * u
            dbm_ref[...] += dmixed
            dvn = jnp.zeros_like(vn)
            for g in range(N_HEADS):
                dm_g = jnp.where(lane_group == g, dmixed, 0.0).astype(BF16)
                dw_ref[g] += lax.dot_general(dm_g, vnb, _DIMS["nt"], preferred_element_type=F32)
                dvn = dvn + jnp.dot(wt_ref[g], dm_g, preferred_element_type=F32)
            du_pre, dv_pre = pre_vjp((dyv * mixed, dvn))
            du_ref[rows, :] = du_pre
            dv_ref[rows, :] = dv_pre

        @pl.when(pl.program_id(0) == t // SGU_TM - 1)
        def _():
            dbs_ref[...] = jnp.dot(dbm_ref[...], gfold_ref[...], precision=HI, preferred_element_type=F32)

    full = lambda a: pl.BlockSpec(a.shape, lambda i: (0,) * a.ndim)
    row = pl.BlockSpec((SGU_TM, GROUP_WIDTH), lambda i: (i, 0))
    return pl.pallas_call(
        body, name=name, grid=(t // SGU_TM,),
        in_specs=[row, pl.BlockSpec((SGU_TM, GROUP_WIDTH), lambda i: (i, 1)), row, full(w_s), full(w_s_t), full(bm), full(mavg),
                  full(gfold)],
        out_specs=[row, row, pl.BlockSpec((N_HEADS, SGU_CHUNK, SGU_CHUNK), lambda i: (0, 0, 0)),
                   pl.BlockSpec((SGU_CHUNK, 128), lambda i: (0, 0))],
        out_shape=[jax.ShapeDtypeStruct((t, GROUP_WIDTH), F32)] * 2 + [jax.ShapeDtypeStruct((N_HEADS, SGU_CHUNK, SGU_CHUNK), F32),
                                                                      jax.ShapeDtypeStruct((SGU_CHUNK, 128), F32)],
        scratch_shapes=[pltpu.VMEM((SGU_CHUNK, GROUP_WIDTH), F32)],
        compiler_params=_cparams(("arbitrary",)),
    )(z, z, dy, w_s, w_s_t, bm, mavg, gfold)


def _pair_softmax(q2, k2, hh, bias, valid):
    head = (lax.broadcasted_iota(jnp.int32, (1, 2 * HEAD_DIM), 1) // HEAD_DIM) == hh
    qm = jnp.where(head, q2, jnp.zeros_like(q2))
    s = lax.dot_general(qm, k2, _DIMS["nt"], preferred_element_type=F32)
    if bias is not None:
        s = s + bias
    if valid is not None:
        s = jnp.where(valid, s, -1e30)
    m = jnp.max(s, axis=-1, keepdims=True)
    e = jnp.exp(s - m)
    l = jnp.sum(e, axis=-1, keepdims=True)
    return head, qm, e / l, m + jnp.log(l)


def _attn_pair_fwd(q2, k2, v2, biases, valid):
    o2 = lse2 = None
    for hh in range(2):
        head, _, p, lse = _pair_softmax(q2, k2, hh, None if biases is None else biases[hh], valid)
        oh = jnp.dot(p.astype(BF16), v2, preferred_element_type=F32)
        o_h = jnp.where(head, oh, 0.0)
        l_h = jnp.where(head, lse, 0.0)
        o2 = o_h if o2 is None else o2 + o_h
        lse2 = l_h if lse2 is None else lse2 + l_h
    return o2, lse2


def _attn_pair_bwd(q2, k2, v2, biases, valid, do2, dlse2):
    dq2 = dk2 = dv2 = None
    ds_heads = []
    for hh in range(2):
        head, qm, p, _ = _pair_softmax(q2, k2, hh, None if biases is None else biases[hh], valid)
        dom = jnp.where(head, do2, 0.0).astype(BF16)
        dp = lax.dot_general(dom, v2, _DIMS["nt"], preferred_element_type=F32)
        delta = jnp.sum(dp * p, axis=-1, keepdims=True)
        if dlse2 is not None:
            delta = delta - jnp.sum(jnp.where(head, dlse2, 0.0), axis=-1, keepdims=True)
        ds = p * (dp - delta)
        dsb = ds.astype(BF16)
        dq_h = jnp.where(head, jnp.dot(dsb, k2, preferred_element_type=F32), 0.0)
        dk_h = lax.dot_general(dsb, qm, _DIMS["tn"], preferred_element_type=F32)
        dv_h = lax.dot_general(p.astype(BF16), dom, _DIMS["tn"], preferred_element_type=F32)
        dq2 = dq_h if dq2 is None else dq2 + dq_h
        dk2 = dk_h if dk2 is None else dk2 + dk_h
        dv2 = dv_h if dv2 is None else dv2 + dv_h
        ds_heads.append(ds)
    return dq2, dk2, dv2, ds_heads


def _dil_valid(r0, length):
    row = lax.broadcasted_iota(jnp.int32, (DIL_QB, DIL_WIN), 0)
    col = lax.broadcasted_iota(jnp.int32, (DIL_QB, DIL_WIN), 1)
    off = col - DIL_HALF - row
    kpos = r0 - DIL_HALF + col
    return (jnp.abs(off) <= DIL_HALF) & (kpos >= 0) & (kpos < length)


def _dil_build_bias(tab_ref, bkt_ref, bias_ref, dil):
    bkt = bkt_ref[...]
    for h in range(N_HEADS):
        acc = jnp.zeros((DIL_QB, DIL_WIN), F32)
        for b in _dil_live_buckets(dil):
            acc = jnp.where(bkt == b, tab_ref[b, h], acc)
        bias_ref[h] = acc


def _dil_fill_pad(pad_ref, src_ref, length):
    zeros = jnp.zeros((DIL_HALF, GROUP_WIDTH), pad_ref.dtype)
    pad_ref[pl.ds(0, DIL_HALF), :] = zeros
    pad_ref[pl.ds(DIL_HALF + length, DIL_HALF), :] = zeros
    pad_ref[pl.ds(DIL_HALF, length), :] = src_ref[...]


def _dil_specs(bsz, length, dil):
    view = lambda a: a.reshape(bsz, length, dil * GROUP_WIDTH)
    blk = pl.BlockSpec((None, DIL_QB, GROUP_WIDTH), lambda b, rho, i: (b, i, rho))
    seq = pl.BlockSpec((None, length, GROUP_WIDTH), lambda b, rho, i: (b, 0, rho))
    return view, blk, seq


def _dil_fwd(qb, kb, vb, table, dil, name):
    bsz = qb.shape[0]
    length = SEQ // dil
    bkt = jnp.asarray(_np_dil_buckets(dil))
    view, blk, seq = _dil_specs(bsz, length, dil)

    def body(tab_ref, bkt_ref, q_ref, k_ref, v_ref, o_ref, lse_ref, kpad, vpad, bias_ref):
        i = pl.program_id(2)

        @pl.when((pl.program_id(0) == 0) & (pl.program_id(1) == 0) & (i == 0))
        def _():
            _dil_build_bias(tab_ref, bkt_ref, bias_ref, dil)

        @pl.when(i == 0)
        def _():
            _dil_fill_pad(kpad, k_ref, length)
            _dil_fill_pad(vpad, v_ref, length)

        r0 = pl.multiple_of(i * DIL_QB, DIL_QB)
        valid = _dil_valid(r0, length)
        for m in range(N_HEADS // 2):
            lanes = pl.ds(m * 128, 128)
            o2, lse2 = _attn_pair_fwd(q_ref[:, lanes], kpad[pl.ds(r0, DIL_WIN), lanes], vpad[pl.ds(r0, DIL_WIN), lanes],
                                      (bias_ref[2 * m], bias_ref[2 * m + 1]), valid)
            o_ref[:, lanes] = o2
            lse_ref[:, lanes] = lse2

    out = jax.ShapeDtypeStruct((bsz, length, dil * GROUP_WIDTH), F32)
    o, lse = pl.pallas_call(
        body, name=name, grid=(bsz, dil, length // DIL_QB),
        in_specs=[pl.BlockSpec(memory_space=pltpu.SMEM), pl.BlockSpec(bkt.shape, lambda b, rho, i: (0, 0)), blk, seq, seq],
        out_specs=[blk, blk], out_shape=[out, out],
        scratch_shapes=[pltpu.VMEM((length + 2 * DIL_HALF, GROUP_WIDTH), BF16), pltpu.VMEM((length + 2 * DIL_HALF, GROUP_WIDTH), BF16),
                        pltpu.VMEM((N_HEADS, DIL_QB, DIL_WIN), F32)],
        compiler_params=_cparams(("arbitrary", "arbitrary", "arbitrary")),
    )(table, bkt, view(qb), view(kb), view(vb))
    return o.reshape(bsz, SEQ, GROUP_WIDTH), lse.reshape(bsz, SEQ, GROUP_WIDTH)


def _dil_bwd(qb, kb, vb, do, dlse, table, dil, name):
    bsz = qb.shape[0]
    length = SEQ // dil
    nqb = length // DIL_QB
    bkt = jnp.asarray(_np_dil_buckets(dil))
    view, blk, seq = _dil_specs(bsz, length, dil)

    def body(tab_ref, bkt_ref, q_ref, k_ref, v_ref, do_ref, dlse_ref, dq_ref, dk_ref, dv_ref, dsc_ref, kpad, vpad, bias_ref):
        i = pl.program_id(2)

        @pl.when((pl.program_id(0) == 0) & (pl.program_id(1) == 0) & (i == 0))
        def _():
            _dil_build_bias(tab_ref, bkt_ref, bias_ref, dil)
            dsc_ref[...] = jnp.zeros_like(dsc_ref)

        @pl.when(i == 0)
        def _():
            _dil_fill_pad(kpad, k_ref, length)
            _dil_fill_pad(vpad, v_ref, length)
            dk_ref[...] = jnp.zeros_like(dk_ref)
            dv_ref[...] = jnp.zeros_like(dv_ref)

        r0 = pl.multiple_of(i * DIL_QB, DIL_QB)
        valid = _dil_valid(r0, length)
        for m in range(N_HEADS // 2):
            lanes = pl.ds(m * 128, 128)
            dq2, dk2, dv2, ds_heads = _attn_pair_bwd(
                q_ref[:, lanes], kpad[pl.ds(r0, DIL_WIN), lanes], vpad[pl.ds(r0, DIL_WIN), lanes],
                (bias_ref[2 * m], bias_ref[2 * m + 1]), valid, do_ref[:, lanes], dlse_ref[:, lanes])
            dq_ref[:, lanes] = dq2
            dsc_ref[2 * m] += ds_heads[0]
            dsc_ref[2 * m + 1] += ds_heads[1]
            for first, size, live in ((0, DIL_HALF, i >= 1), (DIL_HALF, DIL_QB, None), (DIL_HALF + DIL_QB, DIL_HALF, i <= nqb - 2)):
                def add(first=first, size=size, dk2=dk2, dv2=dv2, lanes=lanes):
                    rows = pl.ds(pl.multiple_of(r0 - DIL_HALF + first, DIL_HALF), size)
                    dk_ref[rows, lanes] += dk2[first:first + size]
                    dv_ref[rows, lanes] += dv2[first:first + size]
                if live is None:
                    add()
                else:
                    pl.when(live)(add)

    out = jax.ShapeDtypeStruct((bsz, length, dil * GROUP_WIDTH), F32)
    dsc_shape = (N_HEADS, DIL_QB, DIL_WIN)
    dq, dk, dv, dsc = pl.pallas_call(
        body, name=name, grid=(bsz, dil, nqb),
        in_specs=[pl.BlockSpec(memory_space=pltpu.SMEM), pl.BlockSpec(bkt.shape, lambda b, rho, i: (0, 0)), blk, seq, seq, blk, blk],
        out_specs=[blk, seq, seq, pl.BlockSpec(dsc_shape, lambda b, rho, i: (0, 0, 0))],
        out_shape=[out, out, out, jax.ShapeDtypeStruct(dsc_shape, F32)],
        scratch_shapes=[pltpu.VMEM((length + 2 * DIL_HALF, GROUP_WIDTH), BF16), pltpu.VMEM((length + 2 * DIL_HALF, GROUP_WIDTH), BF16),
                        pltpu.VMEM(dsc_shape, F32)],
        compiler_params=_cparams(("arbitrary", "arbitrary", "arbitrary")),
    )(table, bkt, view(qb), view(kb), view(vb), view(do), view(dlse))
    shp = (bsz, SEQ, GROUP_WIDTH)
    return dq.reshape(shp), dk.reshape(shp), dv.reshape(shp), dsc


def _headnorm(x, g, mavg):
    return x * lax.rsqrt(_head_stat(x * x, mavg) + RMS_EPS) * g


def _fold_gain(dg_full, fold):
    return jnp.dot(jnp.broadcast_to(dg_full, (8, dg_full.shape[1])), fold, precision=HI, preferred_element_type=F32)


def _bprep_fn(qp, kp, gq, gk, mavg):
    return _headnorm(qp, gq, mavg) * (HEAD_DIM ** -0.5), _headnorm(kp, gk, mavg)


def _bprep_fwd(z, gq, gk, name):
    mavg = jnp.asarray(_np_group_avg(GROUP_WIDTH))

    def fn(qp, kp, vp, gqv, gkv, mv):
        qb, kb = _bprep_fn(qp, kp, gqv, gkv, mv)
        return qb, kb, vp

    w = GROUP_WIDTH
    return _rowmap(fn, [(z, w, 2), (z, w, 3), (z, w, 4)], [gq, gk, mavg], [(w, BF16)] * 3, [], name, 512, z.shape[0])


def _bprep_bwd(z, dqs, dks, dvs, gq, gk, name):
    mavg = jnp.asarray(_np_group_avg(GROUP_WIDTH))
    fold = jnp.asarray(_np_tile_fold(GROUP_WIDTH))

    def fn(qp, kp, dq0, dq1, dq2, dk0, dk1, dk2, dv0, dv1, dv2, gqv, gkv, mv, fv):
        _, vjp = jax.vjp(functools.partial(_bprep_fn, mavg=mv), qp, kp, gqv, gkv)
        dqp, dkp, dgq, dgk = vjp((dq0 + dq1 + dq2, dk0 + dk1 + dk2))
        return dqp, dkp, dv0 + dv1 + dv2, _fold_gain(dgq, fv), _fold_gain(dgk, fv)

    w = GROUP_WIDTH
    rows = [(z, w, 2), (z, w, 3)] + [(a, w, 0) for a in (*dqs, *dks, *dvs)]
    return _rowmap(fn, rows, [gq, gk, mavg, fold], [(w, F32)] * 3, [(8, HEAD_DIM)] * 2, name, 256, z.shape[0])


def _mixture_fn(o0, o1, o2, l0, l1, l2):
    m = lax.stop_gradient(jnp.maximum(jnp.maximum(l0, l1), l2))
    e0, e1, e2 = jnp.exp(l0 - m), jnp.exp(l1 - m), jnp.exp(l2 - m)
    return (e0 * o0 + e1 * o1 + e2 * o2) / (e0 + e1 + e2)


def _mixture_fwd(os_, ls_, name):
    w = GROUP_WIDTH
    rows = [(a, w, 0) for a in (*os_, *ls_)]
    return _rowmap(lambda *v: (_mixture_fn(*v),), rows, [], [(w, F32)], [], name, 512, os_[0].shape[0])[0]


def _mixture_bwd(os_, ls_, dy, name):
    w = GROUP_WIDTH

    def fn(*v):
        _, vjp = jax.vjp(_mixture_fn, *v[:6])
        return vjp(v[6])

    rows = [(a, w, 0) for a in (*os_, *ls_, dy)]
    return _rowmap(fn, rows, [], [(w, F32)] * 6, [], name, 512, dy.shape[0])


def _relbias_fold(dscs, name):
    bkts = [jnp.asarray(_np_dil_buckets(dil)) for _, dil in DIL_PATTERNS]
    npat = len(DIL_PATTERNS)

    def body(*refs):
        bkt_refs, d_refs, o_ref = refs[:npat], refs[npat:-1], refs[-1]
        row = lax.broadcasted_iota(jnp.int32, (REL_BUCKETS, 128), 0)
        lane = lax.broadcasted_iota(jnp.int32, (REL_BUCKETS, 128), 1)
        out = jnp.zeros((REL_BUCKETS, 128), F32)
        for p, (_, dil) in enumerate(DIL_PATTERNS):
            bkt = bkt_refs[p][...]
            for h in range(N_HEADS):
                d = d_refs[2 * p][h] + d_refs[2 * p + 1][h]
                for b in _dil_live_buckets(dil):
                    val = jnp.sum(jnp.where(bkt == b, d, 0.0), keepdims=True)
                    out = out + jnp.where((row == b) & (lane == h), val, 0.0)
        o_ref[...] = out

    return pl.pallas_call(
        body, name=name, out_shape=jax.ShapeDtypeStruct((REL_BUCKETS, 128), F32), compiler_params=_cparams(),
    )(*bkts, *dscs)


DPREP_TM = 512


def _dprep_fn(qp, kp, vp, gq, gk, cq, sq, ck, sk, mavg_q, mavg_k, perm_q, perm_k, expand):
    rot = lambda x, perm: jnp.dot(x, perm, precision=HI, preferred_element_type=F32)
    qn = _headnorm(qp, gq, mavg_q)
    kn = _headnorm(kp, gk, mavg_k)
    qr = (qn * cq + rot(qn, perm_q) * sq) * (HEAD_DIM ** -0.5)
    kr = kn * ck + rot(kn, perm_k) * sk
    return qr, rot(kr, expand), rot(vp, expand)


def _dprep_consts():
    cq, sq = _np_rope_tables(N_HEADS)
    ck, sk = _np_rope_tables(KV_WIDTH // HEAD_DIM)
    tables = [jnp.asarray(a) for a in (cq, sq, ck, sk)]
    mats = [jnp.asarray(a) for a in (_np_group_avg(GROUP_WIDTH), _np_group_avg(KV_WIDTH), _np_rope_partner(GROUP_WIDTH),
                                      _np_rope_partner(KV_WIDTH), _np_kv_expand())]
    per = SEQ // DPREP_TM
    w, kw = GROUP_WIDTH, KV_WIDTH
    table_rows = [(tables[0], w, 0, per), (tables[1], w, 0, per), (tables[2], kw, 0, per), (tables[3], kw, 0, per)]
    return table_rows, mats


def _dprep_fwd(z, gq, gk, name):
    table_rows, mats = _dprep_consts()
    w, kw = GROUP_WIDTH, KV_WIDTH

    def fn(qp, kp, vp, cq, sq, ck, sk, gqv, gkv, *m):
        return _dprep_fn(qp, kp, vp, gqv, gkv, cq, sq, ck, sk, *m)

    return _rowmap(fn, [(z, w, 7), (z, kw, 32), (z, kw, 33)] + table_rows, [gq, gk] + mats, [(w, BF16)] * 3, [], name,
                   DPREP_TM, z.shape[0])


def _dprep_bwd(z, dq, dkx, dvx, gq, gk, name):
    table_rows, mats = _dprep_consts()
    fold_q = jnp.asarray(_np_tile_fold(GROUP_WIDTH))
    fold_k = jnp.asarray(_np_tile_fold(KV_WIDTH))
    w, kw = GROUP_WIDTH, KV_WIDTH

    def fn(qp, kp, vp, dqv, dkv, dvv, cq, sq, ck, sk, gqv, gkv, fq, fk, *m):
        f = lambda a, b, c, d, e: _dprep_fn(a, b, c, d, e, cq, sq, ck, sk, *m)
        _, vjp = jax.vjp(f, qp, kp, vp, gqv, gkv)
        dqp, dkp, dvp, dgq, dgk = vjp((dqv, dkv, dvv))
        return dqp, dkp, dvp, _fold_gain(dgq, fq), _fold_gain(dgk, fk)

    return _rowmap(fn, [(z, w, 7), (z, kw, 32), (z, kw, 33), (dq, w, 0), (dkx, w, 0), (dvx, w, 0)] + table_rows,
                   [gq, gk, fold_q, fold_k] + mats, [(w, F32), (kw, F32), (kw, F32)], [(8, HEAD_DIM)] * 2, name,
                   DPREP_TM, z.shape[0])


GQA_QB = 256


def _gqa_fwd(q, kx, vx, name):
    bsz = q.shape[0]
    blk = pl.BlockSpec((None, GQA_QB, GROUP_WIDTH), lambda b, i: (b, i, 0))
    seq = pl.BlockSpec((None, SEQ, GROUP_WIDTH), lambda b, i: (b, 0, 0))

    def body(q_ref, k_ref, v_ref, o_ref):
        for m in range(N_HEADS // 2):
            lanes = pl.ds(m * 128, 128)
            o_ref[:, lanes] = _attn_pair_fwd(q_ref[:, lanes], k_ref[:, lanes], v_ref[:, lanes], None, None)[0]

    return pl.pallas_call(
        body, name=name, grid=(bsz, SEQ // GQA_QB), in_specs=[blk, seq, seq], out_specs=blk,
        out_shape=jax.ShapeDtypeStruct((bsz, SEQ, GROUP_WIDTH), F32), compiler_params=_cparams(("parallel", "parallel")),
    )(q, kx, vx)


def _gqa_bwd(q, kx, vx, do, name):
    bsz = q.shape[0]
    blk = pl.BlockSpec((None, GQA_QB, GROUP_WIDTH), lambda b, i: (b, i, 0))
    seq = pl.BlockSpec((None, SEQ, GROUP_WIDTH), lambda b, i: (b, 0, 0))

    def body(q_ref, k_ref, v_ref, do_ref, dq_ref, dk_ref, dv_ref):
        @pl.when(pl.program_id(1) == 0)
        def _():
            dk_ref[...] = jnp.zeros_like(dk_ref)
            dv_ref[...] = jnp.zeros_like(dv_ref)

        for m in range(N_HEADS // 2):
            lanes = pl.ds(m * 128, 128)
            dq2, dk2, dv2, _ = _attn_pair_bwd(q_ref[:, lanes], k_ref[:, lanes], v_ref[:, lanes], None, None, do_ref[:, lanes], None)
            dq_ref[:, lanes] = dq2
            dk_ref[:, lanes] += dk2
            dv_ref[:, lanes] += dv2

    out = jax.ShapeDtypeStruct((bsz, SEQ, GROUP_WIDTH), F32)
    return pl.pallas_call(
        body, name=name, grid=(bsz, SEQ // GQA_QB), in_specs=[blk, seq, seq, blk], out_specs=[blk, seq, seq],
        out_shape=[out, out, out], compiler_params=_cparams(("parallel", "arbitrary")),
    )(q, kx, vx, do)


CONV_TILE = 64
CONV_LEAD = 16
CONV_WINDOW = CONV_TILE + 32


def _glu(a, g):
    return a * jax.nn.sigmoid(g)


def _conv_post(c, b, ln_g, ln_b):
    x = c + b
    xc = x - jnp.mean(x, axis=-1, keepdims=True)
    y = xc * lax.rsqrt(jnp.mean(xc * xc, axis=-1, keepdims=True) + LN_EPS) * ln_g + ln_b
    return y * jax.nn.sigmoid(y)


def _conv_shifted(win, offset):
    return pltpu.roll(win, CONV_WINDOW - offset, 0)[:CONV_TILE]


def _conv_fill(pad_ref, value_of_tile):
    zeros = jnp.zeros((CONV_LEAD, GROUP_WIDTH), F32)
    pad_ref[pl.ds(0, CONV_LEAD), :] = zeros
    pad_ref[pl.ds(CONV_LEAD + SEQ, CONV_LEAD), :] = zeros

    def step(t, carry):
        r0 = pl.multiple_of(t * CONV_TILE, CONV_TILE)
        pad_ref[pl.ds(CONV_LEAD + r0, CONV_TILE), :] = value_of_tile(r0)
        return carry

    lax.fori_loop(0, SEQ // CONV_TILE, step, 0)


def _conv_tile(pad_ref, w_ref, r0, flip):
    win = pad_ref[pl.ds(r0, CONV_WINDOW), :]
    acc = jnp.zeros((CONV_TILE, GROUP_WIDTH), F32)
    for k in range(CONV_WIDTH):
        offset = (CONV_WIDTH - k) if flip else (k + 1)
        acc = acc + w_ref[pl.ds(k, 1), :] * _conv_shifted(win, offset)
    return acc


def _conv_fwd(z3, w, b, ln_g, ln_b, name):
    bsz = z3.shape[0]
    seq = lambda cb: pl.BlockSpec((None, SEQ, GROUP_WIDTH), functools.partial(lambda i, cb: (i, 0, cb), cb=cb))
    full = lambda a: pl.BlockSpec(a.shape, lambda i: (0,) * a.ndim)

    def body(a_ref, g_ref, w_ref, b_ref, lg_ref, lb_ref, y_ref, pad_ref):
        _conv_fill(pad_ref, lambda r0: _glu(a_ref[pl.ds(r0, CONV_TILE), :], g_ref[pl.ds(r0, CONV_TILE), :]))

        def step(t, carry):
            r0 = pl.multiple_of(t * CONV_TILE, CONV_TILE)
            y_ref[pl.ds(r0, CONV_TILE), :] = _conv_post(_conv_tile(pad_ref, w_ref, r0, False), b_ref[...], lg_ref[...], lb_ref[...])
            return carry

        lax.fori_loop(0, SEQ // CONV_TILE, step, 0)

    return pl.pallas_call(
        body, name=name, grid=(bsz,), in_specs=[seq(5), seq(6), full(w), full(b), full(ln_g), full(ln_b)], out_specs=seq(0),
        out_shape=jax.ShapeDtypeStruct((bsz, SEQ, GROUP_WIDTH), F32),
        scratch_shapes=[pltpu.VMEM((SEQ + 2 * CONV_LEAD, GROUP_WIDTH), F32)], compiler_params=_cparams(("parallel",)),
    )(z3, z3, w, b, ln_g, ln_b)


def _conv_bwd(z3, dy, w, b, ln_g, ln_b, name):
    bsz = z3.shape[0]
    seq = lambda cb: pl.BlockSpec((None, SEQ, GROUP_WIDTH), functools.partial(lambda i, cb: (i, 0, cb), cb=cb))
    full = lambda a: pl.BlockSpec(a.shape, lambda i: (0,) * a.ndim)
    vec = pl.BlockSpec((1, GROUP_WIDTH), lambda i: (0, 0))

    def body(a_ref, g_ref, dy_ref, w_ref, b_ref, lg_ref, lb_ref, da_ref, dg_ref, dw_ref, db_ref, dlg_ref, dlb_ref, hpad, dpad, dw8):
        @pl.when(pl.program_id(0) == 0)
        def _():
            dw8[...] = jnp.zeros_like(dw8)
            db_ref[...] = jnp.zeros_like(db_ref)
            dlg_ref[...] = jnp.zeros_like(dlg_ref)
            dlb_ref[...] = jnp.zeros_like(dlb_ref)

        _conv_fill(hpad, lambda r0: _glu(a_ref[pl.ds(r0, CONV_TILE), :], g_ref[pl.ds(r0, CONV_TILE), :]))
        zeros = jnp.zeros((CONV_LEAD, GROUP_WIDTH), F32)
        dpad[pl.ds(0, CONV_LEAD), :] = zeros
        dpad[pl.ds(CONV_LEAD + SEQ, CONV_LEAD), :] = zeros

        def through_post(t, carry):
            r0 = pl.multiple_of(t * CONV_TILE, CONV_TILE)
            conv = _conv_tile(hpad, w_ref, r0, False)
            _, vjp = jax.vjp(_conv_post, conv, b_ref[...], lg_ref[...], lb_ref[...])
            dconv, db, dlg, dlb = vjp(dy_ref[pl.ds(r0, CONV_TILE), :])
            db_ref[...] += db
            dlg_ref[...] += dlg
            dlb_ref[...] += dlb
            dpad[pl.ds(CONV_LEAD + r0, CONV_TILE), :] = dconv
            win = hpad[pl.ds(r0, CONV_WINDOW), :]
            for k in range(CONV_WIDTH):
                prod = dconv * _conv_shifted(win, k + 1)
                part = prod[0:8]
                for j in range(1, CONV_TILE // 8):
                    part = part + prod[8 * j:8 * j + 8]
                dw8[k] += part
            return carry

        lax.fori_loop(0, SEQ // CONV_TILE, through_post, 0)

        def through_glu(t, carry):
            r0 = pl.multiple_of(t * CONV_TILE, CONV_TILE)
            dh = _conv_tile(dpad, w_ref, r0, True)
            rows = pl.ds(r0, CONV_TILE)
            _, vjp = jax.vjp(_glu, a_ref[rows, :], g_ref[rows, :])
            da, dg = vjp(dh)
            da_ref[rows, :] = da
            dg_ref[rows, :] = dg
            return carry

        lax.fori_loop(0, SEQ // CONV_TILE, through_glu, 0)
        dw_ref[...] = jnp.sum(dw8[...], axis=1)

    out = jax.ShapeDtypeStruct((bsz, SEQ, GROUP_WIDTH), F32)
    v = jax.ShapeDtypeStruct((1, GROUP_WIDTH), F32)
    return pl.pallas_call(
        body, name=name, grid=(bsz,), in_specs=[seq(5), seq(6), seq(0), full(w), full(b), full(ln_g), full(ln_b)],
        out_specs=[seq(0), seq(0), pl.BlockSpec((CONV_WIDTH, GROUP_WIDTH), lambda i: (0, 0)), vec, vec, vec],
        out_shape=[out, out, jax.ShapeDtypeStruct((CONV_WIDTH, GROUP_WIDTH), F32), v, v, v],
        scratch_shapes=[pltpu.VMEM((SEQ + 2 * CONV_LEAD, GROUP_WIDTH), F32), pltpu.VMEM((SEQ + 2 * CONV_LEAD, GROUP_WIDTH), F32),
                        pltpu.VMEM((CONV_WIDTH, 8, GROUP_WIDTH), F32)],
        compiler_params=_cparams(("arbitrary",)),
    )(z3, z3, dy, w, b, ln_g, ln_b)


def _mixnorm_fwd(ys, gains, name):
    w = GROUP_WIDTH

    def fn(*v):
        return (jnp.concatenate([_rms(v[i], v[4 + i]) for i in range(4)], axis=-1),)

    return _rowmap(fn, [(y, w, 0) for y in ys], list(gains), [(4 * w, BF16)], [], name, 512, ys[0].shape[0])[0]


def _mixnorm_bwd(dyn, ys, gains, name, follow=None):
    w = GROUP_WIDTH
    gains = list(gains) if follow is None else [*gains, follow]

    def fn(*v):
        dys, dgs = [], []
        for i in range(4):
            _, vjp = jax.vjp(_rms, v[4 + i], v[8 + i])
            dy, dg = vjp(v[i])
            dys.append(dy)
            dgs.append(dg)
        return (*dys, *dgs)

    rows = [(dyn, w, i) for i in range(4)] + [(y, w, 0) for y in ys]
    return _rowmap(fn, rows, list(gains), [(w, F32)] * 4, [(1, w)] * 4, name, 512, dyn.shape[0])


def _adamw_fn(w, g, m, v):
    m = ADAM_B1 * m + (1.0 - ADAM_B1) * g
    v = ADAM_B2 * v + (1.0 - ADAM_B2) * (g * g)
    m_hat = m / (1.0 - ADAM_B1 ** ADAM_STEP)
    v_hat = v / (1.0 - ADAM_B2 ** ADAM_STEP)
    delta = -ADAM_LR * (m_hat / (jnp.sqrt(v_hat) + ADAM_EPS) + ADAM_WD * w)
    return delta, m, v


def _adamw(w, g, m, v, name):
    r, c = w.shape
    tm = _pick(r, (256, 128, 64, 32, 16, 8))
    return _rowmap(_adamw_fn, [(a, c, 0) for a in (w, g, m, v)], [], [(c, F32)] * 3, [], name, tm, r)


def _layer_params(l, small, big):
    tile_row = lambda g, n: jnp.tile(g, n)[None, :]
    row = lambda g: g[None, :]
    w_s = small["sgu_w"][l].astype(BF16)
    return dict(
        norm1_g=row(small["norm1_g"][l]), norm2_g=row(small["norm2_g"][l]),
        w_s=w_s, w_s_t=jnp.swapaxes(w_s, 1, 2), bm=jnp.repeat(small["sgu_b"][l].T, HEAD_DIM, axis=1),
        gq_dil=tile_row(small["dil_qn_g"][l], N_HEADS), gk_dil=tile_row(small["dil_kn_g"][l], N_HEADS),
        conv_w=small["conv_w"][l], conv_b=row(small["conv_b"][l]), conv_ln_g=row(small["conv_ln_g"][l]),
        conv_ln_b=row(small["conv_ln_b"][l]),
        gq_gqa=tile_row(small["gqa_qn_g"][l], N_HEADS), gk_gqa=tile_row(small["gqa_kn_g"][l], KV_WIDTH // HEAD_DIM),
        mix_g=[row(small["mix_norm_g"][l][i * GROUP_WIDTH:(i + 1) * GROUP_WIDTH]) for i in range(4)],
        big=big,
    )


def _layer_fwd(x, p, table, bsz, tag):
    t = x.shape[0]
    seq3 = lambda a: a.reshape(bsz, SEQ, a.shape[-1])
    flat = lambda a: a.reshape(t, a.shape[-1])
    h1 = _rmsnorm_fwd(x, p["norm1_g"], tag + "rms1")
    z = _matmul([(h1, p["big"]("w_in", h1))], "nn", F32, tag + "mm_z")
    y_a = _sgu_fwd(z, p["w_s"], p["bm"], tag + "sgu_fwd")
    qb, kb, vb = _bprep_fwd(z, p["gq_dil"], p["gk_dil"], tag + "dil_prep")
    outs, lses = [], []
    for _, dil in DIL_PATTERNS:
        o, lse = _dil_fwd(seq3(qb), seq3(kb), seq3(vb), table, dil, f"{tag}dil{dil}_fwd")
        outs.append(flat(o))
        lses.append(flat(lse))
    y_b = _mixture_fwd(outs, lses, tag + "dil_mix")
    y_c = flat(_conv_fwd(seq3(z), p["conv_w"], p["conv_b"], p["conv_ln_g"], p["conv_ln_b"], tag + "conv_fwd"))
    qd, kx, vx = _dprep_fwd(z, p["gq_gqa"], p["gk_gqa"], tag + "gqa_prep")
    y_d = flat(_gqa_fwd(seq3(qd), seq3(kx), seq3(vx), tag + "gqa_fwd"))
    ys = [y_a, y_b, y_c, y_d]
    yn = _mixnorm_fwd(ys, p["mix_g"], tag + "mixnorm")
    x_mid = _matmul([(yn, p["big"]("w_out", yn))], "nn", F32, tag + "mm_out", residual=x)
    h2 = _rmsnorm_fwd(x_mid, p["norm2_g"], tag + "rms2")
    act, gate, up = _ffn_up(h2, p["big"]("w_gate", yn), p["big"]("w_up", yn), tag + "ffn_up")
    x_out = _matmul([(act, p["big"]("w_down", yn))], "nn", F32, tag + "mm_down", residual=x_mid)
    saved = dict(x=x, h1=h1, z=z, qb=qb, kb=kb, vb=vb, outs=outs, lses=lses, qd=qd, kx=kx, vx=vx, ys=ys, yn=yn, x_mid=x_mid,
                 h2=h2, act=act, gate=gate, up=up)
    return x_out, saved


def _layer_bwd(dx_out, s, p, table, bsz, tag, emit, mid_hook):
    t = dx_out.shape[0]
    seq3 = lambda a: a.reshape(bsz, SEQ, a.shape[-1])
    flat = lambda a: a.reshape(t, a.shape[-1])
    z = s["z"]
    small = {}
    weight = lambda name: p["big"](name, None)
    emit("w_down", _matmul([(s["act"], dx_out)], "tn", BF16, tag + "mm_dwdown").reshape(N_CHIPS, FFN_HIDDEN // N_CHIPS, D_MODEL))
    dgate, dup = _ffn_down_bwd(dx_out, weight("w_down"), s["gate"], s["up"], tag + "ffn_dact")
    emit("w_gate", _matmul([(s["h2"], dgate)], "tn", BF16, tag + "mm_dwgate", slabs=N_CHIPS))
    started = emit("w_up", _matmul([(s["h2"], dup)], "tn", BF16, tag + "mm_dwup", slabs=N_CHIPS))
    dh2 = _matmul([(dgate, weight("w_gate")), (dup, weight("w_up"))], "nt", F32, tag + "mm_dh2")
    dx_mid, dg2 = _rmsnorm_bwd(dh2, s["x_mid"], p["norm2_g"], dx_out, tag + "rms2_bwd", follow=started)
    small["norm2_g"] = dg2[0]
    mid_hook(dx_mid)
    dyn = _matmul([(dx_mid, weight("w_out"))], "nt", F32, tag + "mm_dyn")
    started = emit("w_out", _matmul([(s["yn"], dx_mid)], "tn", BF16, tag + "mm_dwout").reshape(N_CHIPS, D_MODEL // N_CHIPS, D_MODEL))
    *dys, dga, dgb, dgc, dgd = _mixnorm_bwd(dyn, s["ys"], p["mix_g"], tag + "mixnorm_bwd", follow=started)
    small["mix_norm_g"] = jnp.concatenate([dga[0], dgb[0], dgc[0], dgd[0]])
    du, dv, dws, dbs = _sgu_bwd(z, dys[0], p["w_s"], p["w_s_t"], p["bm"], tag + "sgu_bwd")
    small["sgu_w"] = dws
    small["sgu_b"] = dbs[:, :N_HEADS].T
    *douts, dl0, dl1, dl2 = _mixture_bwd(s["outs"], s["lses"], dys[1], tag + "dil_mix_bwd")
    dlses = [dl0, dl1, dl2]
    dqs, dks, dvs, dscs = [], [], [], []
    for i, (_, dil) in enumerate(DIL_PATTERNS):
        dq, dk, dvv, dsc = _dil_bwd(seq3(s["qb"]), seq3(s["kb"]), seq3(s["vb"]), seq3(douts[i]), seq3(dlses[i]), table, dil,
                                    f"{tag}dil{dil}_bwd")
        dqs.append(flat(dq))
        dks.append(flat(dk))
        dvs.append(flat(dvv))
        dscs.append(dsc)
    dbq, dbk, dbv, dgq, dgk = _bprep_bwd(z, dqs, dks, dvs, p["gq_dil"], p["gk_dil"], tag + "dil_prep_bwd")
    small["dil_qn_g"], small["dil_kn_g"] = dgq[0], dgk[0]
    dca, dcg, dcw, dcb, dclg, dclb = _conv_bwd(seq3(z), seq3(dys[2]), p["conv_w"], p["conv_b"], p["conv_ln_g"], p["conv_ln_b"],
                                               tag + "conv_bwd")
    small["conv_w"], small["conv_b"], small["conv_ln_g"], small["conv_ln_b"] = dcw, dcb[0], dclg[0], dclb[0]
    dqd, dkx, dvx = _gqa_bwd(seq3(s["qd"]), seq3(s["kx"]), seq3(s["vx"]), seq3(dys[3]), tag + "gqa_bwd")
    ddq, ddk, ddv, dgq, dgk = _dprep_bwd(z, flat(dqd), flat(dkx), flat(dvx), p["gq_gqa"], p["gk_gqa"], tag + "gqa_prep_bwd")
    small["gqa_qn_g"], small["gqa_kn_g"] = dgq[0], dgk[0]
    dz = jnp.concatenate([du, dv, dbq, dbk, dbv, flat(dca), flat(dcg), ddq, ddk, ddv], axis=1)
    dz4 = dz.reshape(t, N_CHIPS, IN_WIDTH // N_CHIPS).transpose(1, 0, 2)
    started = emit("w_in", _matmul([(s["h1"], dz4)], "tn", BF16, tag + "mm_dwin", slabs=N_CHIPS))
    dh1 = _matmul([(dz, weight("w_in"))], "nt", F32, tag + "mm_dh1")
    dx, dg1 = _rmsnorm_bwd(dh1, s["x"], p["norm1_g"], dx_mid, tag + "rms1_bwd", follow=started)
    small["norm1_g"] = dg1[0]
    return dx, small, dscs


def _local_step(x, target, small, big, emit, mid_hook, bsz):
    table = small["rel_bias"]
    params = [_layer_params(l, small, functools.partial(big, l)) for l in range(DEPTH)]
    saved = []
    h = x
    for l in range(DEPTH):
        h, sv = _layer_fwd(h, params[l], table, bsz, f"l{l}_")
        saved.append(sv)
    dh, loss = _loss_fwd_bwd(h, target, "loss")
    small_grads, dscs = [None] * DEPTH, [None] * DEPTH
    for l in reversed(range(DEPTH)):
        dh, small_grads[l], dscs[l] = _layer_bwd(dh, saved[l], params[l], table, bsz, f"l{l}_", functools.partial(emit, l),
                                                 functools.partial(mid_hook, l))
    fold_in = [dscs[l][i] for i in range(len(DIL_PATTERNS)) for l in range(DEPTH)]
    stacked = {k: jnp.stack([small_grads[l][k] for l in range(DEPTH)]) for k in small_grads[0]}
    stacked["rel_bias"] = _relbias_fold(fold_in, "relbias_fold")[:, :N_HEADS]
    return loss, dh, stacked


def _mesh_pos():
    return lax.axis_index("x"), lax.axis_index("y"), lax.axis_index("c")


def _other_chips(x, y):
    return [(1 - x, y), (x, 1 - y), (1 - x, 1 - y)]


_ANY = pl.BlockSpec(memory_space=pl.ANY)


def _swap_other_half(arrs, name):
    n = len(arrs)

    def body(*refs):
        in_refs, out_refs, send_sems, recv_sems = refs[:n], refs[n:2 * n], refs[2 * n], refs[2 * n + 1]
        x, y, c = _mesh_pos()
        copies = []
        for k in range(n):
            h = arrs[k].shape[1] // 2
            copies.append(pltpu.make_async_remote_copy(
                src_ref=in_refs[k].at[:, pl.ds((1 - c) * h, h)], dst_ref=out_refs[k], send_sem=send_sems.at[k],
                recv_sem=recv_sems.at[k], device_id=(x, y, 1 - c), device_id_type=MESH))
        for cp in copies:
            cp.start()
        for cp in copies:
            cp.wait()

    return pl.pallas_call(
        body, name=name, in_specs=[_ANY] * n, out_specs=[_ANY] * n,
        out_shape=[jax.ShapeDtypeStruct((a.shape[0], a.shape[1] // 2, a.shape[2]), a.dtype) for a in arrs],
        scratch_shapes=[pltpu.SemaphoreType.DMA((n,)), pltpu.SemaphoreType.DMA((n,))],
    )(*arrs)


def _swap_sibling(arrs, name):
    n = len(arrs)

    def body(*refs):
        in_refs, out_refs, send_sems, recv_sems = refs[:n], refs[n:2 * n], refs[2 * n], refs[2 * n + 1]
        x, y, c = _mesh_pos()
        copies = [pltpu.make_async_remote_copy(src_ref=in_refs[k], dst_ref=out_refs[k], send_sem=send_sems.at[k],
                                               recv_sem=recv_sems.at[k], device_id=(x, y, 1 - c), device_id_type=MESH)
                  for k in range(n)]
        for cp in copies:
            cp.start()
        for cp in copies:
            cp.wait()

    return pl.pallas_call(
        body, name=name, in_specs=[_ANY] * n, out_specs=[_ANY] * n,
        out_shape=[jax.ShapeDtypeStruct(a.shape, a.dtype) for a in arrs],
        scratch_shapes=[pltpu.SemaphoreType.DMA((n,)), pltpu.SemaphoreType.DMA((n,))],
    )(*arrs)


def _join_halves(mine, theirs, c):
    h = mine.shape[-2]
    axis = mine.ndim - 2
    full = jnp.zeros((*mine.shape[:-2], 2 * h, mine.shape[-1]), mine.dtype)
    full = lax.dynamic_update_slice_in_dim(full, mine, c * h, axis=axis)
    return lax.dynamic_update_slice_in_dim(full, theirs, (1 - c) * h, axis=axis)


_HBM = pl.BlockSpec(memory_space=pltpu.HBM)
_SEM = pl.BlockSpec(memory_space=pltpu.SEMAPHORE)
_DATAFLOW = pltpu.SideEffectType.DATAFLOW_SIDE_EFFECTING


def _chip_copies(src_refs, land_refs, send_sems, recv_sems, scatter):
    x, y, c = _mesh_pos()
    me = 2 * x + y
    out = []
    for k, (src_ref, land_ref) in enumerate(zip(src_refs, land_refs)):
        for j, (cx, cy) in enumerate(_other_chips(x, y)):
            there = 2 * cx + cy
            src = src_ref.at[there] if scatter else src_ref
            sems = dict(send_sem=send_sems.at[3 * k + j], recv_sem=recv_sems.at[3 * k + j], device_id=(cx, cy, c), device_id_type=MESH)
            out.append((pltpu.make_async_remote_copy(src_ref=src, dst_ref=land_ref.at[me], **sems),
                        pltpu.make_async_remote_copy(src_ref=src, dst_ref=land_ref.at[there], **sems)))
    return out


def _chips_start(srcs, scatter, after, name):
    n = len(srcs)
    lands = [lax.empty((N_CHIPS, *s.shape[-2:]), s.dtype) for s in srcs]

    def body(*refs):
        src_refs, land_refs = refs[:n], refs[n:2 * n]
        send_sems, recv_sems, token = refs[2 * n + 1], refs[2 * n + 2], refs[-1]
        for sent, _ in _chip_copies(src_refs, land_refs, send_sems, recv_sems, scatter):
            sent.start()
        token[...] = jnp.zeros_like(token)

    hbm = lambda a: pltpu.HBM(a.shape, a.dtype)
    res = pl.pallas_call(
        body, name=name,
        in_specs=[_HBM] * (2 * n) + [_ANY],
        out_specs=[_SEM, _SEM] + [_HBM] * (2 * n) + [pl.BlockSpec(memory_space=pltpu.VMEM)],
        out_shape=[pltpu.SemaphoreType.DMA((3 * n,)), pltpu.SemaphoreType.DMA((3 * n,))] + [hbm(a) for a in srcs] + [hbm(a) for a in lands]
        + [jax.ShapeDtypeStruct((8, 128), F32)],
        input_output_aliases={i: 2 + i for i in range(2 * n)},
        compiler_params=pltpu.CompilerParams(has_side_effects=_DATAFLOW),
    )(*[pltpu.with_memory_space_constraint(a, pltpu.HBM) for a in (*srcs, *lands)], after)
    return (res[0], res[1], res[2:2 + n], res[2 + n:2 + 2 * n]), res[-1]


def _chips_wait(handle, scatter, after, name):
    send_sems, recv_sems, srcs, lands = handle
    n = len(srcs)

    def body(*refs):
        src_refs, land_refs = refs[:n], refs[n:2 * n]
        send_sems, recv_sems = refs[2 * n], refs[2 * n + 1]
        for sent, landed in _chip_copies(src_refs, land_refs, send_sems, recv_sems, scatter):
            sent.wait_send()
            landed.wait_recv()

    hbm = lambda a: pltpu.HBM(a.shape, a.dtype)
    res = pl.pallas_call(
        body, name=name,
        in_specs=[_HBM] * (2 * n) + [_SEM, _SEM, _ANY], out_specs=[_HBM] * (2 * n),
        out_shape=[hbm(a) for a in srcs] + [hbm(a) for a in lands],
        input_output_aliases={i: i for i in range(2 * n)},
        compiler_params=pltpu.CompilerParams(has_side_effects=_DATAFLOW),
    )(*srcs, *lands, send_sems, recv_sems, after)
    return res[n:]


N_DEV = 8


def _allgather_sum_small(block, name):
    m_per, n = block.shape

    def body(x_ref, out_ref, sum_ref, send_sems, recv_sems, local_sem):
        x, y, c = _mesh_pos()
        me, sibling = (x, y, c), (x, y, 1 - c)
        chips = _other_chips(x, y)

        def rows(px, py, pc):
            return out_ref.at[pl.ds((4 * px + 2 * py + pc) * m_per, m_per), :]

        def copy(k, blk, to, src=None):
            return pltpu.make_async_remote_copy(src_ref=rows(*blk) if src is None else src, dst_ref=rows(*blk),
                                                send_sem=send_sems.at[k], recv_sem=recv_sems.at[k], device_id=to, device_id_type=MESH)

        mine = pltpu.make_async_copy(x_ref, rows(*me), local_sem)
        mine.start()
        first = [copy(0, me, sibling, src=x_ref)]
        first += [copy(1 + j, me, (*chip, c), src=x_ref) for j, chip in enumerate(chips)]
        for cp in first:
            cp.start()
        passed = [copy(4 + j, (*chip, c), sibling) for j, chip in enumerate(chips)]
        for j, chip in enumerate(chips):
            copy(1 + j, (*chip, c), me).wait_recv()
            passed[j].start()
        copy(0, sibling, me).wait_recv()
        for j, chip in enumerate(chips):
            copy(4 + j, (*chip, 1 - c), me).wait_recv()
        for cp in first + passed:
            cp.wait_send()
        mine.wait()
        total = out_ref[pl.ds(0, m_per), :]
        for d in range(1, N_DEV):
            total = total + out_ref[pl.ds(d * m_per, m_per), :]
        sum_ref[...] = total

    vmem = pl.BlockSpec(memory_space=pltpu.VMEM)
    return pl.pallas_call(
        body, name=name, in_specs=[vmem], out_specs=[vmem, vmem],
        out_shape=[jax.ShapeDtypeStruct((N_DEV * m_per, n), F32), jax.ShapeDtypeStruct((m_per, n), F32)],
        scratch_shapes=[pltpu.SemaphoreType.DMA((7,)), pltpu.SemaphoreType.DMA((7,)), pltpu.SemaphoreType.DMA],
        compiler_params=pltpu.CompilerParams(vmem_limit_bytes=V7X_VMEM_LIMIT),
    )(block)


WEIGHTS = ("rel_bias", "norm1_g", "w_in", "sgu_w", "sgu_b", "dil_qn_g", "dil_kn_g", "conv_w", "conv_b", "conv_ln_g", "conv_ln_b",
           "gqa_qn_g", "gqa_kn_g", "mix_norm_g", "w_out", "norm2_g", "w_gate", "w_up", "w_down")
SHARDED = ("w_in", "w_out", "w_gate", "w_up", "w_down")
COLUMN_SHARDED = ("w_in", "w_gate", "w_up")
REPLICATED = tuple(k for k in WEIGHTS if k not in SHARDED and k != "conv_w")


def _pack(parts):
    flat = jnp.concatenate([p.reshape(-1) for p in parts])
    pad = (-flat.shape[0]) % (8 * 128)
    return jnp.pad(flat, (0, pad)).reshape(-1, 128)


def _unpack(buf, shapes):
    flat = buf.reshape(-1)
    out, at = [], 0
    for s in shapes:
        size = math.prod(s)
        out.append(flat[at:at + size].reshape(s))
        at += size
    return out


RS_TM = (256, 128, 64, 32, 16)


def _add_halves(g, sib, c, name):
    slabs, h, cols = sib.shape
    tm = _pick(h, RS_TM)
    nb = h // tm

    def body(c_ref, g_ref, s_ref, o_ref):
        o_ref[...] = (g_ref[...].astype(F32) + s_ref[...].astype(F32)).astype(BF16)

    blk = pl.BlockSpec((None, tm, cols), lambda j, i, c_ref: (j, i, 0))
    grid_spec = pltpu.PrefetchScalarGridSpec(
        num_scalar_prefetch=1, grid=(slabs, nb),
        in_specs=[pl.BlockSpec((None, tm, cols), lambda j, i, c_ref: (j, c_ref[0] * nb + i, 0)), blk], out_specs=blk)
    return pl.pallas_call(body, name=name, grid_spec=grid_spec, out_shape=jax.ShapeDtypeStruct(sib.shape, BF16),
                          compiler_params=_cparams(("parallel", "parallel")))(c, g, sib)


def _add_own_three(own, land, chip, name):
    _, h, cols = own.shape
    tm = _pick(h, RS_TM)

    def body(chip_ref, own_ref, l0_ref, l1_ref, l2_ref, o_ref):
        o_ref[...] = ((own_ref[...].astype(F32) + l0_ref[...].astype(F32)) + l1_ref[...].astype(F32)) + l2_ref[...].astype(F32)

    slot = lambda r: pl.BlockSpec((None, tm, cols), functools.partial(lambda i, chip_ref, r: (jnp.bitwise_xor(chip_ref[0], r), i, 0), r=r))
    grid_spec = pltpu.PrefetchScalarGridSpec(
        num_scalar_prefetch=1, grid=(h // tm,),
        in_specs=[slot(0), slot(1), slot(2), slot(3)],
        out_specs=pl.BlockSpec((tm, cols), lambda i, chip_ref: (i, 0)))
    return pl.pallas_call(body, name=name, grid_spec=grid_spec, out_shape=jax.ShapeDtypeStruct((h, cols), F32),
                          compiler_params=_cparams(("parallel",)))(chip, own, land, land, land)


def _reduce_start(grads, c1, after, tag):
    from_sibling = _swap_other_half(grads, tag + "pair")
    sums = [_add_halves(g, s, c1, f"{tag}add2_{k}") for k, (g, s) in enumerate(zip(grads, from_sibling))]
    handle, token = _chips_start(sums, True, after, tag + "start")
    return (handle, sums), token


def _reduce_finish(started, c, chip1, after, tag):
    handle, sums = started
    lands = _chips_wait(handle, True, after, tag + "wait")
    totals = [_add_own_three(s, land, chip1, f"{tag}add4_{k}") for k, (s, land) in enumerate(zip(sums, lands))]
    return [_join_halves(mine, theirs, c) for mine, theirs in zip(totals, _swap_sibling(totals, tag + "share"))]


GATHER_GROUPS = (("w_in",), ("w_out", "w_gate", "w_up", "w_down"))
REDUCE_GROUPS = (("w_down", "w_gate", "w_up"), ("w_out",), ("w_in",))


def kernel(x, rel_bias, norm1_g, w_in, sgu_w, sgu_b, dil_qn_g, dil_kn_g, conv_w, conv_b, conv_ln_g, conv_ln_b, gqa_qn_g, gqa_kn_g, mix_norm_g, w_out, norm2_g, w_gate, w_up, w_down, loss_target, m_rel_bias, m_norm1_g, m_w_in, m_sgu_w, m_sgu_b, m_dil_qn_g, m_dil_kn_g, m_conv_w, m_conv_b, m_conv_ln_g, m_conv_ln_b, m_gqa_qn_g, m_gqa_kn_g, m_mix_norm_g, m_w_out, m_norm2_g, m_w_gate, m_w_up, m_w_down, v_rel_bias, v_norm1_g, v_w_in, v_sgu_w, v_sgu_b, v_dil_qn_g, v_dil_kn_g, v_conv_w, v_conv_b, v_conv_ln_g, v_conv_ln_b, v_gqa_qn_g, v_gqa_kn_g, v_mix_norm_g, v_w_out, v_norm2_g, v_w_gate, v_w_up, v_w_down):
    w = dict(rel_bias=rel_bias, norm1_g=norm1_g, w_in=w_in, sgu_w=sgu_w, sgu_b=sgu_b, dil_qn_g=dil_qn_g, dil_kn_g=dil_kn_g,
             conv_w=conv_w, conv_b=conv_b, conv_ln_g=conv_ln_g, conv_ln_b=conv_ln_b, gqa_qn_g=gqa_qn_g, gqa_kn_g=gqa_kn_g,
             mix_norm_g=mix_norm_g, w_out=w_out, norm2_g=norm2_g, w_gate=w_gate, w_up=w_up, w_down=w_down)
    m = dict(rel_bias=m_rel_bias, norm1_g=m_norm1_g, w_in=m_w_in, sgu_w=m_sgu_w, sgu_b=m_sgu_b, dil_qn_g=m_dil_qn_g,
             dil_kn_g=m_dil_kn_g, conv_w=m_conv_w, conv_b=m_conv_b, conv_ln_g=m_conv_ln_g, conv_ln_b=m_conv_ln_b,
             gqa_qn_g=m_gqa_qn_g, gqa_kn_g=m_gqa_kn_g, mix_norm_g=m_mix_norm_g, w_out=m_w_out, norm2_g=m_norm2_g,
             w_gate=m_w_gate, w_up=m_w_up, w_down=m_w_down)
    v = dict(rel_bias=v_rel_bias, norm1_g=v_norm1_g, w_in=v_w_in, sgu_w=v_sgu_w, sgu_b=v_sgu_b, dil_qn_g=v_dil_qn_g,
             dil_kn_g=v_dil_kn_g, conv_w=v_conv_w, conv_b=v_conv_b, conv_ln_g=v_conv_ln_g, conv_ln_b=v_conv_ln_b,
             gqa_qn_g=v_gqa_qn_g, gqa_kn_g=v_gqa_kn_g, mix_norm_g=v_mix_norm_g, w_out=v_w_out, norm2_g=v_norm2_g,
             w_gate=v_w_gate, w_up=v_w_up, w_down=v_w_down)
    bsz = x.shape[0]
    t = bsz * SEQ
    xi, yi, ci = _mesh_pos()
    chip = 2 * xi + yi
    conv_cols = conv_w.shape[-1]

    conv_rows = DEPTH * CONV_WIDTH
    conv_block = jnp.pad(conv_w.reshape(conv_rows, conv_cols), ((0, (-conv_rows) % 8), (0, 0)))
    every, _ = _allgather_sum_small(conv_block, "conv_w_gather")
    every = every.reshape(N_DEV, conv_block.shape[0], conv_cols)
    conv_w_full = jnp.concatenate([every[2 * j, :conv_rows].reshape(DEPTH, CONV_WIDTH, conv_cols) for j in range(N_CHIPS)], axis=-1)

    c1 = jnp.reshape(ci, (1,)).astype(jnp.int32)
    chip1 = jnp.reshape(chip, (1,)).astype(jnp.int32)

    def own_half(k, l):
        a = w[k][l].astype(BF16)
        h = a.shape[0] // 2
        return lax.dynamic_slice_in_dim(a, ci * h, h, axis=0)

    fetches, token = {}, every
    for l in range(DEPTH):
        for gi, group in enumerate(GATHER_GROUPS):
            halves = [own_half(k, l) for k in group]
            handle, token = _chips_start(halves, False, token, f"l{l}_gather{gi}_start")
            fetches[l, gi] = (handle, halves)
    all_started = token
    gathered = {}

    def big(l, name, after):
        if (l, name) not in gathered:
            gi = [name in group for group in GATHER_GROUPS].index(True)
            handle, halves = fetches[l, gi]
            lands = _chips_wait(handle, False, all_started if after is None else after, f"l{l}_gather{gi}_wait")
            mine = [lax.dynamic_update_slice_in_dim(land, own[None], chip, axis=0) for land, own in zip(lands, halves)]
            for k, a, b in zip(GATHER_GROUPS[gi], mine, _swap_sibling(mine, f"l{l}_gather{gi}_share")):
                g = _join_halves(a, b, ci)
                rows, cols = g.shape[1:]
                gathered[l, k] = g.transpose(1, 0, 2).reshape(rows, N_CHIPS * cols) if k in COLUMN_SHARDED else g.reshape(N_CHIPS * rows, cols)
        return gathered[l, name]

    big(0, "w_in", None)

    pending, started, reduced = {}, {}, {}

    def emit(l, name, g):
        pending[l, name] = g
        for gi, group in enumerate(REDUCE_GROUPS):
            if name in group and all((l, k) in pending for k in group):
                started[l, gi], token = _reduce_start([pending[l, k] for k in group], c1, g, f"l{l}_reduce{gi}_")
                return token
        return None

    def finish(l, after):
        for gi, group in enumerate(REDUCE_GROUPS):
            for k, r in zip(group, _reduce_finish(started[l, gi], ci, chip1, after, f"l{l}_reduce{gi}_")):
                reduced[l, k] = r
            after = reduced[l, group[0]]

    def mid_hook(l, a):
        if l + 1 < DEPTH:
            finish(l + 1, a)

    small = {k: w[k] for k in REPLICATED}
    small["conv_w"] = conv_w_full
    loss, dx, small_grads = _local_step(x.reshape(t, D_MODEL), loss_target.reshape(t, D_MODEL), small, big, emit, mid_hook, bsz)
    loss = lax.psum(loss[0, 0], ("x", "y", "c"))
    finish(0, dx)

    grads, deltas, new_m, new_v = {}, {}, {}, {}
    for i, k in enumerate(SHARDED):
        g2 = jnp.concatenate([reduced[l, k] for l in range(DEPTH)], axis=0)
        d2, m2, v2 = _adamw(w[k].reshape(g2.shape), g2, m[k].reshape(g2.shape), v[k].reshape(g2.shape), "adamw_" + k)
        grads[k], deltas[k], new_m[k], new_v[k] = (a.reshape(w[k].shape) for a in (g2, d2, m2, v2))

    names = REPLICATED + ("conv_w",)
    shapes = [small_grads[k].shape for k in names]
    _, summed = _allgather_sum_small(_pack([small_grads[k] for k in names]), "small_grads_sum")
    summed_parts = dict(zip(names, _unpack(summed, shapes)))
    rep_shapes = [w[k].shape for k in REPLICATED]
    packed = [_pack([src[k] for k in REPLICATED]) for src in (w, {k: summed_parts[k] for k in REPLICATED}, m, v)]
    d_p, m_p, v_p = _adamw(*packed, "adamw_replicated")
    for k, gk, dk, mk, vk in zip(REPLICATED, _unpack(packed[1], rep_shapes), _unpack(d_p, rep_shapes), _unpack(m_p, rep_shapes),
                                 _unpack(v_p, rep_shapes)):
        grads[k], deltas[k], new_m[k], new_v[k] = gk, dk, mk, vk
    g_conv = lax.dynamic_slice_in_dim(summed_parts["conv_w"], chip * conv_cols, conv_cols, axis=2)
    packed = [_pack([a]) for a in (conv_w, g_conv, m["conv_w"], v["conv_w"])]
    d_p, m_p, v_p = _adamw(*packed, "adamw_conv_w")
    grads["conv_w"] = g_conv
    deltas["conv_w"], new_m["conv_w"], new_v["conv_w"] = (_unpack(a, [conv_w.shape])[0] for a in (d_p, m_p, v_p))

    return (loss, dx.reshape(x.shape), *[grads[k] for k in WEIGHTS], *[deltas[k] for k in WEIGHTS],
            *[new_m[k] for k in WEIGHTS], *[new_v[k] for k in WEIGHTS])
```

```python
import functools
import math

import numpy as np
import jax
import jax.numpy as jnp
from jax import lax
from jax.experimental import pallas as pl
from jax.experimental.pallas import tpu as pltpu

F32 = jnp.float32
BF16 = jnp.bfloat16

D_MODEL = 2048
SEQ = 2048
DEPTH = 2
HEAD_DIM = 64
GROUP_WIDTH = 512
N_HEADS = 8
SGU_CHUNK = 128
DIL_PATTERNS = ((128, 1), (512, 4), (2048, 16))
DIL_HALF = 64
CONV_WIDTH = 31
KV_WIDTH = 128
GRID_W = 64
ROPE_THETA = 10000.0
REL_BUCKETS = 32
REL_MAX_DIST = 1024
FFN_HIDDEN = 5632
IN_WIDTH = 4352
RMS_EPS = 1e-6
LN_EPS = 1e-5
ADAM_LR = 0.001
ADAM_B1 = 0.9
ADAM_B2 = 0.999
ADAM_EPS = 1e-08
ADAM_WD = 0.01
ADAM_STEP = 10
N_CHIPS = 4

V7X_VMEM_LIMIT = 56 * 1024 * 1024
MATMUL_VMEM_BUDGET = 40 * 1024 * 1024
HI = lax.Precision.HIGHEST
MESH = pl.DeviceIdType.MESH


def _cparams(sem=None):
    return pltpu.CompilerParams(dimension_semantics=sem, vmem_limit_bytes=V7X_VMEM_LIMIT)


def _pick(n, cands):
    for c in cands:
        if n % c == 0:
            return c
    raise ValueError(f"no tile for {n}")


_DIMS = {"nn": (((1,), (0,)), ((), ())), "nt": (((1,), (1,)), ((), ())), "tn": (((0,), (0,)), ((), ()))}


def _matmul(pairs, mode, out_dtype, name, residual=None, slabs=1):
    a0, b0 = pairs[0]
    b3 = b0.ndim == 3
    if mode == "nn":
        (M, K), N = a0.shape, b0.shape[1]
    elif mode == "nt":
        (M, K), N = a0.shape, b0.shape[0]
    else:
        (K, M) = a0.shape
        N = b0.shape[-1] if b3 else b0.shape[1] // slabs
    npairs = len(pairs)
    tm = _pick(M, (1024, 1408, 512, 256))
    tn = _pick(N, (1024, 1408, 2176, 512, 256, N))
    a_bytes, b_bytes, o_bytes = a0.dtype.itemsize, b0.dtype.itemsize, jnp.dtype(out_dtype).itemsize
    fits = lambda tk: (2 * npairs * tk * (tm * a_bytes + tn * b_bytes) + tm * tn * (4 + 2 * o_bytes + (8 if residual is not None else 0))
                       <= MATMUL_VMEM_BUDGET)
    tk = next(c for c in (K, 2816, 2176, 2048, 1408, 1024, 512, 256) if K % c == 0 and fits(c))
    nk = K // tk
    nj = N // tn

    if mode in ("nn", "nt"):
        a_spec = pl.BlockSpec((tm, tk), lambda s, i, j, k: (i, k))
    else:
        a_spec = pl.BlockSpec((tk, tm), lambda s, i, j, k: (k, i))
    if mode == "nt":
        b_spec = pl.BlockSpec((tn, tk), lambda s, i, j, k: (j, k))
    elif b3:
        b_spec = pl.BlockSpec((None, tk, tn), lambda s, i, j, k: (s, k, j))
    else:
        b_spec = pl.BlockSpec((tk, tn), lambda s, i, j, k: (k, s * nj + j))
    if slabs > 1:
        o_spec = pl.BlockSpec((None, tm, tn), lambda s, i, j, k: (s, i, j))
        o_shape = jax.ShapeDtypeStruct((slabs, M, N), out_dtype)
    else:
        o_spec = pl.BlockSpec((tm, tn), lambda s, i, j, k: (i, j))
        o_shape = jax.ShapeDtypeStruct((M, N), out_dtype)
    in_specs = [a_spec] * npairs + [b_spec] * npairs
    args = [a for a, _ in pairs] + [b for _, b in pairs]
    if residual is not None:
        in_specs.append(pl.BlockSpec((tm, tn), lambda s, i, j, k: (i, j)))
        args.append(residual)
    dims = _DIMS[mode]

    def body(*refs):
        a_refs, b_refs = refs[:npairs], refs[npairs:2 * npairs]
        res_ref = refs[2 * npairs] if residual is not None else None
        o_ref = refs[2 * npairs + (residual is not None)]
        k = pl.program_id(3)
        part = None
        for a_ref, b_ref in zip(a_refs, b_refs):
            d = lax.dot_general(a_ref[...].astype(BF16), b_ref[...].astype(BF16), dims, preferred_element_type=F32)
            part = d if part is None else part + d

        def finish(r):
            if res_ref is not None:
                r = r + res_ref[...]
            o_ref[...] = r.astype(out_dtype)

        if nk == 1:
            finish(part)
            return
        acc_ref = refs[-1]

        @pl.when(k == 0)
        def _():
            acc_ref[...] = part

        @pl.when((k > 0) & (k < nk - 1))
        def _():
            acc_ref[...] += part

        @pl.when(k == nk - 1)
        def _():
            finish(acc_ref[...] + part)

    return pl.pallas_call(
        body, name=name, grid=(slabs, M // tm, nj, nk), in_specs=in_specs, out_specs=o_spec, out_shape=o_shape,
        scratch_shapes=[pltpu.VMEM((tm, tn), F32)] if nk > 1 else [],
        compiler_params=_cparams(("parallel", "parallel", "parallel", "arbitrary")),
    )(*args)


def _rowmap(fn, rows, fulls, row_outs, acc_outs, name, tm, n_rows):
    nr, nf, nro = len(rows), len(fulls), len(row_outs)
    rows = [r if len(r) == 4 else (*r, n_rows // tm) for r in rows]
    in_specs = [pl.BlockSpec((tm, w), functools.partial(lambda i, cb, per: (i % per, cb), cb=cb, per=per)) for _, w, cb, per in rows]
    in_specs += [pl.BlockSpec(f.shape, lambda i: (0,) * f.ndim) for f in fulls]
    out_specs = [pl.BlockSpec((tm, w), lambda i: (i, 0)) for w, _ in row_outs]
    out_specs += [pl.BlockSpec(s, functools.partial(lambda i, n: (0,) * n, n=len(s))) for s in acc_outs]
    out_shape = [jax.ShapeDtypeStruct((n_rows, w), dt) for w, dt in row_outs]
    out_shape += [jax.ShapeDtypeStruct(s, F32) for s in acc_outs]

    def body(*refs):
        ins = [r[...] for r in refs[:nr + nf]]
        outs = fn(*ins)
        o_refs = refs[nr + nf:]
        for o_ref, val in zip(o_refs[:nro], outs[:nro]):
            o_ref[...] = val.astype(o_ref.dtype)
        if acc_outs:
            first = pl.program_id(0) == 0
            for o_ref, val in zip(o_refs[nro:], outs[nro:]):
                @pl.when(first)
                def _(o_ref=o_ref, val=val):
                    o_ref[...] = val

                @pl.when(jnp.logical_not(first))
                def _(o_ref=o_ref, val=val):
                    o_ref[...] += val

    res = pl.pallas_call(
        body, name=name, grid=(n_rows // tm,), in_specs=in_specs, out_specs=out_specs, out_shape=out_shape,
        compiler_params=_cparams(("arbitrary",) if acc_outs else ("parallel",)),
    )(*[r[0] for r in rows], *fulls)
    return res


def _rms(x, g):
    return x * lax.rsqrt(jnp.mean(x * x, axis=-1, keepdims=True) + RMS_EPS) * g


def _rmsnorm_fwd(x, g, name):
    t = x.shape[0]
    return _rowmap(lambda xv, gv: (_rms(xv, gv),), [(x, D_MODEL, 0)], [g], [(D_MODEL, BF16)], [], name, 512, t)[0]


def _rmsnorm_bwd(dh, x, g, dres, name, follow=None):
    t = x.shape[0]

    def fn(dhv, xv, drv, gv, *_):
        _, vjp = jax.vjp(_rms, xv, gv)
        dx, dg = vjp(dhv)
        return dx + drv, dx + drv, dg

    fulls = [g] if follow is None else [g, follow]
    return _rowmap(fn, [(dh, D_MODEL, 0), (x, D_MODEL, 0), (dres, D_MODEL, 0)], fulls, [(D_MODEL, F32), (D_MODEL, BF16)],
                   [(1, D_MODEL)], name, 256, t)


def _loss_fwd_bwd(y, target, name):
    t = y.shape[0]

    def fn(yv, tv):
        e = yv - tv
        return e * (1.0 / D_MODEL), e * (1.0 / D_MODEL), (0.5 / D_MODEL) * jnp.sum(e * e, keepdims=True)

    return _rowmap(fn, [(y, D_MODEL, 0), (target, D_MODEL, 0)], [], [(D_MODEL, F32), (D_MODEL, BF16)], [(1, 1)], name, 512, t)


def _ffn_up(h, wg, wu, name):
    t, n = h.shape[0], wg.shape[1]
    tm, tn = 512, 512

    def body(h_ref, wg_ref, wu_ref, act_ref, g_ref, u_ref):
        hv = h_ref[...]
        g = jnp.dot(hv, wg_ref[...], preferred_element_type=F32)
        u = jnp.dot(hv, wu_ref[...], preferred_element_type=F32)
        act_ref[...] = (g * jax.nn.sigmoid(g) * u).astype(BF16)
        g_ref[...] = g.astype(BF16)
        u_ref[...] = u.astype(BF16)

    o_spec = pl.BlockSpec((tm, tn), lambda i, j: (i, j))
    o_shape = jax.ShapeDtypeStruct((t, n), BF16)
    return pl.pallas_call(
        body, name=name, grid=(t // tm, n // tn),
        in_specs=[pl.BlockSpec((tm, D_MODEL), lambda i, j: (i, 0)), pl.BlockSpec((D_MODEL, tn), lambda i, j: (0, j)),
                  pl.BlockSpec((D_MODEL, tn), lambda i, j: (0, j))],
        out_specs=[o_spec] * 3, out_shape=[o_shape] * 3, compiler_params=_cparams(("parallel", "parallel")),
    )(h, wg, wu)


def _ffn_down_bwd(dy, wd, g, u, name):
    t, n = dy.shape[0], wd.shape[0]
    tm, tn = 512, 512

    def body(dy_ref, wd_ref, g_ref, u_ref, dg_ref, du_ref):
        dact = lax.dot_general(dy_ref[...].astype(BF16), wd_ref[...], _DIMS["nt"], preferred_element_type=F32)
        gv = g_ref[...].astype(F32)
        uv = u_ref[...].astype(F32)
        sg = jax.nn.sigmoid(gv)
        silu = gv * sg
        du_ref[...] = (dact * silu).astype(BF16)
        dg_ref[...] = (dact * uv * (sg + silu * (1.0 - sg))).astype(BF16)

    o_spec = pl.BlockSpec((tm, tn), lambda i, j: (i, j))
    o_shape = jax.ShapeDtypeStruct((t, n), BF16)
    return pl.pallas_call(
        body, name=name, grid=(t // tm, n // tn),
        in_specs=[pl.BlockSpec((tm, D_MODEL), lambda i, j: (i, 0)), pl.BlockSpec((tn, D_MODEL), lambda i, j: (j, 0)),
                  o_spec, o_spec],
        out_specs=[o_spec] * 2, out_shape=[o_shape] * 2, compiler_params=_cparams(("parallel", "parallel")),
    )(dy, wd, g, u)


def _np_group_avg(width, group=HEAD_DIM):
    i = np.arange(width)
    return ((i[:, None] // group) == (i[None, :] // group)).astype(np.float32) / group


def _np_tile_fold(width, group=HEAD_DIM):
    return ((np.arange(width)[:, None] % group) == np.arange(group)[None, :]).astype(np.float32)


def _np_group_fold(width, group=HEAD_DIM, pad=128):
    return ((np.arange(width)[:, None] // group) == np.arange(pad)[None, :]).astype(np.float32)


def _np_rope_partner(width):
    i = np.arange(width)
    partner = np.where(i % 32 < 16, i + 16, i - 16)
    return (partner[:, None] == i[None, :]).astype(np.float32)


def _np_kv_expand():
    src = np.arange(KV_WIDTH)
    dst = np.arange(GROUP_WIDTH)
    return ((src[:, None] // HEAD_DIM == dst[None, :] // (4 * HEAD_DIM)) & (src[:, None] % HEAD_DIM == dst[None, :] % HEAD_DIM)).astype(np.float32)


def _np_rope_tables(n_heads):
    t = np.arange(SEQ)
    pos = {0: (t // GRID_W).astype(np.float32), 1: (t % GRID_W).astype(np.float32)}
    freqs = (ROPE_THETA ** (-np.arange(16, dtype=np.float32) / 16)).astype(np.float32)
    cos_parts, sin_parts = [], []
    for axis in (0, 1):
        ang = pos[axis][:, None] * freqs[None, :]
        c, s = np.cos(ang).astype(np.float32), np.sin(ang).astype(np.float32)
        cos_parts += [c, c]
        sin_parts += [-s, s]
    cos = np.concatenate(cos_parts, axis=1)
    sin = np.concatenate(sin_parts, axis=1)
    return np.tile(cos, (1, n_heads)), np.tile(sin, (1, n_heads))


def _np_t5_buckets(rel):
    nb = REL_BUCKETS // 2
    max_exact = nb // 2
    ret = np.where(rel > 0, nb, 0)
    n = np.abs(rel)
    nf = np.maximum(n, 1).astype(np.float32)
    large = max_exact + (np.log(nf / max_exact) / math.log(REL_MAX_DIST / max_exact) * (nb - max_exact)).astype(np.int32)
    large = np.minimum(large, nb - 1)
    return (ret + np.where(n < max_exact, n, large)).astype(np.int32)


DIL_QB = 128
DIL_WIN = DIL_QB + 2 * DIL_HALF


def _np_dil_buckets(dil):
    off = np.arange(DIL_WIN)[None, :] - DIL_HALF - np.arange(DIL_QB)[:, None]
    return _np_t5_buckets(off * dil)


def _dil_live_buckets(dil):
    off = np.arange(-DIL_HALF, DIL_HALF + 1)
    return sorted(set(_np_t5_buckets(off * dil).tolist()))


def _head_stat(x, mavg):
    return jnp.dot(x, mavg, precision=HI, preferred_element_type=F32)


def _gelu(x):
    return 0.5 * x * (1.0 + jnp.tanh(math.sqrt(2.0 / math.pi) * (x + 0.044715 * (x * x * x))))


def _sgu_pre(u_pre, v_pre, mavg):
    v = _gelu(v_pre)
    xc = v - _head_stat(v, mavg)
    vn = xc * lax.rsqrt(_head_stat(xc * xc, mavg) + LN_EPS)
    return _gelu(u_pre), vn


def _sgu_mix(w_ref, vnb, bm):
    lane_group = lax.broadcasted_iota(jnp.int32, (1, GROUP_WIDTH), 1) // HEAD_DIM
    mixed = bm
    for g in range(N_HEADS):
        r = jnp.dot(w_ref[g], vnb, preferred_element_type=F32)
        mixed = mixed + jnp.where(lane_group == g, r, 0.0)
    return mixed


SGU_TM = 512


def _sgu_fwd(z, w_s, bm, name):
    t = z.shape[0]
    mavg = jnp.asarray(_np_group_avg(GROUP_WIDTH))

    def body(u_ref, v_ref, w_ref, bm_ref, mavg_ref, y_ref):
        for c in range(SGU_TM // SGU_CHUNK):
            rows = pl.ds(c * SGU_CHUNK, SGU_CHUNK)
            u, vn = _sgu_pre(u_ref[rows, :], v_ref[rows, :], mavg_ref[...])
            y_ref[rows, :] = u * _sgu_mix(w_ref, vn.astype(BF16), bm_ref[...])

    full = lambda a: pl.BlockSpec(a.shape, lambda i: (0,) * a.ndim)
    return pl.pallas_call(
        body, name=name, grid=(t // SGU_TM,),
        in_specs=[pl.BlockSpec((SGU_TM, GROUP_WIDTH), lambda i: (i, 0)), pl.BlockSpec((SGU_TM, GROUP_WIDTH), lambda i: (i, 1)),
                  full(w_s), full(bm), full(mavg)],
        out_specs=pl.BlockSpec((SGU_TM, GROUP_WIDTH), lambda i: (i, 0)),
        out_shape=jax.ShapeDtypeStruct((t, GROUP_WIDTH), F32), compiler_params=_cparams(("parallel",)),
    )(z, z, w_s, bm, mavg)


def _sgu_bwd(z, dy, w_s, w_s_t, bm, name):
    t = z.shape[0]
    mavg = jnp.asarray(_np_group_avg(GROUP_WIDTH))
    gfold = jnp.asarray(_np_group_fold(GROUP_WIDTH))

    def body(u_ref, v_ref, dy_ref, w_ref, wt_ref, bm_ref, mavg_ref, gfold_ref, du_ref, dv_ref, dw_ref, dbs_ref, dbm_ref):
        @pl.when(pl.program_id(0) == 0)
        def _():
            dw_ref[...] = jnp.zeros_like(dw_ref)
            dbm_ref[...] = jnp.zeros_like(dbm_ref)

        lane_group = lax.broadcasted_iota(jnp.int32, (1, GROUP_WIDTH), 1) // HEAD_DIM
        for c in range(SGU_TM // SGU_CHUNK):
            rows = pl.ds(c * SGU_CHUNK, SGU_CHUNK)
            (u, vn), pre_vjp = jax.vjp(functools.partial(_sgu_pre, mavg=mavg_ref[...]), u_ref[rows, :], v_ref[rows, :])
            vnb = vn.astype(BF16)
            mixed = _sgu_mix(w_ref, vnb, bm_ref[...])
            dyv = dy_ref[rows, :]
            dmixed = dyv * u
            dbm_ref[...] += dmixed
            dvn = jnp.zeros_like(vn)
            for g in range(N_HEADS):
                dm_g = jnp.where(lane_group == g, dmixed, 0.0).astype(BF16)
                dw_ref[g] += lax.dot_general(dm_g, vnb, _DIMS["nt"], preferred_element_type=F32)
                dvn = dvn + jnp.dot(wt_ref[g], dm_g, preferred_element_type=F32)
            du_pre, dv_pre = pre_vjp((dyv * mixed, dvn))
            du_ref[rows, :] = du_pre
            dv_ref[rows, :] = dv_pre

        @pl.when(pl.program_id(0) == t // SGU_TM - 1)
        def _():
            dbs_ref[...] = jnp.dot(dbm_ref[...], gfold_ref[...], precision=HI, preferred_element_type=F32)

    full = lambda a: pl.BlockSpec(a.shape, lambda i: (0,) * a.ndim)
    row = pl.BlockSpec((SGU_TM, GROUP_WIDTH), lambda i: (i, 0))
    return pl.pallas_call(
        body, name=name, grid=(t // SGU_TM,),
        in_specs=[row, pl.BlockSpec((SGU_TM, GROUP_WIDTH), lambda i: (i, 1)), row, full(w_s), full(w_s_t), full(bm), full(mavg),
                  full(gfold)],
        out_specs=[row, row, pl.BlockSpec((N_HEADS, SGU_CHUNK, SGU_CHUNK), lambda i: (0, 0, 0)),
                   pl.BlockSpec((SGU_CHUNK, 128), lambda i: (0, 0))],
        out_shape=[jax.ShapeDtypeStruct((t, GROUP_WIDTH), F32)] * 2 + [jax.ShapeDtypeStruct((N_HEADS, SGU_CHUNK, SGU_CHUNK), F32),
                                                                      jax.ShapeDtypeStruct((SGU_CHUNK, 128), F32)],
        scratch_shapes=[pltpu.VMEM((SGU_CHUNK, GROUP_WIDTH), F32)],
        compiler_params=_cparams(("arbitrary",)),
    )(z, z, dy, w_s, w_s_t, bm, mavg, gfold)


def _pair_softmax(q2, k2, hh, bias, valid):
    head = (lax.broadcasted_iota(jnp.int32, (1, 2 * HEAD_DIM), 1) // HEAD_DIM) == hh
    qm = jnp.where(head, q2, jnp.zeros_like(q2))
    s = lax.dot_general(qm, k2, _DIMS["nt"], preferred_element_type=F32)
    if bias is not None:
        s = s + bias
    if valid is not None:
        s = jnp.where(valid, s, -1e30)
    m = jnp.max(s, axis=-1, keepdims=True)
    e = jnp.exp(s - m)
    l = jnp.sum(e, axis=-1, keepdims=True)
    return head, qm, e / l, m + jnp.log(l)


def _attn_pair_fwd(q2, k2, v2, biases, valid):
    o2 = lse2 = None
    for hh in range(2):
        head, _, p, lse = _pair_softmax(q2, k2, hh, None if biases is None else biases[hh], valid)
        oh = jnp.dot(p.astype(BF16), v2, preferred_element_type=F32)
        o_h = jnp.where(head, oh, 0.0)
        l_h = jnp.where(head, lse, 0.0)
        o2 = o_h if o2 is None else o2 + o_h
        lse2 = l_h if lse2 is None else lse2 + l_h
    return o2, lse2


def _attn_pair_bwd(q2, k2, v2, biases, valid, do2, dlse2):
    dq2 = dk2 = dv2 = None
    ds_heads = []
    for hh in range(2):
        head, qm, p, _ = _pair_softmax(q2, k2, hh, None if biases is None else biases[hh], valid)
        dom = jnp.where(head, do2, 0.0).astype(BF16)
        dp = lax.dot_general(dom, v2, _DIMS["nt"], preferred_element_type=F32)
        delta = jnp.sum(dp * p, axis=-1, keepdims=True)
        if dlse2 is not None:
            delta = delta - jnp.sum(jnp.where(head, dlse2, 0.0), axis=-1, keepdims=True)
        ds = p * (dp - delta)
        dsb = ds.astype(BF16)
        dq_h = jnp.where(head, jnp.dot(dsb, k2, preferred_element_type=F32), 0.0)
        dk_h = lax.dot_general(dsb, qm, _DIMS["tn"], preferred_element_type=F32)
        dv_h = lax.dot_general(p.astype(BF16), dom, _DIMS["tn"], preferred_element_type=F32)
        dq2 = dq_h if dq2 is None else dq2 + dq_h
        dk2 = dk_h if dk2 is None else dk2 + dk_h
        dv2 = dv_h if dv2 is None else dv2 + dv_h
        ds_heads.append(ds)
    return dq2, dk2, dv2, ds_heads


def _dil_valid(r0, length):
    row = lax.broadcasted_iota(jnp.int32, (DIL_QB, DIL_WIN), 0)
    col = lax.broadcasted_iota(jnp.int32, (DIL_QB, DIL_WIN), 1)
    off = col - DIL_HALF - row
    kpos = r0 - DIL_HALF + col
    return (jnp.abs(off) <= DIL_HALF) & (kpos >= 0) & (kpos < length)


def _dil_build_bias(tab_ref, bkt_ref, bias_ref, dil):
    bkt = bkt_ref[...]
    for h in range(N_HEADS):
        acc = jnp.zeros((DIL_QB, DIL_WIN), F32)
        for b in _dil_live_buckets(dil):
            acc = jnp.where(bkt == b, tab_ref[b, h], acc)
        bias_ref[h] = acc


def _dil_fill_pad(pad_ref, src_ref, length):
    zeros = jnp.zeros((DIL_HALF, GROUP_WIDTH), pad_ref.dtype)
    pad_ref[pl.ds(0, DIL_HALF), :] = zeros
    pad_ref[pl.ds(DIL_HALF + length, DIL_HALF), :] = zeros
    pad_ref[pl.ds(DIL_HALF, length), :] = src_ref[...]


def _dil_specs(bsz, length, dil):
    view = lambda a: a.reshape(bsz, length, dil * GROUP_WIDTH)
    blk = pl.BlockSpec((None, DIL_QB, GROUP_WIDTH), lambda b, rho, i: (b, i, rho))
    seq = pl.BlockSpec((None, length, GROUP_WIDTH), lambda b, rho, i: (b, 0, rho))
    return view, blk, seq


def _dil_fwd(qb, kb, vb, table, dil, name):
    bsz = qb.shape[0]
    length = SEQ // dil
    bkt = jnp.asarray(_np_dil_buckets(dil))
    view, blk, seq = _dil_specs(bsz, length, dil)

    def body(tab_ref, bkt_ref, q_ref, k_ref, v_ref, o_ref, lse_ref, kpad, vpad, bias_ref):
        i = pl.program_id(2)

        @pl.when((pl.program_id(0) == 0) & (pl.program_id(1) == 0) & (i == 0))
        def _():
            _dil_build_bias(tab_ref, bkt_ref, bias_ref, dil)

        @pl.when(i == 0)
        def _():
            _dil_fill_pad(kpad, k_ref, length)
            _dil_fill_pad(vpad, v_ref, length)

        r0 = pl.multiple_of(i * DIL_QB, DIL_QB)
        valid = _dil_valid(r0, length)
        for m in range(N_HEADS // 2):
            lanes = pl.ds(m * 128, 128)
            o2, lse2 = _attn_pair_fwd(q_ref[:, lanes], kpad[pl.ds(r0, DIL_WIN), lanes], vpad[pl.ds(r0, DIL_WIN), lanes],
                                      (bias_ref[2 * m], bias_ref[2 * m + 1]), valid)
            o_ref[:, lanes] = o2
            lse_ref[:, lanes] = lse2

    out = jax.ShapeDtypeStruct((bsz, length, dil * GROUP_WIDTH), F32)
    o, lse = pl.pallas_call(
        body, name=name, grid=(bsz, dil, length // DIL_QB),
        in_specs=[pl.BlockSpec(memory_space=pltpu.SMEM), pl.BlockSpec(bkt.shape, lambda b, rho, i: (0, 0)), blk, seq, seq],
        out_specs=[blk, blk], out_shape=[out, out],
        scratch_shapes=[pltpu.VMEM((length + 2 * DIL_HALF, GROUP_WIDTH), BF16), pltpu.VMEM((length + 2 * DIL_HALF, GROUP_WIDTH), BF16),
                        pltpu.VMEM((N_HEADS, DIL_QB, DIL_WIN), F32)],
        compiler_params=_cparams(("arbitrary", "arbitrary", "arbitrary")),
    )(table, bkt, view(qb), view(kb), view(vb))
    return o.reshape(bsz, SEQ, GROUP_WIDTH), lse.reshape(bsz, SEQ, GROUP_WIDTH)


def _dil_bwd(qb, kb, vb, do, dlse, table, dil, name):
    bsz = qb.shape[0]
    length = SEQ // dil
    nqb = length // DIL_QB
    bkt = jnp.asarray(_np_dil_buckets(dil))
    view, blk, seq = _dil_specs(bsz, length, dil)

    def body(tab_ref, bkt_ref, q_ref, k_ref, v_ref, do_ref, dlse_ref, dq_ref, dk_ref, dv_ref, dsc_ref, kpad, vpad, bias_ref):
        i = pl.program_id(2)

        @pl.when((pl.program_id(0) == 0) & (pl.program_id(1) == 0) & (i == 0))
        def _():
            _dil_build_bias(tab_ref, bkt_ref, bias_ref, dil)
            dsc_ref[...] = jnp.zeros_like(dsc_ref)

        @pl.when(i == 0)
        def _():
            _dil_fill_pad(kpad, k_ref, length)
            _dil_fill_pad(vpad, v_ref, length)
            dk_ref[...] = jnp.zeros_like(dk_ref)
            dv_ref[...] = jnp.zeros_like(dv_ref)

        r0 = pl.multiple_of(i * DIL_QB, DIL_QB)
        valid = _dil_valid(r0, length)
        for m in range(N_HEADS // 2):
            lanes = pl.ds(m * 128, 128)
            dq2, dk2, dv2, ds_heads = _attn_pair_bwd(
                q_ref[:, lanes], kpad[pl.ds(r0, DIL_WIN), lanes], vpad[pl.ds(r0, DIL_WIN), lanes],
                (bias_ref[2 * m], bias_ref[2 * m + 1]), valid, do_ref[:, lanes], dlse_ref[:, lanes])
            dq_ref[:, lanes] = dq2
            dsc_ref[2 * m] += ds_heads[0]
            dsc_ref[2 * m + 1] += ds_heads[1]
            for first, size, live in ((0, DIL_HALF, i >= 1), (DIL_HALF, DIL_QB, None), (DIL_HALF + DIL_QB, DIL_HALF, i <= nqb - 2)):
                def add(first=first, size=size, dk2=dk2, dv2=dv2, lanes=lanes):
                    rows = pl.ds(pl.multiple_of(r0 - DIL_HALF + first, DIL_HALF), size)
                    dk_ref[rows, lanes] += dk2[first:first + size]
                    dv_ref[rows, lanes] += dv2[first:first + size]
                if live is None:
                    add()
                else:
                    pl.when(live)(add)

    out = jax.ShapeDtypeStruct((bsz, length, dil * GROUP_WIDTH), F32)
    dsc_shape = (N_HEADS, DIL_QB, DIL_WIN)
    dq, dk, dv, dsc = pl.pallas_call(
        body, name=name, grid=(bsz, dil, nqb),
        in_specs=[pl.BlockSpec(memory_space=pltpu.SMEM), pl.BlockSpec(bkt.shape, lambda b, rho, i: (0, 0)), blk, seq, seq, blk, blk],
        out_specs=[blk, seq, seq, pl.BlockSpec(dsc_shape, lambda b, rho, i: (0, 0, 0))],
        out_shape=[out, out, out, jax.ShapeDtypeStruct(dsc_shape, F32)],
        scratch_shapes=[pltpu.VMEM((length + 2 * DIL_HALF, GROUP_WIDTH), BF16), pltpu.VMEM((length + 2 * DIL_HALF, GROUP_WIDTH), BF16),
                        pltpu.VMEM(dsc_shape, F32)],
        compiler_params=_cparams(("arbitrary", "arbitrary", "arbitrary")),
    )(table, bkt, view(qb), view(kb), view(vb), view(do), view(dlse))
    shp = (bsz, SEQ, GROUP_WIDTH)
    return dq.reshape(shp), dk.reshape(shp), dv.reshape(shp), dsc


def _headnorm(x, g, mavg):
    return x * lax.rsqrt(_head_stat(x * x, mavg) + RMS_EPS) * g


def _fold_gain(dg_full, fold):
    return jnp.dot(jnp.broadcast_to(dg_full, (8, dg_full.shape[1])), fold, precision=HI, preferred_element_type=F32)


def _bprep_fn(qp, kp, gq, gk, mavg):
    return _headnorm(qp, gq, mavg) * (HEAD_DIM ** -0.5), _headnorm(kp, gk, mavg)


def _bprep_fwd(z, gq, gk, name):
    mavg = jnp.asarray(_np_group_avg(GROUP_WIDTH))

    def fn(qp, kp, vp, gqv, gkv, mv):
        qb, kb = _bprep_fn(qp, kp, gqv, gkv, mv)
        return qb, kb, vp

    w = GROUP_WIDTH
    return _rowmap(fn, [(z, w, 2), (z, w, 3), (z, w, 4)], [gq, gk, mavg], [(w, BF16)] * 3, [], name, 512, z.shape[0])


def _bprep_bwd(z, dqs, dks, dvs, gq, gk, name):
    mavg = jnp.asarray(_np_group_avg(GROUP_WIDTH))
    fold = jnp.asarray(_np_tile_fold(GROUP_WIDTH))

    def fn(qp, kp, dq0, dq1, dq2, dk0, dk1, dk2, dv0, dv1, dv2, gqv, gkv, mv, fv):
        _, vjp = jax.vjp(functools.partial(_bprep_fn, mavg=mv), qp, kp, gqv, gkv)
        dqp, dkp, dgq, dgk = vjp((dq0 + dq1 + dq2, dk0 + dk1 + dk2))
        return dqp, dkp, dv0 + dv1 + dv2, _fold_gain(dgq, fv), _fold_gain(dgk, fv)

    w = GROUP_WIDTH
    rows = [(z, w, 2), (z, w, 3)] + [(a, w, 0) for a in (*dqs, *dks, *dvs)]
    return _rowmap(fn, rows, [gq, gk, mavg, fold], [(w, F32)] * 3, [(8, HEAD_DIM)] * 2, name, 256, z.shape[0])


def _mixture_fn(o0, o1, o2, l0, l1, l2):
    m = lax.stop_gradient(jnp.maximum(jnp.maximum(l0, l1), l2))
    e0, e1, e2 = jnp.exp(l0 - m), jnp.exp(l1 - m), jnp.exp(l2 - m)
    return (e0 * o0 + e1 * o1 + e2 * o2) / (e0 + e1 + e2)


def _mixture_fwd(os_, ls_, name):
    w = GROUP_WIDTH
    rows = [(a, w, 0) for a in (*os_, *ls_)]
    return _rowmap(lambda *v: (_mixture_fn(*v),), rows, [], [(w, F32)], [], name, 512, os_[0].shape[0])[0]


def _mixture_bwd(os_, ls_, dy, name):
    w = GROUP_WIDTH

    def fn(*v):
        _, vjp = jax.vjp(_mixture_fn, *v[:6])
        return vjp(v[6])

    rows = [(a, w, 0) for a in (*os_, *ls_, dy)]
    return _rowmap(fn, rows, [], [(w, F32)] * 6, [], name, 512, dy.shape[0])


def _relbias_fold(dscs, name):
    bkts = [jnp.asarray(_np_dil_buckets(dil)) for _, dil in DIL_PATTERNS]
    npat = len(DIL_PATTERNS)

    def body(*refs):
        bkt_refs, d_refs, o_ref = refs[:npat], refs[npat:-1], refs[-1]
        row = lax.broadcasted_iota(jnp.int32, (REL_BUCKETS, 128), 0)
        lane = lax.broadcasted_iota(jnp.int32, (REL_BUCKETS, 128), 1)
        out = jnp.zeros((REL_BUCKETS, 128), F32)
        for p, (_, dil) in enumerate(DIL_PATTERNS):
            bkt = bkt_refs[p][...]
            for h in range(N_HEADS):
                d = d_refs[2 * p][h] + d_refs[2 * p + 1][h]
                for b in _dil_live_buckets(dil):
                    val = jnp.sum(jnp.where(bkt == b, d, 0.0), keepdims=True)
                    out = out + jnp.where((row == b) & (lane == h), val, 0.0)
        o_ref[...] = out

    return pl.pallas_call(
        body, name=name, out_shape=jax.ShapeDtypeStruct((REL_BUCKETS, 128), F32), compiler_params=_cparams(),
    )(*bkts, *dscs)


DPREP_TM = 512


def _dprep_fn(qp, kp, vp, gq, gk, cq, sq, ck, sk, mavg_q, mavg_k, perm_q, perm_k, expand):
    rot = lambda x, perm: jnp.dot(x, perm, precision=HI, preferred_element_type=F32)
    qn = _headnorm(qp, gq, mavg_q)
    kn = _headnorm(kp, gk, mavg_k)
    qr = (qn * cq + rot(qn, perm_q) * sq) * (HEAD_DIM ** -0.5)
    kr = kn * ck + rot(kn, perm_k) * sk
    return qr, rot(kr, expand), rot(vp, expand)


def _dprep_consts():
    cq, sq = _np_rope_tables(N_HEADS)
    ck, sk = _np_rope_tables(KV_WIDTH // HEAD_DIM)
    tables = [jnp.asarray(a) for a in (cq, sq, ck, sk)]
    mats = [jnp.asarray(a) for a in (_np_group_avg(GROUP_WIDTH), _np_group_avg(KV_WIDTH), _np_rope_partner(GROUP_WIDTH),
                                      _np_rope_partner(KV_WIDTH), _np_kv_expand())]
    per = SEQ // DPREP_TM
    w, kw = GROUP_WIDTH, KV_WIDTH
    table_rows = [(tables[0], w, 0, per), (tables[1], w, 0, per), (tables[2], kw, 0, per), (tables[3], kw, 0, per)]
    return table_rows, mats


def _dprep_fwd(z, gq, gk, name):
    table_rows, mats = _dprep_consts()
    w, kw = GROUP_WIDTH, KV_WIDTH

    def fn(qp, kp, vp, cq, sq, ck, sk, gqv, gkv, *m):
        return _dprep_fn(qp, kp, vp, gqv, gkv, cq, sq, ck, sk, *m)

    return _rowmap(fn, [(z, w, 7), (z, kw, 32), (z, kw, 33)] + table_rows, [gq, gk] + mats, [(w, BF16)] * 3, [], name,
                   DPREP_TM, z.shape[0])


def _dprep_bwd(z, dq, dkx, dvx, gq, gk, name):
    table_rows, mats = _dprep_consts()
    fold_q = jnp.asarray(_np_tile_fold(GROUP_WIDTH))
    fold_k = jnp.asarray(_np_tile_fold(KV_WIDTH))
    w, kw = GROUP_WIDTH, KV_WIDTH

    def fn(qp, kp, vp, dqv, dkv, dvv, cq, sq, ck, sk, gqv, gkv, fq, fk, *m):
        f = lambda a, b, c, d, e: _dprep_fn(a, b, c, d, e, cq, sq, ck, sk, *m)
        _, vjp = jax.vjp(f, qp, kp, vp, gqv, gkv)
        dqp, dkp, dvp, dgq, dgk = vjp((dqv, dkv, dvv))
        return dqp, dkp, dvp, _fold_gain(dgq, fq), _fold_gain(dgk, fk)

    return _rowmap(fn, [(z, w, 7), (z, kw, 32), (z, kw, 33), (dq, w, 0), (dkx, w, 0), (dvx, w, 0)] + table_rows,
                   [gq, gk, fold_q, fold_k] + mats, [(w, F32), (kw, F32), (kw, F32)], [(8, HEAD_DIM)] * 2, name,
                   DPREP_TM, z.shape[0])


GQA_QB = 256


def _gqa_fwd(q, kx, vx, name):
    bsz = q.shape[0]
    blk = pl.BlockSpec((None, GQA_QB, GROUP_WIDTH), lambda b, i: (b, i, 0))
    seq = pl.BlockSpec((None, SEQ, GROUP_WIDTH), lambda b, i: (b, 0, 0))

    def body(q_ref, k_ref, v_ref, o_ref):
        for m in range(N_HEADS // 2):
            lanes = pl.ds(m * 128, 128)
            o_ref[:, lanes] = _attn_pair_fwd(q_ref[:, lanes], k_ref[:, lanes], v_ref[:, lanes], None, None)[0]

    return pl.pallas_call(
        body, name=name, grid=(bsz, SEQ // GQA_QB), in_specs=[blk, seq, seq], out_specs=blk,
        out_shape=jax.ShapeDtypeStruct((bsz, SEQ, GROUP_WIDTH), F32), compiler_params=_cparams(("parallel", "parallel")),
    )(q, kx, vx)


def _gqa_bwd(q, kx, vx, do, name):
    bsz = q.shape[0]
    blk = pl.BlockSpec((None, GQA_QB, GROUP_WIDTH), lambda b, i: (b, i, 0))
    seq = pl.BlockSpec((None, SEQ, GROUP_WIDTH), lambda b, i: (b, 0, 0))

    def body(q_ref, k_ref, v_ref, do_ref, dq_ref, dk_ref, dv_ref):
        @pl.when(pl.program_id(1) == 0)
        def _():
            dk_ref[...] = jnp.zeros_like(dk_ref)
            dv_ref[...] = jnp.zeros_like(dv_ref)

        for m in range(N_HEADS // 2):
            lanes = pl.ds(m * 128, 128)
            dq2, dk2, dv2, _ = _attn_pair_bwd(q_ref[:, lanes], k_ref[:, lanes], v_ref[:, lanes], None, None, do_ref[:, lanes], None)
            dq_ref[:, lanes] = dq2
            dk_ref[:, lanes] += dk2
            dv_ref[:, lanes] += dv2

    out = jax.ShapeDtypeStruct((bsz, SEQ, GROUP_WIDTH), F32)
    return pl.pallas_call(
        body, name=name, grid=(bsz, SEQ // GQA_QB), in_specs=[blk, seq, seq, blk], out_specs=[blk, seq, seq],
        out_shape=[out, out, out], compiler_params=_cparams(("parallel", "arbitrary")),
    )(q, kx, vx, do)


CONV_TILE = 64
CONV_LEAD = 16
CONV_WINDOW = CONV_TILE + 32


def _glu(a, g):
    return a * jax.nn.sigmoid(g)


def _conv_post(c, b, ln_g, ln_b):
    x = c + b
    xc = x - jnp.mean(x, axis=-1, keepdims=True)
    y = xc * lax.rsqrt(jnp.mean(xc * xc, axis=-1, keepdims=True) + LN_EPS) * ln_g + ln_b
    return y * jax.nn.sigmoid(y)


def _conv_shifted(win, offset):
    return pltpu.roll(win, CONV_WINDOW - offset, 0)[:CONV_TILE]


def _conv_fill(pad_ref, value_of_tile):
    zeros = jnp.zeros((CONV_LEAD, GROUP_WIDTH), F32)
    pad_ref[pl.ds(0, CONV_LEAD), :] = zeros
    pad_ref[pl.ds(CONV_LEAD + SEQ, CONV_LEAD), :] = zeros

    def step(t, carry):
        r0 = pl.multiple_of(t * CONV_TILE, CONV_TILE)
        pad_ref[pl.ds(CONV_LEAD + r0, CONV_TILE), :] = value_of_tile(r0)
        return carry

    lax.fori_loop(0, SEQ // CONV_TILE, step, 0)


def _conv_tile(pad_ref, w_ref, r0, flip):
    win = pad_ref[pl.ds(r0, CONV_WINDOW), :]
    acc = jnp.zeros((CONV_TILE, GROUP_WIDTH), F32)
    for k in range(CONV_WIDTH):
        offset = (CONV_WIDTH - k) if flip else (k + 1)
        acc = acc + w_ref[pl.ds(k, 1), :] * _conv_shifted(win, offset)
    return acc


def _conv_fwd(z3, w, b, ln_g, ln_b, name):
    bsz = z3.shape[0]
    seq = lambda cb: pl.BlockSpec((None, SEQ, GROUP_WIDTH), functools.partial(lambda i, cb: (i, 0, cb), cb=cb))
    full = lambda a: pl.BlockSpec(a.shape, lambda i: (0,) * a.ndim)

    def body(a_ref, g_ref, w_ref, b_ref, lg_ref, lb_ref, y_ref, pad_ref):
        _conv_fill(pad_ref, lambda r0: _glu(a_ref[pl.ds(r0, CONV_TILE), :], g_ref[pl.ds(r0, CONV_TILE), :]))

        def step(t, carry):
            r0 = pl.multiple_of(t * CONV_TILE, CONV_TILE)
            y_ref[pl.ds(r0, CONV_TILE), :] = _conv_post(_conv_tile(pad_ref, w_ref, r0, False), b_ref[...], lg_ref[...], lb_ref[...])
            return carry

        lax.fori_loop(0, SEQ // CONV_TILE, step, 0)

    return pl.pallas_call(
        body, name=name, grid=(bsz,), in_specs=[seq(5), seq(6), full(w), full(b), full(ln_g), full(ln_b)], out_specs=seq(0),
        out_shape=jax.ShapeDtypeStruct((bsz, SEQ, GROUP_WIDTH), F32),
        scratch_shapes=[pltpu.VMEM((SEQ + 2 * CONV_LEAD, GROUP_WIDTH), F32)], compiler_params=_cparams(("parallel",)),
    )(z3, z3, w, b, ln_g, ln_b)


def _conv_bwd(z3, dy, w, b, ln_g, ln_b, name):
    bsz = z3.shape[0]
    seq = lambda cb: pl.BlockSpec((None, SEQ, GROUP_WIDTH), functools.partial(lambda i, cb: (i, 0, cb), cb=cb))
    full = lambda a: pl.BlockSpec(a.shape, lambda i: (0,) * a.ndim)
    vec = pl.BlockSpec((1, GROUP_WIDTH), lambda i: (0, 0))

    def body(a_ref, g_ref, dy_ref, w_ref, b_ref, lg_ref, lb_ref, da_ref, dg_ref, dw_ref, db_ref, dlg_ref, dlb_ref, hpad, dpad, dw8):
        @pl.when(pl.program_id(0) == 0)
        def _():
            dw8[...] = jnp.zeros_like(dw8)
            db_ref[...] = jnp.zeros_like(db_ref)
            dlg_ref[...] = jnp.zeros_like(dlg_ref)
            dlb_ref[...] = jnp.zeros_like(dlb_ref)

        _conv_fill(hpad, lambda r0: _glu(a_ref[pl.ds(r0, CONV_TILE), :], g_ref[pl.ds(r0, CONV_TILE), :]))
        zeros = jnp.zeros((CONV_LEAD, GROUP_WIDTH), F32)
        dpad[pl.ds(0, CONV_LEAD), :] = zeros
        dpad[pl.ds(CONV_LEAD + SEQ, CONV_LEAD), :] = zeros

        def through_post(t, carry):
            r0 = pl.multiple_of(t * CONV_TILE, CONV_TILE)
            conv = _conv_tile(hpad, w_ref, r0, False)
            _, vjp = jax.vjp(_conv_post, conv, b_ref[...], lg_ref[...], lb_ref[...])
            dconv, db, dlg, dlb = vjp(dy_ref[pl.ds(r0, CONV_TILE), :])
            db_ref[...] += db
            dlg_ref[...] += dlg
            dlb_ref[...] += dlb
            dpad[pl.ds(CONV_LEAD + r0, CONV_TILE), :] = dconv
            win = hpad[pl.ds(r0, CONV_WINDOW), :]
            for k in range(CONV_WIDTH):
                prod = dconv * _conv_shifted(win, k + 1)
                part = prod[0:8]
                for j in range(1, CONV_TILE // 8):
                    part = part + prod[8 * j:8 * j + 8]
                dw8[k] += part
            return carry

        lax.fori_loop(0, SEQ // CONV_TILE, through_post, 0)

        def through_glu(t, carry):
            r0 = pl.multiple_of(t * CONV_TILE, CONV_TILE)
            dh = _conv_tile(dpad, w_ref, r0, True)
            rows = pl.ds(r0, CONV_TILE)
            _, vjp = jax.vjp(_glu, a_ref[rows, :], g_ref[rows, :])
            da, dg = vjp(dh)
            da_ref[rows, :] = da
            dg_ref[rows, :] = dg
            return carry

        lax.fori_loop(0, SEQ // CONV_TILE, through_glu, 0)
        dw_ref[...] = jnp.sum(dw8[...], axis=1)

    out = jax.ShapeDtypeStruct((bsz, SEQ, GROUP_WIDTH), F32)
    v = jax.ShapeDtypeStruct((1, GROUP_WIDTH), F32)
    return pl.pallas_call(
        body, name=name, grid=(bsz,), in_specs=[seq(5), seq(6), seq(0), full(w), full(b), full(ln_g), full(ln_b)],
        out_specs=[seq(0), seq(0), pl.BlockSpec((CONV_WIDTH, GROUP_WIDTH), lambda i: (0, 0)), vec, vec, vec],
        out_shape=[out, out, jax.ShapeDtypeStruct((CONV_WIDTH, GROUP_WIDTH), F32), v, v, v],
        scratch_shapes=[pltpu.VMEM((SEQ + 2 * CONV_LEAD, GROUP_WIDTH), F32), pltpu.VMEM((SEQ + 2 * CONV_LEAD, GROUP_WIDTH), F32),
                        pltpu.VMEM((CONV_WIDTH, 8, GROUP_WIDTH), F32)],
        compiler_params=_cparams(("arbitrary",)),
    )(z3, z3, dy, w, b, ln_g, ln_b)


def _mixnorm_fwd(ys, gains, name):
    w = GROUP_WIDTH

    def fn(*v):
        return (jnp.concatenate([_rms(v[i], v[4 + i]) for i in range(4)], axis=-1),)

    return _rowmap(fn, [(y, w, 0) for y in ys], list(gains), [(4 * w, BF16)], [], name, 512, ys[0].shape[0])[0]


def _mixnorm_bwd(dyn, ys, gains, name, follow=None):
    w = GROUP_WIDTH
    gains = list(gains) if follow is None else [*gains, follow]

    def fn(*v):
        dys, dgs = [], []
        for i in range(4):
            _, vjp = jax.vjp(_rms, v[4 + i], v[8 + i])
            dy, dg = vjp(v[i])
            dys.append(dy)
            dgs.append(dg)
        return (*dys, *dgs)

    rows = [(dyn, w, i) for i in range(4)] + [(y, w, 0) for y in ys]
    return _rowmap(fn, rows, list(gains), [(w, F32)] * 4, [(1, w)] * 4, name, 512, dyn.shape[0])


def _adamw_fn(w, g, m, v):
    m = ADAM_B1 * m + (1.0 - ADAM_B1) * g
    v = ADAM_B2 * v + (1.0 - ADAM_B2) * (g * g)
    m_hat = m / (1.0 - ADAM_B1 ** ADAM_STEP)
    v_hat = v / (1.0 - ADAM_B2 ** ADAM_STEP)
    delta = -ADAM_LR * (m_hat / (jnp.sqrt(v_hat) + ADAM_EPS) + ADAM_WD * w)
    return delta, m, v


def _adamw(w, g, m, v, name):
    r, c = w.shape
    tm = _pick(r, (256, 128, 64, 32, 16, 8))
    return _rowmap(_adamw_fn, [(a, c, 0) for a in (w, g, m, v)], [], [(c, F32)] * 3, [], name, tm, r)


def _layer_params(l, small, big):
    tile_row = lambda g, n: jnp.tile(g, n)[None, :]
    row = lambda g: g[None, :]
    w_s = small["sgu_w"][l].astype(BF16)
    return dict(
        norm1_g=row(small["norm1_g"][l]), norm2_g=row(small["norm2_g"][l]),
        w_s=w_s, w_s_t=jnp.swapaxes(w_s, 1, 2), bm=jnp.repeat(small["sgu_b"][l].T, HEAD_DIM, axis=1),
        gq_dil=tile_row(small["dil_qn_g"][l], N_HEADS), gk_dil=tile_row(small["dil_kn_g"][l], N_HEADS),
        conv_w=small["conv_w"][l], conv_b=row(small["conv_b"][l]), conv_ln_g=row(small["conv_ln_g"][l]),
        conv_ln_b=row(small["conv_ln_b"][l]),
        gq_gqa=tile_row(small["gqa_qn_g"][l], N_HEADS), gk_gqa=tile_row(small["gqa_kn_g"][l], KV_WIDTH // HEAD_DIM),
        mix_g=[row(small["mix_norm_g"][l][i * GROUP_WIDTH:(i + 1) * GROUP_WIDTH]) for i in range(4)],
        big=big,
    )


def _layer_fwd(x, p, table, bsz, tag):
    t = x.shape[0]
    seq3 = lambda a: a.reshape(bsz, SEQ, a.shape[-1])
    flat = lambda a: a.reshape(t, a.shape[-1])
    h1 = _rmsnorm_fwd(x, p["norm1_g"], tag + "rms1")
    z = _matmul([(h1, p["big"]("w_in", h1))], "nn", F32, tag + "mm_z")
    y_a = _sgu_fwd(z, p["w_s"], p["bm"], tag + "sgu_fwd")
    qb, kb, vb = _bprep_fwd(z, p["gq_dil"], p["gk_dil"], tag + "dil_prep")
    outs, lses = [], []
    for _, dil in DIL_PATTERNS:
        o, lse = _dil_fwd(seq3(qb), seq3(kb), seq3(vb), table, dil, f"{tag}dil{dil}_fwd")
        outs.append(flat(o))
        lses.append(flat(lse))
    y_b = _mixture_fwd(outs, lses, tag + "dil_mix")
    y_c = flat(_conv_fwd(seq3(z), p["conv_w"], p["conv_b"], p["conv_ln_g"], p["conv_ln_b"], tag + "conv_fwd"))
    qd, kx, vx = _dprep_fwd(z, p["gq_gqa"], p["gk_gqa"], tag + "gqa_prep")
    y_d = flat(_gqa_fwd(seq3(qd), seq3(kx), seq3(vx), tag + "gqa_fwd"))
    ys = [y_a, y_b, y_c, y_d]
    yn = _mixnorm_fwd(ys, p["mix_g"], tag + "mixnorm")
    x_mid = _matmul([(yn, p["big"]("w_out", yn))], "nn", F32, tag + "mm_out", residual=x)
    h2 = _rmsnorm_fwd(x_mid, p["norm2_g"], tag + "rms2")
    act, gate, up = _ffn_up(h2, p["big"]("w_gate", yn), p["big"]("w_up", yn), tag + "ffn_up")
    x_out = _matmul([(act, p["big"]("w_down", yn))], "nn", F32, tag + "mm_down", residual=x_mid)
    saved = dict(x=x, h1=h1, z=z, qb=qb, kb=kb, vb=vb, outs=outs, lses=lses, qd=qd, kx=kx, vx=vx, ys=ys, yn=yn, x_mid=x_mid,
                 h2=h2, act=act, gate=gate, up=up)
    return x_out, saved


def _layer_bwd(dx_out, dx_out_b, s, p, table, bsz, tag, emit, mid_hook):
    t = dx_out.shape[0]
    seq3 = lambda a: a.reshape(bsz, SEQ, a.shape[-1])
    flat = lambda a: a.reshape(t, a.shape[-1])
    z = s["z"]
    small = {}
    weight = lambda name: p["big"](name, None)
    emit("w_down", _matmul([(s["act"], dx_out_b)], "tn", BF16, tag + "mm_dwdown").reshape(N_CHIPS, FFN_HIDDEN // N_CHIPS, D_MODEL))
    dgate, dup = _ffn_down_bwd(dx_out_b, weight("w_down"), s["gate"], s["up"], tag + "ffn_dact")
    emit("w_gate", _matmul([(s["h2"], dgate)], "tn", BF16, tag + "mm_dwgate", slabs=N_CHIPS))
    started = emit("w_up", _matmul([(s["h2"], dup)], "tn", BF16, tag + "mm_dwup", slabs=N_CHIPS))
    dh2 = _matmul([(dgate, weight("w_gate")), (dup, weight("w_up"))], "nt", F32, tag + "mm_dh2")
    dx_mid, dx_mid_b, dg2 = _rmsnorm_bwd(dh2, s["x_mid"], p["norm2_g"], dx_out, tag + "rms2_bwd", follow=started)
    small["norm2_g"] = dg2[0]
    mid_hook(dx_mid)
    dyn = _matmul([(dx_mid_b, weight("w_out"))], "nt", F32, tag + "mm_dyn")
    started = emit("w_out", _matmul([(s["yn"], dx_mid_b)], "tn", BF16, tag + "mm_dwout").reshape(N_CHIPS, D_MODEL // N_CHIPS, D_MODEL))
    *dys, dga, dgb, dgc, dgd = _mixnorm_bwd(dyn, s["ys"], p["mix_g"], tag + "mixnorm_bwd", follow=started)
    small["mix_norm_g"] = jnp.concatenate([dga[0], dgb[0], dgc[0], dgd[0]])
    du, dv, dws, dbs = _sgu_bwd(z, dys[0], p["w_s"], p["w_s_t"], p["bm"], tag + "sgu_bwd")
    small["sgu_w"] = dws
    small["sgu_b"] = dbs[:, :N_HEADS].T
    *douts, dl0, dl1, dl2 = _mixture_bwd(s["outs"], s["lses"], dys[1], tag + "dil_mix_bwd")
    dlses = [dl0, dl1, dl2]
    dqs, dks, dvs, dscs = [], [], [], []
    for i, (_, dil) in enumerate(DIL_PATTERNS):
        dq, dk, dvv, dsc = _dil_bwd(seq3(s["qb"]), seq3(s["kb"]), seq3(s["vb"]), seq3(douts[i]), seq3(dlses[i]), table, dil,
                                    f"{tag}dil{dil}_bwd")
        dqs.append(flat(dq))
        dks.append(flat(dk))
        dvs.append(flat(dvv))
        dscs.append(dsc)
    dbq, dbk, dbv, dgq, dgk = _bprep_bwd(z, dqs, dks, dvs, p["gq_dil"], p["gk_dil"], tag + "dil_prep_bwd")
    small["dil_qn_g"], small["dil_kn_g"] = dgq[0], dgk[0]
    dca, dcg, dcw, dcb, dclg, dclb = _conv_bwd(seq3(z), seq3(dys[2]), p["conv_w"], p["conv_b"], p["conv_ln_g"], p["conv_ln_b"],
                                               tag + "conv_bwd")
    small["conv_w"], small["conv_b"], small["conv_ln_g"], small["conv_ln_b"] = dcw, dcb[0], dclg[0], dclb[0]
    dqd, dkx, dvx = _gqa_bwd(seq3(s["qd"]), seq3(s["kx"]), seq3(s["vx"]), seq3(dys[3]), tag + "gqa_bwd")
    ddq, ddk, ddv, dgq, dgk = _dprep_bwd(z, flat(dqd), flat(dkx), flat(dvx), p["gq_gqa"], p["gk_gqa"], tag + "gqa_prep_bwd")
    small["gqa_qn_g"], small["gqa_kn_g"] = dgq[0], dgk[0]
    dz = jnp.concatenate([a.astype(BF16) for a in (du, dv, dbq, dbk, dbv, flat(dca), flat(dcg), ddq, ddk, ddv)], axis=1)
    dz4 = dz.reshape(t, N_CHIPS, IN_WIDTH // N_CHIPS).transpose(1, 0, 2)
    started = emit("w_in", _matmul([(s["h1"], dz4)], "tn", BF16, tag + "mm_dwin", slabs=N_CHIPS))
    dh1 = _matmul([(dz, weight("w_in"))], "nt", F32, tag + "mm_dh1")
    dx, dx_b, dg1 = _rmsnorm_bwd(dh1, s["x"], p["norm1_g"], dx_mid, tag + "rms1_bwd", follow=started)
    small["norm1_g"] = dg1[0]
    return dx, dx_b, small, dscs


def _local_step(x, target, small, big, emit, mid_hook, bsz):
    table = small["rel_bias"]
    params = [_layer_params(l, small, functools.partial(big, l)) for l in range(DEPTH)]
    saved = []
    h = x
    for l in range(DEPTH):
        h, sv = _layer_fwd(h, params[l], table, bsz, f"l{l}_")
        saved.append(sv)
    dh, dh_b, loss = _loss_fwd_bwd(h, target, "loss")
    small_grads, dscs = [None] * DEPTH, [None] * DEPTH
    for l in reversed(range(DEPTH)):
        dh, dh_b, small_grads[l], dscs[l] = _layer_bwd(dh, dh_b, saved[l], params[l], table, bsz, f"l{l}_",
                                                       functools.partial(emit, l), functools.partial(mid_hook, l))
    fold_in = [dscs[l][i] for i in range(len(DIL_PATTERNS)) for l in range(DEPTH)]
    stacked = {k: jnp.stack([small_grads[l][k] for l in range(DEPTH)]) for k in small_grads[0]}
    stacked["rel_bias"] = _relbias_fold(fold_in, "relbias_fold")[:, :N_HEADS]
    return loss, dh, stacked


def _mesh_pos():
    return lax.axis_index("x"), lax.axis_index("y"), lax.axis_index("c")


def _other_chips(x, y):
    return [(1 - x, y), (x, 1 - y), (1 - x, 1 - y)]


_ANY = pl.BlockSpec(memory_space=pl.ANY)


def _swap_other_half(arrs, name):
    n = len(arrs)

    def body(*refs):
        in_refs, out_refs, send_sems, recv_sems = refs[:n], refs[n:2 * n], refs[2 * n], refs[2 * n + 1]
        x, y, c = _mesh_pos()
        copies = []
        for k in range(n):
            h = arrs[k].shape[1] // 2
            copies.append(pltpu.make_async_remote_copy(
                src_ref=in_refs[k].at[:, pl.ds((1 - c) * h, h)], dst_ref=out_refs[k], send_sem=send_sems.at[k],
                recv_sem=recv_sems.at[k], device_id=(x, y, 1 - c), device_id_type=MESH))
        for cp in copies:
            cp.start()
        for cp in copies:
            cp.wait()

    return pl.pallas_call(
        body, name=name, in_specs=[_ANY] * n, out_specs=[_ANY] * n,
        out_shape=[jax.ShapeDtypeStruct((a.shape[0], a.shape[1] // 2, a.shape[2]), a.dtype) for a in arrs],
        scratch_shapes=[pltpu.SemaphoreType.DMA((n,)), pltpu.SemaphoreType.DMA((n,))],
    )(*arrs)


def _swap_sibling(arrs, name):
    n = len(arrs)

    def body(*refs):
        in_refs, out_refs, send_sems, recv_sems = refs[:n], refs[n:2 * n], refs[2 * n], refs[2 * n + 1]
        x, y, c = _mesh_pos()
        copies = [pltpu.make_async_remote_copy(src_ref=in_refs[k], dst_ref=out_refs[k], send_sem=send_sems.at[k],
                                               recv_sem=recv_sems.at[k], device_id=(x, y, 1 - c), device_id_type=MESH)
                  for k in range(n)]
        for cp in copies:
            cp.start()
        for cp in copies:
            cp.wait()

    return pl.pallas_call(
        body, name=name, in_specs=[_ANY] * n, out_specs=[_ANY] * n,
        out_shape=[jax.ShapeDtypeStruct(a.shape, a.dtype) for a in arrs],
        scratch_shapes=[pltpu.SemaphoreType.DMA((n,)), pltpu.SemaphoreType.DMA((n,))],
    )(*arrs)


def _complete_pairs(arrs, name):
    n = len(arrs)

    def body(*refs):
        in_refs, out_refs, send_sems, recv_sems = refs[:n], refs[n:2 * n], refs[2 * n], refs[2 * n + 1]
        x, y, c = _mesh_pos()
        copies = [pltpu.make_async_remote_copy(src_ref=in_refs[k].at[c], dst_ref=out_refs[k].at[c], send_sem=send_sems.at[k],
                                               recv_sem=recv_sems.at[k], device_id=(x, y, 1 - c), device_id_type=MESH)
                  for k in range(n)]
        for cp in copies:
            cp.start()
        for k, cp in enumerate(copies):
            cp.wait_send()
            pltpu.make_async_remote_copy(src_ref=in_refs[k].at[1 - c], dst_ref=out_refs[k].at[1 - c], send_sem=send_sems.at[k],
                                         recv_sem=recv_sems.at[k], device_id=(x, y, 1 - c), device_id_type=MESH).wait_recv()

    return pl.pallas_call(
        body, name=name, in_specs=[_ANY] * n, out_specs=[_ANY] * n,
        out_shape=[jax.ShapeDtypeStruct(a.shape, a.dtype) for a in arrs], input_output_aliases={k: k for k in range(n)},
        scratch_shapes=[pltpu.SemaphoreType.DMA((n,)), pltpu.SemaphoreType.DMA((n,))],
    )(*arrs)


def _join_halves(mine, theirs, c):
    h = mine.shape[-2]
    axis = mine.ndim - 2
    full = jnp.zeros((*mine.shape[:-2], 2 * h, mine.shape[-1]), mine.dtype)
    full = lax.dynamic_update_slice_in_dim(full, mine, c * h, axis=axis)
    return lax.dynamic_update_slice_in_dim(full, theirs, (1 - c) * h, axis=axis)


_HBM = pl.BlockSpec(memory_space=pltpu.HBM)
_SEM = pl.BlockSpec(memory_space=pltpu.SEMAPHORE)
_DATAFLOW = pltpu.SideEffectType.DATAFLOW_SIDE_EFFECTING


def _chip_copies(src_refs, land_refs, send_sems, recv_sems, scatter):
    x, y, c = _mesh_pos()
    me = 2 * x + y
    out = []
    for k, (src_ref, land_ref) in enumerate(zip(src_refs, land_refs)):
        slot = (lambda chip: land_ref.at[c, chip]) if len(land_ref.shape) == 4 else (lambda chip: land_ref.at[chip])
        for j, (cx, cy) in enumerate(_other_chips(x, y)):
            there = 2 * cx + cy
            src = src_ref.at[there] if scatter else src_ref
            sems = dict(send_sem=send_sems.at[3 * k + j], recv_sem=recv_sems.at[3 * k + j], device_id=(cx, cy, c), device_id_type=MESH)
            out.append((pltpu.make_async_remote_copy(src_ref=src, dst_ref=slot(me), **sems),
                        pltpu.make_async_remote_copy(src_ref=src, dst_ref=slot(there), **sems)))
    return out


def _chips_start(srcs, scatter, after, name, per_core=False):
    n = len(srcs)
    lead = (2, N_CHIPS) if per_core else (N_CHIPS,)
    lands = [lax.empty((*lead, *s.shape[-2:]), s.dtype) for s in srcs]

    def body(*refs):
        src_refs, land_refs = refs[:n], refs[n:2 * n]
        send_sems, recv_sems, token = refs[2 * n + 1], refs[2 * n + 2], refs[-1]
        for sent, _ in _chip_copies(src_refs, land_refs, send_sems, recv_sems, scatter):
            sent.start()
        token[...] = jnp.zeros_like(token)

    hbm = lambda a: pltpu.HBM(a.shape, a.dtype)
    res = pl.pallas_call(
        body, name=name,
        in_specs=[_HBM] * (2 * n) + [_ANY],
        out_specs=[_SEM, _SEM] + [_HBM] * (2 * n) + [pl.BlockSpec(memory_space=pltpu.VMEM)],
        out_shape=[pltpu.SemaphoreType.DMA((3 * n,)), pltpu.SemaphoreType.DMA((3 * n,))] + [hbm(a) for a in srcs] + [hbm(a) for a in lands]
        + [jax.ShapeDtypeStruct((8, 128), F32)],
        input_output_aliases={i: 2 + i for i in range(2 * n)},
        compiler_params=pltpu.CompilerParams(has_side_effects=_DATAFLOW),
    )(*[pltpu.with_memory_space_constraint(a, pltpu.HBM) for a in (*srcs, *lands)], after)
    return (res[0], res[1], res[2:2 + n], res[2 + n:2 + 2 * n]), res[-1]


def _chips_wait(handle, scatter, after, name):
    send_sems, recv_sems, srcs, lands = handle
    n = len(srcs)

    def body(*refs):
        src_refs, land_refs = refs[:n], refs[n:2 * n]
        send_sems, recv_sems = refs[2 * n], refs[2 * n + 1]
        for sent, landed in _chip_copies(src_refs, land_refs, send_sems, recv_sems, scatter):
            sent.wait_send()
            landed.wait_recv()

    hbm = lambda a: pltpu.HBM(a.shape, a.dtype)
    res = pl.pallas_call(
        body, name=name,
        in_specs=[_HBM] * (2 * n) + [_SEM, _SEM, _ANY], out_specs=[_HBM] * (2 * n),
        out_shape=[hbm(a) for a in srcs] + [hbm(a) for a in lands],
        input_output_aliases={i: i for i in range(2 * n)},
        compiler_params=pltpu.CompilerParams(has_side_effects=_DATAFLOW),
    )(*srcs, *lands, send_sems, recv_sems, after)
    return res[n:]


N_DEV = 8


def _allgather_sum_small(block, name):
    m_per, n = block.shape

    def body(x_ref, out_ref, sum_ref, send_sems, recv_sems, local_sem):
        x, y, c = _mesh_pos()
        me, sibling = (x, y, c), (x, y, 1 - c)
        chips = _other_chips(x, y)

        def rows(px, py, pc):
            return out_ref.at[pl.ds((4 * px + 2 * py + pc) * m_per, m_per), :]

        def copy(k, blk, to, src=None):
            return pltpu.make_async_remote_copy(src_ref=rows(*blk) if src is None else src, dst_ref=rows(*blk),
                                                send_sem=send_sems.at[k], recv_sem=recv_sems.at[k], device_id=to, device_id_type=MESH)

        mine = pltpu.make_async_copy(x_ref, rows(*me), local_sem)
        mine.start()
        first = [copy(0, me, sibling, src=x_ref)]
        first += [copy(1 + j, me, (*chip, c), src=x_ref) for j, chip in enumerate(chips)]
        for cp in first:
            cp.start()
        passed = [copy(4 + j, (*chip, c), sibling) for j, chip in enumerate(chips)]
        for j, chip in enumerate(chips):
            copy(1 + j, (*chip, c), me).wait_recv()
            passed[j].start()
        copy(0, sibling, me).wait_recv()
        for j, chip in enumerate(chips):
            copy(4 + j, (*chip, 1 - c), me).wait_recv()
        for cp in first + passed:
            cp.wait_send()
        mine.wait()
        total = out_ref[pl.ds(0, m_per), :]
        for d in range(1, N_DEV):
            total = total + out_ref[pl.ds(d * m_per, m_per), :]
        sum_ref[...] = total

    vmem = pl.BlockSpec(memory_space=pltpu.VMEM)
    return pl.pallas_call(
        body, name=name, in_specs=[vmem], out_specs=[vmem, vmem],
        out_shape=[jax.ShapeDtypeStruct((N_DEV * m_per, n), F32), jax.ShapeDtypeStruct((m_per, n), F32)],
        scratch_shapes=[pltpu.SemaphoreType.DMA((7,)), pltpu.SemaphoreType.DMA((7,)), pltpu.SemaphoreType.DMA],
        compiler_params=pltpu.CompilerParams(vmem_limit_bytes=V7X_VMEM_LIMIT),
    )(block)


WEIGHTS = ("rel_bias", "norm1_g", "w_in", "sgu_w", "sgu_b", "dil_qn_g", "dil_kn_g", "conv_w", "conv_b", "conv_ln_g", "conv_ln_b",
           "gqa_qn_g", "gqa_kn_g", "mix_norm_g", "w_out", "norm2_g", "w_gate", "w_up", "w_down")
SHARDED = ("w_in", "w_out", "w_gate", "w_up", "w_down")
COLUMN_SHARDED = ("w_in", "w_gate", "w_up")
REPLICATED = tuple(k for k in WEIGHTS if k not in SHARDED and k != "conv_w")


def _pack(parts):
    flat = jnp.concatenate([p.reshape(-1) for p in parts])
    pad = (-flat.shape[0]) % (8 * 128)
    return jnp.pad(flat, (0, pad)).reshape(-1, 128)


def _unpack(buf, shapes):
    flat = buf.reshape(-1)
    out, at = [], 0
    for s in shapes:
        size = math.prod(s)
        out.append(flat[at:at + size].reshape(s))
        at += size
    return out


RS_TM = (256, 128, 64, 32, 16)


def _add_halves(g, sib, c, name):
    slabs, h, cols = sib.shape
    tm = _pick(h, RS_TM)
    nb = h // tm

    def body(c_ref, g_ref, s_ref, o_ref):
        o_ref[...] = (g_ref[...].astype(F32) + s_ref[...].astype(F32)).astype(BF16)

    blk = pl.BlockSpec((None, tm, cols), lambda j, i, c_ref: (j, i, 0))
    grid_spec = pltpu.PrefetchScalarGridSpec(
        num_scalar_prefetch=1, grid=(slabs, nb),
        in_specs=[pl.BlockSpec((None, tm, cols), lambda j, i, c_ref: (j, c_ref[0] * nb + i, 0)), blk], out_specs=blk)
    return pl.pallas_call(body, name=name, grid_spec=grid_spec, out_shape=jax.ShapeDtypeStruct(sib.shape, BF16),
                          compiler_params=_cparams(("parallel", "parallel")))(c, g, sib)


def _add_own_three(own, land, chip, name):
    _, h, cols = own.shape
    tm = _pick(h, RS_TM)

    def body(chip_ref, own_ref, l0_ref, l1_ref, l2_ref, o_ref):
        o_ref[...] = ((own_ref[...].astype(F32) + l0_ref[...].astype(F32)) + l1_ref[...].astype(F32)) + l2_ref[...].astype(F32)

    slot = lambda r: pl.BlockSpec((None, tm, cols), functools.partial(lambda i, chip_ref, r: (jnp.bitwise_xor(chip_ref[0], r), i, 0), r=r))
    grid_spec = pltpu.PrefetchScalarGridSpec(
        num_scalar_prefetch=1, grid=(h // tm,),
        in_specs=[slot(0), slot(1), slot(2), slot(3)],
        out_specs=pl.BlockSpec((tm, cols), lambda i, chip_ref: (i, 0)))
    return pl.pallas_call(body, name=name, grid_spec=grid_spec, out_shape=jax.ShapeDtypeStruct((h, cols), F32),
                          compiler_params=_cparams(("parallel",)))(chip, own, land, land, land)


def _reduce_start(grads, c1, after, tag):
    from_sibling = _swap_other_half(grads, tag + "pair")
    sums = [_add_halves(g, s, c1, f"{tag}add2_{k}") for k, (g, s) in enumerate(zip(grads, from_sibling))]
    handle, token = _chips_start(sums, True, after, tag + "start")
    return (handle, sums), token


def _reduce_finish(started, c, chip1, after, tag):
    handle, sums = started
    lands = _chips_wait(handle, True, after, tag + "wait")
    totals = [_add_own_three(s, land, chip1, f"{tag}add4_{k}") for k, (s, land) in enumerate(zip(sums, lands))]
    return [_join_halves(mine, theirs, c) for mine, theirs in zip(totals, _swap_sibling(totals, tag + "share"))]


GATHER_GROUPS = (("w_in",), ("w_out", "w_gate", "w_up", "w_down"))
REDUCE_GROUPS = (("w_down", "w_gate", "w_up"), ("w_out",), ("w_in",))


def kernel(x, rel_bias, norm1_g, w_in, sgu_w, sgu_b, dil_qn_g, dil_kn_g, conv_w, conv_b, conv_ln_g, conv_ln_b, gqa_qn_g, gqa_kn_g, mix_norm_g, w_out, norm2_g, w_gate, w_up, w_down, loss_target, m_rel_bias, m_norm1_g, m_w_in, m_sgu_w, m_sgu_b, m_dil_qn_g, m_dil_kn_g, m_conv_w, m_conv_b, m_conv_ln_g, m_conv_ln_b, m_gqa_qn_g, m_gqa_kn_g, m_mix_norm_g, m_w_out, m_norm2_g, m_w_gate, m_w_up, m_w_down, v_rel_bias, v_norm1_g, v_w_in, v_sgu_w, v_sgu_b, v_dil_qn_g, v_dil_kn_g, v_conv_w, v_conv_b, v_conv_ln_g, v_conv_ln_b, v_gqa_qn_g, v_gqa_kn_g, v_mix_norm_g, v_w_out, v_norm2_g, v_w_gate, v_w_up, v_w_down):
    w = dict(rel_bias=rel_bias, norm1_g=norm1_g, w_in=w_in, sgu_w=sgu_w, sgu_b=sgu_b, dil_qn_g=dil_qn_g, dil_kn_g=dil_kn_g,
             conv_w=conv_w, conv_b=conv_b, conv_ln_g=conv_ln_g, conv_ln_b=conv_ln_b, gqa_qn_g=gqa_qn_g, gqa_kn_g=gqa_kn_g,
             mix_norm_g=mix_norm_g, w_out=w_out, norm2_g=norm2_g, w_gate=w_gate, w_up=w_up, w_down=w_down)
    m = dict(rel_bias=m_rel_bias, norm1_g=m_norm1_g, w_in=m_w_in, sgu_w=m_sgu_w, sgu_b=m_sgu_b, dil_qn_g=m_dil_qn_g,
             dil_kn_g=m_dil_kn_g, conv_w=m_conv_w, conv_b=m_conv_b, conv_ln_g=m_conv_ln_g, conv_ln_b=m_conv_ln_b,
             gqa_qn_g=m_gqa_qn_g, gqa_kn_g=m_gqa_kn_g, mix_norm_g=m_mix_norm_g, w_out=m_w_out, norm2_g=m_norm2_g,
             w_gate=m_w_gate, w_up=m_w_up, w_down=m_w_down)
    v = dict(rel_bias=v_rel_bias, norm1_g=v_norm1_g, w_in=v_w_in, sgu_w=v_sgu_w, sgu_b=v_sgu_b, dil_qn_g=v_dil_qn_g,
             dil_kn_g=v_dil_kn_g, conv_w=v_conv_w, conv_b=v_conv_b, conv_ln_g=v_conv_ln_g, conv_ln_b=v_conv_ln_b,
             gqa_qn_g=v_gqa_qn_g, gqa_kn_g=v_gqa_kn_g, mix_norm_g=v_mix_norm_g, w_out=v_w_out, norm2_g=v_norm2_g,
             w_gate=v_w_gate, w_up=v_w_up, w_down=v_w_down)
    bsz = x.shape[0]
    t = bsz * SEQ
    xi, yi, ci = _mesh_pos()
    chip = 2 * xi + yi
    conv_cols = conv_w.shape[-1]

    conv_rows = DEPTH * CONV_WIDTH
    conv_block = jnp.pad(conv_w.reshape(conv_rows, conv_cols), ((0, (-conv_rows) % 8), (0, 0)))
    every, _ = _allgather_sum_small(conv_block, "conv_w_gather")
    every = every.reshape(N_DEV, conv_block.shape[0], conv_cols)
    conv_w_full = jnp.concatenate([every[2 * j, :conv_rows].reshape(DEPTH, CONV_WIDTH, conv_cols) for j in range(N_CHIPS)], axis=-1)

    c1 = jnp.reshape(ci, (1,)).astype(jnp.int32)
    chip1 = jnp.reshape(chip, (1,)).astype(jnp.int32)

    def own_shard(k):
        return lax.dynamic_index_in_dim(w[k], ci, axis=0, keepdims=False).astype(BF16)

    fetches, token = [], every
    for gi, group in enumerate(GATHER_GROUPS):
        shards = [own_shard(k) for k in group]
        handle, token = _chips_start(shards, False, token, f"gather{gi}_start", per_core=True)
        fetches.append((handle, shards))
    all_started = token
    gathered = {}

    def big(l, name, after):
        if (l, name) not in gathered:
            gi = [name in group for group in GATHER_GROUPS].index(True)
            handle, shards = fetches[gi]
            lands = _chips_wait(handle, False, all_started if after is None else after, f"gather{gi}_wait")
            mine = [lax.dynamic_update_slice(land, own[None, None], (ci, chip, 0, 0)) for land, own in zip(lands, shards)]
            for k, g in zip(GATHER_GROUPS[gi], _complete_pairs(mine, f"gather{gi}_share")):
                rows, cols = g.shape[2:]
                for layer in range(DEPTH):
                    gl = g[layer]
                    gathered[layer, k] = (gl.transpose(1, 0, 2).reshape(rows, N_CHIPS * cols) if k in COLUMN_SHARDED
                                          else gl.reshape(N_CHIPS * rows, cols))
        return gathered[l, name]

    big(0, "w_in", None)

    pending, started, reduced = {}, {}, {}

    def emit(l, name, g):
        pending[l, name] = g
        for gi, group in enumerate(REDUCE_GROUPS):
            if name in group and all((l, k) in pending for k in group):
                started[l, gi], token = _reduce_start([pending[l, k] for k in group], c1, g, f"l{l}_reduce{gi}_")
                return token
        return None

    def finish(l, after):
        for gi, group in enumerate(REDUCE_GROUPS):
            for k, r in zip(group, _reduce_finish(started[l, gi], ci, chip1, after, f"l{l}_reduce{gi}_")):
                reduced[l, k] = r
            after = reduced[l, group[0]]

    def mid_hook(l, a):
        if l + 1 < DEPTH:
            finish(l + 1, a)

    small = {k: w[k] for k in REPLICATED}
    small["conv_w"] = conv_w_full
    loss, dx, small_grads = _local_step(x.reshape(t, D_MODEL), loss_target.reshape(t, D_MODEL), small, big, emit, mid_hook, bsz)
    loss = lax.psum(loss[0, 0], ("x", "y", "c"))
    finish(0, dx)

    grads, deltas, new_m, new_v = {}, {}, {}, {}
    for i, k in enumerate(SHARDED):
        g2 = jnp.concatenate([reduced[l, k] for l in range(DEPTH)], axis=0)
        d2, m2, v2 = _adamw(w[k].reshape(g2.shape), g2, m[k].reshape(g2.shape), v[k].reshape(g2.shape), "adamw_" + k)
        grads[k], deltas[k], new_m[k], new_v[k] = (a.reshape(w[k].shape) for a in (g2, d2, m2, v2))

    names = REPLICATED + ("conv_w",)
    shapes = [small_grads[k].shape for k in names]
    _, summed = _allgather_sum_small(_pack([small_grads[k] for k in names]), "small_grads_sum")
    summed_parts = dict(zip(names, _unpack(summed, shapes)))
    rep_shapes = [w[k].shape for k in REPLICATED]
    packed = [_pack([src[k] for k in REPLICATED]) for src in (w, {k: summed_parts[k] for k in REPLICATED}, m, v)]
    d_p, m_p, v_p = _adamw(*packed, "adamw_replicated")
    for k, gk, dk, mk, vk in zip(REPLICATED, _unpack(packed[1], rep_shapes), _unpack(d_p, rep_shapes), _unpack(m_p, rep_shapes),
                                 _unpack(v_p, rep_shapes)):
        grads[k], deltas[k], new_m[k], new_v[k] = gk, dk, mk, vk
    g_conv = lax.dynamic_slice_in_dim(summed_parts["conv_w"], chip * conv_cols, conv_cols, axis=2)
    packed = [_pack([a]) for a in (conv_w, g_conv, m["conv_w"], v["conv_w"])]
    d_p, m_p, v_p = _adamw(*packed, "adamw_conv_w")
    grads["conv_w"] = g_conv
    deltas["conv_w"], new_m["conv_w"], new_v["conv_w"] = (_unpack(a, [conv_w.shape])[0] for a in (d_p, m_p, v_p))

    return (loss, dx.reshape(x.shape), *[grads[k] for k in WEIGHTS], *[deltas[k] for k in WEIGHTS],
            *[new_m[k] for k in WEIGHTS], *[new_v[k] for k in WEIGHTS])
```

```python
import functools
import math

import numpy as np
import jax
import jax.numpy as jnp
from jax import lax
from jax.experimental import pallas as pl
from jax.experimental.pallas import tpu as pltpu

F32 = jnp.float32
BF16 = jnp.bfloat16

D_MODEL = 2048
SEQ = 2048
DEPTH = 2
HEAD_DIM = 64
GROUP_WIDTH = 512
N_HEADS = 8
SGU_CHUNK = 128
DIL_PATTERNS = ((128, 1), (512, 4), (2048, 16))
DIL_HALF = 64
CONV_WIDTH = 31
KV_WIDTH = 128
GRID_W = 64
ROPE_THETA = 10000.0
REL_BUCKETS = 32
REL_MAX_DIST = 1024
FFN_HIDDEN = 5632
IN_WIDTH = 4352
RMS_EPS = 1e-6
LN_EPS = 1e-5
ADAM_LR = 0.001
ADAM_B1 = 0.9
ADAM_B2 = 0.999
ADAM_EPS = 1e-08
ADAM_WD = 0.01
ADAM_STEP = 10
N_CHIPS = 4

V7X_VMEM_LIMIT = 56 * 1024 * 1024
MATMUL_VMEM_BUDGET = 48 * 1024 * 1024
HI = lax.Precision.HIGHEST
MESH = pl.DeviceIdType.MESH


def _cparams(sem=None):
    return pltpu.CompilerParams(dimension_semantics=sem, vmem_limit_bytes=V7X_VMEM_LIMIT)


def _pick(n, cands):
    for c in cands:
        if n % c == 0:
            return c
    raise ValueError(f"no tile for {n}")


_DIMS = {"nn": (((1,), (0,)), ((), ())), "nt": (((1,), (1,)), ((), ())), "tn": (((0,), (0,)), ((), ()))}


def _matmul(pairs, mode, out_dtype, name, residual=None, slabs=1):
    a0, b0 = pairs[0]
    b3 = b0.ndim == 3
    if mode == "nn":
        (M, K), N = a0.shape, b0.shape[1]
    elif mode == "nt":
        (M, K), N = a0.shape, b0.shape[0]
    else:
        (K, M) = a0.shape
        N = b0.shape[-1] if b3 else b0.shape[1] // slabs
    npairs = len(pairs)
    a_bytes, b_bytes, o_bytes = a0.dtype.itemsize, b0.dtype.itemsize, jnp.dtype(out_dtype).itemsize
    per_out = 4 + 2 * o_bytes + (8 if residual is not None else 0)
    tn_cands = [c for c in ((1024, 512) if K <= 2048 else (512,)) + (1408, 2176, 256) if N % c == 0] + [N]
    tm, tn = next((tm, tn) for tn in tn_cands for tm in (1024, 1408, 512, 256)
                  if M % tm == 0 and 2 * npairs * K * (tm * a_bytes + tn * b_bytes) + tm * tn * per_out <= MATMUL_VMEM_BUDGET)
    tk, nk = K, 1
    nj = N // tn

    if mode in ("nn", "nt"):
        a_spec = pl.BlockSpec((tm, tk), lambda s, i, j, k: (i, k))
    else:
        a_spec = pl.BlockSpec((tk, tm), lambda s, i, j, k: (k, i))
    if mode == "nt":
        b_spec = pl.BlockSpec((tn, tk), lambda s, i, j, k: (j, k))
    elif b3:
        b_spec = pl.BlockSpec((None, tk, tn), lambda s, i, j, k: (s, k, j))
    else:
        b_spec = pl.BlockSpec((tk, tn), lambda s, i, j, k: (k, s * nj + j))
    if slabs > 1:
        o_spec = pl.BlockSpec((None, tm, tn), lambda s, i, j, k: (s, i, j))
        o_shape = jax.ShapeDtypeStruct((slabs, M, N), out_dtype)
    else:
        o_spec = pl.BlockSpec((tm, tn), lambda s, i, j, k: (i, j))
        o_shape = jax.ShapeDtypeStruct((M, N), out_dtype)
    in_specs = [a_spec] * npairs + [b_spec] * npairs
    args = [a for a, _ in pairs] + [b for _, b in pairs]
    if residual is not None:
        in_specs.append(pl.BlockSpec((tm, tn), lambda s, i, j, k: (i, j)))
        args.append(residual)
    dims = _DIMS[mode]

    def body(*refs):
        a_refs, b_refs = refs[:npairs], refs[npairs:2 * npairs]
        res_ref = refs[2 * npairs] if residual is not None else None
        o_ref = refs[-1]
        r = None
        for a_ref, b_ref in zip(a_refs, b_refs):
            d = lax.dot_general(a_ref[...].astype(BF16), b_ref[...].astype(BF16), dims, preferred_element_type=F32)
            r = d if r is None else r + d
        if res_ref is not None:
            r = r + res_ref[...]
        o_ref[...] = r.astype(out_dtype)

    return pl.pallas_call(
        body, name=name, grid=(slabs, M // tm, nj, nk), in_specs=in_specs, out_specs=o_spec, out_shape=o_shape,
        compiler_params=_cparams(("parallel", "parallel", "parallel", "arbitrary")),
    )(*args)


def _rowmap(fn, rows, fulls, row_outs, acc_outs, name, tm, n_rows):
    nr, nf, nro = len(rows), len(fulls), len(row_outs)
    rows = [r if len(r) == 4 else (*r, n_rows // tm) for r in rows]
    in_specs = [pl.BlockSpec((tm, w), functools.partial(lambda i, cb, per: (i % per, cb), cb=cb, per=per)) for _, w, cb, per in rows]
    in_specs += [pl.BlockSpec(f.shape, lambda i: (0,) * f.ndim) for f in fulls]
    out_specs = [pl.BlockSpec((tm, w), lambda i: (i, 0)) for w, _ in row_outs]
    out_specs += [pl.BlockSpec(s, functools.partial(lambda i, n: (0,) * n, n=len(s))) for s in acc_outs]
    out_shape = [jax.ShapeDtypeStruct((n_rows, w), dt) for w, dt in row_outs]
    out_shape += [jax.ShapeDtypeStruct(s, F32) for s in acc_outs]

    def body(*refs):
        ins = [r[...] for r in refs[:nr + nf]]
        outs = fn(*ins)
        o_refs = refs[nr + nf:]
        for o_ref, val in zip(o_refs[:nro], outs[:nro]):
            o_ref[...] = val.astype(o_ref.dtype)
        if acc_outs:
            first = pl.program_id(0) == 0
            for o_ref, val in zip(o_refs[nro:], outs[nro:]):
                @pl.when(first)
                def _(o_ref=o_ref, val=val):
                    o_ref[...] = val

                @pl.when(jnp.logical_not(first))
                def _(o_ref=o_ref, val=val):
                    o_ref[...] += val

    res = pl.pallas_call(
        body, name=name, grid=(n_rows // tm,), in_specs=in_specs, out_specs=out_specs, out_shape=out_shape,
        compiler_params=_cparams(("arbitrary",) if acc_outs else ("parallel",)),
    )(*[r[0] for r in rows], *fulls)
    return res


def _rms(x, g):
    return x * lax.rsqrt(jnp.mean(x * x, axis=-1, keepdims=True) + RMS_EPS) * g


def _rmsnorm_fwd(x, g, name):
    t = x.shape[0]
    return _rowmap(lambda xv, gv: (_rms(xv, gv),), [(x, D_MODEL, 0)], [g], [(D_MODEL, BF16)], [], name, 512, t)[0]


def _rmsnorm_bwd(dh, x, g, dres, name, follow=None):
    t = x.shape[0]

    def fn(dhv, xv, drv, gv, *_):
        _, vjp = jax.vjp(_rms, xv, gv)
        dx, dg = vjp(dhv)
        return dx + drv, dx + drv, dg

    fulls = [g] if follow is None else [g, follow]
    return _rowmap(fn, [(dh, D_MODEL, 0), (x, D_MODEL, 0), (dres, D_MODEL, 0)], fulls, [(D_MODEL, F32), (D_MODEL, BF16)],
                   [(1, D_MODEL)], name, 256, t)


def _loss_fwd_bwd(y, target, name):
    t = y.shape[0]

    def fn(yv, tv):
        e = yv - tv
        return e * (1.0 / D_MODEL), e * (1.0 / D_MODEL), (0.5 / D_MODEL) * jnp.sum(e * e, keepdims=True)

    return _rowmap(fn, [(y, D_MODEL, 0), (target, D_MODEL, 0)], [], [(D_MODEL, F32), (D_MODEL, BF16)], [(1, 1)], name, 512, t)


FFN_TILE = (1024, 512)


def _ffn_up(h, wg, wu, name):
    t, n = h.shape[0], wg.shape[1]
    tm, tn = FFN_TILE

    def body(h_ref, wg_ref, wu_ref, act_ref, du_ref, dg_ref):
        hv = h_ref[...]
        g = jnp.dot(hv, wg_ref[...], preferred_element_type=F32)
        u = jnp.dot(hv, wu_ref[...], preferred_element_type=F32)
        sg = jax.nn.sigmoid(g)
        silu = g * sg
        act_ref[...] = (silu * u).astype(BF16)
        du_ref[...] = silu.astype(BF16)
        dg_ref[...] = (u * (sg + silu * (1.0 - sg))).astype(BF16)

    o_spec = pl.BlockSpec((tm, tn), lambda i, j: (i, j))
    o_shape = jax.ShapeDtypeStruct((t, n), BF16)
    return pl.pallas_call(
        body, name=name, grid=(t // tm, n // tn),
        in_specs=[pl.BlockSpec((tm, D_MODEL), lambda i, j: (i, 0)), pl.BlockSpec((D_MODEL, tn), lambda i, j: (0, j)),
                  pl.BlockSpec((D_MODEL, tn), lambda i, j: (0, j))],
        out_specs=[o_spec] * 3, out_shape=[o_shape] * 3, compiler_params=_cparams(("parallel", "parallel")),
    )(h, wg, wu)


def _ffn_down_bwd(dy, wd, act_du, act_dg, name):
    t, n = dy.shape[0], wd.shape[0]
    tm, tn = FFN_TILE

    def body(dy_ref, wd_ref, adu_ref, adg_ref, dg_ref, du_ref):
        dact = lax.dot_general(dy_ref[...].astype(BF16), wd_ref[...], _DIMS["nt"], preferred_element_type=F32)
        du_ref[...] = (dact * adu_ref[...].astype(F32)).astype(BF16)
        dg_ref[...] = (dact * adg_ref[...].astype(F32)).astype(BF16)

    o_spec = pl.BlockSpec((tm, tn), lambda i, j: (i, j))
    o_shape = jax.ShapeDtypeStruct((t, n), BF16)
    return pl.pallas_call(
        body, name=name, grid=(t // tm, n // tn),
        in_specs=[pl.BlockSpec((tm, D_MODEL), lambda i, j: (i, 0)), pl.BlockSpec((tn, D_MODEL), lambda i, j: (j, 0)),
                  o_spec, o_spec],
        out_specs=[o_spec] * 2, out_shape=[o_shape] * 2, compiler_params=_cparams(("parallel", "parallel")),
    )(dy, wd, act_du, act_dg)


def _np_group_avg(width, group=HEAD_DIM):
    i = np.arange(width)
    return ((i[:, None] // group) == (i[None, :] // group)).astype(np.float32) / group


def _np_tile_fold(width, group=HEAD_DIM):
    return ((np.arange(width)[:, None] % group) == np.arange(group)[None, :]).astype(np.float32)


def _np_group_fold(width, group=HEAD_DIM, pad=128):
    return ((np.arange(width)[:, None] // group) == np.arange(pad)[None, :]).astype(np.float32)


def _np_rope_partner(width):
    i = np.arange(width)
    partner = np.where(i % 32 < 16, i + 16, i - 16)
    return (partner[:, None] == i[None, :]).astype(np.float32)


def _np_kv_expand():
    src = np.arange(KV_WIDTH)
    dst = np.arange(GROUP_WIDTH)
    return ((src[:, None] // HEAD_DIM == dst[None, :] // (4 * HEAD_DIM)) & (src[:, None] % HEAD_DIM == dst[None, :] % HEAD_DIM)).astype(np.float32)


def _np_rope_tables(n_heads):
    t = np.arange(SEQ)
    pos = {0: (t // GRID_W).astype(np.float32), 1: (t % GRID_W).astype(np.float32)}
    freqs = (ROPE_THETA ** (-np.arange(16, dtype=np.float32) / 16)).astype(np.float32)
    cos_parts, sin_parts = [], []
    for axis in (0, 1):
        ang = pos[axis][:, None] * freqs[None, :]
        c, s = np.cos(ang).astype(np.float32), np.sin(ang).astype(np.float32)
        cos_parts += [c, c]
        sin_parts += [-s, s]
    cos = np.concatenate(cos_parts, axis=1)
    sin = np.concatenate(sin_parts, axis=1)
    return np.tile(cos, (1, n_heads)), np.tile(sin, (1, n_heads))


def _np_t5_buckets(rel):
    nb = REL_BUCKETS // 2
    max_exact = nb // 2
    ret = np.where(rel > 0, nb, 0)
    n = np.abs(rel)
    nf = np.maximum(n, 1).astype(np.float32)
    large = max_exact + (np.log(nf / max_exact) / math.log(REL_MAX_DIST / max_exact) * (nb - max_exact)).astype(np.int32)
    large = np.minimum(large, nb - 1)
    return (ret + np.where(n < max_exact, n, large)).astype(np.int32)


DIL_QB = 128
DIL_WIN = DIL_QB + 2 * DIL_HALF


def _np_dil_buckets(dil):
    off = np.arange(DIL_WIN)[None, :] - DIL_HALF - np.arange(DIL_QB)[:, None]
    return _np_t5_buckets(off * dil)


def _dil_live_buckets(dil):
    off = np.arange(-DIL_HALF, DIL_HALF + 1)
    return sorted(set(_np_t5_buckets(off * dil).tolist()))


def _head_stat(x, mavg):
    return jnp.dot(x, mavg, precision=HI, preferred_element_type=F32)


def _gelu(x):
    return 0.5 * x * (1.0 + jnp.tanh(math.sqrt(2.0 / math.pi) * (x + 0.044715 * (x * x * x))))


def _sgu_pre(u_pre, v_pre, mavg):
    v = _gelu(v_pre)
    xc = v - _head_stat(v, mavg)
    vn = xc * lax.rsqrt(_head_stat(xc * xc, mavg) + LN_EPS)
    return _gelu(u_pre), vn


def _sgu_mix(w_ref, vnb, bm):
    lane_group = lax.broadcasted_iota(jnp.int32, (1, GROUP_WIDTH), 1) // HEAD_DIM
    mixed = bm
    for g in range(N_HEADS):
        r = jnp.dot(w_ref[g], vnb, preferred_element_type=F32)
        mixed = mixed + jnp.where(lane_group == g, r, 0.0)
    return mixed


SGU_TM = 512


def _sgu_fwd(z, w_s, bm, name):
    t = z.shape[0]
    mavg = jnp.asarray(_np_group_avg(GROUP_WIDTH))

    def body(u_ref, v_ref, w_ref, bm_ref, mavg_ref, y_ref):
        for c in range(SGU_TM // SGU_CHUNK):
            rows = pl.ds(c * SGU_CHUNK, SGU_CHUNK)
            u, vn = _sgu_pre(u_ref[rows, :], v_ref[rows, :], mavg_ref[...])
            y_ref[rows, :] = u * _sgu_mix(w_ref, vn.astype(BF16), bm_ref[...])

    full = lambda a: pl.BlockSpec(a.shape, lambda i: (0,) * a.ndim)
    return pl.pallas_call(
        body, name=name, grid=(t // SGU_TM,),
        in_specs=[pl.BlockSpec((SGU_TM, GROUP_WIDTH), lambda i: (i, 0)), pl.BlockSpec((SGU_TM, GROUP_WIDTH), lambda i: (i, 1)),
                  full(w_s), full(bm), full(mavg)],
        out_specs=pl.BlockSpec((SGU_TM, GROUP_WIDTH), lambda i: (i, 0)),
        out_shape=jax.ShapeDtypeStruct((t, GROUP_WIDTH), F32), compiler_params=_cparams(("parallel",)),
    )(z, z, w_s, bm, mavg)


def _sgu_bwd(z, dy, w_s, w_s_t, bm, name):
    t = z.shape[0]
    mavg = jnp.asarray(_np_group_avg(GROUP_WIDTH))
    gfold = jnp.asarray(_np_group_fold(GROUP_WIDTH))

    def body(u_ref, v_ref, dy_ref, w_ref, wt_ref, bm_ref, mavg_ref, gfold_ref, du_ref, dv_ref, dw_ref, dbs_ref, dbm_ref):
        @pl.when(pl.program_id(0) == 0)
        def _():
            dw_ref[...] = jnp.zeros_like(dw_ref)
            dbm_ref[...] = jnp.zeros_like(dbm_ref)

        lane_group = lax.broadcasted_iota(jnp.int32, (1, GROUP_WIDTH), 1) // HEAD_DIM
        for c in range(SGU_TM // SGU_CHUNK):
            rows = pl.ds(c * SGU_CHUNK, SGU_CHUNK)
            (u, vn), pre_vjp = jax.vjp(functools.partial(_sgu_pre, mavg=mavg_ref[...]), u_ref[rows, :], v_ref[rows, :])
            vnb = vn.astype(BF16)
            mixed = _sgu_mix(w_ref, vnb, bm_ref[...])
            dyv = dy_ref[rows, :]
            dmixed = dyv * u
            dbm_ref[...] += dmixed
            dvn = jnp.zeros_like(vn)
            for g in range(N_HEADS):
                dm_g = jnp.where(lane_group == g, dmixed, 0.0).astype(BF16)
                dw_ref[g] += lax.dot_general(dm_g, vnb, _DIMS["nt"], preferred_element_type=F32)
                dvn = dvn + jnp.dot(wt_ref[g], dm_g, preferred_element_type=F32)
            du_pre, dv_pre = pre_vjp((dyv * mixed, dvn))
            du_ref[rows, :] = du_pre
            dv_ref[rows, :] = dv_pre

        @pl.when(pl.program_id(0) == t // SGU_TM - 1)
        def _():
            dbs_ref[...] = jnp.dot(dbm_ref[...], gfold_ref[...], precision=HI, preferred_element_type=F32)

    full = lambda a: pl.BlockSpec(a.shape, lambda i: (0,) * a.ndim)
    row = pl.BlockSpec((SGU_TM, GROUP_WIDTH), lambda i: (i, 0))
    return pl.pallas_call(
        body, name=name, grid=(t // SGU_TM,),
        in_specs=[row, pl.BlockSpec((SGU_TM, GROUP_WIDTH), lambda i: (i, 1)), row, full(w_s), full(w_s_t), full(bm), full(mavg),
                  full(gfold)],
        out_specs=[row, row, pl.BlockSpec((N_HEADS, SGU_CHUNK, SGU_CHUNK), lambda i: (0, 0, 0)),
                   pl.BlockSpec((SGU_CHUNK, 128), lambda i: (0, 0))],
        out_shape=[jax.ShapeDtypeStruct((t, GROUP_WIDTH), F32)] * 2 + [jax.ShapeDtypeStruct((N_HEADS, SGU_CHUNK, SGU_CHUNK), F32),
                                                                      jax.ShapeDtypeStruct((SGU_CHUNK, 128), F32)],
        scratch_shapes=[pltpu.VMEM((SGU_CHUNK, GROUP_WIDTH), F32)],
        compiler_params=_cparams(("arbitrary",)),
    )(z, z, dy, w_s, w_s_t, bm, mavg, gfold)


def _pair_softmax(q2, k2, hh, bias, valid):
    head = (lax.broadcasted_iota(jnp.int32, (1, 2 * HEAD_DIM), 1) // HEAD_DIM) == hh
    qm = jnp.where(head, q2, jnp.zeros_like(q2))
    s = lax.dot_general(qm, k2, _DIMS["nt"], preferred_element_type=F32)
    if bias is not None:
        s = s + bias
    if valid is not None:
        s = jnp.where(valid, s, -1e30)
    m = jnp.max(s, axis=-1, keepdims=True)
    e = jnp.exp(s - m)
    l = jnp.sum(e, axis=-1, keepdims=True)
    return head, qm, e / l, m + jnp.log(l)


def _attn_pair_fwd(q2, k2, v2, biases, valid):
    o2 = lse2 = None
    for hh in range(2):
        head, _, p, lse = _pair_softmax(q2, k2, hh, None if biases is None else biases[hh], valid)
        oh = jnp.dot(p.astype(BF16), v2, preferred_element_type=F32)
        o_h = jnp.where(head, oh, 0.0)
        l_h = jnp.where(head, lse, 0.0)
        o2 = o_h if o2 is None else o2 + o_h
        lse2 = l_h if lse2 is None else lse2 + l_h
    return o2, lse2


def _attn_pair_bwd(q2, k2, v2, biases, valid, do2, dlse2):
    dq2 = dk2 = dv2 = None
    ds_heads = []
    for hh in range(2):
        head, qm, p, _ = _pair_softmax(q2, k2, hh, None if biases is None else biases[hh], valid)
        dom = jnp.where(head, do2, 0.0).astype(BF16)
        dp = lax.dot_general(dom, v2, _DIMS["nt"], preferred_element_type=F32)
        delta = jnp.sum(dp * p, axis=-1, keepdims=True)
        if dlse2 is not None:
            delta = delta - jnp.sum(jnp.where(head, dlse2, 0.0), axis=-1, keepdims=True)
        ds = p * (dp - delta)
        dsb = ds.astype(BF16)
        dq_h = jnp.where(head, jnp.dot(dsb, k2, preferred_element_type=F32), 0.0)
        dk_h = lax.dot_general(dsb, qm, _DIMS["tn"], preferred_element_type=F32)
        dv_h = lax.dot_general(p.astype(BF16), dom, _DIMS["tn"], preferred_element_type=F32)
        dq2 = dq_h if dq2 is None else dq2 + dq_h
        dk2 = dk_h if dk2 is None else dk2 + dk_h
        dv2 = dv_h if dv2 is None else dv2 + dv_h
        ds_heads.append(ds)
    return dq2, dk2, dv2, ds_heads


def _dil_valid(r0, length):
    row = lax.broadcasted_iota(jnp.int32, (DIL_QB, DIL_WIN), 0)
    col = lax.broadcasted_iota(jnp.int32, (DIL_QB, DIL_WIN), 1)
    off = col - DIL_HALF - row
    kpos = r0 - DIL_HALF + col
    return (jnp.abs(off) <= DIL_HALF) & (kpos >= 0) & (kpos < length)


def _dil_build_bias(tab_ref, bkt_ref, bias_ref, dil):
    bkt = bkt_ref[...]
    for h in range(N_HEADS):
        acc = jnp.zeros((DIL_QB, DIL_WIN), F32)
        for b in _dil_live_buckets(dil):
            acc = jnp.where(bkt == b, tab_ref[b, h], acc)
        bias_ref[h] = acc


def _dil_fill_pad(pad_ref, src_ref, length):
    zeros = jnp.zeros((DIL_HALF, GROUP_WIDTH), pad_ref.dtype)
    pad_ref[pl.ds(0, DIL_HALF), :] = zeros
    pad_ref[pl.ds(DIL_HALF + length, DIL_HALF), :] = zeros
    pad_ref[pl.ds(DIL_HALF, length), :] = src_ref[...]


def _dil_specs(bsz, length, dil):
    view = lambda a: a.reshape(bsz, length, dil * GROUP_WIDTH)
    blk = pl.BlockSpec((None, DIL_QB, GROUP_WIDTH), lambda b, rho, i: (b, i, rho))
    seq = pl.BlockSpec((None, length, GROUP_WIDTH), lambda b, rho, i: (b, 0, rho))
    return view, blk, seq


def _dil_fwd(qb, kb, vb, table, dil, name):
    bsz = qb.shape[0]
    length = SEQ // dil
    bkt = jnp.asarray(_np_dil_buckets(dil))
    view, blk, seq = _dil_specs(bsz, length, dil)

    def body(tab_ref, bkt_ref, q_ref, k_ref, v_ref, o_ref, lse_ref, kpad, vpad, bias_ref):
        i = pl.program_id(2)

        @pl.when((pl.program_id(0) == 0) & (pl.program_id(1) == 0) & (i == 0))
        def _():
            _dil_build_bias(tab_ref, bkt_ref, bias_ref, dil)

        @pl.when(i == 0)
        def _():
            _dil_fill_pad(kpad, k_ref, length)
            _dil_fill_pad(vpad, v_ref, length)

        r0 = pl.multiple_of(i * DIL_QB, DIL_QB)
        valid = _dil_valid(r0, length)
        for m in range(N_HEADS // 2):
            lanes = pl.ds(m * 128, 128)
            o2, lse2 = _attn_pair_fwd(q_ref[:, lanes], kpad[pl.ds(r0, DIL_WIN), lanes], vpad[pl.ds(r0, DIL_WIN), lanes],
                                      (bias_ref[2 * m], bias_ref[2 * m + 1]), valid)
            o_ref[:, lanes] = o2
            lse_ref[:, lanes] = lse2

    out = jax.ShapeDtypeStruct((bsz, length, dil * GROUP_WIDTH), F32)
    o, lse = pl.pallas_call(
        body, name=name, grid=(bsz, dil, length // DIL_QB),
        in_specs=[pl.BlockSpec(memory_space=pltpu.SMEM), pl.BlockSpec(bkt.shape, lambda b, rho, i: (0, 0)), blk, seq, seq],
        out_specs=[blk, blk], out_shape=[out, out],
        scratch_shapes=[pltpu.VMEM((length + 2 * DIL_HALF, GROUP_WIDTH), BF16), pltpu.VMEM((length + 2 * DIL_HALF, GROUP_WIDTH), BF16),
                        pltpu.VMEM((N_HEADS, DIL_QB, DIL_WIN), F32)],
        compiler_params=_cparams(("arbitrary", "arbitrary", "arbitrary")),
    )(table, bkt, view(qb), view(kb), view(vb))
    return o.reshape(bsz, SEQ, GROUP_WIDTH), lse.reshape(bsz, SEQ, GROUP_WIDTH)


def _dil_bwd(qb, kb, vb, do, dlse, table, dil, name):
    bsz = qb.shape[0]
    length = SEQ // dil
    nqb = length // DIL_QB
    bkt = jnp.asarray(_np_dil_buckets(dil))
    view, blk, seq = _dil_specs(bsz, length, dil)

    def body(tab_ref, bkt_ref, q_ref, k_ref, v_ref, do_ref, dlse_ref, dq_ref, dk_ref, dv_ref, dsc_ref, kpad, vpad, bias_ref):
        i = pl.program_id(2)

        @pl.when((pl.program_id(0) == 0) & (pl.program_id(1) == 0) & (i == 0))
        def _():
            _dil_build_bias(tab_ref, bkt_ref, bias_ref, dil)
            dsc_ref[...] = jnp.zeros_like(dsc_ref)

        @pl.when(i == 0)
        def _():
            _dil_fill_pad(kpad, k_ref, length)
            _dil_fill_pad(vpad, v_ref, length)
            dk_ref[...] = jnp.zeros_like(dk_ref)
            dv_ref[...] = jnp.zeros_like(dv_ref)

        r0 = pl.multiple_of(i * DIL_QB, DIL_QB)
        valid = _dil_valid(r0, length)
        for m in range(N_HEADS // 2):
            lanes = pl.ds(m * 128, 128)
            dq2, dk2, dv2, ds_heads = _attn_pair_bwd(
                q_ref[:, lanes], kpad[pl.ds(r0, DIL_WIN), lanes], vpad[pl.ds(r0, DIL_WIN), lanes],
                (bias_ref[2 * m], bias_ref[2 * m + 1]), valid, do_ref[:, lanes], dlse_ref[:, lanes])
            dq_ref[:, lanes] = dq2
            dsc_ref[2 * m] += ds_heads[0]
            dsc_ref[2 * m + 1] += ds_heads[1]
            for first, size, live in ((0, DIL_HALF, i >= 1), (DIL_HALF, DIL_QB, None), (DIL_HALF + DIL_QB, DIL_HALF, i <= nqb - 2)):
                def add(first=first, size=size, dk2=dk2, dv2=dv2, lanes=lanes):
                    rows = pl.ds(pl.multiple_of(r0 - DIL_HALF + first, DIL_HALF), size)
                    dk_ref[rows, lanes] += dk2[first:first + size]
                    dv_ref[rows, lanes] += dv2[first:first + size]
                if live is None:
                    add()
                else:
                    pl.when(live)(add)

    out = jax.ShapeDtypeStruct((bsz, length, dil * GROUP_WIDTH), F32)
    dsc_shape = (N_HEADS, DIL_QB, DIL_WIN)
    dq, dk, dv, dsc = pl.pallas_call(
        body, name=name, grid=(bsz, dil, nqb),
        in_specs=[pl.BlockSpec(memory_space=pltpu.SMEM), pl.BlockSpec(bkt.shape, lambda b, rho, i: (0, 0)), blk, seq, seq, blk, blk],
        out_specs=[blk, seq, seq, pl.BlockSpec(dsc_shape, lambda b, rho, i: (0, 0, 0))],
        out_shape=[out, out, out, jax.ShapeDtypeStruct(dsc_shape, F32)],
        scratch_shapes=[pltpu.VMEM((length + 2 * DIL_HALF, GROUP_WIDTH), BF16), pltpu.VMEM((length + 2 * DIL_HALF, GROUP_WIDTH), BF16),
                        pltpu.VMEM(dsc_shape, F32)],
        compiler_params=_cparams(("arbitrary", "arbitrary", "arbitrary")),
    )(table, bkt, view(qb), view(kb), view(vb), view(do), view(dlse))
    shp = (bsz, SEQ, GROUP_WIDTH)
    return dq.reshape(shp), dk.reshape(shp), dv.reshape(shp), dsc


def _headnorm(x, g, mavg):
    return x * lax.rsqrt(_head_stat(x * x, mavg) + RMS_EPS) * g


def _fold_gain(dg_full, fold):
    return jnp.dot(jnp.broadcast_to(dg_full, (8, dg_full.shape[1])), fold, precision=HI, preferred_element_type=F32)


def _bprep_fn(qp, kp, gq, gk, mavg):
    return _headnorm(qp, gq, mavg) * (HEAD_DIM ** -0.5), _headnorm(kp, gk, mavg)


def _bprep_fwd(z, gq, gk, name):
    mavg = jnp.asarray(_np_group_avg(GROUP_WIDTH))

    def fn(qp, kp, vp, gqv, gkv, mv):
        qb, kb = _bprep_fn(qp, kp, gqv, gkv, mv)
        return qb, kb, vp

    w = GROUP_WIDTH
    return _rowmap(fn, [(z, w, 2), (z, w, 3), (z, w, 4)], [gq, gk, mavg], [(w, BF16)] * 3, [], name, 512, z.shape[0])


def _bprep_bwd(z, dqs, dks, dvs, gq, gk, name):
    mavg = jnp.asarray(_np_group_avg(GROUP_WIDTH))
    fold = jnp.asarray(_np_tile_fold(GROUP_WIDTH))

    def fn(qp, kp, dq0, dq1, dq2, dk0, dk1, dk2, dv0, dv1, dv2, gqv, gkv, mv, fv):
        _, vjp = jax.vjp(functools.partial(_bprep_fn, mavg=mv), qp, kp, gqv, gkv)
        dqp, dkp, dgq, dgk = vjp((dq0 + dq1 + dq2, dk0 + dk1 + dk2))
        return dqp, dkp, dv0 + dv1 + dv2, _fold_gain(dgq, fv), _fold_gain(dgk, fv)

    w = GROUP_WIDTH
    rows = [(z, w, 2), (z, w, 3)] + [(a, w, 0) for a in (*dqs, *dks, *dvs)]
    return _rowmap(fn, rows, [gq, gk, mavg, fold], [(w, F32)] * 3, [(8, HEAD_DIM)] * 2, name, 256, z.shape[0])


def _mixture_fn(o0, o1, o2, l0, l1, l2):
    m = lax.stop_gradient(jnp.maximum(jnp.maximum(l0, l1), l2))
    e0, e1, e2 = jnp.exp(l0 - m), jnp.exp(l1 - m), jnp.exp(l2 - m)
    return (e0 * o0 + e1 * o1 + e2 * o2) / (e0 + e1 + e2)


def _mixture_fwd(os_, ls_, name):
    w = GROUP_WIDTH
    rows = [(a, w, 0) for a in (*os_, *ls_)]
    return _rowmap(lambda *v: (_mixture_fn(*v),), rows, [], [(w, F32)], [], name, 512, os_[0].shape[0])[0]


def _mixture_bwd(os_, ls_, dy, name):
    w = GROUP_WIDTH

    def fn(*v):
        _, vjp = jax.vjp(_mixture_fn, *v[:6])
        return vjp(v[6])

    rows = [(a, w, 0) for a in (*os_, *ls_, dy)]
    return _rowmap(fn, rows, [], [(w, F32)] * 6, [], name, 512, dy.shape[0])


def _relbias_fold(dscs, name):
    bkts = [jnp.asarray(_np_dil_buckets(dil)) for _, dil in DIL_PATTERNS]
    npat = len(DIL_PATTERNS)

    def body(*refs):
        bkt_refs, d_refs, o_ref = refs[:npat], refs[npat:-1], refs[-1]
        row = lax.broadcasted_iota(jnp.int32, (REL_BUCKETS, 128), 0)
        lane = lax.broadcasted_iota(jnp.int32, (REL_BUCKETS, 128), 1)
        out = jnp.zeros((REL_BUCKETS, 128), F32)
        for p, (_, dil) in enumerate(DIL_PATTERNS):
            bkt = bkt_refs[p][...]
            for h in range(N_HEADS):
                d = d_refs[2 * p][h] + d_refs[2 * p + 1][h]
                for b in _dil_live_buckets(dil):
                    val = jnp.sum(jnp.where(bkt == b, d, 0.0), keepdims=True)
                    out = out + jnp.where((row == b) & (lane == h), val, 0.0)
        o_ref[...] = out

    return pl.pallas_call(
        body, name=name, out_shape=jax.ShapeDtypeStruct((REL_BUCKETS, 128), F32), compiler_params=_cparams(),
    )(*bkts, *dscs)


DPREP_TM = 512


def _dprep_fn(qp, kp, vp, gq, gk, cq, sq, ck, sk, mavg_q, mavg_k, perm_q, perm_k, expand):
    rot = lambda x, perm: jnp.dot(x, perm, precision=HI, preferred_element_type=F32)
    qn = _headnorm(qp, gq, mavg_q)
    kn = _headnorm(kp, gk, mavg_k)
    qr = (qn * cq + rot(qn, perm_q) * sq) * (HEAD_DIM ** -0.5)
    kr = kn * ck + rot(kn, perm_k) * sk
    return qr, rot(kr, expand), rot(vp, expand)


def _dprep_consts():
    cq, sq = _np_rope_tables(N_HEADS)
    ck, sk = _np_rope_tables(KV_WIDTH // HEAD_DIM)
    tables = [jnp.asarray(a) for a in (cq, sq, ck, sk)]
    mats = [jnp.asarray(a) for a in (_np_group_avg(GROUP_WIDTH), _np_group_avg(KV_WIDTH), _np_rope_partner(GROUP_WIDTH),
                                      _np_rope_partner(KV_WIDTH), _np_kv_expand())]
    per = SEQ // DPREP_TM
    w, kw = GROUP_WIDTH, KV_WIDTH
    table_rows = [(tables[0], w, 0, per), (tables[1], w, 0, per), (tables[2], kw, 0, per), (tables[3], kw, 0, per)]
    return table_rows, mats


def _dprep_fwd(z, gq, gk, name):
    table_rows, mats = _dprep_consts()
    w, kw = GROUP_WIDTH, KV_WIDTH

    def fn(qp, kp, vp, cq, sq, ck, sk, gqv, gkv, *m):
        return _dprep_fn(qp, kp, vp, gqv, gkv, cq, sq, ck, sk, *m)

    return _rowmap(fn, [(z, w, 7), (z, kw, 32), (z, kw, 33)] + table_rows, [gq, gk] + mats, [(w, BF16)] * 3, [], name,
                   DPREP_TM, z.shape[0])


def _dprep_bwd(z, dq, dkx, dvx, gq, gk, name):
    table_rows, mats = _dprep_consts()
    fold_q = jnp.asarray(_np_tile_fold(GROUP_WIDTH))
    fold_k = jnp.asarray(_np_tile_fold(KV_WIDTH))
    w, kw = GROUP_WIDTH, KV_WIDTH

    def fn(qp, kp, vp, dqv, dkv, dvv, cq, sq, ck, sk, gqv, gkv, fq, fk, *m):
        f = lambda a, b, c, d, e: _dprep_fn(a, b, c, d, e, cq, sq, ck, sk, *m)
        _, vjp = jax.vjp(f, qp, kp, vp, gqv, gkv)
        dqp, dkp, dvp, dgq, dgk = vjp((dqv, dkv, dvv))
        return dqp, dkp, dvp, _fold_gain(dgq, fq), _fold_gain(dgk, fk)

    return _rowmap(fn, [(z, w, 7), (z, kw, 32), (z, kw, 33), (dq, w, 0), (dkx, w, 0), (dvx, w, 0)] + table_rows,
                   [gq, gk, fold_q, fold_k] + mats, [(w, F32), (kw, F32), (kw, F32)], [(8, HEAD_DIM)] * 2, name,
                   DPREP_TM, z.shape[0])


GQA_QB = 256


def _gqa_fwd(q, kx, vx, name):
    bsz = q.shape[0]
    blk = pl.BlockSpec((None, GQA_QB, GROUP_WIDTH), lambda b, i: (b, i, 0))
    seq = pl.BlockSpec((None, SEQ, GROUP_WIDTH), lambda b, i: (b, 0, 0))

    def body(q_ref, k_ref, v_ref, o_ref):
        for m in range(N_HEADS // 2):
            lanes = pl.ds(m * 128, 128)
            o_ref[:, lanes] = _attn_pair_fwd(q_ref[:, lanes], k_ref[:, lanes], v_ref[:, lanes], None, None)[0]

    return pl.pallas_call(
        body, name=name, grid=(bsz, SEQ // GQA_QB), in_specs=[blk, seq, seq], out_specs=blk,
        out_shape=jax.ShapeDtypeStruct((bsz, SEQ, GROUP_WIDTH), F32), compiler_params=_cparams(("parallel", "parallel")),
    )(q, kx, vx)


def _gqa_bwd(q, kx, vx, do, name):
    bsz = q.shape[0]
    blk = pl.BlockSpec((None, GQA_QB, GROUP_WIDTH), lambda b, i: (b, i, 0))
    seq = pl.BlockSpec((None, SEQ, GROUP_WIDTH), lambda b, i: (b, 0, 0))

    def body(q_ref, k_ref, v_ref, do_ref, dq_ref, dk_ref, dv_ref):
        @pl.when(pl.program_id(1) == 0)
        def _():
            dk_ref[...] = jnp.zeros_like(dk_ref)
            dv_ref[...] = jnp.zeros_like(dv_ref)

        for m in range(N_HEADS // 2):
            lanes = pl.ds(m * 128, 128)
            dq2, dk2, dv2, _ = _attn_pair_bwd(q_ref[:, lanes], k_ref[:, lanes], v_ref[:, lanes], None, None, do_ref[:, lanes], None)
            dq_ref[:, lanes] = dq2
            dk_ref[:, lanes] += dk2
            dv_ref[:, lanes] += dv2

    out = jax.ShapeDtypeStruct((bsz, SEQ, GROUP_WIDTH), F32)
    return pl.pallas_call(
        body, name=name, grid=(bsz, SEQ // GQA_QB), in_specs=[blk, seq, seq, blk], out_specs=[blk, seq, seq],
        out_shape=[out, out, out], compiler_params=_cparams(("parallel", "arbitrary")),
    )(q, kx, vx, do)


CONV_TILE = 64
CONV_LEAD = 16
CONV_WINDOW = CONV_TILE + 32


def _glu(a, g):
    return a * jax.nn.sigmoid(g)


def _conv_post(c, b, ln_g, ln_b):
    x = c + b
    xc = x - jnp.mean(x, axis=-1, keepdims=True)
    y = xc * lax.rsqrt(jnp.mean(xc * xc, axis=-1, keepdims=True) + LN_EPS) * ln_g + ln_b
    return y * jax.nn.sigmoid(y)


def _conv_shifted(win, offset):
    return pltpu.roll(win, CONV_WINDOW - offset, 0)[:CONV_TILE]


def _conv_fill(pad_ref, value_of_tile):
    zeros = jnp.zeros((CONV_LEAD, GROUP_WIDTH), F32)
    pad_ref[pl.ds(0, CONV_LEAD), :] = zeros
    pad_ref[pl.ds(CONV_LEAD + SEQ, CONV_LEAD), :] = zeros

    def step(t, carry):
        r0 = pl.multiple_of(t * CONV_TILE, CONV_TILE)
        pad_ref[pl.ds(CONV_LEAD + r0, CONV_TILE), :] = value_of_tile(r0)
        return carry

    lax.fori_loop(0, SEQ // CONV_TILE, step, 0)


def _conv_tile(pad_ref, w_ref, r0, flip):
    win = pad_ref[pl.ds(r0, CONV_WINDOW), :]
    acc = jnp.zeros((CONV_TILE, GROUP_WIDTH), F32)
    for k in range(CONV_WIDTH):
        offset = (CONV_WIDTH - k) if flip else (k + 1)
        acc = acc + w_ref[pl.ds(k, 1), :] * _conv_shifted(win, offset)
    return acc


def _conv_fwd(z3, w, b, ln_g, ln_b, name):
    bsz = z3.shape[0]
    seq = lambda cb: pl.BlockSpec((None, SEQ, GROUP_WIDTH), functools.partial(lambda i, cb: (i, 0, cb), cb=cb))
    full = lambda a: pl.BlockSpec(a.shape, lambda i: (0,) * a.ndim)

    def body(a_ref, g_ref, w_ref, b_ref, lg_ref, lb_ref, y_ref, pad_ref):
        _conv_fill(pad_ref, lambda r0: _glu(a_ref[pl.ds(r0, CONV_TILE), :], g_ref[pl.ds(r0, CONV_TILE), :]))

        def step(t, carry):
            r0 = pl.multiple_of(t * CONV_TILE, CONV_TILE)
            y_ref[pl.ds(r0, CONV_TILE), :] = _conv_post(_conv_tile(pad_ref, w_ref, r0, False), b_ref[...], lg_ref[...], lb_ref[...])
            return carry

        lax.fori_loop(0, SEQ // CONV_TILE, step, 0)

    return pl.pallas_call(
        body, name=name, grid=(bsz,), in_specs=[seq(5), seq(6), full(w), full(b), full(ln_g), full(ln_b)], out_specs=seq(0),
        out_shape=jax.ShapeDtypeStruct((bsz, SEQ, GROUP_WIDTH), F32),
        scratch_shapes=[pltpu.VMEM((SEQ + 2 * CONV_LEAD, GROUP_WIDTH), F32)], compiler_params=_cparams(("parallel",)),
    )(z3, z3, w, b, ln_g, ln_b)


def _conv_bwd(z3, dy, w, b, ln_g, ln_b, name):
    bsz = z3.shape[0]
    seq = lambda cb: pl.BlockSpec((None, SEQ, GROUP_WIDTH), functools.partial(lambda i, cb: (i, 0, cb), cb=cb))
    full = lambda a: pl.BlockSpec(a.shape, lambda i: (0,) * a.ndim)
    vec = pl.BlockSpec((1, GROUP_WIDTH), lambda i: (0, 0))

    def body(a_ref, g_ref, dy_ref, w_ref, b_ref, lg_ref, lb_ref, da_ref, dg_ref, dw_ref, db_ref, dlg_ref, dlb_ref, hpad, dpad, dw8):
        @pl.when(pl.program_id(0) == 0)
        def _():
            dw8[...] = jnp.zeros_like(dw8)
            db_ref[...] = jnp.zeros_like(db_ref)
            dlg_ref[...] = jnp.zeros_like(dlg_ref)
            dlb_ref[...] = jnp.zeros_like(dlb_ref)

        _conv_fill(hpad, lambda r0: _glu(a_ref[pl.ds(r0, CONV_TILE), :], g_ref[pl.ds(r0, CONV_TILE), :]))
        zeros = jnp.zeros((CONV_LEAD, GROUP_WIDTH), F32)
        dpad[pl.ds(0, CONV_LEAD), :] = zeros
        dpad[pl.ds(CONV_LEAD + SEQ, CONV_LEAD), :] = zeros

        def through_post(t, carry):
            r0 = pl.multiple_of(t * CONV_TILE, CONV_TILE)
            conv = _conv_tile(hpad, w_ref, r0, False)
            _, vjp = jax.vjp(_conv_post, conv, b_ref[...], lg_ref[...], lb_ref[...])
            dconv, db, dlg, dlb = vjp(dy_ref[pl.ds(r0, CONV_TILE), :])
            db_ref[...] += db
            dlg_ref[...] += dlg
            dlb_ref[...] += dlb
            dpad[pl.ds(CONV_LEAD + r0, CONV_TILE), :] = dconv
            win = hpad[pl.ds(r0, CONV_WINDOW), :]
            for k in range(CONV_WIDTH):
                prod = dconv * _conv_shifted(win, k + 1)
                part = prod[0:8]
                for j in range(1, CONV_TILE // 8):
                    part = part + prod[8 * j:8 * j + 8]
                dw8[k] += part
            return carry

        lax.fori_loop(0, SEQ // CONV_TILE, through_post, 0)

        def through_glu(t, carry):
            r0 = pl.multiple_of(t * CONV_TILE, CONV_TILE)
            dh = _conv_tile(dpad, w_ref, r0, True)
            rows = pl.ds(r0, CONV_TILE)
            _, vjp = jax.vjp(_glu, a_ref[rows, :], g_ref[rows, :])
            da, dg = vjp(dh)
            da_ref[rows, :] = da
            dg_ref[rows, :] = dg
            return carry

        lax.fori_loop(0, SEQ // CONV_TILE, through_glu, 0)
        dw_ref[...] = jnp.sum(dw8[...], axis=1)

    out = jax.ShapeDtypeStruct((bsz, SEQ, GROUP_WIDTH), F32)
    v = jax.ShapeDtypeStruct((1, GROUP_WIDTH), F32)
    return pl.pallas_call(
        body, name=name, grid=(bsz,), in_specs=[seq(5), seq(6), seq(0), full(w), full(b), full(ln_g), full(ln_b)],
        out_specs=[seq(0), seq(0), pl.BlockSpec((CONV_WIDTH, GROUP_WIDTH), lambda i: (0, 0)), vec, vec, vec],
        out_shape=[out, out, jax.ShapeDtypeStruct((CONV_WIDTH, GROUP_WIDTH), F32), v, v, v],
        scratch_shapes=[pltpu.VMEM((SEQ + 2 * CONV_LEAD, GROUP_WIDTH), F32), pltpu.VMEM((SEQ + 2 * CONV_LEAD, GROUP_WIDTH), F32),
                        pltpu.VMEM((CONV_WIDTH, 8, GROUP_WIDTH), F32)],
        compiler_params=_cparams(("arbitrary",)),
    )(z3, z3, dy, w, b, ln_g, ln_b)


def _mixnorm_fwd(ys, gains, name):
    w = GROUP_WIDTH

    def fn(*v):
        return (jnp.concatenate([_rms(v[i], v[4 + i]) for i in range(4)], axis=-1),)

    return _rowmap(fn, [(y, w, 0) for y in ys], list(gains), [(4 * w, BF16)], [], name, 512, ys[0].shape[0])[0]


def _mixnorm_bwd(dyn, ys, gains, name, follow=None):
    w = GROUP_WIDTH
    gains = list(gains) if follow is None else [*gains, follow]

    def fn(*v):
        dys, dgs = [], []
        for i in range(4):
            _, vjp = jax.vjp(_rms, v[4 + i], v[8 + i])
            dy, dg = vjp(v[i])
            dys.append(dy)
            dgs.append(dg)
        return (*dys, *dgs)

    rows = [(dyn, w, i) for i in range(4)] + [(y, w, 0) for y in ys]
    return _rowmap(fn, rows, list(gains), [(w, F32)] * 4, [(1, w)] * 4, name, 512, dyn.shape[0])


def _adamw_fn(w, g, m, v):
    m = ADAM_B1 * m + (1.0 - ADAM_B1) * g
    v = ADAM_B2 * v + (1.0 - ADAM_B2) * (g * g)
    m_hat = m / (1.0 - ADAM_B1 ** ADAM_STEP)
    v_hat = v / (1.0 - ADAM_B2 ** ADAM_STEP)
    delta = -ADAM_LR * (m_hat / (jnp.sqrt(v_hat) + ADAM_EPS) + ADAM_WD * w)
    return delta, m, v


def _adamw(w, g, m, v, name):
    r, c = w.shape
    tm = _pick(r, (256, 128, 64, 32, 16, 8))
    return _rowmap(_adamw_fn, [(a, c, 0) for a in (w, g, m, v)], [], [(c, F32)] * 3, [], name, tm, r)


def _layer_params(l, small, big):
    tile_row = lambda g, n: jnp.tile(g, n)[None, :]
    row = lambda g: g[None, :]
    w_s = small["sgu_w"][l].astype(BF16)
    return dict(
        norm1_g=row(small["norm1_g"][l]), norm2_g=row(small["norm2_g"][l]),
        w_s=w_s, w_s_t=jnp.swapaxes(w_s, 1, 2), bm=jnp.repeat(small["sgu_b"][l].T, HEAD_DIM, axis=1),
        gq_dil=tile_row(small["dil_qn_g"][l], N_HEADS), gk_dil=tile_row(small["dil_kn_g"][l], N_HEADS),
        conv_w=small["conv_w"][l], conv_b=row(small["conv_b"][l]), conv_ln_g=row(small["conv_ln_g"][l]),
        conv_ln_b=row(small["conv_ln_b"][l]),
        gq_gqa=tile_row(small["gqa_qn_g"][l], N_HEADS), gk_gqa=tile_row(small["gqa_kn_g"][l], KV_WIDTH // HEAD_DIM),
        mix_g=[row(small["mix_norm_g"][l][i * GROUP_WIDTH:(i + 1) * GROUP_WIDTH]) for i in range(4)],
        big=big,
    )


def _layer_fwd(x, p, table, bsz, tag):
    t = x.shape[0]
    seq3 = lambda a: a.reshape(bsz, SEQ, a.shape[-1])
    flat = lambda a: a.reshape(t, a.shape[-1])
    h1 = _rmsnorm_fwd(x, p["norm1_g"], tag + "rms1")
    z = _matmul([(h1, p["big"]("w_in", h1))], "nn", F32, tag + "mm_z")
    y_a = _sgu_fwd(z, p["w_s"], p["bm"], tag + "sgu_fwd")
    qb, kb, vb = _bprep_fwd(z, p["gq_dil"], p["gk_dil"], tag + "dil_prep")
    outs, lses = [], []
    for _, dil in DIL_PATTERNS:
        o, lse = _dil_fwd(seq3(qb), seq3(kb), seq3(vb), table, dil, f"{tag}dil{dil}_fwd")
        outs.append(flat(o))
        lses.append(flat(lse))
    y_b = _mixture_fwd(outs, lses, tag + "dil_mix")
    y_c = flat(_conv_fwd(seq3(z), p["conv_w"], p["conv_b"], p["conv_ln_g"], p["conv_ln_b"], tag + "conv_fwd"))
    qd, kx, vx = _dprep_fwd(z, p["gq_gqa"], p["gk_gqa"], tag + "gqa_prep")
    y_d = flat(_gqa_fwd(seq3(qd), seq3(kx), seq3(vx), tag + "gqa_fwd"))
    ys = [y_a, y_b, y_c, y_d]
    yn = _mixnorm_fwd(ys, p["mix_g"], tag + "mixnorm")
    x_mid = _matmul([(yn, p["big"]("w_out", yn))], "nn", F32, tag + "mm_out", residual=x)
    h2 = _rmsnorm_fwd(x_mid, p["norm2_g"], tag + "rms2")
    act, act_du, act_dg = _ffn_up(h2, p["big"]("w_gate", yn), p["big"]("w_up", yn), tag + "ffn_up")
    x_out = _matmul([(act, p["big"]("w_down", yn))], "nn", F32, tag + "mm_down", residual=x_mid)
    saved = dict(x=x, h1=h1, z=z, qb=qb, kb=kb, vb=vb, outs=outs, lses=lses, qd=qd, kx=kx, vx=vx, ys=ys, yn=yn, x_mid=x_mid,
                 h2=h2, act=act, act_du=act_du, act_dg=act_dg)
    return x_out, saved


def _layer_bwd(dx_out, dx_out_b, s, p, table, bsz, tag, emit, mid_hook):
    t = dx_out.shape[0]
    seq3 = lambda a: a.reshape(bsz, SEQ, a.shape[-1])
    flat = lambda a: a.reshape(t, a.shape[-1])
    z = s["z"]
    small = {}
    weight = lambda name: p["big"](name, None)
    emit("w_down", _matmul([(s["act"], dx_out_b)], "tn", BF16, tag + "mm_dwdown").reshape(N_CHIPS, FFN_HIDDEN // N_CHIPS, D_MODEL))
    dgate, dup = _ffn_down_bwd(dx_out_b, weight("w_down"), s["act_du"], s["act_dg"], tag + "ffn_dact")
    emit("w_gate", _matmul([(s["h2"], dgate)], "tn", BF16, tag + "mm_dwgate", slabs=N_CHIPS))
    started = emit("w_up", _matmul([(s["h2"], dup)], "tn", BF16, tag + "mm_dwup", slabs=N_CHIPS))
    dh2 = _matmul([(dgate, weight("w_gate")), (dup, weight("w_up"))], "nt", F32, tag + "mm_dh2")
    dx_mid, dx_mid_b, dg2 = _rmsnorm_bwd(dh2, s["x_mid"], p["norm2_g"], dx_out, tag + "rms2_bwd", follow=started)
    small["norm2_g"] = dg2[0]
    mid_hook(dx_mid)
    dyn = _matmul([(dx_mid_b, weight("w_out"))], "nt", F32, tag + "mm_dyn")
    started = emit("w_out", _matmul([(s["yn"], dx_mid_b)], "tn", BF16, tag + "mm_dwout").reshape(N_CHIPS, D_MODEL // N_CHIPS, D_MODEL))
    *dys, dga, dgb, dgc, dgd = _mixnorm_bwd(dyn, s["ys"], p["mix_g"], tag + "mixnorm_bwd", follow=started)
    small["mix_norm_g"] = jnp.concatenate([dga[0], dgb[0], dgc[0], dgd[0]])
    du, dv, dws, dbs = _sgu_bwd(z, dys[0], p["w_s"], p["w_s_t"], p["bm"], tag + "sgu_bwd")
    small["sgu_w"] = dws
    small["sgu_b"] = dbs[:, :N_HEADS].T
    *douts, dl0, dl1, dl2 = _mixture_bwd(s["outs"], s["lses"], dys[1], tag + "dil_mix_bwd")
    dlses = [dl0, dl1, dl2]
    dqs, dks, dvs, dscs = [], [], [], []
    for i, (_, dil) in enumerate(DIL_PATTERNS):
        dq, dk, dvv, dsc = _dil_bwd(seq3(s["qb"]), seq3(s["kb"]), seq3(s["vb"]), seq3(douts[i]), seq3(dlses[i]), table, dil,
                                    f"{tag}dil{dil}_bwd")
        dqs.append(flat(dq))
        dks.append(flat(dk))
        dvs.append(flat(dvv))
        dscs.append(dsc)
    dbq, dbk, dbv, dgq, dgk = _bprep_bwd(z, dqs, dks, dvs, p["gq_dil"], p["gk_dil"], tag + "dil_prep_bwd")
    small["dil_qn_g"], small["dil_kn_g"] = dgq[0], dgk[0]
    dca, dcg, dcw, dcb, dclg, dclb = _conv_bwd(seq3(z), seq3(dys[2]), p["conv_w"], p["conv_b"], p["conv_ln_g"], p["conv_ln_b"],
                                               tag + "conv_bwd")
    small["conv_w"], small["conv_b"], small["conv_ln_g"], small["conv_ln_b"] = dcw, dcb[0], dclg[0], dclb[0]
    dqd, dkx, dvx = _gqa_bwd(seq3(s["qd"]), seq3(s["kx"]), seq3(s["vx"]), seq3(dys[3]), tag + "gqa_bwd")
    ddq, ddk, ddv, dgq, dgk = _dprep_bwd(z, flat(dqd), flat(dkx), flat(dvx), p["gq_gqa"], p["gk_gqa"], tag + "gqa_prep_bwd")
    small["gqa_qn_g"], small["gqa_kn_g"] = dgq[0], dgk[0]
    dz = jnp.concatenate([a.astype(BF16) for a in (du, dv, dbq, dbk, dbv, flat(dca), flat(dcg), ddq, ddk, ddv)], axis=1)
    dz4 = dz.reshape(t, N_CHIPS, IN_WIDTH // N_CHIPS).transpose(1, 0, 2)
    started = emit("w_in", _matmul([(s["h1"], dz4)], "tn", BF16, tag + "mm_dwin", slabs=N_CHIPS))
    dh1 = _matmul([(dz, weight("w_in"))], "nt", F32, tag + "mm_dh1")
    dx, dx_b, dg1 = _rmsnorm_bwd(dh1, s["x"], p["norm1_g"], dx_mid, tag + "rms1_bwd", follow=started)
    small["norm1_g"] = dg1[0]
    return dx, dx_b, small, dscs


def _local_step(x, target, small, big, emit, mid_hook, bsz):
    table = small["rel_bias"]
    params = [_layer_params(l, small, functools.partial(big, l)) for l in range(DEPTH)]
    saved = []
    h = x
    for l in range(DEPTH):
        h, sv = _layer_fwd(h, params[l], table, bsz, f"l{l}_")
        saved.append(sv)
    dh, dh_b, loss = _loss_fwd_bwd(h, target, "loss")
    small_grads, dscs = [None] * DEPTH, [None] * DEPTH
    for l in reversed(range(DEPTH)):
        dh, dh_b, small_grads[l], dscs[l] = _layer_bwd(dh, dh_b, saved[l], params[l], table, bsz, f"l{l}_",
                                                       functools.partial(emit, l), functools.partial(mid_hook, l))
    fold_in = [dscs[l][i] for i in range(len(DIL_PATTERNS)) for l in range(DEPTH)]
    stacked = {k: jnp.stack([small_grads[l][k] for l in range(DEPTH)]) for k in small_grads[0]}
    stacked["rel_bias"] = _relbias_fold(fold_in, "relbias_fold")[:, :N_HEADS]
    return loss, dh, stacked


def _mesh_pos():
    return lax.axis_index("x"), lax.axis_index("y"), lax.axis_index("c")


def _other_chips(x, y):
    return [(1 - x, y), (x, 1 - y), (1 - x, 1 - y)]


_ANY = pl.BlockSpec(memory_space=pl.ANY)


def _swap_other_half(arrs, name):
    n = len(arrs)

    def body(*refs):
        in_refs, out_refs, send_sems, recv_sems = refs[:n], refs[n:2 * n], refs[2 * n], refs[2 * n + 1]
        x, y, c = _mesh_pos()
        copies = []
        for k in range(n):
            h = arrs[k].shape[1] // 2
            copies.append(pltpu.make_async_remote_copy(
                src_ref=in_refs[k].at[:, pl.ds((1 - c) * h, h)], dst_ref=out_refs[k], send_sem=send_sems.at[k],
                recv_sem=recv_sems.at[k], device_id=(x, y, 1 - c), device_id_type=MESH))
        for cp in copies:
            cp.start()
        for cp in copies:
            cp.wait()

    return pl.pallas_call(
        body, name=name, in_specs=[_ANY] * n, out_specs=[_ANY] * n,
        out_shape=[jax.ShapeDtypeStruct((a.shape[0], a.shape[1] // 2, a.shape[2]), a.dtype) for a in arrs],
        scratch_shapes=[pltpu.SemaphoreType.DMA((n,)), pltpu.SemaphoreType.DMA((n,))],
    )(*arrs)


def _swap_sibling(arrs, name):
    n = len(arrs)

    def body(*refs):
        in_refs, out_refs, send_sems, recv_sems = refs[:n], refs[n:2 * n], refs[2 * n], refs[2 * n + 1]
        x, y, c = _mesh_pos()
        copies = [pltpu.make_async_remote_copy(src_ref=in_refs[k], dst_ref=out_refs[k], send_sem=send_sems.at[k],
                                               recv_sem=recv_sems.at[k], device_id=(x, y, 1 - c), device_id_type=MESH)
                  for k in range(n)]
        for cp in copies:
            cp.start()
        for cp in copies:
            cp.wait()

    return pl.pallas_call(
        body, name=name, in_specs=[_ANY] * n, out_specs=[_ANY] * n,
        out_shape=[jax.ShapeDtypeStruct(a.shape, a.dtype) for a in arrs],
        scratch_shapes=[pltpu.SemaphoreType.DMA((n,)), pltpu.SemaphoreType.DMA((n,))],
    )(*arrs)


def _complete_pairs(arrs, name):
    n = len(arrs)

    def body(*refs):
        in_refs, out_refs, send_sems, recv_sems = refs[:n], refs[n:2 * n], refs[2 * n], refs[2 * n + 1]
        x, y, c = _mesh_pos()
        copies = [pltpu.make_async_remote_copy(src_ref=in_refs[k].at[c], dst_ref=out_refs[k].at[c], send_sem=send_sems.at[k],
                                               recv_sem=recv_sems.at[k], device_id=(x, y, 1 - c), device_id_type=MESH)
                  for k in range(n)]
        for cp in copies:
            cp.start()
        for k, cp in enumerate(copies):
            cp.wait_send()
            pltpu.make_async_remote_copy(src_ref=in_refs[k].at[1 - c], dst_ref=out_refs[k].at[1 - c], send_sem=send_sems.at[k],
                                         recv_sem=recv_sems.at[k], device_id=(x, y, 1 - c), device_id_type=MESH).wait_recv()

    return pl.pallas_call(
        body, name=name, in_specs=[_ANY] * n, out_specs=[_ANY] * n,
        out_shape=[jax.ShapeDtypeStruct(a.shape, a.dtype) for a in arrs], input_output_aliases={k: k for k in range(n)},
        scratch_shapes=[pltpu.SemaphoreType.DMA((n,)), pltpu.SemaphoreType.DMA((n,))],
    )(*arrs)


_HBM = pl.BlockSpec(memory_space=pltpu.HBM)
_SEM = pl.BlockSpec(memory_space=pltpu.SEMAPHORE)
_DATAFLOW = pltpu.SideEffectType.DATAFLOW_SIDE_EFFECTING


def _chip_copies(src_refs, land_refs, send_sems, recv_sems, scatter):
    x, y, c = _mesh_pos()
    me = 2 * x + y
    out = []
    for k, (src_ref, land_ref) in enumerate(zip(src_refs, land_refs)):
        slot = (lambda chip: land_ref.at[c, chip]) if len(land_ref.shape) == 4 else (lambda chip: land_ref.at[chip])
        for j, (cx, cy) in enumerate(_other_chips(x, y)):
            there = 2 * cx + cy
            src = src_ref.at[there] if scatter else src_ref
            sems = dict(send_sem=send_sems.at[3 * k + j], recv_sem=recv_sems.at[3 * k + j], device_id=(cx, cy, c), device_id_type=MESH)
            out.append((pltpu.make_async_remote_copy(src_ref=src, dst_ref=slot(me), **sems),
                        pltpu.make_async_remote_copy(src_ref=src, dst_ref=slot(there), **sems)))
    return out


def _chips_start(srcs, scatter, after, name, per_core=False):
    n = len(srcs)
    lead = (2, N_CHIPS) if per_core else (N_CHIPS,)
    lands = [lax.empty((*lead, *s.shape[-2:]), s.dtype) for s in srcs]

    def body(*refs):
        src_refs, land_refs = refs[:n], refs[n:2 * n]
        send_sems, recv_sems, token = refs[2 * n + 1], refs[2 * n + 2], refs[-1]
        for sent, _ in _chip_copies(src_refs, land_refs, send_sems, recv_sems, scatter):
            sent.start()
        token[...] = jnp.zeros_like(token)

    hbm = lambda a: pltpu.HBM(a.shape, a.dtype)
    res = pl.pallas_call(
        body, name=name,
        in_specs=[_HBM] * (2 * n) + [_ANY],
        out_specs=[_SEM, _SEM] + [_HBM] * (2 * n) + [pl.BlockSpec(memory_space=pltpu.VMEM)],
        out_shape=[pltpu.SemaphoreType.DMA((3 * n,)), pltpu.SemaphoreType.DMA((3 * n,))] + [hbm(a) for a in srcs] + [hbm(a) for a in lands]
        + [jax.ShapeDtypeStruct((8, 128), F32)],
        input_output_aliases={i: 2 + i for i in range(2 * n)},
        compiler_params=pltpu.CompilerParams(has_side_effects=_DATAFLOW),
    )(*[pltpu.with_memory_space_constraint(a, pltpu.HBM) for a in (*srcs, *lands)], after)
    return (res[0], res[1], res[2:2 + n], res[2 + n:2 + 2 * n]), res[-1]


def _chips_wait(handle, scatter, after, name):
    send_sems, recv_sems, srcs, lands = handle
    n = len(srcs)

    def body(*refs):
        src_refs, land_refs = refs[:n], refs[n:2 * n]
        send_sems, recv_sems = refs[2 * n], refs[2 * n + 1]
        for sent, landed in _chip_copies(src_refs, land_refs, send_sems, recv_sems, scatter):
            sent.wait_send()
            landed.wait_recv()

    hbm = lambda a: pltpu.HBM(a.shape, a.dtype)
    res = pl.pallas_call(
        body, name=name,
        in_specs=[_HBM] * (2 * n) + [_SEM, _SEM, _ANY], out_specs=[_HBM] * (2 * n),
        out_shape=[hbm(a) for a in srcs] + [hbm(a) for a in lands],
        input_output_aliases={i: i for i in range(2 * n)},
        compiler_params=pltpu.CompilerParams(has_side_effects=_DATAFLOW),
    )(*srcs, *lands, send_sems, recv_sems, after)
    return res[n:]


N_DEV = 8


def _allgather_sum_small(block, name):
    m_per, n = block.shape

    def body(x_ref, out_ref, sum_ref, send_sems, recv_sems, local_sem):
        x, y, c = _mesh_pos()
        me, sibling = (x, y, c), (x, y, 1 - c)
        chips = _other_chips(x, y)

        def rows(px, py, pc):
            return out_ref.at[pl.ds((4 * px + 2 * py + pc) * m_per, m_per), :]

        def copy(k, blk, to, src=None):
            return pltpu.make_async_remote_copy(src_ref=rows(*blk) if src is None else src, dst_ref=rows(*blk),
                                                send_sem=send_sems.at[k], recv_sem=recv_sems.at[k], device_id=to, device_id_type=MESH)

        mine = pltpu.make_async_copy(x_ref, rows(*me), local_sem)
        mine.start()
        first = [copy(0, me, sibling, src=x_ref)]
        first += [copy(1 + j, me, (*chip, c), src=x_ref) for j, chip in enumerate(chips)]
        for cp in first:
            cp.start()
        passed = [copy(4 + j, (*chip, c), sibling) for j, chip in enumerate(chips)]
        for j, chip in enumerate(chips):
            copy(1 + j, (*chip, c), me).wait_recv()
            passed[j].start()
        copy(0, sibling, me).wait_recv()
        for j, chip in enumerate(chips):
            copy(4 + j, (*chip, 1 - c), me).wait_recv()
        for cp in first + passed:
            cp.wait_send()
        mine.wait()
        total = out_ref[pl.ds(0, m_per), :]
        for d in range(1, N_DEV):
            total = total + out_ref[pl.ds(d * m_per, m_per), :]
        sum_ref[...] = total

    vmem = pl.BlockSpec(memory_space=pltpu.VMEM)
    return pl.pallas_call(
        body, name=name, in_specs=[vmem], out_specs=[vmem, vmem],
        out_shape=[jax.ShapeDtypeStruct((N_DEV * m_per, n), F32), jax.ShapeDtypeStruct((m_per, n), F32)],
        scratch_shapes=[pltpu.SemaphoreType.DMA((7,)), pltpu.SemaphoreType.DMA((7,)), pltpu.SemaphoreType.DMA],
        compiler_params=pltpu.CompilerParams(vmem_limit_bytes=V7X_VMEM_LIMIT),
    )(block)


WEIGHTS = ("rel_bias", "norm1_g", "w_in", "sgu_w", "sgu_b", "dil_qn_g", "dil_kn_g", "conv_w", "conv_b", "conv_ln_g", "conv_ln_b",
           "gqa_qn_g", "gqa_kn_g", "mix_norm_g", "w_out", "norm2_g", "w_gate", "w_up", "w_down")
SHARDED = ("w_in", "w_out", "w_gate", "w_up", "w_down")
COLUMN_SHARDED = ("w_in", "w_gate", "w_up")
REPLICATED = tuple(k for k in WEIGHTS if k not in SHARDED and k != "conv_w")


def _pack(parts):
    flat = jnp.concatenate([p.reshape(-1) for p in parts])
    pad = (-flat.shape[0]) % (8 * 128)
    return jnp.pad(flat, (0, pad)).reshape(-1, 128)


def _unpack(buf, shapes):
    flat = buf.reshape(-1)
    out, at = [], 0
    for s in shapes:
        size = math.prod(s)
        out.append(flat[at:at + size].reshape(s))
        at += size
    return out


RS_TM = (256, 128, 64, 32, 16)


def _add_halves(g, sib, c, name):
    slabs, h, cols = sib.shape
    tm = _pick(h, RS_TM)
    nb = h // tm

    def body(c_ref, g_ref, s_ref, o_ref):
        o_ref[...] = (g_ref[...].astype(F32) + s_ref[...].astype(F32)).astype(BF16)

    blk = pl.BlockSpec((None, tm, cols), lambda j, i, c_ref: (j, i, 0))
    grid_spec = pltpu.PrefetchScalarGridSpec(
        num_scalar_prefetch=1, grid=(slabs, nb),
        in_specs=[pl.BlockSpec((None, tm, cols), lambda j, i, c_ref: (j, c_ref[0] * nb + i, 0)), blk], out_specs=blk)
    return pl.pallas_call(body, name=name, grid_spec=grid_spec, out_shape=jax.ShapeDtypeStruct(sib.shape, BF16),
                          compiler_params=_cparams(("parallel", "parallel")))(c, g, sib)


def _add_own_three(own, land, chip, name):
    _, h, cols = own.shape
    tm = _pick(h, RS_TM)

    def body(chip_ref, own_ref, l0_ref, l1_ref, l2_ref, o_ref):
        o_ref[...] = ((own_ref[...].astype(F32) + l0_ref[...].astype(F32)) + l1_ref[...].astype(F32)) + l2_ref[...].astype(F32)

    slot = lambda r: pl.BlockSpec((None, tm, cols), functools.partial(lambda i, chip_ref, r: (jnp.bitwise_xor(chip_ref[0], r), i, 0), r=r))
    grid_spec = pltpu.PrefetchScalarGridSpec(
        num_scalar_prefetch=1, grid=(h // tm,),
        in_specs=[slot(0), slot(1), slot(2), slot(3)],
        out_specs=pl.BlockSpec((tm, cols), lambda i, chip_ref: (i, 0)))
    return pl.pallas_call(body, name=name, grid_spec=grid_spec, out_shape=jax.ShapeDtypeStruct((h, cols), F32),
                          compiler_params=_cparams(("parallel",)))(chip, own, land, land, land)


def _reduce_start(grads, c1, after, tag):
    from_sibling = _swap_other_half(grads, tag + "pair")
    sums = [_add_halves(g, s, c1, f"{tag}add2_{k}") for k, (g, s) in enumerate(zip(grads, from_sibling))]
    handle, token = _chips_start(sums, True, after, tag + "start")
    return (handle, sums), token


def _reduce_finish(started, chip1, after, tag):
    handle, sums = started
    lands = _chips_wait(handle, True, after, tag + "wait")
    totals = [_add_own_three(s, land, chip1, f"{tag}add4_{k}") for k, (s, land) in enumerate(zip(sums, lands))]
    return list(zip(totals, _swap_sibling(totals, tag + "share")))


def _adamw_shard(w, m, v, halves, c1, name):
    depth, rows, cols = w.shape
    h = rows // 2
    tm = _pick(h, RS_TM)
    nb = h // tm
    sources = [a for pair in halves for a in pair]

    def body(c_ref, w_ref, m_ref, v_ref, *refs):
        g_refs, (g_out, d_out, m_out, v_out) = refs[:2 * depth], refs[2 * depth:]
        layer, mine = pl.program_id(0), pl.program_id(1) == c_ref[0]
        g = None
        for l in range(depth):
            g_l = jnp.where(mine, g_refs[2 * l][...], g_refs[2 * l + 1][...])
            g = g_l if g is None else jnp.where(layer == l, g_l, g)
        delta, m_new, v_new = _adamw_fn(w_ref[...], g, m_ref[...], v_ref[...])
        g_out[...], d_out[...], m_out[...], v_out[...] = g, delta, m_new, v_new

    def source_spec(l, own):
        def index(layer, half, i, c_ref):
            return (jnp.where((layer == l) & ((half == c_ref[0]) == own), i, 0), 0)
        return pl.BlockSpec((tm, cols), index)

    blk = pl.BlockSpec((None, tm, cols), lambda layer, half, i, c_ref: (layer, half * nb + i, 0))
    grid_spec = pltpu.PrefetchScalarGridSpec(
        num_scalar_prefetch=1, grid=(depth, 2, nb),
        in_specs=[blk, blk, blk] + [source_spec(l, own) for l in range(depth) for own in (True, False)], out_specs=[blk] * 4)
    return pl.pallas_call(body, name=name, grid_spec=grid_spec, out_shape=[jax.ShapeDtypeStruct(w.shape, F32)] * 4,
                          compiler_params=_cparams(("arbitrary", "arbitrary", "arbitrary")))(c1, w, m, v, *sources)


GATHER_GROUPS = (("w_in",), ("w_out", "w_gate", "w_up", "w_down"))
REDUCE_GROUPS = (("w_down", "w_gate", "w_up"), ("w_out",), ("w_in",))


def kernel(x, rel_bias, norm1_g, w_in, sgu_w, sgu_b, dil_qn_g, dil_kn_g, conv_w, conv_b, conv_ln_g, conv_ln_b, gqa_qn_g, gqa_kn_g, mix_norm_g, w_out, norm2_g, w_gate, w_up, w_down, loss_target, m_rel_bias, m_norm1_g, m_w_in, m_sgu_w, m_sgu_b, m_dil_qn_g, m_dil_kn_g, m_conv_w, m_conv_b, m_conv_ln_g, m_conv_ln_b, m_gqa_qn_g, m_gqa_kn_g, m_mix_norm_g, m_w_out, m_norm2_g, m_w_gate, m_w_up, m_w_down, v_rel_bias, v_norm1_g, v_w_in, v_sgu_w, v_sgu_b, v_dil_qn_g, v_dil_kn_g, v_conv_w, v_conv_b, v_conv_ln_g, v_conv_ln_b, v_gqa_qn_g, v_gqa_kn_g, v_mix_norm_g, v_w_out, v_norm2_g, v_w_gate, v_w_up, v_w_down):
    w = dict(rel_bias=rel_bias, norm1_g=norm1_g, w_in=w_in, sgu_w=sgu_w, sgu_b=sgu_b, dil_qn_g=dil_qn_g, dil_kn_g=dil_kn_g,
             conv_w=conv_w, conv_b=conv_b, conv_ln_g=conv_ln_g, conv_ln_b=conv_ln_b, gqa_qn_g=gqa_qn_g, gqa_kn_g=gqa_kn_g,
             mix_norm_g=mix_norm_g, w_out=w_out, norm2_g=norm2_g, w_gate=w_gate, w_up=w_up, w_down=w_down)
    m = dict(rel_bias=m_rel_bias, norm1_g=m_norm1_g, w_in=m_w_in, sgu_w=m_sgu_w, sgu_b=m_sgu_b, dil_qn_g=m_dil_qn_g,
             dil_kn_g=m_dil_kn_g, conv_w=m_conv_w, conv_b=m_conv_b, conv_ln_g=m_conv_ln_g, conv_ln_b=m_conv_ln_b,
             gqa_qn_g=m_gqa_qn_g, gqa_kn_g=m_gqa_kn_g, mix_norm_g=m_mix_norm_g, w_out=m_w_out, norm2_g=m_norm2_g,
             w_gate=m_w_gate, w_up=m_w_up, w_down=m_w_down)
    v = dict(rel_bias=v_rel_bias, norm1_g=v_norm1_g, w_in=v_w_in, sgu_w=v_sgu_w, sgu_b=v_sgu_b, dil_qn_g=v_dil_qn_g,
             dil_kn_g=v_dil_kn_g, conv_w=v_conv_w, conv_b=v_conv_b, conv_ln_g=v_conv_ln_g, conv_ln_b=v_conv_ln_b,
             gqa_qn_g=v_gqa_qn_g, gqa_kn_g=v_gqa_kn_g, mix_norm_g=v_mix_norm_g, w_out=v_w_out, norm2_g=v_norm2_g,
             w_gate=v_w_gate, w_up=v_w_up, w_down=v_w_down)
    bsz = x.shape[0]
    t = bsz * SEQ
    xi, yi, ci = _mesh_pos()
    chip = 2 * xi + yi
    conv_cols = conv_w.shape[-1]

    conv_rows = DEPTH * CONV_WIDTH
    conv_block = jnp.pad(conv_w.reshape(conv_rows, conv_cols), ((0, (-conv_rows) % 8), (0, 0)))
    every, _ = _allgather_sum_small(conv_block, "conv_w_gather")
    every = every.reshape(N_DEV, conv_block.shape[0], conv_cols)
    conv_w_full = jnp.concatenate([every[2 * j, :conv_rows].reshape(DEPTH, CONV_WIDTH, conv_cols) for j in range(N_CHIPS)], axis=-1)

    c1 = jnp.reshape(ci, (1,)).astype(jnp.int32)
    chip1 = jnp.reshape(chip, (1,)).astype(jnp.int32)

    def own_shard(k):
        return lax.dynamic_index_in_dim(w[k], ci, axis=0, keepdims=False).astype(BF16)

    fetches, token = [], every
    for gi, group in enumerate(GATHER_GROUPS):
        shards = [own_shard(k) for k in group]
        handle, token = _chips_start(shards, False, token, f"gather{gi}_start", per_core=True)
        fetches.append((handle, shards))
    all_started = token
    gathered = {}

    def big(l, name, after):
        if (l, name) not in gathered:
            gi = [name in group for group in GATHER_GROUPS].index(True)
            handle, shards = fetches[gi]
            lands = _chips_wait(handle, False, all_started if after is None else after, f"gather{gi}_wait")
            mine = [lax.dynamic_update_slice(land, own[None, None], (ci, chip, 0, 0)) for land, own in zip(lands, shards)]
            for k, g in zip(GATHER_GROUPS[gi], _complete_pairs(mine, f"gather{gi}_share")):
                rows, cols = g.shape[2:]
                for layer in range(DEPTH):
                    gl = g[layer]
                    gathered[layer, k] = (gl.transpose(1, 0, 2).reshape(rows, N_CHIPS * cols) if k in COLUMN_SHARDED
                                          else gl.reshape(N_CHIPS * rows, cols))
        return gathered[l, name]

    big(0, "w_in", None)

    pending, started, reduced = {}, {}, {}

    def emit(l, name, g):
        pending[l, name] = g
        for gi, group in enumerate(REDUCE_GROUPS):
            if name in group and all((l, k) in pending for k in group):
                started[l, gi], token = _reduce_start([pending[l, k] for k in group], c1, g, f"l{l}_reduce{gi}_")
                return token
        return None

    def finish(l, after):
        for gi, group in enumerate(REDUCE_GROUPS):
            for k, r in zip(group, _reduce_finish(started[l, gi], chip1, after, f"l{l}_reduce{gi}_")):
                reduced[l, k] = r
            after = reduced[l, group[0]][1]

    def mid_hook(l, a):
        if l + 1 < DEPTH:
            finish(l + 1, a)

    small = {k: w[k] for k in REPLICATED}
    small["conv_w"] = conv_w_full
    loss, dx, small_grads = _local_step(x.reshape(t, D_MODEL), loss_target.reshape(t, D_MODEL), small, big, emit, mid_hook, bsz)
    loss = lax.psum(loss[0, 0], ("x", "y", "c"))
    finish(0, dx)

    grads, deltas, new_m, new_v = {}, {}, {}, {}
    for k in SHARDED:
        grads[k], deltas[k], new_m[k], new_v[k] = _adamw_shard(w[k], m[k], v[k], [reduced[l, k] for l in range(DEPTH)], c1, "adamw_" + k)

    names = REPLICATED + ("conv_w",)
    shapes = [small_grads[k].shape for k in names]
    _, summed = _allgather_sum_small(_pack([small_grads[k] for k in names]), "small_grads_sum")
    summed_parts = dict(zip(names, _unpack(summed, shapes)))
    rep_shapes = [w[k].shape for k in REPLICATED]
    packed = [_pack([src[k] for k in REPLICATED]) for src in (w, {k: summed_parts[k] for k in REPLICATED}, m, v)]
    d_p, m_p, v_p = _adamw(*packed, "adamw_replicated")
    for k, gk, dk, mk, vk in zip(REPLICATED, _unpack(packed[1], rep_shapes), _unpack(d_p, rep_shapes), _unpack(m_p, rep_shapes),
                                 _unpack(v_p, rep_shapes)):
        grads[k], deltas[k], new_m[k], new_v[k] = gk, dk, mk, vk
    g_conv = lax.dynamic_slice_in_dim(summed_parts["conv_w"], chip * conv_cols, conv_cols, axis=2)
    packed = [_pack([a]) for a in (conv_w, g_conv, m["conv_w"], v["conv_w"])]
    d_p, m_p, v_p = _adamw(*packed, "adamw_conv_w")
    grads["conv_w"] = g_conv
    deltas["conv_w"], new_m["conv_w"], new_v["conv_w"] = (_unpack(a, [conv_w.shape])[0] for a in (d_p, m_p, v_p))

    return (loss, dx.reshape(x.shape), *[grads[k] for k in WEIGHTS], *[deltas[k] for k in WEIGHTS],
            *[new_m[k] for k in WEIGHTS], *[new_v[k] for k in WEIGHTS])
```

```python
import functools
import math

import numpy as np
import jax
import jax.numpy as jnp
from jax import lax
from jax.experimental import pallas as pl
from jax.experimental.pallas import tpu as pltpu

F32 = jnp.float32
BF16 = jnp.bfloat16

D_MODEL = 2048
SEQ = 2048
DEPTH = 2
HEAD_DIM = 64
GROUP_WIDTH = 512
N_HEADS = 8
SGU_CHUNK = 128
DIL_PATTERNS = ((128, 1), (512, 4), (2048, 16))
DIL_HALF = 64
CONV_WIDTH = 31
KV_WIDTH = 128
GRID_W = 64
ROPE_THETA = 10000.0
REL_BUCKETS = 32
REL_MAX_DIST = 1024
FFN_HIDDEN = 5632
IN_WIDTH = 4352
RMS_EPS = 1e-6
LN_EPS = 1e-5
ADAM_LR = 0.001
ADAM_B1 = 0.9
ADAM_B2 = 0.999
ADAM_EPS = 1e-08
ADAM_WD = 0.01
ADAM_STEP = 10
N_CHIPS = 4

V7X_VMEM_LIMIT = 56 * 1024 * 1024
MATMUL_VMEM_BUDGET = 48 * 1024 * 1024
LANES = 128
HI = lax.Precision.HIGHEST
MESH = pl.DeviceIdType.MESH


def _cparams(sem=None):
    return pltpu.CompilerParams(dimension_semantics=sem, vmem_limit_bytes=V7X_VMEM_LIMIT)


def _pick(n, cands):
    for c in cands:
        if n % c == 0:
            return c
    raise ValueError(f"no tile for {n}")


_DIMS = {"nn": (((1,), (0,)), ((), ())), "nt": (((1,), (1,)), ((), ())), "tn": (((0,), (0,)), ((), ()))}


def _matmul(pairs, mode, out_dtype, name, residual=None, slabs=1):
    a0, b0 = pairs[0]
    b3 = b0.ndim == 3
    if mode == "nn":
        (M, K), N = a0.shape, b0.shape[1]
    elif mode == "nt":
        (M, K), N = a0.shape, b0.shape[0]
    else:
        (K, M) = a0.shape
        N = b0.shape[-1] if b3 else b0.shape[1] // slabs
    npairs = len(pairs)
    a_bytes, b_bytes, o_bytes = a0.dtype.itemsize, b0.dtype.itemsize, jnp.dtype(out_dtype).itemsize
    per_out = 4 + 2 * o_bytes + (8 if residual is not None else 0)
    tn_cands = [c for c in ((1024, 512) if K <= 2048 else (512,)) + (1408, 2176, 256) if N % c == 0] + [N]
    tm, tn = next((tm, tn) for tn in tn_cands for tm in (1024, 1408, 512, 256)
                  if M % tm == 0 and 2 * npairs * K * (tm * a_bytes + tn * b_bytes) + tm * tn * per_out <= MATMUL_VMEM_BUDGET)
    tk = K
    ni, nj = M // tm, N // tn
    j_outer = nj * M * a_bytes + N * b_bytes < M * a_bytes + ni * N * b_bytes
    grid = (slabs, nj, ni) if j_outer else (slabs, ni, nj)
    at = lambda f: (lambda s, g1, g2: f(s, g2, g1)) if j_outer else f

    if mode in ("nn", "nt"):
        a_spec = pl.BlockSpec((tm, tk), at(lambda s, i, j: (i, 0)))
    else:
        a_spec = pl.BlockSpec((tk, tm), at(lambda s, i, j: (0, i)))
    if mode == "nt":
        b_spec = pl.BlockSpec((tn, tk), at(lambda s, i, j: (j, 0)))
    elif b3:
        b_spec = pl.BlockSpec((None, tk, tn), at(lambda s, i, j: (s, 0, j)))
    else:
        b_spec = pl.BlockSpec((tk, tn), at(lambda s, i, j: (0, s * nj + j)))
    if slabs > 1:
        o_spec = pl.BlockSpec((None, tm, tn), at(lambda s, i, j: (s, i, j)))
        o_shape = jax.ShapeDtypeStruct((slabs, M, N), out_dtype)
    else:
        o_spec = pl.BlockSpec((tm, tn), at(lambda s, i, j: (i, j)))
        o_shape = jax.ShapeDtypeStruct((M, N), out_dtype)
    in_specs = [a_spec] * npairs + [b_spec] * npairs
    args = [a for a, _ in pairs] + [b for _, b in pairs]
    if residual is not None:
        in_specs.append(pl.BlockSpec((tm, tn), at(lambda s, i, j: (i, j))))
        args.append(residual)
    dims = _DIMS[mode]

    def body(*refs):
        a_refs, b_refs = refs[:npairs], refs[npairs:2 * npairs]
        res_ref = refs[2 * npairs] if residual is not None else None
        o_ref = refs[-1]
        r = None
        for a_ref, b_ref in zip(a_refs, b_refs):
            d = lax.dot_general(a_ref[...].astype(BF16), b_ref[...].astype(BF16), dims, preferred_element_type=F32)
            r = d if r is None else r + d
        if res_ref is not None:
            r = r + res_ref[...]
        o_ref[...] = r.astype(out_dtype)

    return pl.pallas_call(
        body, name=name, grid=grid, in_specs=in_specs, out_specs=o_spec, out_shape=o_shape,
        compiler_params=_cparams(("parallel", "parallel", "parallel")),
    )(*args)


class Strided:
    def __init__(self, r):
        self.r = r


def _rowmap(fn, rows, fulls, row_outs, acc_outs, name, tm, n_rows):
    nr, nf, nro = len(rows), len(fulls), len(row_outs)
    rows = [r if len(r) == 4 else (*r, n_rows // tm) for r in rows]
    row_outs = [o if len(o) == 3 else (*o, None) for o in row_outs]
    in_specs = [pl.BlockSpec((tm // per.r, per.r * w), lambda i: (i, 0)) if isinstance(per, Strided) else
                pl.BlockSpec((tm, w), functools.partial(lambda i, cb, per: (i % per, cb), cb=cb, per=per)) for _, w, cb, per in rows]
    in_specs += [pl.BlockSpec(f.shape, lambda i: (0,) * f.ndim) for f in fulls]
    out_specs = [pl.BlockSpec((tm, w) if st is None else (tm // st.r, st.r * w), lambda i: (i, 0)) for w, _, st in row_outs]
    out_specs += [pl.BlockSpec(s, functools.partial(lambda i, n: (0,) * n, n=len(s))) for s in acc_outs]
    out_shape = [jax.ShapeDtypeStruct((n_rows, w) if st is None else (n_rows // st.r, st.r * w), dt) for w, dt, st in row_outs]
    out_shape += [jax.ShapeDtypeStruct(s, F32) for s in acc_outs]
    strided = [(k, w, per.r) for k, (_, w, _, per) in enumerate(rows) if isinstance(per, Strided)]
    strided += [(nr + nf + k, w, st.r) for k, (w, _, st) in enumerate(row_outs) if st is not None]
    n_scratch = len(strided)

    def body(*refs):
        refs, scratch = refs[:len(refs) - n_scratch], dict(zip([k for k, _, _ in strided], refs[len(refs) - n_scratch:]))
        ins = []
        for k, ref in enumerate(refs[:nr + nf]):
            if k in scratch:
                w, r, scr = rows[k][1], rows[k][3].r, scratch[k]
                for rho in range(r):
                    for j in range(w // LANES):
                        scr.at[j][pl.ds(rho, tm // r, stride=r), :] = ref[:, pl.ds(rho * w + j * LANES, LANES)]
                ins.append(jnp.concatenate([scr[j] for j in range(w // LANES)], axis=1))
            else:
                ins.append(ref[...])
        outs = fn(*ins)
        o_refs = refs[nr + nf:]
        for k, (o_ref, val) in enumerate(zip(o_refs[:nro], outs[:nro])):
            if nr + nf + k in scratch:
                w, r, scr = row_outs[k][0], row_outs[k][2].r, scratch[nr + nf + k]
                val = val.astype(F32)
                for j in range(w // LANES):
                    scr[j] = val[:, j * LANES:(j + 1) * LANES]
                for rho in range(r):
                    for j in range(w // LANES):
                        o_ref[:, pl.ds(rho * w + j * LANES, LANES)] = scr.at[j][pl.ds(rho, tm // r, stride=r), :].astype(o_ref.dtype)
            else:
                o_ref[...] = val.astype(o_ref.dtype)
        if acc_outs:
            first = pl.program_id(0) == 0
            for o_ref, val in zip(o_refs[nro:], outs[nro:]):
                @pl.when(first)
                def _(o_ref=o_ref, val=val):
                    o_ref[...] = val

                @pl.when(jnp.logical_not(first))
                def _(o_ref=o_ref, val=val):
                    o_ref[...] += val

    res = pl.pallas_call(
        body, name=name, grid=(n_rows // tm,), in_specs=in_specs, out_specs=out_specs, out_shape=out_shape,
        scratch_shapes=[pltpu.VMEM((w // LANES, tm, LANES), F32) for _, w, _ in strided],
        compiler_params=_cparams(("arbitrary",) if acc_outs else ("parallel",)),
    )(*[r[0] for r in rows], *fulls)
    return res


def _rms(x, g):
    return x * lax.rsqrt(jnp.mean(x * x, axis=-1, keepdims=True) + RMS_EPS) * g


def _rmsnorm_fwd(x, g, name):
    t = x.shape[0]
    return _rowmap(lambda xv, gv: (_rms(xv, gv),), [(x, D_MODEL, 0)], [g], [(D_MODEL, BF16)], [], name, 512, t)[0]


def _rmsnorm_bwd(dh, x, g, dres, name, follow=None):
    t = x.shape[0]

    def fn(dhv, xv, drv, gv, *_):
        _, vjp = jax.vjp(_rms, xv, gv)
        dx, dg = vjp(dhv)
        return dx + drv, dx + drv, dg

    fulls = [g] if follow is None else [g, follow]
    return _rowmap(fn, [(dh, D_MODEL, 0), (x, D_MODEL, 0), (dres, D_MODEL, 0)], fulls, [(D_MODEL, F32), (D_MODEL, BF16)],
                   [(1, D_MODEL)], name, 256, t)


def _loss_fwd_bwd(y, target, name):
    t = y.shape[0]

    def fn(yv, tv):
        e = yv - tv
        return e * (1.0 / D_MODEL), e * (1.0 / D_MODEL), (0.5 / D_MODEL) * jnp.sum(e * e, keepdims=True)

    return _rowmap(fn, [(y, D_MODEL, 0), (target, D_MODEL, 0)], [], [(D_MODEL, F32), (D_MODEL, BF16)], [(1, 1)], name, 512, t)


FFN_TILE = (1024, 512)


def _ffn_up(h, wg, wu, name):
    t, n = h.shape[0], wg.shape[1]
    tm, tn = FFN_TILE

    def body(h_ref, wg_ref, wu_ref, act_ref, du_ref, dg_ref):
        hv = h_ref[...]
        g = jnp.dot(hv, wg_ref[...], preferred_element_type=F32)
        u = jnp.dot(hv, wu_ref[...], preferred_element_type=F32)
        sg = jax.nn.sigmoid(g)
        silu = g * sg
        act_ref[...] = (silu * u).astype(BF16)
        du_ref[...] = silu.astype(BF16)
        dg_ref[...] = (u * (sg + silu * (1.0 - sg))).astype(BF16)

    o_spec = pl.BlockSpec((tm, tn), lambda i, j: (i, j))
    o_shape = jax.ShapeDtypeStruct((t, n), BF16)
    return pl.pallas_call(
        body, name=name, grid=(t // tm, n // tn),
        in_specs=[pl.BlockSpec((tm, D_MODEL), lambda i, j: (i, 0)), pl.BlockSpec((D_MODEL, tn), lambda i, j: (0, j)),
                  pl.BlockSpec((D_MODEL, tn), lambda i, j: (0, j))],
        out_specs=[o_spec] * 3, out_shape=[o_shape] * 3, compiler_params=_cparams(("parallel", "parallel")),
    )(h, wg, wu)


def _ffn_down_bwd(dy, wd, act_du, act_dg, name):
    t, n = dy.shape[0], wd.shape[0]
    tm, tn = FFN_TILE

    def body(dy_ref, wd_ref, adu_ref, adg_ref, dg_ref, du_ref):
        dact = lax.dot_general(dy_ref[...].astype(BF16), wd_ref[...], _DIMS["nt"], preferred_element_type=F32)
        du_ref[...] = (dact * adu_ref[...].astype(F32)).astype(BF16)
        dg_ref[...] = (dact * adg_ref[...].astype(F32)).astype(BF16)

    o_spec = pl.BlockSpec((tm, tn), lambda i, j: (i, j))
    o_shape = jax.ShapeDtypeStruct((t, n), BF16)
    return pl.pallas_call(
        body, name=name, grid=(t // tm, n // tn),
        in_specs=[pl.BlockSpec((tm, D_MODEL), lambda i, j: (i, 0)), pl.BlockSpec((tn, D_MODEL), lambda i, j: (j, 0)),
                  o_spec, o_spec],
        out_specs=[o_spec] * 2, out_shape=[o_shape] * 2, compiler_params=_cparams(("parallel", "parallel")),
    )(dy, wd, act_du, act_dg)


def _np_group_avg(width, group=HEAD_DIM):
    i = np.arange(width)
    return ((i[:, None] // group) == (i[None, :] // group)).astype(np.float32) / group


def _np_tile_fold(width, group=HEAD_DIM):
    return ((np.arange(width)[:, None] % group) == np.arange(group)[None, :]).astype(np.float32)


def _np_group_fold(width, group=HEAD_DIM, pad=128):
    return ((np.arange(width)[:, None] // group) == np.arange(pad)[None, :]).astype(np.float32)


def _np_rope_partner(width):
    i = np.arange(width)
    partner = np.where(i % 32 < 16, i + 16, i - 16)
    return (partner[:, None] == i[None, :]).astype(np.float32)


def _np_kv_expand():
    src = np.arange(KV_WIDTH)
    dst = np.arange(GROUP_WIDTH)
    return ((src[:, None] // HEAD_DIM == dst[None, :] // (4 * HEAD_DIM)) & (src[:, None] % HEAD_DIM == dst[None, :] % HEAD_DIM)).astype(np.float32)


def _np_rope_tables(n_heads):
    t = np.arange(SEQ)
    pos = {0: (t // GRID_W).astype(np.float32), 1: (t % GRID_W).astype(np.float32)}
    freqs = (ROPE_THETA ** (-np.arange(16, dtype=np.float32) / 16)).astype(np.float32)
    cos_parts, sin_parts = [], []
    for axis in (0, 1):
        ang = pos[axis][:, None] * freqs[None, :]
        c, s = np.cos(ang).astype(np.float32), np.sin(ang).astype(np.float32)
        cos_parts += [c, c]
        sin_parts += [-s, s]
    cos = np.concatenate(cos_parts, axis=1)
    sin = np.concatenate(sin_parts, axis=1)
    return np.tile(cos, (1, n_heads)), np.tile(sin, (1, n_heads))


def _np_t5_buckets(rel):
    nb = REL_BUCKETS // 2
    max_exact = nb // 2
    ret = np.where(rel > 0, nb, 0)
    n = np.abs(rel)
    nf = np.maximum(n, 1).astype(np.float32)
    large = max_exact + (np.log(nf / max_exact) / math.log(REL_MAX_DIST / max_exact) * (nb - max_exact)).astype(np.int32)
    large = np.minimum(large, nb - 1)
    return (ret + np.where(n < max_exact, n, large)).astype(np.int32)


DIL_QB = 128
DIL_WIN = DIL_QB + 2 * DIL_HALF


def _np_dil_buckets(dil):
    off = np.arange(DIL_WIN)[None, :] - DIL_HALF - np.arange(DIL_QB)[:, None]
    return _np_t5_buckets(off * dil)


def _dil_live_buckets(dil):
    off = np.arange(-DIL_HALF, DIL_HALF + 1)
    return sorted(set(_np_t5_buckets(off * dil).tolist()))


def _head_stat(x, mavg):
    return jnp.dot(x, mavg, precision=HI, preferred_element_type=F32)


def _gelu(x):
    return 0.5 * x * (1.0 + jnp.tanh(math.sqrt(2.0 / math.pi) * (x + 0.044715 * (x * x * x))))


def _sgu_pre(u_pre, v_pre, mavg):
    v = _gelu(v_pre)
    xc = v - _head_stat(v, mavg)
    vn = xc * lax.rsqrt(_head_stat(xc * xc, mavg) + LN_EPS)
    return _gelu(u_pre), vn


def _sgu_mix(w_ref, vnb, bm):
    lane_group = lax.broadcasted_iota(jnp.int32, (1, GROUP_WIDTH), 1) // HEAD_DIM
    mixed = bm
    for g in range(N_HEADS):
        r = jnp.dot(w_ref[g], vnb, preferred_element_type=F32)
        mixed = mixed + jnp.where(lane_group == g, r, 0.0)
    return mixed


SGU_TM = 512


def _sgu_fwd(z, w_s, bm, name):
    t = z.shape[0]
    mavg = jnp.asarray(_np_group_avg(GROUP_WIDTH))

    def body(u_ref, v_ref, w_ref, bm_ref, mavg_ref, y_ref):
        for c in range(SGU_TM // SGU_CHUNK):
            rows = pl.ds(c * SGU_CHUNK, SGU_CHUNK)
            u, vn = _sgu_pre(u_ref[rows, :], v_ref[rows, :], mavg_ref[...])
            y_ref[rows, :] = u * _sgu_mix(w_ref, vn.astype(BF16), bm_ref[...])

    full = lambda a: pl.BlockSpec(a.shape, lambda i: (0,) * a.ndim)
    return pl.pallas_call(
        body, name=name, grid=(t // SGU_TM,),
        in_specs=[pl.BlockSpec((SGU_TM, GROUP_WIDTH), lambda i: (i, 0)), pl.BlockSpec((SGU_TM, GROUP_WIDTH), lambda i: (i, 1)),
                  full(w_s), full(bm), full(mavg)],
        out_specs=pl.BlockSpec((SGU_TM, GROUP_WIDTH), lambda i: (i, 0)),
        out_shape=jax.ShapeDtypeStruct((t, GROUP_WIDTH), F32), compiler_params=_cparams(("parallel",)),
    )(z, z, w_s, bm, mavg)


def _sgu_bwd(z, dy, w_s, w_s_t, bm, name):
    t = z.shape[0]
    mavg = jnp.asarray(_np_group_avg(GROUP_WIDTH))
    gfold = jnp.asarray(_np_group_fold(GROUP_WIDTH))

    def body(u_ref, v_ref, dy_ref, w_ref, wt_ref, bm_ref, mavg_ref, gfold_ref, du_ref, dv_ref, dw_ref, dbs_ref, dbm_ref):
        @pl.when(pl.program_id(0) == 0)
        def _():
            dw_ref[...] = jnp.zeros_like(dw_ref)
            dbm_ref[...] = jnp.zeros_like(dbm_ref)

        lane_group = lax.broadcasted_iota(jnp.int32, (1, GROUP_WIDTH), 1) // HEAD_DIM
        for c in range(SGU_TM // SGU_CHUNK):
            rows = pl.ds(c * SGU_CHUNK, SGU_CHUNK)
            (u, vn), pre_vjp = jax.vjp(functools.partial(_sgu_pre, mavg=mavg_ref[...]), u_ref[rows, :], v_ref[rows, :])
            vnb = vn.astype(BF16)
            mixed = _sgu_mix(w_ref, vnb, bm_ref[...])
            dyv = dy_ref[rows, :]
            dmixed = dyv * u
            dbm_ref[...] += dmixed
            dvn = jnp.zeros_like(vn)
            for g in range(N_HEADS):
                dm_g = jnp.where(lane_group == g, dmixed, 0.0).astype(BF16)
                dw_ref[g] += lax.dot_general(dm_g, vnb, _DIMS["nt"], preferred_element_type=F32)
                dvn = dvn + jnp.dot(wt_ref[g], dm_g, preferred_element_type=F32)
            du_pre, dv_pre = pre_vjp((dyv * mixed, dvn))
            du_ref[rows, :] = du_pre
            dv_ref[rows, :] = dv_pre

        @pl.when(pl.program_id(0) == t // SGU_TM - 1)
        def _():
            dbs_ref[...] = jnp.dot(dbm_ref[...], gfold_ref[...], precision=HI, preferred_element_type=F32)

    full = lambda a: pl.BlockSpec(a.shape, lambda i: (0,) * a.ndim)
    row = pl.BlockSpec((SGU_TM, GROUP_WIDTH), lambda i: (i, 0))
    return pl.pallas_call(
        body, name=name, grid=(t // SGU_TM,),
        in_specs=[row, pl.BlockSpec((SGU_TM, GROUP_WIDTH), lambda i: (i, 1)), row, full(w_s), full(w_s_t), full(bm), full(mavg),
                  full(gfold)],
        out_specs=[row, row, pl.BlockSpec((N_HEADS, SGU_CHUNK, SGU_CHUNK), lambda i: (0, 0, 0)),
                   pl.BlockSpec((SGU_CHUNK, 128), lambda i: (0, 0))],
        out_shape=[jax.ShapeDtypeStruct((t, GROUP_WIDTH), F32)] * 2 + [jax.ShapeDtypeStruct((N_HEADS, SGU_CHUNK, SGU_CHUNK), F32),
                                                                      jax.ShapeDtypeStruct((SGU_CHUNK, 128), F32)],
        scratch_shapes=[pltpu.VMEM((SGU_CHUNK, GROUP_WIDTH), F32)],
        compiler_params=_cparams(("arbitrary",)),
    )(z, z, dy, w_s, w_s_t, bm, mavg, gfold)


def _pair_softmax(q2, k2, hh, bias, valid):
    head = (lax.broadcasted_iota(jnp.int32, (1, 2 * HEAD_DIM), 1) // HEAD_DIM) == hh
    qm = jnp.where(head, q2, jnp.zeros_like(q2))
    s = lax.dot_general(qm, k2, _DIMS["nt"], preferred_element_type=F32)
    if bias is not None:
        s = s + bias
    if valid is not None:
        s = jnp.where(valid, s, -1e30)
    m = jnp.max(s, axis=-1, keepdims=True)
    e = jnp.exp(s - m)
    l = jnp.sum(e, axis=-1, keepdims=True)
    return head, qm, e / l, m + jnp.log(l)


def _attn_pair_fwd(q2, k2, v2, biases, valid):
    o2 = lse2 = None
    for hh in range(2):
        head, _, p, lse = _pair_softmax(q2, k2, hh, None if biases is None else biases[hh], valid)
        oh = jnp.dot(p.astype(BF16), v2, preferred_element_type=F32)
        o_h = jnp.where(head, oh, 0.0)
        l_h = jnp.where(head, lse, 0.0)
        o2 = o_h if o2 is None else o2 + o_h
        lse2 = l_h if lse2 is None else lse2 + l_h
    return o2, lse2


def _attn_pair_bwd(q2, k2, v2, biases, valid, do2, dlse2):
    dq2 = dk2 = dv2 = None
    ds_heads = []
    for hh in range(2):
        head, qm, p, _ = _pair_softmax(q2, k2, hh, None if biases is None else biases[hh], valid)
        dom = jnp.where(head, do2, 0.0).astype(BF16)
        dp = lax.dot_general(dom, v2, _DIMS["nt"], preferred_element_type=F32)
        delta = jnp.sum(dp * p, axis=-1, keepdims=True)
        if dlse2 is not None:
            delta = delta - jnp.sum(jnp.where(head, dlse2, 0.0), axis=-1, keepdims=True)
        ds = p * (dp - delta)
        dsb = ds.astype(BF16)
        dq_h = jnp.where(head, jnp.dot(dsb, k2, preferred_element_type=F32), 0.0)
        dk_h = lax.dot_general(dsb, qm, _DIMS["tn"], preferred_element_type=F32)
        dv_h = lax.dot_general(p.astype(BF16), dom, _DIMS["tn"], preferred_element_type=F32)
        dq2 = dq_h if dq2 is None else dq2 + dq_h
        dk2 = dk_h if dk2 is None else dk2 + dk_h
        dv2 = dv_h if dv2 is None else dv2 + dv_h
        ds_heads.append(ds)
    return dq2, dk2, dv2, ds_heads


def _dil_valid(r0, length):
    row = lax.broadcasted_iota(jnp.int32, (DIL_QB, DIL_WIN), 0)
    col = lax.broadcasted_iota(jnp.int32, (DIL_QB, DIL_WIN), 1)
    off = col - DIL_HALF - row
    kpos = r0 - DIL_HALF + col
    return (jnp.abs(off) <= DIL_HALF) & (kpos >= 0) & (kpos < length)


def _dil_build_bias(tab_ref, bkt_ref, bias_ref, dil):
    bkt = bkt_ref[...]
    for h in range(N_HEADS):
        acc = jnp.zeros((DIL_QB, DIL_WIN), F32)
        for b in _dil_live_buckets(dil):
            acc = jnp.where(bkt == b, tab_ref[b, h], acc)
        bias_ref[h] = acc


def _dil_fill_pad(pad_ref, src_ref, length):
    zeros = jnp.zeros((DIL_HALF, GROUP_WIDTH), pad_ref.dtype)
    pad_ref[pl.ds(0, DIL_HALF), :] = zeros
    pad_ref[pl.ds(DIL_HALF + length, DIL_HALF), :] = zeros
    pad_ref[pl.ds(DIL_HALF, length), :] = src_ref[...]


def _dil_specs(bsz, length, dil):
    view = lambda a: a.reshape(bsz, length, dil * GROUP_WIDTH)
    blk = pl.BlockSpec((None, DIL_QB, GROUP_WIDTH), lambda b, rho, i: (b, i, rho))
    seq = pl.BlockSpec((None, length, GROUP_WIDTH), lambda b, rho, i: (b, 0, rho))
    return view, blk, seq


def _dil_fwd(qb, kb, vb, table, dil, name):
    length = SEQ // dil
    bsz = qb.shape[0] // length
    bkt = jnp.asarray(_np_dil_buckets(dil))
    view, blk, seq = _dil_specs(bsz, length, dil)

    def body(tab_ref, bkt_ref, q_ref, k_ref, v_ref, o_ref, lse_ref, kpad, vpad, bias_ref):
        i = pl.program_id(2)

        @pl.when((pl.program_id(0) == 0) & (pl.program_id(1) == 0) & (i == 0))
        def _():
            _dil_build_bias(tab_ref, bkt_ref, bias_ref, dil)

        @pl.when(i == 0)
        def _():
            _dil_fill_pad(kpad, k_ref, length)
            _dil_fill_pad(vpad, v_ref, length)

        r0 = pl.multiple_of(i * DIL_QB, DIL_QB)
        valid = _dil_valid(r0, length)
        for m in range(N_HEADS // 2):
            lanes = pl.ds(m * 128, 128)
            o2, lse2 = _attn_pair_fwd(q_ref[:, lanes], kpad[pl.ds(r0, DIL_WIN), lanes], vpad[pl.ds(r0, DIL_WIN), lanes],
                                      (bias_ref[2 * m], bias_ref[2 * m + 1]), valid)
            o_ref[:, lanes] = o2
            lse_ref[:, lanes] = lse2

    out = jax.ShapeDtypeStruct((bsz, length, dil * GROUP_WIDTH), F32)
    o, lse = pl.pallas_call(
        body, name=name, grid=(bsz, dil, length // DIL_QB),
        in_specs=[pl.BlockSpec(memory_space=pltpu.SMEM), pl.BlockSpec(bkt.shape, lambda b, rho, i: (0, 0)), blk, seq, seq],
        out_specs=[blk, blk], out_shape=[out, out],
        scratch_shapes=[pltpu.VMEM((length + 2 * DIL_HALF, GROUP_WIDTH), BF16), pltpu.VMEM((length + 2 * DIL_HALF, GROUP_WIDTH), BF16),
                        pltpu.VMEM((N_HEADS, DIL_QB, DIL_WIN), F32)],
        compiler_params=_cparams(("arbitrary", "arbitrary", "arbitrary")),
    )(table, bkt, view(qb), view(kb), view(vb))
    return o.reshape(qb.shape), lse.reshape(qb.shape)


def _dil_bwd(qb, kb, vb, do, dlse, table, dil, name):
    length = SEQ // dil
    bsz = qb.shape[0] // length
    nqb = length // DIL_QB
    bkt = jnp.asarray(_np_dil_buckets(dil))
    view, blk, seq = _dil_specs(bsz, length, dil)

    def body(tab_ref, bkt_ref, q_ref, k_ref, v_ref, do_ref, dlse_ref, dq_ref, dk_ref, dv_ref, dsc_ref, kpad, vpad, bias_ref):
        i = pl.program_id(2)

        @pl.when((pl.program_id(0) == 0) & (pl.program_id(1) == 0) & (i == 0))
        def _():
            _dil_build_bias(tab_ref, bkt_ref, bias_ref, dil)
            dsc_ref[...] = jnp.zeros_like(dsc_ref)

        @pl.when(i == 0)
        def _():
            _dil_fill_pad(kpad, k_ref, length)
            _dil_fill_pad(vpad, v_ref, length)
            dk_ref[...] = jnp.zeros_like(dk_ref)
            dv_ref[...] = jnp.zeros_like(dv_ref)

        r0 = pl.multiple_of(i * DIL_QB, DIL_QB)
        valid = _dil_valid(r0, length)
        for m in range(N_HEADS // 2):
            lanes = pl.ds(m * 128, 128)
            dq2, dk2, dv2, ds_heads = _attn_pair_bwd(
                q_ref[:, lanes], kpad[pl.ds(r0, DIL_WIN), lanes], vpad[pl.ds(r0, DIL_WIN), lanes],
                (bias_ref[2 * m], bias_ref[2 * m + 1]), valid, do_ref[:, lanes], dlse_ref[:, lanes])
            dq_ref[:, lanes] = dq2
            dsc_ref[2 * m] += ds_heads[0]
            dsc_ref[2 * m + 1] += ds_heads[1]
            for first, size, live in ((0, DIL_HALF, i >= 1), (DIL_HALF, DIL_QB, None), (DIL_HALF + DIL_QB, DIL_HALF, i <= nqb - 2)):
                def add(first=first, size=size, dk2=dk2, dv2=dv2, lanes=lanes):
                    rows = pl.ds(pl.multiple_of(r0 - DIL_HALF + first, DIL_HALF), size)
                    dk_ref[rows, lanes] += dk2[first:first + size]
                    dv_ref[rows, lanes] += dv2[first:first + size]
                if live is None:
                    add()
                else:
                    pl.when(live)(add)

    out = jax.ShapeDtypeStruct((bsz, length, dil * GROUP_WIDTH), F32)
    dsc_shape = (N_HEADS, DIL_QB, DIL_WIN)
    dq, dk, dv, dsc = pl.pallas_call(
        body, name=name, grid=(bsz, dil, nqb),
        in_specs=[pl.BlockSpec(memory_space=pltpu.SMEM), pl.BlockSpec(bkt.shape, lambda b, rho, i: (0, 0)), blk, seq, seq, blk, blk],
        out_specs=[blk, seq, seq, pl.BlockSpec(dsc_shape, lambda b, rho, i: (0, 0, 0))],
        out_shape=[out, out, out, jax.ShapeDtypeStruct(dsc_shape, F32)],
        scratch_shapes=[pltpu.VMEM((length + 2 * DIL_HALF, GROUP_WIDTH), BF16), pltpu.VMEM((length + 2 * DIL_HALF, GROUP_WIDTH), BF16),
                        pltpu.VMEM(dsc_shape, F32)],
        compiler_params=_cparams(("arbitrary", "arbitrary", "arbitrary")),
    )(table, bkt, view(qb), view(kb), view(vb), view(do), view(dlse))
    return dq.reshape(qb.shape), dk.reshape(qb.shape), dv.reshape(qb.shape), dsc


def _headnorm(x, g, mavg):
    return x * lax.rsqrt(_head_stat(x * x, mavg) + RMS_EPS) * g


def _fold_gain(dg_full, fold):
    return jnp.dot(jnp.broadcast_to(dg_full, (8, dg_full.shape[1])), fold, precision=HI, preferred_element_type=F32)


def _bprep_fn(qp, kp, gq, gk, mavg):
    return _headnorm(qp, gq, mavg) * (HEAD_DIM ** -0.5), _headnorm(kp, gk, mavg)


def _dil_layout(dil):
    return None if dil == 1 else Strided(dil)


def _bprep_fwd(z, gq, gk, name):
    mavg = jnp.asarray(_np_group_avg(GROUP_WIDTH))

    def fn(qp, kp, vp, gqv, gkv, mv):
        qb, kb = _bprep_fn(qp, kp, gqv, gkv, mv)
        return (qb, kb, vp) * len(DIL_PATTERNS)

    w = GROUP_WIDTH
    outs = [(w, BF16, _dil_layout(dil)) for _, dil in DIL_PATTERNS for _ in range(3)]
    res = _rowmap(fn, [(z, w, 2), (z, w, 3), (z, w, 4)], [gq, gk, mavg], outs, [], name, 512, z.shape[0])
    return [res[3 * i:3 * i + 3] for i in range(len(DIL_PATTERNS))]


def _bprep_bwd(z, dqs, dks, dvs, gq, gk, name):
    mavg = jnp.asarray(_np_group_avg(GROUP_WIDTH))
    fold = jnp.asarray(_np_tile_fold(GROUP_WIDTH))

    def fn(qp, kp, dq0, dq1, dq2, dk0, dk1, dk2, dv0, dv1, dv2, gqv, gkv, mv, fv):
        _, vjp = jax.vjp(functools.partial(_bprep_fn, mavg=mv), qp, kp, gqv, gkv)
        dqp, dkp, dgq, dgk = vjp((dq0 + dq1 + dq2, dk0 + dk1 + dk2))
        return dqp, dkp, dv0 + dv1 + dv2, _fold_gain(dgq, fv), _fold_gain(dgk, fv)

    w = GROUP_WIDTH
    rows = [(z, w, 2), (z, w, 3)] + _pattern_rows(dqs) + _pattern_rows(dks) + _pattern_rows(dvs)
    return _rowmap(fn, rows, [gq, gk, mavg, fold], [(w, F32)] * 3, [(8, HEAD_DIM)] * 2, name, 256, z.shape[0])


def _mixture_fn(o0, o1, o2, l0, l1, l2):
    m = lax.stop_gradient(jnp.maximum(jnp.maximum(l0, l1), l2))
    e0, e1, e2 = jnp.exp(l0 - m), jnp.exp(l1 - m), jnp.exp(l2 - m)
    return (e0 * o0 + e1 * o1 + e2 * o2) / (e0 + e1 + e2)


def _pattern_rows(arrs):
    return [(a, GROUP_WIDTH, 0) if dil == 1 else (a, GROUP_WIDTH, 0, Strided(dil)) for a, (_, dil) in zip(arrs, DIL_PATTERNS)]


def _mixture_fwd(os_, ls_, name):
    n = os_[0].shape[0]
    return _rowmap(lambda *v: (_mixture_fn(*v),), _pattern_rows(os_) + _pattern_rows(ls_), [], [(GROUP_WIDTH, F32)], [], name, 512, n)[0]


def _mixture_bwd(os_, ls_, dy, name):
    w = GROUP_WIDTH

    def fn(*v):
        _, vjp = jax.vjp(_mixture_fn, *v[:6])
        return vjp(v[6])

    outs = [(w, F32, _dil_layout(dil)) for _ in range(2) for _, dil in DIL_PATTERNS]
    return _rowmap(fn, _pattern_rows(os_) + _pattern_rows(ls_) + [(dy, w, 0)], [], outs, [], name, 512, dy.shape[0])


def _relbias_fold(dscs, name):
    bkts = [jnp.asarray(_np_dil_buckets(dil)) for _, dil in DIL_PATTERNS]
    npat = len(DIL_PATTERNS)

    def body(*refs):
        bkt_refs, d_refs, o_ref = refs[:npat], refs[npat:-1], refs[-1]
        row = lax.broadcasted_iota(jnp.int32, (REL_BUCKETS, 128), 0)
        lane = lax.broadcasted_iota(jnp.int32, (REL_BUCKETS, 128), 1)
        out = jnp.zeros((REL_BUCKETS, 128), F32)
        for p, (_, dil) in enumerate(DIL_PATTERNS):
            bkt = bkt_refs[p][...]
            for h in range(N_HEADS):
                d = d_refs[2 * p][h] + d_refs[2 * p + 1][h]
                for b in _dil_live_buckets(dil):
                    val = jnp.sum(jnp.where(bkt == b, d, 0.0), keepdims=True)
                    out = out + jnp.where((row == b) & (lane == h), val, 0.0)
        o_ref[...] = out

    return pl.pallas_call(
        body, name=name, out_shape=jax.ShapeDtypeStruct((REL_BUCKETS, 128), F32), compiler_params=_cparams(),
    )(*bkts, *dscs)


DPREP_TM = 512


def _dprep_fn(qp, kp, vp, gq, gk, cq, sq, ck, sk, mavg_q, mavg_k, perm_q, perm_k, expand):
    rot = lambda x, perm: jnp.dot(x, perm, precision=HI, preferred_element_type=F32)
    qn = _headnorm(qp, gq, mavg_q)
    kn = _headnorm(kp, gk, mavg_k)
    qr = (qn * cq + rot(qn, perm_q) * sq) * (HEAD_DIM ** -0.5)
    kr = kn * ck + rot(kn, perm_k) * sk
    return qr, rot(kr, expand), rot(vp, expand)


def _dprep_consts():
    cq, sq = _np_rope_tables(N_HEADS)
    ck, sk = _np_rope_tables(KV_WIDTH // HEAD_DIM)
    tables = [jnp.asarray(a) for a in (cq, sq, ck, sk)]
    mats = [jnp.asarray(a) for a in (_np_group_avg(GROUP_WIDTH), _np_group_avg(KV_WIDTH), _np_rope_partner(GROUP_WIDTH),
                                      _np_rope_partner(KV_WIDTH), _np_kv_expand())]
    per = SEQ // DPREP_TM
    w, kw = GROUP_WIDTH, KV_WIDTH
    table_rows = [(tables[0], w, 0, per), (tables[1], w, 0, per), (tables[2], kw, 0, per), (tables[3], kw, 0, per)]
    return table_rows, mats


def _dprep_fwd(z, gq, gk, name):
    table_rows, mats = _dprep_consts()
    w, kw = GROUP_WIDTH, KV_WIDTH

    def fn(qp, kp, vp, cq, sq, ck, sk, gqv, gkv, *m):
        return _dprep_fn(qp, kp, vp, gqv, gkv, cq, sq, ck, sk, *m)

    return _rowmap(fn, [(z, w, 7), (z, kw, 32), (z, kw, 33)] + table_rows, [gq, gk] + mats, [(w, BF16)] * 3, [], name,
                   DPREP_TM, z.shape[0])


def _dprep_bwd(z, dq, dkx, dvx, gq, gk, name):
    table_rows, mats = _dprep_consts()
    fold_q = jnp.asarray(_np_tile_fold(GROUP_WIDTH))
    fold_k = jnp.asarray(_np_tile_fold(KV_WIDTH))
    w, kw = GROUP_WIDTH, KV_WIDTH

    def fn(qp, kp, vp, dqv, dkv, dvv, cq, sq, ck, sk, gqv, gkv, fq, fk, *m):
        f = lambda a, b, c, d, e: _dprep_fn(a, b, c, d, e, cq, sq, ck, sk, *m)
        _, vjp = jax.vjp(f, qp, kp, vp, gqv, gkv)
        dqp, dkp, dvp, dgq, dgk = vjp((dqv, dkv, dvv))
        return dqp, dkp, dvp, _fold_gain(dgq, fq), _fold_gain(dgk, fk)

    return _rowmap(fn, [(z, w, 7), (z, kw, 32), (z, kw, 33), (dq, w, 0), (dkx, w, 0), (dvx, w, 0)] + table_rows,
                   [gq, gk, fold_q, fold_k] + mats, [(w, F32), (kw, F32), (kw, F32)], [(8, HEAD_DIM)] * 2, name,
                   DPREP_TM, z.shape[0])


GQA_QB = 256


def _gqa_fwd(q, kx, vx, name):
    bsz = q.shape[0]
    blk = pl.BlockSpec((None, GQA_QB, GROUP_WIDTH), lambda b, i: (b, i, 0))
    seq = pl.BlockSpec((None, SEQ, GROUP_WIDTH), lambda b, i: (b, 0, 0))

    def body(q_ref, k_ref, v_ref, o_ref):
        for m in range(N_HEADS // 2):
            lanes = pl.ds(m * 128, 128)
            o_ref[:, lanes] = _attn_pair_fwd(q_ref[:, lanes], k_ref[:, lanes], v_ref[:, lanes], None, None)[0]

    return pl.pallas_call(
        body, name=name, grid=(bsz, SEQ // GQA_QB), in_specs=[blk, seq, seq], out_specs=blk,
        out_shape=jax.ShapeDtypeStruct((bsz, SEQ, GROUP_WIDTH), F32), compiler_params=_cparams(("parallel", "parallel")),
    )(q, kx, vx)


def _gqa_bwd(q, kx, vx, do, name):
    bsz = q.shape[0]
    blk = pl.BlockSpec((None, GQA_QB, GROUP_WIDTH), lambda b, i: (b, i, 0))
    seq = pl.BlockSpec((None, SEQ, GROUP_WIDTH), lambda b, i: (b, 0, 0))

    def body(q_ref, k_ref, v_ref, do_ref, dq_ref, dk_ref, dv_ref):
        @pl.when(pl.program_id(1) == 0)
        def _():
            dk_ref[...] = jnp.zeros_like(dk_ref)
            dv_ref[...] = jnp.zeros_like(dv_ref)

        for m in range(N_HEADS // 2):
            lanes = pl.ds(m * 128, 128)
            dq2, dk2, dv2, _ = _attn_pair_bwd(q_ref[:, lanes], k_ref[:, lanes], v_ref[:, lanes], None, None, do_ref[:, lanes], None)
            dq_ref[:, lanes] = dq2
            dk_ref[:, lanes] += dk2
            dv_ref[:, lanes] += dv2

    out = jax.ShapeDtypeStruct((bsz, SEQ, GROUP_WIDTH), F32)
    return pl.pallas_call(
        body, name=name, grid=(bsz, SEQ // GQA_QB), in_specs=[blk, seq, seq, blk], out_specs=[blk, seq, seq],
        out_shape=[out, out, out], compiler_params=_cparams(("parallel", "arbitrary")),
    )(q, kx, vx, do)


CONV_TILE = 64
CONV_LEAD = 16
CONV_WINDOW = CONV_TILE + 32


def _glu(a, g):
    return a * jax.nn.sigmoid(g)


def _conv_post(c, b, ln_g, ln_b):
    x = c + b
    xc = x - jnp.mean(x, axis=-1, keepdims=True)
    y = xc * lax.rsqrt(jnp.mean(xc * xc, axis=-1, keepdims=True) + LN_EPS) * ln_g + ln_b
    return y * jax.nn.sigmoid(y)


def _conv_shifted(win, offset):
    return pltpu.roll(win, CONV_WINDOW - offset, 0)[:CONV_TILE]


def _conv_fill(pad_ref, value_of_tile):
    zeros = jnp.zeros((CONV_LEAD, GROUP_WIDTH), F32)
    pad_ref[pl.ds(0, CONV_LEAD), :] = zeros
    pad_ref[pl.ds(CONV_LEAD + SEQ, CONV_LEAD), :] = zeros

    def step(t, carry):
        r0 = pl.multiple_of(t * CONV_TILE, CONV_TILE)
        pad_ref[pl.ds(CONV_LEAD + r0, CONV_TILE), :] = value_of_tile(r0)
        return carry

    lax.fori_loop(0, SEQ // CONV_TILE, step, 0)


def _conv_tile(pad_ref, w_ref, r0, flip):
    win = pad_ref[pl.ds(r0, CONV_WINDOW), :]
    acc = jnp.zeros((CONV_TILE, GROUP_WIDTH), F32)
    for k in range(CONV_WIDTH):
        offset = (CONV_WIDTH - k) if flip else (k + 1)
        acc = acc + w_ref[pl.ds(k, 1), :] * _conv_shifted(win, offset)
    return acc


def _conv_fwd(z3, w, b, ln_g, ln_b, name):
    bsz = z3.shape[0]
    seq = lambda cb: pl.BlockSpec((None, SEQ, GROUP_WIDTH), functools.partial(lambda i, cb: (i, 0, cb), cb=cb))
    full = lambda a: pl.BlockSpec(a.shape, lambda i: (0,) * a.ndim)

    def body(a_ref, g_ref, w_ref, b_ref, lg_ref, lb_ref, y_ref, pad_ref):
        _conv_fill(pad_ref, lambda r0: _glu(a_ref[pl.ds(r0, CONV_TILE), :], g_ref[pl.ds(r0, CONV_TILE), :]))

        def step(t, carry):
            r0 = pl.multiple_of(t * CONV_TILE, CONV_TILE)
            y_ref[pl.ds(r0, CONV_TILE), :] = _conv_post(_conv_tile(pad_ref, w_ref, r0, False), b_ref[...], lg_ref[...], lb_ref[...])
            return carry

        lax.fori_loop(0, SEQ // CONV_TILE, step, 0)

    return pl.pallas_call(
        body, name=name, grid=(bsz,), in_specs=[seq(5), seq(6), full(w), full(b), full(ln_g), full(ln_b)], out_specs=seq(0),
        out_shape=jax.ShapeDtypeStruct((bsz, SEQ, GROUP_WIDTH), F32),
        scratch_shapes=[pltpu.VMEM((SEQ + 2 * CONV_LEAD, GROUP_WIDTH), F32)], compiler_params=_cparams(("parallel",)),
    )(z3, z3, w, b, ln_g, ln_b)


def _conv_bwd(z3, dy, w, b, ln_g, ln_b, name):
    bsz = z3.shape[0]
    seq = lambda cb: pl.BlockSpec((None, SEQ, GROUP_WIDTH), functools.partial(lambda i, cb: (i, 0, cb), cb=cb))
    full = lambda a: pl.BlockSpec(a.shape, lambda i: (0,) * a.ndim)
    vec = pl.BlockSpec((1, GROUP_WIDTH), lambda i: (0, 0))

    def body(a_ref, g_ref, dy_ref, w_ref, b_ref, lg_ref, lb_ref, da_ref, dg_ref, dw_ref, db_ref, dlg_ref, dlb_ref, hpad, dpad, dw8):
        @pl.when(pl.program_id(0) == 0)
        def _():
            dw8[...] = jnp.zeros_like(dw8)
            db_ref[...] = jnp.zeros_like(db_ref)
            dlg_ref[...] = jnp.zeros_like(dlg_ref)
            dlb_ref[...] = jnp.zeros_like(dlb_ref)

        _conv_fill(hpad, lambda r0: _glu(a_ref[pl.ds(r0, CONV_TILE), :], g_ref[pl.ds(r0, CONV_TILE), :]))
        zeros = jnp.zeros((CONV_LEAD, GROUP_WIDTH), F32)
        dpad[pl.ds(0, CONV_LEAD), :] = zeros
        dpad[pl.ds(CONV_LEAD + SEQ, CONV_LEAD), :] = zeros

        def through_post(t, carry):
            r0 = pl.multiple_of(t * CONV_TILE, CONV_TILE)
            conv = _conv_tile(hpad, w_ref, r0, False)
            _, vjp = jax.vjp(_conv_post, conv, b_ref[...], lg_ref[...], lb_ref[...])
            dconv, db, dlg, dlb = vjp(dy_ref[pl.ds(r0, CONV_TILE), :])
            db_ref[...] += db
            dlg_ref[...] += dlg
            dlb_ref[...] += dlb
            dpad[pl.ds(CONV_LEAD + r0, CONV_TILE), :] = dconv
            win = hpad[pl.ds(r0, CONV_WINDOW), :]
            for k in range(CONV_WIDTH):
                prod = dconv * _conv_shifted(win, k + 1)
                part = prod[0:8]
                for j in range(1, CONV_TILE // 8):
                    part = part + prod[8 * j:8 * j + 8]
                dw8[k] += part
            return carry

        lax.fori_loop(0, SEQ // CONV_TILE, through_post, 0)

        def through_glu(t, carry):
            r0 = pl.multiple_of(t * CONV_TILE, CONV_TILE)
            dh = _conv_tile(dpad, w_ref, r0, True)
            rows = pl.ds(r0, CONV_TILE)
            _, vjp = jax.vjp(_glu, a_ref[rows, :], g_ref[rows, :])
            da, dg = vjp(dh)
            da_ref[rows, :] = da
            dg_ref[rows, :] = dg
            return carry

        lax.fori_loop(0, SEQ // CONV_TILE, through_glu, 0)
        dw_ref[...] = jnp.sum(dw8[...], axis=1)

    out = jax.ShapeDtypeStruct((bsz, SEQ, GROUP_WIDTH), F32)
    v = jax.ShapeDtypeStruct((1, GROUP_WIDTH), F32)
    return pl.pallas_call(
        body, name=name, grid=(bsz,), in_specs=[seq(5), seq(6), seq(0), full(w), full(b), full(ln_g), full(ln_b)],
        out_specs=[seq(0), seq(0), pl.BlockSpec((CONV_WIDTH, GROUP_WIDTH), lambda i: (0, 0)), vec, vec, vec],
        out_shape=[out, out, jax.ShapeDtypeStruct((CONV_WIDTH, GROUP_WIDTH), F32), v, v, v],
        scratch_shapes=[pltpu.VMEM((SEQ + 2 * CONV_LEAD, GROUP_WIDTH), F32), pltpu.VMEM((SEQ + 2 * CONV_LEAD, GROUP_WIDTH), F32),
                        pltpu.VMEM((CONV_WIDTH, 8, GROUP_WIDTH), F32)],
        compiler_params=_cparams(("arbitrary",)),
    )(z3, z3, dy, w, b, ln_g, ln_b)


def _mixnorm_fwd(ys, gains, name):
    w = GROUP_WIDTH

    def fn(*v):
        return (jnp.concatenate([_rms(v[i], v[4 + i]) for i in range(4)], axis=-1),)

    return _rowmap(fn, [(y, w, 0) for y in ys], list(gains), [(4 * w, BF16)], [], name, 512, ys[0].shape[0])[0]


def _mixnorm_bwd(dyn, ys, gains, name, follow=None):
    w = GROUP_WIDTH
    gains = list(gains) if follow is None else [*gains, follow]

    def fn(*v):
        dys, dgs = [], []
        for i in range(4):
            _, vjp = jax.vjp(_rms, v[4 + i], v[8 + i])
            dy, dg = vjp(v[i])
            dys.append(dy)
            dgs.append(dg)
        return (*dys, *dgs)

    rows = [(dyn, w, i) for i in range(4)] + [(y, w, 0) for y in ys]
    return _rowmap(fn, rows, list(gains), [(w, F32)] * 4, [(1, w)] * 4, name, 512, dyn.shape[0])


def _adamw_fn(w, g, m, v):
    m = ADAM_B1 * m + (1.0 - ADAM_B1) * g
    v = ADAM_B2 * v + (1.0 - ADAM_B2) * (g * g)
    m_hat = m / (1.0 - ADAM_B1 ** ADAM_STEP)
    v_hat = v / (1.0 - ADAM_B2 ** ADAM_STEP)
    delta = -ADAM_LR * (m_hat / (jnp.sqrt(v_hat) + ADAM_EPS) + ADAM_WD * w)
    return delta, m, v


def _adamw(w, g, m, v, name):
    r, c = w.shape
    tm = _pick(r, (256, 128, 64, 32, 16, 8))
    return _rowmap(_adamw_fn, [(a, c, 0) for a in (w, g, m, v)], [], [(c, F32)] * 3, [], name, tm, r)


def _layer_params(l, small, big):
    tile_row = lambda g, n: jnp.tile(g, n)[None, :]
    row = lambda g: g[None, :]
    w_s = small["sgu_w"][l].astype(BF16)
    return dict(
        norm1_g=row(small["norm1_g"][l]), norm2_g=row(small["norm2_g"][l]),
        w_s=w_s, w_s_t=jnp.swapaxes(w_s, 1, 2), bm=jnp.repeat(small["sgu_b"][l].T, HEAD_DIM, axis=1),
        gq_dil=tile_row(small["dil_qn_g"][l], N_HEADS), gk_dil=tile_row(small["dil_kn_g"][l], N_HEADS),
        conv_w=small["conv_w"][l], conv_b=row(small["conv_b"][l]), conv_ln_g=row(small["conv_ln_g"][l]),
        conv_ln_b=row(small["conv_ln_b"][l]),
        gq_gqa=tile_row(small["gqa_qn_g"][l], N_HEADS), gk_gqa=tile_row(small["gqa_kn_g"][l], KV_WIDTH // HEAD_DIM),
        mix_g=[row(small["mix_norm_g"][l][i * GROUP_WIDTH:(i + 1) * GROUP_WIDTH]) for i in range(4)],
        big=big,
    )


def _layer_fwd(x, p, table, bsz, tag):
    t = x.shape[0]
    seq3 = lambda a: a.reshape(bsz, SEQ, a.shape[-1])
    flat = lambda a: a.reshape(t, a.shape[-1])
    h1 = _rmsnorm_fwd(x, p["norm1_g"], tag + "rms1")
    z = _matmul([(h1, p["big"]("w_in", h1))], "nn", F32, tag + "mm_z")
    y_a = _sgu_fwd(z, p["w_s"], p["bm"], tag + "sgu_fwd")
    dil_qkv = _bprep_fwd(z, p["gq_dil"], p["gk_dil"], tag + "dil_prep")
    outs, lses = [], []
    for (_, dil), (qb, kb, vb) in zip(DIL_PATTERNS, dil_qkv):
        o, lse = _dil_fwd(qb, kb, vb, table, dil, f"{tag}dil{dil}_fwd")
        outs.append(o)
        lses.append(lse)
    y_b = _mixture_fwd(outs, lses, tag + "dil_mix")
    y_c = flat(_conv_fwd(seq3(z), p["conv_w"], p["conv_b"], p["conv_ln_g"], p["conv_ln_b"], tag + "conv_fwd"))
    qd, kx, vx = _dprep_fwd(z, p["gq_gqa"], p["gk_gqa"], tag + "gqa_prep")
    y_d = flat(_gqa_fwd(seq3(qd), seq3(kx), seq3(vx), tag + "gqa_fwd"))
    ys = [y_a, y_b, y_c, y_d]
    yn = _mixnorm_fwd(ys, p["mix_g"], tag + "mixnorm")
    x_mid = _matmul([(yn, p["big"]("w_out", yn))], "nn", F32, tag + "mm_out", residual=x)
    h2 = _rmsnorm_fwd(x_mid, p["norm2_g"], tag + "rms2")
    act, act_du, act_dg = _ffn_up(h2, p["big"]("w_gate", yn), p["big"]("w_up", yn), tag + "ffn_up")
    x_out = _matmul([(act, p["big"]("w_down", yn))], "nn", F32, tag + "mm_down", residual=x_mid)
    saved = dict(x=x, h1=h1, z=z, dil_qkv=dil_qkv, outs=outs, lses=lses, qd=qd, kx=kx, vx=vx, ys=ys, yn=yn, x_mid=x_mid,
                 h2=h2, act=act, act_du=act_du, act_dg=act_dg)
    return x_out, saved


def _layer_bwd(dx_out, dx_out_b, s, p, table, bsz, tag, emit, mid_hook):
    t = dx_out.shape[0]
    seq3 = lambda a: a.reshape(bsz, SEQ, a.shape[-1])
    flat = lambda a: a.reshape(t, a.shape[-1])
    z = s["z"]
    small = {}
    weight = lambda name: p["big"](name, None)
    emit("w_down", _matmul([(s["act"], dx_out_b)], "tn", BF16, tag + "mm_dwdown").reshape(N_CHIPS, FFN_HIDDEN // N_CHIPS, D_MODEL))
    dgate, dup = _ffn_down_bwd(dx_out_b, weight("w_down"), s["act_du"], s["act_dg"], tag + "ffn_dact")
    emit("w_gate", _matmul([(s["h2"], dgate)], "tn", BF16, tag + "mm_dwgate", slabs=N_CHIPS))
    started = emit("w_up", _matmul([(s["h2"], dup)], "tn", BF16, tag + "mm_dwup", slabs=N_CHIPS))
    dh2 = _matmul([(dgate, weight("w_gate")), (dup, weight("w_up"))], "nt", F32, tag + "mm_dh2")
    dx_mid, dx_mid_b, dg2 = _rmsnorm_bwd(dh2, s["x_mid"], p["norm2_g"], dx_out, tag + "rms2_bwd", follow=started)
    small["norm2_g"] = dg2[0]
    mid_hook(dx_mid)
    dyn = _matmul([(dx_mid_b, weight("w_out"))], "nt", F32, tag + "mm_dyn")
    started = emit("w_out", _matmul([(s["yn"], dx_mid_b)], "tn", BF16, tag + "mm_dwout").reshape(N_CHIPS, D_MODEL // N_CHIPS, D_MODEL))
    *dys, dga, dgb, dgc, dgd = _mixnorm_bwd(dyn, s["ys"], p["mix_g"], tag + "mixnorm_bwd", follow=started)
    small["mix_norm_g"] = jnp.concatenate([dga[0], dgb[0], dgc[0], dgd[0]])
    du, dv, dws, dbs = _sgu_bwd(z, dys[0], p["w_s"], p["w_s_t"], p["bm"], tag + "sgu_bwd")
    small["sgu_w"] = dws
    small["sgu_b"] = dbs[:, :N_HEADS].T
    *douts, dl0, dl1, dl2 = _mixture_bwd(s["outs"], s["lses"], dys[1], tag + "dil_mix_bwd")
    dlses = [dl0, dl1, dl2]
    dqs, dks, dvs, dscs = [], [], [], []
    for i, (_, dil) in enumerate(DIL_PATTERNS):
        dq, dk, dvv, dsc = _dil_bwd(*s["dil_qkv"][i], douts[i], dlses[i], table, dil, f"{tag}dil{dil}_bwd")
        dqs.append(dq)
        dks.append(dk)
        dvs.append(dvv)
        dscs.append(dsc)
    dbq, dbk, dbv, dgq, dgk = _bprep_bwd(z, dqs, dks, dvs, p["gq_dil"], p["gk_dil"], tag + "dil_prep_bwd")
    small["dil_qn_g"], small["dil_kn_g"] = dgq[0], dgk[0]
    dca, dcg, dcw, dcb, dclg, dclb = _conv_bwd(seq3(z), seq3(dys[2]), p["conv_w"], p["conv_b"], p["conv_ln_g"], p["conv_ln_b"],
                                               tag + "conv_bwd")
    small["conv_w"], small["conv_b"], small["conv_ln_g"], small["conv_ln_b"] = dcw, dcb[0], dclg[0], dclb[0]
    dqd, dkx, dvx = _gqa_bwd(seq3(s["qd"]), seq3(s["kx"]), seq3(s["vx"]), seq3(dys[3]), tag + "gqa_bwd")
    ddq, ddk, ddv, dgq, dgk = _dprep_bwd(z, flat(dqd), flat(dkx), flat(dvx), p["gq_gqa"], p["gk_gqa"], tag + "gqa_prep_bwd")
    small["gqa_qn_g"], small["gqa_kn_g"] = dgq[0], dgk[0]
    dz = jnp.concatenate([a.astype(BF16) for a in (du, dv, dbq, dbk, dbv, flat(dca), flat(dcg), ddq, ddk, ddv)], axis=1)
    dz4 = dz.reshape(t, N_CHIPS, IN_WIDTH // N_CHIPS).transpose(1, 0, 2)
    started = emit("w_in", _matmul([(s["h1"], dz4)], "tn", BF16, tag + "mm_dwin", slabs=N_CHIPS))
    dh1 = _matmul([(dz, weight("w_in"))], "nt", F32, tag + "mm_dh1")
    dx, dx_b, dg1 = _rmsnorm_bwd(dh1, s["x"], p["norm1_g"], dx_mid, tag + "rms1_bwd", follow=started)
    small["norm1_g"] = dg1[0]
    return dx, dx_b, small, dscs


def _local_step(x, target, small, big, emit, mid_hook, bsz):
    table = small["rel_bias"]
    params = [_layer_params(l, small, functools.partial(big, l)) for l in range(DEPTH)]
    saved = []
    h = x
    for l in range(DEPTH):
        h, sv = _layer_fwd(h, params[l], table, bsz, f"l{l}_")
        saved.append(sv)
    dh, dh_b, loss = _loss_fwd_bwd(h, target, "loss")
    small_grads, dscs = [None] * DEPTH, [None] * DEPTH
    for l in reversed(range(DEPTH)):
        dh, dh_b, small_grads[l], dscs[l] = _layer_bwd(dh, dh_b, saved[l], params[l], table, bsz, f"l{l}_",
                                                       functools.partial(emit, l), functools.partial(mid_hook, l))
    fold_in = [dscs[l][i] for i in range(len(DIL_PATTERNS)) for l in range(DEPTH)]
    stacked = {k: jnp.stack([small_grads[l][k] for l in range(DEPTH)]) for k in small_grads[0]}
    stacked["rel_bias"] = _relbias_fold(fold_in, "relbias_fold")[:, :N_HEADS]
    return loss, dh, stacked


def _mesh_pos():
    return lax.axis_index("x"), lax.axis_index("y"), lax.axis_index("c")


def _other_chips(x, y):
    return [(1 - x, y), (x, 1 - y), (1 - x, 1 - y)]


_ANY = pl.BlockSpec(memory_space=pl.ANY)


def _swap_other_half(arrs, name):
    n = len(arrs)

    def body(*refs):
        in_refs, out_refs, send_sems, recv_sems = refs[:n], refs[n:2 * n], refs[2 * n], refs[2 * n + 1]
        x, y, c = _mesh_pos()
        copies = []
        for k in range(n):
            h = arrs[k].shape[1] // 2
            copies.append(pltpu.make_async_remote_copy(
                src_ref=in_refs[k].at[:, pl.ds((1 - c) * h, h)], dst_ref=out_refs[k], send_sem=send_sems.at[k],
                recv_sem=recv_sems.at[k], device_id=(x, y, 1 - c), device_id_type=MESH))
        for cp in copies:
            cp.start()
        for cp in copies:
            cp.wait()

    return pl.pallas_call(
        body, name=name, in_specs=[_ANY] * n, out_specs=[_ANY] * n,
        out_shape=[jax.ShapeDtypeStruct((a.shape[0], a.shape[1] // 2, a.shape[2]), a.dtype) for a in arrs],
        scratch_shapes=[pltpu.SemaphoreType.DMA((n,)), pltpu.SemaphoreType.DMA((n,))],
    )(*arrs)


def _swap_sibling(arrs, name):
    n = len(arrs)

    def body(*refs):
        in_refs, out_refs, send_sems, recv_sems = refs[:n], refs[n:2 * n], refs[2 * n], refs[2 * n + 1]
        x, y, c = _mesh_pos()
        copies = [pltpu.make_async_remote_copy(src_ref=in_refs[k], dst_ref=out_refs[k], send_sem=send_sems.at[k],
                                               recv_sem=recv_sems.at[k], device_id=(x, y, 1 - c), device_id_type=MESH)
                  for k in range(n)]
        for cp in copies:
            cp.start()
        for cp in copies:
            cp.wait()

    return pl.pallas_call(
        body, name=name, in_specs=[_ANY] * n, out_specs=[_ANY] * n,
        out_shape=[jax.ShapeDtypeStruct(a.shape, a.dtype) for a in arrs],
        scratch_shapes=[pltpu.SemaphoreType.DMA((n,)), pltpu.SemaphoreType.DMA((n,))],
    )(*arrs)


def _complete_pairs(arrs, name):
    n = len(arrs)

    def body(*refs):
        in_refs, out_refs, send_sems, recv_sems = refs[:n], refs[n:2 * n], refs[2 * n], refs[2 * n + 1]
        x, y, c = _mesh_pos()
        copies = [pltpu.make_async_remote_copy(src_ref=in_refs[k].at[c], dst_ref=out_refs[k].at[c], send_sem=send_sems.at[k],
                                               recv_sem=recv_sems.at[k], device_id=(x, y, 1 - c), device_id_type=MESH)
                  for k in range(n)]
        for cp in copies:
            cp.start()
        for k, cp in enumerate(copies):
            cp.wait_send()
            pltpu.make_async_remote_copy(src_ref=in_refs[k].at[1 - c], dst_ref=out_refs[k].at[1 - c], send_sem=send_sems.at[k],
                                         recv_sem=recv_sems.at[k], device_id=(x, y, 1 - c), device_id_type=MESH).wait_recv()

    return pl.pallas_call(
        body, name=name, in_specs=[_ANY] * n, out_specs=[_ANY] * n,
        out_shape=[jax.ShapeDtypeStruct(a.shape, a.dtype) for a in arrs], input_output_aliases={k: k for k in range(n)},
        scratch_shapes=[pltpu.SemaphoreType.DMA((n,)), pltpu.SemaphoreType.DMA((n,))],
    )(*arrs)


_HBM = pl.BlockSpec(memory_space=pltpu.HBM)
_SEM = pl.BlockSpec(memory_space=pltpu.SEMAPHORE)
_DATAFLOW = pltpu.SideEffectType.DATAFLOW_SIDE_EFFECTING


def _chip_copies(src_refs, land_refs, send_sems, recv_sems, scatter):
    x, y, c = _mesh_pos()
    me = 2 * x + y
    out = []
    for k, (src_ref, land_ref) in enumerate(zip(src_refs, land_refs)):
        slot = (lambda chip: land_ref.at[c, chip]) if len(land_ref.shape) == 4 else (lambda chip: land_ref.at[chip])
        for j, (cx, cy) in enumerate(_other_chips(x, y)):
            there = 2 * cx + cy
            src = src_ref.at[there] if scatter else src_ref
            sems = dict(send_sem=send_sems.at[3 * k + j], recv_sem=recv_sems.at[3 * k + j], device_id=(cx, cy, c), device_id_type=MESH)
            out.append((pltpu.make_async_remote_copy(src_ref=src, dst_ref=slot(me), **sems),
                        pltpu.make_async_remote_copy(src_ref=src, dst_ref=slot(there), **sems)))
    return out


def _chips_start(srcs, scatter, after, name, per_core=False):
    n = len(srcs)
    lead = (2, N_CHIPS) if per_core else (N_CHIPS,)
    lands = [lax.empty((*lead, *s.shape[-2:]), s.dtype) for s in srcs]

    def body(*refs):
        src_refs, land_refs = refs[:n], refs[n:2 * n]
        send_sems, recv_sems, token = refs[2 * n + 1], refs[2 * n + 2], refs[-1]
        for sent, _ in _chip_copies(src_refs, land_refs, send_sems, recv_sems, scatter):
            sent.start()
        token[...] = jnp.zeros_like(token)

    hbm = lambda a: pltpu.HBM(a.shape, a.dtype)
    res = pl.pallas_call(
        body, name=name,
        in_specs=[_HBM] * (2 * n) + [_ANY],
        out_specs=[_SEM, _SEM] + [_HBM] * (2 * n) + [pl.BlockSpec(memory_space=pltpu.VMEM)],
        out_shape=[pltpu.SemaphoreType.DMA((3 * n,)), pltpu.SemaphoreType.DMA((3 * n,))] + [hbm(a) for a in srcs] + [hbm(a) for a in lands]
        + [jax.ShapeDtypeStruct((8, 128), F32)],
        input_output_aliases={i: 2 + i for i in range(2 * n)},
        compiler_params=pltpu.CompilerParams(has_side_effects=_DATAFLOW),
    )(*[pltpu.with_memory_space_constraint(a, pltpu.HBM) for a in (*srcs, *lands)], after)
    return (res[0], res[1], res[2:2 + n], res[2 + n:2 + 2 * n]), res[-1]


def _chips_wait(handle, scatter, after, name):
    send_sems, recv_sems, srcs, lands = handle
    n = len(srcs)

    def body(*refs):
        src_refs, land_refs = refs[:n], refs[n:2 * n]
        send_sems, recv_sems = refs[2 * n], refs[2 * n + 1]
        for sent, landed in _chip_copies(src_refs, land_refs, send_sems, recv_sems, scatter):
            sent.wait_send()
            landed.wait_recv()

    hbm = lambda a: pltpu.HBM(a.shape, a.dtype)
    res = pl.pallas_call(
        body, name=name,
        in_specs=[_HBM] * (2 * n) + [_SEM, _SEM, _ANY], out_specs=[_HBM] * (2 * n),
        out_shape=[hbm(a) for a in srcs] + [hbm(a) for a in lands],
        input_output_aliases={i: i for i in range(2 * n)},
        compiler_params=pltpu.CompilerParams(has_side_effects=_DATAFLOW),
    )(*srcs, *lands, send_sems, recv_sems, after)
    return res[n:]


N_DEV = 8


def _allgather_sum_small(block, name):
    m_per, n = block.shape

    def body(x_ref, out_ref, sum_ref, send_sems, recv_sems, local_sem):
        x, y, c = _mesh_pos()
        me, sibling = (x, y, c), (x, y, 1 - c)
        chips = _other_chips(x, y)

        def rows(px, py, pc):
            return out_ref.at[pl.ds((4 * px + 2 * py + pc) * m_per, m_per), :]

        def copy(k, blk, to, src=None):
            return pltpu.make_async_remote_copy(src_ref=rows(*blk) if src is None else src, dst_ref=rows(*blk),
                                                send_sem=send_sems.at[k], recv_sem=recv_sems.at[k], device_id=to, device_id_type=MESH)

        mine = pltpu.make_async_copy(x_ref, rows(*me), local_sem)
        mine.start()
        first = [copy(0, me, sibling, src=x_ref)]
        first += [copy(1 + j, me, (*chip, c), src=x_ref) for j, chip in enumerate(chips)]
        for cp in first:
            cp.start()
        passed = [copy(4 + j, (*chip, c), sibling) for j, chip in enumerate(chips)]
        for j, chip in enumerate(chips):
            copy(1 + j, (*chip, c), me).wait_recv()
            passed[j].start()
        copy(0, sibling, me).wait_recv()
        for j, chip in enumerate(chips):
            copy(4 + j, (*chip, 1 - c), me).wait_recv()
        for cp in first + passed:
            cp.wait_send()
        mine.wait()
        total = out_ref[pl.ds(0, m_per), :]
        for d in range(1, N_DEV):
            total = total + out_ref[pl.ds(d * m_per, m_per), :]
        sum_ref[...] = total

    vmem = pl.BlockSpec(memory_space=pltpu.VMEM)
    return pl.pallas_call(
        body, name=name, in_specs=[vmem], out_specs=[vmem, vmem],
        out_shape=[jax.ShapeDtypeStruct((N_DEV * m_per, n), F32), jax.ShapeDtypeStruct((m_per, n), F32)],
        scratch_shapes=[pltpu.SemaphoreType.DMA((7,)), pltpu.SemaphoreType.DMA((7,)), pltpu.SemaphoreType.DMA],
        compiler_params=pltpu.CompilerParams(vmem_limit_bytes=V7X_VMEM_LIMIT),
    )(block)


WEIGHTS = ("rel_bias", "norm1_g", "w_in", "sgu_w", "sgu_b", "dil_qn_g", "dil_kn_g", "conv_w", "conv_b", "conv_ln_g", "conv_ln_b",
           "gqa_qn_g", "gqa_kn_g", "mix_norm_g", "w_out", "norm2_g", "w_gate", "w_up", "w_down")
SHARDED = ("w_in", "w_out", "w_gate", "w_up", "w_down")
COLUMN_SHARDED = ("w_in", "w_gate", "w_up")
REPLICATED = tuple(k for k in WEIGHTS if k not in SHARDED and k != "conv_w")


def _pack(parts):
    flat = jnp.concatenate([p.reshape(-1) for p in parts])
    pad = (-flat.shape[0]) % (8 * 128)
    return jnp.pad(flat, (0, pad)).reshape(-1, 128)


def _unpack(buf, shapes):
    flat = buf.reshape(-1)
    out, at = [], 0
    for s in shapes:
        size = math.prod(s)
        out.append(flat[at:at + size].reshape(s))
        at += size
    return out


RS_TM = (256, 128, 64, 32, 16)


def _add_halves(g, sib, c, name):
    slabs, h, cols = sib.shape
    tm = _pick(h, RS_TM)
    nb = h // tm

    def body(c_ref, g_ref, s_ref, o_ref):
        o_ref[...] = (g_ref[...].astype(F32) + s_ref[...].astype(F32)).astype(BF16)

    blk = pl.BlockSpec((None, tm, cols), lambda j, i, c_ref: (j, i, 0))
    grid_spec = pltpu.PrefetchScalarGridSpec(
        num_scalar_prefetch=1, grid=(slabs, nb),
        in_specs=[pl.BlockSpec((None, tm, cols), lambda j, i, c_ref: (j, c_ref[0] * nb + i, 0)), blk], out_specs=blk)
    return pl.pallas_call(body, name=name, grid_spec=grid_spec, out_shape=jax.ShapeDtypeStruct(sib.shape, BF16),
                          compiler_params=_cparams(("parallel", "parallel")))(c, g, sib)


def _add_own_three(own, land, chip, name):
    _, h, cols = own.shape
    tm = _pick(h, RS_TM)

    def body(chip_ref, own_ref, l0_ref, l1_ref, l2_ref, o_ref):
        o_ref[...] = ((own_ref[...].astype(F32) + l0_ref[...].astype(F32)) + l1_ref[...].astype(F32)) + l2_ref[...].astype(F32)

    slot = lambda r: pl.BlockSpec((None, tm, cols), functools.partial(lambda i, chip_ref, r: (jnp.bitwise_xor(chip_ref[0], r), i, 0), r=r))
    grid_spec = pltpu.PrefetchScalarGridSpec(
        num_scalar_prefetch=1, grid=(h // tm,),
        in_specs=[slot(0), slot(1), slot(2), slot(3)],
        out_specs=pl.BlockSpec((tm, cols), lambda i, chip_ref: (i, 0)))
    return pl.pallas_call(body, name=name, grid_spec=grid_spec, out_shape=jax.ShapeDtypeStruct((h, cols), F32),
                          compiler_params=_cparams(("parallel",)))(chip, own, land, land, land)


def _reduce_start(grads, c1, after, tag):
    from_sibling = _swap_other_half(grads, tag + "pair")
    sums = [_add_halves(g, s, c1, f"{tag}add2_{k}") for k, (g, s) in enumerate(zip(grads, from_sibling))]
    handle, token = _chips_start(sums, True, after, tag + "start")
    return (handle, sums), token


def _reduce_finish(started, chip1, after, tag):
    handle, sums = started
    lands = _chips_wait(handle, True, after, tag + "wait")
    totals = [_add_own_three(s, land, chip1, f"{tag}add4_{k}") for k, (s, land) in enumerate(zip(sums, lands))]
    return list(zip(totals, _swap_sibling(totals, tag + "share")))


def _adamw_shard(w, m, v, halves, c1, name):
    depth, rows, cols = w.shape
    h = rows // 2
    tm = _pick(h, RS_TM)
    nb = h // tm
    sources = [a for pair in halves for a in pair]

    def body(c_ref, w_ref, m_ref, v_ref, *refs):
        g_refs, (g_out, d_out, m_out, v_out) = refs[:2 * depth], refs[2 * depth:]
        layer, mine = pl.program_id(0), pl.program_id(1) == c_ref[0]
        g = None
        for l in range(depth):
            g_l = jnp.where(mine, g_refs[2 * l][...], g_refs[2 * l + 1][...])
            g = g_l if g is None else jnp.where(layer == l, g_l, g)
        delta, m_new, v_new = _adamw_fn(w_ref[...], g, m_ref[...], v_ref[...])
        g_out[...], d_out[...], m_out[...], v_out[...] = g, delta, m_new, v_new

    def source_spec(l, own):
        def index(layer, half, i, c_ref):
            return (jnp.where((layer == l) & ((half == c_ref[0]) == own), i, 0), 0)
        return pl.BlockSpec((tm, cols), index)

    blk = pl.BlockSpec((None, tm, cols), lambda layer, half, i, c_ref: (layer, half * nb + i, 0))
    grid_spec = pltpu.PrefetchScalarGridSpec(
        num_scalar_prefetch=1, grid=(depth, 2, nb),
        in_specs=[blk, blk, blk] + [source_spec(l, own) for l in range(depth) for own in (True, False)], out_specs=[blk] * 4)
    return pl.pallas_call(body, name=name, grid_spec=grid_spec, out_shape=[jax.ShapeDtypeStruct(w.shape, F32)] * 4,
                          compiler_params=_cparams(("arbitrary", "arbitrary", "arbitrary")))(c1, w, m, v, *sources)


GATHER_GROUPS = (("w_in",), ("w_out", "w_gate", "w_up", "w_down"))
REDUCE_GROUPS = (("w_down", "w_gate", "w_up"), ("w_out",), ("w_in",))


def kernel(x, rel_bias, norm1_g, w_in, sgu_w, sgu_b, dil_qn_g, dil_kn_g, conv_w, conv_b, conv_ln_g, conv_ln_b, gqa_qn_g, gqa_kn_g, mix_norm_g, w_out, norm2_g, w_gate, w_up, w_down, loss_target, m_rel_bias, m_norm1_g, m_w_in, m_sgu_w, m_sgu_b, m_dil_qn_g, m_dil_kn_g, m_conv_w, m_conv_b, m_conv_ln_g, m_conv_ln_b, m_gqa_qn_g, m_gqa_kn_g, m_mix_norm_g, m_w_out, m_norm2_g, m_w_gate, m_w_up, m_w_down, v_rel_bias, v_norm1_g, v_w_in, v_sgu_w, v_sgu_b, v_dil_qn_g, v_dil_kn_g, v_conv_w, v_conv_b, v_conv_ln_g, v_conv_ln_b, v_gqa_qn_g, v_gqa_kn_g, v_mix_norm_g, v_w_out, v_norm2_g, v_w_gate, v_w_up, v_w_down):
    w = dict(rel_bias=rel_bias, norm1_g=norm1_g, w_in=w_in, sgu_w=sgu_w, sgu_b=sgu_b, dil_qn_g=dil_qn_g, dil_kn_g=dil_kn_g,
             conv_w=conv_w, conv_b=conv_b, conv_ln_g=conv_ln_g, conv_ln_b=conv_ln_b, gqa_qn_g=gqa_qn_g, gqa_kn_g=gqa_kn_g,
             mix_norm_g=mix_norm_g, w_out=w_out, norm2_g=norm2_g, w_gate=w_gate, w_up=w_up, w_down=w_down)
    m = dict(rel_bias=m_rel_bias, norm1_g=m_norm1_g, w_in=m_w_in, sgu_w=m_sgu_w, sgu_b=m_sgu_b, dil_qn_g=m_dil_qn_g,
             dil_kn_g=m_dil_kn_g, conv_w=m_conv_w, conv_b=m_conv_b, conv_ln_g=m_conv_ln_g, conv_ln_b=m_conv_ln_b,
             gqa_qn_g=m_gqa_qn_g, gqa_kn_g=m_gqa_kn_g, mix_norm_g=m_mix_norm_g, w_out=m_w_out, norm2_g=m_norm2_g,
             w_gate=m_w_gate, w_up=m_w_up, w_down=m_w_down)
    v = dict(rel_bias=v_rel_bias, norm1_g=v_norm1_g, w_in=v_w_in, sgu_w=v_sgu_w, sgu_b=v_sgu_b, dil_qn_g=v_dil_qn_g,
             dil_kn_g=v_dil_kn_g, conv_w=v_conv_w, conv_b=v_conv_b, conv_ln_g=v_conv_ln_g, conv_ln_b=v_conv_ln_b,
             gqa_qn_g=v_gqa_qn_g, gqa_kn_g=v_gqa_kn_g, mix_norm_g=v_mix_norm_g, w_out=v_w_out, norm2_g=v_norm2_g,
             w_gate=v_w_gate, w_up=v_w_up, w_down=v_w_down)
    bsz = x.shape[0]
    t = bsz * SEQ
    xi, yi, ci = _mesh_pos()
    chip = 2 * xi + yi
    conv_cols = conv_w.shape[-1]

    conv_rows = DEPTH * CONV_WIDTH
    conv_block = jnp.pad(conv_w.reshape(conv_rows, conv_cols), ((0, (-conv_rows) % 8), (0, 0)))
    every, _ = _allgather_sum_small(conv_block, "conv_w_gather")
    every = every.reshape(N_DEV, conv_block.shape[0], conv_cols)
    conv_w_full = jnp.concatenate([every[2 * j, :conv_rows].reshape(DEPTH, CONV_WIDTH, conv_cols) for j in range(N_CHIPS)], axis=-1)

    c1 = jnp.reshape(ci, (1,)).astype(jnp.int32)
    chip1 = jnp.reshape(chip, (1,)).astype(jnp.int32)

    def own_shard(k):
        return lax.dynamic_index_in_dim(w[k], ci, axis=0, keepdims=False).astype(BF16)

    fetches, token = [], every
    for gi, group in enumerate(GATHER_GROUPS):
        shards = [own_shard(k) for k in group]
        handle, token = _chips_start(shards, False, token, f"gather{gi}_start", per_core=True)
        fetches.append((handle, shards))
    all_started = token
    gathered = {}

    def big(l, name, after):
        if (l, name) not in gathered:
            gi = [name in group for group in GATHER_GROUPS].index(True)
            handle, shards = fetches[gi]
            lands = _chips_wait(handle, False, all_started if after is None else after, f"gather{gi}_wait")
            mine = [lax.dynamic_update_slice(land, own[None, None], (ci, chip, 0, 0)) for land, own in zip(lands, shards)]
            for k, g in zip(GATHER_GROUPS[gi], _complete_pairs(mine, f"gather{gi}_share")):
                rows, cols = g.shape[2:]
                for layer in range(DEPTH):
                    gl = g[layer]
                    gathered[layer, k] = (gl.transpose(1, 0, 2).reshape(rows, N_CHIPS * cols) if k in COLUMN_SHARDED
                                          else gl.reshape(N_CHIPS * rows, cols))
        return gathered[l, name]

    big(0, "w_in", None)

    pending, started, reduced = {}, {}, {}

    def emit(l, name, g):
        pending[l, name] = g
        for gi, group in enumerate(REDUCE_GROUPS):
            if name in group and all((l, k) in pending for k in group):
                started[l, gi], token = _reduce_start([pending[l, k] for k in group], c1, g, f"l{l}_reduce{gi}_")
                return token
        return None

    def finish(l, after):
        for gi, group in enumerate(REDUCE_GROUPS):
            for k, r in zip(group, _reduce_finish(started[l, gi], chip1, after, f"l{l}_reduce{gi}_")):
                reduced[l, k] = r
            after = reduced[l, group[0]][1]

    def mid_hook(l, a):
        if l + 1 < DEPTH:
            finish(l + 1, a)

    small = {k: w[k] for k in REPLICATED}
    small["conv_w"] = conv_w_full
    loss, dx, small_grads = _local_step(x.reshape(t, D_MODEL), loss_target.reshape(t, D_MODEL), small, big, emit, mid_hook, bsz)
    loss = lax.psum(loss[0, 0], ("x", "y", "c"))
    finish(0, dx)

    grads, deltas, new_m, new_v = {}, {}, {}, {}
    for k in SHARDED:
        grads[k], deltas[k], new_m[k], new_v[k] = _adamw_shard(w[k], m[k], v[k], [reduced[l, k] for l in range(DEPTH)], c1, "adamw_" + k)

    names = REPLICATED + ("conv_w",)
    shapes = [small_grads[k].shape for k in names]
    _, summed = _allgather_sum_small(_pack([small_grads[k] for k in names]), "small_grads_sum")
    summed_parts = dict(zip(names, _unpack(summed, shapes)))
    rep_shapes = [w[k].shape for k in REPLICATED]
    packed = [_pack([src[k] for k in REPLICATED]) for src in (w, {k: summed_parts[k] for k in REPLICATED}, m, v)]
    d_p, m_p, v_p = _adamw(*packed, "adamw_replicated")
    for k, gk, dk, mk, vk in zip(REPLICATED, _unpack(packed[1], rep_shapes), _unpack(d_p, rep_shapes), _unpack(m_p, rep_shapes),
                                 _unpack(v_p, rep_shapes)):
        grads[k], deltas[k], new_m[k], new_v[k] = gk, dk, mk, vk
    g_conv = lax.dynamic_slice_in_dim(summed_parts["conv_w"], chip * conv_cols, conv_cols, axis=2)
    packed = [_pack([a]) for a in (conv_w, g_conv, m["conv_w"], v["conv_w"])]
    d_p, m_p, v_p = _adamw(*packed, "adamw_conv_w")
    grads["conv_w"] = g_conv
    deltas["conv_w"], new_m["conv_w"], new_v["conv_w"] = (_unpack(a, [conv_w.shape])[0] for a in (d_p, m_p, v_p))

    return (loss, dx.reshape(x.shape), *[grads[k] for k in WEIGHTS], *[deltas[k] for k in WEIGHTS],
            *[new_m[k] for k in WEIGHTS], *[new_v[k] for k in WEIGHTS])
```

```python
import functools
import math

import numpy as np
import jax
import jax.numpy as jnp
from jax import lax
from jax.experimental import pallas as pl
from jax.experimental.pallas import tpu as pltpu

F32 = jnp.float32
BF16 = jnp.bfloat16

D_MODEL = 2048
SEQ = 2048
DEPTH = 2
HEAD_DIM = 64
GROUP_WIDTH = 512
N_HEADS = 8
SGU_CHUNK = 128
DIL_PATTERNS = ((128, 1), (512, 4), (2048, 16))
DIL_HALF = 64
CONV_WIDTH = 31
KV_WIDTH = 128
GRID_W = 64
ROPE_THETA = 10000.0
REL_BUCKETS = 32
REL_MAX_DIST = 1024
FFN_HIDDEN = 5632
IN_WIDTH = 4352
RMS_EPS = 1e-6
LN_EPS = 1e-5
ADAM_LR = 0.001
ADAM_B1 = 0.9
ADAM_B2 = 0.999
ADAM_EPS = 1e-08
ADAM_WD = 0.01
ADAM_STEP = 10
N_CHIPS = 4

V7X_VMEM_LIMIT = 56 * 1024 * 1024
MATMUL_VMEM_BUDGET = 48 * 1024 * 1024
LANES = 128
HI = lax.Precision.HIGHEST
SPLIT3 = lax.Precision.HIGH
MESH = pl.DeviceIdType.MESH


def _cparams(sem=None):
    return pltpu.CompilerParams(dimension_semantics=sem, vmem_limit_bytes=V7X_VMEM_LIMIT)


def _pick(n, cands):
    for c in cands:
        if n % c == 0:
            return c
    raise ValueError(f"no tile for {n}")


_DIMS = {"nn": (((1,), (0,)), ((), ())), "nt": (((1,), (1,)), ((), ())), "tn": (((0,), (0,)), ((), ()))}


def _matmul(pairs, mode, out_dtype, name, residual=None, slabs=1):
    a0, b0 = pairs[0]
    b3 = b0.ndim == 3
    if mode == "nn":
        (M, K), N = a0.shape, b0.shape[1]
    elif mode == "nt":
        (M, K), N = a0.shape, b0.shape[0]
    else:
        (K, M) = a0.shape
        N = b0.shape[-1] if b3 else b0.shape[1] // slabs
    npairs = len(pairs)
    a_bytes, b_bytes, o_bytes = a0.dtype.itemsize, b0.dtype.itemsize, jnp.dtype(out_dtype).itemsize
    per_out = 4 + 2 * o_bytes + (8 if residual is not None else 0)
    tn_cands = [c for c in ((1024, 512) if K <= 2048 else (512,)) + (1408, 2176, 256) if N % c == 0] + [N]
    tm, tn = next((tm, tn) for tn in tn_cands for tm in (1024, 1408, 512, 256)
                  if M % tm == 0 and 2 * npairs * K * (tm * a_bytes + tn * b_bytes) + tm * tn * per_out <= MATMUL_VMEM_BUDGET)
    tk = K
    ni, nj = M // tm, N // tn
    j_outer = nj * M * a_bytes + N * b_bytes < M * a_bytes + ni * N * b_bytes
    grid = (slabs, nj, ni) if j_outer else (slabs, ni, nj)
    at = lambda f: (lambda s, g1, g2: f(s, g2, g1)) if j_outer else f

    if mode in ("nn", "nt"):
        a_spec = pl.BlockSpec((tm, tk), at(lambda s, i, j: (i, 0)))
    else:
        a_spec = pl.BlockSpec((tk, tm), at(lambda s, i, j: (0, i)))
    if mode == "nt":
        b_spec = pl.BlockSpec((tn, tk), at(lambda s, i, j: (j, 0)))
    elif b3:
        b_spec = pl.BlockSpec((None, tk, tn), at(lambda s, i, j: (s, 0, j)))
    else:
        b_spec = pl.BlockSpec((tk, tn), at(lambda s, i, j: (0, s * nj + j)))
    if slabs > 1:
        o_spec = pl.BlockSpec((None, tm, tn), at(lambda s, i, j: (s, i, j)))
        o_shape = jax.ShapeDtypeStruct((slabs, M, N), out_dtype)
    else:
        o_spec = pl.BlockSpec((tm, tn), at(lambda s, i, j: (i, j)))
        o_shape = jax.ShapeDtypeStruct((M, N), out_dtype)
    in_specs = [a_spec] * npairs + [b_spec] * npairs
    args = [a for a, _ in pairs] + [b for _, b in pairs]
    if residual is not None:
        in_specs.append(pl.BlockSpec((tm, tn), at(lambda s, i, j: (i, j))))
        args.append(residual)
    dims = _DIMS[mode]

    def body(*refs):
        a_refs, b_refs = refs[:npairs], refs[npairs:2 * npairs]
        res_ref = refs[2 * npairs] if residual is not None else None
        o_ref = refs[-1]
        r = None
        for a_ref, b_ref in zip(a_refs, b_refs):
            d = lax.dot_general(a_ref[...].astype(BF16), b_ref[...].astype(BF16), dims, preferred_element_type=F32)
            r = d if r is None else r + d
        if res_ref is not None:
            r = r + res_ref[...]
        o_ref[...] = r.astype(out_dtype)

    return pl.pallas_call(
        body, name=name, grid=grid, in_specs=in_specs, out_specs=o_spec, out_shape=o_shape,
        compiler_params=_cparams(("parallel", "parallel", "parallel")),
    )(*args)


class Strided:
    def __init__(self, r):
        self.r = r


def _rowmap(fn, rows, fulls, row_outs, acc_outs, name, tm, n_rows):
    nr, nf, nro = len(rows), len(fulls), len(row_outs)
    rows = [r if len(r) == 4 else (*r, n_rows // tm) for r in rows]
    row_outs = [o if len(o) == 3 else (*o, None) for o in row_outs]
    in_specs = [pl.BlockSpec((tm // per.r, per.r * w), lambda i: (i, 0)) if isinstance(per, Strided) else
                pl.BlockSpec((tm, w), functools.partial(lambda i, cb, per: (i % per, cb), cb=cb, per=per)) for _, w, cb, per in rows]
    in_specs += [pl.BlockSpec(f.shape, lambda i: (0,) * f.ndim) for f in fulls]
    out_specs = [pl.BlockSpec((tm, w) if st is None else (tm // st.r, st.r * w), lambda i: (i, 0)) for w, _, st in row_outs]
    out_specs += [pl.BlockSpec(s, functools.partial(lambda i, n: (0,) * n, n=len(s))) for s in acc_outs]
    out_shape = [jax.ShapeDtypeStruct((n_rows, w) if st is None else (n_rows // st.r, st.r * w), dt) for w, dt, st in row_outs]
    out_shape += [jax.ShapeDtypeStruct(s, F32) for s in acc_outs]
    strided = [(k, w, per.r) for k, (_, w, _, per) in enumerate(rows) if isinstance(per, Strided)]
    strided += [(nr + nf + k, w, st.r) for k, (w, _, st) in enumerate(row_outs) if st is not None]
    n_scratch = len(strided)

    def body(*refs):
        refs, scratch = refs[:len(refs) - n_scratch], dict(zip([k for k, _, _ in strided], refs[len(refs) - n_scratch:]))
        ins = []
        for k, ref in enumerate(refs[:nr + nf]):
            if k in scratch:
                w, r, scr = rows[k][1], rows[k][3].r, scratch[k]
                for rho in range(r):
                    for j in range(w // LANES):
                        scr.at[j][pl.ds(rho, tm // r, stride=r), :] = ref[:, pl.ds(rho * w + j * LANES, LANES)]
                ins.append(jnp.concatenate([scr[j] for j in range(w // LANES)], axis=1))
            else:
                ins.append(ref[...])
        outs = fn(*ins)
        o_refs = refs[nr + nf:]
        for k, (o_ref, val) in enumerate(zip(o_refs[:nro], outs[:nro])):
            if nr + nf + k in scratch:
                w, r, scr = row_outs[k][0], row_outs[k][2].r, scratch[nr + nf + k]
                val = val.astype(F32)
                for j in range(w // LANES):
                    scr[j] = val[:, j * LANES:(j + 1) * LANES]
                for rho in range(r):
                    for j in range(w // LANES):
                        o_ref[:, pl.ds(rho * w + j * LANES, LANES)] = scr.at[j][pl.ds(rho, tm // r, stride=r), :].astype(o_ref.dtype)
            else:
                o_ref[...] = val.astype(o_ref.dtype)
        if acc_outs:
            first = pl.program_id(0) == 0
            for o_ref, val in zip(o_refs[nro:], outs[nro:]):
                @pl.when(first)
                def _(o_ref=o_ref, val=val):
                    o_ref[...] = val

                @pl.when(jnp.logical_not(first))
                def _(o_ref=o_ref, val=val):
                    o_ref[...] += val

    res = pl.pallas_call(
        body, name=name, grid=(n_rows // tm,), in_specs=in_specs, out_specs=out_specs, out_shape=out_shape,
        scratch_shapes=[pltpu.VMEM((w // LANES, tm, LANES), F32) for _, w, _ in strided],
        compiler_params=_cparams(("arbitrary",) if acc_outs else ("parallel",)),
    )(*[r[0] for r in rows], *fulls)
    return res


def _rms(x, g):
    return x * lax.rsqrt(jnp.mean(x * x, axis=-1, keepdims=True) + RMS_EPS) * g


def _rmsnorm_fwd(x, g, name):
    t = x.shape[0]
    return _rowmap(lambda xv, gv: (_rms(xv, gv),), [(x, D_MODEL, 0)], [g], [(D_MODEL, BF16)], [], name, 512, t)[0]


def _rmsnorm_bwd(dh, x, g, dres, name, follow=None):
    t = x.shape[0]

    def fn(dhv, xv, drv, gv, *_):
        _, vjp = jax.vjp(_rms, xv, gv)
        dx, dg = vjp(dhv)
        return dx + drv, dx + drv, dg

    fulls = [g] if follow is None else [g, follow]
    return _rowmap(fn, [(dh, D_MODEL, 0), (x, D_MODEL, 0), (dres, D_MODEL, 0)], fulls, [(D_MODEL, F32), (D_MODEL, BF16)],
                   [(1, D_MODEL)], name, 256, t)


def _loss_fwd_bwd(y, target, name):
    t = y.shape[0]

    def fn(yv, tv):
        e = yv - tv
        return e * (1.0 / D_MODEL), e * (1.0 / D_MODEL), (0.5 / D_MODEL) * jnp.sum(e * e, keepdims=True)

    return _rowmap(fn, [(y, D_MODEL, 0), (target, D_MODEL, 0)], [], [(D_MODEL, F32), (D_MODEL, BF16)], [(1, 1)], name, 512, t)


FFN_TILE = (1024, 512)


def _ffn_up(h, wg, wu, name):
    t, n = h.shape[0], wg.shape[1]
    tm, tn = FFN_TILE

    def body(h_ref, wg_ref, wu_ref, act_ref, du_ref, dg_ref):
        hv = h_ref[...]
        g = jnp.dot(hv, wg_ref[...], preferred_element_type=F32)
        u = jnp.dot(hv, wu_ref[...], preferred_element_type=F32)
        sg = jax.nn.sigmoid(g)
        silu = g * sg
        act_ref[...] = (silu * u).astype(BF16)
        du_ref[...] = silu.astype(BF16)
        dg_ref[...] = (u * (sg + silu * (1.0 - sg))).astype(BF16)

    o_spec = pl.BlockSpec((tm, tn), lambda i, j: (i, j))
    o_shape = jax.ShapeDtypeStruct((t, n), BF16)
    return pl.pallas_call(
        body, name=name, grid=(t // tm, n // tn),
        in_specs=[pl.BlockSpec((tm, D_MODEL), lambda i, j: (i, 0)), pl.BlockSpec((D_MODEL, tn), lambda i, j: (0, j)),
                  pl.BlockSpec((D_MODEL, tn), lambda i, j: (0, j))],
        out_specs=[o_spec] * 3, out_shape=[o_shape] * 3, compiler_params=_cparams(("parallel", "parallel")),
    )(h, wg, wu)


def _ffn_down_bwd(dy, wd, act_du, act_dg, name):
    t, n = dy.shape[0], wd.shape[0]
    tm, tn = FFN_TILE

    def body(dy_ref, wd_ref, adu_ref, adg_ref, dg_ref, du_ref):
        dact = lax.dot_general(dy_ref[...].astype(BF16), wd_ref[...], _DIMS["nt"], preferred_element_type=F32)
        du_ref[...] = (dact * adu_ref[...].astype(F32)).astype(BF16)
        dg_ref[...] = (dact * adg_ref[...].astype(F32)).astype(BF16)

    o_spec = pl.BlockSpec((tm, tn), lambda i, j: (i, j))
    o_shape = jax.ShapeDtypeStruct((t, n), BF16)
    return pl.pallas_call(
        body, name=name, grid=(t // tm, n // tn),
        in_specs=[pl.BlockSpec((tm, D_MODEL), lambda i, j: (i, 0)), pl.BlockSpec((tn, D_MODEL), lambda i, j: (j, 0)),
                  o_spec, o_spec],
        out_specs=[o_spec] * 2, out_shape=[o_shape] * 2, compiler_params=_cparams(("parallel", "parallel")),
    )(dy, wd, act_du, act_dg)


def _np_group_avg(width, group=HEAD_DIM):
    i = np.arange(width)
    return ((i[:, None] // group) == (i[None, :] // group)).astype(np.float32) / group


def _np_tile_fold(width, group=HEAD_DIM):
    return ((np.arange(width)[:, None] % group) == np.arange(group)[None, :]).astype(np.float32)


def _np_group_fold(width, group=HEAD_DIM, pad=128):
    return ((np.arange(width)[:, None] // group) == np.arange(pad)[None, :]).astype(np.float32)


def _np_rope_partner(width):
    i = np.arange(width)
    partner = np.where(i % 32 < 16, i + 16, i - 16)
    return (partner[:, None] == i[None, :]).astype(np.float32)


def _np_kv_expand():
    src = np.arange(KV_WIDTH)
    dst = np.arange(GROUP_WIDTH)
    return ((src[:, None] // HEAD_DIM == dst[None, :] // (4 * HEAD_DIM)) & (src[:, None] % HEAD_DIM == dst[None, :] % HEAD_DIM)).astype(np.float32)


def _np_rope_tables(n_heads):
    t = np.arange(SEQ)
    pos = {0: (t // GRID_W).astype(np.float32), 1: (t % GRID_W).astype(np.float32)}
    freqs = (ROPE_THETA ** (-np.arange(16, dtype=np.float32) / 16)).astype(np.float32)
    cos_parts, sin_parts = [], []
    for axis in (0, 1):
        ang = pos[axis][:, None] * freqs[None, :]
        c, s = np.cos(ang).astype(np.float32), np.sin(ang).astype(np.float32)
        cos_parts += [c, c]
        sin_parts += [-s, s]
    cos = np.concatenate(cos_parts, axis=1)
    sin = np.concatenate(sin_parts, axis=1)
    return np.tile(cos, (1, n_heads)), np.tile(sin, (1, n_heads))


def _np_t5_buckets(rel):
    nb = REL_BUCKETS // 2
    max_exact = nb // 2
    ret = np.where(rel > 0, nb, 0)
    n = np.abs(rel)
    nf = np.maximum(n, 1).astype(np.float32)
    large = max_exact + (np.log(nf / max_exact) / math.log(REL_MAX_DIST / max_exact) * (nb - max_exact)).astype(np.int32)
    large = np.minimum(large, nb - 1)
    return (ret + np.where(n < max_exact, n, large)).astype(np.int32)


DIL_QB = 128
DIL_WIN = DIL_QB + 2 * DIL_HALF


def _np_dil_buckets(dil):
    off = np.arange(DIL_WIN)[None, :] - DIL_HALF - np.arange(DIL_QB)[:, None]
    return _np_t5_buckets(off * dil)


def _dil_live_buckets(dil):
    off = np.arange(-DIL_HALF, DIL_HALF + 1)
    return sorted(set(_np_t5_buckets(off * dil).tolist()))


def _head_stat(x, mavg):
    return jnp.dot(x, mavg, precision=SPLIT3, preferred_element_type=F32)


def _gelu(x):
    return 0.5 * x * (1.0 + jnp.tanh(math.sqrt(2.0 / math.pi) * (x + 0.044715 * (x * x * x))))


def _sgu_pre(u_pre, v_pre, mavg):
    v = _gelu(v_pre)
    xc = v - _head_stat(v, mavg)
    vn = xc * lax.rsqrt(_head_stat(xc * xc, mavg) + LN_EPS)
    return _gelu(u_pre), vn


def _sgu_mix(w_ref, vnb, bm):
    lane_group = lax.broadcasted_iota(jnp.int32, (1, GROUP_WIDTH), 1) // HEAD_DIM
    mixed = bm
    for g in range(N_HEADS):
        r = jnp.dot(w_ref[g], vnb, preferred_element_type=F32)
        mixed = mixed + jnp.where(lane_group == g, r, 0.0)
    return mixed


SGU_TM = 512


def _sgu_fwd(z, w_s, bm, name):
    t = z.shape[0]
    mavg = jnp.asarray(_np_group_avg(GROUP_WIDTH))

    def body(u_ref, v_ref, w_ref, bm_ref, mavg_ref, y_ref):
        for c in range(SGU_TM // SGU_CHUNK):
            rows = pl.ds(c * SGU_CHUNK, SGU_CHUNK)
            u, vn = _sgu_pre(u_ref[rows, :], v_ref[rows, :], mavg_ref[...])
            y_ref[rows, :] = u * _sgu_mix(w_ref, vn.astype(BF16), bm_ref[...])

    full = lambda a: pl.BlockSpec(a.shape, lambda i: (0,) * a.ndim)
    return pl.pallas_call(
        body, name=name, grid=(t // SGU_TM,),
        in_specs=[pl.BlockSpec((SGU_TM, GROUP_WIDTH), lambda i: (i, 0)), pl.BlockSpec((SGU_TM, GROUP_WIDTH), lambda i: (i, 1)),
                  full(w_s), full(bm), full(mavg)],
        out_specs=pl.BlockSpec((SGU_TM, GROUP_WIDTH), lambda i: (i, 0)),
        out_shape=jax.ShapeDtypeStruct((t, GROUP_WIDTH), F32), compiler_params=_cparams(("parallel",)),
    )(z, z, w_s, bm, mavg)


def _sgu_bwd(z, dy, w_s, w_s_t, bm, name):
    t = z.shape[0]
    mavg = jnp.asarray(_np_group_avg(GROUP_WIDTH))
    gfold = jnp.asarray(_np_group_fold(GROUP_WIDTH))

    def body(u_ref, v_ref, dy_ref, w_ref, wt_ref, bm_ref, mavg_ref, gfold_ref, du_ref, dv_ref, dw_ref, dbs_ref, dbm_ref):
        @pl.when(pl.program_id(0) == 0)
        def _():
            dw_ref[...] = jnp.zeros_like(dw_ref)
            dbm_ref[...] = jnp.zeros_like(dbm_ref)

        lane_group = lax.broadcasted_iota(jnp.int32, (1, GROUP_WIDTH), 1) // HEAD_DIM
        for c in range(SGU_TM // SGU_CHUNK):
            rows = pl.ds(c * SGU_CHUNK, SGU_CHUNK)
            (u, vn), pre_vjp = jax.vjp(functools.partial(_sgu_pre, mavg=mavg_ref[...]), u_ref[rows, :], v_ref[rows, :])
            vnb = vn.astype(BF16)
            mixed = _sgu_mix(w_ref, vnb, bm_ref[...])
            dyv = dy_ref[rows, :]
            dmixed = dyv * u
            dbm_ref[...] += dmixed
            dvn = jnp.zeros_like(vn)
            for g in range(N_HEADS):
                dm_g = jnp.where(lane_group == g, dmixed, 0.0).astype(BF16)
                dw_ref[g] += lax.dot_general(dm_g, vnb, _DIMS["nt"], preferred_element_type=F32)
                dvn = dvn + jnp.dot(wt_ref[g], dm_g, preferred_element_type=F32)
            du_pre, dv_pre = pre_vjp((dyv * mixed, dvn))
            du_ref[rows, :] = du_pre
            dv_ref[rows, :] = dv_pre

        @pl.when(pl.program_id(0) == t // SGU_TM - 1)
        def _():
            dbs_ref[...] = jnp.dot(dbm_ref[...], gfold_ref[...], precision=HI, preferred_element_type=F32)

    full = lambda a: pl.BlockSpec(a.shape, lambda i: (0,) * a.ndim)
    row = pl.BlockSpec((SGU_TM, GROUP_WIDTH), lambda i: (i, 0))
    return pl.pallas_call(
        body, name=name, grid=(t // SGU_TM,),
        in_specs=[row, pl.BlockSpec((SGU_TM, GROUP_WIDTH), lambda i: (i, 1)), row, full(w_s), full(w_s_t), full(bm), full(mavg),
                  full(gfold)],
        out_specs=[row, row, pl.BlockSpec((N_HEADS, SGU_CHUNK, SGU_CHUNK), lambda i: (0, 0, 0)),
                   pl.BlockSpec((SGU_CHUNK, 128), lambda i: (0, 0))],
        out_shape=[jax.ShapeDtypeStruct((t, GROUP_WIDTH), F32)] * 2 + [jax.ShapeDtypeStruct((N_HEADS, SGU_CHUNK, SGU_CHUNK), F32),
                                                                      jax.ShapeDtypeStruct((SGU_CHUNK, 128), F32)],
        scratch_shapes=[pltpu.VMEM((SGU_CHUNK, GROUP_WIDTH), F32)],
        compiler_params=_cparams(("arbitrary",)),
    )(z, z, dy, w_s, w_s_t, bm, mavg, gfold)


def _pair_softmax(q2, k2, hh, bias, valid):
    head = (lax.broadcasted_iota(jnp.int32, (1, 2 * HEAD_DIM), 1) // HEAD_DIM) == hh
    qm = jnp.where(head, q2, jnp.zeros_like(q2))
    s = lax.dot_general(qm, k2, _DIMS["nt"], preferred_element_type=F32)
    if bias is not None:
        s = s + bias
    if valid is not None:
        s = jnp.where(valid, s, -1e30)
    m = jnp.max(s, axis=-1, keepdims=True)
    e = jnp.exp(s - m)
    l = jnp.sum(e, axis=-1, keepdims=True)
    return head, qm, e / l, m + jnp.log(l)


def _attn_pair_fwd(q2, k2, v2, biases, valid):
    o2 = lse2 = None
    for hh in range(2):
        head, _, p, lse = _pair_softmax(q2, k2, hh, None if biases is None else biases[hh], valid)
        oh = jnp.dot(p.astype(BF16), v2, preferred_element_type=F32)
        o_h = jnp.where(head, oh, 0.0)
        l_h = jnp.where(head, lse, 0.0)
        o2 = o_h if o2 is None else o2 + o_h
        lse2 = l_h if lse2 is None else lse2 + l_h
    return o2, lse2


def _attn_pair_bwd(q2, k2, v2, biases, valid, do2, dlse2):
    dq2 = dk2 = dv2 = None
    ds_heads = []
    for hh in range(2):
        head, qm, p, _ = _pair_softmax(q2, k2, hh, None if biases is None else biases[hh], valid)
        dom = jnp.where(head, do2, 0.0).astype(BF16)
        dp = lax.dot_general(dom, v2, _DIMS["nt"], preferred_element_type=F32)
        delta = jnp.sum(dp * p, axis=-1, keepdims=True)
        if dlse2 is not None:
            delta = delta - jnp.sum(jnp.where(head, dlse2, 0.0), axis=-1, keepdims=True)
        ds = p * (dp - delta)
        dsb = ds.astype(BF16)
        dq_h = jnp.where(head, jnp.dot(dsb, k2, preferred_element_type=F32), 0.0)
        dk_h = lax.dot_general(dsb, qm, _DIMS["tn"], preferred_element_type=F32)
        dv_h = lax.dot_general(p.astype(BF16), dom, _DIMS["tn"], preferred_element_type=F32)
        dq2 = dq_h if dq2 is None else dq2 + dq_h
        dk2 = dk_h if dk2 is None else dk2 + dk_h
        dv2 = dv_h if dv2 is None else dv2 + dv_h
        ds_heads.append(ds)
    return dq2, dk2, dv2, ds_heads


def _dil_valid(r0, length):
    row = lax.broadcasted_iota(jnp.int32, (DIL_QB, DIL_WIN), 0)
    col = lax.broadcasted_iota(jnp.int32, (DIL_QB, DIL_WIN), 1)
    off = col - DIL_HALF - row
    kpos = r0 - DIL_HALF + col
    return (jnp.abs(off) <= DIL_HALF) & (kpos >= 0) & (kpos < length)


def _dil_build_bias(tab_ref, bkt_ref, bias_ref, dil):
    bkt = bkt_ref[...]
    for h in range(N_HEADS):
        acc = jnp.zeros((DIL_QB, DIL_WIN), F32)
        for b in _dil_live_buckets(dil):
            acc = jnp.where(bkt == b, tab_ref[b, h], acc)
        bias_ref[h] = acc


def _dil_fill_pad(pad_ref, src_ref, length):
    zeros = jnp.zeros((DIL_HALF, GROUP_WIDTH), pad_ref.dtype)
    pad_ref[pl.ds(0, DIL_HALF), :] = zeros
    pad_ref[pl.ds(DIL_HALF + length, DIL_HALF), :] = zeros
    pad_ref[pl.ds(DIL_HALF, length), :] = src_ref[...]


def _dil_specs(bsz, length, dil):
    view = lambda a: a.reshape(bsz, length, dil * GROUP_WIDTH)
    blk = pl.BlockSpec((None, DIL_QB, GROUP_WIDTH), lambda b, rho, i: (b, i, rho))
    seq = pl.BlockSpec((None, length, GROUP_WIDTH), lambda b, rho, i: (b, 0, rho))
    return view, blk, seq


def _dil_fwd(qb, kb, vb, table, dil, name):
    length = SEQ // dil
    bsz = qb.shape[0] // length
    bkt = jnp.asarray(_np_dil_buckets(dil))
    view, blk, seq = _dil_specs(bsz, length, dil)

    def body(tab_ref, bkt_ref, q_ref, k_ref, v_ref, o_ref, lse_ref, kpad, vpad, bias_ref):
        i = pl.program_id(2)

        @pl.when((pl.program_id(0) == 0) & (pl.program_id(1) == 0) & (i == 0))
        def _():
            _dil_build_bias(tab_ref, bkt_ref, bias_ref, dil)

        @pl.when(i == 0)
        def _():
            _dil_fill_pad(kpad, k_ref, length)
            _dil_fill_pad(vpad, v_ref, length)

        r0 = pl.multiple_of(i * DIL_QB, DIL_QB)
        valid = _dil_valid(r0, length)
        for m in range(N_HEADS // 2):
            lanes = pl.ds(m * 128, 128)
            o2, lse2 = _attn_pair_fwd(q_ref[:, lanes], kpad[pl.ds(r0, DIL_WIN), lanes], vpad[pl.ds(r0, DIL_WIN), lanes],
                                      (bias_ref[2 * m], bias_ref[2 * m + 1]), valid)
            o_ref[:, lanes] = o2
            lse_ref[:, lanes] = lse2

    out = jax.ShapeDtypeStruct((bsz, length, dil * GROUP_WIDTH), F32)
    o, lse = pl.pallas_call(
        body, name=name, grid=(bsz, dil, length // DIL_QB),
        in_specs=[pl.BlockSpec(memory_space=pltpu.SMEM), pl.BlockSpec(bkt.shape, lambda b, rho, i: (0, 0)), blk, seq, seq],
        out_specs=[blk, blk], out_shape=[out, out],
        scratch_shapes=[pltpu.VMEM((length + 2 * DIL_HALF, GROUP_WIDTH), BF16), pltpu.VMEM((length + 2 * DIL_HALF, GROUP_WIDTH), BF16),
                        pltpu.VMEM((N_HEADS, DIL_QB, DIL_WIN), F32)],
        compiler_params=_cparams(("arbitrary", "arbitrary", "arbitrary")),
    )(table, bkt, view(qb), view(kb), view(vb))
    return o.reshape(qb.shape), lse.reshape(qb.shape)


def _dil_bwd(qb, kb, vb, do, dlse, table, dil, name):
    length = SEQ // dil
    bsz = qb.shape[0] // length
    nqb = length // DIL_QB
    bkt = jnp.asarray(_np_dil_buckets(dil))
    view, blk, seq = _dil_specs(bsz, length, dil)

    def body(tab_ref, bkt_ref, q_ref, k_ref, v_ref, do_ref, dlse_ref, dq_ref, dk_ref, dv_ref, dsc_ref, kpad, vpad, bias_ref):
        i = pl.program_id(2)

        @pl.when((pl.program_id(0) == 0) & (pl.program_id(1) == 0) & (i == 0))
        def _():
            _dil_build_bias(tab_ref, bkt_ref, bias_ref, dil)
            dsc_ref[...] = jnp.zeros_like(dsc_ref)

        @pl.when(i == 0)
        def _():
            _dil_fill_pad(kpad, k_ref, length)
            _dil_fill_pad(vpad, v_ref, length)
            dk_ref[...] = jnp.zeros_like(dk_ref)
            dv_ref[...] = jnp.zeros_like(dv_ref)

        r0 = pl.multiple_of(i * DIL_QB, DIL_QB)
        valid = _dil_valid(r0, length)
        for m in range(N_HEADS // 2):
            lanes = pl.ds(m * 128, 128)
            dq2, dk2, dv2, ds_heads = _attn_pair_bwd(
                q_ref[:, lanes], kpad[pl.ds(r0, DIL_WIN), lanes], vpad[pl.ds(r0, DIL_WIN), lanes],
                (bias_ref[2 * m], bias_ref[2 * m + 1]), valid, do_ref[:, lanes], dlse_ref[:, lanes])
            dq_ref[:, lanes] = dq2
            dsc_ref[2 * m] += ds_heads[0]
            dsc_ref[2 * m + 1] += ds_heads[1]
            for first, size, live in ((0, DIL_HALF, i >= 1), (DIL_HALF, DIL_QB, None), (DIL_HALF + DIL_QB, DIL_HALF, i <= nqb - 2)):
                def add(first=first, size=size, dk2=dk2, dv2=dv2, lanes=lanes):
                    rows = pl.ds(pl.multiple_of(r0 - DIL_HALF + first, DIL_HALF), size)
                    dk_ref[rows, lanes] += dk2[first:first + size]
                    dv_ref[rows, lanes] += dv2[first:first + size]
                if live is None:
                    add()
                else:
                    pl.when(live)(add)

    out = jax.ShapeDtypeStruct((bsz, length, dil * GROUP_WIDTH), F32)
    dsc_shape = (N_HEADS, DIL_QB, DIL_WIN)
    dq, dk, dv, dsc = pl.pallas_call(
        body, name=name, grid=(bsz, dil, nqb),
        in_specs=[pl.BlockSpec(memory_space=pltpu.SMEM), pl.BlockSpec(bkt.shape, lambda b, rho, i: (0, 0)), blk, seq, seq, blk, blk],
        out_specs=[blk, seq, seq, pl.BlockSpec(dsc_shape, lambda b, rho, i: (0, 0, 0))],
        out_shape=[out, out, out, jax.ShapeDtypeStruct(dsc_shape, F32)],
        scratch_shapes=[pltpu.VMEM((length + 2 * DIL_HALF, GROUP_WIDTH), BF16), pltpu.VMEM((length + 2 * DIL_HALF, GROUP_WIDTH), BF16),
                        pltpu.VMEM(dsc_shape, F32)],
        compiler_params=_cparams(("arbitrary", "arbitrary", "arbitrary")),
    )(table, bkt, view(qb), view(kb), view(vb), view(do), view(dlse))
    return dq.reshape(qb.shape), dk.reshape(qb.shape), dv.reshape(qb.shape), dsc


def _headnorm(x, g, mavg):
    return x * lax.rsqrt(_head_stat(x * x, mavg) + RMS_EPS) * g


def _fold_gain(dg_full, fold):
    return jnp.dot(jnp.broadcast_to(dg_full, (8, dg_full.shape[1])), fold, precision=HI, preferred_element_type=F32)


def _bprep_fn(qp, kp, gq, gk, mavg):
    return _headnorm(qp, gq, mavg) * (HEAD_DIM ** -0.5), _headnorm(kp, gk, mavg)


def _dil_layout(dil):
    return None if dil == 1 else Strided(dil)


def _bprep_fwd(z, gq, gk, name):
    mavg = jnp.asarray(_np_group_avg(GROUP_WIDTH))

    def fn(qp, kp, vp, gqv, gkv, mv):
        qb, kb = _bprep_fn(qp, kp, gqv, gkv, mv)
        return (qb, kb, vp) * len(DIL_PATTERNS)

    w = GROUP_WIDTH
    outs = [(w, BF16, _dil_layout(dil)) for _, dil in DIL_PATTERNS for _ in range(3)]
    res = _rowmap(fn, [(z, w, 2), (z, w, 3), (z, w, 4)], [gq, gk, mavg], outs, [], name, 512, z.shape[0])
    return [res[3 * i:3 * i + 3] for i in range(len(DIL_PATTERNS))]


def _bprep_bwd(z, dqs, dks, dvs, gq, gk, name):
    mavg = jnp.asarray(_np_group_avg(GROUP_WIDTH))
    fold = jnp.asarray(_np_tile_fold(GROUP_WIDTH))

    def fn(qp, kp, dq0, dq1, dq2, dk0, dk1, dk2, dv0, dv1, dv2, gqv, gkv, mv, fv):
        _, vjp = jax.vjp(functools.partial(_bprep_fn, mavg=mv), qp, kp, gqv, gkv)
        dqp, dkp, dgq, dgk = vjp((dq0 + dq1 + dq2, dk0 + dk1 + dk2))
        return dqp, dkp, dv0 + dv1 + dv2, _fold_gain(dgq, fv), _fold_gain(dgk, fv)

    w = GROUP_WIDTH
    rows = [(z, w, 2), (z, w, 3)] + _pattern_rows(dqs) + _pattern_rows(dks) + _pattern_rows(dvs)
    return _rowmap(fn, rows, [gq, gk, mavg, fold], [(w, F32)] * 3, [(8, HEAD_DIM)] * 2, name, 256, z.shape[0])


def _mixture_fn(o0, o1, o2, l0, l1, l2):
    m = lax.stop_gradient(jnp.maximum(jnp.maximum(l0, l1), l2))
    e0, e1, e2 = jnp.exp(l0 - m), jnp.exp(l1 - m), jnp.exp(l2 - m)
    return (e0 * o0 + e1 * o1 + e2 * o2) / (e0 + e1 + e2)


def _pattern_rows(arrs):
    return [(a, GROUP_WIDTH, 0) if dil == 1 else (a, GROUP_WIDTH, 0, Strided(dil)) for a, (_, dil) in zip(arrs, DIL_PATTERNS)]


def _mixture_fwd(os_, ls_, name):
    n = os_[0].shape[0]
    return _rowmap(lambda *v: (_mixture_fn(*v),), _pattern_rows(os_) + _pattern_rows(ls_), [], [(GROUP_WIDTH, F32)], [], name, 512, n)[0]


def _mixture_bwd(os_, ls_, dy, name):
    w = GROUP_WIDTH

    def fn(*v):
        _, vjp = jax.vjp(_mixture_fn, *v[:6])
        return vjp(v[6])

    outs = [(w, F32, _dil_layout(dil)) for _ in range(2) for _, dil in DIL_PATTERNS]
    return _rowmap(fn, _pattern_rows(os_) + _pattern_rows(ls_) + [(dy, w, 0)], [], outs, [], name, 512, dy.shape[0])


def _relbias_fold(dscs, name):
    bkts = [jnp.asarray(_np_dil_buckets(dil)) for _, dil in DIL_PATTERNS]
    npat = len(DIL_PATTERNS)

    def body(*refs):
        bkt_refs, d_refs, o_ref = refs[:npat], refs[npat:-1], refs[-1]
        row = lax.broadcasted_iota(jnp.int32, (REL_BUCKETS, 128), 0)
        lane = lax.broadcasted_iota(jnp.int32, (REL_BUCKETS, 128), 1)
        out = jnp.zeros((REL_BUCKETS, 128), F32)
        for p, (_, dil) in enumerate(DIL_PATTERNS):
            bkt = bkt_refs[p][...]
            for h in range(N_HEADS):
                d = d_refs[2 * p][h] + d_refs[2 * p + 1][h]
                for b in _dil_live_buckets(dil):
                    val = jnp.sum(jnp.where(bkt == b, d, 0.0), keepdims=True)
                    out = out + jnp.where((row == b) & (lane == h), val, 0.0)
        o_ref[...] = out

    return pl.pallas_call(
        body, name=name, out_shape=jax.ShapeDtypeStruct((REL_BUCKETS, 128), F32), compiler_params=_cparams(),
    )(*bkts, *dscs)


DPREP_TM = 512


def _dprep_fn(qp, kp, vp, gq, gk, cq, sq, ck, sk, mavg_q, mavg_k, perm_q, perm_k, expand):
    rot = lambda x, perm: jnp.dot(x, perm, precision=SPLIT3, preferred_element_type=F32)
    qn = _headnorm(qp, gq, mavg_q)
    kn = _headnorm(kp, gk, mavg_k)
    qr = (qn * cq + rot(qn, perm_q) * sq) * (HEAD_DIM ** -0.5)
    kr = kn * ck + rot(kn, perm_k) * sk
    return qr, rot(kr, expand), rot(vp, expand)


def _dprep_consts():
    cq, sq = _np_rope_tables(N_HEADS)
    ck, sk = _np_rope_tables(KV_WIDTH // HEAD_DIM)
    tables = [jnp.asarray(a) for a in (cq, sq, ck, sk)]
    mats = [jnp.asarray(a) for a in (_np_group_avg(GROUP_WIDTH), _np_group_avg(KV_WIDTH), _np_rope_partner(GROUP_WIDTH),
                                      _np_rope_partner(KV_WIDTH), _np_kv_expand())]
    per = SEQ // DPREP_TM
    w, kw = GROUP_WIDTH, KV_WIDTH
    table_rows = [(tables[0], w, 0, per), (tables[1], w, 0, per), (tables[2], kw, 0, per), (tables[3], kw, 0, per)]
    return table_rows, mats


def _dprep_fwd(z, gq, gk, name):
    table_rows, mats = _dprep_consts()
    w, kw = GROUP_WIDTH, KV_WIDTH

    def fn(qp, kp, vp, cq, sq, ck, sk, gqv, gkv, *m):
        return _dprep_fn(qp, kp, vp, gqv, gkv, cq, sq, ck, sk, *m)

    return _rowmap(fn, [(z, w, 7), (z, kw, 32), (z, kw, 33)] + table_rows, [gq, gk] + mats, [(w, BF16)] * 3, [], name,
                   DPREP_TM, z.shape[0])


def _dprep_bwd(z, dq, dkx, dvx, gq, gk, name):
    table_rows, mats = _dprep_consts()
    fold_q = jnp.asarray(_np_tile_fold(GROUP_WIDTH))
    fold_k = jnp.asarray(_np_tile_fold(KV_WIDTH))
    w, kw = GROUP_WIDTH, KV_WIDTH

    def fn(qp, kp, vp, dqv, dkv, dvv, cq, sq, ck, sk, gqv, gkv, fq, fk, *m):
        f = lambda a, b, c, d, e: _dprep_fn(a, b, c, d, e, cq, sq, ck, sk, *m)
        _, vjp = jax.vjp(f, qp, kp, vp, gqv, gkv)
        dqp, dkp, dvp, dgq, dgk = vjp((dqv, dkv, dvv))
        return dqp, dkp, dvp, _fold_gain(dgq, fq), _fold_gain(dgk, fk)

    return _rowmap(fn, [(z, w, 7), (z, kw, 32), (z, kw, 33), (dq, w, 0), (dkx, w, 0), (dvx, w, 0)] + table_rows,
                   [gq, gk, fold_q, fold_k] + mats, [(w, F32), (kw, F32), (kw, F32)], [(8, HEAD_DIM)] * 2, name,
                   DPREP_TM, z.shape[0])


GQA_QB = 256


def _gqa_fwd(q, kx, vx, name):
    bsz = q.shape[0]
    blk = pl.BlockSpec((None, GQA_QB, GROUP_WIDTH), lambda b, i: (b, i, 0))
    seq = pl.BlockSpec((None, SEQ, GROUP_WIDTH), lambda b, i: (b, 0, 0))

    def body(q_ref, k_ref, v_ref, o_ref):
        for m in range(N_HEADS // 2):
            lanes = pl.ds(m * 128, 128)
            o_ref[:, lanes] = _attn_pair_fwd(q_ref[:, lanes], k_ref[:, lanes], v_ref[:, lanes], None, None)[0]

    return pl.pallas_call(
        body, name=name, grid=(bsz, SEQ // GQA_QB), in_specs=[blk, seq, seq], out_specs=blk,
        out_shape=jax.ShapeDtypeStruct((bsz, SEQ, GROUP_WIDTH), F32), compiler_params=_cparams(("parallel", "parallel")),
    )(q, kx, vx)


def _gqa_bwd(q, kx, vx, do, name):
    bsz = q.shape[0]
    blk = pl.BlockSpec((None, GQA_QB, GROUP_WIDTH), lambda b, i: (b, i, 0))
    seq = pl.BlockSpec((None, SEQ, GROUP_WIDTH), lambda b, i: (b, 0, 0))

    def body(q_ref, k_ref, v_ref, do_ref, dq_ref, dk_ref, dv_ref):
        @pl.when(pl.program_id(1) == 0)
        def _():
            dk_ref[...] = jnp.zeros_like(dk_ref)
            dv_ref[...] = jnp.zeros_like(dv_ref)

        for m in range(N_HEADS // 2):
            lanes = pl.ds(m * 128, 128)
            dq2, dk2, dv2, _ = _attn_pair_bwd(q_ref[:, lanes], k_ref[:, lanes], v_ref[:, lanes], None, None, do_ref[:, lanes], None)
            dq_ref[:, lanes] = dq2
            dk_ref[:, lanes] += dk2
            dv_ref[:, lanes] += dv2

    out = jax.ShapeDtypeStruct((bsz, SEQ, GROUP_WIDTH), F32)
    return pl.pallas_call(
        body, name=name, grid=(bsz, SEQ // GQA_QB), in_specs=[blk, seq, seq, blk], out_specs=[blk, seq, seq],
        out_shape=[out, out, out], compiler_params=_cparams(("parallel", "arbitrary")),
    )(q, kx, vx, do)


CONV_TILE = 64
CONV_LEAD = 16
CONV_WINDOW = CONV_TILE + 32


def _glu(a, g):
    return a * jax.nn.sigmoid(g)


def _conv_post(c, b, ln_g, ln_b):
    x = c + b
    xc = x - jnp.mean(x, axis=-1, keepdims=True)
    y = xc * lax.rsqrt(jnp.mean(xc * xc, axis=-1, keepdims=True) + LN_EPS) * ln_g + ln_b
    return y * jax.nn.sigmoid(y)


def _conv_shifted(win, offset):
    return pltpu.roll(win, CONV_WINDOW - offset, 0)[:CONV_TILE]


def _conv_fill(pad_ref, value_of_tile):
    zeros = jnp.zeros((CONV_LEAD, GROUP_WIDTH), F32)
    pad_ref[pl.ds(0, CONV_LEAD), :] = zeros
    pad_ref[pl.ds(CONV_LEAD + SEQ, CONV_LEAD), :] = zeros

    def step(t, carry):
        r0 = pl.multiple_of(t * CONV_TILE, CONV_TILE)
        pad_ref[pl.ds(CONV_LEAD + r0, CONV_TILE), :] = value_of_tile(r0)
        return carry

    lax.fori_loop(0, SEQ // CONV_TILE, step, 0)


def _conv_tile(pad_ref, w_ref, r0, flip):
    win = pad_ref[pl.ds(r0, CONV_WINDOW), :]
    acc = jnp.zeros((CONV_TILE, GROUP_WIDTH), F32)
    for k in range(CONV_WIDTH):
        offset = (CONV_WIDTH - k) if flip else (k + 1)
        acc = acc + w_ref[pl.ds(k, 1), :] * _conv_shifted(win, offset)
    return acc


def _conv_fwd(z3, w, b, ln_g, ln_b, name):
    bsz = z3.shape[0]
    seq = lambda cb: pl.BlockSpec((None, SEQ, GROUP_WIDTH), functools.partial(lambda i, cb: (i, 0, cb), cb=cb))
    full = lambda a: pl.BlockSpec(a.shape, lambda i: (0,) * a.ndim)

    def body(a_ref, g_ref, w_ref, b_ref, lg_ref, lb_ref, y_ref, pad_ref):
        _conv_fill(pad_ref, lambda r0: _glu(a_ref[pl.ds(r0, CONV_TILE), :], g_ref[pl.ds(r0, CONV_TILE), :]))

        def step(t, carry):
            r0 = pl.multiple_of(t * CONV_TILE, CONV_TILE)
            y_ref[pl.ds(r0, CONV_TILE), :] = _conv_post(_conv_tile(pad_ref, w_ref, r0, False), b_ref[...], lg_ref[...], lb_ref[...])
            return carry

        lax.fori_loop(0, SEQ // CONV_TILE, step, 0)

    return pl.pallas_call(
        body, name=name, grid=(bsz,), in_specs=[seq(5), seq(6), full(w), full(b), full(ln_g), full(ln_b)], out_specs=seq(0),
        out_shape=jax.ShapeDtypeStruct((bsz, SEQ, GROUP_WIDTH), F32),
        scratch_shapes=[pltpu.VMEM((SEQ + 2 * CONV_LEAD, GROUP_WIDTH), F32)], compiler_params=_cparams(("parallel",)),
    )(z3, z3, w, b, ln_g, ln_b)


def _conv_bwd(z3, dy, w, b, ln_g, ln_b, name):
    bsz = z3.shape[0]
    seq = lambda cb: pl.BlockSpec((None, SEQ, GROUP_WIDTH), functools.partial(lambda i, cb: (i, 0, cb), cb=cb))
    full = lambda a: pl.BlockSpec(a.shape, lambda i: (0,) * a.ndim)
    vec = pl.BlockSpec((1, GROUP_WIDTH), lambda i: (0, 0))

    def body(a_ref, g_ref, dy_ref, w_ref, b_ref, lg_ref, lb_ref, da_ref, dg_ref, dw_ref, db_ref, dlg_ref, dlb_ref, hpad, dpad, dw8):
        @pl.when(pl.program_id(0) == 0)
        def _():
            dw8[...] = jnp.zeros_like(dw8)
            db_ref[...] = jnp.zeros_like(db_ref)
            dlg_ref[...] = jnp.zeros_like(dlg_ref)
            dlb_ref[...] = jnp.zeros_like(dlb_ref)

        _conv_fill(hpad, lambda r0: _glu(a_ref[pl.ds(r0, CONV_TILE), :], g_ref[pl.ds(r0, CONV_TILE), :]))
        zeros = jnp.zeros((CONV_LEAD, GROUP_WIDTH), F32)
        dpad[pl.ds(0, CONV_LEAD), :] = zeros
        dpad[pl.ds(CONV_LEAD + SEQ, CONV_LEAD), :] = zeros

        def through_post(t, carry):
            r0 = pl.multiple_of(t * CONV_TILE, CONV_TILE)
            conv = _conv_tile(hpad, w_ref, r0, False)
            _, vjp = jax.vjp(_conv_post, conv, b_ref[...], lg_ref[...], lb_ref[...])
            dconv, db, dlg, dlb = vjp(dy_ref[pl.ds(r0, CONV_TILE), :])
            db_ref[...] += db
            dlg_ref[...] += dlg
            dlb_ref[...] += dlb
            dpad[pl.ds(CONV_LEAD + r0, CONV_TILE), :] = dconv
            win = hpad[pl.ds(r0, CONV_WINDOW), :]
            for k in range(CONV_WIDTH):
                prod = dconv * _conv_shifted(win, k + 1)
                part = prod[0:8]
                for j in range(1, CONV_TILE // 8):
                    part = part + prod[8 * j:8 * j + 8]
                dw8[k] += part
            return carry

        lax.fori_loop(0, SEQ // CONV_TILE, through_post, 0)

        def through_glu(t, carry):
            r0 = pl.multiple_of(t * CONV_TILE, CONV_TILE)
            dh = _conv_tile(dpad, w_ref, r0, True)
            rows = pl.ds(r0, CONV_TILE)
            _, vjp = jax.vjp(_glu, a_ref[rows, :], g_ref[rows, :])
            da, dg = vjp(dh)
            da_ref[rows, :] = da
            dg_ref[rows, :] = dg
            return carry

        lax.fori_loop(0, SEQ // CONV_TILE, through_glu, 0)
        dw_ref[...] = jnp.sum(dw8[...], axis=1)

    out = jax.ShapeDtypeStruct((bsz, SEQ, GROUP_WIDTH), F32)
    v = jax.ShapeDtypeStruct((1, GROUP_WIDTH), F32)
    return pl.pallas_call(
        body, name=name, grid=(bsz,), in_specs=[seq(5), seq(6), seq(0), full(w), full(b), full(ln_g), full(ln_b)],
        out_specs=[seq(0), seq(0), pl.BlockSpec((CONV_WIDTH, GROUP_WIDTH), lambda i: (0, 0)), vec, vec, vec],
        out_shape=[out, out, jax.ShapeDtypeStruct((CONV_WIDTH, GROUP_WIDTH), F32), v, v, v],
        scratch_shapes=[pltpu.VMEM((SEQ + 2 * CONV_LEAD, GROUP_WIDTH), F32), pltpu.VMEM((SEQ + 2 * CONV_LEAD, GROUP_WIDTH), F32),
                        pltpu.VMEM((CONV_WIDTH, 8, GROUP_WIDTH), F32)],
        compiler_params=_cparams(("arbitrary",)),
    )(z3, z3, dy, w, b, ln_g, ln_b)


def _mixnorm_fwd(ys, gains, name):
    w = GROUP_WIDTH

    def fn(*v):
        return (jnp.concatenate([_rms(v[i], v[4 + i]) for i in range(4)], axis=-1),)

    return _rowmap(fn, [(y, w, 0) for y in ys], list(gains), [(4 * w, BF16)], [], name, 512, ys[0].shape[0])[0]


def _mixnorm_bwd(dyn, ys, gains, name, follow=None):
    w = GROUP_WIDTH
    gains = list(gains) if follow is None else [*gains, follow]

    def fn(*v):
        dys, dgs = [], []
        for i in range(4):
            _, vjp = jax.vjp(_rms, v[4 + i], v[8 + i])
            dy, dg = vjp(v[i])
            dys.append(dy)
            dgs.append(dg)
        return (*dys, *dgs)

    rows = [(dyn, w, i) for i in range(4)] + [(y, w, 0) for y in ys]
    return _rowmap(fn, rows, list(gains), [(w, F32)] * 4, [(1, w)] * 4, name, 512, dyn.shape[0])


def _adamw_fn(w, g, m, v):
    m = ADAM_B1 * m + (1.0 - ADAM_B1) * g
    v = ADAM_B2 * v + (1.0 - ADAM_B2) * (g * g)
    m_hat = m / (1.0 - ADAM_B1 ** ADAM_STEP)
    v_hat = v / (1.0 - ADAM_B2 ** ADAM_STEP)
    delta = -ADAM_LR * (m_hat / (jnp.sqrt(v_hat) + ADAM_EPS) + ADAM_WD * w)
    return delta, m, v


def _adamw(w, g, m, v, name):
    r, c = w.shape
    tm = _pick(r, (256, 128, 64, 32, 16, 8))
    return _rowmap(_adamw_fn, [(a, c, 0) for a in (w, g, m, v)], [], [(c, F32)] * 3, [], name, tm, r)


def _layer_params(l, small, big):
    tile_row = lambda g, n: jnp.tile(g, n)[None, :]
    row = lambda g: g[None, :]
    w_s = small["sgu_w"][l].astype(BF16)
    return dict(
        norm1_g=row(small["norm1_g"][l]), norm2_g=row(small["norm2_g"][l]),
        w_s=w_s, w_s_t=jnp.swapaxes(w_s, 1, 2), bm=jnp.repeat(small["sgu_b"][l].T, HEAD_DIM, axis=1),
        gq_dil=tile_row(small["dil_qn_g"][l], N_HEADS), gk_dil=tile_row(small["dil_kn_g"][l], N_HEADS),
        conv_w=small["conv_w"][l], conv_b=row(small["conv_b"][l]), conv_ln_g=row(small["conv_ln_g"][l]),
        conv_ln_b=row(small["conv_ln_b"][l]),
        gq_gqa=tile_row(small["gqa_qn_g"][l], N_HEADS), gk_gqa=tile_row(small["gqa_kn_g"][l], KV_WIDTH // HEAD_DIM),
        mix_g=[row(small["mix_norm_g"][l][i * GROUP_WIDTH:(i + 1) * GROUP_WIDTH]) for i in range(4)],
        big=big,
    )


def _layer_fwd(x, p, table, bsz, tag):
    t = x.shape[0]
    seq3 = lambda a: a.reshape(bsz, SEQ, a.shape[-1])
    flat = lambda a: a.reshape(t, a.shape[-1])
    h1 = _rmsnorm_fwd(x, p["norm1_g"], tag + "rms1")
    z = _matmul([(h1, p["big"]("w_in", h1))], "nn", F32, tag + "mm_z")
    y_a = _sgu_fwd(z, p["w_s"], p["bm"], tag + "sgu_fwd")
    dil_qkv = _bprep_fwd(z, p["gq_dil"], p["gk_dil"], tag + "dil_prep")
    outs, lses = [], []
    for (_, dil), (qb, kb, vb) in zip(DIL_PATTERNS, dil_qkv):
        o, lse = _dil_fwd(qb, kb, vb, table, dil, f"{tag}dil{dil}_fwd")
        outs.append(o)
        lses.append(lse)
    y_b = _mixture_fwd(outs, lses, tag + "dil_mix")
    y_c = flat(_conv_fwd(seq3(z), p["conv_w"], p["conv_b"], p["conv_ln_g"], p["conv_ln_b"], tag + "conv_fwd"))
    qd, kx, vx = _dprep_fwd(z, p["gq_gqa"], p["gk_gqa"], tag + "gqa_prep")
    y_d = flat(_gqa_fwd(seq3(qd), seq3(kx), seq3(vx), tag + "gqa_fwd"))
    ys = [y_a, y_b, y_c, y_d]
    yn = _mixnorm_fwd(ys, p["mix_g"], tag + "mixnorm")
    x_mid = _matmul([(yn, p["big"]("w_out", yn))], "nn", F32, tag + "mm_out", residual=x)
    h2 = _rmsnorm_fwd(x_mid, p["norm2_g"], tag + "rms2")
    act, act_du, act_dg = _ffn_up(h2, p["big"]("w_gate", yn), p["big"]("w_up", yn), tag + "ffn_up")
    x_out = _matmul([(act, p["big"]("w_down", yn))], "nn", F32, tag + "mm_down", residual=x_mid)
    saved = dict(x=x, h1=h1, z=z, dil_qkv=dil_qkv, outs=outs, lses=lses, qd=qd, kx=kx, vx=vx, ys=ys, yn=yn, x_mid=x_mid,
                 h2=h2, act=act, act_du=act_du, act_dg=act_dg)
    return x_out, saved


def _layer_bwd(dx_out, dx_out_b, s, p, table, bsz, tag, emit, mid_hook):
    t = dx_out.shape[0]
    seq3 = lambda a: a.reshape(bsz, SEQ, a.shape[-1])
    flat = lambda a: a.reshape(t, a.shape[-1])
    z = s["z"]
    small = {}
    weight = lambda name: p["big"](name, None)
    emit("w_down", _matmul([(s["act"], dx_out_b)], "tn", BF16, tag + "mm_dwdown").reshape(N_CHIPS, FFN_HIDDEN // N_CHIPS, D_MODEL))
    dgate, dup = _ffn_down_bwd(dx_out_b, weight("w_down"), s["act_du"], s["act_dg"], tag + "ffn_dact")
    emit("w_gate", _matmul([(s["h2"], dgate)], "tn", BF16, tag + "mm_dwgate", slabs=N_CHIPS))
    started = emit("w_up", _matmul([(s["h2"], dup)], "tn", BF16, tag + "mm_dwup", slabs=N_CHIPS))
    dh2 = _matmul([(dgate, weight("w_gate")), (dup, weight("w_up"))], "nt", F32, tag + "mm_dh2")
    dx_mid, dx_mid_b, dg2 = _rmsnorm_bwd(dh2, s["x_mid"], p["norm2_g"], dx_out, tag + "rms2_bwd", follow=started)
    small["norm2_g"] = dg2[0]
    mid_hook(dx_mid)
    dyn = _matmul([(dx_mid_b, weight("w_out"))], "nt", F32, tag + "mm_dyn")
    started = emit("w_out", _matmul([(s["yn"], dx_mid_b)], "tn", BF16, tag + "mm_dwout").reshape(N_CHIPS, D_MODEL // N_CHIPS, D_MODEL))
    *dys, dga, dgb, dgc, dgd = _mixnorm_bwd(dyn, s["ys"], p["mix_g"], tag + "mixnorm_bwd", follow=started)
    small["mix_norm_g"] = jnp.concatenate([dga[0], dgb[0], dgc[0], dgd[0]])
    du, dv, dws, dbs = _sgu_bwd(z, dys[0], p["w_s"], p["w_s_t"], p["bm"], tag + "sgu_bwd")
    small["sgu_w"] = dws
    small["sgu_b"] = dbs[:, :N_HEADS].T
    *douts, dl0, dl1, dl2 = _mixture_bwd(s["outs"], s["lses"], dys[1], tag + "dil_mix_bwd")
    dlses = [dl0, dl1, dl2]
    dqs, dks, dvs, dscs = [], [], [], []
    for i, (_, dil) in enumerate(DIL_PATTERNS):
        dq, dk, dvv, dsc = _dil_bwd(*s["dil_qkv"][i], douts[i], dlses[i], table, dil, f"{tag}dil{dil}_bwd")
        dqs.append(dq)
        dks.append(dk)
        dvs.append(dvv)
        dscs.append(dsc)
    dbq, dbk, dbv, dgq, dgk = _bprep_bwd(z, dqs, dks, dvs, p["gq_dil"], p["gk_dil"], tag + "dil_prep_bwd")
    small["dil_qn_g"], small["dil_kn_g"] = dgq[0], dgk[0]
    dca, dcg, dcw, dcb, dclg, dclb = _conv_bwd(seq3(z), seq3(dys[2]), p["conv_w"], p["conv_b"], p["conv_ln_g"], p["conv_ln_b"],
                                               tag + "conv_bwd")
    small["conv_w"], small["conv_b"], small["conv_ln_g"], small["conv_ln_b"] = dcw, dcb[0], dclg[0], dclb[0]
    dqd, dkx, dvx = _gqa_bwd(seq3(s["qd"]), seq3(s["kx"]), seq3(s["vx"]), seq3(dys[3]), tag + "gqa_bwd")
    ddq, ddk, ddv, dgq, dgk = _dprep_bwd(z, flat(dqd), flat(dkx), flat(dvx), p["gq_gqa"], p["gk_gqa"], tag + "gqa_prep_bwd")
    small["gqa_qn_g"], small["gqa_kn_g"] = dgq[0], dgk[0]
    dz = jnp.concatenate([a.astype(BF16) for a in (du, dv, dbq, dbk, dbv, flat(dca), flat(dcg), ddq, ddk, ddv)], axis=1)
    dz4 = dz.reshape(t, N_CHIPS, IN_WIDTH // N_CHIPS).transpose(1, 0, 2)
    started = emit("w_in", _matmul([(s["h1"], dz4)], "tn", BF16, tag + "mm_dwin", slabs=N_CHIPS))
    dh1 = _matmul([(dz, weight("w_in"))], "nt", F32, tag + "mm_dh1")
    dx, dx_b, dg1 = _rmsnorm_bwd(dh1, s["x"], p["norm1_g"], dx_mid, tag + "rms1_bwd", follow=started)
    small["norm1_g"] = dg1[0]
    return dx, dx_b, small, dscs


def _local_step(x, target, small, big, emit, mid_hook, bsz):
    table = small["rel_bias"]
    params = [_layer_params(l, small, functools.partial(big, l)) for l in range(DEPTH)]
    saved = []
    h = x
    for l in range(DEPTH):
        h, sv = _layer_fwd(h, params[l], table, bsz, f"l{l}_")
        saved.append(sv)
    dh, dh_b, loss = _loss_fwd_bwd(h, target, "loss")
    small_grads, dscs = [None] * DEPTH, [None] * DEPTH
    for l in reversed(range(DEPTH)):
        dh, dh_b, small_grads[l], dscs[l] = _layer_bwd(dh, dh_b, saved[l], params[l], table, bsz, f"l{l}_",
                                                       functools.partial(emit, l), functools.partial(mid_hook, l))
    fold_in = [dscs[l][i] for i in range(len(DIL_PATTERNS)) for l in range(DEPTH)]
    stacked = {k: jnp.stack([small_grads[l][k] for l in range(DEPTH)]) for k in small_grads[0]}
    stacked["rel_bias"] = _relbias_fold(fold_in, "relbias_fold")[:, :N_HEADS]
    return loss, dh, stacked


def _mesh_pos():
    return lax.axis_index("x"), lax.axis_index("y"), lax.axis_index("c")


def _other_chips(x, y):
    return [(1 - x, y), (x, 1 - y), (1 - x, 1 - y)]


_ANY = pl.BlockSpec(memory_space=pl.ANY)


def _swap_other_half(arrs, name):
    n = len(arrs)

    def body(*refs):
        in_refs, out_refs, send_sems, recv_sems = refs[:n], refs[n:2 * n], refs[2 * n], refs[2 * n + 1]
        x, y, c = _mesh_pos()
        copies = []
        for k in range(n):
            h = arrs[k].shape[1] // 2
            copies.append(pltpu.make_async_remote_copy(
                src_ref=in_refs[k].at[:, pl.ds((1 - c) * h, h)], dst_ref=out_refs[k], send_sem=send_sems.at[k],
                recv_sem=recv_sems.at[k], device_id=(x, y, 1 - c), device_id_type=MESH))
        for cp in copies:
            cp.start()
        for cp in copies:
            cp.wait()

    return pl.pallas_call(
        body, name=name, in_specs=[_ANY] * n, out_specs=[_ANY] * n,
        out_shape=[jax.ShapeDtypeStruct((a.shape[0], a.shape[1] // 2, a.shape[2]), a.dtype) for a in arrs],
        scratch_shapes=[pltpu.SemaphoreType.DMA((n,)), pltpu.SemaphoreType.DMA((n,))],
    )(*arrs)


def _swap_sibling(arrs, name):
    n = len(arrs)

    def body(*refs):
        in_refs, out_refs, send_sems, recv_sems = refs[:n], refs[n:2 * n], refs[2 * n], refs[2 * n + 1]
        x, y, c = _mesh_pos()
        copies = [pltpu.make_async_remote_copy(src_ref=in_refs[k], dst_ref=out_refs[k], send_sem=send_sems.at[k],
                                               recv_sem=recv_sems.at[k], device_id=(x, y, 1 - c), device_id_type=MESH)
                  for k in range(n)]
        for cp in copies:
            cp.start()
        for cp in copies:
            cp.wait()

    return pl.pallas_call(
        body, name=name, in_specs=[_ANY] * n, out_specs=[_ANY] * n,
        out_shape=[jax.ShapeDtypeStruct(a.shape, a.dtype) for a in arrs],
        scratch_shapes=[pltpu.SemaphoreType.DMA((n,)), pltpu.SemaphoreType.DMA((n,))],
    )(*arrs)


def _complete_pairs(arrs, name):
    n = len(arrs)

    def body(*refs):
        in_refs, out_refs, send_sems, recv_sems = refs[:n], refs[n:2 * n], refs[2 * n], refs[2 * n + 1]
        x, y, c = _mesh_pos()
        copies = [pltpu.make_async_remote_copy(src_ref=in_refs[k].at[c], dst_ref=out_refs[k].at[c], send_sem=send_sems.at[k],
                                               recv_sem=recv_sems.at[k], device_id=(x, y, 1 - c), device_id_type=MESH)
                  for k in range(n)]
        for cp in copies:
            cp.start()
        for k, cp in enumerate(copies):
            cp.wait_send()
            pltpu.make_async_remote_copy(src_ref=in_refs[k].at[1 - c], dst_ref=out_refs[k].at[1 - c], send_sem=send_sems.at[k],
                                         recv_sem=recv_sems.at[k], device_id=(x, y, 1 - c), device_id_type=MESH).wait_recv()

    return pl.pallas_call(
        body, name=name, in_specs=[_ANY] * n, out_specs=[_ANY] * n,
        out_shape=[jax.ShapeDtypeStruct(a.shape, a.dtype) for a in arrs], input_output_aliases={k: k for k in range(n)},
        scratch_shapes=[pltpu.SemaphoreType.DMA((n,)), pltpu.SemaphoreType.DMA((n,))],
    )(*arrs)


_HBM = pl.BlockSpec(memory_space=pltpu.HBM)
_SEM = pl.BlockSpec(memory_space=pltpu.SEMAPHORE)
_DATAFLOW = pltpu.SideEffectType.DATAFLOW_SIDE_EFFECTING


def _chip_copies(src_refs, land_refs, send_sems, recv_sems, scatter):
    x, y, c = _mesh_pos()
    me = 2 * x + y
    out = []
    for k, (src_ref, land_ref) in enumerate(zip(src_refs, land_refs)):
        slot = (lambda chip: land_ref.at[c, chip]) if len(land_ref.shape) == 4 else (lambda chip: land_ref.at[chip])
        for j, (cx, cy) in enumerate(_other_chips(x, y)):
            there = 2 * cx + cy
            src = src_ref.at[there] if scatter else src_ref
            sems = dict(send_sem=send_sems.at[3 * k + j], recv_sem=recv_sems.at[3 * k + j], device_id=(cx, cy, c), device_id_type=MESH)
            out.append((pltpu.make_async_remote_copy(src_ref=src, dst_ref=slot(me), **sems),
                        pltpu.make_async_remote_copy(src_ref=src, dst_ref=slot(there), **sems)))
    return out


def _chips_start(srcs, scatter, after, name, per_core=False):
    n = len(srcs)
    lead = (2, N_CHIPS) if per_core else (N_CHIPS,)
    lands = [lax.empty((*lead, *s.shape[-2:]), s.dtype) for s in srcs]

    def body(*refs):
        src_refs, land_refs = refs[:n], refs[n:2 * n]
        send_sems, recv_sems, token = refs[2 * n + 1], refs[2 * n + 2], refs[-1]
        for sent, _ in _chip_copies(src_refs, land_refs, send_sems, recv_sems, scatter):
            sent.start()
        token[...] = jnp.zeros_like(token)

    hbm = lambda a: pltpu.HBM(a.shape, a.dtype)
    res = pl.pallas_call(
        body, name=name,
        in_specs=[_HBM] * (2 * n) + [_ANY],
        out_specs=[_SEM, _SEM] + [_HBM] * (2 * n) + [pl.BlockSpec(memory_space=pltpu.VMEM)],
        out_shape=[pltpu.SemaphoreType.DMA((3 * n,)), pltpu.SemaphoreType.DMA((3 * n,))] + [hbm(a) for a in srcs] + [hbm(a) for a in lands]
        + [jax.ShapeDtypeStruct((8, 128), F32)],
        input_output_aliases={i: 2 + i for i in range(2 * n)},
        compiler_params=pltpu.CompilerParams(has_side_effects=_DATAFLOW),
    )(*[pltpu.with_memory_space_constraint(a, pltpu.HBM) for a in (*srcs, *lands)], after)
    return (res[0], res[1], res[2:2 + n], res[2 + n:2 + 2 * n]), res[-1]


def _chips_wait(handle, scatter, after, name):
    send_sems, recv_sems, srcs, lands = handle
    n = len(srcs)

    def body(*refs):
        src_refs, land_refs = refs[:n], refs[n:2 * n]
        send_sems, recv_sems = refs[2 * n], refs[2 * n + 1]
        for sent, landed in _chip_copies(src_refs, land_refs, send_sems, recv_sems, scatter):
            sent.wait_send()
            landed.wait_recv()

    hbm = lambda a: pltpu.HBM(a.shape, a.dtype)
    res = pl.pallas_call(
        body, name=name,
        in_specs=[_HBM] * (2 * n) + [_SEM, _SEM, _ANY], out_specs=[_HBM] * (2 * n),
        out_shape=[hbm(a) for a in srcs] + [hbm(a) for a in lands],
        input_output_aliases={i: i for i in range(2 * n)},
        compiler_params=pltpu.CompilerParams(has_side_effects=_DATAFLOW),
    )(*srcs, *lands, send_sems, recv_sems, after)
    return res[n:]


N_DEV = 8


def _everyone_copies(src_ref, land_ref, send_sems, recv_sems):
    x, y, c = _mesh_pos()
    out = []
    for q in range(N_DEV - 1):
        fx, fy, fc = ((q + 1) >> 2) & 1, ((q + 1) >> 1) & 1, (q + 1) & 1
        px, py, pc = (1 - x if fx else x), (1 - y if fy else y), (1 - c if fc else c)
        sems = dict(send_sem=send_sems.at[q], recv_sem=recv_sems.at[q], device_id=(px, py, pc), device_id_type=MESH)
        out.append((pltpu.make_async_remote_copy(src_ref=src_ref, dst_ref=land_ref.at[4 * x + 2 * y + c], **sems),
                    pltpu.make_async_remote_copy(src_ref=src_ref, dst_ref=land_ref.at[4 * px + 2 * py + pc], **sems)))
    return out


def _everyone_start(block, after, name):
    land = lax.empty((N_DEV, *block.shape), block.dtype)

    def body(src_ref, land_ref, after_ref, send_sems, recv_sems, src_thru, land_thru, token):
        for sent, _ in _everyone_copies(src_ref, land_ref, send_sems, recv_sems):
            sent.start()
        token[...] = jnp.zeros_like(token)

    hbm = lambda a: pltpu.HBM(a.shape, a.dtype)
    n_sem = N_DEV - 1
    res = pl.pallas_call(
        body, name=name, in_specs=[_HBM, _HBM, _ANY],
        out_specs=[_SEM, _SEM, _HBM, _HBM, pl.BlockSpec(memory_space=pltpu.VMEM)],
        out_shape=[pltpu.SemaphoreType.DMA((n_sem,)), pltpu.SemaphoreType.DMA((n_sem,)), hbm(block), hbm(land),
                   jax.ShapeDtypeStruct((8, 128), F32)],
        input_output_aliases={0: 2, 1: 3}, compiler_params=pltpu.CompilerParams(has_side_effects=_DATAFLOW),
    )(pltpu.with_memory_space_constraint(block, pltpu.HBM), pltpu.with_memory_space_constraint(land, pltpu.HBM), after)
    return res[:4]


def _everyone_wait(handle, after, name):
    send_sems, recv_sems, block, land = handle

    def body(src_ref, land_ref, send_sems, recv_sems, after_ref, src_thru, land_thru):
        for sent, landed in _everyone_copies(src_ref, land_ref, send_sems, recv_sems):
            sent.wait_send()
            landed.wait_recv()

    hbm = lambda a: pltpu.HBM(a.shape, a.dtype)
    return pl.pallas_call(
        body, name=name, in_specs=[_HBM, _HBM, _SEM, _SEM, _ANY], out_specs=[_HBM, _HBM], out_shape=[hbm(block), hbm(land)],
        input_output_aliases={0: 0, 1: 1}, compiler_params=pltpu.CompilerParams(has_side_effects=_DATAFLOW),
    )(block, land, send_sems, recv_sems, after)


def _sum_devices(blocks, name):
    n_dev, m, lanes = blocks.shape
    tm = _pick(m, (512, 256, 128, 64, 32, 16, 8))

    def body(b_ref, o_ref):
        total = b_ref[0]
        for d in range(1, n_dev):
            total = total + b_ref[d]
        o_ref[...] = total

    return pl.pallas_call(
        body, name=name, grid=(m // tm,), in_specs=[pl.BlockSpec((n_dev, tm, lanes), lambda i: (0, i, 0))],
        out_specs=pl.BlockSpec((tm, lanes), lambda i: (i, 0)), out_shape=jax.ShapeDtypeStruct((m, lanes), F32),
        compiler_params=_cparams(("parallel",)),
    )(blocks)


def _allgather_sum_small(block, name):
    m_per, n = block.shape

    def body(x_ref, out_ref, sum_ref, send_sems, recv_sems, local_sem):
        x, y, c = _mesh_pos()
        me, sibling = (x, y, c), (x, y, 1 - c)
        chips = _other_chips(x, y)

        def rows(px, py, pc):
            return out_ref.at[pl.ds((4 * px + 2 * py + pc) * m_per, m_per), :]

        def copy(k, blk, to, src=None):
            return pltpu.make_async_remote_copy(src_ref=rows(*blk) if src is None else src, dst_ref=rows(*blk),
                                                send_sem=send_sems.at[k], recv_sem=recv_sems.at[k], device_id=to, device_id_type=MESH)

        mine = pltpu.make_async_copy(x_ref, rows(*me), local_sem)
        mine.start()
        first = [copy(0, me, sibling, src=x_ref)]
        first += [copy(1 + j, me, (*chip, c), src=x_ref) for j, chip in enumerate(chips)]
        for cp in first:
            cp.start()
        passed = [copy(4 + j, (*chip, c), sibling) for j, chip in enumerate(chips)]
        for j, chip in enumerate(chips):
            copy(1 + j, (*chip, c), me).wait_recv()
            passed[j].start()
        copy(0, sibling, me).wait_recv()
        for j, chip in enumerate(chips):
            copy(4 + j, (*chip, 1 - c), me).wait_recv()
        for cp in first + passed:
            cp.wait_send()
        mine.wait()
        total = out_ref[pl.ds(0, m_per), :]
        for d in range(1, N_DEV):
            total = total + out_ref[pl.ds(d * m_per, m_per), :]
        sum_ref[...] = total

    vmem = pl.BlockSpec(memory_space=pltpu.VMEM)
    return pl.pallas_call(
        body, name=name, in_specs=[vmem], out_specs=[vmem, vmem],
        out_shape=[jax.ShapeDtypeStruct((N_DEV * m_per, n), F32), jax.ShapeDtypeStruct((m_per, n), F32)],
        scratch_shapes=[pltpu.SemaphoreType.DMA((7,)), pltpu.SemaphoreType.DMA((7,)), pltpu.SemaphoreType.DMA],
        compiler_params=pltpu.CompilerParams(vmem_limit_bytes=V7X_VMEM_LIMIT),
    )(block)


WEIGHTS = ("rel_bias", "norm1_g", "w_in", "sgu_w", "sgu_b", "dil_qn_g", "dil_kn_g", "conv_w", "conv_b", "conv_ln_g", "conv_ln_b",
           "gqa_qn_g", "gqa_kn_g", "mix_norm_g", "w_out", "norm2_g", "w_gate", "w_up", "w_down")
SHARDED = ("w_in", "w_out", "w_gate", "w_up", "w_down")
COLUMN_SHARDED = ("w_in", "w_gate", "w_up")
REPLICATED = tuple(k for k in WEIGHTS if k not in SHARDED and k != "conv_w")


def _pack(parts):
    flat = jnp.concatenate([p.reshape(-1) for p in parts])
    pad = (-flat.shape[0]) % (8 * 128)
    return jnp.pad(flat, (0, pad)).reshape(-1, 128)


def _unpack(buf, shapes):
    flat = buf.reshape(-1)
    out, at = [], 0
    for s in shapes:
        size = math.prod(s)
        out.append(flat[at:at + size].reshape(s))
        at += size
    return out


RS_TM = (256, 128, 64, 32, 16)


def _add_halves(g, sib, c, name):
    slabs, h, cols = sib.shape
    tm = _pick(h, RS_TM)
    nb = h // tm

    def body(c_ref, g_ref, s_ref, o_ref):
        o_ref[...] = (g_ref[...].astype(F32) + s_ref[...].astype(F32)).astype(BF16)

    blk = pl.BlockSpec((None, tm, cols), lambda j, i, c_ref: (j, i, 0))
    grid_spec = pltpu.PrefetchScalarGridSpec(
        num_scalar_prefetch=1, grid=(slabs, nb),
        in_specs=[pl.BlockSpec((None, tm, cols), lambda j, i, c_ref: (j, c_ref[0] * nb + i, 0)), blk], out_specs=blk)
    return pl.pallas_call(body, name=name, grid_spec=grid_spec, out_shape=jax.ShapeDtypeStruct(sib.shape, BF16),
                          compiler_params=_cparams(("parallel", "parallel")))(c, g, sib)


def _add_own_three(own, land, chip, name):
    _, h, cols = own.shape
    tm = _pick(h, RS_TM)

    def body(chip_ref, own_ref, l0_ref, l1_ref, l2_ref, o_ref):
        o_ref[...] = ((own_ref[...].astype(F32) + l0_ref[...].astype(F32)) + l1_ref[...].astype(F32)) + l2_ref[...].astype(F32)

    slot = lambda r: pl.BlockSpec((None, tm, cols), functools.partial(lambda i, chip_ref, r: (jnp.bitwise_xor(chip_ref[0], r), i, 0), r=r))
    grid_spec = pltpu.PrefetchScalarGridSpec(
        num_scalar_prefetch=1, grid=(h // tm,),
        in_specs=[slot(0), slot(1), slot(2), slot(3)],
        out_specs=pl.BlockSpec((tm, cols), lambda i, chip_ref: (i, 0)))
    return pl.pallas_call(body, name=name, grid_spec=grid_spec, out_shape=jax.ShapeDtypeStruct((h, cols), F32),
                          compiler_params=_cparams(("parallel",)))(chip, own, land, land, land)


def _reduce_start(grads, c1, after, tag):
    from_sibling = _swap_other_half(grads, tag + "pair")
    sums = [_add_halves(g, s, c1, f"{tag}add2_{k}") for k, (g, s) in enumerate(zip(grads, from_sibling))]
    handle, token = _chips_start(sums, True, after, tag + "start")
    return (handle, sums), token


def _reduce_finish(started, chip1, after, tag):
    handle, sums = started
    lands = _chips_wait(handle, True, after, tag + "wait")
    totals = [_add_own_three(s, land, chip1, f"{tag}add4_{k}") for k, (s, land) in enumerate(zip(sums, lands))]
    return list(zip(totals, _swap_sibling(totals, tag + "share")))


def _adamw_shard(w, m, v, halves, c1, name):
    depth, rows, cols = w.shape
    h = rows // 2
    tm = _pick(h, RS_TM)
    nb = h // tm
    sources = [a for pair in halves for a in pair]

    def body(c_ref, w_ref, m_ref, v_ref, *refs):
        g_refs, (g_out, d_out, m_out, v_out) = refs[:2 * depth], refs[2 * depth:]
        layer, mine = pl.program_id(0), pl.program_id(1) == c_ref[0]
        g = None
        for l in range(depth):
            g_l = jnp.where(mine, g_refs[2 * l][...], g_refs[2 * l + 1][...])
            g = g_l if g is None else jnp.where(layer == l, g_l, g)
        delta, m_new, v_new = _adamw_fn(w_ref[...], g, m_ref[...], v_ref[...])
        g_out[...], d_out[...], m_out[...], v_out[...] = g, delta, m_new, v_new

    def source_spec(l, own):
        def index(layer, half, i, c_ref):
            return (jnp.where((layer == l) & ((half == c_ref[0]) == own), i, 0), 0)
        return pl.BlockSpec((tm, cols), index)

    blk = pl.BlockSpec((None, tm, cols), lambda layer, half, i, c_ref: (layer, half * nb + i, 0))
    grid_spec = pltpu.PrefetchScalarGridSpec(
        num_scalar_prefetch=1, grid=(depth, 2, nb),
        in_specs=[blk, blk, blk] + [source_spec(l, own) for l in range(depth) for own in (True, False)], out_specs=[blk] * 4)
    return pl.pallas_call(body, name=name, grid_spec=grid_spec, out_shape=[jax.ShapeDtypeStruct(w.shape, F32)] * 4,
                          compiler_params=_cparams(("arbitrary", "arbitrary", "arbitrary")))(c1, w, m, v, *sources)


GATHER_GROUPS = (("w_in",), ("w_out", "w_gate", "w_up", "w_down"))
REDUCE_GROUPS = (("w_down", "w_gate", "w_up"), ("w_out",), ("w_in",))


def kernel(x, rel_bias, norm1_g, w_in, sgu_w, sgu_b, dil_qn_g, dil_kn_g, conv_w, conv_b, conv_ln_g, conv_ln_b, gqa_qn_g, gqa_kn_g, mix_norm_g, w_out, norm2_g, w_gate, w_up, w_down, loss_target, m_rel_bias, m_norm1_g, m_w_in, m_sgu_w, m_sgu_b, m_dil_qn_g, m_dil_kn_g, m_conv_w, m_conv_b, m_conv_ln_g, m_conv_ln_b, m_gqa_qn_g, m_gqa_kn_g, m_mix_norm_g, m_w_out, m_norm2_g, m_w_gate, m_w_up, m_w_down, v_rel_bias, v_norm1_g, v_w_in, v_sgu_w, v_sgu_b, v_dil_qn_g, v_dil_kn_g, v_conv_w, v_conv_b, v_conv_ln_g, v_conv_ln_b, v_gqa_qn_g, v_gqa_kn_g, v_mix_norm_g, v_w_out, v_norm2_g, v_w_gate, v_w_up, v_w_down):
    w = dict(rel_bias=rel_bias, norm1_g=norm1_g, w_in=w_in, sgu_w=sgu_w, sgu_b=sgu_b, dil_qn_g=dil_qn_g, dil_kn_g=dil_kn_g,
             conv_w=conv_w, conv_b=conv_b, conv_ln_g=conv_ln_g, conv_ln_b=conv_ln_b, gqa_qn_g=gqa_qn_g, gqa_kn_g=gqa_kn_g,
             mix_norm_g=mix_norm_g, w_out=w_out, norm2_g=norm2_g, w_gate=w_gate, w_up=w_up, w_down=w_down)
    m = dict(rel_bias=m_rel_bias, norm1_g=m_norm1_g, w_in=m_w_in, sgu_w=m_sgu_w, sgu_b=m_sgu_b, dil_qn_g=m_dil_qn_g,
             dil_kn_g=m_dil_kn_g, conv_w=m_conv_w, conv_b=m_conv_b, conv_ln_g=m_conv_ln_g, conv_ln_b=m_conv_ln_b,
             gqa_qn_g=m_gqa_qn_g, gqa_kn_g=m_gqa_kn_g, mix_norm_g=m_mix_norm_g, w_out=m_w_out, norm2_g=m_norm2_g,
             w_gate=m_w_gate, w_up=m_w_up, w_down=m_w_down)
    v = dict(rel_bias=v_rel_bias, norm1_g=v_norm1_g, w_in=v_w_in, sgu_w=v_sgu_w, sgu_b=v_sgu_b, dil_qn_g=v_dil_qn_g,
             dil_kn_g=v_dil_kn_g, conv_w=v_conv_w, conv_b=v_conv_b, conv_ln_g=v_conv_ln_g, conv_ln_b=v_conv_ln_b,
             gqa_qn_g=v_gqa_qn_g, gqa_kn_g=v_gqa_kn_g, mix_norm_g=v_mix_norm_g, w_out=v_w_out, norm2_g=v_norm2_g,
             w_gate=v_w_gate, w_up=v_w_up, w_down=v_w_down)
    bsz = x.shape[0]
    t = bsz * SEQ
    xi, yi, ci = _mesh_pos()
    chip = 2 * xi + yi
    conv_cols = conv_w.shape[-1]

    conv_rows = DEPTH * CONV_WIDTH
    conv_block = jnp.pad(conv_w.reshape(conv_rows, conv_cols), ((0, (-conv_rows) % 8), (0, 0)))
    every, _ = _allgather_sum_small(conv_block, "conv_w_gather")
    every = every.reshape(N_DEV, conv_block.shape[0], conv_cols)
    conv_w_full = jnp.concatenate([every[2 * j, :conv_rows].reshape(DEPTH, CONV_WIDTH, conv_cols) for j in range(N_CHIPS)], axis=-1)

    c1 = jnp.reshape(ci, (1,)).astype(jnp.int32)
    chip1 = jnp.reshape(chip, (1,)).astype(jnp.int32)

    def own_shard(k):
        return lax.dynamic_index_in_dim(w[k], ci, axis=0, keepdims=False).astype(BF16)

    fetches, token = [], every
    for gi, group in enumerate(GATHER_GROUPS):
        shards = [own_shard(k) for k in group]
        handle, token = _chips_start(shards, False, token, f"gather{gi}_start", per_core=True)
        fetches.append((handle, shards))
    all_started = token
    gathered = {}

    def big(l, name, after):
        if (l, name) not in gathered:
            gi = [name in group for group in GATHER_GROUPS].index(True)
            handle, shards = fetches[gi]
            lands = _chips_wait(handle, False, all_started if after is None else after, f"gather{gi}_wait")
            mine = [lax.dynamic_update_slice(land, own[None, None], (ci, chip, 0, 0)) for land, own in zip(lands, shards)]
            for k, g in zip(GATHER_GROUPS[gi], _complete_pairs(mine, f"gather{gi}_share")):
                rows, cols = g.shape[2:]
                for layer in range(DEPTH):
                    gl = g[layer]
                    gathered[layer, k] = (gl.transpose(1, 0, 2).reshape(rows, N_CHIPS * cols) if k in COLUMN_SHARDED
                                          else gl.reshape(N_CHIPS * rows, cols))
        return gathered[l, name]

    big(0, "w_in", None)

    pending, started, reduced = {}, {}, {}

    def emit(l, name, g):
        pending[l, name] = g
        for gi, group in enumerate(REDUCE_GROUPS):
            if name in group and all((l, k) in pending for k in group):
                started[l, gi], token = _reduce_start([pending[l, k] for k in group], c1, g, f"l{l}_reduce{gi}_")
                return token
        return None

    def finish(l, after):
        for gi, group in enumerate(REDUCE_GROUPS):
            for k, r in zip(group, _reduce_finish(started[l, gi], chip1, after, f"l{l}_reduce{gi}_")):
                reduced[l, k] = r
            after = reduced[l, group[0]][1]

    def mid_hook(l, a):
        if l + 1 < DEPTH:
            finish(l + 1, a)

    small = {k: w[k] for k in REPLICATED}
    small["conv_w"] = conv_w_full
    loss, dx, small_grads = _local_step(x.reshape(t, D_MODEL), loss_target.reshape(t, D_MODEL), small, big, emit, mid_hook, bsz)
    loss = lax.psum(loss[0, 0], ("x", "y", "c"))

    names = REPLICATED + ("conv_w",)
    shapes = [small_grads[k].shape for k in names]
    small_exchange = _everyone_start(_pack([small_grads[k] for k in names]), dx, "small_grads_start")
    finish(0, dx)

    grads, deltas, new_m, new_v = {}, {}, {}, {}
    for k in SHARDED:
        grads[k], deltas[k], new_m[k], new_v[k] = _adamw_shard(w[k], m[k], v[k], [reduced[l, k] for l in range(DEPTH)], c1, "adamw_" + k)

    own, others = _everyone_wait(small_exchange, new_v[SHARDED[-1]], "small_grads_wait")
    summed = _sum_devices(lax.dynamic_update_slice_in_dim(others, own[None], 4 * xi + 2 * yi + ci, axis=0), "small_grads_sum")
    summed_parts = dict(zip(names, _unpack(summed, shapes)))
    rep_shapes = [w[k].shape for k in REPLICATED]
    packed = [_pack([src[k] for k in REPLICATED]) for src in (w, {k: summed_parts[k] for k in REPLICATED}, m, v)]
    d_p, m_p, v_p = _adamw(*packed, "adamw_replicated")
    for k, gk, dk, mk, vk in zip(REPLICATED, _unpack(packed[1], rep_shapes), _unpack(d_p, rep_shapes), _unpack(m_p, rep_shapes),
                                 _unpack(v_p, rep_shapes)):
        grads[k], deltas[k], new_m[k], new_v[k] = gk, dk, mk, vk
    g_conv = lax.dynamic_slice_in_dim(summed_parts["conv_w"], chip * conv_cols, conv_cols, axis=2)
    packed = [_pack([a]) for a in (conv_w, g_conv, m["conv_w"], v["conv_w"])]
    d_p, m_p, v_p = _adamw(*packed, "adamw_conv_w")
    grads["conv_w"] = g_conv
    deltas["conv_w"], new_m["conv_w"], new_v["conv_w"] = (_unpack(a, [conv_w.shape])[0] for a in (d_p, m_p, v_p))

    return (loss, dx.reshape(x.shape), *[grads[k] for k in WEIGHTS], *[deltas[k] for k in WEIGHTS],
            *[new_m[k] for k in WEIGHTS], *[new_v[k] for k in WEIGHTS])
```

```python
import functools
import math

import numpy as np
import jax
import jax.numpy as jnp
from jax import lax
from jax.experimental import pallas as pl
from jax.experimental.pallas import tpu as pltpu

F32 = jnp.float32
BF16 = jnp.bfloat16

D_MODEL = 2048
SEQ = 2048
DEPTH = 2
HEAD_DIM = 64
GROUP_WIDTH = 512
N_HEADS = 8
SGU_CHUNK = 128
DIL_PATTERNS = ((128, 1), (512, 4), (2048, 16))
DIL_HALF = 64
CONV_WIDTH = 31
KV_WIDTH = 128
GRID_W = 64
ROPE_THETA = 10000.0
REL_BUCKETS = 32
REL_MAX_DIST = 1024
FFN_HIDDEN = 5632
IN_WIDTH = 4352
RMS_EPS = 1e-6
LN_EPS = 1e-5
ADAM_LR = 0.001
ADAM_B1 = 0.9
ADAM_B2 = 0.999
ADAM_EPS = 1e-08
ADAM_WD = 0.01
ADAM_STEP = 10
N_CHIPS = 4

V7X_VMEM_LIMIT = 56 * 1024 * 1024
MATMUL_VMEM_BUDGET = 48 * 1024 * 1024
LANES = 128
HI = lax.Precision.HIGHEST
SPLIT3 = lax.Precision.HIGH
MESH = pl.DeviceIdType.MESH


def _cparams(sem=None):
    return pltpu.CompilerParams(dimension_semantics=sem, vmem_limit_bytes=V7X_VMEM_LIMIT)


def _pick(n, cands):
    for c in cands:
        if n % c == 0:
            return c
    raise ValueError(f"no tile for {n}")


_DIMS = {"nn": (((1,), (0,)), ((), ())), "nt": (((1,), (1,)), ((), ())), "tn": (((0,), (0,)), ((), ()))}


def _matmul(pairs, mode, out_dtype, name, residual=None, slabs=1):
    a0, b0 = pairs[0]
    b3 = b0.ndim == 3
    if mode == "nn":
        (M, K), N = a0.shape, b0.shape[1]
    elif mode == "nt":
        (M, K), N = a0.shape, b0.shape[0]
    else:
        (K, M) = a0.shape
        N = b0.shape[-1] if b3 else b0.shape[1] // slabs
    npairs = len(pairs)
    a_bytes, b_bytes, o_bytes = a0.dtype.itemsize, b0.dtype.itemsize, jnp.dtype(out_dtype).itemsize
    per_out = 4 + 2 * o_bytes + (8 if residual is not None else 0)
    tn_cands = [c for c in ((1024, 512) if K <= 2048 else (512,)) + (1408, 2176, 256) if N % c == 0] + [N]
    tm, tn = next((tm, tn) for tn in tn_cands for tm in (1024, 1408, 512, 256)
                  if M % tm == 0 and 2 * npairs * K * (tm * a_bytes + tn * b_bytes) + tm * tn * per_out <= MATMUL_VMEM_BUDGET)
    tk = K
    ni, nj = M // tm, N // tn
    j_outer = nj * M * a_bytes + N * b_bytes < M * a_bytes + ni * N * b_bytes
    grid = (slabs, nj, ni) if j_outer else (slabs, ni, nj)
    at = lambda f: (lambda s, g1, g2: f(s, g2, g1)) if j_outer else f

    if mode in ("nn", "nt"):
        a_spec = pl.BlockSpec((tm, tk), at(lambda s, i, j: (i, 0)))
    else:
        a_spec = pl.BlockSpec((tk, tm), at(lambda s, i, j: (0, i)))
    if mode == "nt":
        b_spec = pl.BlockSpec((tn, tk), at(lambda s, i, j: (j, 0)))
    elif b3:
        b_spec = pl.BlockSpec((None, tk, tn), at(lambda s, i, j: (s, 0, j)))
    else:
        b_spec = pl.BlockSpec((tk, tn), at(lambda s, i, j: (0, s * nj + j)))
    if slabs > 1:
        o_spec = pl.BlockSpec((None, tm, tn), at(lambda s, i, j: (s, i, j)))
        o_shape = jax.ShapeDtypeStruct((slabs, M, N), out_dtype)
    else:
        o_spec = pl.BlockSpec((tm, tn), at(lambda s, i, j: (i, j)))
        o_shape = jax.ShapeDtypeStruct((M, N), out_dtype)
    in_specs = [a_spec] * npairs + [b_spec] * npairs
    args = [a for a, _ in pairs] + [b for _, b in pairs]
    if residual is not None:
        in_specs.append(pl.BlockSpec((tm, tn), at(lambda s, i, j: (i, j))))
        args.append(residual)
    dims = _DIMS[mode]

    def body(*refs):
        a_refs, b_refs = refs[:npairs], refs[npairs:2 * npairs]
        res_ref = refs[2 * npairs] if residual is not None else None
        o_ref = refs[-1]
        r = None
        for a_ref, b_ref in zip(a_refs, b_refs):
            d = lax.dot_general(a_ref[...].astype(BF16), b_ref[...].astype(BF16), dims, preferred_element_type=F32)
            r = d if r is None else r + d
        if res_ref is not None:
            r = r + res_ref[...]
        o_ref[...] = r.astype(out_dtype)

    return pl.pallas_call(
        body, name=name, grid=grid, in_specs=in_specs, out_specs=o_spec, out_shape=o_shape,
        compiler_params=_cparams(("parallel", "parallel", "parallel")),
    )(*args)


class Strided:
    def __init__(self, r):
        self.r = r


def _rowmap(fn, rows, fulls, row_outs, acc_outs, name, tm, n_rows):
    nr, nf, nro = len(rows), len(fulls), len(row_outs)
    rows = [r if len(r) == 4 else (*r, n_rows // tm) for r in rows]
    row_outs = [o if len(o) == 3 else (*o, None) for o in row_outs]
    in_specs = [pl.BlockSpec((tm // per.r, per.r * w), lambda i: (i, 0)) if isinstance(per, Strided) else
                pl.BlockSpec((tm, w), functools.partial(lambda i, cb, per: (i % per, cb), cb=cb, per=per)) for _, w, cb, per in rows]
    in_specs += [pl.BlockSpec(f.shape, lambda i: (0,) * f.ndim) for f in fulls]
    out_specs = [pl.BlockSpec((tm, w) if st is None else (tm // st.r, st.r * w), lambda i: (i, 0)) for w, _, st in row_outs]
    out_specs += [pl.BlockSpec(s, functools.partial(lambda i, n: (0,) * n, n=len(s))) for s in acc_outs]
    out_shape = [jax.ShapeDtypeStruct((n_rows, w) if st is None else (n_rows // st.r, st.r * w), dt) for w, dt, st in row_outs]
    out_shape += [jax.ShapeDtypeStruct(s, F32) for s in acc_outs]
    strided = [(k, w, per.r) for k, (_, w, _, per) in enumerate(rows) if isinstance(per, Strided)]
    strided += [(nr + nf + k, w, st.r) for k, (w, _, st) in enumerate(row_outs) if st is not None]
    n_scratch = len(strided)

    def body(*refs):
        refs, scratch = refs[:len(refs) - n_scratch], dict(zip([k for k, _, _ in strided], refs[len(refs) - n_scratch:]))
        ins = []
        for k, ref in enumerate(refs[:nr + nf]):
            if k in scratch:
                w, r, scr = rows[k][1], rows[k][3].r, scratch[k]
                for rho in range(r):
                    for j in range(w // LANES):
                        scr.at[j][pl.ds(rho, tm // r, stride=r), :] = ref[:, pl.ds(rho * w + j * LANES, LANES)]
                ins.append(jnp.concatenate([scr[j] for j in range(w // LANES)], axis=1))
            else:
                ins.append(ref[...])
        outs = fn(*ins)
        o_refs = refs[nr + nf:]
        for k, (o_ref, val) in enumerate(zip(o_refs[:nro], outs[:nro])):
            if nr + nf + k in scratch:
                w, r, scr = row_outs[k][0], row_outs[k][2].r, scratch[nr + nf + k]
                val = val.astype(F32)
                for j in range(w // LANES):
                    scr[j] = val[:, j * LANES:(j + 1) * LANES]
                for rho in range(r):
                    for j in range(w // LANES):
                        o_ref[:, pl.ds(rho * w + j * LANES, LANES)] = scr.at[j][pl.ds(rho, tm // r, stride=r), :].astype(o_ref.dtype)
            else:
                o_ref[...] = val.astype(o_ref.dtype)
        if acc_outs:
            first = pl.program_id(0) == 0
            for o_ref, val in zip(o_refs[nro:], outs[nro:]):
                @pl.when(first)
                def _(o_ref=o_ref, val=val):
                    o_ref[...] = val

                @pl.when(jnp.logical_not(first))
                def _(o_ref=o_ref, val=val):
                    o_ref[...] += val

    res = pl.pallas_call(
        body, name=name, grid=(n_rows // tm,), in_specs=in_specs, out_specs=out_specs, out_shape=out_shape,
        scratch_shapes=[pltpu.VMEM((w // LANES, tm, LANES), F32) for _, w, _ in strided],
        compiler_params=_cparams(("arbitrary",) if acc_outs else ("parallel",)),
    )(*[r[0] for r in rows], *fulls)
    return res


def _rms(x, g):
    return x * lax.rsqrt(jnp.mean(x * x, axis=-1, keepdims=True) + RMS_EPS) * g


def _rmsnorm_fwd(x, g, name):
    t = x.shape[0]
    return _rowmap(lambda xv, gv: (_rms(xv, gv),), [(x, D_MODEL, 0)], [g], [(D_MODEL, BF16)], [], name, 512, t)[0]


def _rmsnorm_bwd(dh, x, g, dres, name, follow=None):
    t = x.shape[0]

    def fn(dhv, xv, drv, gv, *_):
        _, vjp = jax.vjp(_rms, xv, gv)
        dx, dg = vjp(dhv)
        return dx + drv, dx + drv, dg

    fulls = [g] if follow is None else [g, follow]
    return _rowmap(fn, [(dh, D_MODEL, 0), (x, D_MODEL, 0), (dres, D_MODEL, 0)], fulls, [(D_MODEL, F32), (D_MODEL, BF16)],
                   [(1, D_MODEL)], name, 256, t)


def _loss_fwd_bwd(y, target, name):
    t = y.shape[0]

    def fn(yv, tv):
        e = yv - tv
        return e * (1.0 / D_MODEL), e * (1.0 / D_MODEL), (0.5 / D_MODEL) * jnp.sum(e * e, keepdims=True)

    return _rowmap(fn, [(y, D_MODEL, 0), (target, D_MODEL, 0)], [], [(D_MODEL, F32), (D_MODEL, BF16)], [(1, 1)], name, 512, t)


FFN_TILE = (1024, 512)


def _ffn_up(h, wg, wu, name):
    t, n = h.shape[0], wg.shape[1]
    tm, tn = FFN_TILE

    def body(h_ref, wg_ref, wu_ref, act_ref, du_ref, dg_ref):
        hv = h_ref[...]
        g = jnp.dot(hv, wg_ref[...], preferred_element_type=F32)
        u = jnp.dot(hv, wu_ref[...], preferred_element_type=F32)
        sg = jax.nn.sigmoid(g)
        silu = g * sg
        act_ref[...] = (silu * u).astype(BF16)
        du_ref[...] = silu.astype(BF16)
        dg_ref[...] = (u * (sg + silu * (1.0 - sg))).astype(BF16)

    o_spec = pl.BlockSpec((tm, tn), lambda i, j: (i, j))
    o_shape = jax.ShapeDtypeStruct((t, n), BF16)
    return pl.pallas_call(
        body, name=name, grid=(t // tm, n // tn),
        in_specs=[pl.BlockSpec((tm, D_MODEL), lambda i, j: (i, 0)), pl.BlockSpec((D_MODEL, tn), lambda i, j: (0, j)),
                  pl.BlockSpec((D_MODEL, tn), lambda i, j: (0, j))],
        out_specs=[o_spec] * 3, out_shape=[o_shape] * 3, compiler_params=_cparams(("parallel", "parallel")),
    )(h, wg, wu)


def _ffn_down_bwd(dy, wd, act_du, act_dg, name):
    t, n = dy.shape[0], wd.shape[0]
    tm, tn = FFN_TILE

    def body(dy_ref, wd_ref, adu_ref, adg_ref, dg_ref, du_ref):
        dact = lax.dot_general(dy_ref[...].astype(BF16), wd_ref[...], _DIMS["nt"], preferred_element_type=F32)
        du_ref[...] = (dact * adu_ref[...].astype(F32)).astype(BF16)
        dg_ref[...] = (dact * adg_ref[...].astype(F32)).astype(BF16)

    o_spec = pl.BlockSpec((tm, tn), lambda i, j: (i, j))
    o_shape = jax.ShapeDtypeStruct((t, n), BF16)
    return pl.pallas_call(
        body, name=name, grid=(t // tm, n // tn),
        in_specs=[pl.BlockSpec((tm, D_MODEL), lambda i, j: (i, 0)), pl.BlockSpec((tn, D_MODEL), lambda i, j: (j, 0)),
                  o_spec, o_spec],
        out_specs=[o_spec] * 2, out_shape=[o_shape] * 2, compiler_params=_cparams(("parallel", "parallel")),
    )(dy, wd, act_du, act_dg)


def _np_group_avg(width, group=HEAD_DIM):
    i = np.arange(width)
    return ((i[:, None] // group) == (i[None, :] // group)).astype(np.float32) / group


def _np_tile_fold(width, group=HEAD_DIM):
    return ((np.arange(width)[:, None] % group) == np.arange(group)[None, :]).astype(np.float32)


def _np_group_fold(width, group=HEAD_DIM, pad=128):
    return ((np.arange(width)[:, None] // group) == np.arange(pad)[None, :]).astype(np.float32)


def _np_rope_partner(width):
    i = np.arange(width)
    partner = np.where(i % 32 < 16, i + 16, i - 16)
    return (partner[:, None] == i[None, :]).astype(np.float32)


def _np_kv_expand():
    src = np.arange(KV_WIDTH)
    dst = np.arange(GROUP_WIDTH)
    return ((src[:, None] // HEAD_DIM == dst[None, :] // (4 * HEAD_DIM)) & (src[:, None] % HEAD_DIM == dst[None, :] % HEAD_DIM)).astype(np.float32)


def _np_rope_tables(n_heads):
    t = np.arange(SEQ)
    pos = {0: (t // GRID_W).astype(np.float32), 1: (t % GRID_W).astype(np.float32)}
    freqs = (ROPE_THETA ** (-np.arange(16, dtype=np.float32) / 16)).astype(np.float32)
    cos_parts, sin_parts = [], []
    for axis in (0, 1):
        ang = pos[axis][:, None] * freqs[None, :]
        c, s = np.cos(ang).astype(np.float32), np.sin(ang).astype(np.float32)
        cos_parts += [c, c]
        sin_parts += [-s, s]
    cos = np.concatenate(cos_parts, axis=1)
    sin = np.concatenate(sin_parts, axis=1)
    return np.tile(cos, (1, n_heads)), np.tile(sin, (1, n_heads))


def _np_t5_buckets(rel):
    nb = REL_BUCKETS // 2
    max_exact = nb // 2
    ret = np.where(rel > 0, nb, 0)
    n = np.abs(rel)
    nf = np.maximum(n, 1).astype(np.float32)
    large = max_exact + (np.log(nf / max_exact) / math.log(REL_MAX_DIST / max_exact) * (nb - max_exact)).astype(np.int32)
    large = np.minimum(large, nb - 1)
    return (ret + np.where(n < max_exact, n, large)).astype(np.int32)


DIL_QB = 128
DIL_WIN = DIL_QB + 2 * DIL_HALF


def _np_dil_buckets(dil):
    off = np.arange(DIL_WIN)[None, :] - DIL_HALF - np.arange(DIL_QB)[:, None]
    return _np_t5_buckets(off * dil)


def _dil_live_buckets(dil):
    off = np.arange(-DIL_HALF, DIL_HALF + 1)
    return sorted(set(_np_t5_buckets(off * dil).tolist()))


def _head_stat(x, mavg):
    return jnp.dot(x, mavg, precision=SPLIT3, preferred_element_type=F32)


def _gelu(x):
    return 0.5 * x * (1.0 + jnp.tanh(math.sqrt(2.0 / math.pi) * (x + 0.044715 * (x * x * x))))


def _sgu_pre(u_pre, v_pre, mavg):
    v = _gelu(v_pre)
    xc = v - _head_stat(v, mavg)
    vn = xc * lax.rsqrt(_head_stat(xc * xc, mavg) + LN_EPS)
    return _gelu(u_pre), vn


def _sgu_mix(w_ref, vnb, bm):
    lane_group = lax.broadcasted_iota(jnp.int32, (1, GROUP_WIDTH), 1) // HEAD_DIM
    mixed = bm
    for g in range(N_HEADS):
        r = jnp.dot(w_ref[g], vnb, preferred_element_type=F32)
        mixed = mixed + jnp.where(lane_group == g, r, 0.0)
    return mixed


SGU_TM = 512


def _sgu_fwd(z, w_s, bm, name):
    t = z.shape[0]
    mavg = jnp.asarray(_np_group_avg(GROUP_WIDTH))

    def body(u_ref, v_ref, w_ref, bm_ref, mavg_ref, y_ref):
        for c in range(SGU_TM // SGU_CHUNK):
            rows = pl.ds(c * SGU_CHUNK, SGU_CHUNK)
            u, vn = _sgu_pre(u_ref[rows, :], v_ref[rows, :], mavg_ref[...])
            y_ref[rows, :] = u * _sgu_mix(w_ref, vn.astype(BF16), bm_ref[...])

    full = lambda a: pl.BlockSpec(a.shape, lambda i: (0,) * a.ndim)
    return pl.pallas_call(
        body, name=name, grid=(t // SGU_TM,),
        in_specs=[pl.BlockSpec((SGU_TM, GROUP_WIDTH), lambda i: (i, 0)), pl.BlockSpec((SGU_TM, GROUP_WIDTH), lambda i: (i, 1)),
                  full(w_s), full(bm), full(mavg)],
        out_specs=pl.BlockSpec((SGU_TM, GROUP_WIDTH), lambda i: (i, 0)),
        out_shape=jax.ShapeDtypeStruct((t, GROUP_WIDTH), F32), compiler_params=_cparams(("parallel",)),
    )(z, z, w_s, bm, mavg)


def _sgu_bwd(z, dy, w_s, w_s_t, bm, name):
    t = z.shape[0]
    mavg = jnp.asarray(_np_group_avg(GROUP_WIDTH))
    gfold = jnp.asarray(_np_group_fold(GROUP_WIDTH))

    def body(u_ref, v_ref, dy_ref, w_ref, wt_ref, bm_ref, mavg_ref, gfold_ref, du_ref, dv_ref, dw_ref, dbs_ref, dbm_ref):
        @pl.when(pl.program_id(0) == 0)
        def _():
            dw_ref[...] = jnp.zeros_like(dw_ref)
            dbm_ref[...] = jnp.zeros_like(dbm_ref)

        lane_group = lax.broadcasted_iota(jnp.int32, (1, GROUP_WIDTH), 1) // HEAD_DIM
        for c in range(SGU_TM // SGU_CHUNK):
            rows = pl.ds(c * SGU_CHUNK, SGU_CHUNK)
            (u, vn), pre_vjp = jax.vjp(functools.partial(_sgu_pre, mavg=mavg_ref[...]), u_ref[rows, :], v_ref[rows, :])
            vnb = vn.astype(BF16)
            mixed = _sgu_mix(w_ref, vnb, bm_ref[...])
            dyv = dy_ref[rows, :]
            dmixed = dyv * u
            dbm_ref[...] += dmixed
            dvn = jnp.zeros_like(vn)
            for g in range(N_HEADS):
                dm_g = jnp.where(lane_group == g, dmixed, 0.0).astype(BF16)
                dw_ref[g] += lax.dot_general(dm_g, vnb, _DIMS["nt"], preferred_element_type=F32)
                dvn = dvn + jnp.dot(wt_ref[g], dm_g, preferred_element_type=F32)
            du_pre, dv_pre = pre_vjp((dyv * mixed, dvn))
            du_ref[rows, :] = du_pre
            dv_ref[rows, :] = dv_pre

        @pl.when(pl.program_id(0) == t // SGU_TM - 1)
        def _():
            dbs_ref[...] = jnp.dot(dbm_ref[...], gfold_ref[...], precision=HI, preferred_element_type=F32)

    full = lambda a: pl.BlockSpec(a.shape, lambda i: (0,) * a.ndim)
    row = pl.BlockSpec((SGU_TM, GROUP_WIDTH), lambda i: (i, 0))
    return pl.pallas_call(
        body, name=name, grid=(t // SGU_TM,),
        in_specs=[row, pl.BlockSpec((SGU_TM, GROUP_WIDTH), lambda i: (i, 1)), row, full(w_s), full(w_s_t), full(bm), full(mavg),
                  full(gfold)],
        out_specs=[row, row, pl.BlockSpec((N_HEADS, SGU_CHUNK, SGU_CHUNK), lambda i: (0, 0, 0)),
                   pl.BlockSpec((SGU_CHUNK, 128), lambda i: (0, 0))],
        out_shape=[jax.ShapeDtypeStruct((t, GROUP_WIDTH), F32)] * 2 + [jax.ShapeDtypeStruct((N_HEADS, SGU_CHUNK, SGU_CHUNK), F32),
                                                                      jax.ShapeDtypeStruct((SGU_CHUNK, 128), F32)],
        scratch_shapes=[pltpu.VMEM((SGU_CHUNK, GROUP_WIDTH), F32)],
        compiler_params=_cparams(("arbitrary",)),
    )(z, z, dy, w_s, w_s_t, bm, mavg, gfold)


def _head_lanes():
    lane_head = lax.broadcasted_iota(jnp.int32, (1, 2 * HEAD_DIM), 1) // HEAD_DIM
    return lane_head == 0, lane_head == 1


def _stack_heads(x2):
    h0, h1 = _head_lanes()
    zero = jnp.zeros_like(x2)
    return jnp.concatenate([jnp.where(h0, x2, zero), jnp.where(h1, x2, zero)], axis=0)


def _unstack_heads(y):
    r = y.shape[0] // 2
    h0, _ = _head_lanes()
    return jnp.where(h0, y[:r], y[r:])


def _pair_softmax(qs, k2, biases, valid):
    s = lax.dot_general(qs, k2, _DIMS["nt"], preferred_element_type=F32)
    if biases is not None:
        s = s + jnp.concatenate(biases, axis=0)
    if valid is not None:
        s = jnp.where(jnp.concatenate([valid, valid], axis=0), s, -1e30)
    m = jnp.max(s, axis=-1, keepdims=True)
    e = jnp.exp(s - m)
    l = jnp.sum(e, axis=-1, keepdims=True)
    return e / l, m + jnp.log(l)


def _attn_pair_fwd(q2, k2, v2, biases, valid):
    p, lse = _pair_softmax(_stack_heads(q2), k2, biases, valid)
    o = jnp.dot(p.astype(BF16), v2, preferred_element_type=F32)
    return _unstack_heads(o), _unstack_heads(jnp.broadcast_to(lse, o.shape))


def _attn_pair_bwd(q2, k2, v2, biases, valid, do2, dlse2):
    r = q2.shape[0]
    qs = _stack_heads(q2)
    p, _ = _pair_softmax(qs, k2, biases, valid)
    dos = _stack_heads(do2).astype(BF16)
    dp = lax.dot_general(dos, v2, _DIMS["nt"], preferred_element_type=F32)
    delta = jnp.sum(dp * p, axis=-1, keepdims=True)
    if dlse2 is not None:
        delta = delta - jnp.sum(_stack_heads(dlse2), axis=-1, keepdims=True)
    ds = p * (dp - delta)
    dsb = ds.astype(BF16)
    dq2 = _unstack_heads(jnp.dot(dsb, k2, preferred_element_type=F32))
    dk2 = lax.dot_general(dsb, qs, _DIMS["tn"], preferred_element_type=F32)
    dv2 = lax.dot_general(p.astype(BF16), dos, _DIMS["tn"], preferred_element_type=F32)
    return dq2, dk2, dv2, [ds[:r], ds[r:]]


def _dil_valid(r0, length):
    row = lax.broadcasted_iota(jnp.int32, (DIL_QB, DIL_WIN), 0)
    col = lax.broadcasted_iota(jnp.int32, (DIL_QB, DIL_WIN), 1)
    off = col - DIL_HALF - row
    kpos = r0 - DIL_HALF + col
    return (jnp.abs(off) <= DIL_HALF) & (kpos >= 0) & (kpos < length)


def _dil_build_bias(tab_ref, bkt_ref, bias_ref, dil):
    bkt = bkt_ref[...]
    for h in range(N_HEADS):
        acc = jnp.zeros((DIL_QB, DIL_WIN), F32)
        for b in _dil_live_buckets(dil):
            acc = jnp.where(bkt == b, tab_ref[b, h], acc)
        bias_ref[h] = acc


def _dil_fill_pad(pad_ref, src_ref, length):
    zeros = jnp.zeros((DIL_HALF, GROUP_WIDTH), pad_ref.dtype)
    pad_ref[pl.ds(0, DIL_HALF), :] = zeros
    pad_ref[pl.ds(DIL_HALF + length, DIL_HALF), :] = zeros
    pad_ref[pl.ds(DIL_HALF, length), :] = src_ref[...]


def _dil_specs(bsz, length, dil):
    view = lambda a: a.reshape(bsz, length, dil * GROUP_WIDTH)
    blk = pl.BlockSpec((None, DIL_QB, GROUP_WIDTH), lambda b, rho, i: (b, i, rho))
    seq = pl.BlockSpec((None, length, GROUP_WIDTH), lambda b, rho, i: (b, 0, rho))
    return view, blk, seq


def _dil_fwd(qb, kb, vb, table, dil, name):
    length = SEQ // dil
    bsz = qb.shape[0] // length
    bkt = jnp.asarray(_np_dil_buckets(dil))
    view, blk, seq = _dil_specs(bsz, length, dil)

    def body(tab_ref, bkt_ref, q_ref, k_ref, v_ref, o_ref, lse_ref, kpad, vpad, bias_ref):
        i = pl.program_id(2)

        @pl.when((pl.program_id(0) == 0) & (pl.program_id(1) == 0) & (i == 0))
        def _():
            _dil_build_bias(tab_ref, bkt_ref, bias_ref, dil)

        @pl.when(i == 0)
        def _():
            _dil_fill_pad(kpad, k_ref, length)
            _dil_fill_pad(vpad, v_ref, length)

        r0 = pl.multiple_of(i * DIL_QB, DIL_QB)
        valid = _dil_valid(r0, length)
        for m in range(N_HEADS // 2):
            lanes = pl.ds(m * 128, 128)
            o2, lse2 = _attn_pair_fwd(q_ref[:, lanes], kpad[pl.ds(r0, DIL_WIN), lanes], vpad[pl.ds(r0, DIL_WIN), lanes],
                                      (bias_ref[2 * m], bias_ref[2 * m + 1]), valid)
            o_ref[:, lanes] = o2
            lse_ref[:, lanes] = lse2

    out = jax.ShapeDtypeStruct((bsz, length, dil * GROUP_WIDTH), F32)
    o, lse = pl.pallas_call(
        body, name=name, grid=(bsz, dil, length // DIL_QB),
        in_specs=[pl.BlockSpec(memory_space=pltpu.SMEM), pl.BlockSpec(bkt.shape, lambda b, rho, i: (0, 0)), blk, seq, seq],
        out_specs=[blk, blk], out_shape=[out, out],
        scratch_shapes=[pltpu.VMEM((length + 2 * DIL_HALF, GROUP_WIDTH), BF16), pltpu.VMEM((length + 2 * DIL_HALF, GROUP_WIDTH), BF16),
                        pltpu.VMEM((N_HEADS, DIL_QB, DIL_WIN), F32)],
        compiler_params=_cparams(("arbitrary", "arbitrary", "arbitrary")),
    )(table, bkt, view(qb), view(kb), view(vb))
    return o.reshape(qb.shape), lse.reshape(qb.shape)


def _dil_bwd(qb, kb, vb, do, dlse, table, dil, name):
    length = SEQ // dil
    bsz = qb.shape[0] // length
    nqb = length // DIL_QB
    bkt = jnp.asarray(_np_dil_buckets(dil))
    view, blk, seq = _dil_specs(bsz, length, dil)

    def body(tab_ref, bkt_ref, q_ref, k_ref, v_ref, do_ref, dlse_ref, dq_ref, dk_ref, dv_ref, dsc_ref, kpad, vpad, bias_ref):
        i = pl.program_id(2)

        @pl.when((pl.program_id(0) == 0) & (pl.program_id(1) == 0) & (i == 0))
        def _():
            _dil_build_bias(tab_ref, bkt_ref, bias_ref, dil)
            dsc_ref[...] = jnp.zeros_like(dsc_ref)

        @pl.when(i == 0)
        def _():
            _dil_fill_pad(kpad, k_ref, length)
            _dil_fill_pad(vpad, v_ref, length)
            dk_ref[...] = jnp.zeros_like(dk_ref)
            dv_ref[...] = jnp.zeros_like(dv_ref)

        r0 = pl.multiple_of(i * DIL_QB, DIL_QB)
        valid = _dil_valid(r0, length)
        for m in range(N_HEADS // 2):
            lanes = pl.ds(m * 128, 128)
            dq2, dk2, dv2, ds_heads = _attn_pair_bwd(
                q_ref[:, lanes], kpad[pl.ds(r0, DIL_WIN), lanes], vpad[pl.ds(r0, DIL_WIN), lanes],
                (bias_ref[2 * m], bias_ref[2 * m + 1]), valid, do_ref[:, lanes], dlse_ref[:, lanes])
            dq_ref[:, lanes] = dq2
            dsc_ref[2 * m] += ds_heads[0]
            dsc_ref[2 * m + 1] += ds_heads[1]
            for first, size, live in ((0, DIL_HALF, i >= 1), (DIL_HALF, DIL_QB, None), (DIL_HALF + DIL_QB, DIL_HALF, i <= nqb - 2)):
                def add(first=first, size=size, dk2=dk2, dv2=dv2, lanes=lanes):
                    rows = pl.ds(pl.multiple_of(r0 - DIL_HALF + first, DIL_HALF), size)
                    dk_ref[rows, lanes] += dk2[first:first + size]
                    dv_ref[rows, lanes] += dv2[first:first + size]
                if live is None:
                    add()
                else:
                    pl.when(live)(add)

    out = jax.ShapeDtypeStruct((bsz, length, dil * GROUP_WIDTH), F32)
    dsc_shape = (N_HEADS, DIL_QB, DIL_WIN)
    dq, dk, dv, dsc = pl.pallas_call(
        body, name=name, grid=(bsz, dil, nqb),
        in_specs=[pl.BlockSpec(memory_space=pltpu.SMEM), pl.BlockSpec(bkt.shape, lambda b, rho, i: (0, 0)), blk, seq, seq, blk, blk],
        out_specs=[blk, seq, seq, pl.BlockSpec(dsc_shape, lambda b, rho, i: (0, 0, 0))],
        out_shape=[out, out, out, jax.ShapeDtypeStruct(dsc_shape, F32)],
        scratch_shapes=[pltpu.VMEM((length + 2 * DIL_HALF, GROUP_WIDTH), BF16), pltpu.VMEM((length + 2 * DIL_HALF, GROUP_WIDTH), BF16),
                        pltpu.VMEM(dsc_shape, F32)],
        compiler_params=_cparams(("arbitrary", "arbitrary", "arbitrary")),
    )(table, bkt, view(qb), view(kb), view(vb), view(do), view(dlse))
    return dq.reshape(qb.shape), dk.reshape(qb.shape), dv.reshape(qb.shape), dsc


def _headnorm(x, g, mavg):
    return x * lax.rsqrt(_head_stat(x * x, mavg) + RMS_EPS) * g


def _fold_gain(dg_full, fold):
    return jnp.dot(jnp.broadcast_to(dg_full, (8, dg_full.shape[1])), fold, precision=HI, preferred_element_type=F32)


def _bprep_fn(qp, kp, gq, gk, mavg):
    return _headnorm(qp, gq, mavg) * (HEAD_DIM ** -0.5), _headnorm(kp, gk, mavg)


def _dil_layout(dil):
    return None if dil == 1 else Strided(dil)


def _bprep_fwd(z, gq, gk, name):
    mavg = jnp.asarray(_np_group_avg(GROUP_WIDTH))

    def fn(qp, kp, vp, gqv, gkv, mv):
        qb, kb = _bprep_fn(qp, kp, gqv, gkv, mv)
        return (qb, kb, vp) * len(DIL_PATTERNS)

    w = GROUP_WIDTH
    outs = [(w, BF16, _dil_layout(dil)) for _, dil in DIL_PATTERNS for _ in range(3)]
    res = _rowmap(fn, [(z, w, 2), (z, w, 3), (z, w, 4)], [gq, gk, mavg], outs, [], name, 512, z.shape[0])
    return [res[3 * i:3 * i + 3] for i in range(len(DIL_PATTERNS))]


def _bprep_bwd(z, dqs, dks, dvs, gq, gk, name):
    mavg = jnp.asarray(_np_group_avg(GROUP_WIDTH))
    fold = jnp.asarray(_np_tile_fold(GROUP_WIDTH))

    def fn(qp, kp, dq0, dq1, dq2, dk0, dk1, dk2, dv0, dv1, dv2, gqv, gkv, mv, fv):
        _, vjp = jax.vjp(functools.partial(_bprep_fn, mavg=mv), qp, kp, gqv, gkv)
        dqp, dkp, dgq, dgk = vjp((dq0 + dq1 + dq2, dk0 + dk1 + dk2))
        return dqp, dkp, dv0 + dv1 + dv2, _fold_gain(dgq, fv), _fold_gain(dgk, fv)

    w = GROUP_WIDTH
    rows = [(z, w, 2), (z, w, 3)] + _pattern_rows(dqs) + _pattern_rows(dks) + _pattern_rows(dvs)
    return _rowmap(fn, rows, [gq, gk, mavg, fold], [(w, F32)] * 3, [(8, HEAD_DIM)] * 2, name, 256, z.shape[0])


def _mixture_fn(o0, o1, o2, l0, l1, l2):
    m = lax.stop_gradient(jnp.maximum(jnp.maximum(l0, l1), l2))
    e0, e1, e2 = jnp.exp(l0 - m), jnp.exp(l1 - m), jnp.exp(l2 - m)
    return (e0 * o0 + e1 * o1 + e2 * o2) / (e0 + e1 + e2)


def _pattern_rows(arrs):
    return [(a, GROUP_WIDTH, 0) if dil == 1 else (a, GROUP_WIDTH, 0, Strided(dil)) for a, (_, dil) in zip(arrs, DIL_PATTERNS)]


def _mixture_fwd(os_, ls_, name):
    n = os_[0].shape[0]
    return _rowmap(lambda *v: (_mixture_fn(*v),), _pattern_rows(os_) + _pattern_rows(ls_), [], [(GROUP_WIDTH, F32)], [], name, 512, n)[0]


def _mixture_bwd(os_, ls_, dy, name):
    w = GROUP_WIDTH

    def fn(*v):
        _, vjp = jax.vjp(_mixture_fn, *v[:6])
        return vjp(v[6])

    outs = [(w, F32, _dil_layout(dil)) for _ in range(2) for _, dil in DIL_PATTERNS]
    return _rowmap(fn, _pattern_rows(os_) + _pattern_rows(ls_) + [(dy, w, 0)], [], outs, [], name, 512, dy.shape[0])


def _relbias_fold(dscs, name):
    bkts = [jnp.asarray(_np_dil_buckets(dil)) for _, dil in DIL_PATTERNS]
    npat = len(DIL_PATTERNS)

    def body(*refs):
        bkt_refs, d_refs, o_ref = refs[:npat], refs[npat:-1], refs[-1]
        row = lax.broadcasted_iota(jnp.int32, (REL_BUCKETS, 128), 0)
        lane = lax.broadcasted_iota(jnp.int32, (REL_BUCKETS, 128), 1)
        out = jnp.zeros((REL_BUCKETS, 128), F32)
        for p, (_, dil) in enumerate(DIL_PATTERNS):
            bkt = bkt_refs[p][...]
            for h in range(N_HEADS):
                d = d_refs[2 * p][h] + d_refs[2 * p + 1][h]
                for b in _dil_live_buckets(dil):
                    val = jnp.sum(jnp.where(bkt == b, d, 0.0), keepdims=True)
                    out = out + jnp.where((row == b) & (lane == h), val, 0.0)
        o_ref[...] = out

    return pl.pallas_call(
        body, name=name, out_shape=jax.ShapeDtypeStruct((REL_BUCKETS, 128), F32), compiler_params=_cparams(),
    )(*bkts, *dscs)


DPREP_TM = 512


def _dprep_fn(qp, kp, vp, gq, gk, cq, sq, ck, sk, mavg_q, mavg_k, perm_q, perm_k, expand):
    rot = lambda x, perm: jnp.dot(x, perm, precision=SPLIT3, preferred_element_type=F32)
    qn = _headnorm(qp, gq, mavg_q)
    kn = _headnorm(kp, gk, mavg_k)
    qr = (qn * cq + rot(qn, perm_q) * sq) * (HEAD_DIM ** -0.5)
    kr = kn * ck + rot(kn, perm_k) * sk
    return qr, rot(kr, expand), rot(vp, expand)


def _dprep_consts():
    cq, sq = _np_rope_tables(N_HEADS)
    ck, sk = _np_rope_tables(KV_WIDTH // HEAD_DIM)
    tables = [jnp.asarray(a) for a in (cq, sq, ck, sk)]
    mats = [jnp.asarray(a) for a in (_np_group_avg(GROUP_WIDTH), _np_group_avg(KV_WIDTH), _np_rope_partner(GROUP_WIDTH),
                                      _np_rope_partner(KV_WIDTH), _np_kv_expand())]
    per = SEQ // DPREP_TM
    w, kw = GROUP_WIDTH, KV_WIDTH
    table_rows = [(tables[0], w, 0, per), (tables[1], w, 0, per), (tables[2], kw, 0, per), (tables[3], kw, 0, per)]
    return table_rows, mats


def _dprep_fwd(z, gq, gk, name):
    table_rows, mats = _dprep_consts()
    w, kw = GROUP_WIDTH, KV_WIDTH

    def fn(qp, kp, vp, cq, sq, ck, sk, gqv, gkv, *m):
        return _dprep_fn(qp, kp, vp, gqv, gkv, cq, sq, ck, sk, *m)

    return _rowmap(fn, [(z, w, 7), (z, kw, 32), (z, kw, 33)] + table_rows, [gq, gk] + mats, [(w, BF16)] * 3, [], name,
                   DPREP_TM, z.shape[0])


def _dprep_bwd(z, dq, dkx, dvx, gq, gk, name):
    table_rows, mats = _dprep_consts()
    fold_q = jnp.asarray(_np_tile_fold(GROUP_WIDTH))
    fold_k = jnp.asarray(_np_tile_fold(KV_WIDTH))
    w, kw = GROUP_WIDTH, KV_WIDTH

    def fn(qp, kp, vp, dqv, dkv, dvv, cq, sq, ck, sk, gqv, gkv, fq, fk, *m):
        f = lambda a, b, c, d, e: _dprep_fn(a, b, c, d, e, cq, sq, ck, sk, *m)
        _, vjp = jax.vjp(f, qp, kp, vp, gqv, gkv)
        dqp, dkp, dvp, dgq, dgk = vjp((dqv, dkv, dvv))
        return dqp, dkp, dvp, _fold_gain(dgq, fq), _fold_gain(dgk, fk)

    return _rowmap(fn, [(z, w, 7), (z, kw, 32), (z, kw, 33), (dq, w, 0), (dkx, w, 0), (dvx, w, 0)] + table_rows,
                   [gq, gk, fold_q, fold_k] + mats, [(w, F32), (kw, F32), (kw, F32)], [(8, HEAD_DIM)] * 2, name,
                   DPREP_TM, z.shape[0])


GQA_QB = 256


def _gqa_fwd(q, kx, vx, name):
    bsz = q.shape[0]
    blk = pl.BlockSpec((None, GQA_QB, GROUP_WIDTH), lambda b, i: (b, i, 0))
    seq = pl.BlockSpec((None, SEQ, GROUP_WIDTH), lambda b, i: (b, 0, 0))

    def body(q_ref, k_ref, v_ref, o_ref):
        for m in range(N_HEADS // 2):
            lanes = pl.ds(m * 128, 128)
            o_ref[:, lanes] = _attn_pair_fwd(q_ref[:, lanes], k_ref[:, lanes], v_ref[:, lanes], None, None)[0]

    return pl.pallas_call(
        body, name=name, grid=(bsz, SEQ // GQA_QB), in_specs=[blk, seq, seq], out_specs=blk,
        out_shape=jax.ShapeDtypeStruct((bsz, SEQ, GROUP_WIDTH), F32), compiler_params=_cparams(("parallel", "parallel")),
    )(q, kx, vx)


def _gqa_bwd(q, kx, vx, do, name):
    bsz = q.shape[0]
    blk = pl.BlockSpec((None, GQA_QB, GROUP_WIDTH), lambda b, i: (b, i, 0))
    seq = pl.BlockSpec((None, SEQ, GROUP_WIDTH), lambda b, i: (b, 0, 0))

    def body(q_ref, k_ref, v_ref, do_ref, dq_ref, dk_ref, dv_ref):
        @pl.when(pl.program_id(1) == 0)
        def _():
            dk_ref[...] = jnp.zeros_like(dk_ref)
            dv_ref[...] = jnp.zeros_like(dv_ref)

        for m in range(N_HEADS // 2):
            lanes = pl.ds(m * 128, 128)
            dq2, dk2, dv2, _ = _attn_pair_bwd(q_ref[:, lanes], k_ref[:, lanes], v_ref[:, lanes], None, None, do_ref[:, lanes], None)
            dq_ref[:, lanes] = dq2
            dk_ref[:, lanes] += dk2
            dv_ref[:, lanes] += dv2

    out = jax.ShapeDtypeStruct((bsz, SEQ, GROUP_WIDTH), F32)
    return pl.pallas_call(
        body, name=name, grid=(bsz, SEQ // GQA_QB), in_specs=[blk, seq, seq, blk], out_specs=[blk, seq, seq],
        out_shape=[out, out, out], compiler_params=_cparams(("parallel", "arbitrary")),
    )(q, kx, vx, do)


CONV_TILE = 64
CONV_LEAD = 16
CONV_WINDOW = CONV_TILE + 32


def _glu(a, g):
    return a * jax.nn.sigmoid(g)


def _conv_post(c, b, ln_g, ln_b):
    x = c + b
    xc = x - jnp.mean(x, axis=-1, keepdims=True)
    y = xc * lax.rsqrt(jnp.mean(xc * xc, axis=-1, keepdims=True) + LN_EPS) * ln_g + ln_b
    return y * jax.nn.sigmoid(y)


def _conv_shifted(win, offset):
    return pltpu.roll(win, CONV_WINDOW - offset, 0)[:CONV_TILE]


def _conv_fill(pad_ref, value_of_tile):
    zeros = jnp.zeros((CONV_LEAD, GROUP_WIDTH), F32)
    pad_ref[pl.ds(0, CONV_LEAD), :] = zeros
    pad_ref[pl.ds(CONV_LEAD + SEQ, CONV_LEAD), :] = zeros

    def step(t, carry):
        r0 = pl.multiple_of(t * CONV_TILE, CONV_TILE)
        pad_ref[pl.ds(CONV_LEAD + r0, CONV_TILE), :] = value_of_tile(r0)
        return carry

    lax.fori_loop(0, SEQ // CONV_TILE, step, 0)


def _conv_tile(pad_ref, w_ref, r0, flip):
    win = pad_ref[pl.ds(r0, CONV_WINDOW), :]
    acc = jnp.zeros((CONV_TILE, GROUP_WIDTH), F32)
    for k in range(CONV_WIDTH):
        offset = (CONV_WIDTH - k) if flip else (k + 1)
        acc = acc + w_ref[pl.ds(k, 1), :] * _conv_shifted(win, offset)
    return acc


def _conv_fwd(z3, w, b, ln_g, ln_b, name):
    bsz = z3.shape[0]
    seq = lambda cb: pl.BlockSpec((None, SEQ, GROUP_WIDTH), functools.partial(lambda i, cb: (i, 0, cb), cb=cb))
    full = lambda a: pl.BlockSpec(a.shape, lambda i: (0,) * a.ndim)

    def body(a_ref, g_ref, w_ref, b_ref, lg_ref, lb_ref, y_ref, pad_ref):
        _conv_fill(pad_ref, lambda r0: _glu(a_ref[pl.ds(r0, CONV_TILE), :], g_ref[pl.ds(r0, CONV_TILE), :]))

        def step(t, carry):
            r0 = pl.multiple_of(t * CONV_TILE, CONV_TILE)
            y_ref[pl.ds(r0, CONV_TILE), :] = _conv_post(_conv_tile(pad_ref, w_ref, r0, False), b_ref[...], lg_ref[...], lb_ref[...])
            return carry

        lax.fori_loop(0, SEQ // CONV_TILE, step, 0)

    return pl.pallas_call(
        body, name=name, grid=(bsz,), in_specs=[seq(5), seq(6), full(w), full(b), full(ln_g), full(ln_b)], out_specs=seq(0),
        out_shape=jax.ShapeDtypeStruct((bsz, SEQ, GROUP_WIDTH), F32),
        scratch_shapes=[pltpu.VMEM((SEQ + 2 * CONV_LEAD, GROUP_WIDTH), F32)], compiler_params=_cparams(("parallel",)),
    )(z3, z3, w, b, ln_g, ln_b)


def _conv_bwd(z3, dy, w, b, ln_g, ln_b, name):
    bsz = z3.shape[0]
    seq = lambda cb: pl.BlockSpec((None, SEQ, GROUP_WIDTH), functools.partial(lambda i, cb: (i, 0, cb), cb=cb))
    full = lambda a: pl.BlockSpec(a.shape, lambda i: (0,) * a.ndim)
    vec = pl.BlockSpec((1, GROUP_WIDTH), lambda i: (0, 0))

    def body(a_ref, g_ref, dy_ref, w_ref, b_ref, lg_ref, lb_ref, da_ref, dg_ref, dw_ref, db_ref, dlg_ref, dlb_ref, hpad, dpad, dw8):
        @pl.when(pl.program_id(0) == 0)
        def _():
            dw8[...] = jnp.zeros_like(dw8)
            db_ref[...] = jnp.zeros_like(db_ref)
            dlg_ref[...] = jnp.zeros_like(dlg_ref)
            dlb_ref[...] = jnp.zeros_like(dlb_ref)

        _conv_fill(hpad, lambda r0: _glu(a_ref[pl.ds(r0, CONV_TILE), :], g_ref[pl.ds(r0, CONV_TILE), :]))
        zeros = jnp.zeros((CONV_LEAD, GROUP_WIDTH), F32)
        dpad[pl.ds(0, CONV_LEAD), :] = zeros
        dpad[pl.ds(CONV_LEAD + SEQ, CONV_LEAD), :] = zeros

        def through_post(t, carry):
            r0 = pl.multiple_of(t * CONV_TILE, CONV_TILE)
            conv = _conv_tile(hpad, w_ref, r0, False)
            _, vjp = jax.vjp(_conv_post, conv, b_ref[...], lg_ref[...], lb_ref[...])
            dconv, db, dlg, dlb = vjp(dy_ref[pl.ds(r0, CONV_TILE), :])
            db_ref[...] += db
            dlg_ref[...] += dlg
            dlb_ref[...] += dlb
            dpad[pl.ds(CONV_LEAD + r0, CONV_TILE), :] = dconv
            win = hpad[pl.ds(r0, CONV_WINDOW), :]
            for k in range(CONV_WIDTH):
                prod = dconv * _conv_shifted(win, k + 1)
                part = prod[0:8]
                for j in range(1, CONV_TILE // 8):
                    part = part + prod[8 * j:8 * j + 8]
                dw8[k] += part
            return carry

        lax.fori_loop(0, SEQ // CONV_TILE, through_post, 0)

        def through_glu(t, carry):
            r0 = pl.multiple_of(t * CONV_TILE, CONV_TILE)
            dh = _conv_tile(dpad, w_ref, r0, True)
            rows = pl.ds(r0, CONV_TILE)
            _, vjp = jax.vjp(_glu, a_ref[rows, :], g_ref[rows, :])
            da, dg = vjp(dh)
            da_ref[rows, :] = da
            dg_ref[rows, :] = dg
            return carry

        lax.fori_loop(0, SEQ // CONV_TILE, through_glu, 0)
        dw_ref[...] = jnp.sum(dw8[...], axis=1)

    out = jax.ShapeDtypeStruct((bsz, SEQ, GROUP_WIDTH), F32)
    v = jax.ShapeDtypeStruct((1, GROUP_WIDTH), F32)
    return pl.pallas_call(
        body, name=name, grid=(bsz,), in_specs=[seq(5), seq(6), seq(0), full(w), full(b), full(ln_g), full(ln_b)],
        out_specs=[seq(0), seq(0), pl.BlockSpec((CONV_WIDTH, GROUP_WIDTH), lambda i: (0, 0)), vec, vec, vec],
        out_shape=[out, out, jax.ShapeDtypeStruct((CONV_WIDTH, GROUP_WIDTH), F32), v, v, v],
        scratch_shapes=[pltpu.VMEM((SEQ + 2 * CONV_LEAD, GROUP_WIDTH), F32), pltpu.VMEM((SEQ + 2 * CONV_LEAD, GROUP_WIDTH), F32),
                        pltpu.VMEM((CONV_WIDTH, 8, GROUP_WIDTH), F32)],
        compiler_params=_cparams(("arbitrary",)),
    )(z3, z3, dy, w, b, ln_g, ln_b)


def _mixnorm_fwd(ys, gains, name):
    w = GROUP_WIDTH

    def fn(*v):
        return (jnp.concatenate([_rms(v[i], v[4 + i]) for i in range(4)], axis=-1),)

    return _rowmap(fn, [(y, w, 0) for y in ys], list(gains), [(4 * w, BF16)], [], name, 512, ys[0].shape[0])[0]


def _mixnorm_bwd(dyn, ys, gains, name, follow=None):
    w = GROUP_WIDTH
    gains = list(gains) if follow is None else [*gains, follow]

    def fn(*v):
        dys, dgs = [], []
        for i in range(4):
            _, vjp = jax.vjp(_rms, v[4 + i], v[8 + i])
            dy, dg = vjp(v[i])
            dys.append(dy)
            dgs.append(dg)
        return (*dys, *dgs)

    rows = [(dyn, w, i) for i in range(4)] + [(y, w, 0) for y in ys]
    return _rowmap(fn, rows, list(gains), [(w, F32)] * 4, [(1, w)] * 4, name, 512, dyn.shape[0])


def _adamw_fn(w, g, m, v):
    m = ADAM_B1 * m + (1.0 - ADAM_B1) * g
    v = ADAM_B2 * v + (1.0 - ADAM_B2) * (g * g)
    m_hat = m / (1.0 - ADAM_B1 ** ADAM_STEP)
    v_hat = v / (1.0 - ADAM_B2 ** ADAM_STEP)
    delta = -ADAM_LR * (m_hat / (jnp.sqrt(v_hat) + ADAM_EPS) + ADAM_WD * w)
    return delta, m, v


def _adamw(w, g, m, v, name):
    r, c = w.shape
    tm = _pick(r, (256, 128, 64, 32, 16, 8))
    return _rowmap(_adamw_fn, [(a, c, 0) for a in (w, g, m, v)], [], [(c, F32)] * 3, [], name, tm, r)


def _layer_params(l, small, big):
    tile_row = lambda g, n: jnp.tile(g, n)[None, :]
    row = lambda g: g[None, :]
    w_s = small["sgu_w"][l].astype(BF16)
    return dict(
        norm1_g=row(small["norm1_g"][l]), norm2_g=row(small["norm2_g"][l]),
        w_s=w_s, w_s_t=jnp.swapaxes(w_s, 1, 2), bm=jnp.repeat(small["sgu_b"][l].T, HEAD_DIM, axis=1),
        gq_dil=tile_row(small["dil_qn_g"][l], N_HEADS), gk_dil=tile_row(small["dil_kn_g"][l], N_HEADS),
        conv_w=small["conv_w"][l], conv_b=row(small["conv_b"][l]), conv_ln_g=row(small["conv_ln_g"][l]),
        conv_ln_b=row(small["conv_ln_b"][l]),
        gq_gqa=tile_row(small["gqa_qn_g"][l], N_HEADS), gk_gqa=tile_row(small["gqa_kn_g"][l], KV_WIDTH // HEAD_DIM),
        mix_g=[row(small["mix_norm_g"][l][i * GROUP_WIDTH:(i + 1) * GROUP_WIDTH]) for i in range(4)],
        big=big,
    )


def _layer_fwd(x, p, table, bsz, tag):
    t = x.shape[0]
    seq3 = lambda a: a.reshape(bsz, SEQ, a.shape[-1])
    flat = lambda a: a.reshape(t, a.shape[-1])
    h1 = _rmsnorm_fwd(x, p["norm1_g"], tag + "rms1")
    z = _matmul([(h1, p["big"]("w_in", h1))], "nn", F32, tag + "mm_z")
    y_a = _sgu_fwd(z, p["w_s"], p["bm"], tag + "sgu_fwd")
    dil_qkv = _bprep_fwd(z, p["gq_dil"], p["gk_dil"], tag + "dil_prep")
    outs, lses = [], []
    for (_, dil), (qb, kb, vb) in zip(DIL_PATTERNS, dil_qkv):
        o, lse = _dil_fwd(qb, kb, vb, table, dil, f"{tag}dil{dil}_fwd")
        outs.append(o)
        lses.append(lse)
    y_b = _mixture_fwd(outs, lses, tag + "dil_mix")
    y_c = flat(_conv_fwd(seq3(z), p["conv_w"], p["conv_b"], p["conv_ln_g"], p["conv_ln_b"], tag + "conv_fwd"))
    qd, kx, vx = _dprep_fwd(z, p["gq_gqa"], p["gk_gqa"], tag + "gqa_prep")
    y_d = flat(_gqa_fwd(seq3(qd), seq3(kx), seq3(vx), tag + "gqa_fwd"))
    ys = [y_a, y_b, y_c, y_d]
    yn = _mixnorm_fwd(ys, p["mix_g"], tag + "mixnorm")
    x_mid = _matmul([(yn, p["big"]("w_out", yn))], "nn", F32, tag + "mm_out", residual=x)
    h2 = _rmsnorm_fwd(x_mid, p["norm2_g"], tag + "rms2")
    act, act_du, act_dg = _ffn_up(h2, p["big"]("w_gate", h2), p["big"]("w_up", h2), tag + "ffn_up")
    x_out = _matmul([(act, p["big"]("w_down", act))], "nn", F32, tag + "mm_down", residual=x_mid)
    saved = dict(x=x, h1=h1, z=z, dil_qkv=dil_qkv, outs=outs, lses=lses, qd=qd, kx=kx, vx=vx, ys=ys, yn=yn, x_mid=x_mid,
                 h2=h2, act=act, act_du=act_du, act_dg=act_dg)
    return x_out, saved


def _layer_bwd(dx_out, dx_out_b, s, p, table, bsz, tag, emit, mid_hook):
    t = dx_out.shape[0]
    seq3 = lambda a: a.reshape(bsz, SEQ, a.shape[-1])
    flat = lambda a: a.reshape(t, a.shape[-1])
    z = s["z"]
    small = {}
    weight = lambda name: p["big"](name, None)
    emit("w_down", _matmul([(s["act"], dx_out_b)], "tn", BF16, tag + "mm_dwdown").reshape(N_CHIPS, FFN_HIDDEN // N_CHIPS, D_MODEL))
    dgate, dup = _ffn_down_bwd(dx_out_b, weight("w_down"), s["act_du"], s["act_dg"], tag + "ffn_dact")
    emit("w_gate", _matmul([(s["h2"], dgate)], "tn", BF16, tag + "mm_dwgate", slabs=N_CHIPS))
    started = emit("w_up", _matmul([(s["h2"], dup)], "tn", BF16, tag + "mm_dwup", slabs=N_CHIPS))
    dh2 = _matmul([(dgate, weight("w_gate")), (dup, weight("w_up"))], "nt", F32, tag + "mm_dh2")
    dx_mid, dx_mid_b, dg2 = _rmsnorm_bwd(dh2, s["x_mid"], p["norm2_g"], dx_out, tag + "rms2_bwd", follow=started)
    small["norm2_g"] = dg2[0]
    mid_hook(dx_mid)
    dyn = _matmul([(dx_mid_b, weight("w_out"))], "nt", F32, tag + "mm_dyn")
    started = emit("w_out", _matmul([(s["yn"], dx_mid_b)], "tn", BF16, tag + "mm_dwout").reshape(N_CHIPS, D_MODEL // N_CHIPS, D_MODEL))
    *dys, dga, dgb, dgc, dgd = _mixnorm_bwd(dyn, s["ys"], p["mix_g"], tag + "mixnorm_bwd", follow=started)
    small["mix_norm_g"] = jnp.concatenate([dga[0], dgb[0], dgc[0], dgd[0]])
    du, dv, dws, dbs = _sgu_bwd(z, dys[0], p["w_s"], p["w_s_t"], p["bm"], tag + "sgu_bwd")
    small["sgu_w"] = dws
    small["sgu_b"] = dbs[:, :N_HEADS].T
    *douts, dl0, dl1, dl2 = _mixture_bwd(s["outs"], s["lses"], dys[1], tag + "dil_mix_bwd")
    dlses = [dl0, dl1, dl2]
    dqs, dks, dvs, dscs = [], [], [], []
    for i, (_, dil) in enumerate(DIL_PATTERNS):
        dq, dk, dvv, dsc = _dil_bwd(*s["dil_qkv"][i], douts[i], dlses[i], table, dil, f"{tag}dil{dil}_bwd")
        dqs.append(dq)
        dks.append(dk)
        dvs.append(dvv)
        dscs.append(dsc)
    dbq, dbk, dbv, dgq, dgk = _bprep_bwd(z, dqs, dks, dvs, p["gq_dil"], p["gk_dil"], tag + "dil_prep_bwd")
    small["dil_qn_g"], small["dil_kn_g"] = dgq[0], dgk[0]
    dca, dcg, dcw, dcb, dclg, dclb = _conv_bwd(seq3(z), seq3(dys[2]), p["conv_w"], p["conv_b"], p["conv_ln_g"], p["conv_ln_b"],
                                               tag + "conv_bwd")
    small["conv_w"], small["conv_b"], small["conv_ln_g"], small["conv_ln_b"] = dcw, dcb[0], dclg[0], dclb[0]
    dqd, dkx, dvx = _gqa_bwd(seq3(s["qd"]), seq3(s["kx"]), seq3(s["vx"]), seq3(dys[3]), tag + "gqa_bwd")
    ddq, ddk, ddv, dgq, dgk = _dprep_bwd(z, flat(dqd), flat(dkx), flat(dvx), p["gq_gqa"], p["gk_gqa"], tag + "gqa_prep_bwd")
    small["gqa_qn_g"], small["gqa_kn_g"] = dgq[0], dgk[0]
    dz = jnp.concatenate([a.astype(BF16) for a in (du, dv, dbq, dbk, dbv, flat(dca), flat(dcg), ddq, ddk, ddv)], axis=1)
    dz4 = dz.reshape(t, N_CHIPS, IN_WIDTH // N_CHIPS).transpose(1, 0, 2)
    started = emit("w_in", _matmul([(s["h1"], dz4)], "tn", BF16, tag + "mm_dwin", slabs=N_CHIPS))
    dh1 = _matmul([(dz, weight("w_in"))], "nt", F32, tag + "mm_dh1")
    dx, dx_b, dg1 = _rmsnorm_bwd(dh1, s["x"], p["norm1_g"], dx_mid, tag + "rms1_bwd", follow=started)
    small["norm1_g"] = dg1[0]
    return dx, dx_b, small, dscs


def _local_step(x, target, small, big, emit, mid_hook, bsz):
    table = small["rel_bias"]
    params = [_layer_params(l, small, functools.partial(big, l)) for l in range(DEPTH)]
    saved = []
    h = x
    for l in range(DEPTH):
        h, sv = _layer_fwd(h, params[l], table, bsz, f"l{l}_")
        saved.append(sv)
    dh, dh_b, loss = _loss_fwd_bwd(h, target, "loss")
    small_grads, dscs = [None] * DEPTH, [None] * DEPTH
    for l in reversed(range(DEPTH)):
        dh, dh_b, small_grads[l], dscs[l] = _layer_bwd(dh, dh_b, saved[l], params[l], table, bsz, f"l{l}_",
                                                       functools.partial(emit, l), functools.partial(mid_hook, l))
    fold_in = [dscs[l][i] for i in range(len(DIL_PATTERNS)) for l in range(DEPTH)]
    stacked = {k: jnp.stack([small_grads[l][k] for l in range(DEPTH)]) for k in small_grads[0]}
    stacked["rel_bias"] = _relbias_fold(fold_in, "relbias_fold")[:, :N_HEADS]
    return loss, dh, stacked


def _mesh_pos():
    return lax.axis_index("x"), lax.axis_index("y"), lax.axis_index("c")


def _other_chips(x, y):
    return [(1 - x, y), (x, 1 - y), (1 - x, 1 - y)]


_ANY = pl.BlockSpec(memory_space=pl.ANY)


def _swap_other_half(arrs, name):
    n = len(arrs)

    def body(*refs):
        in_refs, out_refs, send_sems, recv_sems = refs[:n], refs[n:2 * n], refs[2 * n], refs[2 * n + 1]
        x, y, c = _mesh_pos()
        copies = []
        for k in range(n):
            h = arrs[k].shape[1] // 2
            copies.append(pltpu.make_async_remote_copy(
                src_ref=in_refs[k].at[:, pl.ds((1 - c) * h, h)], dst_ref=out_refs[k], send_sem=send_sems.at[k],
                recv_sem=recv_sems.at[k], device_id=(x, y, 1 - c), device_id_type=MESH))
        for cp in copies:
            cp.start()
        for cp in copies:
            cp.wait()

    return pl.pallas_call(
        body, name=name, in_specs=[_ANY] * n, out_specs=[_ANY] * n,
        out_shape=[jax.ShapeDtypeStruct((a.shape[0], a.shape[1] // 2, a.shape[2]), a.dtype) for a in arrs],
        scratch_shapes=[pltpu.SemaphoreType.DMA((n,)), pltpu.SemaphoreType.DMA((n,))],
    )(*arrs)


def _swap_sibling(arrs, name):
    n = len(arrs)

    def body(*refs):
        in_refs, out_refs, send_sems, recv_sems = refs[:n], refs[n:2 * n], refs[2 * n], refs[2 * n + 1]
        x, y, c = _mesh_pos()
        copies = [pltpu.make_async_remote_copy(src_ref=in_refs[k], dst_ref=out_refs[k], send_sem=send_sems.at[k],
                                               recv_sem=recv_sems.at[k], device_id=(x, y, 1 - c), device_id_type=MESH)
                  for k in range(n)]
        for cp in copies:
            cp.start()
        for cp in copies:
            cp.wait()

    return pl.pallas_call(
        body, name=name, in_specs=[_ANY] * n, out_specs=[_ANY] * n,
        out_shape=[jax.ShapeDtypeStruct(a.shape, a.dtype) for a in arrs],
        scratch_shapes=[pltpu.SemaphoreType.DMA((n,)), pltpu.SemaphoreType.DMA((n,))],
    )(*arrs)


def _complete_pairs(arrs, name):
    n = len(arrs)

    def body(*refs):
        in_refs, out_refs, send_sems, recv_sems = refs[:n], refs[n:2 * n], refs[2 * n], refs[2 * n + 1]
        x, y, c = _mesh_pos()
        copies = [pltpu.make_async_remote_copy(src_ref=in_refs[k].at[c], dst_ref=out_refs[k].at[c], send_sem=send_sems.at[k],
                                               recv_sem=recv_sems.at[k], device_id=(x, y, 1 - c), device_id_type=MESH)
                  for k in range(n)]
        for cp in copies:
            cp.start()
        for k, cp in enumerate(copies):
            cp.wait_send()
            pltpu.make_async_remote_copy(src_ref=in_refs[k].at[1 - c], dst_ref=out_refs[k].at[1 - c], send_sem=send_sems.at[k],
                                         recv_sem=recv_sems.at[k], device_id=(x, y, 1 - c), device_id_type=MESH).wait_recv()

    return pl.pallas_call(
        body, name=name, in_specs=[_ANY] * n, out_specs=[_ANY] * n,
        out_shape=[jax.ShapeDtypeStruct(a.shape, a.dtype) for a in arrs], input_output_aliases={k: k for k in range(n)},
        scratch_shapes=[pltpu.SemaphoreType.DMA((n,)), pltpu.SemaphoreType.DMA((n,))],
    )(*arrs)


_HBM = pl.BlockSpec(memory_space=pltpu.HBM)
_SEM = pl.BlockSpec(memory_space=pltpu.SEMAPHORE)
_DATAFLOW = pltpu.SideEffectType.DATAFLOW_SIDE_EFFECTING


def _chip_copies(src_refs, land_refs, send_sems, recv_sems, scatter):
    x, y, c = _mesh_pos()
    me = 2 * x + y
    out = []
    for k, (src_ref, land_ref) in enumerate(zip(src_refs, land_refs)):
        slot = (lambda chip: land_ref.at[c, chip]) if len(land_ref.shape) == 4 else (lambda chip: land_ref.at[chip])
        for j, (cx, cy) in enumerate(_other_chips(x, y)):
            there = 2 * cx + cy
            src = src_ref.at[there] if scatter else src_ref
            sems = dict(send_sem=send_sems.at[3 * k + j], recv_sem=recv_sems.at[3 * k + j], device_id=(cx, cy, c), device_id_type=MESH)
            out.append((pltpu.make_async_remote_copy(src_ref=src, dst_ref=slot(me), **sems),
                        pltpu.make_async_remote_copy(src_ref=src, dst_ref=slot(there), **sems)))
    return out


def _chips_start(srcs, scatter, after, name, per_core=False):
    n = len(srcs)
    lead = (2, N_CHIPS) if per_core else (N_CHIPS,)
    lands = [lax.empty((*lead, *s.shape[-2:]), s.dtype) for s in srcs]

    def body(*refs):
        src_refs, land_refs = refs[:n], refs[n:2 * n]
        send_sems, recv_sems, token = refs[2 * n + 1], refs[2 * n + 2], refs[-1]
        for sent, _ in _chip_copies(src_refs, land_refs, send_sems, recv_sems, scatter):
            sent.start()
        token[...] = jnp.zeros_like(token)

    hbm = lambda a: pltpu.HBM(a.shape, a.dtype)
    res = pl.pallas_call(
        body, name=name,
        in_specs=[_HBM] * (2 * n) + [_ANY],
        out_specs=[_SEM, _SEM] + [_HBM] * (2 * n) + [pl.BlockSpec(memory_space=pltpu.VMEM)],
        out_shape=[pltpu.SemaphoreType.DMA((3 * n,)), pltpu.SemaphoreType.DMA((3 * n,))] + [hbm(a) for a in srcs] + [hbm(a) for a in lands]
        + [jax.ShapeDtypeStruct((8, 128), F32)],
        input_output_aliases={i: 2 + i for i in range(2 * n)},
        compiler_params=pltpu.CompilerParams(has_side_effects=_DATAFLOW),
    )(*[pltpu.with_memory_space_constraint(a, pltpu.HBM) for a in (*srcs, *lands)], after)
    return (res[0], res[1], res[2:2 + n], res[2 + n:2 + 2 * n]), res[-1]


def _chips_wait(handle, scatter, after, name):
    send_sems, recv_sems, srcs, lands = handle
    n = len(srcs)

    def body(*refs):
        src_refs, land_refs = refs[:n], refs[n:2 * n]
        send_sems, recv_sems = refs[2 * n], refs[2 * n + 1]
        for sent, landed in _chip_copies(src_refs, land_refs, send_sems, recv_sems, scatter):
            sent.wait_send()
            landed.wait_recv()

    hbm = lambda a: pltpu.HBM(a.shape, a.dtype)
    res = pl.pallas_call(
        body, name=name,
        in_specs=[_HBM] * (2 * n) + [_SEM, _SEM, _ANY], out_specs=[_HBM] * (2 * n),
        out_shape=[hbm(a) for a in srcs] + [hbm(a) for a in lands],
        input_output_aliases={i: i for i in range(2 * n)},
        compiler_params=pltpu.CompilerParams(has_side_effects=_DATAFLOW),
    )(*srcs, *lands, send_sems, recv_sems, after)
    return res[n:]


N_DEV = 8


def _everyone_copies(src_ref, land_ref, send_sems, recv_sems):
    x, y, c = _mesh_pos()
    out = []
    for q in range(N_DEV - 1):
        fx, fy, fc = ((q + 1) >> 2) & 1, ((q + 1) >> 1) & 1, (q + 1) & 1
        px, py, pc = (1 - x if fx else x), (1 - y if fy else y), (1 - c if fc else c)
        sems = dict(send_sem=send_sems.at[q], recv_sem=recv_sems.at[q], device_id=(px, py, pc), device_id_type=MESH)
        out.append((pltpu.make_async_remote_copy(src_ref=src_ref, dst_ref=land_ref.at[4 * x + 2 * y + c], **sems),
                    pltpu.make_async_remote_copy(src_ref=src_ref, dst_ref=land_ref.at[4 * px + 2 * py + pc], **sems)))
    return out


def _everyone_start(block, after, name):
    land = lax.empty((N_DEV, *block.shape), block.dtype)

    def body(src_ref, land_ref, after_ref, send_sems, recv_sems, src_thru, land_thru, token):
        for sent, _ in _everyone_copies(src_ref, land_ref, send_sems, recv_sems):
            sent.start()
        token[...] = jnp.zeros_like(token)

    hbm = lambda a: pltpu.HBM(a.shape, a.dtype)
    n_sem = N_DEV - 1
    res = pl.pallas_call(
        body, name=name, in_specs=[_HBM, _HBM, _ANY],
        out_specs=[_SEM, _SEM, _HBM, _HBM, pl.BlockSpec(memory_space=pltpu.VMEM)],
        out_shape=[pltpu.SemaphoreType.DMA((n_sem,)), pltpu.SemaphoreType.DMA((n_sem,)), hbm(block), hbm(land),
                   jax.ShapeDtypeStruct((8, 128), F32)],
        input_output_aliases={0: 2, 1: 3}, compiler_params=pltpu.CompilerParams(has_side_effects=_DATAFLOW),
    )(pltpu.with_memory_space_constraint(block, pltpu.HBM), pltpu.with_memory_space_constraint(land, pltpu.HBM), after)
    return res[:4]


def _everyone_wait(handle, after, name):
    send_sems, recv_sems, block, land = handle

    def body(src_ref, land_ref, send_sems, recv_sems, after_ref, src_thru, land_thru):
        for sent, landed in _everyone_copies(src_ref, land_ref, send_sems, recv_sems):
            sent.wait_send()
            landed.wait_recv()

    hbm = lambda a: pltpu.HBM(a.shape, a.dtype)
    return pl.pallas_call(
        body, name=name, in_specs=[_HBM, _HBM, _SEM, _SEM, _ANY], out_specs=[_HBM, _HBM], out_shape=[hbm(block), hbm(land)],
        input_output_aliases={0: 0, 1: 1}, compiler_params=pltpu.CompilerParams(has_side_effects=_DATAFLOW),
    )(block, land, send_sems, recv_sems, after)


def _sum_devices(blocks, name):
    n_dev, m, lanes = blocks.shape
    tm = _pick(m, (512, 256, 128, 64, 32, 16, 8))

    def body(b_ref, o_ref):
        total = b_ref[0]
        for d in range(1, n_dev):
            total = total + b_ref[d]
        o_ref[...] = total

    return pl.pallas_call(
        body, name=name, grid=(m // tm,), in_specs=[pl.BlockSpec((n_dev, tm, lanes), lambda i: (0, i, 0))],
        out_specs=pl.BlockSpec((tm, lanes), lambda i: (i, 0)), out_shape=jax.ShapeDtypeStruct((m, lanes), F32),
        compiler_params=_cparams(("parallel",)),
    )(blocks)


def _allgather_sum_small(block, name):
    m_per, n = block.shape

    def body(x_ref, out_ref, sum_ref, send_sems, recv_sems, local_sem):
        x, y, c = _mesh_pos()
        me, sibling = (x, y, c), (x, y, 1 - c)
        chips = _other_chips(x, y)

        def rows(px, py, pc):
            return out_ref.at[pl.ds((4 * px + 2 * py + pc) * m_per, m_per), :]

        def copy(k, blk, to, src=None):
            return pltpu.make_async_remote_copy(src_ref=rows(*blk) if src is None else src, dst_ref=rows(*blk),
                                                send_sem=send_sems.at[k], recv_sem=recv_sems.at[k], device_id=to, device_id_type=MESH)

        mine = pltpu.make_async_copy(x_ref, rows(*me), local_sem)
        mine.start()
        first = [copy(0, me, sibling, src=x_ref)]
        first += [copy(1 + j, me, (*chip, c), src=x_ref) for j, chip in enumerate(chips)]
        for cp in first:
            cp.start()
        passed = [copy(4 + j, (*chip, c), sibling) for j, chip in enumerate(chips)]
        for j, chip in enumerate(chips):
            copy(1 + j, (*chip, c), me).wait_recv()
            passed[j].start()
        copy(0, sibling, me).wait_recv()
        for j, chip in enumerate(chips):
            copy(4 + j, (*chip, 1 - c), me).wait_recv()
        for cp in first + passed:
            cp.wait_send()
        mine.wait()
        total = out_ref[pl.ds(0, m_per), :]
        for d in range(1, N_DEV):
            total = total + out_ref[pl.ds(d * m_per, m_per), :]
        sum_ref[...] = total

    vmem = pl.BlockSpec(memory_space=pltpu.VMEM)
    return pl.pallas_call(
        body, name=name, in_specs=[vmem], out_specs=[vmem, vmem],
        out_shape=[jax.ShapeDtypeStruct((N_DEV * m_per, n), F32), jax.ShapeDtypeStruct((m_per, n), F32)],
        scratch_shapes=[pltpu.SemaphoreType.DMA((7,)), pltpu.SemaphoreType.DMA((7,)), pltpu.SemaphoreType.DMA],
        compiler_params=pltpu.CompilerParams(vmem_limit_bytes=V7X_VMEM_LIMIT),
    )(block)


WEIGHTS = ("rel_bias", "norm1_g", "w_in", "sgu_w", "sgu_b", "dil_qn_g", "dil_kn_g", "conv_w", "conv_b", "conv_ln_g", "conv_ln_b",
           "gqa_qn_g", "gqa_kn_g", "mix_norm_g", "w_out", "norm2_g", "w_gate", "w_up", "w_down")
SHARDED = ("w_in", "w_out", "w_gate", "w_up", "w_down")
COLUMN_SHARDED = ("w_in", "w_gate", "w_up")
REPLICATED = tuple(k for k in WEIGHTS if k not in SHARDED and k != "conv_w")


PACK_ROWS = 256


def _pack(parts):
    flat = jnp.concatenate([p.reshape(-1) for p in parts])
    pad = (-flat.shape[0]) % (PACK_ROWS * LANES)
    return jnp.pad(flat, (0, pad)).reshape(-1, LANES)


def _unpack(buf, shapes):
    flat = buf.reshape(-1)
    out, at = [], 0
    for s in shapes:
        size = math.prod(s)
        out.append(flat[at:at + size].reshape(s))
        at += size
    return out


RS_TM = (256, 128, 64, 32, 16)


def _add_halves(g, sib, c, name):
    slabs, h, cols = sib.shape
    tm = _pick(h, RS_TM)
    nb = h // tm

    def body(c_ref, g_ref, s_ref, o_ref):
        o_ref[...] = (g_ref[...].astype(F32) + s_ref[...].astype(F32)).astype(BF16)

    blk = pl.BlockSpec((None, tm, cols), lambda j, i, c_ref: (j, i, 0))
    grid_spec = pltpu.PrefetchScalarGridSpec(
        num_scalar_prefetch=1, grid=(slabs, nb),
        in_specs=[pl.BlockSpec((None, tm, cols), lambda j, i, c_ref: (j, c_ref[0] * nb + i, 0)), blk], out_specs=blk)
    return pl.pallas_call(body, name=name, grid_spec=grid_spec, out_shape=jax.ShapeDtypeStruct(sib.shape, BF16),
                          compiler_params=_cparams(("parallel", "parallel")))(c, g, sib)


def _add_own_three(own, land, chip, name):
    _, h, cols = own.shape
    tm = _pick(h, RS_TM)

    def body(chip_ref, own_ref, l0_ref, l1_ref, l2_ref, o_ref):
        o_ref[...] = ((own_ref[...].astype(F32) + l0_ref[...].astype(F32)) + l1_ref[...].astype(F32)) + l2_ref[...].astype(F32)

    slot = lambda r: pl.BlockSpec((None, tm, cols), functools.partial(lambda i, chip_ref, r: (jnp.bitwise_xor(chip_ref[0], r), i, 0), r=r))
    grid_spec = pltpu.PrefetchScalarGridSpec(
        num_scalar_prefetch=1, grid=(h // tm,),
        in_specs=[slot(0), slot(1), slot(2), slot(3)],
        out_specs=pl.BlockSpec((tm, cols), lambda i, chip_ref: (i, 0)))
    return pl.pallas_call(body, name=name, grid_spec=grid_spec, out_shape=jax.ShapeDtypeStruct((h, cols), F32),
                          compiler_params=_cparams(("parallel",)))(chip, own, land, land, land)


def _reduce_start(grads, c1, after, tag):
    from_sibling = _swap_other_half(grads, tag + "pair")
    sums = [_add_halves(g, s, c1, f"{tag}add2_{k}") for k, (g, s) in enumerate(zip(grads, from_sibling))]
    handle, token = _chips_start(sums, True, after, tag + "start")
    return (handle, sums), token


def _reduce_finish(started, chip1, after, tag):
    handle, sums = started
    lands = _chips_wait(handle, True, after, tag + "wait")
    totals = [_add_own_three(s, land, chip1, f"{tag}add4_{k}") for k, (s, land) in enumerate(zip(sums, lands))]
    return list(zip(totals, _swap_sibling(totals, tag + "share")))


def _adamw_shard(w, m, v, halves, c1, name):
    depth, rows, cols = w.shape
    h = rows // 2
    tm = _pick(h, RS_TM)
    nb = h // tm
    sources = [a for pair in halves for a in pair]

    def body(c_ref, w_ref, m_ref, v_ref, *refs):
        g_refs, (g_out, d_out, m_out, v_out) = refs[:2 * depth], refs[2 * depth:]
        layer, mine = pl.program_id(0), pl.program_id(1) == c_ref[0]
        g = None
        for l in range(depth):
            g_l = jnp.where(mine, g_refs[2 * l][...], g_refs[2 * l + 1][...])
            g = g_l if g is None else jnp.where(layer == l, g_l, g)
        delta, m_new, v_new = _adamw_fn(w_ref[...], g, m_ref[...], v_ref[...])
        g_out[...], d_out[...], m_out[...], v_out[...] = g, delta, m_new, v_new

    def source_spec(l, own):
        def index(layer, half, i, c_ref):
            return (jnp.where((layer == l) & ((half == c_ref[0]) == own), i, 0), 0)
        return pl.BlockSpec((tm, cols), index)

    blk = pl.BlockSpec((None, tm, cols), lambda layer, half, i, c_ref: (layer, half * nb + i, 0))
    grid_spec = pltpu.PrefetchScalarGridSpec(
        num_scalar_prefetch=1, grid=(depth, 2, nb),
        in_specs=[blk, blk, blk] + [source_spec(l, own) for l in range(depth) for own in (True, False)], out_specs=[blk] * 4)
    return pl.pallas_call(body, name=name, grid_spec=grid_spec, out_shape=[jax.ShapeDtypeStruct(w.shape, F32)] * 4,
                          compiler_params=_cparams(("arbitrary", "arbitrary", "arbitrary")))(c1, w, m, v, *sources)


GATHER_GROUPS = ((("w_in",), 0), (("w_out",), None), (("w_gate", "w_up"), None), (("w_down",), None), (("w_in",), 1))
REDUCE_GROUPS = (("w_down", "w_gate", "w_up"), ("w_out",), ("w_in",))


def kernel(x, rel_bias, norm1_g, w_in, sgu_w, sgu_b, dil_qn_g, dil_kn_g, conv_w, conv_b, conv_ln_g, conv_ln_b, gqa_qn_g, gqa_kn_g, mix_norm_g, w_out, norm2_g, w_gate, w_up, w_down, loss_target, m_rel_bias, m_norm1_g, m_w_in, m_sgu_w, m_sgu_b, m_dil_qn_g, m_dil_kn_g, m_conv_w, m_conv_b, m_conv_ln_g, m_conv_ln_b, m_gqa_qn_g, m_gqa_kn_g, m_mix_norm_g, m_w_out, m_norm2_g, m_w_gate, m_w_up, m_w_down, v_rel_bias, v_norm1_g, v_w_in, v_sgu_w, v_sgu_b, v_dil_qn_g, v_dil_kn_g, v_conv_w, v_conv_b, v_conv_ln_g, v_conv_ln_b, v_gqa_qn_g, v_gqa_kn_g, v_mix_norm_g, v_w_out, v_norm2_g, v_w_gate, v_w_up, v_w_down):
    w = dict(rel_bias=rel_bias, norm1_g=norm1_g, w_in=w_in, sgu_w=sgu_w, sgu_b=sgu_b, dil_qn_g=dil_qn_g, dil_kn_g=dil_kn_g,
             conv_w=conv_w, conv_b=conv_b, conv_ln_g=conv_ln_g, conv_ln_b=conv_ln_b, gqa_qn_g=gqa_qn_g, gqa_kn_g=gqa_kn_g,
             mix_norm_g=mix_norm_g, w_out=w_out, norm2_g=norm2_g, w_gate=w_gate, w_up=w_up, w_down=w_down)
    m = dict(rel_bias=m_rel_bias, norm1_g=m_norm1_g, w_in=m_w_in, sgu_w=m_sgu_w, sgu_b=m_sgu_b, dil_qn_g=m_dil_qn_g,
             dil_kn_g=m_dil_kn_g, conv_w=m_conv_w, conv_b=m_conv_b, conv_ln_g=m_conv_ln_g, conv_ln_b=m_conv_ln_b,
             gqa_qn_g=m_gqa_qn_g, gqa_kn_g=m_gqa_kn_g, mix_norm_g=m_mix_norm_g, w_out=m_w_out, norm2_g=m_norm2_g,
             w_gate=m_w_gate, w_up=m_w_up, w_down=m_w_down)
    v = dict(rel_bias=v_rel_bias, norm1_g=v_norm1_g, w_in=v_w_in, sgu_w=v_sgu_w, sgu_b=v_sgu_b, dil_qn_g=v_dil_qn_g,
             dil_kn_g=v_dil_kn_g, conv_w=v_conv_w, conv_b=v_conv_b, conv_ln_g=v_conv_ln_g, conv_ln_b=v_conv_ln_b,
             gqa_qn_g=v_gqa_qn_g, gqa_kn_g=v_gqa_kn_g, mix_norm_g=v_mix_norm_g, w_out=v_w_out, norm2_g=v_norm2_g,
             w_gate=v_w_gate, w_up=v_w_up, w_down=v_w_down)
    bsz = x.shape[0]
    t = bsz * SEQ
    xi, yi, ci = _mesh_pos()
    chip = 2 * xi + yi
    conv_cols = conv_w.shape[-1]

    conv_rows = DEPTH * CONV_WIDTH
    conv_block = jnp.pad(conv_w.reshape(conv_rows, conv_cols), ((0, (-conv_rows) % 8), (0, 0)))
    every, _ = _allgather_sum_small(conv_block, "conv_w_gather")
    every = every.reshape(N_DEV, conv_block.shape[0], conv_cols)
    conv_w_full = jnp.concatenate([every[2 * j, :conv_rows].reshape(DEPTH, CONV_WIDTH, conv_cols) for j in range(N_CHIPS)], axis=-1)

    c1 = jnp.reshape(ci, (1,)).astype(jnp.int32)
    chip1 = jnp.reshape(chip, (1,)).astype(jnp.int32)

    def own_part(k, layer):
        if layer is None:
            return lax.dynamic_index_in_dim(w[k], ci, axis=0, keepdims=False).astype(BF16)
        half = w[k].shape[1] // 2
        return lax.dynamic_slice_in_dim(w[k][layer], ci * half, half, axis=0).astype(BF16)

    fetches, token = [], every
    for gi, (group, layer) in enumerate(GATHER_GROUPS):
        parts = [own_part(k, layer) for k in group]
        handle, token = _chips_start(parts, False, token, f"gather{gi}_start", per_core=True)
        fetches.append((handle, parts))
    all_started = token
    gathered = {}

    def big(l, name, after):
        if (l, name) not in gathered:
            gi = [name in group and layer in (None, l) for group, layer in GATHER_GROUPS].index(True)
            (group, layer), (handle, parts) = GATHER_GROUPS[gi], fetches[gi]
            lands = _chips_wait(handle, False, all_started if after is None else after, f"gather{gi}_wait")
            mine = [lax.dynamic_update_slice(land, own[None, None], (ci, chip, 0, 0)) for land, own in zip(lands, parts)]
            for k, g in zip(group, _complete_pairs(mine, f"gather{gi}_share")):
                rows, cols = g.shape[2:]
                if layer is not None:
                    gathered[layer, k] = g.transpose(0, 2, 1, 3).reshape(2 * rows, N_CHIPS * cols)
                    continue
                for each in range(DEPTH):
                    gathered[each, k] = (g[each].transpose(1, 0, 2).reshape(rows, N_CHIPS * cols) if k in COLUMN_SHARDED
                                         else g[each].reshape(N_CHIPS * rows, cols))
        return gathered[l, name]

    big(0, "w_in", None)

    pending, started, reduced = {}, {}, {}

    def emit(l, name, g):
        pending[l, name] = g
        for gi, group in enumerate(REDUCE_GROUPS):
            if name in group and all((l, k) in pending for k in group):
                started[l, gi], token = _reduce_start([pending[l, k] for k in group], c1, g, f"l{l}_reduce{gi}_")
                return token
        return None

    def finish(l, after):
        for gi, group in enumerate(REDUCE_GROUPS):
            for k, r in zip(group, _reduce_finish(started[l, gi], chip1, after, f"l{l}_reduce{gi}_")):
                reduced[l, k] = r
            after = reduced[l, group[0]][1]

    def mid_hook(l, a):
        if l + 1 < DEPTH:
            finish(l + 1, a)

    small = {k: w[k] for k in REPLICATED}
    small["conv_w"] = conv_w_full
    loss, dx, small_grads = _local_step(x.reshape(t, D_MODEL), loss_target.reshape(t, D_MODEL), small, big, emit, mid_hook, bsz)
    loss = lax.psum(loss[0, 0], ("x", "y", "c"))

    names = REPLICATED + ("conv_w",)
    shapes = [small_grads[k].shape for k in names]
    small_exchange = _everyone_start(_pack([small_grads[k] for k in names]), dx, "small_grads_start")
    finish(0, dx)

    grads, deltas, new_m, new_v = {}, {}, {}, {}
    for k in SHARDED:
        grads[k], deltas[k], new_m[k], new_v[k] = _adamw_shard(w[k], m[k], v[k], [reduced[l, k] for l in range(DEPTH)], c1, "adamw_" + k)

    own, others = _everyone_wait(small_exchange, new_v[SHARDED[-1]], "small_grads_wait")
    summed = _sum_devices(lax.dynamic_update_slice_in_dim(others, own[None], 4 * xi + 2 * yi + ci, axis=0), "small_grads_sum")
    summed_parts = dict(zip(names, _unpack(summed, shapes)))
    rep_shapes = [w[k].shape for k in REPLICATED]
    packed = [_pack([src[k] for k in REPLICATED]) for src in (w, {k: summed_parts[k] for k in REPLICATED}, m, v)]
    d_p, m_p, v_p = _adamw(*packed, "adamw_replicated")
    for k, gk, dk, mk, vk in zip(REPLICATED, _unpack(packed[1], rep_shapes), _unpack(d_p, rep_shapes), _unpack(m_p, rep_shapes),
                                 _unpack(v_p, rep_shapes)):
        grads[k], deltas[k], new_m[k], new_v[k] = gk, dk, mk, vk
    g_conv = lax.dynamic_slice_in_dim(summed_parts["conv_w"], chip * conv_cols, conv_cols, axis=2)
    packed = [_pack([a]) for a in (conv_w, g_conv, m["conv_w"], v["conv_w"])]
    d_p, m_p, v_p = _adamw(*packed, "adamw_conv_w")
    grads["conv_w"] = g_conv
    deltas["conv_w"], new_m["conv_w"], new_v["conv_w"] = (_unpack(a, [conv_w.shape])[0] for a in (d_p, m_p, v_p))

    return (loss, dx.reshape(x.shape), *[grads[k] for k in WEIGHTS], *[deltas[k] for k in WEIGHTS],
            *[new_m[k] for k in WEIGHTS], *[new_v[k] for k in WEIGHTS])
```

```python
import functools
import math

import numpy as np
import jax
import jax.numpy as jnp
from jax import lax
from jax.experimental import pallas as pl
from jax.experimental.pallas import tpu as pltpu

F32 = jnp.float32
BF16 = jnp.bfloat16

D_MODEL = 2048
SEQ = 2048
DEPTH = 2
HEAD_DIM = 64
GROUP_WIDTH = 512
N_HEADS = 8
SGU_CHUNK = 128
DIL_PATTERNS = ((128, 1), (512, 4), (2048, 16))
DIL_HALF = 64
CONV_WIDTH = 31
KV_WIDTH = 128
GRID_W = 64
ROPE_THETA = 10000.0
REL_BUCKETS = 32
REL_MAX_DIST = 1024
FFN_HIDDEN = 5632
IN_WIDTH = 4352
RMS_EPS = 1e-6
LN_EPS = 1e-5
ADAM_LR = 0.001
ADAM_B1 = 0.9
ADAM_B2 = 0.999
ADAM_EPS = 1e-08
ADAM_WD = 0.01
ADAM_STEP = 10
N_CHIPS = 4

V7X_VMEM_LIMIT = 56 * 1024 * 1024
MATMUL_VMEM_BUDGET = 48 * 1024 * 1024
LANES = 128
HI = lax.Precision.HIGHEST
SPLIT3 = lax.Precision.HIGH
MESH = pl.DeviceIdType.MESH


def _cparams(sem=None):
    return pltpu.CompilerParams(dimension_semantics=sem, vmem_limit_bytes=V7X_VMEM_LIMIT)


def _pick(n, cands):
    for c in cands:
        if n % c == 0:
            return c
    raise ValueError(f"no tile for {n}")


_DIMS = {"nn": (((1,), (0,)), ((), ())), "nt": (((1,), (1,)), ((), ())), "tn": (((0,), (0,)), ((), ()))}


def _matmul(pairs, mode, out_dtype, name, residual=None, slabs=1):
    a0, b0 = pairs[0]
    b3 = b0.ndim == 3
    if mode == "nn":
        (M, K), N = a0.shape, b0.shape[1]
    elif mode == "nt":
        (M, K), N = a0.shape, b0.shape[0]
    else:
        (K, M) = a0.shape
        N = b0.shape[-1] if b3 else b0.shape[1] // slabs
    npairs = len(pairs)
    a_bytes, b_bytes, o_bytes = a0.dtype.itemsize, b0.dtype.itemsize, jnp.dtype(out_dtype).itemsize
    per_out = 4 + 2 * o_bytes + (8 if residual is not None else 0)
    tn_cands = [c for c in ((1024, 512) if K <= 2048 else (512,)) + (1408, 2176, 256) if N % c == 0] + [N]
    tm, tn = next((tm, tn) for tn in tn_cands for tm in (1024, 1408, 512, 256)
                  if M % tm == 0 and 2 * npairs * K * (tm * a_bytes + tn * b_bytes) + tm * tn * per_out <= MATMUL_VMEM_BUDGET)
    tk = K
    ni, nj = M // tm, N // tn
    j_outer = nj * M * a_bytes + N * b_bytes < M * a_bytes + ni * N * b_bytes
    grid = (slabs, nj, ni) if j_outer else (slabs, ni, nj)
    at = lambda f: (lambda s, g1, g2: f(s, g2, g1)) if j_outer else f

    if mode in ("nn", "nt"):
        a_spec = pl.BlockSpec((tm, tk), at(lambda s, i, j: (i, 0)))
    else:
        a_spec = pl.BlockSpec((tk, tm), at(lambda s, i, j: (0, i)))
    if mode == "nt":
        b_spec = pl.BlockSpec((tn, tk), at(lambda s, i, j: (j, 0)))
    elif b3:
        b_spec = pl.BlockSpec((None, tk, tn), at(lambda s, i, j: (s, 0, j)))
    else:
        b_spec = pl.BlockSpec((tk, tn), at(lambda s, i, j: (0, s * nj + j)))
    if slabs > 1:
        o_spec = pl.BlockSpec((None, tm, tn), at(lambda s, i, j: (s, i, j)))
        o_shape = jax.ShapeDtypeStruct((slabs, M, N), out_dtype)
    else:
        o_spec = pl.BlockSpec((tm, tn), at(lambda s, i, j: (i, j)))
        o_shape = jax.ShapeDtypeStruct((M, N), out_dtype)
    in_specs = [a_spec] * npairs + [b_spec] * npairs
    args = [a for a, _ in pairs] + [b for _, b in pairs]
    if residual is not None:
        in_specs.append(pl.BlockSpec((tm, tn), at(lambda s, i, j: (i, j))))
        args.append(residual)
    dims = _DIMS[mode]

    def body(*refs):
        a_refs, b_refs = refs[:npairs], refs[npairs:2 * npairs]
        res_ref = refs[2 * npairs] if residual is not None else None
        o_ref = refs[-1]
        r = None
        for a_ref, b_ref in zip(a_refs, b_refs):
            d = lax.dot_general(a_ref[...].astype(BF16), b_ref[...].astype(BF16), dims, preferred_element_type=F32)
            r = d if r is None else r + d
        if res_ref is not None:
            r = r + res_ref[...]
        o_ref[...] = r.astype(out_dtype)

    return pl.pallas_call(
        body, name=name, grid=grid, in_specs=in_specs, out_specs=o_spec, out_shape=o_shape,
        compiler_params=_cparams(("parallel", "parallel", "parallel")),
    )(*args)


class Strided:
    def __init__(self, r):
        self.r = r


def _rowmap(fn, rows, fulls, row_outs, acc_outs, name, tm, n_rows):
    nr, nf, nro = len(rows), len(fulls), len(row_outs)
    rows = [r if len(r) == 4 else (*r, n_rows // tm) for r in rows]
    row_outs = [o if len(o) == 3 else (*o, None) for o in row_outs]
    in_specs = [pl.BlockSpec((tm // per.r, per.r * w), lambda i: (i, 0)) if isinstance(per, Strided) else
                pl.BlockSpec((tm, w), functools.partial(lambda i, cb, per: (i % per, cb), cb=cb, per=per)) for _, w, cb, per in rows]
    in_specs += [pl.BlockSpec(f.shape, lambda i: (0,) * f.ndim) for f in fulls]
    out_specs = [pl.BlockSpec((tm, w) if st is None else (tm // st.r, st.r * w), lambda i: (i, 0)) for w, _, st in row_outs]
    out_specs += [pl.BlockSpec(s, functools.partial(lambda i, n: (0,) * n, n=len(s))) for s in acc_outs]
    out_shape = [jax.ShapeDtypeStruct((n_rows, w) if st is None else (n_rows // st.r, st.r * w), dt) for w, dt, st in row_outs]
    out_shape += [jax.ShapeDtypeStruct(s, F32) for s in acc_outs]
    strided = [(k, w, per.r) for k, (_, w, _, per) in enumerate(rows) if isinstance(per, Strided)]
    strided += [(nr + nf + k, w, st.r) for k, (w, _, st) in enumerate(row_outs) if st is not None]
    n_scratch = len(strided)

    def body(*refs):
        refs, scratch = refs[:len(refs) - n_scratch], dict(zip([k for k, _, _ in strided], refs[len(refs) - n_scratch:]))
        ins = []
        for k, ref in enumerate(refs[:nr + nf]):
            if k in scratch:
                w, r, scr = rows[k][1], rows[k][3].r, scratch[k]
                for rho in range(r):
                    for j in range(w // LANES):
                        scr.at[j][pl.ds(rho, tm // r, stride=r), :] = ref[:, pl.ds(rho * w + j * LANES, LANES)]
                ins.append(jnp.concatenate([scr[j] for j in range(w // LANES)], axis=1))
            else:
                ins.append(ref[...])
        outs = fn(*ins)
        o_refs = refs[nr + nf:]
        for k, (o_ref, val) in enumerate(zip(o_refs[:nro], outs[:nro])):
            if nr + nf + k in scratch:
                w, r, scr = row_outs[k][0], row_outs[k][2].r, scratch[nr + nf + k]
                val = val.astype(F32)
                for j in range(w // LANES):
                    scr[j] = val[:, j * LANES:(j + 1) * LANES]
                for rho in range(r):
                    for j in range(w // LANES):
                        o_ref[:, pl.ds(rho * w + j * LANES, LANES)] = scr.at[j][pl.ds(rho, tm // r, stride=r), :].astype(o_ref.dtype)
            else:
                o_ref[...] = val.astype(o_ref.dtype)
        if acc_outs:
            first = pl.program_id(0) == 0
            for o_ref, val in zip(o_refs[nro:], outs[nro:]):
                @pl.when(first)
                def _(o_ref=o_ref, val=val):
                    o_ref[...] = val

                @pl.when(jnp.logical_not(first))
                def _(o_ref=o_ref, val=val):
                    o_ref[...] += val

    res = pl.pallas_call(
        body, name=name, grid=(n_rows // tm,), in_specs=in_specs, out_specs=out_specs, out_shape=out_shape,
        scratch_shapes=[pltpu.VMEM((w // LANES, tm, LANES), F32) for _, w, _ in strided],
        compiler_params=_cparams(("arbitrary",) if acc_outs else ("parallel",)),
    )(*[r[0] for r in rows], *fulls)
    return res


def _rms(x, g):
    return x * lax.rsqrt(jnp.mean(x * x, axis=-1, keepdims=True) + RMS_EPS) * g


def _rmsnorm_fwd(x, g, name):
    t = x.shape[0]
    return _rowmap(lambda xv, gv: (_rms(xv, gv),), [(x, D_MODEL, 0)], [g], [(D_MODEL, BF16)], [], name, 512, t)[0]


def _rmsnorm_bwd(dh, x, g, dres, name, follow=None):
    t = x.shape[0]

    def fn(dhv, xv, drv, gv, *_):
        _, vjp = jax.vjp(_rms, xv, gv)
        dx, dg = vjp(dhv)
        return dx + drv, dx + drv, dg

    fulls = [g] if follow is None else [g, follow]
    return _rowmap(fn, [(dh, D_MODEL, 0), (x, D_MODEL, 0), (dres, D_MODEL, 0)], fulls, [(D_MODEL, F32), (D_MODEL, BF16)],
                   [(1, D_MODEL)], name, 256, t)


def _loss_fwd_bwd(y, target, name):
    t = y.shape[0]

    def fn(yv, tv):
        e = yv - tv
        return e * (1.0 / D_MODEL), e * (1.0 / D_MODEL), (0.5 / D_MODEL) * jnp.sum(e * e, keepdims=True)

    return _rowmap(fn, [(y, D_MODEL, 0), (target, D_MODEL, 0)], [], [(D_MODEL, F32), (D_MODEL, BF16)], [(1, 1)], name, 512, t)


FFN_TILE = (1024, 512)


def _ffn_up(h, wg, wu, name):
    t, n = h.shape[0], wg.shape[1]
    tm, tn = FFN_TILE

    def body(h_ref, wg_ref, wu_ref, act_ref, du_ref, dg_ref):
        hv = h_ref[...]
        g = jnp.dot(hv, wg_ref[...], preferred_element_type=F32)
        u = jnp.dot(hv, wu_ref[...], preferred_element_type=F32)
        sg = jax.nn.sigmoid(g)
        silu = g * sg
        act_ref[...] = (silu * u).astype(BF16)
        du_ref[...] = silu.astype(BF16)
        dg_ref[...] = (u * (sg + silu * (1.0 - sg))).astype(BF16)

    o_spec = pl.BlockSpec((tm, tn), lambda i, j: (i, j))
    o_shape = jax.ShapeDtypeStruct((t, n), BF16)
    return pl.pallas_call(
        body, name=name, grid=(t // tm, n // tn),
        in_specs=[pl.BlockSpec((tm, D_MODEL), lambda i, j: (i, 0)), pl.BlockSpec((D_MODEL, tn), lambda i, j: (0, j)),
                  pl.BlockSpec((D_MODEL, tn), lambda i, j: (0, j))],
        out_specs=[o_spec] * 3, out_shape=[o_shape] * 3, compiler_params=_cparams(("parallel", "parallel")),
    )(h, wg, wu)


def _ffn_down_bwd(dy, wd, act_du, act_dg, name):
    t, n = dy.shape[0], wd.shape[0]
    tm, tn = FFN_TILE

    def body(dy_ref, wd_ref, adu_ref, adg_ref, dg_ref, du_ref):
        dact = lax.dot_general(dy_ref[...].astype(BF16), wd_ref[...], _DIMS["nt"], preferred_element_type=F32)
        du_ref[...] = (dact * adu_ref[...].astype(F32)).astype(BF16)
        dg_ref[...] = (dact * adg_ref[...].astype(F32)).astype(BF16)

    o_spec = pl.BlockSpec((tm, tn), lambda i, j: (i, j))
    o_shape = jax.ShapeDtypeStruct((t, n), BF16)
    return pl.pallas_call(
        body, name=name, grid=(t // tm, n // tn),
        in_specs=[pl.BlockSpec((tm, D_MODEL), lambda i, j: (i, 0)), pl.BlockSpec((tn, D_MODEL), lambda i, j: (j, 0)),
                  o_spec, o_spec],
        out_specs=[o_spec] * 2, out_shape=[o_shape] * 2, compiler_params=_cparams(("parallel", "parallel")),
    )(dy, wd, act_du, act_dg)


def _np_group_avg(width, group=HEAD_DIM):
    i = np.arange(width)
    return ((i[:, None] // group) == (i[None, :] // group)).astype(np.float32) / group


def _np_tile_fold(width, group=HEAD_DIM):
    return ((np.arange(width)[:, None] % group) == np.arange(group)[None, :]).astype(np.float32)


def _np_group_fold(width, group=HEAD_DIM, pad=128):
    return ((np.arange(width)[:, None] // group) == np.arange(pad)[None, :]).astype(np.float32)


def _np_rope_partner(width):
    i = np.arange(width)
    partner = np.where(i % 32 < 16, i + 16, i - 16)
    return (partner[:, None] == i[None, :]).astype(np.float32)


def _np_kv_expand():
    src = np.arange(KV_WIDTH)
    dst = np.arange(GROUP_WIDTH)
    return ((src[:, None] // HEAD_DIM == dst[None, :] // (4 * HEAD_DIM)) & (src[:, None] % HEAD_DIM == dst[None, :] % HEAD_DIM)).astype(np.float32)


def _np_rope_tables(n_heads):
    t = np.arange(SEQ)
    pos = {0: (t // GRID_W).astype(np.float32), 1: (t % GRID_W).astype(np.float32)}
    freqs = (ROPE_THETA ** (-np.arange(16, dtype=np.float32) / 16)).astype(np.float32)
    cos_parts, sin_parts = [], []
    for axis in (0, 1):
        ang = pos[axis][:, None] * freqs[None, :]
        c, s = np.cos(ang).astype(np.float32), np.sin(ang).astype(np.float32)
        cos_parts += [c, c]
        sin_parts += [-s, s]
    cos = np.concatenate(cos_parts, axis=1)
    sin = np.concatenate(sin_parts, axis=1)
    return np.tile(cos, (1, n_heads)), np.tile(sin, (1, n_heads))


def _np_t5_buckets(rel):
    nb = REL_BUCKETS // 2
    max_exact = nb // 2
    ret = np.where(rel > 0, nb, 0)
    n = np.abs(rel)
    nf = np.maximum(n, 1).astype(np.float32)
    large = max_exact + (np.log(nf / max_exact) / math.log(REL_MAX_DIST / max_exact) * (nb - max_exact)).astype(np.int32)
    large = np.minimum(large, nb - 1)
    return (ret + np.where(n < max_exact, n, large)).astype(np.int32)


DIL_QB = 128
DIL_WIN = DIL_QB + 2 * DIL_HALF


def _np_dil_buckets(dil):
    off = np.arange(DIL_WIN)[None, :] - DIL_HALF - np.arange(DIL_QB)[:, None]
    return _np_t5_buckets(off * dil)


def _dil_live_buckets(dil):
    off = np.arange(-DIL_HALF, DIL_HALF + 1)
    return sorted(set(_np_t5_buckets(off * dil).tolist()))


def _head_stat(x, mavg):
    return jnp.dot(x, mavg, precision=SPLIT3, preferred_element_type=F32)


def _gelu(x):
    return 0.5 * x * (1.0 + jnp.tanh(math.sqrt(2.0 / math.pi) * (x + 0.044715 * (x * x * x))))


def _sgu_pre(u_pre, v_pre, mavg):
    v = _gelu(v_pre)
    xc = v - _head_stat(v, mavg)
    vn = xc * lax.rsqrt(_head_stat(xc * xc, mavg) + LN_EPS)
    return _gelu(u_pre), vn


def _sgu_mix(w_ref, vnb, bm):
    lane_group = lax.broadcasted_iota(jnp.int32, (1, GROUP_WIDTH), 1) // HEAD_DIM
    mixed = bm
    for g in range(N_HEADS):
        r = jnp.dot(w_ref[g], vnb, preferred_element_type=F32)
        mixed = mixed + jnp.where(lane_group == g, r, 0.0)
    return mixed


SGU_TM = 512


def _sgu_fwd(z, w_s, bm, name):
    t = z.shape[0]
    mavg = jnp.asarray(_np_group_avg(GROUP_WIDTH))

    def body(u_ref, v_ref, w_ref, bm_ref, mavg_ref, y_ref):
        for c in range(SGU_TM // SGU_CHUNK):
            rows = pl.ds(c * SGU_CHUNK, SGU_CHUNK)
            u, vn = _sgu_pre(u_ref[rows, :], v_ref[rows, :], mavg_ref[...])
            y_ref[rows, :] = u * _sgu_mix(w_ref, vn.astype(BF16), bm_ref[...])

    full = lambda a: pl.BlockSpec(a.shape, lambda i: (0,) * a.ndim)
    return pl.pallas_call(
        body, name=name, grid=(t // SGU_TM,),
        in_specs=[pl.BlockSpec((SGU_TM, GROUP_WIDTH), lambda i: (i, 0)), pl.BlockSpec((SGU_TM, GROUP_WIDTH), lambda i: (i, 1)),
                  full(w_s), full(bm), full(mavg)],
        out_specs=pl.BlockSpec((SGU_TM, GROUP_WIDTH), lambda i: (i, 0)),
        out_shape=jax.ShapeDtypeStruct((t, GROUP_WIDTH), F32), compiler_params=_cparams(("parallel",)),
    )(z, z, w_s, bm, mavg)


def _sgu_bwd(z, dy, w_s, w_s_t, bm, name):
    t = z.shape[0]
    mavg = jnp.asarray(_np_group_avg(GROUP_WIDTH))
    gfold = jnp.asarray(_np_group_fold(GROUP_WIDTH))

    def body(u_ref, v_ref, dy_ref, w_ref, wt_ref, bm_ref, mavg_ref, gfold_ref, du_ref, dv_ref, dw_ref, dbs_ref, dbm_ref):
        @pl.when(pl.program_id(0) == 0)
        def _():
            dw_ref[...] = jnp.zeros_like(dw_ref)
            dbm_ref[...] = jnp.zeros_like(dbm_ref)

        lane_group = lax.broadcasted_iota(jnp.int32, (1, GROUP_WIDTH), 1) // HEAD_DIM
        for c in range(SGU_TM // SGU_CHUNK):
            rows = pl.ds(c * SGU_CHUNK, SGU_CHUNK)
            (u, vn), pre_vjp = jax.vjp(functools.partial(_sgu_pre, mavg=mavg_ref[...]), u_ref[rows, :], v_ref[rows, :])
            vnb = vn.astype(BF16)
            mixed = _sgu_mix(w_ref, vnb, bm_ref[...])
            dyv = dy_ref[rows, :]
            dmixed = dyv * u
            dbm_ref[...] += dmixed
            dvn = jnp.zeros_like(vn)
            for g in range(N_HEADS):
                dm_g = jnp.where(lane_group == g, dmixed, 0.0).astype(BF16)
                dw_ref[g] += lax.dot_general(dm_g, vnb, _DIMS["nt"], preferred_element_type=F32)
                dvn = dvn + jnp.dot(wt_ref[g], dm_g, preferred_element_type=F32)
            du_pre, dv_pre = pre_vjp((dyv * mixed, dvn))
            du_ref[rows, :] = du_pre
            dv_ref[rows, :] = dv_pre

        @pl.when(pl.program_id(0) == t // SGU_TM - 1)
        def _():
            dbs_ref[...] = jnp.dot(dbm_ref[...], gfold_ref[...], precision=HI, preferred_element_type=F32)

    full = lambda a: pl.BlockSpec(a.shape, lambda i: (0,) * a.ndim)
    row = pl.BlockSpec((SGU_TM, GROUP_WIDTH), lambda i: (i, 0))
    return pl.pallas_call(
        body, name=name, grid=(t // SGU_TM,),
        in_specs=[row, pl.BlockSpec((SGU_TM, GROUP_WIDTH), lambda i: (i, 1)), row, full(w_s), full(w_s_t), full(bm), full(mavg),
                  full(gfold)],
        out_specs=[row, row, pl.BlockSpec((N_HEADS, SGU_CHUNK, SGU_CHUNK), lambda i: (0, 0, 0)),
                   pl.BlockSpec((SGU_CHUNK, 128), lambda i: (0, 0))],
        out_shape=[jax.ShapeDtypeStruct((t, GROUP_WIDTH), F32)] * 2 + [jax.ShapeDtypeStruct((N_HEADS, SGU_CHUNK, SGU_CHUNK), F32),
                                                                      jax.ShapeDtypeStruct((SGU_CHUNK, 128), F32)],
        scratch_shapes=[pltpu.VMEM((SGU_CHUNK, GROUP_WIDTH), F32)],
        compiler_params=_cparams(("arbitrary",)),
    )(z, z, dy, w_s, w_s_t, bm, mavg, gfold)


def _head_lanes():
    lane_head = lax.broadcasted_iota(jnp.int32, (1, 2 * HEAD_DIM), 1) // HEAD_DIM
    return lane_head == 0, lane_head == 1


def _stack_heads(x2):
    h0, h1 = _head_lanes()
    zero = jnp.zeros_like(x2)
    return jnp.concatenate([jnp.where(h0, x2, zero), jnp.where(h1, x2, zero)], axis=0)


def _unstack_heads(y):
    r = y.shape[0] // 2
    h0, _ = _head_lanes()
    return jnp.where(h0, y[:r], y[r:])


def _pair_softmax(qs, k2, biases, valid):
    s = lax.dot_general(qs, k2, _DIMS["nt"], preferred_element_type=F32)
    if biases is not None:
        s = s + jnp.concatenate(biases, axis=0)
    if valid is not None:
        s = jnp.where(jnp.concatenate([valid, valid], axis=0), s, -1e30)
    m = jnp.max(s, axis=-1, keepdims=True)
    e = jnp.exp(s - m)
    l = jnp.sum(e, axis=-1, keepdims=True)
    return e / l, m + jnp.log(l)


def _attn_pair_fwd(q2, k2, v2, biases, valid):
    p, lse = _pair_softmax(_stack_heads(q2), k2, biases, valid)
    o = jnp.dot(p.astype(BF16), v2, preferred_element_type=F32)
    return _unstack_heads(o), _unstack_heads(jnp.broadcast_to(lse, o.shape))


def _attn_pair_bwd(q2, k2, v2, biases, valid, do2, dlse2):
    r = q2.shape[0]
    qs = _stack_heads(q2)
    p, _ = _pair_softmax(qs, k2, biases, valid)
    dos = _stack_heads(do2).astype(BF16)
    dp = lax.dot_general(dos, v2, _DIMS["nt"], preferred_element_type=F32)
    delta = jnp.sum(dp * p, axis=-1, keepdims=True)
    if dlse2 is not None:
        delta = delta - jnp.sum(_stack_heads(dlse2), axis=-1, keepdims=True)
    ds = p * (dp - delta)
    dsb = ds.astype(BF16)
    dq2 = _unstack_heads(jnp.dot(dsb, k2, preferred_element_type=F32))
    dk2 = lax.dot_general(dsb, qs, _DIMS["tn"], preferred_element_type=F32)
    dv2 = lax.dot_general(p.astype(BF16), dos, _DIMS["tn"], preferred_element_type=F32)
    return dq2, dk2, dv2, [ds[:r], ds[r:]]


def _dil_valid(r0, length):
    row = lax.broadcasted_iota(jnp.int32, (DIL_QB, DIL_WIN), 0)
    col = lax.broadcasted_iota(jnp.int32, (DIL_QB, DIL_WIN), 1)
    off = col - DIL_HALF - row
    kpos = r0 - DIL_HALF + col
    return (jnp.abs(off) <= DIL_HALF) & (kpos >= 0) & (kpos < length)


def _dil_build_bias(tab_ref, bkt_ref, bias_ref, dil):
    bkt = bkt_ref[...]
    for h in range(N_HEADS):
        acc = jnp.zeros((DIL_QB, DIL_WIN), F32)
        for b in _dil_live_buckets(dil):
            acc = jnp.where(bkt == b, tab_ref[b, h], acc)
        bias_ref[h] = acc


def _dil_fill_pad(pad_ref, src_ref, length):
    zeros = jnp.zeros((DIL_HALF, GROUP_WIDTH), pad_ref.dtype)
    pad_ref[pl.ds(0, DIL_HALF), :] = zeros
    pad_ref[pl.ds(DIL_HALF + length, DIL_HALF), :] = zeros
    pad_ref[pl.ds(DIL_HALF, length), :] = src_ref[...]


def _dil_specs(bsz, length, dil):
    view = lambda a: a.reshape(bsz, length, dil * GROUP_WIDTH)
    blk = pl.BlockSpec((None, DIL_QB, GROUP_WIDTH), lambda b, rho, i: (b, i, rho))
    seq = pl.BlockSpec((None, length, GROUP_WIDTH), lambda b, rho, i: (b, 0, rho))
    return view, blk, seq


def _dil_fwd(qb, kb, vb, table, dil, name):
    length = SEQ // dil
    bsz = qb.shape[0] // length
    bkt = jnp.asarray(_np_dil_buckets(dil))
    view, blk, seq = _dil_specs(bsz, length, dil)

    def body(tab_ref, bkt_ref, q_ref, k_ref, v_ref, o_ref, lse_ref, kpad, vpad, bias_ref):
        i = pl.program_id(2)

        @pl.when((pl.program_id(0) == 0) & (pl.program_id(1) == 0) & (i == 0))
        def _():
            _dil_build_bias(tab_ref, bkt_ref, bias_ref, dil)

        @pl.when(i == 0)
        def _():
            _dil_fill_pad(kpad, k_ref, length)
            _dil_fill_pad(vpad, v_ref, length)

        r0 = pl.multiple_of(i * DIL_QB, DIL_QB)
        valid = _dil_valid(r0, length)
        for m in range(N_HEADS // 2):
            lanes = pl.ds(m * 128, 128)
            o2, lse2 = _attn_pair_fwd(q_ref[:, lanes], kpad[pl.ds(r0, DIL_WIN), lanes], vpad[pl.ds(r0, DIL_WIN), lanes],
                                      (bias_ref[2 * m], bias_ref[2 * m + 1]), valid)
            o_ref[:, lanes] = o2
            lse_ref[:, lanes] = lse2

    out = jax.ShapeDtypeStruct((bsz, length, dil * GROUP_WIDTH), F32)
    o, lse = pl.pallas_call(
        body, name=name, grid=(bsz, dil, length // DIL_QB),
        in_specs=[pl.BlockSpec(memory_space=pltpu.SMEM), pl.BlockSpec(bkt.shape, lambda b, rho, i: (0, 0)), blk, seq, seq],
        out_specs=[blk, blk], out_shape=[out, out],
        scratch_shapes=[pltpu.VMEM((length + 2 * DIL_HALF, GROUP_WIDTH), BF16), pltpu.VMEM((length + 2 * DIL_HALF, GROUP_WIDTH), BF16),
                        pltpu.VMEM((N_HEADS, DIL_QB, DIL_WIN), F32)],
        compiler_params=_cparams(("arbitrary", "arbitrary", "arbitrary")),
    )(table, bkt, view(qb), view(kb), view(vb))
    return o.reshape(qb.shape), lse.reshape(qb.shape)


def _dil_bwd(qb, kb, vb, do, dlse, table, dil, name):
    length = SEQ // dil
    bsz = qb.shape[0] // length
    nqb = length // DIL_QB
    bkt = jnp.asarray(_np_dil_buckets(dil))
    view, blk, seq = _dil_specs(bsz, length, dil)

    def body(tab_ref, bkt_ref, q_ref, k_ref, v_ref, do_ref, dlse_ref, dq_ref, dk_ref, dv_ref, dsc_ref, kpad, vpad, bias_ref):
        i = pl.program_id(2)

        @pl.when((pl.program_id(0) == 0) & (pl.program_id(1) == 0) & (i == 0))
        def _():
            _dil_build_bias(tab_ref, bkt_ref, bias_ref, dil)
            dsc_ref[...] = jnp.zeros_like(dsc_ref)

        @pl.when(i == 0)
        def _():
            _dil_fill_pad(kpad, k_ref, length)
            _dil_fill_pad(vpad, v_ref, length)
            dk_ref[...] = jnp.zeros_like(dk_ref)
            dv_ref[...] = jnp.zeros_like(dv_ref)

        r0 = pl.multiple_of(i * DIL_QB, DIL_QB)
        valid = _dil_valid(r0, length)
        for m in range(N_HEADS // 2):
            lanes = pl.ds(m * 128, 128)
            dq2, dk2, dv2, ds_heads = _attn_pair_bwd(
                q_ref[:, lanes], kpad[pl.ds(r0, DIL_WIN), lanes], vpad[pl.ds(r0, DIL_WIN), lanes],
                (bias_ref[2 * m], bias_ref[2 * m + 1]), valid, do_ref[:, lanes], dlse_ref[:, lanes])
            dq_ref[:, lanes] = dq2
            dsc_ref[2 * m] += ds_heads[0]
            dsc_ref[2 * m + 1] += ds_heads[1]
            for first, size, live in ((0, DIL_HALF, i >= 1), (DIL_HALF, DIL_QB, None), (DIL_HALF + DIL_QB, DIL_HALF, i <= nqb - 2)):
                def add(first=first, size=size, dk2=dk2, dv2=dv2, lanes=lanes):
                    rows = pl.ds(pl.multiple_of(r0 - DIL_HALF + first, DIL_HALF), size)
                    dk_ref[rows, lanes] += dk2[first:first + size]
                    dv_ref[rows, lanes] += dv2[first:first + size]
                if live is None:
                    add()
                else:
                    pl.when(live)(add)

    out = jax.ShapeDtypeStruct((bsz, length, dil * GROUP_WIDTH), F32)
    dsc_shape = (N_HEADS, DIL_QB, DIL_WIN)
    dq, dk, dv, dsc = pl.pallas_call(
        body, name=name, grid=(bsz, dil, nqb),
        in_specs=[pl.BlockSpec(memory_space=pltpu.SMEM), pl.BlockSpec(bkt.shape, lambda b, rho, i: (0, 0)), blk, seq, seq, blk, blk],
        out_specs=[blk, seq, seq, pl.BlockSpec(dsc_shape, lambda b, rho, i: (0, 0, 0))],
        out_shape=[out, out, out, jax.ShapeDtypeStruct(dsc_shape, F32)],
        scratch_shapes=[pltpu.VMEM((length + 2 * DIL_HALF, GROUP_WIDTH), BF16), pltpu.VMEM((length + 2 * DIL_HALF, GROUP_WIDTH), BF16),
                        pltpu.VMEM(dsc_shape, F32)],
        compiler_params=_cparams(("arbitrary", "arbitrary", "arbitrary")),
    )(table, bkt, view(qb), view(kb), view(vb), view(do), view(dlse))
    return dq.reshape(qb.shape), dk.reshape(qb.shape), dv.reshape(qb.shape), dsc


def _headnorm(x, g, mavg):
    return x * lax.rsqrt(_head_stat(x * x, mavg) + RMS_EPS) * g


def _fold_gain(dg_full, fold):
    return jnp.dot(jnp.broadcast_to(dg_full, (8, dg_full.shape[1])), fold, precision=HI, preferred_element_type=F32)


def _bprep_fn(qp, kp, gq, gk, mavg):
    return _headnorm(qp, gq, mavg) * (HEAD_DIM ** -0.5), _headnorm(kp, gk, mavg)


def _dil_layout(dil):
    return None if dil == 1 else Strided(dil)


def _bprep_fwd(z, gq, gk, name):
    mavg = jnp.asarray(_np_group_avg(GROUP_WIDTH))

    def fn(qp, kp, vp, gqv, gkv, mv):
        qb, kb = _bprep_fn(qp, kp, gqv, gkv, mv)
        return (qb, kb, vp) * len(DIL_PATTERNS)

    w = GROUP_WIDTH
    outs = [(w, BF16, _dil_layout(dil)) for _, dil in DIL_PATTERNS for _ in range(3)]
    res = _rowmap(fn, [(z, w, 2), (z, w, 3), (z, w, 4)], [gq, gk, mavg], outs, [], name, 512, z.shape[0])
    return [res[3 * i:3 * i + 3] for i in range(len(DIL_PATTERNS))]


def _bprep_bwd(z, dqs, dks, dvs, gq, gk, name):
    mavg = jnp.asarray(_np_group_avg(GROUP_WIDTH))
    fold = jnp.asarray(_np_tile_fold(GROUP_WIDTH))

    def fn(qp, kp, dq0, dq1, dq2, dk0, dk1, dk2, dv0, dv1, dv2, gqv, gkv, mv, fv):
        _, vjp = jax.vjp(functools.partial(_bprep_fn, mavg=mv), qp, kp, gqv, gkv)
        dqp, dkp, dgq, dgk = vjp((dq0 + dq1 + dq2, dk0 + dk1 + dk2))
        return dqp, dkp, dv0 + dv1 + dv2, _fold_gain(dgq, fv), _fold_gain(dgk, fv)

    w = GROUP_WIDTH
    rows = [(z, w, 2), (z, w, 3)] + _pattern_rows(dqs) + _pattern_rows(dks) + _pattern_rows(dvs)
    return _rowmap(fn, rows, [gq, gk, mavg, fold], [(w, F32)] * 3, [(8, HEAD_DIM)] * 2, name, 256, z.shape[0])


def _mixture_fn(o0, o1, o2, l0, l1, l2):
    m = lax.stop_gradient(jnp.maximum(jnp.maximum(l0, l1), l2))
    e0, e1, e2 = jnp.exp(l0 - m), jnp.exp(l1 - m), jnp.exp(l2 - m)
    return (e0 * o0 + e1 * o1 + e2 * o2) / (e0 + e1 + e2)


def _pattern_rows(arrs):
    return [(a, GROUP_WIDTH, 0) if dil == 1 else (a, GROUP_WIDTH, 0, Strided(dil)) for a, (_, dil) in zip(arrs, DIL_PATTERNS)]


def _mixture_fwd(os_, ls_, name):
    n = os_[0].shape[0]
    return _rowmap(lambda *v: (_mixture_fn(*v),), _pattern_rows(os_) + _pattern_rows(ls_), [], [(GROUP_WIDTH, F32)], [], name, 512, n)[0]


def _mixture_bwd(os_, ls_, dy, name):
    w = GROUP_WIDTH

    def fn(*v):
        _, vjp = jax.vjp(_mixture_fn, *v[:6])
        return vjp(v[6])

    outs = [(w, F32, _dil_layout(dil)) for _ in range(2) for _, dil in DIL_PATTERNS]
    return _rowmap(fn, _pattern_rows(os_) + _pattern_rows(ls_) + [(dy, w, 0)], [], outs, [], name, 512, dy.shape[0])


def _relbias_fold(dscs, name):
    bkts = [jnp.asarray(_np_dil_buckets(dil)) for _, dil in DIL_PATTERNS]
    npat = len(DIL_PATTERNS)

    def body(*refs):
        bkt_refs, d_refs, o_ref = refs[:npat], refs[npat:-1], refs[-1]
        row = lax.broadcasted_iota(jnp.int32, (REL_BUCKETS, 128), 0)
        lane = lax.broadcasted_iota(jnp.int32, (REL_BUCKETS, 128), 1)
        out = jnp.zeros((REL_BUCKETS, 128), F32)
        for p, (_, dil) in enumerate(DIL_PATTERNS):
            bkt = bkt_refs[p][...]
            for h in range(N_HEADS):
                d = d_refs[2 * p][h] + d_refs[2 * p + 1][h]
                for b in _dil_live_buckets(dil):
                    val = jnp.sum(jnp.where(bkt == b, d, 0.0), keepdims=True)
                    out = out + jnp.where((row == b) & (lane == h), val, 0.0)
        o_ref[...] = out

    return pl.pallas_call(
        body, name=name, out_shape=jax.ShapeDtypeStruct((REL_BUCKETS, 128), F32), compiler_params=_cparams(),
    )(*bkts, *dscs)


DPREP_TM = 512


def _dprep_fn(qp, kp, vp, gq, gk, cq, sq, ck, sk, mavg_q, mavg_k, perm_q, perm_k, expand):
    rot = lambda x, perm: jnp.dot(x, perm, precision=SPLIT3, preferred_element_type=F32)
    qn = _headnorm(qp, gq, mavg_q)
    kn = _headnorm(kp, gk, mavg_k)
    qr = (qn * cq + rot(qn, perm_q) * sq) * (HEAD_DIM ** -0.5)
    kr = kn * ck + rot(kn, perm_k) * sk
    return qr, rot(kr, expand), rot(vp, expand)


def _dprep_consts():
    cq, sq = _np_rope_tables(N_HEADS)
    ck, sk = _np_rope_tables(KV_WIDTH // HEAD_DIM)
    tables = [jnp.asarray(a) for a in (cq, sq, ck, sk)]
    mats = [jnp.asarray(a) for a in (_np_group_avg(GROUP_WIDTH), _np_group_avg(KV_WIDTH), _np_rope_partner(GROUP_WIDTH),
                                      _np_rope_partner(KV_WIDTH), _np_kv_expand())]
    per = SEQ // DPREP_TM
    w, kw = GROUP_WIDTH, KV_WIDTH
    table_rows = [(tables[0], w, 0, per), (tables[1], w, 0, per), (tables[2], kw, 0, per), (tables[3], kw, 0, per)]
    return table_rows, mats


def _dprep_fwd(z, gq, gk, name):
    table_rows, mats = _dprep_consts()
    w, kw = GROUP_WIDTH, KV_WIDTH

    def fn(qp, kp, vp, cq, sq, ck, sk, gqv, gkv, *m):
        return _dprep_fn(qp, kp, vp, gqv, gkv, cq, sq, ck, sk, *m)

    return _rowmap(fn, [(z, w, 7), (z, kw, 32), (z, kw, 33)] + table_rows, [gq, gk] + mats, [(w, BF16)] * 3, [], name,
                   DPREP_TM, z.shape[0])


def _dprep_bwd(z, dq, dkx, dvx, gq, gk, name):
    table_rows, mats = _dprep_consts()
    fold_q = jnp.asarray(_np_tile_fold(GROUP_WIDTH))
    fold_k = jnp.asarray(_np_tile_fold(KV_WIDTH))
    w, kw = GROUP_WIDTH, KV_WIDTH

    def fn(qp, kp, vp, dqv, dkv, dvv, cq, sq, ck, sk, gqv, gkv, fq, fk, *m):
        f = lambda a, b, c, d, e: _dprep_fn(a, b, c, d, e, cq, sq, ck, sk, *m)
        _, vjp = jax.vjp(f, qp, kp, vp, gqv, gkv)
        dqp, dkp, dvp, dgq, dgk = vjp((dqv, dkv, dvv))
        return dqp, dkp, dvp, _fold_gain(dgq, fq), _fold_gain(dgk, fk)

    return _rowmap(fn, [(z, w, 7), (z, kw, 32), (z, kw, 33), (dq, w, 0), (dkx, w, 0), (dvx, w, 0)] + table_rows,
                   [gq, gk, fold_q, fold_k] + mats, [(w, F32), (kw, F32), (kw, F32)], [(8, HEAD_DIM)] * 2, name,
                   DPREP_TM, z.shape[0])


GQA_QB = 256


def _gqa_fwd(q, kx, vx, name):
    bsz = q.shape[0]
    blk = pl.BlockSpec((None, GQA_QB, GROUP_WIDTH), lambda b, i: (b, i, 0))
    seq = pl.BlockSpec((None, SEQ, GROUP_WIDTH), lambda b, i: (b, 0, 0))

    def body(q_ref, k_ref, v_ref, o_ref):
        for m in range(N_HEADS // 2):
            lanes = pl.ds(m * 128, 128)
            o_ref[:, lanes] = _attn_pair_fwd(q_ref[:, lanes], k_ref[:, lanes], v_ref[:, lanes], None, None)[0]

    return pl.pallas_call(
        body, name=name, grid=(bsz, SEQ // GQA_QB), in_specs=[blk, seq, seq], out_specs=blk,
        out_shape=jax.ShapeDtypeStruct((bsz, SEQ, GROUP_WIDTH), F32), compiler_params=_cparams(("parallel", "parallel")),
    )(q, kx, vx)


def _gqa_bwd(q, kx, vx, do, name):
    bsz = q.shape[0]
    blk = pl.BlockSpec((None, GQA_QB, GROUP_WIDTH), lambda b, i: (b, i, 0))
    seq = pl.BlockSpec((None, SEQ, GROUP_WIDTH), lambda b, i: (b, 0, 0))

    def body(q_ref, k_ref, v_ref, do_ref, dq_ref, dk_ref, dv_ref):
        @pl.when(pl.program_id(1) == 0)
        def _():
            dk_ref[...] = jnp.zeros_like(dk_ref)
            dv_ref[...] = jnp.zeros_like(dv_ref)

        for m in range(N_HEADS // 2):
            lanes = pl.ds(m * 128, 128)
            dq2, dk2, dv2, _ = _attn_pair_bwd(q_ref[:, lanes], k_ref[:, lanes], v_ref[:, lanes], None, None, do_ref[:, lanes], None)
            dq_ref[:, lanes] = dq2
            dk_ref[:, lanes] += dk2
            dv_ref[:, lanes] += dv2

    out = jax.ShapeDtypeStruct((bsz, SEQ, GROUP_WIDTH), F32)
    return pl.pallas_call(
        body, name=name, grid=(bsz, SEQ // GQA_QB), in_specs=[blk, seq, seq, blk], out_specs=[blk, seq, seq],
        out_shape=[out, out, out], compiler_params=_cparams(("parallel", "arbitrary")),
    )(q, kx, vx, do)


CONV_TILE = 64
CONV_LEAD = 16
CONV_WINDOW = CONV_TILE + 32


def _glu(a, g):
    return a * jax.nn.sigmoid(g)


def _conv_post(c, b, ln_g, ln_b):
    x = c + b
    xc = x - jnp.mean(x, axis=-1, keepdims=True)
    y = xc * lax.rsqrt(jnp.mean(xc * xc, axis=-1, keepdims=True) + LN_EPS) * ln_g + ln_b
    return y * jax.nn.sigmoid(y)


def _conv_shifted(win, offset):
    return pltpu.roll(win, CONV_WINDOW - offset, 0)[:CONV_TILE]


def _conv_fill(pad_ref, value_of_tile):
    zeros = jnp.zeros((CONV_LEAD, GROUP_WIDTH), F32)
    pad_ref[pl.ds(0, CONV_LEAD), :] = zeros
    pad_ref[pl.ds(CONV_LEAD + SEQ, CONV_LEAD), :] = zeros

    def step(t, carry):
        r0 = pl.multiple_of(t * CONV_TILE, CONV_TILE)
        pad_ref[pl.ds(CONV_LEAD + r0, CONV_TILE), :] = value_of_tile(r0)
        return carry

    lax.fori_loop(0, SEQ // CONV_TILE, step, 0)


def _conv_tile(pad_ref, w_ref, r0, flip):
    win = pad_ref[pl.ds(r0, CONV_WINDOW), :]
    acc = jnp.zeros((CONV_TILE, GROUP_WIDTH), F32)
    for k in range(CONV_WIDTH):
        offset = (CONV_WIDTH - k) if flip else (k + 1)
        acc = acc + w_ref[pl.ds(k, 1), :] * _conv_shifted(win, offset)
    return acc


def _conv_fwd(z3, w, b, ln_g, ln_b, name):
    bsz = z3.shape[0]
    seq = lambda cb: pl.BlockSpec((None, SEQ, GROUP_WIDTH), functools.partial(lambda i, cb: (i, 0, cb), cb=cb))
    full = lambda a: pl.BlockSpec(a.shape, lambda i: (0,) * a.ndim)

    def body(a_ref, g_ref, w_ref, b_ref, lg_ref, lb_ref, y_ref, pad_ref):
        _conv_fill(pad_ref, lambda r0: _glu(a_ref[pl.ds(r0, CONV_TILE), :], g_ref[pl.ds(r0, CONV_TILE), :]))

        def step(t, carry):
            r0 = pl.multiple_of(t * CONV_TILE, CONV_TILE)
            y_ref[pl.ds(r0, CONV_TILE), :] = _conv_post(_conv_tile(pad_ref, w_ref, r0, False), b_ref[...], lg_ref[...], lb_ref[...])
            return carry

        lax.fori_loop(0, SEQ // CONV_TILE, step, 0)

    return pl.pallas_call(
        body, name=name, grid=(bsz,), in_specs=[seq(5), seq(6), full(w), full(b), full(ln_g), full(ln_b)], out_specs=seq(0),
        out_shape=jax.ShapeDtypeStruct((bsz, SEQ, GROUP_WIDTH), F32),
        scratch_shapes=[pltpu.VMEM((SEQ + 2 * CONV_LEAD, GROUP_WIDTH), F32)], compiler_params=_cparams(("parallel",)),
    )(z3, z3, w, b, ln_g, ln_b)


def _conv_bwd(z3, dy, w, b, ln_g, ln_b, name):
    bsz = z3.shape[0]
    seq = lambda cb: pl.BlockSpec((None, SEQ, GROUP_WIDTH), functools.partial(lambda i, cb: (i, 0, cb), cb=cb))
    full = lambda a: pl.BlockSpec(a.shape, lambda i: (0,) * a.ndim)
    vec = pl.BlockSpec((1, GROUP_WIDTH), lambda i: (0, 0))

    def body(a_ref, g_ref, dy_ref, w_ref, b_ref, lg_ref, lb_ref, da_ref, dg_ref, dw_ref, db_ref, dlg_ref, dlb_ref, hpad, dpad, dw8):
        @pl.when(pl.program_id(0) == 0)
        def _():
            dw8[...] = jnp.zeros_like(dw8)
            db_ref[...] = jnp.zeros_like(db_ref)
            dlg_ref[...] = jnp.zeros_like(dlg_ref)
            dlb_ref[...] = jnp.zeros_like(dlb_ref)

        _conv_fill(hpad, lambda r0: _glu(a_ref[pl.ds(r0, CONV_TILE), :], g_ref[pl.ds(r0, CONV_TILE), :]))
        zeros = jnp.zeros((CONV_LEAD, GROUP_WIDTH), F32)
        dpad[pl.ds(0, CONV_LEAD), :] = zeros
        dpad[pl.ds(CONV_LEAD + SEQ, CONV_LEAD), :] = zeros

        def through_post(t, carry):
            r0 = pl.multiple_of(t * CONV_TILE, CONV_TILE)
            conv = _conv_tile(hpad, w_ref, r0, False)
            _, vjp = jax.vjp(_conv_post, conv, b_ref[...], lg_ref[...], lb_ref[...])
            dconv, db, dlg, dlb = vjp(dy_ref[pl.ds(r0, CONV_TILE), :])
            db_ref[...] += db
            dlg_ref[...] += dlg
            dlb_ref[...] += dlb
            dpad[pl.ds(CONV_LEAD + r0, CONV_TILE), :] = dconv
            win = hpad[pl.ds(r0, CONV_WINDOW), :]
            for k in range(CONV_WIDTH):
                prod = dconv * _conv_shifted(win, k + 1)
                part = prod[0:8]
                for j in range(1, CONV_TILE // 8):
                    part = part + prod[8 * j:8 * j + 8]
                dw8[k] += part
            return carry

        lax.fori_loop(0, SEQ // CONV_TILE, through_post, 0)

        def through_glu(t, carry):
            r0 = pl.multiple_of(t * CONV_TILE, CONV_TILE)
            dh = _conv_tile(dpad, w_ref, r0, True)
            rows = pl.ds(r0, CONV_TILE)
            _, vjp = jax.vjp(_glu, a_ref[rows, :], g_ref[rows, :])
            da, dg = vjp(dh)
            da_ref[rows, :] = da
            dg_ref[rows, :] = dg
            return carry

        lax.fori_loop(0, SEQ // CONV_TILE, through_glu, 0)
        dw_ref[...] = jnp.sum(dw8[...], axis=1)

    out = jax.ShapeDtypeStruct((bsz, SEQ, GROUP_WIDTH), F32)
    v = jax.ShapeDtypeStruct((1, GROUP_WIDTH), F32)
    return pl.pallas_call(
        body, name=name, grid=(bsz,), in_specs=[seq(5), seq(6), seq(0), full(w), full(b), full(ln_g), full(ln_b)],
        out_specs=[seq(0), seq(0), pl.BlockSpec((CONV_WIDTH, GROUP_WIDTH), lambda i: (0, 0)), vec, vec, vec],
        out_shape=[out, out, jax.ShapeDtypeStruct((CONV_WIDTH, GROUP_WIDTH), F32), v, v, v],
        scratch_shapes=[pltpu.VMEM((SEQ + 2 * CONV_LEAD, GROUP_WIDTH), F32), pltpu.VMEM((SEQ + 2 * CONV_LEAD, GROUP_WIDTH), F32),
                        pltpu.VMEM((CONV_WIDTH, 8, GROUP_WIDTH), F32)],
        compiler_params=_cparams(("arbitrary",)),
    )(z3, z3, dy, w, b, ln_g, ln_b)


def _mixnorm_fwd(ys, gains, name):
    w = GROUP_WIDTH

    def fn(*v):
        return (jnp.concatenate([_rms(v[i], v[4 + i]) for i in range(4)], axis=-1),)

    return _rowmap(fn, [(y, w, 0) for y in ys], list(gains), [(4 * w, BF16)], [], name, 512, ys[0].shape[0])[0]


def _mixnorm_bwd(dyn, ys, gains, name, follow=None):
    w = GROUP_WIDTH
    gains = list(gains) if follow is None else [*gains, follow]

    def fn(*v):
        dys, dgs = [], []
        for i in range(4):
            _, vjp = jax.vjp(_rms, v[4 + i], v[8 + i])
            dy, dg = vjp(v[i])
            dys.append(dy)
            dgs.append(dg)
        return (*dys, *dgs)

    rows = [(dyn, w, i) for i in range(4)] + [(y, w, 0) for y in ys]
    return _rowmap(fn, rows, list(gains), [(w, F32)] * 4, [(1, w)] * 4, name, 512, dyn.shape[0])


def _adamw_fn(w, g, m, v):
    m = ADAM_B1 * m + (1.0 - ADAM_B1) * g
    v = ADAM_B2 * v + (1.0 - ADAM_B2) * (g * g)
    m_hat = m / (1.0 - ADAM_B1 ** ADAM_STEP)
    v_hat = v / (1.0 - ADAM_B2 ** ADAM_STEP)
    delta = -ADAM_LR * (m_hat / (jnp.sqrt(v_hat) + ADAM_EPS) + ADAM_WD * w)
    return delta, m, v


def _adamw(w, g, m, v, name):
    r, c = w.shape
    tm = _pick(r, (256, 128, 64, 32, 16, 8))
    return _rowmap(_adamw_fn, [(a, c, 0) for a in (w, g, m, v)], [], [(c, F32)] * 3, [], name, tm, r)


def _layer_params(l, small, big):
    tile_row = lambda g, n: jnp.tile(g, n)[None, :]
    row = lambda g: g[None, :]
    w_s = small["sgu_w"][l].astype(BF16)
    return dict(
        norm1_g=row(small["norm1_g"][l]), norm2_g=row(small["norm2_g"][l]),
        w_s=w_s, w_s_t=jnp.swapaxes(w_s, 1, 2), bm=jnp.repeat(small["sgu_b"][l].T, HEAD_DIM, axis=1),
        gq_dil=tile_row(small["dil_qn_g"][l], N_HEADS), gk_dil=tile_row(small["dil_kn_g"][l], N_HEADS),
        conv_w=small["conv_w"][l], conv_b=row(small["conv_b"][l]), conv_ln_g=row(small["conv_ln_g"][l]),
        conv_ln_b=row(small["conv_ln_b"][l]),
        gq_gqa=tile_row(small["gqa_qn_g"][l], N_HEADS), gk_gqa=tile_row(small["gqa_kn_g"][l], KV_WIDTH // HEAD_DIM),
        mix_g=[row(small["mix_norm_g"][l][i * GROUP_WIDTH:(i + 1) * GROUP_WIDTH]) for i in range(4)],
        big=big,
    )


def _layer_fwd(x, p, table, bsz, tag):
    t = x.shape[0]
    seq3 = lambda a: a.reshape(bsz, SEQ, a.shape[-1])
    flat = lambda a: a.reshape(t, a.shape[-1])
    h1 = _rmsnorm_fwd(x, p["norm1_g"], tag + "rms1")
    z = _matmul([(h1, p["big"]("w_in", h1))], "nn", F32, tag + "mm_z")
    y_a = _sgu_fwd(z, p["w_s"], p["bm"], tag + "sgu_fwd")
    dil_qkv = _bprep_fwd(z, p["gq_dil"], p["gk_dil"], tag + "dil_prep")
    outs, lses = [], []
    for (_, dil), (qb, kb, vb) in zip(DIL_PATTERNS, dil_qkv):
        o, lse = _dil_fwd(qb, kb, vb, table, dil, f"{tag}dil{dil}_fwd")
        outs.append(o)
        lses.append(lse)
    y_b = _mixture_fwd(outs, lses, tag + "dil_mix")
    y_c = flat(_conv_fwd(seq3(z), p["conv_w"], p["conv_b"], p["conv_ln_g"], p["conv_ln_b"], tag + "conv_fwd"))
    qd, kx, vx = _dprep_fwd(z, p["gq_gqa"], p["gk_gqa"], tag + "gqa_prep")
    y_d = flat(_gqa_fwd(seq3(qd), seq3(kx), seq3(vx), tag + "gqa_fwd"))
    ys = [y_a, y_b, y_c, y_d]
    yn = _mixnorm_fwd(ys, p["mix_g"], tag + "mixnorm")
    x_mid = _matmul([(yn, p["big"]("w_out", yn))], "nn", F32, tag + "mm_out", residual=x)
    h2 = _rmsnorm_fwd(x_mid, p["norm2_g"], tag + "rms2")
    act, act_du, act_dg = _ffn_up(h2, p["big"]("w_gate", h2), p["big"]("w_up", h2), tag + "ffn_up")
    x_out = _matmul([(act, p["big"]("w_down", act))], "nn", F32, tag + "mm_down", residual=x_mid)
    saved = dict(x=x, h1=h1, z=z, dil_qkv=dil_qkv, outs=outs, lses=lses, qd=qd, kx=kx, vx=vx, ys=ys, yn=yn, x_mid=x_mid,
                 h2=h2, act=act, act_du=act_du, act_dg=act_dg)
    return x_out, saved


def _layer_bwd(dx_out, dx_out_b, s, p, table, bsz, tag, emit, mid_hook):
    t = dx_out.shape[0]
    seq3 = lambda a: a.reshape(bsz, SEQ, a.shape[-1])
    flat = lambda a: a.reshape(t, a.shape[-1])
    z = s["z"]
    small = {}
    weight = lambda name: p["big"](name, None)
    emit("w_down", _matmul([(s["act"], dx_out_b)], "tn", BF16, tag + "mm_dwdown").reshape(N_CHIPS, FFN_HIDDEN // N_CHIPS, D_MODEL))
    dgate, dup = _ffn_down_bwd(dx_out_b, weight("w_down"), s["act_du"], s["act_dg"], tag + "ffn_dact")
    emit("w_gate", _matmul([(s["h2"], dgate)], "tn", BF16, tag + "mm_dwgate", slabs=N_CHIPS))
    started = emit("w_up", _matmul([(s["h2"], dup)], "tn", BF16, tag + "mm_dwup", slabs=N_CHIPS))
    dh2 = _matmul([(dgate, weight("w_gate")), (dup, weight("w_up"))], "nt", F32, tag + "mm_dh2")
    dx_mid, dx_mid_b, dg2 = _rmsnorm_bwd(dh2, s["x_mid"], p["norm2_g"], dx_out, tag + "rms2_bwd", follow=started)
    small["norm2_g"] = dg2[0]
    mid_hook(dx_mid)
    dyn = _matmul([(dx_mid_b, weight("w_out"))], "nt", F32, tag + "mm_dyn")
    started = emit("w_out", _matmul([(s["yn"], dx_mid_b)], "tn", BF16, tag + "mm_dwout").reshape(N_CHIPS, D_MODEL // N_CHIPS, D_MODEL))
    *dys, dga, dgb, dgc, dgd = _mixnorm_bwd(dyn, s["ys"], p["mix_g"], tag + "mixnorm_bwd", follow=started)
    small["mix_norm_g"] = jnp.concatenate([dga[0], dgb[0], dgc[0], dgd[0]])
    du, dv, dws, dbs = _sgu_bwd(z, dys[0], p["w_s"], p["w_s_t"], p["bm"], tag + "sgu_bwd")
    small["sgu_w"] = dws
    small["sgu_b"] = dbs[:, :N_HEADS].T
    *douts, dl0, dl1, dl2 = _mixture_bwd(s["outs"], s["lses"], dys[1], tag + "dil_mix_bwd")
    dlses = [dl0, dl1, dl2]
    dqs, dks, dvs, dscs = [], [], [], []
    for i, (_, dil) in enumerate(DIL_PATTERNS):
        dq, dk, dvv, dsc = _dil_bwd(*s["dil_qkv"][i], douts[i], dlses[i], table, dil, f"{tag}dil{dil}_bwd")
        dqs.append(dq)
        dks.append(dk)
        dvs.append(dvv)
        dscs.append(dsc)
    dbq, dbk, dbv, dgq, dgk = _bprep_bwd(z, dqs, dks, dvs, p["gq_dil"], p["gk_dil"], tag + "dil_prep_bwd")
    small["dil_qn_g"], small["dil_kn_g"] = dgq[0], dgk[0]
    dca, dcg, dcw, dcb, dclg, dclb = _conv_bwd(seq3(z), seq3(dys[2]), p["conv_w"], p["conv_b"], p["conv_ln_g"], p["conv_ln_b"],
                                               tag + "conv_bwd")
    small["conv_w"], small["conv_b"], small["conv_ln_g"], small["conv_ln_b"] = dcw, dcb[0], dclg[0], dclb[0]
    dqd, dkx, dvx = _gqa_bwd(seq3(s["qd"]), seq3(s["kx"]), seq3(s["vx"]), seq3(dys[3]), tag + "gqa_bwd")
    ddq, ddk, ddv, dgq, dgk = _dprep_bwd(z, flat(dqd), flat(dkx), flat(dvx), p["gq_gqa"], p["gk_gqa"], tag + "gqa_prep_bwd")
    small["gqa_qn_g"], small["gqa_kn_g"] = dgq[0], dgk[0]
    dz = jnp.concatenate([a.astype(BF16) for a in (du, dv, dbq, dbk, dbv, flat(dca), flat(dcg), ddq, ddk, ddv)], axis=1)
    dz4 = dz.reshape(t, N_CHIPS, IN_WIDTH // N_CHIPS).transpose(1, 0, 2)
    started = emit("w_in", _matmul([(s["h1"], dz4)], "tn", BF16, tag + "mm_dwin", slabs=N_CHIPS))
    dh1 = _matmul([(dz, weight("w_in"))], "nt", F32, tag + "mm_dh1")
    dx, dx_b, dg1 = _rmsnorm_bwd(dh1, s["x"], p["norm1_g"], dx_mid, tag + "rms1_bwd", follow=started)
    small["norm1_g"] = dg1[0]
    return dx, dx_b, small, dscs


def _local_step(x, target, small, big, emit, mid_hook, bsz):
    table = small["rel_bias"]
    params = [_layer_params(l, small, functools.partial(big, l)) for l in range(DEPTH)]
    saved = []
    h = x
    for l in range(DEPTH):
        h, sv = _layer_fwd(h, params[l], table, bsz, f"l{l}_")
        saved.append(sv)
    dh, dh_b, loss = _loss_fwd_bwd(h, target, "loss")
    small_grads, dscs = [None] * DEPTH, [None] * DEPTH
    for l in reversed(range(DEPTH)):
        dh, dh_b, small_grads[l], dscs[l] = _layer_bwd(dh, dh_b, saved[l], params[l], table, bsz, f"l{l}_",
                                                       functools.partial(emit, l), functools.partial(mid_hook, l))
    fold_in = [dscs[l][i] for i in range(len(DIL_PATTERNS)) for l in range(DEPTH)]
    stacked = {k: jnp.stack([small_grads[l][k] for l in range(DEPTH)]) for k in small_grads[0]}
    stacked["rel_bias"] = _relbias_fold(fold_in, "relbias_fold")[:, :N_HEADS]
    return loss, dh, stacked


def _mesh_pos():
    return lax.axis_index("x"), lax.axis_index("y"), lax.axis_index("c")


def _other_chips(x, y):
    return [(1 - x, y), (x, 1 - y), (1 - x, 1 - y)]


_ANY = pl.BlockSpec(memory_space=pl.ANY)


def _swap_sibling(arrs, name):
    n = len(arrs)

    def body(*refs):
        in_refs, out_refs, send_sems, recv_sems = refs[:n], refs[n:2 * n], refs[2 * n], refs[2 * n + 1]
        x, y, c = _mesh_pos()
        copies = [pltpu.make_async_remote_copy(src_ref=in_refs[k], dst_ref=out_refs[k], send_sem=send_sems.at[k],
                                               recv_sem=recv_sems.at[k], device_id=(x, y, 1 - c), device_id_type=MESH)
                  for k in range(n)]
        for cp in copies:
            cp.start()
        for cp in copies:
            cp.wait()

    return pl.pallas_call(
        body, name=name, in_specs=[_ANY] * n, out_specs=[_ANY] * n,
        out_shape=[jax.ShapeDtypeStruct(a.shape, a.dtype) for a in arrs],
        scratch_shapes=[pltpu.SemaphoreType.DMA((n,)), pltpu.SemaphoreType.DMA((n,))],
    )(*arrs)


def _complete_pairs(arrs, name):
    n = len(arrs)

    def body(*refs):
        in_refs, out_refs, send_sems, recv_sems = refs[:n], refs[n:2 * n], refs[2 * n], refs[2 * n + 1]
        x, y, c = _mesh_pos()
        copies = [pltpu.make_async_remote_copy(src_ref=in_refs[k].at[c], dst_ref=out_refs[k].at[c], send_sem=send_sems.at[k],
                                               recv_sem=recv_sems.at[k], device_id=(x, y, 1 - c), device_id_type=MESH)
                  for k in range(n)]
        for cp in copies:
            cp.start()
        for k, cp in enumerate(copies):
            cp.wait_send()
            pltpu.make_async_remote_copy(src_ref=in_refs[k].at[1 - c], dst_ref=out_refs[k].at[1 - c], send_sem=send_sems.at[k],
                                         recv_sem=recv_sems.at[k], device_id=(x, y, 1 - c), device_id_type=MESH).wait_recv()

    return pl.pallas_call(
        body, name=name, in_specs=[_ANY] * n, out_specs=[_ANY] * n,
        out_shape=[jax.ShapeDtypeStruct(a.shape, a.dtype) for a in arrs], input_output_aliases={k: k for k in range(n)},
        scratch_shapes=[pltpu.SemaphoreType.DMA((n,)), pltpu.SemaphoreType.DMA((n,))],
    )(*arrs)


_HBM = pl.BlockSpec(memory_space=pltpu.HBM)
_SEM = pl.BlockSpec(memory_space=pltpu.SEMAPHORE)
_DATAFLOW = pltpu.SideEffectType.DATAFLOW_SIDE_EFFECTING


def _chip_copies(src_refs, land_refs, send_sems, recv_sems, scatter):
    x, y, c = _mesh_pos()
    me = 2 * x + y
    out = []
    for k, (src_ref, land_ref) in enumerate(zip(src_refs, land_refs)):
        if scatter:
            h = src_ref.shape[1] // 2
            for q in range(N_DEV - 1):
                fx, fy, fc = ((q + 1) >> 2) & 1, ((q + 1) >> 1) & 1, (q + 1) & 1
                px, py, pc = (1 - x if fx else x), (1 - y if fy else y), (1 - c if fc else c)
                src = src_ref.at[2 * px + py, pl.ds(pc * h, h)]
                sems = dict(send_sem=send_sems.at[7 * k + q], recv_sem=recv_sems.at[7 * k + q], device_id=(px, py, pc), device_id_type=MESH)
                out.append((pltpu.make_async_remote_copy(src_ref=src, dst_ref=land_ref.at[4 * x + 2 * y + c], **sems),
                            pltpu.make_async_remote_copy(src_ref=src, dst_ref=land_ref.at[4 * px + 2 * py + pc], **sems)))
            continue
        slot = (lambda chip: land_ref.at[c, chip]) if len(land_ref.shape) == 4 else (lambda chip: land_ref.at[chip])
        for j, (cx, cy) in enumerate(_other_chips(x, y)):
            sems = dict(send_sem=send_sems.at[3 * k + j], recv_sem=recv_sems.at[3 * k + j], device_id=(cx, cy, c), device_id_type=MESH)
            out.append((pltpu.make_async_remote_copy(src_ref=src_ref, dst_ref=slot(me), **sems),
                        pltpu.make_async_remote_copy(src_ref=src_ref, dst_ref=slot(2 * cx + cy), **sems)))
    return out


def _chips_start(srcs, scatter, after, name, per_core=False):
    n = len(srcs)
    n_sems = (N_DEV - 1 if scatter else N_CHIPS - 1) * n
    if scatter:
        lands = [lax.empty((N_DEV, s.shape[1] // 2, s.shape[2]), s.dtype) for s in srcs]
    else:
        lands = [lax.empty((*((2, N_CHIPS) if per_core else (N_CHIPS,)), *s.shape), s.dtype) for s in srcs]

    def body(*refs):
        src_refs, land_refs = refs[:n], refs[n:2 * n]
        send_sems, recv_sems, token = refs[2 * n + 1], refs[2 * n + 2], refs[-1]
        for sent, _ in _chip_copies(src_refs, land_refs, send_sems, recv_sems, scatter):
            sent.start()
        token[...] = jnp.zeros_like(token)

    hbm = lambda a: pltpu.HBM(a.shape, a.dtype)
    res = pl.pallas_call(
        body, name=name,
        in_specs=[_HBM] * (2 * n) + [_ANY],
        out_specs=[_SEM, _SEM] + [_HBM] * (2 * n) + [pl.BlockSpec(memory_space=pltpu.VMEM)],
        out_shape=[pltpu.SemaphoreType.DMA((n_sems,)), pltpu.SemaphoreType.DMA((n_sems,))] + [hbm(a) for a in srcs] + [hbm(a) for a in lands]
        + [jax.ShapeDtypeStruct((8, 128), F32)],
        input_output_aliases={i: 2 + i for i in range(2 * n)},
        compiler_params=pltpu.CompilerParams(has_side_effects=_DATAFLOW),
    )(*[pltpu.with_memory_space_constraint(a, pltpu.HBM) for a in (*srcs, *lands)], after)
    return (res[0], res[1], res[2:2 + n], res[2 + n:2 + 2 * n]), res[-1]


def _chips_wait(handle, scatter, after, name):
    send_sems, recv_sems, srcs, lands = handle
    n = len(srcs)

    def body(*refs):
        src_refs, land_refs = refs[:n], refs[n:2 * n]
        send_sems, recv_sems = refs[2 * n], refs[2 * n + 1]
        for sent, landed in _chip_copies(src_refs, land_refs, send_sems, recv_sems, scatter):
            sent.wait_send()
            landed.wait_recv()

    hbm = lambda a: pltpu.HBM(a.shape, a.dtype)
    res = pl.pallas_call(
        body, name=name,
        in_specs=[_HBM] * (2 * n) + [_SEM, _SEM, _ANY], out_specs=[_HBM] * (2 * n),
        out_shape=[hbm(a) for a in srcs] + [hbm(a) for a in lands],
        input_output_aliases={i: i for i in range(2 * n)},
        compiler_params=pltpu.CompilerParams(has_side_effects=_DATAFLOW),
    )(*srcs, *lands, send_sems, recv_sems, after)
    return res[n:]


N_DEV = 8


def _everyone_copies(src_ref, land_ref, send_sems, recv_sems):
    x, y, c = _mesh_pos()
    out = []
    for q in range(N_DEV - 1):
        fx, fy, fc = ((q + 1) >> 2) & 1, ((q + 1) >> 1) & 1, (q + 1) & 1
        px, py, pc = (1 - x if fx else x), (1 - y if fy else y), (1 - c if fc else c)
        sems = dict(send_sem=send_sems.at[q], recv_sem=recv_sems.at[q], device_id=(px, py, pc), device_id_type=MESH)
        out.append((pltpu.make_async_remote_copy(src_ref=src_ref, dst_ref=land_ref.at[4 * x + 2 * y + c], **sems),
                    pltpu.make_async_remote_copy(src_ref=src_ref, dst_ref=land_ref.at[4 * px + 2 * py + pc], **sems)))
    return out


def _everyone_start(block, after, name):
    land = lax.empty((N_DEV, *block.shape), block.dtype)

    def body(src_ref, land_ref, after_ref, send_sems, recv_sems, src_thru, land_thru, token):
        for sent, _ in _everyone_copies(src_ref, land_ref, send_sems, recv_sems):
            sent.start()
        token[...] = jnp.zeros_like(token)

    hbm = lambda a: pltpu.HBM(a.shape, a.dtype)
    n_sem = N_DEV - 1
    res = pl.pallas_call(
        body, name=name, in_specs=[_HBM, _HBM, _ANY],
        out_specs=[_SEM, _SEM, _HBM, _HBM, pl.BlockSpec(memory_space=pltpu.VMEM)],
        out_shape=[pltpu.SemaphoreType.DMA((n_sem,)), pltpu.SemaphoreType.DMA((n_sem,)), hbm(block), hbm(land),
                   jax.ShapeDtypeStruct((8, 128), F32)],
        input_output_aliases={0: 2, 1: 3}, compiler_params=pltpu.CompilerParams(has_side_effects=_DATAFLOW),
    )(pltpu.with_memory_space_constraint(block, pltpu.HBM), pltpu.with_memory_space_constraint(land, pltpu.HBM), after)
    return res[:4]


def _everyone_wait(handle, after, name):
    send_sems, recv_sems, block, land = handle

    def body(src_ref, land_ref, send_sems, recv_sems, after_ref, src_thru, land_thru):
        for sent, landed in _everyone_copies(src_ref, land_ref, send_sems, recv_sems):
            sent.wait_send()
            landed.wait_recv()

    hbm = lambda a: pltpu.HBM(a.shape, a.dtype)
    return pl.pallas_call(
        body, name=name, in_specs=[_HBM, _HBM, _SEM, _SEM, _ANY], out_specs=[_HBM, _HBM], out_shape=[hbm(block), hbm(land)],
        input_output_aliases={0: 0, 1: 1}, compiler_params=pltpu.CompilerParams(has_side_effects=_DATAFLOW),
    )(block, land, send_sems, recv_sems, after)


def _sum_devices(blocks, name):
    n_dev, m, lanes = blocks.shape
    tm = _pick(m, (512, 256, 128, 64, 32, 16, 8))

    def body(b_ref, o_ref):
        total = b_ref[0]
        for d in range(1, n_dev):
            total = total + b_ref[d]
        o_ref[...] = total

    return pl.pallas_call(
        body, name=name, grid=(m // tm,), in_specs=[pl.BlockSpec((n_dev, tm, lanes), lambda i: (0, i, 0))],
        out_specs=pl.BlockSpec((tm, lanes), lambda i: (i, 0)), out_shape=jax.ShapeDtypeStruct((m, lanes), F32),
        compiler_params=_cparams(("parallel",)),
    )(blocks)


def _allgather_sum_small(block, name):
    m_per, n = block.shape

    def body(x_ref, out_ref, sum_ref, send_sems, recv_sems, local_sem):
        x, y, c = _mesh_pos()
        me, sibling = (x, y, c), (x, y, 1 - c)
        chips = _other_chips(x, y)

        def rows(px, py, pc):
            return out_ref.at[pl.ds((4 * px + 2 * py + pc) * m_per, m_per), :]

        def copy(k, blk, to, src=None):
            return pltpu.make_async_remote_copy(src_ref=rows(*blk) if src is None else src, dst_ref=rows(*blk),
                                                send_sem=send_sems.at[k], recv_sem=recv_sems.at[k], device_id=to, device_id_type=MESH)

        mine = pltpu.make_async_copy(x_ref, rows(*me), local_sem)
        mine.start()
        first = [copy(0, me, sibling, src=x_ref)]
        first += [copy(1 + j, me, (*chip, c), src=x_ref) for j, chip in enumerate(chips)]
        for cp in first:
            cp.start()
        passed = [copy(4 + j, (*chip, c), sibling) for j, chip in enumerate(chips)]
        for j, chip in enumerate(chips):
            copy(1 + j, (*chip, c), me).wait_recv()
            passed[j].start()
        copy(0, sibling, me).wait_recv()
        for j, chip in enumerate(chips):
            copy(4 + j, (*chip, 1 - c), me).wait_recv()
        for cp in first + passed:
            cp.wait_send()
        mine.wait()
        total = out_ref[pl.ds(0, m_per), :]
        for d in range(1, N_DEV):
            total = total + out_ref[pl.ds(d * m_per, m_per), :]
        sum_ref[...] = total

    vmem = pl.BlockSpec(memory_space=pltpu.VMEM)
    return pl.pallas_call(
        body, name=name, in_specs=[vmem], out_specs=[vmem, vmem],
        out_shape=[jax.ShapeDtypeStruct((N_DEV * m_per, n), F32), jax.ShapeDtypeStruct((m_per, n), F32)],
        scratch_shapes=[pltpu.SemaphoreType.DMA((7,)), pltpu.SemaphoreType.DMA((7,)), pltpu.SemaphoreType.DMA],
        compiler_params=pltpu.CompilerParams(vmem_limit_bytes=V7X_VMEM_LIMIT),
    )(block)


WEIGHTS = ("rel_bias", "norm1_g", "w_in", "sgu_w", "sgu_b", "dil_qn_g", "dil_kn_g", "conv_w", "conv_b", "conv_ln_g", "conv_ln_b",
           "gqa_qn_g", "gqa_kn_g", "mix_norm_g", "w_out", "norm2_g", "w_gate", "w_up", "w_down")
SHARDED = ("w_in", "w_out", "w_gate", "w_up", "w_down")
COLUMN_SHARDED = ("w_in", "w_gate", "w_up")
REPLICATED = tuple(k for k in WEIGHTS if k not in SHARDED and k != "conv_w")


PACK_ROWS = 256


def _pack(parts):
    flat = jnp.concatenate([p.reshape(-1) for p in parts])
    pad = (-flat.shape[0]) % (PACK_ROWS * LANES)
    return jnp.pad(flat, (0, pad)).reshape(-1, LANES)


def _unpack(buf, shapes):
    flat = buf.reshape(-1)
    out, at = [], 0
    for s in shapes:
        size = math.prod(s)
        out.append(flat[at:at + size].reshape(s))
        at += size
    return out


RS_TM = (256, 128, 64, 32, 16)


def _add_own_seven(own, land, place, name):
    _, h, cols = land.shape
    tm = _pick(h, RS_TM)
    nb = h // tm

    def body(place_ref, own_ref, *refs):
        total = own_ref[...].astype(F32)
        for l_ref in refs[:-1]:
            total = total + l_ref[...].astype(F32)
        refs[-1][...] = total

    slot = lambda r: pl.BlockSpec((None, tm, cols), functools.partial(lambda i, p, r: (jnp.bitwise_xor(p[2], r), i, 0), r=r))
    grid_spec = pltpu.PrefetchScalarGridSpec(
        num_scalar_prefetch=1, grid=(nb,),
        in_specs=[pl.BlockSpec((None, tm, cols), lambda i, p: (p[0], p[1] * nb + i, 0))] + [slot(r) for r in range(1, N_DEV)],
        out_specs=pl.BlockSpec((tm, cols), lambda i, p: (i, 0)))
    return pl.pallas_call(body, name=name, grid_spec=grid_spec, out_shape=jax.ShapeDtypeStruct((h, cols), F32),
                          compiler_params=_cparams(("parallel",)))(place, own, *[land] * (N_DEV - 1))


def _reduce_start(grads, after, tag):
    handle, token = _chips_start(grads, True, after, tag + "start")
    return (handle, grads), token


def _reduce_finish(started, place, after, tag):
    handle, grads = started
    lands = _chips_wait(handle, True, after, tag + "wait")
    totals = [_add_own_seven(g, land, place, f"{tag}sum_{k}") for k, (g, land) in enumerate(zip(grads, lands))]
    return list(zip(totals, _swap_sibling(totals, tag + "share")))


def _adamw_shard(w, m, v, halves, c1, name):
    depth, rows, cols = w.shape
    h = rows // 2
    tm = _pick(h, RS_TM)
    nb = h // tm
    sources = [a for pair in halves for a in pair]

    def body(c_ref, w_ref, m_ref, v_ref, *refs):
        g_refs, (g_out, d_out, m_out, v_out) = refs[:2 * depth], refs[2 * depth:]
        layer, mine = pl.program_id(0), pl.program_id(1) == c_ref[0]
        g = None
        for l in range(depth):
            g_l = jnp.where(mine, g_refs[2 * l][...], g_refs[2 * l + 1][...])
            g = g_l if g is None else jnp.where(layer == l, g_l, g)
        delta, m_new, v_new = _adamw_fn(w_ref[...], g, m_ref[...], v_ref[...])
        g_out[...], d_out[...], m_out[...], v_out[...] = g, delta, m_new, v_new

    def source_spec(l, own):
        def index(layer, half, i, c_ref):
            return (jnp.where((layer == l) & ((half == c_ref[0]) == own), i, 0), 0)
        return pl.BlockSpec((tm, cols), index)

    blk = pl.BlockSpec((None, tm, cols), lambda layer, half, i, c_ref: (layer, half * nb + i, 0))
    grid_spec = pltpu.PrefetchScalarGridSpec(
        num_scalar_prefetch=1, grid=(depth, 2, nb),
        in_specs=[blk, blk, blk] + [source_spec(l, own) for l in range(depth) for own in (True, False)], out_specs=[blk] * 4)
    return pl.pallas_call(body, name=name, grid_spec=grid_spec, out_shape=[jax.ShapeDtypeStruct(w.shape, F32)] * 4,
                          compiler_params=_cparams(("arbitrary", "arbitrary", "arbitrary")))(c1, w, m, v, *sources)


GATHER_GROUPS = ((("w_in",), 0), (("w_out",), None), (("w_gate", "w_up"), None), (("w_down",), None), (("w_in",), 1))
REDUCE_GROUPS = (("w_down", "w_gate", "w_up"), ("w_out",), ("w_in",))


def kernel(x, rel_bias, norm1_g, w_in, sgu_w, sgu_b, dil_qn_g, dil_kn_g, conv_w, conv_b, conv_ln_g, conv_ln_b, gqa_qn_g, gqa_kn_g, mix_norm_g, w_out, norm2_g, w_gate, w_up, w_down, loss_target, m_rel_bias, m_norm1_g, m_w_in, m_sgu_w, m_sgu_b, m_dil_qn_g, m_dil_kn_g, m_conv_w, m_conv_b, m_conv_ln_g, m_conv_ln_b, m_gqa_qn_g, m_gqa_kn_g, m_mix_norm_g, m_w_out, m_norm2_g, m_w_gate, m_w_up, m_w_down, v_rel_bias, v_norm1_g, v_w_in, v_sgu_w, v_sgu_b, v_dil_qn_g, v_dil_kn_g, v_conv_w, v_conv_b, v_conv_ln_g, v_conv_ln_b, v_gqa_qn_g, v_gqa_kn_g, v_mix_norm_g, v_w_out, v_norm2_g, v_w_gate, v_w_up, v_w_down):
    w = dict(rel_bias=rel_bias, norm1_g=norm1_g, w_in=w_in, sgu_w=sgu_w, sgu_b=sgu_b, dil_qn_g=dil_qn_g, dil_kn_g=dil_kn_g,
             conv_w=conv_w, conv_b=conv_b, conv_ln_g=conv_ln_g, conv_ln_b=conv_ln_b, gqa_qn_g=gqa_qn_g, gqa_kn_g=gqa_kn_g,
             mix_norm_g=mix_norm_g, w_out=w_out, norm2_g=norm2_g, w_gate=w_gate, w_up=w_up, w_down=w_down)
    m = dict(rel_bias=m_rel_bias, norm1_g=m_norm1_g, w_in=m_w_in, sgu_w=m_sgu_w, sgu_b=m_sgu_b, dil_qn_g=m_dil_qn_g,
             dil_kn_g=m_dil_kn_g, conv_w=m_conv_w, conv_b=m_conv_b, conv_ln_g=m_conv_ln_g, conv_ln_b=m_conv_ln_b,
             gqa_qn_g=m_gqa_qn_g, gqa_kn_g=m_gqa_kn_g, mix_norm_g=m_mix_norm_g, w_out=m_w_out, norm2_g=m_norm2_g,
             w_gate=m_w_gate, w_up=m_w_up, w_down=m_w_down)
    v = dict(rel_bias=v_rel_bias, norm1_g=v_norm1_g, w_in=v_w_in, sgu_w=v_sgu_w, sgu_b=v_sgu_b, dil_qn_g=v_dil_qn_g,
             dil_kn_g=v_dil_kn_g, conv_w=v_conv_w, conv_b=v_conv_b, conv_ln_g=v_conv_ln_g, conv_ln_b=v_conv_ln_b,
             gqa_qn_g=v_gqa_qn_g, gqa_kn_g=v_gqa_kn_g, mix_norm_g=v_mix_norm_g, w_out=v_w_out, norm2_g=v_norm2_g,
             w_gate=v_w_gate, w_up=v_w_up, w_down=v_w_down)
    bsz = x.shape[0]
    t = bsz * SEQ
    xi, yi, ci = _mesh_pos()
    chip = 2 * xi + yi
    conv_cols = conv_w.shape[-1]

    conv_rows = DEPTH * CONV_WIDTH
    conv_block = jnp.pad(conv_w.reshape(conv_rows, conv_cols), ((0, (-conv_rows) % 8), (0, 0)))
    every, _ = _allgather_sum_small(conv_block, "conv_w_gather")
    every = every.reshape(N_DEV, conv_block.shape[0], conv_cols)
    conv_w_full = jnp.concatenate([every[2 * j, :conv_rows].reshape(DEPTH, CONV_WIDTH, conv_cols) for j in range(N_CHIPS)], axis=-1)

    c1 = jnp.reshape(ci, (1,)).astype(jnp.int32)
    place = jnp.stack([chip, ci, 4 * xi + 2 * yi + ci]).astype(jnp.int32)

    def own_part(k, layer):
        if layer is None:
            return lax.dynamic_index_in_dim(w[k], ci, axis=0, keepdims=False).astype(BF16)
        half = w[k].shape[1] // 2
        return lax.dynamic_slice_in_dim(w[k][layer], ci * half, half, axis=0).astype(BF16)

    fetches, token = [], every
    for gi, (group, layer) in enumerate(GATHER_GROUPS):
        parts = [own_part(k, layer) for k in group]
        handle, token = _chips_start(parts, False, token, f"gather{gi}_start", per_core=True)
        fetches.append((handle, parts))
    all_started = token
    gathered = {}

    def big(l, name, after):
        if (l, name) not in gathered:
            gi = [name in group and layer in (None, l) for group, layer in GATHER_GROUPS].index(True)
            (group, layer), (handle, parts) = GATHER_GROUPS[gi], fetches[gi]
            lands = _chips_wait(handle, False, all_started if after is None else after, f"gather{gi}_wait")
            mine = [lax.dynamic_update_slice(land, own[None, None], (ci, chip, 0, 0)) for land, own in zip(lands, parts)]
            for k, g in zip(group, _complete_pairs(mine, f"gather{gi}_share")):
                rows, cols = g.shape[2:]
                if layer is not None:
                    gathered[layer, k] = g.transpose(0, 2, 1, 3).reshape(2 * rows, N_CHIPS * cols)
                    continue
                for each in range(DEPTH):
                    gathered[each, k] = (g[each].transpose(1, 0, 2).reshape(rows, N_CHIPS * cols) if k in COLUMN_SHARDED
                                         else g[each].reshape(N_CHIPS * rows, cols))
        return gathered[l, name]

    big(0, "w_in", None)

    pending, started, reduced = {}, {}, {}

    def emit(l, name, g):
        pending[l, name] = g
        for gi, group in enumerate(REDUCE_GROUPS):
            if name in group and all((l, k) in pending for k in group):
                started[l, gi], token = _reduce_start([pending[l, k] for k in group], g, f"l{l}_reduce{gi}_")
                return token
        return None

    def finish(l, after):
        for gi, group in enumerate(REDUCE_GROUPS):
            for k, r in zip(group, _reduce_finish(started[l, gi], place, after, f"l{l}_reduce{gi}_")):
                reduced[l, k] = r
            after = reduced[l, group[0]][1]

    def mid_hook(l, a):
        if l + 1 < DEPTH:
            finish(l + 1, a)

    small = {k: w[k] for k in REPLICATED}
    small["conv_w"] = conv_w_full
    loss, dx, small_grads = _local_step(x.reshape(t, D_MODEL), loss_target.reshape(t, D_MODEL), small, big, emit, mid_hook, bsz)
    loss = lax.psum(loss[0, 0], ("x", "y", "c"))

    names = REPLICATED + ("conv_w",)
    shapes = [small_grads[k].shape for k in names]
    small_exchange = _everyone_start(_pack([small_grads[k] for k in names]), dx, "small_grads_start")
    finish(0, dx)

    grads, deltas, new_m, new_v = {}, {}, {}, {}
    for k in SHARDED:
        grads[k], deltas[k], new_m[k], new_v[k] = _adamw_shard(w[k], m[k], v[k], [reduced[l, k] for l in range(DEPTH)], c1, "adamw_" + k)

    own, others = _everyone_wait(small_exchange, new_v[SHARDED[-1]], "small_grads_wait")
    summed = _sum_devices(lax.dynamic_update_slice_in_dim(others, own[None], 4 * xi + 2 * yi + ci, axis=0), "small_grads_sum")
    summed_parts = dict(zip(names, _unpack(summed, shapes)))
    rep_shapes = [w[k].shape for k in REPLICATED]
    packed = [_pack([src[k] for k in REPLICATED]) for src in (w, {k: summed_parts[k] for k in REPLICATED}, m, v)]
    d_p, m_p, v_p = _adamw(*packed, "adamw_replicated")
    for k, gk, dk, mk, vk in zip(REPLICATED, _unpack(packed[1], rep_shapes), _unpack(d_p, rep_shapes), _unpack(m_p, rep_shapes),
                                 _unpack(v_p, rep_shapes)):
        grads[k], deltas[k], new_m[k], new_v[k] = gk, dk, mk, vk
    g_conv = lax.dynamic_slice_in_dim(summed_parts["conv_w"], chip * conv_cols, conv_cols, axis=2)
    packed = [_pack([a]) for a in (conv_w, g_conv, m["conv_w"], v["conv_w"])]
    d_p, m_p, v_p = _adamw(*packed, "adamw_conv_w")
    grads["conv_w"] = g_conv
    deltas["conv_w"], new_m["conv_w"], new_v["conv_w"] = (_unpack(a, [conv_w.shape])[0] for a in (d_p, m_p, v_p))

    return (loss, dx.reshape(x.shape), *[grads[k] for k in WEIGHTS], *[deltas[k] for k in WEIGHTS],
            *[new_m[k] for k in WEIGHTS], *[new_v[k] for k in WEIGHTS])
```

```python
import functools
import math

import numpy as np
import jax
import jax.numpy as jnp
from jax import lax
from jax.experimental import pallas as pl
from jax.experimental.pallas import tpu as pltpu

F32 = jnp.float32
BF16 = jnp.bfloat16

D_MODEL = 2048
SEQ = 2048
DEPTH = 2
HEAD_DIM = 64
GROUP_WIDTH = 512
N_HEADS = 8
SGU_CHUNK = 128
DIL_PATTERNS = ((128, 1), (512, 4), (2048, 16))
DIL_HALF = 64
CONV_WIDTH = 31
KV_WIDTH = 128
GRID_W = 64
ROPE_THETA = 10000.0
REL_BUCKETS = 32
REL_MAX_DIST = 1024
FFN_HIDDEN = 5632
IN_WIDTH = 4352
RMS_EPS = 1e-6
LN_EPS = 1e-5
ADAM_LR = 0.001
ADAM_B1 = 0.9
ADAM_B2 = 0.999
ADAM_EPS = 1e-08
ADAM_WD = 0.01
ADAM_STEP = 10
N_CHIPS = 4

V7X_VMEM_LIMIT = 56 * 1024 * 1024
MATMUL_VMEM_BUDGET = 48 * 1024 * 1024
LANES = 128
HI = lax.Precision.HIGHEST
SPLIT3 = lax.Precision.HIGH
MESH = pl.DeviceIdType.MESH


def _cparams(sem=None):
    return pltpu.CompilerParams(dimension_semantics=sem, vmem_limit_bytes=V7X_VMEM_LIMIT)


def _pick(n, cands):
    for c in cands:
        if n % c == 0:
            return c
    raise ValueError(f"no tile for {n}")


_DIMS = {"nn": (((1,), (0,)), ((), ())), "nt": (((1,), (1,)), ((), ())), "tn": (((0,), (0,)), ((), ()))}


def _matmul(pairs, mode, out_dtype, name, residual=None, slabs=1):
    a0, b0 = pairs[0]
    b3 = b0.ndim == 3
    if mode == "nn":
        (M, K), N = a0.shape, b0.shape[1]
    elif mode == "nt":
        (M, K), N = a0.shape, b0.shape[0]
    else:
        (K, M) = a0.shape
        N = b0.shape[-1] if b3 else b0.shape[1] // slabs
    npairs = len(pairs)
    a_bytes, b_bytes, o_bytes = a0.dtype.itemsize, b0.dtype.itemsize, jnp.dtype(out_dtype).itemsize
    per_out = 4 + 2 * o_bytes + (8 if residual is not None else 0)
    tn_cands = [c for c in ((1024, 512) if K <= 2048 else (512,)) + (1408, 2176, 256) if N % c == 0] + [N]
    tm, tn = next((tm, tn) for tn in tn_cands for tm in (1024, 1408, 512, 256)
                  if M % tm == 0 and 2 * npairs * K * (tm * a_bytes + tn * b_bytes) + tm * tn * per_out <= MATMUL_VMEM_BUDGET)
    tk = K
    ni, nj = M // tm, N // tn
    j_outer = nj * M * a_bytes + N * b_bytes < M * a_bytes + ni * N * b_bytes
    grid = (slabs, nj, ni) if j_outer else (slabs, ni, nj)
    at = lambda f: (lambda s, g1, g2: f(s, g2, g1)) if j_outer else f

    if mode in ("nn", "nt"):
        a_spec = pl.BlockSpec((tm, tk), at(lambda s, i, j: (i, 0)))
    else:
        a_spec = pl.BlockSpec((tk, tm), at(lambda s, i, j: (0, i)))
    if mode == "nt":
        b_spec = pl.BlockSpec((tn, tk), at(lambda s, i, j: (j, 0)))
    elif b3:
        b_spec = pl.BlockSpec((None, tk, tn), at(lambda s, i, j: (s, 0, j)))
    else:
        b_spec = pl.BlockSpec((tk, tn), at(lambda s, i, j: (0, s * nj + j)))
    if slabs > 1:
        o_spec = pl.BlockSpec((None, tm, tn), at(lambda s, i, j: (s, i, j)))
        o_shape = jax.ShapeDtypeStruct((slabs, M, N), out_dtype)
    else:
        o_spec = pl.BlockSpec((tm, tn), at(lambda s, i, j: (i, j)))
        o_shape = jax.ShapeDtypeStruct((M, N), out_dtype)
    in_specs = [a_spec] * npairs + [b_spec] * npairs
    args = [a for a, _ in pairs] + [b for _, b in pairs]
    if residual is not None:
        in_specs.append(pl.BlockSpec((tm, tn), at(lambda s, i, j: (i, j))))
        args.append(residual)
    dims = _DIMS[mode]

    def body(*refs):
        a_refs, b_refs = refs[:npairs], refs[npairs:2 * npairs]
        res_ref = refs[2 * npairs] if residual is not None else None
        o_ref = refs[-1]
        r = None
        for a_ref, b_ref in zip(a_refs, b_refs):
            d = lax.dot_general(a_ref[...].astype(BF16), b_ref[...].astype(BF16), dims, preferred_element_type=F32)
            r = d if r is None else r + d
        if res_ref is not None:
            r = r + res_ref[...]
        o_ref[...] = r.astype(out_dtype)

    return pl.pallas_call(
        body, name=name, grid=grid, in_specs=in_specs, out_specs=o_spec, out_shape=o_shape,
        compiler_params=_cparams(("parallel", "parallel", "parallel")),
    )(*args)


class Strided:
    def __init__(self, r):
        self.r = r


def _rowmap(fn, rows, fulls, row_outs, acc_outs, name, tm, n_rows):
    nr, nf, nro = len(rows), len(fulls), len(row_outs)
    rows = [r if len(r) == 4 else (*r, n_rows // tm) for r in rows]
    row_outs = [o if len(o) == 3 else (*o, None) for o in row_outs]
    in_specs = [pl.BlockSpec((tm // per.r, per.r * w), lambda i: (i, 0)) if isinstance(per, Strided) else
                pl.BlockSpec((tm, w), functools.partial(lambda i, cb, per: (i % per, cb), cb=cb, per=per)) for _, w, cb, per in rows]
    in_specs += [pl.BlockSpec(f.shape, lambda i: (0,) * f.ndim) for f in fulls]
    out_specs = [pl.BlockSpec((tm, w) if st is None else (tm // st.r, st.r * w), lambda i: (i, 0)) for w, _, st in row_outs]
    out_specs += [pl.BlockSpec(s, functools.partial(lambda i, n: (0,) * n, n=len(s))) for s in acc_outs]
    out_shape = [jax.ShapeDtypeStruct((n_rows, w) if st is None else (n_rows // st.r, st.r * w), dt) for w, dt, st in row_outs]
    out_shape += [jax.ShapeDtypeStruct(s, F32) for s in acc_outs]
    strided = [(k, w, per.r) for k, (_, w, _, per) in enumerate(rows) if isinstance(per, Strided)]
    strided += [(nr + nf + k, w, st.r) for k, (w, _, st) in enumerate(row_outs) if st is not None]
    n_scratch = len(strided)

    def body(*refs):
        refs, scratch = refs[:len(refs) - n_scratch], dict(zip([k for k, _, _ in strided], refs[len(refs) - n_scratch:]))
        ins = []
        for k, ref in enumerate(refs[:nr + nf]):
            if k in scratch:
                w, r, scr = rows[k][1], rows[k][3].r, scratch[k]
                for rho in range(r):
                    for j in range(w // LANES):
                        scr.at[j][pl.ds(rho, tm // r, stride=r), :] = ref[:, pl.ds(rho * w + j * LANES, LANES)]
                ins.append(jnp.concatenate([scr[j] for j in range(w // LANES)], axis=1))
            else:
                ins.append(ref[...])
        outs = fn(*ins)
        o_refs = refs[nr + nf:]
        for k, (o_ref, val) in enumerate(zip(o_refs[:nro], outs[:nro])):
            if nr + nf + k in scratch:
                w, r, scr = row_outs[k][0], row_outs[k][2].r, scratch[nr + nf + k]
                val = val.astype(F32)
                for j in range(w // LANES):
                    scr[j] = val[:, j * LANES:(j + 1) * LANES]
                for rho in range(r):
                    for j in range(w // LANES):
                        o_ref[:, pl.ds(rho * w + j * LANES, LANES)] = scr.at[j][pl.ds(rho, tm // r, stride=r), :].astype(o_ref.dtype)
            else:
                o_ref[...] = val.astype(o_ref.dtype)
        if acc_outs:
            first = pl.program_id(0) == 0
            for o_ref, val in zip(o_refs[nro:], outs[nro:]):
                @pl.when(first)
                def _(o_ref=o_ref, val=val):
                    o_ref[...] = val

                @pl.when(jnp.logical_not(first))
                def _(o_ref=o_ref, val=val):
                    o_ref[...] += val

    res = pl.pallas_call(
        body, name=name, grid=(n_rows // tm,), in_specs=in_specs, out_specs=out_specs, out_shape=out_shape,
        scratch_shapes=[pltpu.VMEM((w // LANES, tm, LANES), F32) for _, w, _ in strided],
        compiler_params=_cparams(("arbitrary",) if acc_outs else ("parallel",)),
    )(*[r[0] for r in rows], *fulls)
    return res


def _rms(x, g):
    return x * lax.rsqrt(jnp.mean(x * x, axis=-1, keepdims=True) + RMS_EPS) * g


def _rmsnorm_fwd(x, g, name):
    t = x.shape[0]
    return _rowmap(lambda xv, gv: (_rms(xv, gv),), [(x, D_MODEL, 0)], [g], [(D_MODEL, BF16)], [], name, 512, t)[0]


def _rmsnorm_bwd(dh, x, g, dres, name, follow=None):
    t = x.shape[0]

    def fn(dhv, xv, drv, gv, *_):
        _, vjp = jax.vjp(_rms, xv, gv)
        dx, dg = vjp(dhv)
        return dx + drv, dx + drv, dg

    fulls = [g] if follow is None else [g, follow]
    return _rowmap(fn, [(dh, D_MODEL, 0), (x, D_MODEL, 0), (dres, D_MODEL, 0)], fulls, [(D_MODEL, F32), (D_MODEL, BF16)],
                   [(1, D_MODEL)], name, 256, t)


def _loss_fwd_bwd(y, target, name):
    t = y.shape[0]

    def fn(yv, tv):
        e = yv - tv
        return e * (1.0 / D_MODEL), e * (1.0 / D_MODEL), (0.5 / D_MODEL) * jnp.sum(e * e, keepdims=True)

    return _rowmap(fn, [(y, D_MODEL, 0), (target, D_MODEL, 0)], [], [(D_MODEL, F32), (D_MODEL, BF16)], [(1, 1)], name, 512, t)


FFN_TILE = (1024, 512)


def _ffn_up(h, wg, wu, name):
    t, n = h.shape[0], wg.shape[1]
    tm, tn = FFN_TILE

    def body(h_ref, wg_ref, wu_ref, act_ref, du_ref, dg_ref):
        hv = h_ref[...]
        g = jnp.dot(hv, wg_ref[...], preferred_element_type=F32)
        u = jnp.dot(hv, wu_ref[...], preferred_element_type=F32)
        sg = jax.nn.sigmoid(g)
        silu = g * sg
        act_ref[...] = (silu * u).astype(BF16)
        du_ref[...] = silu.astype(BF16)
        dg_ref[...] = (u * (sg + silu * (1.0 - sg))).astype(BF16)

    o_spec = pl.BlockSpec((tm, tn), lambda i, j: (i, j))
    o_shape = jax.ShapeDtypeStruct((t, n), BF16)
    return pl.pallas_call(
        body, name=name, grid=(t // tm, n // tn),
        in_specs=[pl.BlockSpec((tm, D_MODEL), lambda i, j: (i, 0)), pl.BlockSpec((D_MODEL, tn), lambda i, j: (0, j)),
                  pl.BlockSpec((D_MODEL, tn), lambda i, j: (0, j))],
        out_specs=[o_spec] * 3, out_shape=[o_shape] * 3, compiler_params=_cparams(("parallel", "parallel")),
    )(h, wg, wu)


def _ffn_down_bwd(dy, wd, act_du, act_dg, name):
    t, n = dy.shape[0], wd.shape[0]
    tm, tn = FFN_TILE

    def body(dy_ref, wd_ref, adu_ref, adg_ref, dg_ref, du_ref):
        dact = lax.dot_general(dy_ref[...].astype(BF16), wd_ref[...], _DIMS["nt"], preferred_element_type=F32)
        du_ref[...] = (dact * adu_ref[...].astype(F32)).astype(BF16)
        dg_ref[...] = (dact * adg_ref[...].astype(F32)).astype(BF16)

    o_spec = pl.BlockSpec((tm, tn), lambda i, j: (i, j))
    o_shape = jax.ShapeDtypeStruct((t, n), BF16)
    return pl.pallas_call(
        body, name=name, grid=(t // tm, n // tn),
        in_specs=[pl.BlockSpec((tm, D_MODEL), lambda i, j: (i, 0)), pl.BlockSpec((tn, D_MODEL), lambda i, j: (j, 0)),
                  o_spec, o_spec],
        out_specs=[o_spec] * 2, out_shape=[o_shape] * 2, compiler_params=_cparams(("parallel", "parallel")),
    )(dy, wd, act_du, act_dg)


def _np_group_avg(width, group=HEAD_DIM):
    i = np.arange(width)
    return ((i[:, None] // group) == (i[None, :] // group)).astype(np.float32) / group


def _np_tile_fold(width, group=HEAD_DIM):
    return ((np.arange(width)[:, None] % group) == np.arange(group)[None, :]).astype(np.float32)


def _np_group_fold(width, group=HEAD_DIM, pad=128):
    return ((np.arange(width)[:, None] // group) == np.arange(pad)[None, :]).astype(np.float32)


def _np_rope_partner(width):
    i = np.arange(width)
    partner = np.where(i % 32 < 16, i + 16, i - 16)
    return (partner[:, None] == i[None, :]).astype(np.float32)


def _np_kv_expand():
    src = np.arange(KV_WIDTH)
    dst = np.arange(GROUP_WIDTH)
    return ((src[:, None] // HEAD_DIM == dst[None, :] // (4 * HEAD_DIM)) & (src[:, None] % HEAD_DIM == dst[None, :] % HEAD_DIM)).astype(np.float32)


def _np_rope_tables(n_heads):
    t = np.arange(SEQ)
    pos = {0: (t // GRID_W).astype(np.float32), 1: (t % GRID_W).astype(np.float32)}
    freqs = (ROPE_THETA ** (-np.arange(16, dtype=np.float32) / 16)).astype(np.float32)
    cos_parts, sin_parts = [], []
    for axis in (0, 1):
        ang = pos[axis][:, None] * freqs[None, :]
        c, s = np.cos(ang).astype(np.float32), np.sin(ang).astype(np.float32)
        cos_parts += [c, c]
        sin_parts += [-s, s]
    cos = np.concatenate(cos_parts, axis=1)
    sin = np.concatenate(sin_parts, axis=1)
    return np.tile(cos, (1, n_heads)), np.tile(sin, (1, n_heads))


def _np_t5_buckets(rel):
    nb = REL_BUCKETS // 2
    max_exact = nb // 2
    ret = np.where(rel > 0, nb, 0)
    n = np.abs(rel)
    nf = np.maximum(n, 1).astype(np.float32)
    large = max_exact + (np.log(nf / max_exact) / math.log(REL_MAX_DIST / max_exact) * (nb - max_exact)).astype(np.int32)
    large = np.minimum(large, nb - 1)
    return (ret + np.where(n < max_exact, n, large)).astype(np.int32)


DIL_QB = 128
DIL_WIN = DIL_QB + 2 * DIL_HALF


def _np_dil_buckets(dil):
    off = np.arange(DIL_WIN)[None, :] - DIL_HALF - np.arange(DIL_QB)[:, None]
    return _np_t5_buckets(off * dil)


def _dil_live_buckets(dil):
    off = np.arange(-DIL_HALF, DIL_HALF + 1)
    return sorted(set(_np_t5_buckets(off * dil).tolist()))


def _head_stat(x, mavg):
    return jnp.dot(x, mavg, precision=SPLIT3, preferred_element_type=F32)


def _gelu(x):
    return 0.5 * x * (1.0 + jnp.tanh(math.sqrt(2.0 / math.pi) * (x + 0.044715 * (x * x * x))))


def _sgu_pre(u_pre, v_pre, mavg):
    v = _gelu(v_pre)
    xc = v - _head_stat(v, mavg)
    vn = xc * lax.rsqrt(_head_stat(xc * xc, mavg) + LN_EPS)
    return _gelu(u_pre), vn


def _sgu_mix(w_ref, vnb, bm):
    lane_group = lax.broadcasted_iota(jnp.int32, (1, GROUP_WIDTH), 1) // HEAD_DIM
    mixed = bm
    for g in range(N_HEADS):
        r = jnp.dot(w_ref[g], vnb, preferred_element_type=F32)
        mixed = mixed + jnp.where(lane_group == g, r, 0.0)
    return mixed


SGU_TM = 512


def _sgu_fwd(z, w_s, bm, name):
    t = z.shape[0]
    mavg = jnp.asarray(_np_group_avg(GROUP_WIDTH))

    def body(u_ref, v_ref, w_ref, bm_ref, mavg_ref, y_ref):
        for c in range(SGU_TM // SGU_CHUNK):
            rows = pl.ds(c * SGU_CHUNK, SGU_CHUNK)
            u, vn = _sgu_pre(u_ref[rows, :], v_ref[rows, :], mavg_ref[...])
            y_ref[rows, :] = u * _sgu_mix(w_ref, vn.astype(BF16), bm_ref[...])

    full = lambda a: pl.BlockSpec(a.shape, lambda i: (0,) * a.ndim)
    return pl.pallas_call(
        body, name=name, grid=(t // SGU_TM,),
        in_specs=[pl.BlockSpec((SGU_TM, GROUP_WIDTH), lambda i: (i, 0)), pl.BlockSpec((SGU_TM, GROUP_WIDTH), lambda i: (i, 1)),
                  full(w_s), full(bm), full(mavg)],
        out_specs=pl.BlockSpec((SGU_TM, GROUP_WIDTH), lambda i: (i, 0)),
        out_shape=jax.ShapeDtypeStruct((t, GROUP_WIDTH), F32), compiler_params=_cparams(("parallel",)),
    )(z, z, w_s, bm, mavg)


def _sgu_bwd(z, dy, w_s, w_s_t, bm, name):
    t = z.shape[0]
    mavg = jnp.asarray(_np_group_avg(GROUP_WIDTH))
    gfold = jnp.asarray(_np_group_fold(GROUP_WIDTH))

    def body(u_ref, v_ref, dy_ref, w_ref, wt_ref, bm_ref, mavg_ref, gfold_ref, du_ref, dv_ref, dw_ref, dbs_ref, dbm_ref):
        @pl.when(pl.program_id(0) == 0)
        def _():
            dw_ref[...] = jnp.zeros_like(dw_ref)
            dbm_ref[...] = jnp.zeros_like(dbm_ref)

        lane_group = lax.broadcasted_iota(jnp.int32, (1, GROUP_WIDTH), 1) // HEAD_DIM
        for c in range(SGU_TM // SGU_CHUNK):
            rows = pl.ds(c * SGU_CHUNK, SGU_CHUNK)
            (u, vn), pre_vjp = jax.vjp(functools.partial(_sgu_pre, mavg=mavg_ref[...]), u_ref[rows, :], v_ref[rows, :])
            vnb = vn.astype(BF16)
            mixed = _sgu_mix(w_ref, vnb, bm_ref[...])
            dyv = dy_ref[rows, :]
            dmixed = dyv * u
            dbm_ref[...] += dmixed
            dvn = jnp.zeros_like(vn)
            for g in range(N_HEADS):
                dm_g = jnp.where(lane_group == g, dmixed, 0.0).astype(BF16)
                dw_ref[g] += lax.dot_general(dm_g, vnb, _DIMS["nt"], preferred_element_type=F32)
                dvn = dvn + jnp.dot(wt_ref[g], dm_g, preferred_element_type=F32)
            du_pre, dv_pre = pre_vjp((dyv * mixed, dvn))
            du_ref[rows, :] = du_pre
            dv_ref[rows, :] = dv_pre

        @pl.when(pl.program_id(0) == t // SGU_TM - 1)
        def _():
            dbs_ref[...] = jnp.dot(dbm_ref[...], gfold_ref[...], precision=HI, preferred_element_type=F32)

    full = lambda a: pl.BlockSpec(a.shape, lambda i: (0,) * a.ndim)
    row = pl.BlockSpec((SGU_TM, GROUP_WIDTH), lambda i: (i, 0))
    return pl.pallas_call(
        body, name=name, grid=(t // SGU_TM,),
        in_specs=[row, pl.BlockSpec((SGU_TM, GROUP_WIDTH), lambda i: (i, 1)), row, full(w_s), full(w_s_t), full(bm), full(mavg),
                  full(gfold)],
        out_specs=[row, row, pl.BlockSpec((N_HEADS, SGU_CHUNK, SGU_CHUNK), lambda i: (0, 0, 0)),
                   pl.BlockSpec((SGU_CHUNK, 128), lambda i: (0, 0))],
        out_shape=[jax.ShapeDtypeStruct((t, GROUP_WIDTH), F32)] * 2 + [jax.ShapeDtypeStruct((N_HEADS, SGU_CHUNK, SGU_CHUNK), F32),
                                                                      jax.ShapeDtypeStruct((SGU_CHUNK, 128), F32)],
        scratch_shapes=[pltpu.VMEM((SGU_CHUNK, GROUP_WIDTH), F32)],
        compiler_params=_cparams(("arbitrary",)),
    )(z, z, dy, w_s, w_s_t, bm, mavg, gfold)


def _head_lanes():
    lane_head = lax.broadcasted_iota(jnp.int32, (1, 2 * HEAD_DIM), 1) // HEAD_DIM
    return lane_head == 0, lane_head == 1


def _stack_heads(x2):
    h0, h1 = _head_lanes()
    zero = jnp.zeros_like(x2)
    return jnp.concatenate([jnp.where(h0, x2, zero), jnp.where(h1, x2, zero)], axis=0)


def _unstack_heads(y):
    r = y.shape[0] // 2
    h0, _ = _head_lanes()
    return jnp.where(h0, y[:r], y[r:])


def _pair_softmax(qs, k2, biases, valid):
    s = lax.dot_general(qs, k2, _DIMS["nt"], preferred_element_type=F32)
    if biases is not None:
        s = s + jnp.concatenate(biases, axis=0)
    if valid is not None:
        s = jnp.where(jnp.concatenate([valid, valid], axis=0), s, -1e30)
    m = jnp.max(s, axis=-1, keepdims=True)
    e = jnp.exp(s - m)
    l = jnp.sum(e, axis=-1, keepdims=True)
    return e / l, m + jnp.log(l)


def _attn_pair_fwd(q2, k2, v2, biases, valid):
    p, lse = _pair_softmax(_stack_heads(q2), k2, biases, valid)
    o = jnp.dot(p.astype(BF16), v2, preferred_element_type=F32)
    return _unstack_heads(o), _unstack_heads(jnp.broadcast_to(lse, o.shape))


def _attn_pair_bwd(q2, k2, v2, biases, valid, do2, dlse2):
    r = q2.shape[0]
    qs = _stack_heads(q2)
    p, _ = _pair_softmax(qs, k2, biases, valid)
    dos = _stack_heads(do2).astype(BF16)
    dp = lax.dot_general(dos, v2, _DIMS["nt"], preferred_element_type=F32)
    delta = jnp.sum(dp * p, axis=-1, keepdims=True)
    if dlse2 is not None:
        delta = delta - jnp.sum(_stack_heads(dlse2), axis=-1, keepdims=True)
    ds = p * (dp - delta)
    dsb = ds.astype(BF16)
    dq2 = _unstack_heads(jnp.dot(dsb, k2, preferred_element_type=F32))
    dk2 = lax.dot_general(dsb, qs, _DIMS["tn"], preferred_element_type=F32)
    dv2 = lax.dot_general(p.astype(BF16), dos, _DIMS["tn"], preferred_element_type=F32)
    return dq2, dk2, dv2, [ds[:r], ds[r:]]


def _dil_valid(r0, length):
    row = lax.broadcasted_iota(jnp.int32, (DIL_QB, DIL_WIN), 0)
    col = lax.broadcasted_iota(jnp.int32, (DIL_QB, DIL_WIN), 1)
    off = col - DIL_HALF - row
    kpos = r0 - DIL_HALF + col
    return (jnp.abs(off) <= DIL_HALF) & (kpos >= 0) & (kpos < length)


def _dil_build_bias(tab_ref, bkt_ref, bias_ref, dil):
    bkt = bkt_ref[...]
    for h in range(N_HEADS):
        acc = jnp.zeros((DIL_QB, DIL_WIN), F32)
        for b in _dil_live_buckets(dil):
            acc = jnp.where(bkt == b, tab_ref[b, h], acc)
        bias_ref[h] = acc


def _dil_fill_pad(pad_ref, src_ref, length):
    zeros = jnp.zeros((DIL_HALF, GROUP_WIDTH), pad_ref.dtype)
    pad_ref[pl.ds(0, DIL_HALF), :] = zeros
    pad_ref[pl.ds(DIL_HALF + length, DIL_HALF), :] = zeros
    pad_ref[pl.ds(DIL_HALF, length), :] = src_ref[...]


def _dil_specs(bsz, length, dil):
    view = lambda a: a.reshape(bsz, length, dil * GROUP_WIDTH)
    blk = pl.BlockSpec((None, DIL_QB, GROUP_WIDTH), lambda b, rho, i: (b, i, rho))
    seq = pl.BlockSpec((None, length, GROUP_WIDTH), lambda b, rho, i: (b, 0, rho))
    return view, blk, seq


def _dil_fwd(qb, kb, vb, table, dil, name):
    length = SEQ // dil
    bsz = qb.shape[0] // length
    bkt = jnp.asarray(_np_dil_buckets(dil))
    view, blk, seq = _dil_specs(bsz, length, dil)

    def body(tab_ref, bkt_ref, q_ref, k_ref, v_ref, o_ref, lse_ref, kpad, vpad, bias_ref):
        i = pl.program_id(2)

        @pl.when((pl.program_id(0) == 0) & (pl.program_id(1) == 0) & (i == 0))
        def _():
            _dil_build_bias(tab_ref, bkt_ref, bias_ref, dil)

        @pl.when(i == 0)
        def _():
            _dil_fill_pad(kpad, k_ref, length)
            _dil_fill_pad(vpad, v_ref, length)

        r0 = pl.multiple_of(i * DIL_QB, DIL_QB)
        valid = _dil_valid(r0, length)
        for m in range(N_HEADS // 2):
            lanes = pl.ds(m * 128, 128)
            o2, lse2 = _attn_pair_fwd(q_ref[:, lanes], kpad[pl.ds(r0, DIL_WIN), lanes], vpad[pl.ds(r0, DIL_WIN), lanes],
                                      (bias_ref[2 * m], bias_ref[2 * m + 1]), valid)
            o_ref[:, lanes] = o2
            lse_ref[:, lanes] = lse2

    out = jax.ShapeDtypeStruct((bsz, length, dil * GROUP_WIDTH), F32)
    o, lse = pl.pallas_call(
        body, name=name, grid=(bsz, dil, length // DIL_QB),
        in_specs=[pl.BlockSpec(memory_space=pltpu.SMEM), pl.BlockSpec(bkt.shape, lambda b, rho, i: (0, 0)), blk, seq, seq],
        out_specs=[blk, blk], out_shape=[out, out],
        scratch_shapes=[pltpu.VMEM((length + 2 * DIL_HALF, GROUP_WIDTH), BF16), pltpu.VMEM((length + 2 * DIL_HALF, GROUP_WIDTH), BF16),
                        pltpu.VMEM((N_HEADS, DIL_QB, DIL_WIN), F32)],
        compiler_params=_cparams(("arbitrary", "arbitrary", "arbitrary")),
    )(table, bkt, view(qb), view(kb), view(vb))
    return o.reshape(qb.shape), lse.reshape(qb.shape)


def _dil_bwd(qb, kb, vb, do, dlse, table, dil, name):
    length = SEQ // dil
    bsz = qb.shape[0] // length
    nqb = length // DIL_QB
    bkt = jnp.asarray(_np_dil_buckets(dil))
    view, blk, seq = _dil_specs(bsz, length, dil)

    def body(tab_ref, bkt_ref, q_ref, k_ref, v_ref, do_ref, dlse_ref, dq_ref, dk_ref, dv_ref, dsc_ref, kpad, vpad, bias_ref):
        i = pl.program_id(2)

        @pl.when((pl.program_id(0) == 0) & (pl.program_id(1) == 0) & (i == 0))
        def _():
            _dil_build_bias(tab_ref, bkt_ref, bias_ref, dil)
            dsc_ref[...] = jnp.zeros_like(dsc_ref)

        @pl.when(i == 0)
        def _():
            _dil_fill_pad(kpad, k_ref, length)
            _dil_fill_pad(vpad, v_ref, length)
            dk_ref[...] = jnp.zeros_like(dk_ref)
            dv_ref[...] = jnp.zeros_like(dv_ref)

        r0 = pl.multiple_of(i * DIL_QB, DIL_QB)
        valid = _dil_valid(r0, length)
        for m in range(N_HEADS // 2):
            lanes = pl.ds(m * 128, 128)
            dq2, dk2, dv2, ds_heads = _attn_pair_bwd(
                q_ref[:, lanes], kpad[pl.ds(r0, DIL_WIN), lanes], vpad[pl.ds(r0, DIL_WIN), lanes],
                (bias_ref[2 * m], bias_ref[2 * m + 1]), valid, do_ref[:, lanes], dlse_ref[:, lanes])
            dq_ref[:, lanes] = dq2
            dsc_ref[2 * m] += ds_heads[0]
            dsc_ref[2 * m + 1] += ds_heads[1]
            for first, size, live in ((0, DIL_HALF, i >= 1), (DIL_HALF, DIL_QB, None), (DIL_HALF + DIL_QB, DIL_HALF, i <= nqb - 2)):
                def add(first=first, size=size, dk2=dk2, dv2=dv2, lanes=lanes):
                    rows = pl.ds(pl.multiple_of(r0 - DIL_HALF + first, DIL_HALF), size)
                    dk_ref[rows, lanes] += dk2[first:first + size]
                    dv_ref[rows, lanes] += dv2[first:first + size]
                if live is None:
                    add()
                else:
                    pl.when(live)(add)

    out = jax.ShapeDtypeStruct((bsz, length, dil * GROUP_WIDTH), F32)
    dsc_shape = (N_HEADS, DIL_QB, DIL_WIN)
    dq, dk, dv, dsc = pl.pallas_call(
        body, name=name, grid=(bsz, dil, nqb),
        in_specs=[pl.BlockSpec(memory_space=pltpu.SMEM), pl.BlockSpec(bkt.shape, lambda b, rho, i: (0, 0)), blk, seq, seq, blk, blk],
        out_specs=[blk, seq, seq, pl.BlockSpec(dsc_shape, lambda b, rho, i: (0, 0, 0))],
        out_shape=[out, out, out, jax.ShapeDtypeStruct(dsc_shape, F32)],
        scratch_shapes=[pltpu.VMEM((length + 2 * DIL_HALF, GROUP_WIDTH), BF16), pltpu.VMEM((length + 2 * DIL_HALF, GROUP_WIDTH), BF16),
                        pltpu.VMEM(dsc_shape, F32)],
        compiler_params=_cparams(("arbitrary", "arbitrary", "arbitrary")),
    )(table, bkt, view(qb), view(kb), view(vb), view(do), view(dlse))
    return dq.reshape(qb.shape), dk.reshape(qb.shape), dv.reshape(qb.shape), dsc


def _headnorm(x, g, mavg):
    return x * lax.rsqrt(_head_stat(x * x, mavg) + RMS_EPS) * g


def _fold_gain(dg_full, fold):
    return jnp.dot(jnp.broadcast_to(dg_full, (8, dg_full.shape[1])), fold, precision=HI, preferred_element_type=F32)


def _bprep_fn(qp, kp, gq, gk, mavg):
    return _headnorm(qp, gq, mavg) * (HEAD_DIM ** -0.5), _headnorm(kp, gk, mavg)


def _dil_layout(dil):
    return None if dil == 1 else Strided(dil)


def _bprep_fwd(z, gq, gk, name):
    mavg = jnp.asarray(_np_group_avg(GROUP_WIDTH))

    def fn(qp, kp, vp, gqv, gkv, mv):
        qb, kb = _bprep_fn(qp, kp, gqv, gkv, mv)
        return (qb, kb, vp) * len(DIL_PATTERNS)

    w = GROUP_WIDTH
    outs = [(w, BF16, _dil_layout(dil)) for _, dil in DIL_PATTERNS for _ in range(3)]
    res = _rowmap(fn, [(z, w, 2), (z, w, 3), (z, w, 4)], [gq, gk, mavg], outs, [], name, 512, z.shape[0])
    return [res[3 * i:3 * i + 3] for i in range(len(DIL_PATTERNS))]


def _bprep_bwd(z, dqs, dks, dvs, gq, gk, name):
    mavg = jnp.asarray(_np_group_avg(GROUP_WIDTH))
    fold = jnp.asarray(_np_tile_fold(GROUP_WIDTH))

    def fn(qp, kp, dq0, dq1, dq2, dk0, dk1, dk2, dv0, dv1, dv2, gqv, gkv, mv, fv):
        _, vjp = jax.vjp(functools.partial(_bprep_fn, mavg=mv), qp, kp, gqv, gkv)
        dqp, dkp, dgq, dgk = vjp((dq0 + dq1 + dq2, dk0 + dk1 + dk2))
        return dqp, dkp, dv0 + dv1 + dv2, _fold_gain(dgq, fv), _fold_gain(dgk, fv)

    w = GROUP_WIDTH
    rows = [(z, w, 2), (z, w, 3)] + _pattern_rows(dqs) + _pattern_rows(dks) + _pattern_rows(dvs)
    return _rowmap(fn, rows, [gq, gk, mavg, fold], [(w, F32)] * 3, [(8, HEAD_DIM)] * 2, name, 256, z.shape[0])


def _mixture_fn(o0, o1, o2, l0, l1, l2):
    m = lax.stop_gradient(jnp.maximum(jnp.maximum(l0, l1), l2))
    e0, e1, e2 = jnp.exp(l0 - m), jnp.exp(l1 - m), jnp.exp(l2 - m)
    return (e0 * o0 + e1 * o1 + e2 * o2) / (e0 + e1 + e2)


def _pattern_rows(arrs):
    return [(a, GROUP_WIDTH, 0) if dil == 1 else (a, GROUP_WIDTH, 0, Strided(dil)) for a, (_, dil) in zip(arrs, DIL_PATTERNS)]


def _mixture_fwd(os_, ls_, name):
    n = os_[0].shape[0]
    return _rowmap(lambda *v: (_mixture_fn(*v),), _pattern_rows(os_) + _pattern_rows(ls_), [], [(GROUP_WIDTH, F32)], [], name, 512, n)[0]


def _mixture_bwd(os_, ls_, dy, name):
    w = GROUP_WIDTH

    def fn(*v):
        _, vjp = jax.vjp(_mixture_fn, *v[:6])
        return vjp(v[6])

    outs = [(w, F32, _dil_layout(dil)) for _ in range(2) for _, dil in DIL_PATTERNS]
    return _rowmap(fn, _pattern_rows(os_) + _pattern_rows(ls_) + [(dy, w, 0)], [], outs, [], name, 512, dy.shape[0])


def _relbias_fold(dscs, name):
    bkts = [jnp.asarray(_np_dil_buckets(dil)) for _, dil in DIL_PATTERNS]
    npat = len(DIL_PATTERNS)

    def body(*refs):
        bkt_refs, d_refs, o_ref = refs[:npat], refs[npat:-1], refs[-1]
        row = lax.broadcasted_iota(jnp.int32, (REL_BUCKETS, 128), 0)
        lane = lax.broadcasted_iota(jnp.int32, (REL_BUCKETS, 128), 1)
        out = jnp.zeros((REL_BUCKETS, 128), F32)
        for p, (_, dil) in enumerate(DIL_PATTERNS):
            bkt = bkt_refs[p][...]
            for h in range(N_HEADS):
                d = d_refs[2 * p][h] + d_refs[2 * p + 1][h]
                for b in _dil_live_buckets(dil):
                    val = jnp.sum(jnp.where(bkt == b, d, 0.0), keepdims=True)
                    out = out + jnp.where((row == b) & (lane == h), val, 0.0)
        o_ref[...] = out

    return pl.pallas_call(
        body, name=name, out_shape=jax.ShapeDtypeStruct((REL_BUCKETS, 128), F32), compiler_params=_cparams(),
    )(*bkts, *dscs)


DPREP_TM = 512


def _dprep_fn(qp, kp, vp, gq, gk, cq, sq, ck, sk, mavg_q, mavg_k, perm_q, perm_k, expand):
    rot = lambda x, perm: jnp.dot(x, perm, precision=SPLIT3, preferred_element_type=F32)
    qn = _headnorm(qp, gq, mavg_q)
    kn = _headnorm(kp, gk, mavg_k)
    qr = (qn * cq + rot(qn, perm_q) * sq) * (HEAD_DIM ** -0.5)
    kr = kn * ck + rot(kn, perm_k) * sk
    return qr, rot(kr, expand), rot(vp, expand)


def _dprep_consts():
    cq, sq = _np_rope_tables(N_HEADS)
    ck, sk = _np_rope_tables(KV_WIDTH // HEAD_DIM)
    tables = [jnp.asarray(a) for a in (cq, sq, ck, sk)]
    mats = [jnp.asarray(a) for a in (_np_group_avg(GROUP_WIDTH), _np_group_avg(KV_WIDTH), _np_rope_partner(GROUP_WIDTH),
                                      _np_rope_partner(KV_WIDTH), _np_kv_expand())]
    per = SEQ // DPREP_TM
    w, kw = GROUP_WIDTH, KV_WIDTH
    table_rows = [(tables[0], w, 0, per), (tables[1], w, 0, per), (tables[2], kw, 0, per), (tables[3], kw, 0, per)]
    return table_rows, mats


def _dprep_fwd(z, gq, gk, name):
    table_rows, mats = _dprep_consts()
    w, kw = GROUP_WIDTH, KV_WIDTH

    def fn(qp, kp, vp, cq, sq, ck, sk, gqv, gkv, *m):
        return _dprep_fn(qp, kp, vp, gqv, gkv, cq, sq, ck, sk, *m)

    return _rowmap(fn, [(z, w, 7), (z, kw, 32), (z, kw, 33)] + table_rows, [gq, gk] + mats, [(w, BF16)] * 3, [], name,
                   DPREP_TM, z.shape[0])


def _dprep_bwd(z, dq, dkx, dvx, gq, gk, name):
    table_rows, mats = _dprep_consts()
    fold_q = jnp.asarray(_np_tile_fold(GROUP_WIDTH))
    fold_k = jnp.asarray(_np_tile_fold(KV_WIDTH))
    w, kw = GROUP_WIDTH, KV_WIDTH

    def fn(qp, kp, vp, dqv, dkv, dvv, cq, sq, ck, sk, gqv, gkv, fq, fk, *m):
        f = lambda a, b, c, d, e: _dprep_fn(a, b, c, d, e, cq, sq, ck, sk, *m)
        _, vjp = jax.vjp(f, qp, kp, vp, gqv, gkv)
        dqp, dkp, dvp, dgq, dgk = vjp((dqv, dkv, dvv))
        return dqp, dkp, dvp, _fold_gain(dgq, fq), _fold_gain(dgk, fk)

    return _rowmap(fn, [(z, w, 7), (z, kw, 32), (z, kw, 33), (dq, w, 0), (dkx, w, 0), (dvx, w, 0)] + table_rows,
                   [gq, gk, fold_q, fold_k] + mats, [(w, F32), (kw, F32), (kw, F32)], [(8, HEAD_DIM)] * 2, name,
                   DPREP_TM, z.shape[0])


GQA_QB = 256


def _gqa_fwd(q, kx, vx, name):
    bsz = q.shape[0]
    blk = pl.BlockSpec((None, GQA_QB, GROUP_WIDTH), lambda b, i: (b, i, 0))
    seq = pl.BlockSpec((None, SEQ, GROUP_WIDTH), lambda b, i: (b, 0, 0))

    def body(q_ref, k_ref, v_ref, o_ref):
        for m in range(N_HEADS // 2):
            lanes = pl.ds(m * 128, 128)
            o_ref[:, lanes] = _attn_pair_fwd(q_ref[:, lanes], k_ref[:, lanes], v_ref[:, lanes], None, None)[0]

    return pl.pallas_call(
        body, name=name, grid=(bsz, SEQ // GQA_QB), in_specs=[blk, seq, seq], out_specs=blk,
        out_shape=jax.ShapeDtypeStruct((bsz, SEQ, GROUP_WIDTH), F32), compiler_params=_cparams(("parallel", "parallel")),
    )(q, kx, vx)


def _gqa_bwd(q, kx, vx, do, name):
    bsz = q.shape[0]
    blk = pl.BlockSpec((None, GQA_QB, GROUP_WIDTH), lambda b, i: (b, i, 0))
    seq = pl.BlockSpec((None, SEQ, GROUP_WIDTH), lambda b, i: (b, 0, 0))

    def body(q_ref, k_ref, v_ref, do_ref, dq_ref, dk_ref, dv_ref):
        @pl.when(pl.program_id(1) == 0)
        def _():
            dk_ref[...] = jnp.zeros_like(dk_ref)
            dv_ref[...] = jnp.zeros_like(dv_ref)

        for m in range(N_HEADS // 2):
            lanes = pl.ds(m * 128, 128)
            dq2, dk2, dv2, _ = _attn_pair_bwd(q_ref[:, lanes], k_ref[:, lanes], v_ref[:, lanes], None, None, do_ref[:, lanes], None)
            dq_ref[:, lanes] = dq2
            dk_ref[:, lanes] += dk2
            dv_ref[:, lanes] += dv2

    out = jax.ShapeDtypeStruct((bsz, SEQ, GROUP_WIDTH), F32)
    return pl.pallas_call(
        body, name=name, grid=(bsz, SEQ // GQA_QB), in_specs=[blk, seq, seq, blk], out_specs=[blk, seq, seq],
        out_shape=[out, out, out], compiler_params=_cparams(("parallel", "arbitrary")),
    )(q, kx, vx, do)


CONV_TILE = 64
CONV_LEAD = 16
CONV_WINDOW = CONV_TILE + 32


def _glu(a, g):
    return a * jax.nn.sigmoid(g)


def _conv_post(c, b, ln_g, ln_b):
    x = c + b
    xc = x - jnp.mean(x, axis=-1, keepdims=True)
    y = xc * lax.rsqrt(jnp.mean(xc * xc, axis=-1, keepdims=True) + LN_EPS) * ln_g + ln_b
    return y * jax.nn.sigmoid(y)


def _conv_shifts(win):
    out = []
    for phase in range(8):
        rolled = win if phase == 0 else pltpu.roll(win, CONV_WINDOW - phase, 0)
        for base in range(0, CONV_WINDOW - CONV_TILE + 1, 8):
            if 1 <= base + phase <= CONV_WIDTH:
                out.append((base + phase, rolled[base:base + CONV_TILE]))
    return out


def _conv_fill(pad_ref, value_of_tile):
    zeros = jnp.zeros((CONV_LEAD, GROUP_WIDTH), F32)
    pad_ref[pl.ds(0, CONV_LEAD), :] = zeros
    pad_ref[pl.ds(CONV_LEAD + SEQ, CONV_LEAD), :] = zeros

    def step(t, carry):
        r0 = pl.multiple_of(t * CONV_TILE, CONV_TILE)
        pad_ref[pl.ds(CONV_LEAD + r0, CONV_TILE), :] = value_of_tile(r0)
        return carry

    lax.fori_loop(0, SEQ // CONV_TILE, step, 0)


def _conv_tile(pad_ref, w_ref, r0, flip):
    acc = jnp.zeros((CONV_TILE, GROUP_WIDTH), F32)
    for offset, rows in _conv_shifts(pad_ref[pl.ds(r0, CONV_WINDOW), :]):
        k = (CONV_WIDTH - offset) if flip else (offset - 1)
        acc = acc + w_ref[pl.ds(k, 1), :] * rows
    return acc


def _conv_fwd(z3, w, b, ln_g, ln_b, name):
    bsz = z3.shape[0]
    seq = lambda cb: pl.BlockSpec((None, SEQ, GROUP_WIDTH), functools.partial(lambda i, cb: (i, 0, cb), cb=cb))
    full = lambda a: pl.BlockSpec(a.shape, lambda i: (0,) * a.ndim)

    def body(a_ref, g_ref, w_ref, b_ref, lg_ref, lb_ref, y_ref, pad_ref):
        _conv_fill(pad_ref, lambda r0: _glu(a_ref[pl.ds(r0, CONV_TILE), :], g_ref[pl.ds(r0, CONV_TILE), :]))

        def step(t, carry):
            r0 = pl.multiple_of(t * CONV_TILE, CONV_TILE)
            y_ref[pl.ds(r0, CONV_TILE), :] = _conv_post(_conv_tile(pad_ref, w_ref, r0, False), b_ref[...], lg_ref[...], lb_ref[...])
            return carry

        lax.fori_loop(0, SEQ // CONV_TILE, step, 0)

    return pl.pallas_call(
        body, name=name, grid=(bsz,), in_specs=[seq(5), seq(6), full(w), full(b), full(ln_g), full(ln_b)], out_specs=seq(0),
        out_shape=jax.ShapeDtypeStruct((bsz, SEQ, GROUP_WIDTH), F32),
        scratch_shapes=[pltpu.VMEM((SEQ + 2 * CONV_LEAD, GROUP_WIDTH), F32)], compiler_params=_cparams(("parallel",)),
    )(z3, z3, w, b, ln_g, ln_b)


def _conv_bwd(z3, dy, w, b, ln_g, ln_b, name):
    bsz = z3.shape[0]
    seq = lambda cb: pl.BlockSpec((None, SEQ, GROUP_WIDTH), functools.partial(lambda i, cb: (i, 0, cb), cb=cb))
    full = lambda a: pl.BlockSpec(a.shape, lambda i: (0,) * a.ndim)
    vec = pl.BlockSpec((1, GROUP_WIDTH), lambda i: (0, 0))

    def body(a_ref, g_ref, dy_ref, w_ref, b_ref, lg_ref, lb_ref, da_ref, dg_ref, dw_ref, db_ref, dlg_ref, dlb_ref, hpad, dpad, dw8):
        @pl.when(pl.program_id(0) == 0)
        def _():
            dw8[...] = jnp.zeros_like(dw8)
            db_ref[...] = jnp.zeros_like(db_ref)
            dlg_ref[...] = jnp.zeros_like(dlg_ref)
            dlb_ref[...] = jnp.zeros_like(dlb_ref)

        _conv_fill(hpad, lambda r0: _glu(a_ref[pl.ds(r0, CONV_TILE), :], g_ref[pl.ds(r0, CONV_TILE), :]))
        zeros = jnp.zeros((CONV_LEAD, GROUP_WIDTH), F32)
        dpad[pl.ds(0, CONV_LEAD), :] = zeros
        dpad[pl.ds(CONV_LEAD + SEQ, CONV_LEAD), :] = zeros

        def through_post(t, carry):
            r0 = pl.multiple_of(t * CONV_TILE, CONV_TILE)
            conv = _conv_tile(hpad, w_ref, r0, False)
            _, vjp = jax.vjp(_conv_post, conv, b_ref[...], lg_ref[...], lb_ref[...])
            dconv, db, dlg, dlb = vjp(dy_ref[pl.ds(r0, CONV_TILE), :])
            db_ref[...] += db
            dlg_ref[...] += dlg
            dlb_ref[...] += dlb
            dpad[pl.ds(CONV_LEAD + r0, CONV_TILE), :] = dconv
            for offset, rows in _conv_shifts(hpad[pl.ds(r0, CONV_WINDOW), :]):
                prod = dconv * rows
                part = prod[0:8]
                for j in range(1, CONV_TILE // 8):
                    part = part + prod[8 * j:8 * j + 8]
                dw8[offset - 1] += part
            return carry

        lax.fori_loop(0, SEQ // CONV_TILE, through_post, 0)

        def through_glu(t, carry):
            r0 = pl.multiple_of(t * CONV_TILE, CONV_TILE)
            dh = _conv_tile(dpad, w_ref, r0, True)
            rows = pl.ds(r0, CONV_TILE)
            _, vjp = jax.vjp(_glu, a_ref[rows, :], g_ref[rows, :])
            da, dg = vjp(dh)
            da_ref[rows, :] = da
            dg_ref[rows, :] = dg
            return carry

        lax.fori_loop(0, SEQ // CONV_TILE, through_glu, 0)
        dw_ref[...] = jnp.sum(dw8[...], axis=1)

    out = jax.ShapeDtypeStruct((bsz, SEQ, GROUP_WIDTH), F32)
    v = jax.ShapeDtypeStruct((1, GROUP_WIDTH), F32)
    return pl.pallas_call(
        body, name=name, grid=(bsz,), in_specs=[seq(5), seq(6), seq(0), full(w), full(b), full(ln_g), full(ln_b)],
        out_specs=[seq(0), seq(0), pl.BlockSpec((CONV_WIDTH, GROUP_WIDTH), lambda i: (0, 0)), vec, vec, vec],
        out_shape=[out, out, jax.ShapeDtypeStruct((CONV_WIDTH, GROUP_WIDTH), F32), v, v, v],
        scratch_shapes=[pltpu.VMEM((SEQ + 2 * CONV_LEAD, GROUP_WIDTH), F32), pltpu.VMEM((SEQ + 2 * CONV_LEAD, GROUP_WIDTH), F32),
                        pltpu.VMEM((CONV_WIDTH, 8, GROUP_WIDTH), F32)],
        compiler_params=_cparams(("arbitrary",)),
    )(z3, z3, dy, w, b, ln_g, ln_b)


def _mixnorm_fwd(ys, gains, name):
    w = GROUP_WIDTH

    def fn(*v):
        return (jnp.concatenate([_rms(v[i], v[4 + i]) for i in range(4)], axis=-1),)

    return _rowmap(fn, [(y, w, 0) for y in ys], list(gains), [(4 * w, BF16)], [], name, 512, ys[0].shape[0])[0]


def _mixnorm_bwd(dyn, ys, gains, name, follow=None):
    w = GROUP_WIDTH
    gains = list(gains) if follow is None else [*gains, follow]

    def fn(*v):
        dys, dgs = [], []
        for i in range(4):
            _, vjp = jax.vjp(_rms, v[4 + i], v[8 + i])
            dy, dg = vjp(v[i])
            dys.append(dy)
            dgs.append(dg)
        return (*dys, *dgs)

    rows = [(dyn, w, i) for i in range(4)] + [(y, w, 0) for y in ys]
    return _rowmap(fn, rows, list(gains), [(w, F32)] * 4, [(1, w)] * 4, name, 512, dyn.shape[0])


def _adamw_fn(w, g, m, v):
    m = ADAM_B1 * m + (1.0 - ADAM_B1) * g
    v = ADAM_B2 * v + (1.0 - ADAM_B2) * (g * g)
    m_hat = m / (1.0 - ADAM_B1 ** ADAM_STEP)
    v_hat = v / (1.0 - ADAM_B2 ** ADAM_STEP)
    delta = -ADAM_LR * (m_hat / (jnp.sqrt(v_hat) + ADAM_EPS) + ADAM_WD * w)
    return delta, m, v


def _adamw(w, g, m, v, name):
    r, c = w.shape
    tm = _pick(r, (256, 128, 64, 32, 16, 8))
    return _rowmap(_adamw_fn, [(a, c, 0) for a in (w, g, m, v)], [], [(c, F32)] * 3, [], name, tm, r)


def _layer_params(l, small, big):
    tile_row = lambda g, n: jnp.tile(g, n)[None, :]
    row = lambda g: g[None, :]
    w_s = small["sgu_w"][l].astype(BF16)
    return dict(
        norm1_g=row(small["norm1_g"][l]), norm2_g=row(small["norm2_g"][l]),
        w_s=w_s, w_s_t=jnp.swapaxes(w_s, 1, 2), bm=jnp.repeat(small["sgu_b"][l].T, HEAD_DIM, axis=1),
        gq_dil=tile_row(small["dil_qn_g"][l], N_HEADS), gk_dil=tile_row(small["dil_kn_g"][l], N_HEADS),
        conv_w=small["conv_w"][l], conv_b=row(small["conv_b"][l]), conv_ln_g=row(small["conv_ln_g"][l]),
        conv_ln_b=row(small["conv_ln_b"][l]),
        gq_gqa=tile_row(small["gqa_qn_g"][l], N_HEADS), gk_gqa=tile_row(small["gqa_kn_g"][l], KV_WIDTH // HEAD_DIM),
        mix_g=[row(small["mix_norm_g"][l][i * GROUP_WIDTH:(i + 1) * GROUP_WIDTH]) for i in range(4)],
        big=big,
    )


def _layer_fwd(x, p, table, bsz, tag):
    t = x.shape[0]
    seq3 = lambda a: a.reshape(bsz, SEQ, a.shape[-1])
    flat = lambda a: a.reshape(t, a.shape[-1])
    h1 = _rmsnorm_fwd(x, p["norm1_g"], tag + "rms1")
    z = _matmul([(h1, p["big"]("w_in", h1))], "nn", F32, tag + "mm_z")
    y_a = _sgu_fwd(z, p["w_s"], p["bm"], tag + "sgu_fwd")
    dil_qkv = _bprep_fwd(z, p["gq_dil"], p["gk_dil"], tag + "dil_prep")
    outs, lses = [], []
    for (_, dil), (qb, kb, vb) in zip(DIL_PATTERNS, dil_qkv):
        o, lse = _dil_fwd(qb, kb, vb, table, dil, f"{tag}dil{dil}_fwd")
        outs.append(o)
        lses.append(lse)
    y_b = _mixture_fwd(outs, lses, tag + "dil_mix")
    y_c = flat(_conv_fwd(seq3(z), p["conv_w"], p["conv_b"], p["conv_ln_g"], p["conv_ln_b"], tag + "conv_fwd"))
    qd, kx, vx = _dprep_fwd(z, p["gq_gqa"], p["gk_gqa"], tag + "gqa_prep")
    y_d = flat(_gqa_fwd(seq3(qd), seq3(kx), seq3(vx), tag + "gqa_fwd"))
    ys = [y_a, y_b, y_c, y_d]
    yn = _mixnorm_fwd(ys, p["mix_g"], tag + "mixnorm")
    x_mid = _matmul([(yn, p["big"]("w_out", yn))], "nn", F32, tag + "mm_out", residual=x)
    h2 = _rmsnorm_fwd(x_mid, p["norm2_g"], tag + "rms2")
    act, act_du, act_dg = _ffn_up(h2, p["big"]("w_gate", h2), p["big"]("w_up", h2), tag + "ffn_up")
    x_out = _matmul([(act, p["big"]("w_down", act))], "nn", F32, tag + "mm_down", residual=x_mid)
    saved = dict(x=x, h1=h1, z=z, dil_qkv=dil_qkv, outs=outs, lses=lses, qd=qd, kx=kx, vx=vx, ys=ys, yn=yn, x_mid=x_mid,
                 h2=h2, act=act, act_du=act_du, act_dg=act_dg)
    return x_out, saved


def _layer_bwd(dx_out, dx_out_b, s, p, table, bsz, tag, emit, mid_hook):
    t = dx_out.shape[0]
    seq3 = lambda a: a.reshape(bsz, SEQ, a.shape[-1])
    flat = lambda a: a.reshape(t, a.shape[-1])
    z = s["z"]
    small = {}
    weight = lambda name: p["big"](name, None)
    emit("w_down", _matmul([(s["act"], dx_out_b)], "tn", BF16, tag + "mm_dwdown").reshape(N_CHIPS, FFN_HIDDEN // N_CHIPS, D_MODEL))
    dgate, dup = _ffn_down_bwd(dx_out_b, weight("w_down"), s["act_du"], s["act_dg"], tag + "ffn_dact")
    emit("w_gate", _matmul([(s["h2"], dgate)], "tn", BF16, tag + "mm_dwgate", slabs=N_CHIPS))
    started = emit("w_up", _matmul([(s["h2"], dup)], "tn", BF16, tag + "mm_dwup", slabs=N_CHIPS))
    dh2 = _matmul([(dgate, weight("w_gate")), (dup, weight("w_up"))], "nt", F32, tag + "mm_dh2")
    dx_mid, dx_mid_b, dg2 = _rmsnorm_bwd(dh2, s["x_mid"], p["norm2_g"], dx_out, tag + "rms2_bwd", follow=started)
    small["norm2_g"] = dg2[0]
    mid_hook(dx_mid)
    dyn = _matmul([(dx_mid_b, weight("w_out"))], "nt", F32, tag + "mm_dyn")
    started = emit("w_out", _matmul([(s["yn"], dx_mid_b)], "tn", BF16, tag + "mm_dwout").reshape(N_CHIPS, D_MODEL // N_CHIPS, D_MODEL))
    *dys, dga, dgb, dgc, dgd = _mixnorm_bwd(dyn, s["ys"], p["mix_g"], tag + "mixnorm_bwd", follow=started)
    small["mix_norm_g"] = jnp.concatenate([dga[0], dgb[0], dgc[0], dgd[0]])
    du, dv, dws, dbs = _sgu_bwd(z, dys[0], p["w_s"], p["w_s_t"], p["bm"], tag + "sgu_bwd")
    small["sgu_w"] = dws
    small["sgu_b"] = dbs[:, :N_HEADS].T
    *douts, dl0, dl1, dl2 = _mixture_bwd(s["outs"], s["lses"], dys[1], tag + "dil_mix_bwd")
    dlses = [dl0, dl1, dl2]
    dqs, dks, dvs, dscs = [], [], [], []
    for i, (_, dil) in enumerate(DIL_PATTERNS):
        dq, dk, dvv, dsc = _dil_bwd(*s["dil_qkv"][i], douts[i], dlses[i], table, dil, f"{tag}dil{dil}_bwd")
        dqs.append(dq)
        dks.append(dk)
        dvs.append(dvv)
        dscs.append(dsc)
    dbq, dbk, dbv, dgq, dgk = _bprep_bwd(z, dqs, dks, dvs, p["gq_dil"], p["gk_dil"], tag + "dil_prep_bwd")
    small["dil_qn_g"], small["dil_kn_g"] = dgq[0], dgk[0]
    dca, dcg, dcw, dcb, dclg, dclb = _conv_bwd(seq3(z), seq3(dys[2]), p["conv_w"], p["conv_b"], p["conv_ln_g"], p["conv_ln_b"],
                                               tag + "conv_bwd")
    small["conv_w"], small["conv_b"], small["conv_ln_g"], small["conv_ln_b"] = dcw, dcb[0], dclg[0], dclb[0]
    dqd, dkx, dvx = _gqa_bwd(seq3(s["qd"]), seq3(s["kx"]), seq3(s["vx"]), seq3(dys[3]), tag + "gqa_bwd")
    ddq, ddk, ddv, dgq, dgk = _dprep_bwd(z, flat(dqd), flat(dkx), flat(dvx), p["gq_gqa"], p["gk_gqa"], tag + "gqa_prep_bwd")
    small["gqa_qn_g"], small["gqa_kn_g"] = dgq[0], dgk[0]
    dz = jnp.concatenate([a.astype(BF16) for a in (du, dv, dbq, dbk, dbv, flat(dca), flat(dcg), ddq, ddk, ddv)], axis=1)
    dz4 = dz.reshape(t, N_CHIPS, IN_WIDTH // N_CHIPS).transpose(1, 0, 2)
    started = emit("w_in", _matmul([(s["h1"], dz4)], "tn", BF16, tag + "mm_dwin", slabs=N_CHIPS))
    dh1 = _matmul([(dz, weight("w_in"))], "nt", F32, tag + "mm_dh1")
    dx, dx_b, dg1 = _rmsnorm_bwd(dh1, s["x"], p["norm1_g"], dx_mid, tag + "rms1_bwd", follow=started)
    small["norm1_g"] = dg1[0]
    return dx, dx_b, small, dscs


def _local_step(x, target, small, big, emit, mid_hook, bsz):
    table = small["rel_bias"]
    params = [_layer_params(l, small, functools.partial(big, l)) for l in range(DEPTH)]
    saved = []
    h = x
    for l in range(DEPTH):
        h, sv = _layer_fwd(h, params[l], table, bsz, f"l{l}_")
        saved.append(sv)
    dh, dh_b, loss = _loss_fwd_bwd(h, target, "loss")
    small_grads, dscs = [None] * DEPTH, [None] * DEPTH
    for l in reversed(range(DEPTH)):
        dh, dh_b, small_grads[l], dscs[l] = _layer_bwd(dh, dh_b, saved[l], params[l], table, bsz, f"l{l}_",
                                                       functools.partial(emit, l), functools.partial(mid_hook, l))
    fold_in = [dscs[l][i] for i in range(len(DIL_PATTERNS)) for l in range(DEPTH)]
    stacked = {k: jnp.stack([small_grads[l][k] for l in range(DEPTH)]) for k in small_grads[0]}
    stacked["rel_bias"] = _relbias_fold(fold_in, "relbias_fold")[:, :N_HEADS]
    return loss, dh, stacked


def _mesh_pos():
    return lax.axis_index("x"), lax.axis_index("y"), lax.axis_index("c")


def _other_chips(x, y):
    return [(1 - x, y), (x, 1 - y), (1 - x, 1 - y)]


_ANY = pl.BlockSpec(memory_space=pl.ANY)


def _swap_sibling(arrs, name):
    n = len(arrs)

    def body(*refs):
        in_refs, out_refs, send_sems, recv_sems = refs[:n], refs[n:2 * n], refs[2 * n], refs[2 * n + 1]
        x, y, c = _mesh_pos()
        copies = [pltpu.make_async_remote_copy(src_ref=in_refs[k], dst_ref=out_refs[k], send_sem=send_sems.at[k],
                                               recv_sem=recv_sems.at[k], device_id=(x, y, 1 - c), device_id_type=MESH)
                  for k in range(n)]
        for cp in copies:
            cp.start()
        for cp in copies:
            cp.wait()

    return pl.pallas_call(
        body, name=name, in_specs=[_ANY] * n, out_specs=[_ANY] * n,
        out_shape=[jax.ShapeDtypeStruct(a.shape, a.dtype) for a in arrs],
        scratch_shapes=[pltpu.SemaphoreType.DMA((n,)), pltpu.SemaphoreType.DMA((n,))],
    )(*arrs)


def _complete_pairs(arrs, name):
    n = len(arrs)

    def body(*refs):
        in_refs, out_refs, send_sems, recv_sems = refs[:n], refs[n:2 * n], refs[2 * n], refs[2 * n + 1]
        x, y, c = _mesh_pos()
        copies = [pltpu.make_async_remote_copy(src_ref=in_refs[k].at[c], dst_ref=out_refs[k].at[c], send_sem=send_sems.at[k],
                                               recv_sem=recv_sems.at[k], device_id=(x, y, 1 - c), device_id_type=MESH)
                  for k in range(n)]
        for cp in copies:
            cp.start()
        for k, cp in enumerate(copies):
            cp.wait_send()
            pltpu.make_async_remote_copy(src_ref=in_refs[k].at[1 - c], dst_ref=out_refs[k].at[1 - c], send_sem=send_sems.at[k],
                                         recv_sem=recv_sems.at[k], device_id=(x, y, 1 - c), device_id_type=MESH).wait_recv()

    return pl.pallas_call(
        body, name=name, in_specs=[_ANY] * n, out_specs=[_ANY] * n,
        out_shape=[jax.ShapeDtypeStruct(a.shape, a.dtype) for a in arrs], input_output_aliases={k: k for k in range(n)},
        scratch_shapes=[pltpu.SemaphoreType.DMA((n,)), pltpu.SemaphoreType.DMA((n,))],
    )(*arrs)


_HBM = pl.BlockSpec(memory_space=pltpu.HBM)
_SEM = pl.BlockSpec(memory_space=pltpu.SEMAPHORE)
_DATAFLOW = pltpu.SideEffectType.DATAFLOW_SIDE_EFFECTING


def _chip_copies(src_refs, land_refs, send_sems, recv_sems, scatter):
    x, y, c = _mesh_pos()
    me = 2 * x + y
    out = []
    for k, (src_ref, land_ref) in enumerate(zip(src_refs, land_refs)):
        if scatter:
            h = src_ref.shape[1] // 2
            for q in range(N_DEV - 1):
                fx, fy, fc = ((q + 1) >> 2) & 1, ((q + 1) >> 1) & 1, (q + 1) & 1
                px, py, pc = (1 - x if fx else x), (1 - y if fy else y), (1 - c if fc else c)
                src = src_ref.at[2 * px + py, pl.ds(pc * h, h)]
                sems = dict(send_sem=send_sems.at[7 * k + q], recv_sem=recv_sems.at[7 * k + q], device_id=(px, py, pc), device_id_type=MESH)
                out.append((pltpu.make_async_remote_copy(src_ref=src, dst_ref=land_ref.at[4 * x + 2 * y + c], **sems),
                            pltpu.make_async_remote_copy(src_ref=src, dst_ref=land_ref.at[4 * px + 2 * py + pc], **sems)))
            continue
        slot = (lambda chip: land_ref.at[c, chip]) if len(land_ref.shape) == 4 else (lambda chip: land_ref.at[chip])
        for j, (cx, cy) in enumerate(_other_chips(x, y)):
            sems = dict(send_sem=send_sems.at[3 * k + j], recv_sem=recv_sems.at[3 * k + j], device_id=(cx, cy, c), device_id_type=MESH)
            out.append((pltpu.make_async_remote_copy(src_ref=src_ref, dst_ref=slot(me), **sems),
                        pltpu.make_async_remote_copy(src_ref=src_ref, dst_ref=slot(2 * cx + cy), **sems)))
    return out


def _chips_start(srcs, scatter, after, name, per_core=False):
    n = len(srcs)
    n_sems = (N_DEV - 1 if scatter else N_CHIPS - 1) * n
    if scatter:
        lands = [lax.empty((N_DEV, s.shape[1] // 2, s.shape[2]), s.dtype) for s in srcs]
    else:
        lands = [lax.empty((*((2, N_CHIPS) if per_core else (N_CHIPS,)), *s.shape), s.dtype) for s in srcs]

    def body(*refs):
        src_refs, land_refs = refs[:n], refs[n:2 * n]
        send_sems, recv_sems, token = refs[2 * n + 1], refs[2 * n + 2], refs[-1]
        for sent, _ in _chip_copies(src_refs, land_refs, send_sems, recv_sems, scatter):
            sent.start()
        token[...] = jnp.zeros_like(token)

    hbm = lambda a: pltpu.HBM(a.shape, a.dtype)
    res = pl.pallas_call(
        body, name=name,
        in_specs=[_HBM] * (2 * n) + [_ANY],
        out_specs=[_SEM, _SEM] + [_HBM] * (2 * n) + [pl.BlockSpec(memory_space=pltpu.VMEM)],
        out_shape=[pltpu.SemaphoreType.DMA((n_sems,)), pltpu.SemaphoreType.DMA((n_sems,))] + [hbm(a) for a in srcs] + [hbm(a) for a in lands]
        + [jax.ShapeDtypeStruct((8, 128), F32)],
        input_output_aliases={i: 2 + i for i in range(2 * n)},
        compiler_params=pltpu.CompilerParams(has_side_effects=_DATAFLOW),
    )(*[pltpu.with_memory_space_constraint(a, pltpu.HBM) for a in (*srcs, *lands)], after)
    return (res[0], res[1], res[2:2 + n], res[2 + n:2 + 2 * n]), res[-1]


def _chips_wait(handle, scatter, after, name):
    send_sems, recv_sems, srcs, lands = handle
    n = len(srcs)

    def body(*refs):
        src_refs, land_refs = refs[:n], refs[n:2 * n]
        send_sems, recv_sems = refs[2 * n], refs[2 * n + 1]
        for sent, landed in _chip_copies(src_refs, land_refs, send_sems, recv_sems, scatter):
            sent.wait_send()
            landed.wait_recv()

    hbm = lambda a: pltpu.HBM(a.shape, a.dtype)
    res = pl.pallas_call(
        body, name=name,
        in_specs=[_HBM] * (2 * n) + [_SEM, _SEM, _ANY], out_specs=[_HBM] * (2 * n),
        out_shape=[hbm(a) for a in srcs] + [hbm(a) for a in lands],
        input_output_aliases={i: i for i in range(2 * n)},
        compiler_params=pltpu.CompilerParams(has_side_effects=_DATAFLOW),
    )(*srcs, *lands, send_sems, recv_sems, after)
    return res[n:]


N_DEV = 8


def _everyone_copies(src_ref, land_ref, send_sems, recv_sems):
    x, y, c = _mesh_pos()
    out = []
    for q in range(N_DEV - 1):
        fx, fy, fc = ((q + 1) >> 2) & 1, ((q + 1) >> 1) & 1, (q + 1) & 1
        px, py, pc = (1 - x if fx else x), (1 - y if fy else y), (1 - c if fc else c)
        sems = dict(send_sem=send_sems.at[q], recv_sem=recv_sems.at[q], device_id=(px, py, pc), device_id_type=MESH)
        out.append((pltpu.make_async_remote_copy(src_ref=src_ref, dst_ref=land_ref.at[4 * x + 2 * y + c], **sems),
                    pltpu.make_async_remote_copy(src_ref=src_ref, dst_ref=land_ref.at[4 * px + 2 * py + pc], **sems)))
    return out


def _everyone_start(block, after, name):
    land = lax.empty((N_DEV, *block.shape), block.dtype)

    def body(src_ref, land_ref, after_ref, send_sems, recv_sems, src_thru, land_thru, token):
        for sent, _ in _everyone_copies(src_ref, land_ref, send_sems, recv_sems):
            sent.start()
        token[...] = jnp.zeros_like(token)

    hbm = lambda a: pltpu.HBM(a.shape, a.dtype)
    n_sem = N_DEV - 1
    res = pl.pallas_call(
        body, name=name, in_specs=[_HBM, _HBM, _ANY],
        out_specs=[_SEM, _SEM, _HBM, _HBM, pl.BlockSpec(memory_space=pltpu.VMEM)],
        out_shape=[pltpu.SemaphoreType.DMA((n_sem,)), pltpu.SemaphoreType.DMA((n_sem,)), hbm(block), hbm(land),
                   jax.ShapeDtypeStruct((8, 128), F32)],
        input_output_aliases={0: 2, 1: 3}, compiler_params=pltpu.CompilerParams(has_side_effects=_DATAFLOW),
    )(pltpu.with_memory_space_constraint(block, pltpu.HBM), pltpu.with_memory_space_constraint(land, pltpu.HBM), after)
    return res[:4]


def _everyone_wait(handle, after, name):
    send_sems, recv_sems, block, land = handle

    def body(src_ref, land_ref, send_sems, recv_sems, after_ref, src_thru, land_thru):
        for sent, landed in _everyone_copies(src_ref, land_ref, send_sems, recv_sems):
            sent.wait_send()
            landed.wait_recv()

    hbm = lambda a: pltpu.HBM(a.shape, a.dtype)
    return pl.pallas_call(
        body, name=name, in_specs=[_HBM, _HBM, _SEM, _SEM, _ANY], out_specs=[_HBM, _HBM], out_shape=[hbm(block), hbm(land)],
        input_output_aliases={0: 0, 1: 1}, compiler_params=pltpu.CompilerParams(has_side_effects=_DATAFLOW),
    )(block, land, send_sems, recv_sems, after)


def _allgather_sum_small(block, name):
    m_per, n = block.shape

    def body(x_ref, out_ref, sum_ref, send_sems, recv_sems, local_sem):
        x, y, c = _mesh_pos()
        me, sibling = (x, y, c), (x, y, 1 - c)
        chips = _other_chips(x, y)

        def rows(px, py, pc):
            return out_ref.at[pl.ds((4 * px + 2 * py + pc) * m_per, m_per), :]

        def copy(k, blk, to, src=None):
            return pltpu.make_async_remote_copy(src_ref=rows(*blk) if src is None else src, dst_ref=rows(*blk),
                                                send_sem=send_sems.at[k], recv_sem=recv_sems.at[k], device_id=to, device_id_type=MESH)

        mine = pltpu.make_async_copy(x_ref, rows(*me), local_sem)
        mine.start()
        first = [copy(0, me, sibling, src=x_ref)]
        first += [copy(1 + j, me, (*chip, c), src=x_ref) for j, chip in enumerate(chips)]
        for cp in first:
            cp.start()
        passed = [copy(4 + j, (*chip, c), sibling) for j, chip in enumerate(chips)]
        for j, chip in enumerate(chips):
            copy(1 + j, (*chip, c), me).wait_recv()
            passed[j].start()
        copy(0, sibling, me).wait_recv()
        for j, chip in enumerate(chips):
            copy(4 + j, (*chip, 1 - c), me).wait_recv()
        for cp in first + passed:
            cp.wait_send()
        mine.wait()
        total = out_ref[pl.ds(0, m_per), :]
        for d in range(1, N_DEV):
            total = total + out_ref[pl.ds(d * m_per, m_per), :]
        sum_ref[...] = total

    vmem = pl.BlockSpec(memory_space=pltpu.VMEM)
    return pl.pallas_call(
        body, name=name, in_specs=[vmem], out_specs=[vmem, vmem],
        out_shape=[jax.ShapeDtypeStruct((N_DEV * m_per, n), F32), jax.ShapeDtypeStruct((m_per, n), F32)],
        scratch_shapes=[pltpu.SemaphoreType.DMA((7,)), pltpu.SemaphoreType.DMA((7,)), pltpu.SemaphoreType.DMA],
        compiler_params=pltpu.CompilerParams(vmem_limit_bytes=V7X_VMEM_LIMIT),
    )(block)


WEIGHTS = ("rel_bias", "norm1_g", "w_in", "sgu_w", "sgu_b", "dil_qn_g", "dil_kn_g", "conv_w", "conv_b", "conv_ln_g", "conv_ln_b",
           "gqa_qn_g", "gqa_kn_g", "mix_norm_g", "w_out", "norm2_g", "w_gate", "w_up", "w_down")
SHARDED = ("w_in", "w_out", "w_gate", "w_up", "w_down")
COLUMN_SHARDED = ("w_in", "w_gate", "w_up")
REPLICATED = tuple(k for k in WEIGHTS if k not in SHARDED and k != "conv_w")


PACK_ROWS = 256


def _pack(parts):
    flat = jnp.concatenate([p.reshape(-1) for p in parts])
    pad = (-flat.shape[0]) % (PACK_ROWS * LANES)
    return jnp.pad(flat, (0, pad)).reshape(-1, LANES)


def _unpack(buf, shapes):
    flat = buf.reshape(-1)
    out, at = [], 0
    for s in shapes:
        size = math.prod(s)
        out.append(flat[at:at + size].reshape(s))
        at += size
    return out


RS_TM = (256, 128, 64, 32, 16)


def _add_own_seven(own, land, place, name):
    _, h, cols = land.shape
    tm = _pick(h, RS_TM)
    nb = h // tm

    def body(place_ref, own_ref, *refs):
        total = own_ref[...].astype(F32)
        for l_ref in refs[:-1]:
            total = total + l_ref[...].astype(F32)
        refs[-1][...] = total

    slot = lambda r: pl.BlockSpec((None, tm, cols), functools.partial(lambda i, p, r: (jnp.bitwise_xor(p[2], r), i, 0), r=r))
    grid_spec = pltpu.PrefetchScalarGridSpec(
        num_scalar_prefetch=1, grid=(nb,),
        in_specs=[pl.BlockSpec((None, tm, cols), lambda i, p: (p[0], p[1] * nb + i, 0))] + [slot(r) for r in range(1, N_DEV)],
        out_specs=pl.BlockSpec((tm, cols), lambda i, p: (i, 0)))
    return pl.pallas_call(body, name=name, grid_spec=grid_spec, out_shape=jax.ShapeDtypeStruct((h, cols), F32),
                          compiler_params=_cparams(("parallel",)))(place, own, *[land] * (N_DEV - 1))


def _reduce_start(grads, after, tag):
    handle, token = _chips_start(grads, True, after, tag + "start")
    return (handle, grads), token


def _reduce_finish(started, place, after, tag):
    handle, grads = started
    lands = _chips_wait(handle, True, after, tag + "wait")
    totals = [_add_own_seven(g, land, place, f"{tag}sum_{k}") for k, (g, land) in enumerate(zip(grads, lands))]
    return list(zip(totals, _swap_sibling(totals, tag + "share")))


def _adamw_shard(w, m, v, halves, c1, name):
    depth, rows, cols = w.shape
    h = rows // 2
    tm = _pick(h, RS_TM)
    nb = h // tm
    sources = [a for pair in halves for a in pair]

    def body(c_ref, w_ref, m_ref, v_ref, *refs):
        g_refs, (g_out, d_out, m_out, v_out) = refs[:2 * depth], refs[2 * depth:]
        layer, mine = pl.program_id(0), pl.program_id(1) == c_ref[0]
        g = None
        for l in range(depth):
            g_l = jnp.where(mine, g_refs[2 * l][...], g_refs[2 * l + 1][...])
            g = g_l if g is None else jnp.where(layer == l, g_l, g)
        delta, m_new, v_new = _adamw_fn(w_ref[...], g, m_ref[...], v_ref[...])
        g_out[...], d_out[...], m_out[...], v_out[...] = g, delta, m_new, v_new

    def source_spec(l, own):
        def index(layer, half, i, c_ref):
            return (jnp.where((layer == l) & ((half == c_ref[0]) == own), i, 0), 0)
        return pl.BlockSpec((tm, cols), index)

    blk = pl.BlockSpec((None, tm, cols), lambda layer, half, i, c_ref: (layer, half * nb + i, 0))
    grid_spec = pltpu.PrefetchScalarGridSpec(
        num_scalar_prefetch=1, grid=(depth, 2, nb),
        in_specs=[blk, blk, blk] + [source_spec(l, own) for l in range(depth) for own in (True, False)], out_specs=[blk] * 4)
    return pl.pallas_call(body, name=name, grid_spec=grid_spec, out_shape=[jax.ShapeDtypeStruct(w.shape, F32)] * 4,
                          compiler_params=_cparams(("arbitrary", "arbitrary", "arbitrary")))(c1, w, m, v, *sources)


GATHER_GROUPS = ((("w_in",), 0), (("w_out",), None), (("w_gate", "w_up"), None), (("w_down",), None), (("w_in",), 1))
REDUCE_GROUPS = (("w_down", "w_gate", "w_up"), ("w_out",), ("w_in",))


def kernel(x, rel_bias, norm1_g, w_in, sgu_w, sgu_b, dil_qn_g, dil_kn_g, conv_w, conv_b, conv_ln_g, conv_ln_b, gqa_qn_g, gqa_kn_g, mix_norm_g, w_out, norm2_g, w_gate, w_up, w_down, loss_target, m_rel_bias, m_norm1_g, m_w_in, m_sgu_w, m_sgu_b, m_dil_qn_g, m_dil_kn_g, m_conv_w, m_conv_b, m_conv_ln_g, m_conv_ln_b, m_gqa_qn_g, m_gqa_kn_g, m_mix_norm_g, m_w_out, m_norm2_g, m_w_gate, m_w_up, m_w_down, v_rel_bias, v_norm1_g, v_w_in, v_sgu_w, v_sgu_b, v_dil_qn_g, v_dil_kn_g, v_conv_w, v_conv_b, v_conv_ln_g, v_conv_ln_b, v_gqa_qn_g, v_gqa_kn_g, v_mix_norm_g, v_w_out, v_norm2_g, v_w_gate, v_w_up, v_w_down):
    w = dict(rel_bias=rel_bias, norm1_g=norm1_g, w_in=w_in, sgu_w=sgu_w, sgu_b=sgu_b, dil_qn_g=dil_qn_g, dil_kn_g=dil_kn_g,
             conv_w=conv_w, conv_b=conv_b, conv_ln_g=conv_ln_g, conv_ln_b=conv_ln_b, gqa_qn_g=gqa_qn_g, gqa_kn_g=gqa_kn_g,
             mix_norm_g=mix_norm_g, w_out=w_out, norm2_g=norm2_g, w_gate=w_gate, w_up=w_up, w_down=w_down)
    m = dict(rel_bias=m_rel_bias, norm1_g=m_norm1_g, w_in=m_w_in, sgu_w=m_sgu_w, sgu_b=m_sgu_b, dil_qn_g=m_dil_qn_g,
             dil_kn_g=m_dil_kn_g, conv_w=m_conv_w, conv_b=m_conv_b, conv_ln_g=m_conv_ln_g, conv_ln_b=m_conv_ln_b,
             gqa_qn_g=m_gqa_qn_g, gqa_kn_g=m_gqa_kn_g, mix_norm_g=m_mix_norm_g, w_out=m_w_out, norm2_g=m_norm2_g,
             w_gate=m_w_gate, w_up=m_w_up, w_down=m_w_down)
    v = dict(rel_bias=v_rel_bias, norm1_g=v_norm1_g, w_in=v_w_in, sgu_w=v_sgu_w, sgu_b=v_sgu_b, dil_qn_g=v_dil_qn_g,
             dil_kn_g=v_dil_kn_g, conv_w=v_conv_w, conv_b=v_conv_b, conv_ln_g=v_conv_ln_g, conv_ln_b=v_conv_ln_b,
             gqa_qn_g=v_gqa_qn_g, gqa_kn_g=v_gqa_kn_g, mix_norm_g=v_mix_norm_g, w_out=v_w_out, norm2_g=v_norm2_g,
             w_gate=v_w_gate, w_up=v_w_up, w_down=v_w_down)
    bsz = x.shape[0]
    t = bsz * SEQ
    xi, yi, ci = _mesh_pos()
    chip = 2 * xi + yi
    conv_cols = conv_w.shape[-1]

    conv_rows = DEPTH * CONV_WIDTH
    conv_block = jnp.pad(conv_w.reshape(conv_rows, conv_cols), ((0, (-conv_rows) % 8), (0, 0)))
    every, _ = _allgather_sum_small(conv_block, "conv_w_gather")
    every = every.reshape(N_DEV, conv_block.shape[0], conv_cols)
    conv_w_full = jnp.concatenate([every[2 * j, :conv_rows].reshape(DEPTH, CONV_WIDTH, conv_cols) for j in range(N_CHIPS)], axis=-1)

    c1 = jnp.reshape(ci, (1,)).astype(jnp.int32)
    place = jnp.stack([chip, ci, 4 * xi + 2 * yi + ci]).astype(jnp.int32)

    def own_part(k, layer):
        if layer is None:
            return lax.dynamic_index_in_dim(w[k], ci, axis=0, keepdims=False).astype(BF16)
        half = w[k].shape[1] // 2
        return lax.dynamic_slice_in_dim(w[k][layer], ci * half, half, axis=0).astype(BF16)

    fetches, token = [], every
    for gi, (group, layer) in enumerate(GATHER_GROUPS):
        parts = [own_part(k, layer) for k in group]
        handle, token = _chips_start(parts, False, token, f"gather{gi}_start", per_core=True)
        fetches.append((handle, parts))
    all_started = token
    gathered = {}

    def big(l, name, after):
        if (l, name) not in gathered:
            gi = [name in group and layer in (None, l) for group, layer in GATHER_GROUPS].index(True)
            (group, layer), (handle, parts) = GATHER_GROUPS[gi], fetches[gi]
            lands = _chips_wait(handle, False, all_started if after is None else after, f"gather{gi}_wait")
            mine = [lax.dynamic_update_slice(land, own[None, None], (ci, chip, 0, 0)) for land, own in zip(lands, parts)]
            for k, g in zip(group, _complete_pairs(mine, f"gather{gi}_share")):
                rows, cols = g.shape[2:]
                if layer is not None:
                    gathered[layer, k] = g.transpose(0, 2, 1, 3).reshape(2 * rows, N_CHIPS * cols)
                    continue
                for each in range(DEPTH):
                    gathered[each, k] = (g[each].transpose(1, 0, 2).reshape(rows, N_CHIPS * cols) if k in COLUMN_SHARDED
                                         else g[each].reshape(N_CHIPS * rows, cols))
        return gathered[l, name]

    big(0, "w_in", None)

    pending, started, reduced = {}, {}, {}

    def emit(l, name, g):
        pending[l, name] = g
        for gi, group in enumerate(REDUCE_GROUPS):
            if name in group and all((l, k) in pending for k in group):
                started[l, gi], token = _reduce_start([pending[l, k] for k in group], g, f"l{l}_reduce{gi}_")
                return token
        return None

    def finish(l, after):
        for gi, group in enumerate(REDUCE_GROUPS):
            for k, r in zip(group, _reduce_finish(started[l, gi], place, after, f"l{l}_reduce{gi}_")):
                reduced[l, k] = r
            after = reduced[l, group[0]][1]

    def mid_hook(l, a):
        if l + 1 < DEPTH:
            finish(l + 1, a)

    small = {k: w[k] for k in REPLICATED}
    small["conv_w"] = conv_w_full
    loss, dx, small_grads = _local_step(x.reshape(t, D_MODEL), loss_target.reshape(t, D_MODEL), small, big, emit, mid_hook, bsz)
    loss = lax.psum(loss[0, 0], ("x", "y", "c"))

    names = REPLICATED + ("conv_w",)
    shapes = [small_grads[k].shape for k in names]
    packed_grads = _pack([small_grads[k] for k in names])
    eighths = packed_grads.reshape(N_CHIPS, packed_grads.shape[0] // N_CHIPS, LANES)
    small_reduce, _ = _reduce_start([eighths], dx, "small_grads_")
    finish(0, dx)

    grads, deltas, new_m, new_v = {}, {}, {}, {}
    for k in SHARDED:
        grads[k], deltas[k], new_m[k], new_v[k] = _adamw_shard(w[k], m[k], v[k], [reduced[l, k] for l in range(DEPTH)], c1, "adamw_" + k)

    landed = _chips_wait(small_reduce[0], True, new_v[SHARDED[-1]], "small_grads_wait")[0]
    mine = _add_own_seven(eighths, landed, place, "small_grads_sum")
    own, others = _everyone_wait(_everyone_start(mine, mine, "small_grads_spread"), mine, "small_grads_spread_wait")
    summed = lax.dynamic_update_slice_in_dim(others, own[None], 4 * xi + 2 * yi + ci, axis=0).reshape(packed_grads.shape)
    summed_parts = dict(zip(names, _unpack(summed, shapes)))
    rep_shapes = [w[k].shape for k in REPLICATED]
    packed = [_pack([src[k] for k in REPLICATED]) for src in (w, {k: summed_parts[k] for k in REPLICATED}, m, v)]
    d_p, m_p, v_p = _adamw(*packed, "adamw_replicated")
    for k, gk, dk, mk, vk in zip(REPLICATED, _unpack(packed[1], rep_shapes), _unpack(d_p, rep_shapes), _unpack(m_p, rep_shapes),
                                 _unpack(v_p, rep_shapes)):
        grads[k], deltas[k], new_m[k], new_v[k] = gk, dk, mk, vk
    g_conv = lax.dynamic_slice_in_dim(summed_parts["conv_w"], chip * conv_cols, conv_cols, axis=2)
    packed = [_pack([a]) for a in (conv_w, g_conv, m["conv_w"], v["conv_w"])]
    d_p, m_p, v_p = _adamw(*packed, "adamw_conv_w")
    grads["conv_w"] = g_conv
    deltas["conv_w"], new_m["conv_w"], new_v["conv_w"] = (_unpack(a, [conv_w.shape])[0] for a in (d_p, m_p, v_p))

    return (loss, dx.reshape(x.shape), *[grads[k] for k in WEIGHTS], *[deltas[k] for k in WEIGHTS],
            *[new_m[k] for k in WEIGHTS], *[new_v[k] for k in WEIGHTS])
```

```python
import functools
import math

import numpy as np
import jax
import jax.numpy as jnp
from jax import lax
from jax.experimental import pallas as pl
from jax.experimental.pallas import tpu as pltpu

F32 = jnp.float32
BF16 = jnp.bfloat16

D_MODEL = 2048
SEQ = 2048
DEPTH = 2
HEAD_DIM = 64
GROUP_WIDTH = 512
N_HEADS = 8
SGU_CHUNK = 128
DIL_PATTERNS = ((128, 1), (512, 4), (2048, 16))
DIL_HALF = 64
CONV_WIDTH = 31
KV_WIDTH = 128
GRID_W = 64
ROPE_THETA = 10000.0
REL_BUCKETS = 32
REL_MAX_DIST = 1024
FFN_HIDDEN = 5632
IN_WIDTH = 4352
RMS_EPS = 1e-6
LN_EPS = 1e-5
ADAM_LR = 0.001
ADAM_B1 = 0.9
ADAM_B2 = 0.999
ADAM_EPS = 1e-08
ADAM_WD = 0.01
ADAM_STEP = 10
N_CHIPS = 4

V7X_VMEM_LIMIT = 56 * 1024 * 1024
MATMUL_VMEM_BUDGET = 48 * 1024 * 1024
LANES = 128
HI = lax.Precision.HIGHEST
SPLIT3 = lax.Precision.HIGH
MESH = pl.DeviceIdType.MESH


def _cparams(sem=None):
    return pltpu.CompilerParams(dimension_semantics=sem, vmem_limit_bytes=V7X_VMEM_LIMIT)


def _pick(n, cands):
    for c in cands:
        if n % c == 0:
            return c
    raise ValueError(f"no tile for {n}")


_DIMS = {"nn": (((1,), (0,)), ((), ())), "nt": (((1,), (1,)), ((), ())), "tn": (((0,), (0,)), ((), ()))}


def _matmul(pairs, mode, out_dtype, name, residual=None, slabs=1):
    a0, b0 = pairs[0]
    b3 = b0.ndim == 3
    if mode == "nn":
        (M, K), N = a0.shape, b0.shape[1]
    elif mode == "nt":
        (M, K), N = a0.shape, b0.shape[0]
    else:
        (K, M) = a0.shape
        N = b0.shape[-1] if b3 else b0.shape[1] // slabs
    npairs = len(pairs)
    a_bytes, b_bytes, o_bytes = a0.dtype.itemsize, b0.dtype.itemsize, jnp.dtype(out_dtype).itemsize
    per_out = 4 + 2 * o_bytes + (8 if residual is not None else 0)
    tn_cands = [c for c in ((1024, 512) if K <= 2048 else (512,)) + (1408, 2176, 256) if N % c == 0] + [N]
    tm, tn = next((tm, tn) for tn in tn_cands for tm in (1024, 1408, 512, 256)
                  if M % tm == 0 and 2 * npairs * K * (tm * a_bytes + tn * b_bytes) + tm * tn * per_out <= MATMUL_VMEM_BUDGET)
    tk = K
    ni, nj = M // tm, N // tn
    j_outer = nj * M * a_bytes + N * b_bytes < M * a_bytes + ni * N * b_bytes
    grid = (slabs, nj, ni) if j_outer else (slabs, ni, nj)
    at = lambda f: (lambda s, g1, g2: f(s, g2, g1)) if j_outer else f

    if mode in ("nn", "nt"):
        a_spec = pl.BlockSpec((tm, tk), at(lambda s, i, j: (i, 0)))
    else:
        a_spec = pl.BlockSpec((tk, tm), at(lambda s, i, j: (0, i)))
    if mode == "nt":
        b_spec = pl.BlockSpec((tn, tk), at(lambda s, i, j: (j, 0)))
    elif b3:
        b_spec = pl.BlockSpec((None, tk, tn), at(lambda s, i, j: (s, 0, j)))
    else:
        b_spec = pl.BlockSpec((tk, tn), at(lambda s, i, j: (0, s * nj + j)))
    if slabs > 1:
        o_spec = pl.BlockSpec((None, tm, tn), at(lambda s, i, j: (s, i, j)))
        o_shape = jax.ShapeDtypeStruct((slabs, M, N), out_dtype)
    else:
        o_spec = pl.BlockSpec((tm, tn), at(lambda s, i, j: (i, j)))
        o_shape = jax.ShapeDtypeStruct((M, N), out_dtype)
    in_specs = [a_spec] * npairs + [b_spec] * npairs
    args = [a for a, _ in pairs] + [b for _, b in pairs]
    if residual is not None:
        in_specs.append(pl.BlockSpec((tm, tn), at(lambda s, i, j: (i, j))))
        args.append(residual)
    dims = _DIMS[mode]

    def body(*refs):
        a_refs, b_refs = refs[:npairs], refs[npairs:2 * npairs]
        res_ref = refs[2 * npairs] if residual is not None else None
        o_ref = refs[-1]
        r = None
        for a_ref, b_ref in zip(a_refs, b_refs):
            d = lax.dot_general(a_ref[...].astype(BF16), b_ref[...].astype(BF16), dims, preferred_element_type=F32)
            r = d if r is None else r + d
        if res_ref is not None:
            r = r + res_ref[...]
        o_ref[...] = r.astype(out_dtype)

    return pl.pallas_call(
        body, name=name, grid=grid, in_specs=in_specs, out_specs=o_spec, out_shape=o_shape,
        compiler_params=_cparams(("parallel", "parallel", "parallel")),
    )(*args)


class Strided:
    def __init__(self, r):
        self.r = r


def _rowmap(fn, rows, fulls, row_outs, acc_outs, name, tm, n_rows):
    nr, nf, nro = len(rows), len(fulls), len(row_outs)
    rows = [r if len(r) == 4 else (*r, n_rows // tm) for r in rows]
    row_outs = [o if len(o) == 3 else (*o, None) for o in row_outs]
    in_specs = [pl.BlockSpec((tm // per.r, per.r * w), lambda i: (i, 0)) if isinstance(per, Strided) else
                pl.BlockSpec((tm, w), functools.partial(lambda i, cb, per: (i % per, cb), cb=cb, per=per)) for _, w, cb, per in rows]
    in_specs += [pl.BlockSpec(f.shape, lambda i: (0,) * f.ndim) for f in fulls]
    out_specs = [pl.BlockSpec((tm, w) if st is None else (tm // st.r, st.r * w), lambda i: (i, 0)) for w, _, st in row_outs]
    out_specs += [pl.BlockSpec(s, functools.partial(lambda i, n: (0,) * n, n=len(s))) for s in acc_outs]
    out_shape = [jax.ShapeDtypeStruct((n_rows, w) if st is None else (n_rows // st.r, st.r * w), dt) for w, dt, st in row_outs]
    out_shape += [jax.ShapeDtypeStruct(s, F32) for s in acc_outs]
    strided = [(k, w, per.r) for k, (_, w, _, per) in enumerate(rows) if isinstance(per, Strided)]
    strided += [(nr + nf + k, w, st.r) for k, (w, _, st) in enumerate(row_outs) if st is not None]
    n_scratch = len(strided)

    def body(*refs):
        refs, scratch = refs[:len(refs) - n_scratch], dict(zip([k for k, _, _ in strided], refs[len(refs) - n_scratch:]))
        ins = []
        for k, ref in enumerate(refs[:nr + nf]):
            if k in scratch:
                w, r, scr = rows[k][1], rows[k][3].r, scratch[k]
                for rho in range(r):
                    for j in range(w // LANES):
                        scr.at[j][pl.ds(rho, tm // r, stride=r), :] = ref[:, pl.ds(rho * w + j * LANES, LANES)]
                ins.append(jnp.concatenate([scr[j] for j in range(w // LANES)], axis=1))
            else:
                ins.append(ref[...])
        outs = fn(*ins)
        o_refs = refs[nr + nf:]
        for k, (o_ref, val) in enumerate(zip(o_refs[:nro], outs[:nro])):
            if nr + nf + k in scratch:
                w, r, scr = row_outs[k][0], row_outs[k][2].r, scratch[nr + nf + k]
                val = val.astype(F32)
                for j in range(w // LANES):
                    scr[j] = val[:, j * LANES:(j + 1) * LANES]
                for rho in range(r):
                    for j in range(w // LANES):
                        o_ref[:, pl.ds(rho * w + j * LANES, LANES)] = scr.at[j][pl.ds(rho, tm // r, stride=r), :].astype(o_ref.dtype)
            else:
                o_ref[...] = val.astype(o_ref.dtype)
        if acc_outs:
            first = pl.program_id(0) == 0
            for o_ref, val in zip(o_refs[nro:], outs[nro:]):
                @pl.when(first)
                def _(o_ref=o_ref, val=val):
                    o_ref[...] = val

                @pl.when(jnp.logical_not(first))
                def _(o_ref=o_ref, val=val):
                    o_ref[...] += val

    res = pl.pallas_call(
        body, name=name, grid=(n_rows // tm,), in_specs=in_specs, out_specs=out_specs, out_shape=out_shape,
        scratch_shapes=[pltpu.VMEM((w // LANES, tm, LANES), F32) for _, w, _ in strided],
        compiler_params=_cparams(("arbitrary",) if acc_outs else ("parallel",)),
    )(*[r[0] for r in rows], *fulls)
    return res


def _rms(x, g):
    return x * lax.rsqrt(jnp.mean(x * x, axis=-1, keepdims=True) + RMS_EPS) * g


def _rmsnorm_fwd(x, g, name):
    t = x.shape[0]
    return _rowmap(lambda xv, gv: (_rms(xv, gv),), [(x, D_MODEL, 0)], [g], [(D_MODEL, BF16)], [], name, 512, t)[0]


def _rmsnorm_bwd(dh, x, g, dres, name, follow=None):
    t = x.shape[0]

    def fn(dhv, xv, drv, gv, *_):
        _, vjp = jax.vjp(_rms, xv, gv)
        dx, dg = vjp(dhv)
        return dx + drv, dx + drv, dg

    fulls = [g] if follow is None else [g, follow]
    return _rowmap(fn, [(dh, D_MODEL, 0), (x, D_MODEL, 0), (dres, D_MODEL, 0)], fulls, [(D_MODEL, F32), (D_MODEL, BF16)],
                   [(1, D_MODEL)], name, 256, t)


def _loss_fwd_bwd(y, target, name):
    t = y.shape[0]

    def fn(yv, tv):
        e = yv - tv
        return e * (1.0 / D_MODEL), e * (1.0 / D_MODEL), (0.5 / D_MODEL) * jnp.sum(e * e, keepdims=True)

    return _rowmap(fn, [(y, D_MODEL, 0), (target, D_MODEL, 0)], [], [(D_MODEL, F32), (D_MODEL, BF16)], [(1, 1)], name, 512, t)


FFN_TILE = (1024, 512)


def _ffn_up(h, wg, wu, name):
    t, n = h.shape[0], wg.shape[1]
    tm, tn = FFN_TILE

    def body(h_ref, wg_ref, wu_ref, act_ref, du_ref, dg_ref):
        hv = h_ref[...]
        g = jnp.dot(hv, wg_ref[...], preferred_element_type=F32)
        u = jnp.dot(hv, wu_ref[...], preferred_element_type=F32)
        sg = jax.nn.sigmoid(g)
        silu = g * sg
        act_ref[...] = (silu * u).astype(BF16)
        du_ref[...] = silu.astype(BF16)
        dg_ref[...] = (u * (sg + silu * (1.0 - sg))).astype(BF16)

    o_spec = pl.BlockSpec((tm, tn), lambda i, j: (i, j))
    o_shape = jax.ShapeDtypeStruct((t, n), BF16)
    return pl.pallas_call(
        body, name=name, grid=(t // tm, n // tn),
        in_specs=[pl.BlockSpec((tm, D_MODEL), lambda i, j: (i, 0)), pl.BlockSpec((D_MODEL, tn), lambda i, j: (0, j)),
                  pl.BlockSpec((D_MODEL, tn), lambda i, j: (0, j))],
        out_specs=[o_spec] * 3, out_shape=[o_shape] * 3, compiler_params=_cparams(("parallel", "parallel")),
    )(h, wg, wu)


def _ffn_down_bwd(dy, wd, act_du, act_dg, name):
    t, n = dy.shape[0], wd.shape[0]
    tm, tn = FFN_TILE

    def body(dy_ref, wd_ref, adu_ref, adg_ref, dg_ref, du_ref):
        dact = lax.dot_general(dy_ref[...].astype(BF16), wd_ref[...], _DIMS["nt"], preferred_element_type=F32)
        du_ref[...] = (dact * adu_ref[...].astype(F32)).astype(BF16)
        dg_ref[...] = (dact * adg_ref[...].astype(F32)).astype(BF16)

    o_spec = pl.BlockSpec((tm, tn), lambda i, j: (i, j))
    o_shape = jax.ShapeDtypeStruct((t, n), BF16)
    return pl.pallas_call(
        body, name=name, grid=(t // tm, n // tn),
        in_specs=[pl.BlockSpec((tm, D_MODEL), lambda i, j: (i, 0)), pl.BlockSpec((tn, D_MODEL), lambda i, j: (j, 0)),
                  o_spec, o_spec],
        out_specs=[o_spec] * 2, out_shape=[o_shape] * 2, compiler_params=_cparams(("parallel", "parallel")),
    )(dy, wd, act_du, act_dg)


def _np_group_avg(width, group=HEAD_DIM):
    i = np.arange(width)
    return ((i[:, None] // group) == (i[None, :] // group)).astype(np.float32) / group


def _np_tile_fold(width, group=HEAD_DIM):
    return ((np.arange(width)[:, None] % group) == np.arange(group)[None, :]).astype(np.float32)


def _np_group_fold(width, group=HEAD_DIM, pad=128):
    return ((np.arange(width)[:, None] // group) == np.arange(pad)[None, :]).astype(np.float32)


def _np_rope_partner(width):
    i = np.arange(width)
    partner = np.where(i % 32 < 16, i + 16, i - 16)
    return (partner[:, None] == i[None, :]).astype(np.float32)


def _np_kv_expand():
    src = np.arange(KV_WIDTH)
    dst = np.arange(GROUP_WIDTH)
    return ((src[:, None] // HEAD_DIM == dst[None, :] // (4 * HEAD_DIM)) & (src[:, None] % HEAD_DIM == dst[None, :] % HEAD_DIM)).astype(np.float32)


def _np_rope_tables(n_heads):
    t = np.arange(SEQ)
    pos = {0: (t // GRID_W).astype(np.float32), 1: (t % GRID_W).astype(np.float32)}
    freqs = (ROPE_THETA ** (-np.arange(16, dtype=np.float32) / 16)).astype(np.float32)
    cos_parts, sin_parts = [], []
    for axis in (0, 1):
        ang = pos[axis][:, None] * freqs[None, :]
        c, s = np.cos(ang).astype(np.float32), np.sin(ang).astype(np.float32)
        cos_parts += [c, c]
        sin_parts += [-s, s]
    cos = np.concatenate(cos_parts, axis=1)
    sin = np.concatenate(sin_parts, axis=1)
    return np.tile(cos, (1, n_heads)), np.tile(sin, (1, n_heads))


def _np_t5_buckets(rel):
    nb = REL_BUCKETS // 2
    max_exact = nb // 2
    ret = np.where(rel > 0, nb, 0)
    n = np.abs(rel)
    nf = np.maximum(n, 1).astype(np.float32)
    large = max_exact + (np.log(nf / max_exact) / math.log(REL_MAX_DIST / max_exact) * (nb - max_exact)).astype(np.int32)
    large = np.minimum(large, nb - 1)
    return (ret + np.where(n < max_exact, n, large)).astype(np.int32)


DIL_QB = 128
DIL_WIN = DIL_QB + 2 * DIL_HALF


def _np_dil_buckets(dil):
    off = np.arange(DIL_WIN)[None, :] - DIL_HALF - np.arange(DIL_QB)[:, None]
    return _np_t5_buckets(off * dil)


def _dil_live_buckets(dil):
    off = np.arange(-DIL_HALF, DIL_HALF + 1)
    return sorted(set(_np_t5_buckets(off * dil).tolist()))


def _head_stat(x, mavg):
    return jnp.dot(x, mavg, precision=SPLIT3, preferred_element_type=F32)


def _gelu(x):
    return 0.5 * x * (1.0 + jnp.tanh(math.sqrt(2.0 / math.pi) * (x + 0.044715 * (x * x * x))))


def _sgu_pre(u_pre, v_pre, mavg):
    v = _gelu(v_pre)
    xc = v - _head_stat(v, mavg)
    vn = xc * lax.rsqrt(_head_stat(xc * xc, mavg) + LN_EPS)
    return _gelu(u_pre), vn


def _sgu_mix(w_ref, vnb, bm):
    lane_group = lax.broadcasted_iota(jnp.int32, (1, GROUP_WIDTH), 1) // HEAD_DIM
    mixed = bm
    for g in range(N_HEADS):
        r = jnp.dot(w_ref[g], vnb, preferred_element_type=F32)
        mixed = mixed + jnp.where(lane_group == g, r, 0.0)
    return mixed


SGU_TM = 512


def _sgu_fwd(z, w_s, bm, name):
    t = z.shape[0]
    mavg = jnp.asarray(_np_group_avg(GROUP_WIDTH))

    def body(u_ref, v_ref, w_ref, bm_ref, mavg_ref, y_ref):
        for c in range(SGU_TM // SGU_CHUNK):
            rows = pl.ds(c * SGU_CHUNK, SGU_CHUNK)
            u, vn = _sgu_pre(u_ref[rows, :], v_ref[rows, :], mavg_ref[...])
            y_ref[rows, :] = u * _sgu_mix(w_ref, vn.astype(BF16), bm_ref[...])

    full = lambda a: pl.BlockSpec(a.shape, lambda i: (0,) * a.ndim)
    return pl.pallas_call(
        body, name=name, grid=(t // SGU_TM,),
        in_specs=[pl.BlockSpec((SGU_TM, GROUP_WIDTH), lambda i: (i, 0)), pl.BlockSpec((SGU_TM, GROUP_WIDTH), lambda i: (i, 1)),
                  full(w_s), full(bm), full(mavg)],
        out_specs=pl.BlockSpec((SGU_TM, GROUP_WIDTH), lambda i: (i, 0)),
        out_shape=jax.ShapeDtypeStruct((t, GROUP_WIDTH), F32), compiler_params=_cparams(("parallel",)),
    )(z, z, w_s, bm, mavg)


def _sgu_bwd(z, dy, w_s, w_s_t, bm, name):
    t = z.shape[0]
    mavg = jnp.asarray(_np_group_avg(GROUP_WIDTH))
    gfold = jnp.asarray(_np_group_fold(GROUP_WIDTH))

    def body(u_ref, v_ref, dy_ref, w_ref, wt_ref, bm_ref, mavg_ref, gfold_ref, du_ref, dv_ref, dw_ref, dbs_ref, dbm_ref):
        @pl.when(pl.program_id(0) == 0)
        def _():
            dw_ref[...] = jnp.zeros_like(dw_ref)
            dbm_ref[...] = jnp.zeros_like(dbm_ref)

        lane_group = lax.broadcasted_iota(jnp.int32, (1, GROUP_WIDTH), 1) // HEAD_DIM
        for c in range(SGU_TM // SGU_CHUNK):
            rows = pl.ds(c * SGU_CHUNK, SGU_CHUNK)
            (u, vn), pre_vjp = jax.vjp(functools.partial(_sgu_pre, mavg=mavg_ref[...]), u_ref[rows, :], v_ref[rows, :])
            vnb = vn.astype(BF16)
            mixed = _sgu_mix(w_ref, vnb, bm_ref[...])
            dyv = dy_ref[rows, :]
            dmixed = dyv * u
            dbm_ref[...] += dmixed
            dvn = jnp.zeros_like(vn)
            for g in range(N_HEADS):
                dm_g = jnp.where(lane_group == g, dmixed, 0.0).astype(BF16)
                dw_ref[g] += lax.dot_general(dm_g, vnb, _DIMS["nt"], preferred_element_type=F32)
                dvn = dvn + jnp.dot(wt_ref[g], dm_g, preferred_element_type=F32)
            du_pre, dv_pre = pre_vjp((dyv * mixed, dvn))
            du_ref[rows, :] = du_pre
            dv_ref[rows, :] = dv_pre

        @pl.when(pl.program_id(0) == t // SGU_TM - 1)
        def _():
            dbs_ref[...] = jnp.dot(dbm_ref[...], gfold_ref[...], precision=HI, preferred_element_type=F32)

    full = lambda a: pl.BlockSpec(a.shape, lambda i: (0,) * a.ndim)
    row = pl.BlockSpec((SGU_TM, GROUP_WIDTH), lambda i: (i, 0))
    return pl.pallas_call(
        body, name=name, grid=(t // SGU_TM,),
        in_specs=[row, pl.BlockSpec((SGU_TM, GROUP_WIDTH), lambda i: (i, 1)), row, full(w_s), full(w_s_t), full(bm), full(mavg),
                  full(gfold)],
        out_specs=[row, row, pl.BlockSpec((N_HEADS, SGU_CHUNK, SGU_CHUNK), lambda i: (0, 0, 0)),
                   pl.BlockSpec((SGU_CHUNK, 128), lambda i: (0, 0))],
        out_shape=[jax.ShapeDtypeStruct((t, GROUP_WIDTH), F32)] * 2 + [jax.ShapeDtypeStruct((N_HEADS, SGU_CHUNK, SGU_CHUNK), F32),
                                                                      jax.ShapeDtypeStruct((SGU_CHUNK, 128), F32)],
        scratch_shapes=[pltpu.VMEM((SGU_CHUNK, GROUP_WIDTH), F32)],
        compiler_params=_cparams(("arbitrary",)),
    )(z, z, dy, w_s, w_s_t, bm, mavg, gfold)


def _head_lanes():
    lane_head = lax.broadcasted_iota(jnp.int32, (1, 2 * HEAD_DIM), 1) // HEAD_DIM
    return lane_head == 0, lane_head == 1


def _stack_heads(x2):
    h0, h1 = _head_lanes()
    zero = jnp.zeros_like(x2)
    return jnp.concatenate([jnp.where(h0, x2, zero), jnp.where(h1, x2, zero)], axis=0)


def _unstack_heads(y):
    r = y.shape[0] // 2
    h0, _ = _head_lanes()
    return jnp.where(h0, y[:r], y[r:])


def _pair_softmax(qs, k2, biases, valid):
    s = lax.dot_general(qs, k2, _DIMS["nt"], preferred_element_type=F32)
    if biases is not None:
        s = s + jnp.concatenate(biases, axis=0)
    if valid is not None:
        s = jnp.where(jnp.concatenate([valid, valid], axis=0), s, -1e30)
    m = jnp.max(s, axis=-1, keepdims=True)
    e = jnp.exp(s - m)
    l = jnp.sum(e, axis=-1, keepdims=True)
    return e / l, m + jnp.log(l)


def _attn_pair_fwd(q2, k2, v2, biases, valid):
    p, lse = _pair_softmax(_stack_heads(q2), k2, biases, valid)
    o = jnp.dot(p.astype(BF16), v2, preferred_element_type=F32)
    return _unstack_heads(o), _unstack_heads(jnp.broadcast_to(lse, o.shape))


def _attn_pair_bwd(q2, k2, v2, biases, valid, do2, dlse2):
    r = q2.shape[0]
    qs = _stack_heads(q2)
    p, _ = _pair_softmax(qs, k2, biases, valid)
    dos = _stack_heads(do2).astype(BF16)
    dp = lax.dot_general(dos, v2, _DIMS["nt"], preferred_element_type=F32)
    delta = jnp.sum(dp * p, axis=-1, keepdims=True)
    if dlse2 is not None:
        delta = delta - jnp.sum(_stack_heads(dlse2), axis=-1, keepdims=True)
    ds = p * (dp - delta)
    dsb = ds.astype(BF16)
    dq2 = _unstack_heads(jnp.dot(dsb, k2, preferred_element_type=F32))
    dk2 = lax.dot_general(dsb, qs, _DIMS["tn"], preferred_element_type=F32)
    dv2 = lax.dot_general(p.astype(BF16), dos, _DIMS["tn"], preferred_element_type=F32)
    return dq2, dk2, dv2, [ds[:r], ds[r:]]


def _dil_valid(r0, length):
    row = lax.broadcasted_iota(jnp.int32, (DIL_QB, DIL_WIN), 0)
    col = lax.broadcasted_iota(jnp.int32, (DIL_QB, DIL_WIN), 1)
    off = col - DIL_HALF - row
    kpos = r0 - DIL_HALF + col
    return (jnp.abs(off) <= DIL_HALF) & (kpos >= 0) & (kpos < length)


def _dil_build_bias(tab_ref, bkt_ref, bias_ref, dil):
    bkt = bkt_ref[...]
    for h in range(N_HEADS):
        acc = jnp.zeros((DIL_QB, DIL_WIN), F32)
        for b in _dil_live_buckets(dil):
            acc = jnp.where(bkt == b, tab_ref[b, h], acc)
        bias_ref[h] = acc


def _dil_fill_pad(pad_ref, src_ref, length):
    zeros = jnp.zeros((DIL_HALF, GROUP_WIDTH), pad_ref.dtype)
    pad_ref[pl.ds(0, DIL_HALF), :] = zeros
    pad_ref[pl.ds(DIL_HALF + length, DIL_HALF), :] = zeros
    pad_ref[pl.ds(DIL_HALF, length), :] = src_ref[...]


def _dil_specs(bsz, length, dil):
    view = lambda a: a.reshape(bsz, length, dil * GROUP_WIDTH)
    blk = pl.BlockSpec((None, DIL_QB, GROUP_WIDTH), lambda b, rho, i: (b, i, rho))
    seq = pl.BlockSpec((None, length, GROUP_WIDTH), lambda b, rho, i: (b, 0, rho))
    return view, blk, seq


def _dil_fwd(qb, kb, vb, table, dil, name):
    length = SEQ // dil
    bsz = qb.shape[0] // length
    bkt = jnp.asarray(_np_dil_buckets(dil))
    view, blk, seq = _dil_specs(bsz, length, dil)

    def body(tab_ref, bkt_ref, q_ref, k_ref, v_ref, o_ref, lse_ref, kpad, vpad, bias_ref):
        i = pl.program_id(2)

        @pl.when((pl.program_id(0) == 0) & (pl.program_id(1) == 0) & (i == 0))
        def _():
            _dil_build_bias(tab_ref, bkt_ref, bias_ref, dil)

        @pl.when(i == 0)
        def _():
            _dil_fill_pad(kpad, k_ref, length)
            _dil_fill_pad(vpad, v_ref, length)

        r0 = pl.multiple_of(i * DIL_QB, DIL_QB)
        valid = _dil_valid(r0, length)
        for m in range(N_HEADS // 2):
            lanes = pl.ds(m * 128, 128)
            o2, lse2 = _attn_pair_fwd(q_ref[:, lanes], kpad[pl.ds(r0, DIL_WIN), lanes], vpad[pl.ds(r0, DIL_WIN), lanes],
                                      (bias_ref[2 * m], bias_ref[2 * m + 1]), valid)
            o_ref[:, lanes] = o2
            lse_ref[:, lanes] = lse2

    out = jax.ShapeDtypeStruct((bsz, length, dil * GROUP_WIDTH), F32)
    o, lse = pl.pallas_call(
        body, name=name, grid=(bsz, dil, length // DIL_QB),
        in_specs=[pl.BlockSpec(memory_space=pltpu.SMEM), pl.BlockSpec(bkt.shape, lambda b, rho, i: (0, 0)), blk, seq, seq],
        out_specs=[blk, blk], out_shape=[out, out],
        scratch_shapes=[pltpu.VMEM((length + 2 * DIL_HALF, GROUP_WIDTH), BF16), pltpu.VMEM((length + 2 * DIL_HALF, GROUP_WIDTH), BF16),
                        pltpu.VMEM((N_HEADS, DIL_QB, DIL_WIN), F32)],
        compiler_params=_cparams(("arbitrary", "arbitrary", "arbitrary")),
    )(table, bkt, view(qb), view(kb), view(vb))
    return o.reshape(qb.shape), lse.reshape(qb.shape)


def _dil_bwd(qb, kb, vb, do, dlse, table, dil, name):
    length = SEQ // dil
    bsz = qb.shape[0] // length
    nqb = length // DIL_QB
    bkt = jnp.asarray(_np_dil_buckets(dil))
    view, blk, seq = _dil_specs(bsz, length, dil)

    def body(tab_ref, bkt_ref, q_ref, k_ref, v_ref, do_ref, dlse_ref, dq_ref, dk_ref, dv_ref, dsc_ref, kpad, vpad, bias_ref):
        i = pl.program_id(2)

        @pl.when((pl.program_id(0) == 0) & (pl.program_id(1) == 0) & (i == 0))
        def _():
            _dil_build_bias(tab_ref, bkt_ref, bias_ref, dil)
            dsc_ref[...] = jnp.zeros_like(dsc_ref)

        @pl.when(i == 0)
        def _():
            _dil_fill_pad(kpad, k_ref, length)
            _dil_fill_pad(vpad, v_ref, length)
            dk_ref[...] = jnp.zeros_like(dk_ref)
            dv_ref[...] = jnp.zeros_like(dv_ref)

        r0 = pl.multiple_of(i * DIL_QB, DIL_QB)
        valid = _dil_valid(r0, length)
        for m in range(N_HEADS // 2):
            lanes = pl.ds(m * 128, 128)
            dq2, dk2, dv2, ds_heads = _attn_pair_bwd(
                q_ref[:, lanes], kpad[pl.ds(r0, DIL_WIN), lanes], vpad[pl.ds(r0, DIL_WIN), lanes],
                (bias_ref[2 * m], bias_ref[2 * m + 1]), valid, do_ref[:, lanes], dlse_ref[:, lanes])
            dq_ref[:, lanes] = dq2
            dsc_ref[2 * m] += ds_heads[0]
            dsc_ref[2 * m + 1] += ds_heads[1]
            for first, size, live in ((0, DIL_HALF, i >= 1), (DIL_HALF, DIL_QB, None), (DIL_HALF + DIL_QB, DIL_HALF, i <= nqb - 2)):
                def add(first=first, size=size, dk2=dk2, dv2=dv2, lanes=lanes):
                    rows = pl.ds(pl.multiple_of(r0 - DIL_HALF + first, DIL_HALF), size)
                    dk_ref[rows, lanes] += dk2[first:first + size]
                    dv_ref[rows, lanes] += dv2[first:first + size]
                if live is None:
                    add()
                else:
                    pl.when(live)(add)

    out = jax.ShapeDtypeStruct((bsz, length, dil * GROUP_WIDTH), F32)
    dsc_shape = (N_HEADS, DIL_QB, DIL_WIN)
    dq, dk, dv, dsc = pl.pallas_call(
        body, name=name, grid=(bsz, dil, nqb),
        in_specs=[pl.BlockSpec(memory_space=pltpu.SMEM), pl.BlockSpec(bkt.shape, lambda b, rho, i: (0, 0)), blk, seq, seq, blk, blk],
        out_specs=[blk, seq, seq, pl.BlockSpec(dsc_shape, lambda b, rho, i: (0, 0, 0))],
        out_shape=[out, out, out, jax.ShapeDtypeStruct(dsc_shape, F32)],
        scratch_shapes=[pltpu.VMEM((length + 2 * DIL_HALF, GROUP_WIDTH), BF16), pltpu.VMEM((length + 2 * DIL_HALF, GROUP_WIDTH), BF16),
                        pltpu.VMEM(dsc_shape, F32)],
        compiler_params=_cparams(("arbitrary", "arbitrary", "arbitrary")),
    )(table, bkt, view(qb), view(kb), view(vb), view(do), view(dlse))
    return dq.reshape(qb.shape), dk.reshape(qb.shape), dv.reshape(qb.shape), dsc


def _headnorm(x, g, mavg):
    return x * lax.rsqrt(_head_stat(x * x, mavg) + RMS_EPS) * g


def _fold_gain(dg_full, fold):
    return jnp.dot(jnp.broadcast_to(dg_full, (8, dg_full.shape[1])), fold, precision=HI, preferred_element_type=F32)


def _bprep_fn(qp, kp, gq, gk, mavg):
    return _headnorm(qp, gq, mavg) * (HEAD_DIM ** -0.5), _headnorm(kp, gk, mavg)


def _dil_layout(dil):
    return None if dil == 1 else Strided(dil)


def _bprep_fwd(z, gq, gk, name):
    mavg = jnp.asarray(_np_group_avg(GROUP_WIDTH))

    def fn(qp, kp, vp, gqv, gkv, mv):
        qb, kb = _bprep_fn(qp, kp, gqv, gkv, mv)
        return (qb, kb, vp) * len(DIL_PATTERNS)

    w = GROUP_WIDTH
    outs = [(w, BF16, _dil_layout(dil)) for _, dil in DIL_PATTERNS for _ in range(3)]
    res = _rowmap(fn, [(z, w, 2), (z, w, 3), (z, w, 4)], [gq, gk, mavg], outs, [], name, 512, z.shape[0])
    return [res[3 * i:3 * i + 3] for i in range(len(DIL_PATTERNS))]


def _bprep_bwd(z, dqs, dks, dvs, gq, gk, name):
    mavg = jnp.asarray(_np_group_avg(GROUP_WIDTH))
    fold = jnp.asarray(_np_tile_fold(GROUP_WIDTH))

    def fn(qp, kp, dq0, dq1, dq2, dk0, dk1, dk2, dv0, dv1, dv2, gqv, gkv, mv, fv):
        _, vjp = jax.vjp(functools.partial(_bprep_fn, mavg=mv), qp, kp, gqv, gkv)
        dqp, dkp, dgq, dgk = vjp((dq0 + dq1 + dq2, dk0 + dk1 + dk2))
        return dqp, dkp, dv0 + dv1 + dv2, _fold_gain(dgq, fv), _fold_gain(dgk, fv)

    w = GROUP_WIDTH
    rows = [(z, w, 2), (z, w, 3)] + _pattern_rows(dqs) + _pattern_rows(dks) + _pattern_rows(dvs)
    return _rowmap(fn, rows, [gq, gk, mavg, fold], [(w, F32)] * 3, [(8, HEAD_DIM)] * 2, name, 256, z.shape[0])


def _mixture_fn(o0, o1, o2, l0, l1, l2):
    m = lax.stop_gradient(jnp.maximum(jnp.maximum(l0, l1), l2))
    e0, e1, e2 = jnp.exp(l0 - m), jnp.exp(l1 - m), jnp.exp(l2 - m)
    return (e0 * o0 + e1 * o1 + e2 * o2) / (e0 + e1 + e2)


def _pattern_rows(arrs):
    return [(a, GROUP_WIDTH, 0) if dil == 1 else (a, GROUP_WIDTH, 0, Strided(dil)) for a, (_, dil) in zip(arrs, DIL_PATTERNS)]


def _mixture_fwd(os_, ls_, name):
    n = os_[0].shape[0]
    return _rowmap(lambda *v: (_mixture_fn(*v),), _pattern_rows(os_) + _pattern_rows(ls_), [], [(GROUP_WIDTH, F32)], [], name, 512, n)[0]


def _mixture_bwd(os_, ls_, dy, name):
    w = GROUP_WIDTH

    def fn(*v):
        _, vjp = jax.vjp(_mixture_fn, *v[:6])
        return vjp(v[6])

    outs = [(w, F32, _dil_layout(dil)) for _ in range(2) for _, dil in DIL_PATTERNS]
    return _rowmap(fn, _pattern_rows(os_) + _pattern_rows(ls_) + [(dy, w, 0)], [], outs, [], name, 512, dy.shape[0])


def _relbias_fold(dscs, name):
    bkts = [jnp.asarray(_np_dil_buckets(dil)) for _, dil in DIL_PATTERNS]
    npat = len(DIL_PATTERNS)

    def body(*refs):
        bkt_refs, d_refs, o_ref = refs[:npat], refs[npat:-1], refs[-1]
        row = lax.broadcasted_iota(jnp.int32, (REL_BUCKETS, 128), 0)
        lane = lax.broadcasted_iota(jnp.int32, (REL_BUCKETS, 128), 1)
        out = jnp.zeros((REL_BUCKETS, 128), F32)
        for p, (_, dil) in enumerate(DIL_PATTERNS):
            bkt = bkt_refs[p][...]
            for h in range(N_HEADS):
                d = d_refs[2 * p][h] + d_refs[2 * p + 1][h]
                for b in _dil_live_buckets(dil):
                    val = jnp.sum(jnp.where(bkt == b, d, 0.0), keepdims=True)
                    out = out + jnp.where((row == b) & (lane == h), val, 0.0)
        o_ref[...] = out

    return pl.pallas_call(
        body, name=name, out_shape=jax.ShapeDtypeStruct((REL_BUCKETS, 128), F32), compiler_params=_cparams(),
    )(*bkts, *dscs)


DPREP_TM = 512


def _dprep_fn(qp, kp, vp, gq, gk, cq, sq, ck, sk, mavg_q, mavg_k, perm_q, perm_k, expand):
    rot = lambda x, perm: jnp.dot(x, perm, precision=SPLIT3, preferred_element_type=F32)
    qn = _headnorm(qp, gq, mavg_q)
    kn = _headnorm(kp, gk, mavg_k)
    qr = (qn * cq + rot(qn, perm_q) * sq) * (HEAD_DIM ** -0.5)
    kr = kn * ck + rot(kn, perm_k) * sk
    return qr, rot(kr, expand), rot(vp, expand)


def _dprep_consts():
    cq, sq = _np_rope_tables(N_HEADS)
    ck, sk = _np_rope_tables(KV_WIDTH // HEAD_DIM)
    tables = [jnp.asarray(a) for a in (cq, sq, ck, sk)]
    mats = [jnp.asarray(a) for a in (_np_group_avg(GROUP_WIDTH), _np_group_avg(KV_WIDTH), _np_rope_partner(GROUP_WIDTH),
                                      _np_rope_partner(KV_WIDTH), _np_kv_expand())]
    per = SEQ // DPREP_TM
    w, kw = GROUP_WIDTH, KV_WIDTH
    table_rows = [(tables[0], w, 0, per), (tables[1], w, 0, per), (tables[2], kw, 0, per), (tables[3], kw, 0, per)]
    return table_rows, mats


def _dprep_fwd(z, gq, gk, name):
    table_rows, mats = _dprep_consts()
    w, kw = GROUP_WIDTH, KV_WIDTH

    def fn(qp, kp, vp, cq, sq, ck, sk, gqv, gkv, *m):
        return _dprep_fn(qp, kp, vp, gqv, gkv, cq, sq, ck, sk, *m)

    return _rowmap(fn, [(z, w, 7), (z, kw, 32), (z, kw, 33)] + table_rows, [gq, gk] + mats, [(w, BF16)] * 3, [], name,
                   DPREP_TM, z.shape[0])


def _dprep_bwd(z, dq, dkx, dvx, gq, gk, name):
    table_rows, mats = _dprep_consts()
    fold_q = jnp.asarray(_np_tile_fold(GROUP_WIDTH))
    fold_k = jnp.asarray(_np_tile_fold(KV_WIDTH))
    w, kw = GROUP_WIDTH, KV_WIDTH

    def fn(qp, kp, vp, dqv, dkv, dvv, cq, sq, ck, sk, gqv, gkv, fq, fk, *m):
        f = lambda a, b, c, d, e: _dprep_fn(a, b, c, d, e, cq, sq, ck, sk, *m)
        _, vjp = jax.vjp(f, qp, kp, vp, gqv, gkv)
        dqp, dkp, dvp, dgq, dgk = vjp((dqv, dkv, dvv))
        return dqp, dkp, dvp, _fold_gain(dgq, fq), _fold_gain(dgk, fk)

    return _rowmap(fn, [(z, w, 7), (z, kw, 32), (z, kw, 33), (dq, w, 0), (dkx, w, 0), (dvx, w, 0)] + table_rows,
                   [gq, gk, fold_q, fold_k] + mats, [(w, F32), (kw, F32), (kw, F32)], [(8, HEAD_DIM)] * 2, name,
                   DPREP_TM, z.shape[0])


GQA_QB = 256


def _gqa_fwd(q, kx, vx, name):
    bsz = q.shape[0]
    blk = pl.BlockSpec((None, GQA_QB, GROUP_WIDTH), lambda b, i: (b, i, 0))
    seq = pl.BlockSpec((None, SEQ, GROUP_WIDTH), lambda b, i: (b, 0, 0))

    def body(q_ref, k_ref, v_ref, o_ref):
        for m in range(N_HEADS // 2):
            lanes = pl.ds(m * 128, 128)
            o_ref[:, lanes] = _attn_pair_fwd(q_ref[:, lanes], k_ref[:, lanes], v_ref[:, lanes], None, None)[0]

    return pl.pallas_call(
        body, name=name, grid=(bsz, SEQ // GQA_QB), in_specs=[blk, seq, seq], out_specs=blk,
        out_shape=jax.ShapeDtypeStruct((bsz, SEQ, GROUP_WIDTH), F32), compiler_params=_cparams(("parallel", "parallel")),
    )(q, kx, vx)


def _gqa_bwd(q, kx, vx, do, name):
    bsz = q.shape[0]
    blk = pl.BlockSpec((None, GQA_QB, GROUP_WIDTH), lambda b, i: (b, i, 0))
    seq = pl.BlockSpec((None, SEQ, GROUP_WIDTH), lambda b, i: (b, 0, 0))

    def body(q_ref, k_ref, v_ref, do_ref, dq_ref, dk_ref, dv_ref):
        @pl.when(pl.program_id(1) == 0)
        def _():
            dk_ref[...] = jnp.zeros_like(dk_ref)
            dv_ref[...] = jnp.zeros_like(dv_ref)

        for m in range(N_HEADS // 2):
            lanes = pl.ds(m * 128, 128)
            dq2, dk2, dv2, _ = _attn_pair_bwd(q_ref[:, lanes], k_ref[:, lanes], v_ref[:, lanes], None, None, do_ref[:, lanes], None)
            dq_ref[:, lanes] = dq2
            dk_ref[:, lanes] += dk2
            dv_ref[:, lanes] += dv2

    out = jax.ShapeDtypeStruct((bsz, SEQ, GROUP_WIDTH), F32)
    return pl.pallas_call(
        body, name=name, grid=(bsz, SEQ // GQA_QB), in_specs=[blk, seq, seq, blk], out_specs=[blk, seq, seq],
        out_shape=[out, out, out], compiler_params=_cparams(("parallel", "arbitrary")),
    )(q, kx, vx, do)


CONV_TILE = 64
CONV_LEAD = 16
CONV_WINDOW = CONV_TILE + 32


def _glu(a, g):
    return a * jax.nn.sigmoid(g)


def _conv_post(c, b, ln_g, ln_b):
    x = c + b
    xc = x - jnp.mean(x, axis=-1, keepdims=True)
    y = xc * lax.rsqrt(jnp.mean(xc * xc, axis=-1, keepdims=True) + LN_EPS) * ln_g + ln_b
    return y * jax.nn.sigmoid(y)


def _conv_shifts(win):
    out = []
    for phase in range(8):
        rolled = win if phase == 0 else pltpu.roll(win, CONV_WINDOW - phase, 0)
        for base in range(0, CONV_WINDOW - CONV_TILE + 1, 8):
            if 1 <= base + phase <= CONV_WIDTH:
                out.append((base + phase, rolled[base:base + CONV_TILE]))
    return out


def _conv_fill(pad_ref, value_of_tile):
    zeros = jnp.zeros((CONV_LEAD, GROUP_WIDTH), F32)
    pad_ref[pl.ds(0, CONV_LEAD), :] = zeros
    pad_ref[pl.ds(CONV_LEAD + SEQ, CONV_LEAD), :] = zeros

    def step(t, carry):
        r0 = pl.multiple_of(t * CONV_TILE, CONV_TILE)
        pad_ref[pl.ds(CONV_LEAD + r0, CONV_TILE), :] = value_of_tile(r0)
        return carry

    lax.fori_loop(0, SEQ // CONV_TILE, step, 0)


def _conv_tile(pad_ref, w_ref, r0, flip):
    acc = jnp.zeros((CONV_TILE, GROUP_WIDTH), F32)
    for offset, rows in _conv_shifts(pad_ref[pl.ds(r0, CONV_WINDOW), :]):
        k = (CONV_WIDTH - offset) if flip else (offset - 1)
        acc = acc + w_ref[pl.ds(k, 1), :] * rows
    return acc


def _conv_fwd(z3, w, b, ln_g, ln_b, name):
    bsz = z3.shape[0]
    seq = lambda cb: pl.BlockSpec((None, SEQ, GROUP_WIDTH), functools.partial(lambda i, cb: (i, 0, cb), cb=cb))
    full = lambda a: pl.BlockSpec(a.shape, lambda i: (0,) * a.ndim)

    def body(a_ref, g_ref, w_ref, b_ref, lg_ref, lb_ref, y_ref, pad_ref):
        _conv_fill(pad_ref, lambda r0: _glu(a_ref[pl.ds(r0, CONV_TILE), :], g_ref[pl.ds(r0, CONV_TILE), :]))

        def step(t, carry):
            r0 = pl.multiple_of(t * CONV_TILE, CONV_TILE)
            y_ref[pl.ds(r0, CONV_TILE), :] = _conv_post(_conv_tile(pad_ref, w_ref, r0, False), b_ref[...], lg_ref[...], lb_ref[...])
            return carry

        lax.fori_loop(0, SEQ // CONV_TILE, step, 0)

    return pl.pallas_call(
        body, name=name, grid=(bsz,), in_specs=[seq(5), seq(6), full(w), full(b), full(ln_g), full(ln_b)], out_specs=seq(0),
        out_shape=jax.ShapeDtypeStruct((bsz, SEQ, GROUP_WIDTH), F32),
        scratch_shapes=[pltpu.VMEM((SEQ + 2 * CONV_LEAD, GROUP_WIDTH), F32)], compiler_params=_cparams(("parallel",)),
    )(z3, z3, w, b, ln_g, ln_b)


def _conv_bwd(z3, dy, w, b, ln_g, ln_b, name):
    bsz = z3.shape[0]
    seq = lambda cb: pl.BlockSpec((None, SEQ, GROUP_WIDTH), functools.partial(lambda i, cb: (i, 0, cb), cb=cb))
    full = lambda a: pl.BlockSpec(a.shape, lambda i: (0,) * a.ndim)
    vec = pl.BlockSpec((1, GROUP_WIDTH), lambda i: (0, 0))

    def body(a_ref, g_ref, dy_ref, w_ref, b_ref, lg_ref, lb_ref, da_ref, dg_ref, dw_ref, db_ref, dlg_ref, dlb_ref, hpad, dpad, dw8):
        @pl.when(pl.program_id(0) == 0)
        def _():
            dw8[...] = jnp.zeros_like(dw8)
            db_ref[...] = jnp.zeros_like(db_ref)
            dlg_ref[...] = jnp.zeros_like(dlg_ref)
            dlb_ref[...] = jnp.zeros_like(dlb_ref)

        _conv_fill(hpad, lambda r0: _glu(a_ref[pl.ds(r0, CONV_TILE), :], g_ref[pl.ds(r0, CONV_TILE), :]))
        zeros = jnp.zeros((CONV_LEAD, GROUP_WIDTH), F32)
        dpad[pl.ds(0, CONV_LEAD), :] = zeros
        dpad[pl.ds(CONV_LEAD + SEQ, CONV_LEAD), :] = zeros

        def through_post(t, carry):
            r0 = pl.multiple_of(t * CONV_TILE, CONV_TILE)
            conv = _conv_tile(hpad, w_ref, r0, False)
            _, vjp = jax.vjp(_conv_post, conv, b_ref[...], lg_ref[...], lb_ref[...])
            dconv, db, dlg, dlb = vjp(dy_ref[pl.ds(r0, CONV_TILE), :])
            db_ref[...] += db
            dlg_ref[...] += dlg
            dlb_ref[...] += dlb
            dpad[pl.ds(CONV_LEAD + r0, CONV_TILE), :] = dconv
            for offset, rows in _conv_shifts(hpad[pl.ds(r0, CONV_WINDOW), :]):
                prod = dconv * rows
                part = prod[0:8]
                for j in range(1, CONV_TILE // 8):
                    part = part + prod[8 * j:8 * j + 8]
                dw8[offset - 1] += part
            return carry

        lax.fori_loop(0, SEQ // CONV_TILE, through_post, 0)

        def through_glu(t, carry):
            r0 = pl.multiple_of(t * CONV_TILE, CONV_TILE)
            dh = _conv_tile(dpad, w_ref, r0, True)
            rows = pl.ds(r0, CONV_TILE)
            _, vjp = jax.vjp(_glu, a_ref[rows, :], g_ref[rows, :])
            da, dg = vjp(dh)
            da_ref[rows, :] = da
            dg_ref[rows, :] = dg
            return carry

        lax.fori_loop(0, SEQ // CONV_TILE, through_glu, 0)
        dw_ref[...] = jnp.sum(dw8[...], axis=1)

    out = jax.ShapeDtypeStruct((bsz, SEQ, GROUP_WIDTH), F32)
    v = jax.ShapeDtypeStruct((1, GROUP_WIDTH), F32)
    return pl.pallas_call(
        body, name=name, grid=(bsz,), in_specs=[seq(5), seq(6), seq(0), full(w), full(b), full(ln_g), full(ln_b)],
        out_specs=[seq(0), seq(0), pl.BlockSpec((CONV_WIDTH, GROUP_WIDTH), lambda i: (0, 0)), vec, vec, vec],
        out_shape=[out, out, jax.ShapeDtypeStruct((CONV_WIDTH, GROUP_WIDTH), F32), v, v, v],
        scratch_shapes=[pltpu.VMEM((SEQ + 2 * CONV_LEAD, GROUP_WIDTH), F32), pltpu.VMEM((SEQ + 2 * CONV_LEAD, GROUP_WIDTH), F32),
                        pltpu.VMEM((CONV_WIDTH, 8, GROUP_WIDTH), F32)],
        compiler_params=_cparams(("arbitrary",)),
    )(z3, z3, dy, w, b, ln_g, ln_b)


def _mixnorm_fwd(ys, gains, name):
    w = GROUP_WIDTH

    def fn(*v):
        return (jnp.concatenate([_rms(v[i], v[4 + i]) for i in range(4)], axis=-1),)

    return _rowmap(fn, [(y, w, 0) for y in ys], list(gains), [(4 * w, BF16)], [], name, 512, ys[0].shape[0])[0]


def _mixnorm_bwd(dyn, ys, gains, name, follow=None):
    w = GROUP_WIDTH
    gains = list(gains) if follow is None else [*gains, follow]

    def fn(*v):
        dys, dgs = [], []
        for i in range(4):
            _, vjp = jax.vjp(_rms, v[4 + i], v[8 + i])
            dy, dg = vjp(v[i])
            dys.append(dy)
            dgs.append(dg)
        return (*dys, *dgs)

    rows = [(dyn, w, i) for i in range(4)] + [(y, w, 0) for y in ys]
    return _rowmap(fn, rows, list(gains), [(w, F32)] * 4, [(1, w)] * 4, name, 512, dyn.shape[0])


def _adamw_fn(w, g, m, v):
    m = ADAM_B1 * m + (1.0 - ADAM_B1) * g
    v = ADAM_B2 * v + (1.0 - ADAM_B2) * (g * g)
    m_hat = m / (1.0 - ADAM_B1 ** ADAM_STEP)
    v_hat = v / (1.0 - ADAM_B2 ** ADAM_STEP)
    delta = -ADAM_LR * (m_hat / (jnp.sqrt(v_hat) + ADAM_EPS) + ADAM_WD * w)
    return delta, m, v


def _adamw(w, g, m, v, name):
    r, c = w.shape
    tm = _pick(r, (256, 128, 64, 32, 16, 8))
    return _rowmap(_adamw_fn, [(a, c, 0) for a in (w, g, m, v)], [], [(c, F32)] * 3, [], name, tm, r)


def _layer_params(l, small, big, prefetch):
    tile_row = lambda g, n: jnp.tile(g, n)[None, :]
    row = lambda g: g[None, :]
    w_s = small["sgu_w"][l].astype(BF16)
    return dict(
        norm1_g=row(small["norm1_g"][l]), norm2_g=row(small["norm2_g"][l]),
        w_s=w_s, w_s_t=jnp.swapaxes(w_s, 1, 2), bm=jnp.repeat(small["sgu_b"][l].T, HEAD_DIM, axis=1),
        gq_dil=tile_row(small["dil_qn_g"][l], N_HEADS), gk_dil=tile_row(small["dil_kn_g"][l], N_HEADS),
        conv_w=small["conv_w"][l], conv_b=row(small["conv_b"][l]), conv_ln_g=row(small["conv_ln_g"][l]),
        conv_ln_b=row(small["conv_ln_b"][l]),
        gq_gqa=tile_row(small["gqa_qn_g"][l], N_HEADS), gk_gqa=tile_row(small["gqa_kn_g"][l], KV_WIDTH // HEAD_DIM),
        mix_g=[row(small["mix_norm_g"][l][i * GROUP_WIDTH:(i + 1) * GROUP_WIDTH]) for i in range(4)],
        big=big, prefetch=prefetch,
    )


def _layer_fwd(x, p, table, bsz, tag):
    t = x.shape[0]
    seq3 = lambda a: a.reshape(bsz, SEQ, a.shape[-1])
    flat = lambda a: a.reshape(t, a.shape[-1])
    h1 = _rmsnorm_fwd(x, p["norm1_g"], tag + "rms1")
    z = _matmul([(h1, p["big"]("w_in", h1))], "nn", F32, tag + "mm_z")
    y_a = _sgu_fwd(z, p["w_s"], p["bm"], tag + "sgu_fwd")
    p["prefetch"](0, "w_out", y_a)
    dil_qkv = _bprep_fwd(z, p["gq_dil"], p["gk_dil"], tag + "dil_prep")
    outs, lses = [], []
    for (_, dil), (qb, kb, vb) in zip(DIL_PATTERNS, dil_qkv):
        o, lse = _dil_fwd(qb, kb, vb, table, dil, f"{tag}dil{dil}_fwd")
        outs.append(o)
        lses.append(lse)
    y_b = _mixture_fwd(outs, lses, tag + "dil_mix")
    y_c = flat(_conv_fwd(seq3(z), p["conv_w"], p["conv_b"], p["conv_ln_g"], p["conv_ln_b"], tag + "conv_fwd"))
    qd, kx, vx = _dprep_fwd(z, p["gq_gqa"], p["gk_gqa"], tag + "gqa_prep")
    y_d = flat(_gqa_fwd(seq3(qd), seq3(kx), seq3(vx), tag + "gqa_fwd"))
    ys = [y_a, y_b, y_c, y_d]
    p["prefetch"](0, "w_gate", y_d)
    yn = _mixnorm_fwd(ys, p["mix_g"], tag + "mixnorm")
    x_mid = _matmul([(yn, p["big"]("w_out", yn))], "nn", F32, tag + "mm_out", residual=x)
    h2 = _rmsnorm_fwd(x_mid, p["norm2_g"], tag + "rms2")
    act, act_du, act_dg = _ffn_up(h2, p["big"]("w_gate", h2), p["big"]("w_up", h2), tag + "ffn_up")
    p["prefetch"](1, "w_in", act)
    x_out = _matmul([(act, p["big"]("w_down", act))], "nn", F32, tag + "mm_down", residual=x_mid)
    saved = dict(x=x, h1=h1, z=z, dil_qkv=dil_qkv, outs=outs, lses=lses, qd=qd, kx=kx, vx=vx, ys=ys, yn=yn, x_mid=x_mid,
                 h2=h2, act=act, act_du=act_du, act_dg=act_dg)
    return x_out, saved


def _layer_bwd(dx_out, dx_out_b, s, p, table, bsz, tag, emit, mid_hook):
    t = dx_out.shape[0]
    seq3 = lambda a: a.reshape(bsz, SEQ, a.shape[-1])
    flat = lambda a: a.reshape(t, a.shape[-1])
    z = s["z"]
    small = {}
    weight = lambda name: p["big"](name, None)
    emit("w_down", _matmul([(s["act"], dx_out_b)], "tn", BF16, tag + "mm_dwdown").reshape(N_CHIPS, FFN_HIDDEN // N_CHIPS, D_MODEL))
    dgate, dup = _ffn_down_bwd(dx_out_b, weight("w_down"), s["act_du"], s["act_dg"], tag + "ffn_dact")
    emit("w_gate", _matmul([(s["h2"], dgate)], "tn", BF16, tag + "mm_dwgate", slabs=N_CHIPS))
    started = emit("w_up", _matmul([(s["h2"], dup)], "tn", BF16, tag + "mm_dwup", slabs=N_CHIPS))
    dh2 = _matmul([(dgate, weight("w_gate")), (dup, weight("w_up"))], "nt", F32, tag + "mm_dh2")
    dx_mid, dx_mid_b, dg2 = _rmsnorm_bwd(dh2, s["x_mid"], p["norm2_g"], dx_out, tag + "rms2_bwd", follow=started)
    small["norm2_g"] = dg2[0]
    mid_hook(dx_mid)
    dyn = _matmul([(dx_mid_b, weight("w_out"))], "nt", F32, tag + "mm_dyn")
    started = emit("w_out", _matmul([(s["yn"], dx_mid_b)], "tn", BF16, tag + "mm_dwout").reshape(N_CHIPS, D_MODEL // N_CHIPS, D_MODEL))
    *dys, dga, dgb, dgc, dgd = _mixnorm_bwd(dyn, s["ys"], p["mix_g"], tag + "mixnorm_bwd", follow=started)
    small["mix_norm_g"] = jnp.concatenate([dga[0], dgb[0], dgc[0], dgd[0]])
    du, dv, dws, dbs = _sgu_bwd(z, dys[0], p["w_s"], p["w_s_t"], p["bm"], tag + "sgu_bwd")
    small["sgu_w"] = dws
    small["sgu_b"] = dbs[:, :N_HEADS].T
    *douts, dl0, dl1, dl2 = _mixture_bwd(s["outs"], s["lses"], dys[1], tag + "dil_mix_bwd")
    dlses = [dl0, dl1, dl2]
    dqs, dks, dvs, dscs = [], [], [], []
    for i, (_, dil) in enumerate(DIL_PATTERNS):
        dq, dk, dvv, dsc = _dil_bwd(*s["dil_qkv"][i], douts[i], dlses[i], table, dil, f"{tag}dil{dil}_bwd")
        dqs.append(dq)
        dks.append(dk)
        dvs.append(dvv)
        dscs.append(dsc)
    dbq, dbk, dbv, dgq, dgk = _bprep_bwd(z, dqs, dks, dvs, p["gq_dil"], p["gk_dil"], tag + "dil_prep_bwd")
    small["dil_qn_g"], small["dil_kn_g"] = dgq[0], dgk[0]
    dca, dcg, dcw, dcb, dclg, dclb = _conv_bwd(seq3(z), seq3(dys[2]), p["conv_w"], p["conv_b"], p["conv_ln_g"], p["conv_ln_b"],
                                               tag + "conv_bwd")
    small["conv_w"], small["conv_b"], small["conv_ln_g"], small["conv_ln_b"] = dcw, dcb[0], dclg[0], dclb[0]
    dqd, dkx, dvx = _gqa_bwd(seq3(s["qd"]), seq3(s["kx"]), seq3(s["vx"]), seq3(dys[3]), tag + "gqa_bwd")
    ddq, ddk, ddv, dgq, dgk = _dprep_bwd(z, flat(dqd), flat(dkx), flat(dvx), p["gq_gqa"], p["gk_gqa"], tag + "gqa_prep_bwd")
    small["gqa_qn_g"], small["gqa_kn_g"] = dgq[0], dgk[0]
    dz = jnp.concatenate([a.astype(BF16) for a in (du, dv, dbq, dbk, dbv, flat(dca), flat(dcg), ddq, ddk, ddv)], axis=1)
    dz4 = dz.reshape(t, N_CHIPS, IN_WIDTH // N_CHIPS).transpose(1, 0, 2)
    started = emit("w_in", _matmul([(s["h1"], dz4)], "tn", BF16, tag + "mm_dwin", slabs=N_CHIPS))
    dh1 = _matmul([(dz, weight("w_in"))], "nt", F32, tag + "mm_dh1")
    dx, dx_b, dg1 = _rmsnorm_bwd(dh1, s["x"], p["norm1_g"], dx_mid, tag + "rms1_bwd", follow=started)
    small["norm1_g"] = dg1[0]
    return dx, dx_b, small, dscs


def _local_step(x, target, small, big, emit, mid_hook, bsz, prefetch=lambda l, name, after: None):
    table = small["rel_bias"]
    ahead = lambda l: (lambda more, name, after: prefetch(l + more, name, after))
    params = [_layer_params(l, small, functools.partial(big, l), ahead(l)) for l in range(DEPTH)]
    saved = []
    h = x
    for l in range(DEPTH):
        h, sv = _layer_fwd(h, params[l], table, bsz, f"l{l}_")
        saved.append(sv)
    dh, dh_b, loss = _loss_fwd_bwd(h, target, "loss")
    small_grads, dscs = [None] * DEPTH, [None] * DEPTH
    for l in reversed(range(DEPTH)):
        dh, dh_b, small_grads[l], dscs[l] = _layer_bwd(dh, dh_b, saved[l], params[l], table, bsz, f"l{l}_",
                                                       functools.partial(emit, l), functools.partial(mid_hook, l))
    fold_in = [dscs[l][i] for i in range(len(DIL_PATTERNS)) for l in range(DEPTH)]
    stacked = {k: jnp.stack([small_grads[l][k] for l in range(DEPTH)]) for k in small_grads[0]}
    stacked["rel_bias"] = _relbias_fold(fold_in, "relbias_fold")[:, :N_HEADS]
    return loss, dh, stacked


def _mesh_pos():
    return lax.axis_index("x"), lax.axis_index("y"), lax.axis_index("c")


def _other_chips(x, y):
    return [(1 - x, y), (x, 1 - y), (1 - x, 1 - y)]


_ANY = pl.BlockSpec(memory_space=pl.ANY)


def _swap_sibling(arrs, name):
    n = len(arrs)

    def body(*refs):
        in_refs, out_refs, send_sems, recv_sems = refs[:n], refs[n:2 * n], refs[2 * n], refs[2 * n + 1]
        x, y, c = _mesh_pos()
        copies = [pltpu.make_async_remote_copy(src_ref=in_refs[k], dst_ref=out_refs[k], send_sem=send_sems.at[k],
                                               recv_sem=recv_sems.at[k], device_id=(x, y, 1 - c), device_id_type=MESH)
                  for k in range(n)]
        for cp in copies:
            cp.start()
        for cp in copies:
            cp.wait()

    return pl.pallas_call(
        body, name=name, in_specs=[_ANY] * n, out_specs=[_ANY] * n,
        out_shape=[jax.ShapeDtypeStruct(a.shape, a.dtype) for a in arrs],
        scratch_shapes=[pltpu.SemaphoreType.DMA((n,)), pltpu.SemaphoreType.DMA((n,))],
    )(*arrs)


def _complete_pairs(arrs, name):
    n = len(arrs)

    def body(*refs):
        in_refs, out_refs, send_sems, recv_sems = refs[:n], refs[n:2 * n], refs[2 * n], refs[2 * n + 1]
        x, y, c = _mesh_pos()
        copies = [pltpu.make_async_remote_copy(src_ref=in_refs[k].at[c], dst_ref=out_refs[k].at[c], send_sem=send_sems.at[k],
                                               recv_sem=recv_sems.at[k], device_id=(x, y, 1 - c), device_id_type=MESH)
                  for k in range(n)]
        for cp in copies:
            cp.start()
        for k, cp in enumerate(copies):
            cp.wait_send()
            pltpu.make_async_remote_copy(src_ref=in_refs[k].at[1 - c], dst_ref=out_refs[k].at[1 - c], send_sem=send_sems.at[k],
                                         recv_sem=recv_sems.at[k], device_id=(x, y, 1 - c), device_id_type=MESH).wait_recv()

    return pl.pallas_call(
        body, name=name, in_specs=[_ANY] * n, out_specs=[_ANY] * n,
        out_shape=[jax.ShapeDtypeStruct(a.shape, a.dtype) for a in arrs], input_output_aliases={k: k for k in range(n)},
        scratch_shapes=[pltpu.SemaphoreType.DMA((n,)), pltpu.SemaphoreType.DMA((n,))],
    )(*arrs)


_HBM = pl.BlockSpec(memory_space=pltpu.HBM)
_SEM = pl.BlockSpec(memory_space=pltpu.SEMAPHORE)
_DATAFLOW = pltpu.SideEffectType.DATAFLOW_SIDE_EFFECTING


def _chip_copies(src_refs, land_refs, send_sems, recv_sems, scatter):
    x, y, c = _mesh_pos()
    me = 2 * x + y
    out = []
    for k, (src_ref, land_ref) in enumerate(zip(src_refs, land_refs)):
        if scatter:
            h = src_ref.shape[1] // 2
            for q in range(N_DEV - 1):
                fx, fy, fc = ((q + 1) >> 2) & 1, ((q + 1) >> 1) & 1, (q + 1) & 1
                px, py, pc = (1 - x if fx else x), (1 - y if fy else y), (1 - c if fc else c)
                src = src_ref.at[2 * px + py, pl.ds(pc * h, h)]
                sems = dict(send_sem=send_sems.at[7 * k + q], recv_sem=recv_sems.at[7 * k + q], device_id=(px, py, pc), device_id_type=MESH)
                out.append((pltpu.make_async_remote_copy(src_ref=src, dst_ref=land_ref.at[4 * x + 2 * y + c], **sems),
                            pltpu.make_async_remote_copy(src_ref=src, dst_ref=land_ref.at[4 * px + 2 * py + pc], **sems)))
            continue
        slot = (lambda chip: land_ref.at[c, chip]) if len(land_ref.shape) == 4 else (lambda chip: land_ref.at[chip])
        for j, (cx, cy) in enumerate(_other_chips(x, y)):
            sems = dict(send_sem=send_sems.at[3 * k + j], recv_sem=recv_sems.at[3 * k + j], device_id=(cx, cy, c), device_id_type=MESH)
            out.append((pltpu.make_async_remote_copy(src_ref=src_ref, dst_ref=slot(me), **sems),
                        pltpu.make_async_remote_copy(src_ref=src_ref, dst_ref=slot(2 * cx + cy), **sems)))
    return out


def _chips_start(srcs, scatter, after, name, per_core=False):
    n = len(srcs)
    n_sems = (N_DEV - 1 if scatter else N_CHIPS - 1) * n
    if scatter:
        lands = [lax.empty((N_DEV, s.shape[1] // 2, s.shape[2]), s.dtype) for s in srcs]
    else:
        lands = [lax.empty((*((2, N_CHIPS) if per_core else (N_CHIPS,)), *s.shape), s.dtype) for s in srcs]

    def body(*refs):
        src_refs, land_refs = refs[:n], refs[n:2 * n]
        send_sems, recv_sems, token = refs[2 * n + 1], refs[2 * n + 2], refs[-1]
        for sent, _ in _chip_copies(src_refs, land_refs, send_sems, recv_sems, scatter):
            sent.start()
        token[...] = jnp.zeros_like(token)

    hbm = lambda a: pltpu.HBM(a.shape, a.dtype)
    res = pl.pallas_call(
        body, name=name,
        in_specs=[_HBM] * (2 * n) + [_ANY],
        out_specs=[_SEM, _SEM] + [_HBM] * (2 * n) + [pl.BlockSpec(memory_space=pltpu.VMEM)],
        out_shape=[pltpu.SemaphoreType.DMA((n_sems,)), pltpu.SemaphoreType.DMA((n_sems,))] + [hbm(a) for a in srcs] + [hbm(a) for a in lands]
        + [jax.ShapeDtypeStruct((8, 128), F32)],
        input_output_aliases={i: 2 + i for i in range(2 * n)},
        compiler_params=pltpu.CompilerParams(has_side_effects=_DATAFLOW),
    )(*[pltpu.with_memory_space_constraint(a, pltpu.HBM) for a in (*srcs, *lands)], after)
    return (res[0], res[1], res[2:2 + n], res[2 + n:2 + 2 * n]), res[-1]


def _chips_wait(handle, scatter, after, name):
    send_sems, recv_sems, srcs, lands = handle
    n = len(srcs)

    def body(*refs):
        src_refs, land_refs = refs[:n], refs[n:2 * n]
        send_sems, recv_sems = refs[2 * n], refs[2 * n + 1]
        for sent, landed in _chip_copies(src_refs, land_refs, send_sems, recv_sems, scatter):
            sent.wait_send()
            landed.wait_recv()

    hbm = lambda a: pltpu.HBM(a.shape, a.dtype)
    res = pl.pallas_call(
        body, name=name,
        in_specs=[_HBM] * (2 * n) + [_SEM, _SEM, _ANY], out_specs=[_HBM] * (2 * n),
        out_shape=[hbm(a) for a in srcs] + [hbm(a) for a in lands],
        input_output_aliases={i: i for i in range(2 * n)},
        compiler_params=pltpu.CompilerParams(has_side_effects=_DATAFLOW),
    )(*srcs, *lands, send_sems, recv_sems, after)
    return res[n:]


N_DEV = 8


def _everyone_copies(src_ref, land_ref, send_sems, recv_sems):
    x, y, c = _mesh_pos()
    out = []
    for q in range(N_DEV - 1):
        fx, fy, fc = ((q + 1) >> 2) & 1, ((q + 1) >> 1) & 1, (q + 1) & 1
        px, py, pc = (1 - x if fx else x), (1 - y if fy else y), (1 - c if fc else c)
        sems = dict(send_sem=send_sems.at[q], recv_sem=recv_sems.at[q], device_id=(px, py, pc), device_id_type=MESH)
        out.append((pltpu.make_async_remote_copy(src_ref=src_ref, dst_ref=land_ref.at[4 * x + 2 * y + c], **sems),
                    pltpu.make_async_remote_copy(src_ref=src_ref, dst_ref=land_ref.at[4 * px + 2 * py + pc], **sems)))
    return out


def _everyone_start(block, after, name):
    land = lax.empty((N_DEV, *block.shape), block.dtype)

    def body(src_ref, land_ref, after_ref, send_sems, recv_sems, src_thru, land_thru, token):
        for sent, _ in _everyone_copies(src_ref, land_ref, send_sems, recv_sems):
            sent.start()
        token[...] = jnp.zeros_like(token)

    hbm = lambda a: pltpu.HBM(a.shape, a.dtype)
    n_sem = N_DEV - 1
    res = pl.pallas_call(
        body, name=name, in_specs=[_HBM, _HBM, _ANY],
        out_specs=[_SEM, _SEM, _HBM, _HBM, pl.BlockSpec(memory_space=pltpu.VMEM)],
        out_shape=[pltpu.SemaphoreType.DMA((n_sem,)), pltpu.SemaphoreType.DMA((n_sem,)), hbm(block), hbm(land),
                   jax.ShapeDtypeStruct((8, 128), F32)],
        input_output_aliases={0: 2, 1: 3}, compiler_params=pltpu.CompilerParams(has_side_effects=_DATAFLOW),
    )(pltpu.with_memory_space_constraint(block, pltpu.HBM), pltpu.with_memory_space_constraint(land, pltpu.HBM), after)
    return res[:4]


def _everyone_wait(handle, after, name):
    send_sems, recv_sems, block, land = handle

    def body(src_ref, land_ref, send_sems, recv_sems, after_ref, src_thru, land_thru):
        for sent, landed in _everyone_copies(src_ref, land_ref, send_sems, recv_sems):
            sent.wait_send()
            landed.wait_recv()

    hbm = lambda a: pltpu.HBM(a.shape, a.dtype)
    return pl.pallas_call(
        body, name=name, in_specs=[_HBM, _HBM, _SEM, _SEM, _ANY], out_specs=[_HBM, _HBM], out_shape=[hbm(block), hbm(land)],
        input_output_aliases={0: 0, 1: 1}, compiler_params=pltpu.CompilerParams(has_side_effects=_DATAFLOW),
    )(block, land, send_sems, recv_sems, after)


def _pair_copies(arr_refs, send_sems, recv_sems):
    x, y, c = _mesh_pos()
    out = []
    for k, ref in enumerate(arr_refs):
        sems = dict(send_sem=send_sems.at[k], recv_sem=recv_sems.at[k], device_id=(x, y, 1 - c), device_id_type=MESH)
        out.append((pltpu.make_async_remote_copy(src_ref=ref.at[c], dst_ref=ref.at[c], **sems),
                    pltpu.make_async_remote_copy(src_ref=ref.at[1 - c], dst_ref=ref.at[1 - c], **sems)))
    return out


def _pairs_start(arrs, after, name):
    n = len(arrs)

    def body(*refs):
        send_sems, recv_sems = refs[n + 1], refs[n + 2]
        for sent, _ in _pair_copies(refs[:n], send_sems, recv_sems):
            sent.start()
        refs[-1][...] = jnp.zeros_like(refs[-1])

    hbm = lambda a: pltpu.HBM(a.shape, a.dtype)
    res = pl.pallas_call(
        body, name=name, in_specs=[_HBM] * n + [_ANY],
        out_specs=[_SEM, _SEM] + [_HBM] * n + [pl.BlockSpec(memory_space=pltpu.VMEM)],
        out_shape=[pltpu.SemaphoreType.DMA((n,)), pltpu.SemaphoreType.DMA((n,))] + [hbm(a) for a in arrs] + [jax.ShapeDtypeStruct((8, 128), F32)],
        input_output_aliases={i: 2 + i for i in range(n)}, compiler_params=pltpu.CompilerParams(has_side_effects=_DATAFLOW),
    )(*[pltpu.with_memory_space_constraint(a, pltpu.HBM) for a in arrs], after)
    return res[0], res[1], res[2:2 + n]


def _pairs_wait(handle, after, name):
    send_sems, recv_sems, arrs = handle
    n = len(arrs)

    def body(*refs):
        for sent, landed in _pair_copies(refs[:n], refs[n], refs[n + 1]):
            sent.wait_send()
            landed.wait_recv()

    hbm = lambda a: pltpu.HBM(a.shape, a.dtype)
    return pl.pallas_call(
        body, name=name, in_specs=[_HBM] * n + [_SEM, _SEM, _ANY], out_specs=[_HBM] * n, out_shape=[hbm(a) for a in arrs],
        input_output_aliases={i: i for i in range(n)}, compiler_params=pltpu.CompilerParams(has_side_effects=_DATAFLOW),
    )(*arrs, send_sems, recv_sems, after)


def _allgather_sum_small(block, name):
    m_per, n = block.shape

    def body(x_ref, out_ref, sum_ref, send_sems, recv_sems, local_sem):
        x, y, c = _mesh_pos()
        me, sibling = (x, y, c), (x, y, 1 - c)
        chips = _other_chips(x, y)

        def rows(px, py, pc):
            return out_ref.at[pl.ds((4 * px + 2 * py + pc) * m_per, m_per), :]

        def copy(k, blk, to, src=None):
            return pltpu.make_async_remote_copy(src_ref=rows(*blk) if src is None else src, dst_ref=rows(*blk),
                                                send_sem=send_sems.at[k], recv_sem=recv_sems.at[k], device_id=to, device_id_type=MESH)

        mine = pltpu.make_async_copy(x_ref, rows(*me), local_sem)
        mine.start()
        first = [copy(0, me, sibling, src=x_ref)]
        first += [copy(1 + j, me, (*chip, c), src=x_ref) for j, chip in enumerate(chips)]
        for cp in first:
            cp.start()
        passed = [copy(4 + j, (*chip, c), sibling) for j, chip in enumerate(chips)]
        for j, chip in enumerate(chips):
            copy(1 + j, (*chip, c), me).wait_recv()
            passed[j].start()
        copy(0, sibling, me).wait_recv()
        for j, chip in enumerate(chips):
            copy(4 + j, (*chip, 1 - c), me).wait_recv()
        for cp in first + passed:
            cp.wait_send()
        mine.wait()
        total = out_ref[pl.ds(0, m_per), :]
        for d in range(1, N_DEV):
            total = total + out_ref[pl.ds(d * m_per, m_per), :]
        sum_ref[...] = total

    vmem = pl.BlockSpec(memory_space=pltpu.VMEM)
    return pl.pallas_call(
        body, name=name, in_specs=[vmem], out_specs=[vmem, vmem],
        out_shape=[jax.ShapeDtypeStruct((N_DEV * m_per, n), F32), jax.ShapeDtypeStruct((m_per, n), F32)],
        scratch_shapes=[pltpu.SemaphoreType.DMA((7,)), pltpu.SemaphoreType.DMA((7,)), pltpu.SemaphoreType.DMA],
        compiler_params=pltpu.CompilerParams(vmem_limit_bytes=V7X_VMEM_LIMIT),
    )(block)


WEIGHTS = ("rel_bias", "norm1_g", "w_in", "sgu_w", "sgu_b", "dil_qn_g", "dil_kn_g", "conv_w", "conv_b", "conv_ln_g", "conv_ln_b",
           "gqa_qn_g", "gqa_kn_g", "mix_norm_g", "w_out", "norm2_g", "w_gate", "w_up", "w_down")
SHARDED = ("w_in", "w_out", "w_gate", "w_up", "w_down")
COLUMN_SHARDED = ("w_in", "w_gate", "w_up")
REPLICATED = tuple(k for k in WEIGHTS if k not in SHARDED and k != "conv_w")


PACK_ROWS = 256


def _pack(parts):
    flat = jnp.concatenate([p.reshape(-1) for p in parts])
    pad = (-flat.shape[0]) % (PACK_ROWS * LANES)
    return jnp.pad(flat, (0, pad)).reshape(-1, LANES)


def _unpack(buf, shapes):
    flat = buf.reshape(-1)
    out, at = [], 0
    for s in shapes:
        size = math.prod(s)
        out.append(flat[at:at + size].reshape(s))
        at += size
    return out


RS_TM = (256, 128, 64, 32, 16)


def _add_own_seven(own, land, place, name):
    _, h, cols = land.shape
    tm = _pick(h, RS_TM)
    nb = h // tm

    def body(place_ref, own_ref, *refs):
        total = own_ref[...].astype(F32)
        for l_ref in refs[:-1]:
            total = total + l_ref[...].astype(F32)
        refs[-1][...] = total

    slot = lambda r: pl.BlockSpec((None, tm, cols), functools.partial(lambda i, p, r: (jnp.bitwise_xor(p[2], r), i, 0), r=r))
    grid_spec = pltpu.PrefetchScalarGridSpec(
        num_scalar_prefetch=1, grid=(nb,),
        in_specs=[pl.BlockSpec((None, tm, cols), lambda i, p: (p[0], p[1] * nb + i, 0))] + [slot(r) for r in range(1, N_DEV)],
        out_specs=pl.BlockSpec((tm, cols), lambda i, p: (i, 0)))
    return pl.pallas_call(body, name=name, grid_spec=grid_spec, out_shape=jax.ShapeDtypeStruct((h, cols), F32),
                          compiler_params=_cparams(("parallel",)))(place, own, *[land] * (N_DEV - 1))


def _reduce_start(grads, after, tag):
    handle, token = _chips_start(grads, True, after, tag + "start")
    return (handle, grads), token


def _reduce_finish(started, place, after, tag):
    handle, grads = started
    lands = _chips_wait(handle, True, after, tag + "wait")
    totals = [_add_own_seven(g, land, place, f"{tag}sum_{k}") for k, (g, land) in enumerate(zip(grads, lands))]
    return list(zip(totals, _swap_sibling(totals, tag + "share")))


def _adamw_shard(w, m, v, halves, c1, name):
    depth, rows, cols = w.shape
    h = rows // 2
    tm = _pick(h, RS_TM)
    nb = h // tm
    sources = [a for pair in halves for a in pair]

    def body(c_ref, w_ref, m_ref, v_ref, *refs):
        g_refs, (g_out, d_out, m_out, v_out) = refs[:2 * depth], refs[2 * depth:]
        layer, mine = pl.program_id(0), pl.program_id(1) == c_ref[0]
        g = None
        for l in range(depth):
            g_l = jnp.where(mine, g_refs[2 * l][...], g_refs[2 * l + 1][...])
            g = g_l if g is None else jnp.where(layer == l, g_l, g)
        delta, m_new, v_new = _adamw_fn(w_ref[...], g, m_ref[...], v_ref[...])
        g_out[...], d_out[...], m_out[...], v_out[...] = g, delta, m_new, v_new

    def source_spec(l, own):
        def index(layer, half, i, c_ref):
            return (jnp.where((layer == l) & ((half == c_ref[0]) == own), i, 0), 0)
        return pl.BlockSpec((tm, cols), index)

    blk = pl.BlockSpec((None, tm, cols), lambda layer, half, i, c_ref: (layer, half * nb + i, 0))
    grid_spec = pltpu.PrefetchScalarGridSpec(
        num_scalar_prefetch=1, grid=(depth, 2, nb),
        in_specs=[blk, blk, blk] + [source_spec(l, own) for l in range(depth) for own in (True, False)], out_specs=[blk] * 4)
    return pl.pallas_call(body, name=name, grid_spec=grid_spec, out_shape=[jax.ShapeDtypeStruct(w.shape, F32)] * 4,
                          compiler_params=_cparams(("arbitrary", "arbitrary", "arbitrary")))(c1, w, m, v, *sources)


GATHER_GROUPS = ((("w_in",), 0), (("w_out",), None), (("w_gate", "w_up"), None), (("w_down",), None), (("w_in",), 1))
REDUCE_GROUPS = (("w_down", "w_gate", "w_up"), ("w_out",), ("w_in",))


def kernel(x, rel_bias, norm1_g, w_in, sgu_w, sgu_b, dil_qn_g, dil_kn_g, conv_w, conv_b, conv_ln_g, conv_ln_b, gqa_qn_g, gqa_kn_g, mix_norm_g, w_out, norm2_g, w_gate, w_up, w_down, loss_target, m_rel_bias, m_norm1_g, m_w_in, m_sgu_w, m_sgu_b, m_dil_qn_g, m_dil_kn_g, m_conv_w, m_conv_b, m_conv_ln_g, m_conv_ln_b, m_gqa_qn_g, m_gqa_kn_g, m_mix_norm_g, m_w_out, m_norm2_g, m_w_gate, m_w_up, m_w_down, v_rel_bias, v_norm1_g, v_w_in, v_sgu_w, v_sgu_b, v_dil_qn_g, v_dil_kn_g, v_conv_w, v_conv_b, v_conv_ln_g, v_conv_ln_b, v_gqa_qn_g, v_gqa_kn_g, v_mix_norm_g, v_w_out, v_norm2_g, v_w_gate, v_w_up, v_w_down):
    w = dict(rel_bias=rel_bias, norm1_g=norm1_g, w_in=w_in, sgu_w=sgu_w, sgu_b=sgu_b, dil_qn_g=dil_qn_g, dil_kn_g=dil_kn_g,
             conv_w=conv_w, conv_b=conv_b, conv_ln_g=conv_ln_g, conv_ln_b=conv_ln_b, gqa_qn_g=gqa_qn_g, gqa_kn_g=gqa_kn_g,
             mix_norm_g=mix_norm_g, w_out=w_out, norm2_g=norm2_g, w_gate=w_gate, w_up=w_up, w_down=w_down)
    m = dict(rel_bias=m_rel_bias, norm1_g=m_norm1_g, w_in=m_w_in, sgu_w=m_sgu_w, sgu_b=m_sgu_b, dil_qn_g=m_dil_qn_g,
             dil_kn_g=m_dil_kn_g, conv_w=m_conv_w, conv_b=m_conv_b, conv_ln_g=m_conv_ln_g, conv_ln_b=m_conv_ln_b,
             gqa_qn_g=m_gqa_qn_g, gqa_kn_g=m_gqa_kn_g, mix_norm_g=m_mix_norm_g, w_out=m_w_out, norm2_g=m_norm2_g,
             w_gate=m_w_gate, w_up=m_w_up, w_down=m_w_down)
    v = dict(rel_bias=v_rel_bias, norm1_g=v_norm1_g, w_in=v_w_in, sgu_w=v_sgu_w, sgu_b=v_sgu_b, dil_qn_g=v_dil_qn_g,
             dil_kn_g=v_dil_kn_g, conv_w=v_conv_w, conv_b=v_conv_b, conv_ln_g=v_conv_ln_g, conv_ln_b=v_conv_ln_b,
             gqa_qn_g=v_gqa_qn_g, gqa_kn_g=v_gqa_kn_g, mix_norm_g=v_mix_norm_g, w_out=v_w_out, norm2_g=v_norm2_g,
             w_gate=v_w_gate, w_up=v_w_up, w_down=v_w_down)
    bsz = x.shape[0]
    t = bsz * SEQ
    xi, yi, ci = _mesh_pos()
    chip = 2 * xi + yi
    conv_cols = conv_w.shape[-1]

    conv_rows = DEPTH * CONV_WIDTH
    conv_block = jnp.pad(conv_w.reshape(conv_rows, conv_cols), ((0, (-conv_rows) % 8), (0, 0)))
    every, _ = _allgather_sum_small(conv_block, "conv_w_gather")
    every = every.reshape(N_DEV, conv_block.shape[0], conv_cols)
    conv_w_full = jnp.concatenate([every[2 * j, :conv_rows].reshape(DEPTH, CONV_WIDTH, conv_cols) for j in range(N_CHIPS)], axis=-1)

    c1 = jnp.reshape(ci, (1,)).astype(jnp.int32)
    place = jnp.stack([chip, ci, 4 * xi + 2 * yi + ci]).astype(jnp.int32)

    def own_part(k, layer):
        if layer is None:
            return lax.dynamic_index_in_dim(w[k], ci, axis=0, keepdims=False).astype(BF16)
        half = w[k].shape[1] // 2
        return lax.dynamic_slice_in_dim(w[k][layer], ci * half, half, axis=0).astype(BF16)

    fetches, token = [], every
    for gi, (group, layer) in enumerate(GATHER_GROUPS):
        parts = [own_part(k, layer) for k in group]
        handle, token = _chips_start(parts, False, token, f"gather{gi}_start", per_core=True)
        fetches.append((handle, parts))
    all_started = token
    gathered, handed = {}, {}

    def group_of(l, name):
        return [name in group and layer in (None, l) for group, layer in GATHER_GROUPS].index(True)

    def landed(gi, after):
        handle, parts = fetches[gi]
        lands = _chips_wait(handle, False, all_started if after is None else after, f"gather{gi}_wait")
        return [lax.dynamic_update_slice(land, own[None, None], (ci, chip, 0, 0)) for land, own in zip(lands, parts)]

    def prefetch(l, name, after):
        if l < DEPTH and group_of(l, name) not in handed:
            gi = group_of(l, name)
            handed[gi] = _pairs_start(landed(gi, after), after, f"gather{gi}_share_start")

    def big(l, name, after):
        if (l, name) not in gathered:
            gi = group_of(l, name)
            group, layer = GATHER_GROUPS[gi]
            if gi in handed:
                whole = _pairs_wait(handed[gi], after, f"gather{gi}_share_wait")
            else:
                whole = _complete_pairs(landed(gi, after), f"gather{gi}_share")
            for k, g in zip(group, whole):
                rows, cols = g.shape[2:]
                if layer is not None:
                    gathered[layer, k] = g.transpose(0, 2, 1, 3).reshape(2 * rows, N_CHIPS * cols)
                    continue
                for each in range(DEPTH):
                    gathered[each, k] = (g[each].transpose(1, 0, 2).reshape(rows, N_CHIPS * cols) if k in COLUMN_SHARDED
                                         else g[each].reshape(N_CHIPS * rows, cols))
        return gathered[l, name]

    big(0, "w_in", None)

    pending, started, reduced = {}, {}, {}

    def emit(l, name, g):
        pending[l, name] = g
        for gi, group in enumerate(REDUCE_GROUPS):
            if name in group and all((l, k) in pending for k in group):
                started[l, gi], token = _reduce_start([pending[l, k] for k in group], g, f"l{l}_reduce{gi}_")
                return token
        return None

    def finish(l, after):
        for gi, group in enumerate(REDUCE_GROUPS):
            for k, r in zip(group, _reduce_finish(started[l, gi], place, after, f"l{l}_reduce{gi}_")):
                reduced[l, k] = r
            after = reduced[l, group[0]][1]

    def mid_hook(l, a):
        if l + 1 < DEPTH:
            finish(l + 1, a)

    small = {k: w[k] for k in REPLICATED}
    small["conv_w"] = conv_w_full
    loss, dx, small_grads = _local_step(x.reshape(t, D_MODEL), loss_target.reshape(t, D_MODEL), small, big, emit, mid_hook, bsz,
                                        prefetch)
    loss = lax.psum(loss[0, 0], ("x", "y", "c"))

    names = REPLICATED + ("conv_w",)
    shapes = [small_grads[k].shape for k in names]
    packed_grads = _pack([small_grads[k] for k in names])
    eighths = packed_grads.reshape(N_CHIPS, packed_grads.shape[0] // N_CHIPS, LANES)
    small_reduce, _ = _reduce_start([eighths], dx, "small_grads_")
    finish(0, dx)

    grads, deltas, new_m, new_v = {}, {}, {}, {}
    for k in SHARDED:
        grads[k], deltas[k], new_m[k], new_v[k] = _adamw_shard(w[k], m[k], v[k], [reduced[l, k] for l in range(DEPTH)], c1, "adamw_" + k)

    landed = _chips_wait(small_reduce[0], True, new_v[SHARDED[-1]], "small_grads_wait")[0]
    mine = _add_own_seven(eighths, landed, place, "small_grads_sum")
    own, others = _everyone_wait(_everyone_start(mine, mine, "small_grads_spread"), mine, "small_grads_spread_wait")
    summed = lax.dynamic_update_slice_in_dim(others, own[None], 4 * xi + 2 * yi + ci, axis=0).reshape(packed_grads.shape)
    summed_parts = dict(zip(names, _unpack(summed, shapes)))
    rep_shapes = [w[k].shape for k in REPLICATED]
    packed = [_pack([src[k] for k in REPLICATED]) for src in (w, {k: summed_parts[k] for k in REPLICATED}, m, v)]
    d_p, m_p, v_p = _adamw(*packed, "adamw_replicated")
    for k, gk, dk, mk, vk in zip(REPLICATED, _unpack(packed[1], rep_shapes), _unpack(d_p, rep_shapes), _unpack(m_p, rep_shapes),
                                 _unpack(v_p, rep_shapes)):
        grads[k], deltas[k], new_m[k], new_v[k] = gk, dk, mk, vk
    g_conv = lax.dynamic_slice_in_dim(summed_parts["conv_w"], chip * conv_cols, conv_cols, axis=2)
    packed = [_pack([a]) for a in (conv_w, g_conv, m["conv_w"], v["conv_w"])]
    d_p, m_p, v_p = _adamw(*packed, "adamw_conv_w")
    grads["conv_w"] = g_conv
    deltas["conv_w"], new_m["conv_w"], new_v["conv_w"] = (_unpack(a, [conv_w.shape])[0] for a in (d_p, m_p, v_p))

    return (loss, dx.reshape(x.shape), *[grads[k] for k in WEIGHTS], *[deltas[k] for k in WEIGHTS],
            *[new_m[k] for k in WEIGHTS], *[new_v[k] for k in WEIGHTS])
```

```python
import functools
import math

import numpy as np
import jax
import jax.numpy as jnp
from jax import lax
from jax.experimental import pallas as pl
from jax.experimental.pallas import tpu as pltpu

F32 = jnp.float32
BF16 = jnp.bfloat16

D_MODEL = 2048
SEQ = 2048
DEPTH = 2
HEAD_DIM = 64
GROUP_WIDTH = 512
N_HEADS = 8
SGU_CHUNK = 128
DIL_PATTERNS = ((128, 1), (512, 4), (2048, 16))
DIL_HALF = 64
CONV_WIDTH = 31
KV_WIDTH = 128
GRID_W = 64
ROPE_THETA = 10000.0
REL_BUCKETS = 32
REL_MAX_DIST = 1024
FFN_HIDDEN = 5632
IN_WIDTH = 4352
RMS_EPS = 1e-6
LN_EPS = 1e-5
ADAM_LR = 0.001
ADAM_B1 = 0.9
ADAM_B2 = 0.999
ADAM_EPS = 1e-08
ADAM_WD = 0.01
ADAM_STEP = 10
N_CHIPS = 4

V7X_VMEM_LIMIT = 56 * 1024 * 1024
MATMUL_VMEM_BUDGET = 48 * 1024 * 1024
LANES = 128
HI = lax.Precision.HIGHEST
SPLIT3 = lax.Precision.HIGH
MESH = pl.DeviceIdType.MESH


def _cparams(sem=None):
    return pltpu.CompilerParams(dimension_semantics=sem, vmem_limit_bytes=V7X_VMEM_LIMIT)


def _pick(n, cands):
    for c in cands:
        if n % c == 0:
            return c
    raise ValueError(f"no tile for {n}")


_DIMS = {"nn": (((1,), (0,)), ((), ())), "nt": (((1,), (1,)), ((), ())), "tn": (((0,), (0,)), ((), ()))}


def _matmul(pairs, mode, out_dtype, name, residual=None, slabs=1):
    a0, b0 = pairs[0]
    b3 = b0.ndim == 3
    if mode == "nn":
        (M, K), N = a0.shape, b0.shape[1]
    elif mode == "nt":
        (M, K), N = a0.shape, b0.shape[0]
    else:
        (K, M) = a0.shape
        N = b0.shape[-1] if b3 else b0.shape[1] // slabs
    npairs = len(pairs)
    a_bytes, b_bytes, o_bytes = a0.dtype.itemsize, b0.dtype.itemsize, jnp.dtype(out_dtype).itemsize
    per_out = 4 + 2 * o_bytes + (8 if residual is not None else 0)
    tn_cands = [c for c in ((1024, 512) if K <= 2048 else (512,)) + (1408, 2176, 256) if N % c == 0] + [N]
    tm, tn = next((tm, tn) for tn in tn_cands for tm in (1024, 1408, 512, 256)
                  if M % tm == 0 and 2 * npairs * K * (tm * a_bytes + tn * b_bytes) + tm * tn * per_out <= MATMUL_VMEM_BUDGET)
    tk = K
    ni, nj = M // tm, N // tn
    j_outer = nj * M * a_bytes + N * b_bytes < M * a_bytes + ni * N * b_bytes
    grid = (slabs, nj, ni) if j_outer else (slabs, ni, nj)
    at = lambda f: (lambda s, g1, g2: f(s, g2, g1)) if j_outer else f

    if mode in ("nn", "nt"):
        a_spec = pl.BlockSpec((tm, tk), at(lambda s, i, j: (i, 0)))
    else:
        a_spec = pl.BlockSpec((tk, tm), at(lambda s, i, j: (0, i)))
    if mode == "nt":
        b_spec = pl.BlockSpec((tn, tk), at(lambda s, i, j: (j, 0)))
    elif b3:
        b_spec = pl.BlockSpec((None, tk, tn), at(lambda s, i, j: (s, 0, j)))
    else:
        b_spec = pl.BlockSpec((tk, tn), at(lambda s, i, j: (0, s * nj + j)))
    if slabs > 1:
        o_spec = pl.BlockSpec((None, tm, tn), at(lambda s, i, j: (s, i, j)))
        o_shape = jax.ShapeDtypeStruct((slabs, M, N), out_dtype)
    else:
        o_spec = pl.BlockSpec((tm, tn), at(lambda s, i, j: (i, j)))
        o_shape = jax.ShapeDtypeStruct((M, N), out_dtype)
    in_specs = [a_spec] * npairs + [b_spec] * npairs
    args = [a for a, _ in pairs] + [b for _, b in pairs]
    if residual is not None:
        in_specs.append(pl.BlockSpec((tm, tn), at(lambda s, i, j: (i, j))))
        args.append(residual)
    dims = _DIMS[mode]

    def body(*refs):
        a_refs, b_refs = refs[:npairs], refs[npairs:2 * npairs]
        res_ref = refs[2 * npairs] if residual is not None else None
        o_ref = refs[-1]
        r = None
        for a_ref, b_ref in zip(a_refs, b_refs):
            d = lax.dot_general(a_ref[...].astype(BF16), b_ref[...].astype(BF16), dims, preferred_element_type=F32)
            r = d if r is None else r + d
        if res_ref is not None:
            r = r + res_ref[...]
        o_ref[...] = r.astype(out_dtype)

    return pl.pallas_call(
        body, name=name, grid=grid, in_specs=in_specs, out_specs=o_spec, out_shape=o_shape,
        compiler_params=_cparams(("parallel", "parallel", "parallel")),
    )(*args)


class Strided:
    def __init__(self, r):
        self.r = r


def _rowmap(fn, rows, fulls, row_outs, acc_outs, name, tm, n_rows):
    nr, nf, nro = len(rows), len(fulls), len(row_outs)
    rows = [r if len(r) == 4 else (*r, n_rows // tm) for r in rows]
    row_outs = [o if len(o) == 3 else (*o, None) for o in row_outs]
    in_specs = [pl.BlockSpec((tm // per.r, per.r * w), lambda i: (i, 0)) if isinstance(per, Strided) else
                pl.BlockSpec((tm, w), functools.partial(lambda i, cb, per: (i % per, cb), cb=cb, per=per)) for _, w, cb, per in rows]
    in_specs += [pl.BlockSpec(f.shape, lambda i: (0,) * f.ndim) for f in fulls]
    out_specs = [pl.BlockSpec((tm, w) if st is None else (tm // st.r, st.r * w), lambda i: (i, 0)) for w, _, st in row_outs]
    out_specs += [pl.BlockSpec(s, functools.partial(lambda i, n: (0,) * n, n=len(s))) for s in acc_outs]
    out_shape = [jax.ShapeDtypeStruct((n_rows, w) if st is None else (n_rows // st.r, st.r * w), dt) for w, dt, st in row_outs]
    out_shape += [jax.ShapeDtypeStruct(s, F32) for s in acc_outs]
    strided = [(k, w, per.r) for k, (_, w, _, per) in enumerate(rows) if isinstance(per, Strided)]
    strided += [(nr + nf + k, w, st.r) for k, (w, _, st) in enumerate(row_outs) if st is not None]
    n_scratch = len(strided)

    def body(*refs):
        refs, scratch = refs[:len(refs) - n_scratch], dict(zip([k for k, _, _ in strided], refs[len(refs) - n_scratch:]))
        ins = []
        for k, ref in enumerate(refs[:nr + nf]):
            if k in scratch:
                w, r, scr = rows[k][1], rows[k][3].r, scratch[k]
                for rho in range(r):
                    for j in range(w // LANES):
                        scr.at[j][pl.ds(rho, tm // r, stride=r), :] = ref[:, pl.ds(rho * w + j * LANES, LANES)]
                ins.append(jnp.concatenate([scr[j] for j in range(w // LANES)], axis=1))
            else:
                ins.append(ref[...])
        outs = fn(*ins)
        o_refs = refs[nr + nf:]
        for k, (o_ref, val) in enumerate(zip(o_refs[:nro], outs[:nro])):
            if nr + nf + k in scratch:
                w, r, scr = row_outs[k][0], row_outs[k][2].r, scratch[nr + nf + k]
                val = val.astype(F32)
                for j in range(w // LANES):
                    scr[j] = val[:, j * LANES:(j + 1) * LANES]
                for rho in range(r):
                    for j in range(w // LANES):
                        o_ref[:, pl.ds(rho * w + j * LANES, LANES)] = scr.at[j][pl.ds(rho, tm // r, stride=r), :].astype(o_ref.dtype)
            else:
                o_ref[...] = val.astype(o_ref.dtype)
        if acc_outs:
            first = pl.program_id(0) == 0
            for o_ref, val in zip(o_refs[nro:], outs[nro:]):
                @pl.when(first)
                def _(o_ref=o_ref, val=val):
                    o_ref[...] = val

                @pl.when(jnp.logical_not(first))
                def _(o_ref=o_ref, val=val):
                    o_ref[...] += val

    res = pl.pallas_call(
        body, name=name, grid=(n_rows // tm,), in_specs=in_specs, out_specs=out_specs, out_shape=out_shape,
        scratch_shapes=[pltpu.VMEM((w // LANES, tm, LANES), F32) for _, w, _ in strided],
        compiler_params=_cparams(("arbitrary",) if acc_outs else ("parallel",)),
    )(*[r[0] for r in rows], *fulls)
    return res


def _rms(x, g):
    return x * lax.rsqrt(jnp.mean(x * x, axis=-1, keepdims=True) + RMS_EPS) * g


def _rmsnorm_fwd(x, g, name):
    t = x.shape[0]
    return _rowmap(lambda xv, gv: (_rms(xv, gv),), [(x, D_MODEL, 0)], [g], [(D_MODEL, BF16)], [], name, 512, t)[0]


def _rmsnorm_bwd(dh, x, g, dres, name, follow=None):
    t = x.shape[0]

    def fn(dhv, xv, drv, gv, *_):
        _, vjp = jax.vjp(_rms, xv, gv)
        dx, dg = vjp(dhv)
        return dx + drv, dx + drv, dg

    fulls = [g] if follow is None else [g, follow]
    return _rowmap(fn, [(dh, D_MODEL, 0), (x, D_MODEL, 0), (dres, D_MODEL, 0)], fulls, [(D_MODEL, F32), (D_MODEL, BF16)],
                   [(1, D_MODEL)], name, 256, t)


def _loss_fwd_bwd(y, target, name):
    t = y.shape[0]

    def fn(yv, tv):
        e = yv - tv
        return e * (1.0 / D_MODEL), e * (1.0 / D_MODEL), (0.5 / D_MODEL) * jnp.sum(e * e, keepdims=True)

    return _rowmap(fn, [(y, D_MODEL, 0), (target, D_MODEL, 0)], [], [(D_MODEL, F32), (D_MODEL, BF16)], [(1, 1)], name, 512, t)


FFN_TILE = (1024, 512)


def _ffn_up(h, wg, wu, name):
    t, n = h.shape[0], wg.shape[1]
    tm, tn = FFN_TILE

    def body(h_ref, wg_ref, wu_ref, act_ref, du_ref, dg_ref):
        hv = h_ref[...]
        g = jnp.dot(hv, wg_ref[...], preferred_element_type=F32)
        u = jnp.dot(hv, wu_ref[...], preferred_element_type=F32)
        sg = jax.nn.sigmoid(g)
        silu = g * sg
        act_ref[...] = (silu * u).astype(BF16)
        du_ref[...] = silu.astype(BF16)
        dg_ref[...] = (u * (sg + silu * (1.0 - sg))).astype(BF16)

    o_spec = pl.BlockSpec((tm, tn), lambda i, j: (i, j))
    o_shape = jax.ShapeDtypeStruct((t, n), BF16)
    return pl.pallas_call(
        body, name=name, grid=(t // tm, n // tn),
        in_specs=[pl.BlockSpec((tm, D_MODEL), lambda i, j: (i, 0)), pl.BlockSpec((D_MODEL, tn), lambda i, j: (0, j)),
                  pl.BlockSpec((D_MODEL, tn), lambda i, j: (0, j))],
        out_specs=[o_spec] * 3, out_shape=[o_shape] * 3, compiler_params=_cparams(("parallel", "parallel")),
    )(h, wg, wu)


def _ffn_down_bwd(dy, wd, act_du, act_dg, name):
    t, n = dy.shape[0], wd.shape[0]
    tm, tn = FFN_TILE

    def body(dy_ref, wd_ref, adu_ref, adg_ref, dg_ref, du_ref):
        dact = lax.dot_general(dy_ref[...].astype(BF16), wd_ref[...], _DIMS["nt"], preferred_element_type=F32)
        du_ref[...] = (dact * adu_ref[...].astype(F32)).astype(BF16)
        dg_ref[...] = (dact * adg_ref[...].astype(F32)).astype(BF16)

    o_spec = pl.BlockSpec((tm, tn), lambda i, j: (i, j))
    o_shape = jax.ShapeDtypeStruct((t, n), BF16)
    return pl.pallas_call(
        body, name=name, grid=(t // tm, n // tn),
        in_specs=[pl.BlockSpec((tm, D_MODEL), lambda i, j: (i, 0)), pl.BlockSpec((tn, D_MODEL), lambda i, j: (j, 0)),
                  o_spec, o_spec],
        out_specs=[o_spec] * 2, out_shape=[o_shape] * 2, compiler_params=_cparams(("parallel", "parallel")),
    )(dy, wd, act_du, act_dg)


def _np_group_avg(width, group=HEAD_DIM):
    i = np.arange(width)
    return ((i[:, None] // group) == (i[None, :] // group)).astype(np.float32) / group


def _np_tile_fold(width, group=HEAD_DIM):
    return ((np.arange(width)[:, None] % group) == np.arange(group)[None, :]).astype(np.float32)


def _np_group_fold(width, group=HEAD_DIM, pad=128):
    return ((np.arange(width)[:, None] // group) == np.arange(pad)[None, :]).astype(np.float32)


def _np_rope_partner(width):
    i = np.arange(width)
    partner = np.where(i % 32 < 16, i + 16, i - 16)
    return (partner[:, None] == i[None, :]).astype(np.float32)


def _np_kv_expand():
    src = np.arange(KV_WIDTH)
    dst = np.arange(GROUP_WIDTH)
    return ((src[:, None] // HEAD_DIM == dst[None, :] // (4 * HEAD_DIM)) & (src[:, None] % HEAD_DIM == dst[None, :] % HEAD_DIM)).astype(np.float32)


def _np_rope_tables(n_heads):
    t = np.arange(SEQ)
    pos = {0: (t // GRID_W).astype(np.float32), 1: (t % GRID_W).astype(np.float32)}
    freqs = (ROPE_THETA ** (-np.arange(16, dtype=np.float32) / 16)).astype(np.float32)
    cos_parts, sin_parts = [], []
    for axis in (0, 1):
        ang = pos[axis][:, None] * freqs[None, :]
        c, s = np.cos(ang).astype(np.float32), np.sin(ang).astype(np.float32)
        cos_parts += [c, c]
        sin_parts += [-s, s]
    cos = np.concatenate(cos_parts, axis=1)
    sin = np.concatenate(sin_parts, axis=1)
    return np.tile(cos, (1, n_heads)), np.tile(sin, (1, n_heads))


def _np_t5_buckets(rel):
    nb = REL_BUCKETS // 2
    max_exact = nb // 2
    ret = np.where(rel > 0, nb, 0)
    n = np.abs(rel)
    nf = np.maximum(n, 1).astype(np.float32)
    large = max_exact + (np.log(nf / max_exact) / math.log(REL_MAX_DIST / max_exact) * (nb - max_exact)).astype(np.int32)
    large = np.minimum(large, nb - 1)
    return (ret + np.where(n < max_exact, n, large)).astype(np.int32)


DIL_QB = 128
DIL_WIN = DIL_QB + 2 * DIL_HALF


def _np_dil_buckets(dil):
    off = np.arange(DIL_WIN)[None, :] - DIL_HALF - np.arange(DIL_QB)[:, None]
    return _np_t5_buckets(off * dil)


def _dil_live_buckets(dil):
    off = np.arange(-DIL_HALF, DIL_HALF + 1)
    return sorted(set(_np_t5_buckets(off * dil).tolist()))


def _head_stat(x, mavg):
    return jnp.dot(x, mavg, precision=SPLIT3, preferred_element_type=F32)


def _gelu(x):
    return 0.5 * x * (1.0 + jnp.tanh(math.sqrt(2.0 / math.pi) * (x + 0.044715 * (x * x * x))))


def _sgu_pre(u_pre, v_pre, mavg):
    v = _gelu(v_pre)
    xc = v - _head_stat(v, mavg)
    vn = xc * lax.rsqrt(_head_stat(xc * xc, mavg) + LN_EPS)
    return _gelu(u_pre), vn


def _sgu_mix(w_ref, vnb, bm):
    lane_group = lax.broadcasted_iota(jnp.int32, (1, GROUP_WIDTH), 1) // HEAD_DIM
    mixed = bm
    for g in range(N_HEADS):
        r = jnp.dot(w_ref[g], vnb, preferred_element_type=F32)
        mixed = mixed + jnp.where(lane_group == g, r, 0.0)
    return mixed


SGU_TM = 512


def _sgu_fwd(z, w_s, bm, name):
    t = z.shape[0]
    mavg = jnp.asarray(_np_group_avg(GROUP_WIDTH))

    def body(u_ref, v_ref, w_ref, bm_ref, mavg_ref, y_ref):
        for c in range(SGU_TM // SGU_CHUNK):
            rows = pl.ds(c * SGU_CHUNK, SGU_CHUNK)
            u, vn = _sgu_pre(u_ref[rows, :], v_ref[rows, :], mavg_ref[...])
            y_ref[rows, :] = u * _sgu_mix(w_ref, vn.astype(BF16), bm_ref[...])

    full = lambda a: pl.BlockSpec(a.shape, lambda i: (0,) * a.ndim)
    return pl.pallas_call(
        body, name=name, grid=(t // SGU_TM,),
        in_specs=[pl.BlockSpec((SGU_TM, GROUP_WIDTH), lambda i: (i, 0)), pl.BlockSpec((SGU_TM, GROUP_WIDTH), lambda i: (i, 1)),
                  full(w_s), full(bm), full(mavg)],
        out_specs=pl.BlockSpec((SGU_TM, GROUP_WIDTH), lambda i: (i, 0)),
        out_shape=jax.ShapeDtypeStruct((t, GROUP_WIDTH), F32), compiler_params=_cparams(("parallel",)),
    )(z, z, w_s, bm, mavg)


def _sgu_bwd(z, dy, w_s, w_s_t, bm, name):
    t = z.shape[0]
    mavg = jnp.asarray(_np_group_avg(GROUP_WIDTH))
    gfold = jnp.asarray(_np_group_fold(GROUP_WIDTH))

    def body(u_ref, v_ref, dy_ref, w_ref, wt_ref, bm_ref, mavg_ref, gfold_ref, du_ref, dv_ref, dw_ref, dbs_ref, dbm_ref):
        @pl.when(pl.program_id(0) == 0)
        def _():
            dw_ref[...] = jnp.zeros_like(dw_ref)
            dbm_ref[...] = jnp.zeros_like(dbm_ref)

        lane_group = lax.broadcasted_iota(jnp.int32, (1, GROUP_WIDTH), 1) // HEAD_DIM
        for c in range(SGU_TM // SGU_CHUNK):
            rows = pl.ds(c * SGU_CHUNK, SGU_CHUNK)
            (u, vn), pre_vjp = jax.vjp(functools.partial(_sgu_pre, mavg=mavg_ref[...]), u_ref[rows, :], v_ref[rows, :])
            vnb = vn.astype(BF16)
            mixed = _sgu_mix(w_ref, vnb, bm_ref[...])
            dyv = dy_ref[rows, :]
            dmixed = dyv * u
            dbm_ref[...] += dmixed
            dvn = jnp.zeros_like(vn)
            for g in range(N_HEADS):
                dm_g = jnp.where(lane_group == g, dmixed, 0.0).astype(BF16)
                dw_ref[g] += lax.dot_general(dm_g, vnb, _DIMS["nt"], preferred_element_type=F32)
                dvn = dvn + jnp.dot(wt_ref[g], dm_g, preferred_element_type=F32)
            du_pre, dv_pre = pre_vjp((dyv * mixed, dvn))
            du_ref[rows, :] = du_pre
            dv_ref[rows, :] = dv_pre

        @pl.when(pl.program_id(0) == t // SGU_TM - 1)
        def _():
            dbs_ref[...] = jnp.dot(dbm_ref[...], gfold_ref[...], precision=HI, preferred_element_type=F32)

    full = lambda a: pl.BlockSpec(a.shape, lambda i: (0,) * a.ndim)
    row = pl.BlockSpec((SGU_TM, GROUP_WIDTH), lambda i: (i, 0))
    return pl.pallas_call(
        body, name=name, grid=(t // SGU_TM,),
        in_specs=[row, pl.BlockSpec((SGU_TM, GROUP_WIDTH), lambda i: (i, 1)), row, full(w_s), full(w_s_t), full(bm), full(mavg),
                  full(gfold)],
        out_specs=[row, row, pl.BlockSpec((N_HEADS, SGU_CHUNK, SGU_CHUNK), lambda i: (0, 0, 0)),
                   pl.BlockSpec((SGU_CHUNK, 128), lambda i: (0, 0))],
        out_shape=[jax.ShapeDtypeStruct((t, GROUP_WIDTH), F32)] * 2 + [jax.ShapeDtypeStruct((N_HEADS, SGU_CHUNK, SGU_CHUNK), F32),
                                                                      jax.ShapeDtypeStruct((SGU_CHUNK, 128), F32)],
        scratch_shapes=[pltpu.VMEM((SGU_CHUNK, GROUP_WIDTH), F32)],
        compiler_params=_cparams(("arbitrary",)),
    )(z, z, dy, w_s, w_s_t, bm, mavg, gfold)


def _head_lanes():
    lane_head = lax.broadcasted_iota(jnp.int32, (1, 2 * HEAD_DIM), 1) // HEAD_DIM
    return lane_head == 0, lane_head == 1


def _stack_heads(x2):
    h0, h1 = _head_lanes()
    zero = jnp.zeros_like(x2)
    return jnp.concatenate([jnp.where(h0, x2, zero), jnp.where(h1, x2, zero)], axis=0)


def _unstack_heads(y):
    r = y.shape[0] // 2
    h0, _ = _head_lanes()
    return jnp.where(h0, y[:r], y[r:])


def _pair_softmax(qs, k2, biases, valid):
    s = lax.dot_general(qs, k2, _DIMS["nt"], preferred_element_type=F32)
    if biases is not None:
        s = s + jnp.concatenate(biases, axis=0)
    if valid is not None:
        s = jnp.where(jnp.concatenate([valid, valid], axis=0), s, -1e30)
    m = jnp.max(s, axis=-1, keepdims=True)
    e = jnp.exp(s - m)
    l = jnp.sum(e, axis=-1, keepdims=True)
    return e / l, m + jnp.log(l)


def _attn_pair_fwd(q2, k2, v2, biases, valid):
    p, lse = _pair_softmax(_stack_heads(q2), k2, biases, valid)
    o = jnp.dot(p.astype(BF16), v2, preferred_element_type=F32)
    return _unstack_heads(o), _unstack_heads(jnp.broadcast_to(lse, o.shape))


def _attn_pair_bwd(q2, k2, v2, biases, valid, do2, dlse2):
    r = q2.shape[0]
    qs = _stack_heads(q2)
    p, _ = _pair_softmax(qs, k2, biases, valid)
    dos = _stack_heads(do2).astype(BF16)
    dp = lax.dot_general(dos, v2, _DIMS["nt"], preferred_element_type=F32)
    delta = jnp.sum(dp * p, axis=-1, keepdims=True)
    if dlse2 is not None:
        delta = delta - jnp.sum(_stack_heads(dlse2), axis=-1, keepdims=True)
    ds = p * (dp - delta)
    dsb = ds.astype(BF16)
    dq2 = _unstack_heads(jnp.dot(dsb, k2, preferred_element_type=F32))
    dk2 = lax.dot_general(dsb, qs, _DIMS["tn"], preferred_element_type=F32)
    dv2 = lax.dot_general(p.astype(BF16), dos, _DIMS["tn"], preferred_element_type=F32)
    return dq2, dk2, dv2, [ds[:r], ds[r:]]


def _dil_valid(r0, length):
    row = lax.broadcasted_iota(jnp.int32, (DIL_QB, DIL_WIN), 0)
    col = lax.broadcasted_iota(jnp.int32, (DIL_QB, DIL_WIN), 1)
    off = col - DIL_HALF - row
    kpos = r0 - DIL_HALF + col
    return (jnp.abs(off) <= DIL_HALF) & (kpos >= 0) & (kpos < length)


def _dil_build_bias(tab_ref, bkt_ref, bias_ref, dil):
    bkt = bkt_ref[...]
    for h in range(N_HEADS):
        acc = jnp.zeros((DIL_QB, DIL_WIN), F32)
        for b in _dil_live_buckets(dil):
            acc = jnp.where(bkt == b, tab_ref[b, h], acc)
        bias_ref[h] = acc


def _dil_fill_pad(pad_ref, src_ref, length):
    zeros = jnp.zeros((DIL_HALF, GROUP_WIDTH), pad_ref.dtype)
    pad_ref[pl.ds(0, DIL_HALF), :] = zeros
    pad_ref[pl.ds(DIL_HALF + length, DIL_HALF), :] = zeros
    pad_ref[pl.ds(DIL_HALF, length), :] = src_ref[...]


def _dil_specs(bsz, length, dil):
    view = lambda a: a.reshape(bsz, length, dil * GROUP_WIDTH)
    blk = pl.BlockSpec((None, DIL_QB, GROUP_WIDTH), lambda b, rho, i: (b, i, rho))
    seq = pl.BlockSpec((None, length, GROUP_WIDTH), lambda b, rho, i: (b, 0, rho))
    return view, blk, seq


def _dil_fwd(qb, kb, vb, table, dil, name):
    length = SEQ // dil
    bsz = qb.shape[0] // length
    bkt = jnp.asarray(_np_dil_buckets(dil))
    view, blk, seq = _dil_specs(bsz, length, dil)

    def body(tab_ref, bkt_ref, q_ref, k_ref, v_ref, o_ref, lse_ref, kpad, vpad, bias_ref):
        i = pl.program_id(2)

        @pl.when((pl.program_id(0) == 0) & (pl.program_id(1) == 0) & (i == 0))
        def _():
            _dil_build_bias(tab_ref, bkt_ref, bias_ref, dil)

        @pl.when(i == 0)
        def _():
            _dil_fill_pad(kpad, k_ref, length)
            _dil_fill_pad(vpad, v_ref, length)

        r0 = pl.multiple_of(i * DIL_QB, DIL_QB)
        valid = _dil_valid(r0, length)
        for m in range(N_HEADS // 2):
            lanes = pl.ds(m * 128, 128)
            o2, lse2 = _attn_pair_fwd(q_ref[:, lanes], kpad[pl.ds(r0, DIL_WIN), lanes], vpad[pl.ds(r0, DIL_WIN), lanes],
                                      (bias_ref[2 * m], bias_ref[2 * m + 1]), valid)
            o_ref[:, lanes] = o2
            lse_ref[:, lanes] = lse2

    out = jax.ShapeDtypeStruct((bsz, length, dil * GROUP_WIDTH), F32)
    o, lse = pl.pallas_call(
        body, name=name, grid=(bsz, dil, length // DIL_QB),
        in_specs=[pl.BlockSpec(memory_space=pltpu.SMEM), pl.BlockSpec(bkt.shape, lambda b, rho, i: (0, 0)), blk, seq, seq],
        out_specs=[blk, blk], out_shape=[out, out],
        scratch_shapes=[pltpu.VMEM((length + 2 * DIL_HALF, GROUP_WIDTH), BF16), pltpu.VMEM((length + 2 * DIL_HALF, GROUP_WIDTH), BF16),
                        pltpu.VMEM((N_HEADS, DIL_QB, DIL_WIN), F32)],
        compiler_params=_cparams(("arbitrary", "arbitrary", "arbitrary")),
    )(table, bkt, view(qb), view(kb), view(vb))
    return o.reshape(qb.shape), lse.reshape(qb.shape)


def _dil_bwd(qb, kb, vb, do, dlse, table, dil, name):
    length = SEQ // dil
    bsz = qb.shape[0] // length
    nqb = length // DIL_QB
    bkt = jnp.asarray(_np_dil_buckets(dil))
    view, blk, seq = _dil_specs(bsz, length, dil)

    def body(tab_ref, bkt_ref, q_ref, k_ref, v_ref, do_ref, dlse_ref, dq_ref, dk_ref, dv_ref, dsc_ref, kpad, vpad, bias_ref):
        i = pl.program_id(2)

        @pl.when((pl.program_id(0) == 0) & (pl.program_id(1) == 0) & (i == 0))
        def _():
            _dil_build_bias(tab_ref, bkt_ref, bias_ref, dil)
            dsc_ref[...] = jnp.zeros_like(dsc_ref)

        @pl.when(i == 0)
        def _():
            _dil_fill_pad(kpad, k_ref, length)
            _dil_fill_pad(vpad, v_ref, length)
            dk_ref[...] = jnp.zeros_like(dk_ref)
            dv_ref[...] = jnp.zeros_like(dv_ref)

        r0 = pl.multiple_of(i * DIL_QB, DIL_QB)
        valid = _dil_valid(r0, length)
        for m in range(N_HEADS // 2):
            lanes = pl.ds(m * 128, 128)
            dq2, dk2, dv2, ds_heads = _attn_pair_bwd(
                q_ref[:, lanes], kpad[pl.ds(r0, DIL_WIN), lanes], vpad[pl.ds(r0, DIL_WIN), lanes],
                (bias_ref[2 * m], bias_ref[2 * m + 1]), valid, do_ref[:, lanes], dlse_ref[:, lanes])
            dq_ref[:, lanes] = dq2
            dsc_ref[2 * m] += ds_heads[0]
            dsc_ref[2 * m + 1] += ds_heads[1]
            for first, size, live in ((0, DIL_HALF, i >= 1), (DIL_HALF, DIL_QB, None), (DIL_HALF + DIL_QB, DIL_HALF, i <= nqb - 2)):
                def add(first=first, size=size, dk2=dk2, dv2=dv2, lanes=lanes):
                    rows = pl.ds(pl.multiple_of(r0 - DIL_HALF + first, DIL_HALF), size)
                    dk_ref[rows, lanes] += dk2[first:first + size]
                    dv_ref[rows, lanes] += dv2[first:first + size]
                if live is None:
                    add()
                else:
                    pl.when(live)(add)

    out = jax.ShapeDtypeStruct((bsz, length, dil * GROUP_WIDTH), F32)
    dsc_shape = (N_HEADS, DIL_QB, DIL_WIN)
    dq, dk, dv, dsc = pl.pallas_call(
        body, name=name, grid=(bsz, dil, nqb),
        in_specs=[pl.BlockSpec(memory_space=pltpu.SMEM), pl.BlockSpec(bkt.shape, lambda b, rho, i: (0, 0)), blk, seq, seq, blk, blk],
        out_specs=[blk, seq, seq, pl.BlockSpec(dsc_shape, lambda b, rho, i: (0, 0, 0))],
        out_shape=[out, out, out, jax.ShapeDtypeStruct(dsc_shape, F32)],
        scratch_shapes=[pltpu.VMEM((length + 2 * DIL_HALF, GROUP_WIDTH), BF16), pltpu.VMEM((length + 2 * DIL_HALF, GROUP_WIDTH), BF16),
                        pltpu.VMEM(dsc_shape, F32)],
        compiler_params=_cparams(("arbitrary", "arbitrary", "arbitrary")),
    )(table, bkt, view(qb), view(kb), view(vb), view(do), view(dlse))
    return dq.reshape(qb.shape), dk.reshape(qb.shape), dv.reshape(qb.shape), dsc


def _headnorm(x, g, mavg):
    return x * lax.rsqrt(_head_stat(x * x, mavg) + RMS_EPS) * g


def _fold_gain(dg_full, fold):
    return jnp.dot(jnp.broadcast_to(dg_full, (8, dg_full.shape[1])), fold, precision=HI, preferred_element_type=F32)


def _bprep_fn(qp, kp, gq, gk, mavg):
    return _headnorm(qp, gq, mavg) * (HEAD_DIM ** -0.5), _headnorm(kp, gk, mavg)


def _dil_layout(dil):
    return None if dil == 1 else Strided(dil)


def _bprep_fwd(z, gq, gk, name):
    mavg = jnp.asarray(_np_group_avg(GROUP_WIDTH))

    def fn(qp, kp, vp, gqv, gkv, mv):
        qb, kb = _bprep_fn(qp, kp, gqv, gkv, mv)
        return (qb, kb, vp) * len(DIL_PATTERNS)

    w = GROUP_WIDTH
    outs = [(w, BF16, _dil_layout(dil)) for _, dil in DIL_PATTERNS for _ in range(3)]
    res = _rowmap(fn, [(z, w, 2), (z, w, 3), (z, w, 4)], [gq, gk, mavg], outs, [], name, 512, z.shape[0])
    return [res[3 * i:3 * i + 3] for i in range(len(DIL_PATTERNS))]


def _bprep_bwd(z, dqs, dks, dvs, gq, gk, name):
    mavg = jnp.asarray(_np_group_avg(GROUP_WIDTH))
    fold = jnp.asarray(_np_tile_fold(GROUP_WIDTH))

    def fn(qp, kp, dq0, dq1, dq2, dk0, dk1, dk2, dv0, dv1, dv2, gqv, gkv, mv, fv):
        _, vjp = jax.vjp(functools.partial(_bprep_fn, mavg=mv), qp, kp, gqv, gkv)
        dqp, dkp, dgq, dgk = vjp((dq0 + dq1 + dq2, dk0 + dk1 + dk2))
        return dqp, dkp, dv0 + dv1 + dv2, _fold_gain(dgq, fv), _fold_gain(dgk, fv)

    w = GROUP_WIDTH
    rows = [(z, w, 2), (z, w, 3)] + _pattern_rows(dqs) + _pattern_rows(dks) + _pattern_rows(dvs)
    return _rowmap(fn, rows, [gq, gk, mavg, fold], [(w, F32)] * 3, [(8, HEAD_DIM)] * 2, name, 256, z.shape[0])


def _mixture_fn(o0, o1, o2, l0, l1, l2):
    m = lax.stop_gradient(jnp.maximum(jnp.maximum(l0, l1), l2))
    e0, e1, e2 = jnp.exp(l0 - m), jnp.exp(l1 - m), jnp.exp(l2 - m)
    return (e0 * o0 + e1 * o1 + e2 * o2) / (e0 + e1 + e2)


def _pattern_rows(arrs):
    return [(a, GROUP_WIDTH, 0) if dil == 1 else (a, GROUP_WIDTH, 0, Strided(dil)) for a, (_, dil) in zip(arrs, DIL_PATTERNS)]


def _mixture_fwd(os_, ls_, name):
    n = os_[0].shape[0]
    return _rowmap(lambda *v: (_mixture_fn(*v),), _pattern_rows(os_) + _pattern_rows(ls_), [], [(GROUP_WIDTH, F32)], [], name, 512, n)[0]


def _mixture_bwd(os_, ls_, dy, name):
    w = GROUP_WIDTH

    def fn(*v):
        _, vjp = jax.vjp(_mixture_fn, *v[:6])
        return vjp(v[6])

    outs = [(w, F32, _dil_layout(dil)) for _ in range(2) for _, dil in DIL_PATTERNS]
    return _rowmap(fn, _pattern_rows(os_) + _pattern_rows(ls_) + [(dy, w, 0)], [], outs, [], name, 512, dy.shape[0])


def _relbias_fold(dscs, name):
    bkts = [jnp.asarray(_np_dil_buckets(dil)) for _, dil in DIL_PATTERNS]
    npat = len(DIL_PATTERNS)

    def body(*refs):
        bkt_refs, d_refs, o_ref = refs[:npat], refs[npat:-1], refs[-1]
        row = lax.broadcasted_iota(jnp.int32, (REL_BUCKETS, 128), 0)
        lane = lax.broadcasted_iota(jnp.int32, (REL_BUCKETS, 128), 1)
        out = jnp.zeros((REL_BUCKETS, 128), F32)
        for p, (_, dil) in enumerate(DIL_PATTERNS):
            bkt = bkt_refs[p][...]
            for h in range(N_HEADS):
                d = d_refs[2 * p][h] + d_refs[2 * p + 1][h]
                for b in _dil_live_buckets(dil):
                    val = jnp.sum(jnp.where(bkt == b, d, 0.0), keepdims=True)
                    out = out + jnp.where((row == b) & (lane == h), val, 0.0)
        o_ref[...] = out

    return pl.pallas_call(
        body, name=name, out_shape=jax.ShapeDtypeStruct((REL_BUCKETS, 128), F32), compiler_params=_cparams(),
    )(*bkts, *dscs)


DPREP_TM = 512


def _dprep_fn(qp, kp, vp, gq, gk, cq, sq, ck, sk, mavg_q, mavg_k, perm_q, perm_k, expand):
    rot = lambda x, perm: jnp.dot(x, perm, precision=SPLIT3, preferred_element_type=F32)
    qn = _headnorm(qp, gq, mavg_q)
    kn = _headnorm(kp, gk, mavg_k)
    qr = (qn * cq + rot(qn, perm_q) * sq) * (HEAD_DIM ** -0.5)
    kr = kn * ck + rot(kn, perm_k) * sk
    return qr, rot(kr, expand), rot(vp, expand)


def _dprep_consts():
    cq, sq = _np_rope_tables(N_HEADS)
    ck, sk = _np_rope_tables(KV_WIDTH // HEAD_DIM)
    tables = [jnp.asarray(a) for a in (cq, sq, ck, sk)]
    mats = [jnp.asarray(a) for a in (_np_group_avg(GROUP_WIDTH), _np_group_avg(KV_WIDTH), _np_rope_partner(GROUP_WIDTH),
                                      _np_rope_partner(KV_WIDTH), _np_kv_expand())]
    per = SEQ // DPREP_TM
    w, kw = GROUP_WIDTH, KV_WIDTH
    table_rows = [(tables[0], w, 0, per), (tables[1], w, 0, per), (tables[2], kw, 0, per), (tables[3], kw, 0, per)]
    return table_rows, mats


def _dprep_fwd(z, gq, gk, name):
    table_rows, mats = _dprep_consts()
    w, kw = GROUP_WIDTH, KV_WIDTH

    def fn(qp, kp, vp, cq, sq, ck, sk, gqv, gkv, *m):
        return _dprep_fn(qp, kp, vp, gqv, gkv, cq, sq, ck, sk, *m)

    return _rowmap(fn, [(z, w, 7), (z, kw, 32), (z, kw, 33)] + table_rows, [gq, gk] + mats, [(w, BF16)] * 3, [], name,
                   DPREP_TM, z.shape[0])


def _dprep_bwd(z, dq, dkx, dvx, gq, gk, name):
    table_rows, mats = _dprep_consts()
    fold_q = jnp.asarray(_np_tile_fold(GROUP_WIDTH))
    fold_k = jnp.asarray(_np_tile_fold(KV_WIDTH))
    w, kw = GROUP_WIDTH, KV_WIDTH

    def fn(qp, kp, vp, dqv, dkv, dvv, cq, sq, ck, sk, gqv, gkv, fq, fk, *m):
        f = lambda a, b, c, d, e: _dprep_fn(a, b, c, d, e, cq, sq, ck, sk, *m)
        _, vjp = jax.vjp(f, qp, kp, vp, gqv, gkv)
        dqp, dkp, dvp, dgq, dgk = vjp((dqv, dkv, dvv))
        return dqp, dkp, dvp, _fold_gain(dgq, fq), _fold_gain(dgk, fk)

    return _rowmap(fn, [(z, w, 7), (z, kw, 32), (z, kw, 33), (dq, w, 0), (dkx, w, 0), (dvx, w, 0)] + table_rows,
                   [gq, gk, fold_q, fold_k] + mats, [(w, F32), (kw, F32), (kw, F32)], [(8, HEAD_DIM)] * 2, name,
                   DPREP_TM, z.shape[0])


GQA_QB = 256


def _gqa_fwd(q, kx, vx, name):
    bsz = q.shape[0]
    blk = pl.BlockSpec((None, GQA_QB, GROUP_WIDTH), lambda b, i: (b, i, 0))
    seq = pl.BlockSpec((None, SEQ, GROUP_WIDTH), lambda b, i: (b, 0, 0))

    def body(q_ref, k_ref, v_ref, o_ref):
        for m in range(N_HEADS // 2):
            lanes = pl.ds(m * 128, 128)
            o_ref[:, lanes] = _attn_pair_fwd(q_ref[:, lanes], k_ref[:, lanes], v_ref[:, lanes], None, None)[0]

    return pl.pallas_call(
        body, name=name, grid=(bsz, SEQ // GQA_QB), in_specs=[blk, seq, seq], out_specs=blk,
        out_shape=jax.ShapeDtypeStruct((bsz, SEQ, GROUP_WIDTH), F32), compiler_params=_cparams(("parallel", "parallel")),
    )(q, kx, vx)


def _gqa_bwd(q, kx, vx, do, name):
    bsz = q.shape[0]
    blk = pl.BlockSpec((None, GQA_QB, GROUP_WIDTH), lambda b, i: (b, i, 0))
    seq = pl.BlockSpec((None, SEQ, GROUP_WIDTH), lambda b, i: (b, 0, 0))

    def body(q_ref, k_ref, v_ref, do_ref, dq_ref, dk_ref, dv_ref):
        @pl.when(pl.program_id(1) == 0)
        def _():
            dk_ref[...] = jnp.zeros_like(dk_ref)
            dv_ref[...] = jnp.zeros_like(dv_ref)

        for m in range(N_HEADS // 2):
            lanes = pl.ds(m * 128, 128)
            dq2, dk2, dv2, _ = _attn_pair_bwd(q_ref[:, lanes], k_ref[:, lanes], v_ref[:, lanes], None, None, do_ref[:, lanes], None)
            dq_ref[:, lanes] = dq2
            dk_ref[:, lanes] += dk2
            dv_ref[:, lanes] += dv2

    out = jax.ShapeDtypeStruct((bsz, SEQ, GROUP_WIDTH), F32)
    return pl.pallas_call(
        body, name=name, grid=(bsz, SEQ // GQA_QB), in_specs=[blk, seq, seq, blk], out_specs=[blk, seq, seq],
        out_shape=[out, out, out], compiler_params=_cparams(("parallel", "arbitrary")),
    )(q, kx, vx, do)


CONV_TILE = 64
CONV_LEAD = 16
CONV_WINDOW = CONV_TILE + 32


def _glu(a, g):
    return a * jax.nn.sigmoid(g)


def _conv_post(c, b, ln_g, ln_b):
    x = c + b
    xc = x - jnp.mean(x, axis=-1, keepdims=True)
    y = xc * lax.rsqrt(jnp.mean(xc * xc, axis=-1, keepdims=True) + LN_EPS) * ln_g + ln_b
    return y * jax.nn.sigmoid(y)


def _conv_shifts(win):
    out = []
    for phase in range(8):
        rolled = win if phase == 0 else pltpu.roll(win, CONV_WINDOW - phase, 0)
        for base in range(0, CONV_WINDOW - CONV_TILE + 1, 8):
            if 1 <= base + phase <= CONV_WIDTH:
                out.append((base + phase, rolled[base:base + CONV_TILE]))
    return out


def _conv_fill(pad_ref, value_of_tile):
    zeros = jnp.zeros((CONV_LEAD, GROUP_WIDTH), F32)
    pad_ref[pl.ds(0, CONV_LEAD), :] = zeros
    pad_ref[pl.ds(CONV_LEAD + SEQ, CONV_LEAD), :] = zeros

    def step(t, carry):
        r0 = pl.multiple_of(t * CONV_TILE, CONV_TILE)
        pad_ref[pl.ds(CONV_LEAD + r0, CONV_TILE), :] = value_of_tile(r0)
        return carry

    lax.fori_loop(0, SEQ // CONV_TILE, step, 0)


def _conv_tile(pad_ref, w_ref, r0, flip):
    acc = jnp.zeros((CONV_TILE, GROUP_WIDTH), F32)
    for offset, rows in _conv_shifts(pad_ref[pl.ds(r0, CONV_WINDOW), :]):
        k = (CONV_WIDTH - offset) if flip else (offset - 1)
        acc = acc + w_ref[pl.ds(k, 1), :] * rows
    return acc


def _conv_fwd(z3, w, b, ln_g, ln_b, name):
    bsz = z3.shape[0]
    seq = lambda cb: pl.BlockSpec((None, SEQ, GROUP_WIDTH), functools.partial(lambda i, cb: (i, 0, cb), cb=cb))
    full = lambda a: pl.BlockSpec(a.shape, lambda i: (0,) * a.ndim)

    def body(a_ref, g_ref, w_ref, b_ref, lg_ref, lb_ref, y_ref, pad_ref):
        _conv_fill(pad_ref, lambda r0: _glu(a_ref[pl.ds(r0, CONV_TILE), :], g_ref[pl.ds(r0, CONV_TILE), :]))

        def step(t, carry):
            r0 = pl.multiple_of(t * CONV_TILE, CONV_TILE)
            y_ref[pl.ds(r0, CONV_TILE), :] = _conv_post(_conv_tile(pad_ref, w_ref, r0, False), b_ref[...], lg_ref[...], lb_ref[...])
            return carry

        lax.fori_loop(0, SEQ // CONV_TILE, step, 0)

    return pl.pallas_call(
        body, name=name, grid=(bsz,), in_specs=[seq(5), seq(6), full(w), full(b), full(ln_g), full(ln_b)], out_specs=seq(0),
        out_shape=jax.ShapeDtypeStruct((bsz, SEQ, GROUP_WIDTH), F32),
        scratch_shapes=[pltpu.VMEM((SEQ + 2 * CONV_LEAD, GROUP_WIDTH), F32)], compiler_params=_cparams(("parallel",)),
    )(z3, z3, w, b, ln_g, ln_b)


def _conv_bwd(z3, dy, w, b, ln_g, ln_b, name):
    bsz = z3.shape[0]
    seq = lambda cb: pl.BlockSpec((None, SEQ, GROUP_WIDTH), functools.partial(lambda i, cb: (i, 0, cb), cb=cb))
    full = lambda a: pl.BlockSpec(a.shape, lambda i: (0,) * a.ndim)
    vec = pl.BlockSpec((1, GROUP_WIDTH), lambda i: (0, 0))

    def body(a_ref, g_ref, dy_ref, w_ref, b_ref, lg_ref, lb_ref, da_ref, dg_ref, dw_ref, db_ref, dlg_ref, dlb_ref, hpad, dpad, dw8):
        @pl.when(pl.program_id(0) == 0)
        def _():
            dw8[...] = jnp.zeros_like(dw8)
            db_ref[...] = jnp.zeros_like(db_ref)
            dlg_ref[...] = jnp.zeros_like(dlg_ref)
            dlb_ref[...] = jnp.zeros_like(dlb_ref)

        _conv_fill(hpad, lambda r0: _glu(a_ref[pl.ds(r0, CONV_TILE), :], g_ref[pl.ds(r0, CONV_TILE), :]))
        zeros = jnp.zeros((CONV_LEAD, GROUP_WIDTH), F32)
        dpad[pl.ds(0, CONV_LEAD), :] = zeros
        dpad[pl.ds(CONV_LEAD + SEQ, CONV_LEAD), :] = zeros

        def through_post(t, carry):
            r0 = pl.multiple_of(t * CONV_TILE, CONV_TILE)
            conv = _conv_tile(hpad, w_ref, r0, False)
            _, vjp = jax.vjp(_conv_post, conv, b_ref[...], lg_ref[...], lb_ref[...])
            dconv, db, dlg, dlb = vjp(dy_ref[pl.ds(r0, CONV_TILE), :])
            db_ref[...] += db
            dlg_ref[...] += dlg
            dlb_ref[...] += dlb
            dpad[pl.ds(CONV_LEAD + r0, CONV_TILE), :] = dconv
            for offset, rows in _conv_shifts(hpad[pl.ds(r0, CONV_WINDOW), :]):
                prod = dconv * rows
                part = prod[0:8]
                for j in range(1, CONV_TILE // 8):
                    part = part + prod[8 * j:8 * j + 8]
                dw8[offset - 1] += part
            return carry

        lax.fori_loop(0, SEQ // CONV_TILE, through_post, 0)

        def through_glu(t, carry):
            r0 = pl.multiple_of(t * CONV_TILE, CONV_TILE)
            dh = _conv_tile(dpad, w_ref, r0, True)
            rows = pl.ds(r0, CONV_TILE)
            _, vjp = jax.vjp(_glu, a_ref[rows, :], g_ref[rows, :])
            da, dg = vjp(dh)
            da_ref[rows, :] = da
            dg_ref[rows, :] = dg
            return carry

        lax.fori_loop(0, SEQ // CONV_TILE, through_glu, 0)
        dw_ref[...] = jnp.sum(dw8[...], axis=1)

    out = jax.ShapeDtypeStruct((bsz, SEQ, GROUP_WIDTH), F32)
    v = jax.ShapeDtypeStruct((1, GROUP_WIDTH), F32)
    return pl.pallas_call(
        body, name=name, grid=(bsz,), in_specs=[seq(5), seq(6), seq(0), full(w), full(b), full(ln_g), full(ln_b)],
        out_specs=[seq(0), seq(0), pl.BlockSpec((CONV_WIDTH, GROUP_WIDTH), lambda i: (0, 0)), vec, vec, vec],
        out_shape=[out, out, jax.ShapeDtypeStruct((CONV_WIDTH, GROUP_WIDTH), F32), v, v, v],
        scratch_shapes=[pltpu.VMEM((SEQ + 2 * CONV_LEAD, GROUP_WIDTH), F32), pltpu.VMEM((SEQ + 2 * CONV_LEAD, GROUP_WIDTH), F32),
                        pltpu.VMEM((CONV_WIDTH, 8, GROUP_WIDTH), F32)],
        compiler_params=_cparams(("arbitrary",)),
    )(z3, z3, dy, w, b, ln_g, ln_b)


def _mixnorm_fwd(ys, gains, name, follow=None):
    w = GROUP_WIDTH

    def fn(*v):
        return (jnp.concatenate([_rms(v[i], v[4 + i]) for i in range(4)], axis=-1),)

    fulls = list(gains) if follow is None else [*gains, follow]
    return _rowmap(fn, [(y, w, 0) for y in ys], fulls, [(4 * w, BF16)], [], name, 512, ys[0].shape[0])[0]


def _mixnorm_bwd(dyn, ys, gains, name, follow=None):
    w = GROUP_WIDTH
    gains = list(gains) if follow is None else [*gains, follow]

    def fn(*v):
        dys, dgs = [], []
        for i in range(4):
            _, vjp = jax.vjp(_rms, v[4 + i], v[8 + i])
            dy, dg = vjp(v[i])
            dys.append(dy)
            dgs.append(dg)
        return (*dys, *dgs)

    rows = [(dyn, w, i) for i in range(4)] + [(y, w, 0) for y in ys]
    return _rowmap(fn, rows, list(gains), [(w, F32)] * 4, [(1, w)] * 4, name, 512, dyn.shape[0])


def _adamw_fn(w, g, m, v):
    m = ADAM_B1 * m + (1.0 - ADAM_B1) * g
    v = ADAM_B2 * v + (1.0 - ADAM_B2) * (g * g)
    m_hat = m / (1.0 - ADAM_B1 ** ADAM_STEP)
    v_hat = v / (1.0 - ADAM_B2 ** ADAM_STEP)
    delta = -ADAM_LR * (m_hat / (jnp.sqrt(v_hat) + ADAM_EPS) + ADAM_WD * w)
    return delta, m, v


def _adamw(w, g, m, v, name):
    r, c = w.shape
    tm = _pick(r, (256, 128, 64, 32, 16, 8))
    return _rowmap(_adamw_fn, [(a, c, 0) for a in (w, g, m, v)], [], [(c, F32)] * 3, [], name, tm, r)


def _layer_params(l, small, big, prefetch):
    tile_row = lambda g, n: jnp.tile(g, n)[None, :]
    row = lambda g: g[None, :]
    w_s = small["sgu_w"][l].astype(BF16)
    return dict(
        norm1_g=row(small["norm1_g"][l]), norm2_g=row(small["norm2_g"][l]),
        w_s=w_s, w_s_t=jnp.swapaxes(w_s, 1, 2), bm=jnp.repeat(small["sgu_b"][l].T, HEAD_DIM, axis=1),
        gq_dil=tile_row(small["dil_qn_g"][l], N_HEADS), gk_dil=tile_row(small["dil_kn_g"][l], N_HEADS),
        conv_w=small["conv_w"][l], conv_b=row(small["conv_b"][l]), conv_ln_g=row(small["conv_ln_g"][l]),
        conv_ln_b=row(small["conv_ln_b"][l]),
        gq_gqa=tile_row(small["gqa_qn_g"][l], N_HEADS), gk_gqa=tile_row(small["gqa_kn_g"][l], KV_WIDTH // HEAD_DIM),
        mix_g=[row(small["mix_norm_g"][l][i * GROUP_WIDTH:(i + 1) * GROUP_WIDTH]) for i in range(4)],
        big=big, prefetch=prefetch,
    )


def _layer_fwd(x, p, table, bsz, tag):
    t = x.shape[0]
    seq3 = lambda a: a.reshape(bsz, SEQ, a.shape[-1])
    flat = lambda a: a.reshape(t, a.shape[-1])
    h1 = _rmsnorm_fwd(x, p["norm1_g"], tag + "rms1")
    z = _matmul([(h1, p["big"]("w_in", h1))], "nn", F32, tag + "mm_z")
    y_a = _sgu_fwd(z, p["w_s"], p["bm"], tag + "sgu_fwd")
    p["prefetch"](0, "w_out", y_a)
    dil_qkv = _bprep_fwd(z, p["gq_dil"], p["gk_dil"], tag + "dil_prep")
    outs, lses = [], []
    for (_, dil), (qb, kb, vb) in zip(DIL_PATTERNS, dil_qkv):
        o, lse = _dil_fwd(qb, kb, vb, table, dil, f"{tag}dil{dil}_fwd")
        outs.append(o)
        lses.append(lse)
    y_b = _mixture_fwd(outs, lses, tag + "dil_mix")
    y_c = flat(_conv_fwd(seq3(z), p["conv_w"], p["conv_b"], p["conv_ln_g"], p["conv_ln_b"], tag + "conv_fwd"))
    qd, kx, vx = _dprep_fwd(z, p["gq_gqa"], p["gk_gqa"], tag + "gqa_prep")
    y_d = flat(_gqa_fwd(seq3(qd), seq3(kx), seq3(vx), tag + "gqa_fwd"))
    ys = [y_a, y_b, y_c, y_d]
    started = p["prefetch"](0, "w_gate", y_d)
    yn = _mixnorm_fwd(ys, p["mix_g"], tag + "mixnorm", follow=started)
    x_mid = _matmul([(yn, p["big"]("w_out", yn))], "nn", F32, tag + "mm_out", residual=x)
    h2 = _rmsnorm_fwd(x_mid, p["norm2_g"], tag + "rms2")
    act, act_du, act_dg = _ffn_up(h2, p["big"]("w_gate", h2), p["big"]("w_up", h2), tag + "ffn_up")
    p["prefetch"](1, "w_in", act)
    x_out = _matmul([(act, p["big"]("w_down", act))], "nn", F32, tag + "mm_down", residual=x_mid)
    saved = dict(x=x, h1=h1, z=z, dil_qkv=dil_qkv, outs=outs, lses=lses, qd=qd, kx=kx, vx=vx, ys=ys, yn=yn, x_mid=x_mid,
                 h2=h2, act=act, act_du=act_du, act_dg=act_dg)
    return x_out, saved


def _layer_bwd(dx_out, dx_out_b, s, p, table, bsz, tag, emit, mid_hook):
    t = dx_out.shape[0]
    seq3 = lambda a: a.reshape(bsz, SEQ, a.shape[-1])
    flat = lambda a: a.reshape(t, a.shape[-1])
    z = s["z"]
    small = {}
    weight = lambda name: p["big"](name, None)
    emit("w_down", _matmul([(s["act"], dx_out_b)], "tn", BF16, tag + "mm_dwdown").reshape(N_CHIPS, FFN_HIDDEN // N_CHIPS, D_MODEL))
    dgate, dup = _ffn_down_bwd(dx_out_b, weight("w_down"), s["act_du"], s["act_dg"], tag + "ffn_dact")
    emit("w_gate", _matmul([(s["h2"], dgate)], "tn", BF16, tag + "mm_dwgate", slabs=N_CHIPS))
    started = emit("w_up", _matmul([(s["h2"], dup)], "tn", BF16, tag + "mm_dwup", slabs=N_CHIPS))
    dh2 = _matmul([(dgate, weight("w_gate")), (dup, weight("w_up"))], "nt", F32, tag + "mm_dh2")
    dx_mid, dx_mid_b, dg2 = _rmsnorm_bwd(dh2, s["x_mid"], p["norm2_g"], dx_out, tag + "rms2_bwd", follow=started)
    small["norm2_g"] = dg2[0]
    mid_hook(dx_mid)
    dyn = _matmul([(dx_mid_b, weight("w_out"))], "nt", F32, tag + "mm_dyn")
    started = emit("w_out", _matmul([(s["yn"], dx_mid_b)], "tn", BF16, tag + "mm_dwout").reshape(N_CHIPS, D_MODEL // N_CHIPS, D_MODEL))
    *dys, dga, dgb, dgc, dgd = _mixnorm_bwd(dyn, s["ys"], p["mix_g"], tag + "mixnorm_bwd", follow=started)
    small["mix_norm_g"] = jnp.concatenate([dga[0], dgb[0], dgc[0], dgd[0]])
    du, dv, dws, dbs = _sgu_bwd(z, dys[0], p["w_s"], p["w_s_t"], p["bm"], tag + "sgu_bwd")
    small["sgu_w"] = dws
    small["sgu_b"] = dbs[:, :N_HEADS].T
    *douts, dl0, dl1, dl2 = _mixture_bwd(s["outs"], s["lses"], dys[1], tag + "dil_mix_bwd")
    dlses = [dl0, dl1, dl2]
    dqs, dks, dvs, dscs = [], [], [], []
    for i, (_, dil) in enumerate(DIL_PATTERNS):
        dq, dk, dvv, dsc = _dil_bwd(*s["dil_qkv"][i], douts[i], dlses[i], table, dil, f"{tag}dil{dil}_bwd")
        dqs.append(dq)
        dks.append(dk)
        dvs.append(dvv)
        dscs.append(dsc)
    dbq, dbk, dbv, dgq, dgk = _bprep_bwd(z, dqs, dks, dvs, p["gq_dil"], p["gk_dil"], tag + "dil_prep_bwd")
    small["dil_qn_g"], small["dil_kn_g"] = dgq[0], dgk[0]
    dca, dcg, dcw, dcb, dclg, dclb = _conv_bwd(seq3(z), seq3(dys[2]), p["conv_w"], p["conv_b"], p["conv_ln_g"], p["conv_ln_b"],
                                               tag + "conv_bwd")
    small["conv_w"], small["conv_b"], small["conv_ln_g"], small["conv_ln_b"] = dcw, dcb[0], dclg[0], dclb[0]
    dqd, dkx, dvx = _gqa_bwd(seq3(s["qd"]), seq3(s["kx"]), seq3(s["vx"]), seq3(dys[3]), tag + "gqa_bwd")
    ddq, ddk, ddv, dgq, dgk = _dprep_bwd(z, flat(dqd), flat(dkx), flat(dvx), p["gq_gqa"], p["gk_gqa"], tag + "gqa_prep_bwd")
    small["gqa_qn_g"], small["gqa_kn_g"] = dgq[0], dgk[0]
    dz = jnp.concatenate([a.astype(BF16) for a in (du, dv, dbq, dbk, dbv, flat(dca), flat(dcg), ddq, ddk, ddv)], axis=1)
    dz4 = dz.reshape(t, N_CHIPS, IN_WIDTH // N_CHIPS).transpose(1, 0, 2)
    started = emit("w_in", _matmul([(s["h1"], dz4)], "tn", BF16, tag + "mm_dwin", slabs=N_CHIPS))
    dh1 = _matmul([(dz, weight("w_in"))], "nt", F32, tag + "mm_dh1")
    dx, dx_b, dg1 = _rmsnorm_bwd(dh1, s["x"], p["norm1_g"], dx_mid, tag + "rms1_bwd", follow=started)
    small["norm1_g"] = dg1[0]
    return dx, dx_b, small, dscs


def _local_step(x, target, small, big, emit, mid_hook, bsz, prefetch=lambda l, name, after: None):
    table = small["rel_bias"]
    ahead = lambda l: (lambda more, name, after: prefetch(l + more, name, after))
    params = [_layer_params(l, small, functools.partial(big, l), ahead(l)) for l in range(DEPTH)]
    saved = []
    h = x
    for l in range(DEPTH):
        h, sv = _layer_fwd(h, params[l], table, bsz, f"l{l}_")
        saved.append(sv)
    dh, dh_b, loss = _loss_fwd_bwd(h, target, "loss")
    small_grads, dscs = [None] * DEPTH, [None] * DEPTH
    for l in reversed(range(DEPTH)):
        dh, dh_b, small_grads[l], dscs[l] = _layer_bwd(dh, dh_b, saved[l], params[l], table, bsz, f"l{l}_",
                                                       functools.partial(emit, l), functools.partial(mid_hook, l))
    fold_in = [dscs[l][i] for i in range(len(DIL_PATTERNS)) for l in range(DEPTH)]
    stacked = {k: jnp.stack([small_grads[l][k] for l in range(DEPTH)]) for k in small_grads[0]}
    stacked["rel_bias"] = _relbias_fold(fold_in, "relbias_fold")[:, :N_HEADS]
    return loss, dh, stacked


def _mesh_pos():
    return lax.axis_index("x"), lax.axis_index("y"), lax.axis_index("c")


def _other_chips(x, y):
    return [(1 - x, y), (x, 1 - y), (1 - x, 1 - y)]


_ANY = pl.BlockSpec(memory_space=pl.ANY)


def _swap_sibling(arrs, name):
    n = len(arrs)

    def body(*refs):
        in_refs, out_refs, send_sems, recv_sems = refs[:n], refs[n:2 * n], refs[2 * n], refs[2 * n + 1]
        x, y, c = _mesh_pos()
        copies = [pltpu.make_async_remote_copy(src_ref=in_refs[k], dst_ref=out_refs[k], send_sem=send_sems.at[k],
                                               recv_sem=recv_sems.at[k], device_id=(x, y, 1 - c), device_id_type=MESH)
                  for k in range(n)]
        for cp in copies:
            cp.start()
        for cp in copies:
            cp.wait()

    return pl.pallas_call(
        body, name=name, in_specs=[_ANY] * n, out_specs=[_ANY] * n,
        out_shape=[jax.ShapeDtypeStruct(a.shape, a.dtype) for a in arrs],
        scratch_shapes=[pltpu.SemaphoreType.DMA((n,)), pltpu.SemaphoreType.DMA((n,))],
    )(*arrs)


def _complete_pairs(arrs, name):
    n = len(arrs)

    def body(*refs):
        in_refs, out_refs, send_sems, recv_sems = refs[:n], refs[n:2 * n], refs[2 * n], refs[2 * n + 1]
        x, y, c = _mesh_pos()
        copies = [pltpu.make_async_remote_copy(src_ref=in_refs[k].at[c], dst_ref=out_refs[k].at[c], send_sem=send_sems.at[k],
                                               recv_sem=recv_sems.at[k], device_id=(x, y, 1 - c), device_id_type=MESH)
                  for k in range(n)]
        for cp in copies:
            cp.start()
        for k, cp in enumerate(copies):
            cp.wait_send()
            pltpu.make_async_remote_copy(src_ref=in_refs[k].at[1 - c], dst_ref=out_refs[k].at[1 - c], send_sem=send_sems.at[k],
                                         recv_sem=recv_sems.at[k], device_id=(x, y, 1 - c), device_id_type=MESH).wait_recv()

    return pl.pallas_call(
        body, name=name, in_specs=[_ANY] * n, out_specs=[_ANY] * n,
        out_shape=[jax.ShapeDtypeStruct(a.shape, a.dtype) for a in arrs], input_output_aliases={k: k for k in range(n)},
        scratch_shapes=[pltpu.SemaphoreType.DMA((n,)), pltpu.SemaphoreType.DMA((n,))],
    )(*arrs)


_HBM = pl.BlockSpec(memory_space=pltpu.HBM)
_SEM = pl.BlockSpec(memory_space=pltpu.SEMAPHORE)
_DATAFLOW = pltpu.SideEffectType.DATAFLOW_SIDE_EFFECTING


def _chip_copies(src_refs, land_refs, send_sems, recv_sems, scatter):
    x, y, c = _mesh_pos()
    me = 2 * x + y
    out = []
    for k, (src_ref, land_ref) in enumerate(zip(src_refs, land_refs)):
        if scatter:
            h = src_ref.shape[1] // 2
            for q in range(N_DEV - 1):
                fx, fy, fc = ((q + 1) >> 2) & 1, ((q + 1) >> 1) & 1, (q + 1) & 1
                px, py, pc = (1 - x if fx else x), (1 - y if fy else y), (1 - c if fc else c)
                src = src_ref.at[2 * px + py, pl.ds(pc * h, h)]
                sems = dict(send_sem=send_sems.at[7 * k + q], recv_sem=recv_sems.at[7 * k + q], device_id=(px, py, pc), device_id_type=MESH)
                out.append((pltpu.make_async_remote_copy(src_ref=src, dst_ref=land_ref.at[4 * x + 2 * y + c], **sems),
                            pltpu.make_async_remote_copy(src_ref=src, dst_ref=land_ref.at[4 * px + 2 * py + pc], **sems)))
            continue
        slot = (lambda chip: land_ref.at[c, chip]) if len(land_ref.shape) == 4 else (lambda chip: land_ref.at[chip])
        for j, (cx, cy) in enumerate(_other_chips(x, y)):
            sems = dict(send_sem=send_sems.at[3 * k + j], recv_sem=recv_sems.at[3 * k + j], device_id=(cx, cy, c), device_id_type=MESH)
            out.append((pltpu.make_async_remote_copy(src_ref=src_ref, dst_ref=slot(me), **sems),
                        pltpu.make_async_remote_copy(src_ref=src_ref, dst_ref=slot(2 * cx + cy), **sems)))
    return out


def _chips_start(srcs, scatter, after, name, per_core=False):
    n = len(srcs)
    n_sems = (N_DEV - 1 if scatter else N_CHIPS - 1) * n
    if scatter:
        lands = [lax.empty((N_DEV, s.shape[1] // 2, s.shape[2]), s.dtype) for s in srcs]
    else:
        lands = [lax.empty((*((2, N_CHIPS) if per_core else (N_CHIPS,)), *s.shape), s.dtype) for s in srcs]

    def body(*refs):
        src_refs, land_refs = refs[:n], refs[n:2 * n]
        send_sems, recv_sems, token = refs[2 * n + 1], refs[2 * n + 2], refs[-1]
        for sent, _ in _chip_copies(src_refs, land_refs, send_sems, recv_sems, scatter):
            sent.start()
        token[...] = jnp.zeros_like(token)

    hbm = lambda a: pltpu.HBM(a.shape, a.dtype)
    res = pl.pallas_call(
        body, name=name,
        in_specs=[_HBM] * (2 * n) + [_ANY],
        out_specs=[_SEM, _SEM] + [_HBM] * (2 * n) + [pl.BlockSpec(memory_space=pltpu.VMEM)],
        out_shape=[pltpu.SemaphoreType.DMA((n_sems,)), pltpu.SemaphoreType.DMA((n_sems,))] + [hbm(a) for a in srcs] + [hbm(a) for a in lands]
        + [jax.ShapeDtypeStruct((8, 128), F32)],
        input_output_aliases={i: 2 + i for i in range(2 * n)},
        compiler_params=pltpu.CompilerParams(has_side_effects=_DATAFLOW),
    )(*[pltpu.with_memory_space_constraint(a, pltpu.HBM) for a in (*srcs, *lands)], after)
    return (res[0], res[1], res[2:2 + n], res[2 + n:2 + 2 * n]), res[-1]


def _chips_wait(handle, scatter, after, name):
    send_sems, recv_sems, srcs, lands = handle
    n = len(srcs)

    def body(*refs):
        src_refs, land_refs = refs[:n], refs[n:2 * n]
        send_sems, recv_sems = refs[2 * n], refs[2 * n + 1]
        for sent, landed in _chip_copies(src_refs, land_refs, send_sems, recv_sems, scatter):
            sent.wait_send()
            landed.wait_recv()

    hbm = lambda a: pltpu.HBM(a.shape, a.dtype)
    res = pl.pallas_call(
        body, name=name,
        in_specs=[_HBM] * (2 * n) + [_SEM, _SEM, _ANY], out_specs=[_HBM] * (2 * n),
        out_shape=[hbm(a) for a in srcs] + [hbm(a) for a in lands],
        input_output_aliases={i: i for i in range(2 * n)},
        compiler_params=pltpu.CompilerParams(has_side_effects=_DATAFLOW),
    )(*srcs, *lands, send_sems, recv_sems, after)
    return res[n:]


N_DEV = 8


def _everyone_copies(src_ref, land_ref, send_sems, recv_sems):
    x, y, c = _mesh_pos()
    out = []
    for q in range(N_DEV - 1):
        fx, fy, fc = ((q + 1) >> 2) & 1, ((q + 1) >> 1) & 1, (q + 1) & 1
        px, py, pc = (1 - x if fx else x), (1 - y if fy else y), (1 - c if fc else c)
        sems = dict(send_sem=send_sems.at[q], recv_sem=recv_sems.at[q], device_id=(px, py, pc), device_id_type=MESH)
        out.append((pltpu.make_async_remote_copy(src_ref=src_ref, dst_ref=land_ref.at[4 * x + 2 * y + c], **sems),
                    pltpu.make_async_remote_copy(src_ref=src_ref, dst_ref=land_ref.at[4 * px + 2 * py + pc], **sems)))
    return out


def _everyone_start(block, after, name):
    land = lax.empty((N_DEV, *block.shape), block.dtype)

    def body(src_ref, land_ref, after_ref, send_sems, recv_sems, src_thru, land_thru, token):
        for sent, _ in _everyone_copies(src_ref, land_ref, send_sems, recv_sems):
            sent.start()
        token[...] = jnp.zeros_like(token)

    hbm = lambda a: pltpu.HBM(a.shape, a.dtype)
    n_sem = N_DEV - 1
    res = pl.pallas_call(
        body, name=name, in_specs=[_HBM, _HBM, _ANY],
        out_specs=[_SEM, _SEM, _HBM, _HBM, pl.BlockSpec(memory_space=pltpu.VMEM)],
        out_shape=[pltpu.SemaphoreType.DMA((n_sem,)), pltpu.SemaphoreType.DMA((n_sem,)), hbm(block), hbm(land),
                   jax.ShapeDtypeStruct((8, 128), F32)],
        input_output_aliases={0: 2, 1: 3}, compiler_params=pltpu.CompilerParams(has_side_effects=_DATAFLOW),
    )(pltpu.with_memory_space_constraint(block, pltpu.HBM), pltpu.with_memory_space_constraint(land, pltpu.HBM), after)
    return res[:4]


def _everyone_wait(handle, after, name):
    send_sems, recv_sems, block, land = handle

    def body(src_ref, land_ref, send_sems, recv_sems, after_ref, src_thru, land_thru):
        for sent, landed in _everyone_copies(src_ref, land_ref, send_sems, recv_sems):
            sent.wait_send()
            landed.wait_recv()

    hbm = lambda a: pltpu.HBM(a.shape, a.dtype)
    return pl.pallas_call(
        body, name=name, in_specs=[_HBM, _HBM, _SEM, _SEM, _ANY], out_specs=[_HBM, _HBM], out_shape=[hbm(block), hbm(land)],
        input_output_aliases={0: 0, 1: 1}, compiler_params=pltpu.CompilerParams(has_side_effects=_DATAFLOW),
    )(block, land, send_sems, recv_sems, after)


def _pair_copies(arr_refs, send_sems, recv_sems):
    x, y, c = _mesh_pos()
    out = []
    for k, ref in enumerate(arr_refs):
        sems = dict(send_sem=send_sems.at[k], recv_sem=recv_sems.at[k], device_id=(x, y, 1 - c), device_id_type=MESH)
        out.append((pltpu.make_async_remote_copy(src_ref=ref.at[c], dst_ref=ref.at[c], **sems),
                    pltpu.make_async_remote_copy(src_ref=ref.at[1 - c], dst_ref=ref.at[1 - c], **sems)))
    return out


def _pairs_start(arrs, after, name):
    n = len(arrs)

    def body(*refs):
        send_sems, recv_sems = refs[n + 1], refs[n + 2]
        for sent, _ in _pair_copies(refs[:n], send_sems, recv_sems):
            sent.start()
        refs[-1][...] = jnp.zeros_like(refs[-1])

    hbm = lambda a: pltpu.HBM(a.shape, a.dtype)
    res = pl.pallas_call(
        body, name=name, in_specs=[_HBM] * n + [_ANY],
        out_specs=[_SEM, _SEM] + [_HBM] * n + [pl.BlockSpec(memory_space=pltpu.VMEM)],
        out_shape=[pltpu.SemaphoreType.DMA((n,)), pltpu.SemaphoreType.DMA((n,))] + [hbm(a) for a in arrs] + [jax.ShapeDtypeStruct((8, 128), F32)],
        input_output_aliases={i: 2 + i for i in range(n)}, compiler_params=pltpu.CompilerParams(has_side_effects=_DATAFLOW),
    )(*[pltpu.with_memory_space_constraint(a, pltpu.HBM) for a in arrs], after)
    return (res[0], res[1], res[2:2 + n]), res[-1]


def _pairs_wait(handle, after, name):
    send_sems, recv_sems, arrs = handle
    n = len(arrs)

    def body(*refs):
        for sent, landed in _pair_copies(refs[:n], refs[n], refs[n + 1]):
            sent.wait_send()
            landed.wait_recv()

    hbm = lambda a: pltpu.HBM(a.shape, a.dtype)
    return pl.pallas_call(
        body, name=name, in_specs=[_HBM] * n + [_SEM, _SEM, _ANY], out_specs=[_HBM] * n, out_shape=[hbm(a) for a in arrs],
        input_output_aliases={i: i for i in range(n)}, compiler_params=pltpu.CompilerParams(has_side_effects=_DATAFLOW),
    )(*arrs, send_sems, recv_sems, after)


def _allgather_sum_small(block, name):
    m_per, n = block.shape

    def body(x_ref, out_ref, sum_ref, send_sems, recv_sems, local_sem):
        x, y, c = _mesh_pos()
        me, sibling = (x, y, c), (x, y, 1 - c)
        chips = _other_chips(x, y)

        def rows(px, py, pc):
            return out_ref.at[pl.ds((4 * px + 2 * py + pc) * m_per, m_per), :]

        def copy(k, blk, to, src=None):
            return pltpu.make_async_remote_copy(src_ref=rows(*blk) if src is None else src, dst_ref=rows(*blk),
                                                send_sem=send_sems.at[k], recv_sem=recv_sems.at[k], device_id=to, device_id_type=MESH)

        mine = pltpu.make_async_copy(x_ref, rows(*me), local_sem)
        mine.start()
        first = [copy(0, me, sibling, src=x_ref)]
        first += [copy(1 + j, me, (*chip, c), src=x_ref) for j, chip in enumerate(chips)]
        for cp in first:
            cp.start()
        passed = [copy(4 + j, (*chip, c), sibling) for j, chip in enumerate(chips)]
        for j, chip in enumerate(chips):
            copy(1 + j, (*chip, c), me).wait_recv()
            passed[j].start()
        copy(0, sibling, me).wait_recv()
        for j, chip in enumerate(chips):
            copy(4 + j, (*chip, 1 - c), me).wait_recv()
        for cp in first + passed:
            cp.wait_send()
        mine.wait()
        total = out_ref[pl.ds(0, m_per), :]
        for d in range(1, N_DEV):
            total = total + out_ref[pl.ds(d * m_per, m_per), :]
        sum_ref[...] = total

    vmem = pl.BlockSpec(memory_space=pltpu.VMEM)
    return pl.pallas_call(
        body, name=name, in_specs=[vmem], out_specs=[vmem, vmem],
        out_shape=[jax.ShapeDtypeStruct((N_DEV * m_per, n), F32), jax.ShapeDtypeStruct((m_per, n), F32)],
        scratch_shapes=[pltpu.SemaphoreType.DMA((7,)), pltpu.SemaphoreType.DMA((7,)), pltpu.SemaphoreType.DMA],
        compiler_params=pltpu.CompilerParams(vmem_limit_bytes=V7X_VMEM_LIMIT),
    )(block)


WEIGHTS = ("rel_bias", "norm1_g", "w_in", "sgu_w", "sgu_b", "dil_qn_g", "dil_kn_g", "conv_w", "conv_b", "conv_ln_g", "conv_ln_b",
           "gqa_qn_g", "gqa_kn_g", "mix_norm_g", "w_out", "norm2_g", "w_gate", "w_up", "w_down")
SHARDED = ("w_in", "w_out", "w_gate", "w_up", "w_down")
COLUMN_SHARDED = ("w_in", "w_gate", "w_up")
REPLICATED = tuple(k for k in WEIGHTS if k not in SHARDED and k != "conv_w")


PACK_ROWS = 256


def _pack(parts):
    flat = jnp.concatenate([p.reshape(-1) for p in parts])
    pad = (-flat.shape[0]) % (PACK_ROWS * LANES)
    return jnp.pad(flat, (0, pad)).reshape(-1, LANES)


def _unpack(buf, shapes):
    flat = buf.reshape(-1)
    out, at = [], 0
    for s in shapes:
        size = math.prod(s)
        out.append(flat[at:at + size].reshape(s))
        at += size
    return out


RS_TM = (256, 128, 64, 32, 16)


def _add_own_seven(own, land, place, name):
    _, h, cols = land.shape
    tm = _pick(h, RS_TM)
    nb = h // tm

    def body(place_ref, own_ref, *refs):
        total = own_ref[...].astype(F32)
        for l_ref in refs[:-1]:
            total = total + l_ref[...].astype(F32)
        refs[-1][...] = total

    slot = lambda r: pl.BlockSpec((None, tm, cols), functools.partial(lambda i, p, r: (jnp.bitwise_xor(p[2], r), i, 0), r=r))
    grid_spec = pltpu.PrefetchScalarGridSpec(
        num_scalar_prefetch=1, grid=(nb,),
        in_specs=[pl.BlockSpec((None, tm, cols), lambda i, p: (p[0], p[1] * nb + i, 0))] + [slot(r) for r in range(1, N_DEV)],
        out_specs=pl.BlockSpec((tm, cols), lambda i, p: (i, 0)))
    return pl.pallas_call(body, name=name, grid_spec=grid_spec, out_shape=jax.ShapeDtypeStruct((h, cols), F32),
                          compiler_params=_cparams(("parallel",)))(place, own, *[land] * (N_DEV - 1))


def _reduce_start(grads, after, tag):
    handle, token = _chips_start(grads, True, after, tag + "start")
    return (handle, grads), token


def _reduce_finish(started, place, after, tag):
    handle, grads = started
    lands = _chips_wait(handle, True, after, tag + "wait")
    totals = [_add_own_seven(g, land, place, f"{tag}sum_{k}") for k, (g, land) in enumerate(zip(grads, lands))]
    return list(zip(totals, _swap_sibling(totals, tag + "share")))


def _adamw_shard(w, m, v, halves, c1, name):
    depth, rows, cols = w.shape
    h = rows // 2
    tm = _pick(h, RS_TM)
    nb = h // tm
    sources = [a for pair in halves for a in pair]

    def body(c_ref, w_ref, m_ref, v_ref, *refs):
        g_refs, (g_out, d_out, m_out, v_out) = refs[:2 * depth], refs[2 * depth:]
        layer, mine = pl.program_id(0), pl.program_id(1) == c_ref[0]
        g = None
        for l in range(depth):
            g_l = jnp.where(mine, g_refs[2 * l][...], g_refs[2 * l + 1][...])
            g = g_l if g is None else jnp.where(layer == l, g_l, g)
        delta, m_new, v_new = _adamw_fn(w_ref[...], g, m_ref[...], v_ref[...])
        g_out[...], d_out[...], m_out[...], v_out[...] = g, delta, m_new, v_new

    def source_spec(l, own):
        def index(layer, half, i, c_ref):
            return (jnp.where((layer == l) & ((half == c_ref[0]) == own), i, 0), 0)
        return pl.BlockSpec((tm, cols), index)

    blk = pl.BlockSpec((None, tm, cols), lambda layer, half, i, c_ref: (layer, half * nb + i, 0))
    grid_spec = pltpu.PrefetchScalarGridSpec(
        num_scalar_prefetch=1, grid=(depth, 2, nb),
        in_specs=[blk, blk, blk] + [source_spec(l, own) for l in range(depth) for own in (True, False)], out_specs=[blk] * 4)
    return pl.pallas_call(body, name=name, grid_spec=grid_spec, out_shape=[jax.ShapeDtypeStruct(w.shape, F32)] * 4,
                          compiler_params=_cparams(("arbitrary", "arbitrary", "arbitrary")))(c1, w, m, v, *sources)


GATHER_GROUPS = ((("w_in",), 0), (("w_out",), None), (("w_gate", "w_up"), None), (("w_down",), None), (("w_in",), 1))
REDUCE_GROUPS = (("w_down", "w_gate", "w_up"), ("w_out",), ("w_in",))


def kernel(x, rel_bias, norm1_g, w_in, sgu_w, sgu_b, dil_qn_g, dil_kn_g, conv_w, conv_b, conv_ln_g, conv_ln_b, gqa_qn_g, gqa_kn_g, mix_norm_g, w_out, norm2_g, w_gate, w_up, w_down, loss_target, m_rel_bias, m_norm1_g, m_w_in, m_sgu_w, m_sgu_b, m_dil_qn_g, m_dil_kn_g, m_conv_w, m_conv_b, m_conv_ln_g, m_conv_ln_b, m_gqa_qn_g, m_gqa_kn_g, m_mix_norm_g, m_w_out, m_norm2_g, m_w_gate, m_w_up, m_w_down, v_rel_bias, v_norm1_g, v_w_in, v_sgu_w, v_sgu_b, v_dil_qn_g, v_dil_kn_g, v_conv_w, v_conv_b, v_conv_ln_g, v_conv_ln_b, v_gqa_qn_g, v_gqa_kn_g, v_mix_norm_g, v_w_out, v_norm2_g, v_w_gate, v_w_up, v_w_down):
    w = dict(rel_bias=rel_bias, norm1_g=norm1_g, w_in=w_in, sgu_w=sgu_w, sgu_b=sgu_b, dil_qn_g=dil_qn_g, dil_kn_g=dil_kn_g,
             conv_w=conv_w, conv_b=conv_b, conv_ln_g=conv_ln_g, conv_ln_b=conv_ln_b, gqa_qn_g=gqa_qn_g, gqa_kn_g=gqa_kn_g,
             mix_norm_g=mix_norm_g, w_out=w_out, norm2_g=norm2_g, w_gate=w_gate, w_up=w_up, w_down=w_down)
    m = dict(rel_bias=m_rel_bias, norm1_g=m_norm1_g, w_in=m_w_in, sgu_w=m_sgu_w, sgu_b=m_sgu_b, dil_qn_g=m_dil_qn_g,
             dil_kn_g=m_dil_kn_g, conv_w=m_conv_w, conv_b=m_conv_b, conv_ln_g=m_conv_ln_g, conv_ln_b=m_conv_ln_b,
             gqa_qn_g=m_gqa_qn_g, gqa_kn_g=m_gqa_kn_g, mix_norm_g=m_mix_norm_g, w_out=m_w_out, norm2_g=m_norm2_g,
             w_gate=m_w_gate, w_up=m_w_up, w_down=m_w_down)
    v = dict(rel_bias=v_rel_bias, norm1_g=v_norm1_g, w_in=v_w_in, sgu_w=v_sgu_w, sgu_b=v_sgu_b, dil_qn_g=v_dil_qn_g,
             dil_kn_g=v_dil_kn_g, conv_w=v_conv_w, conv_b=v_conv_b, conv_ln_g=v_conv_ln_g, conv_ln_b=v_conv_ln_b,
             gqa_qn_g=v_gqa_qn_g, gqa_kn_g=v_gqa_kn_g, mix_norm_g=v_mix_norm_g, w_out=v_w_out, norm2_g=v_norm2_g,
             w_gate=v_w_gate, w_up=v_w_up, w_down=v_w_down)
    bsz = x.shape[0]
    t = bsz * SEQ
    xi, yi, ci = _mesh_pos()
    chip = 2 * xi + yi
    conv_cols = conv_w.shape[-1]

    conv_rows = DEPTH * CONV_WIDTH
    conv_block = jnp.pad(conv_w.reshape(conv_rows, conv_cols), ((0, (-conv_rows) % 8), (0, 0)))
    every, _ = _allgather_sum_small(conv_block, "conv_w_gather")
    every = every.reshape(N_DEV, conv_block.shape[0], conv_cols)
    conv_w_full = jnp.concatenate([every[2 * j, :conv_rows].reshape(DEPTH, CONV_WIDTH, conv_cols) for j in range(N_CHIPS)], axis=-1)

    c1 = jnp.reshape(ci, (1,)).astype(jnp.int32)
    place = jnp.stack([chip, ci, 4 * xi + 2 * yi + ci]).astype(jnp.int32)

    def own_part(k, layer):
        if layer is None:
            return lax.dynamic_index_in_dim(w[k], ci, axis=0, keepdims=False).astype(BF16)
        half = w[k].shape[1] // 2
        return lax.dynamic_slice_in_dim(w[k][layer], ci * half, half, axis=0).astype(BF16)

    fetches, token = [], every
    for gi, (group, layer) in enumerate(GATHER_GROUPS):
        parts = [own_part(k, layer) for k in group]
        handle, token = _chips_start(parts, False, token, f"gather{gi}_start", per_core=True)
        fetches.append((handle, parts))
    all_started = token
    gathered, handed = {}, {}

    def group_of(l, name):
        return [name in group and layer in (None, l) for group, layer in GATHER_GROUPS].index(True)

    def landed(gi, after):
        handle, parts = fetches[gi]
        lands = _chips_wait(handle, False, all_started if after is None else after, f"gather{gi}_wait")
        return [lax.dynamic_update_slice(land, own[None, None], (ci, chip, 0, 0)) for land, own in zip(lands, parts)]

    def prefetch(l, name, after):
        if l < DEPTH and group_of(l, name) not in handed:
            gi = group_of(l, name)
            handed[gi], token = _pairs_start(landed(gi, after), after, f"gather{gi}_share_start")
            return token
        return None

    def big(l, name, after):
        if (l, name) not in gathered:
            gi = group_of(l, name)
            group, layer = GATHER_GROUPS[gi]
            if gi in handed:
                whole = _pairs_wait(handed[gi], after, f"gather{gi}_share_wait")
            else:
                whole = _complete_pairs(landed(gi, after), f"gather{gi}_share")
            for k, g in zip(group, whole):
                rows, cols = g.shape[2:]
                if layer is not None:
                    gathered[layer, k] = g.transpose(0, 2, 1, 3).reshape(2 * rows, N_CHIPS * cols)
                    continue
                for each in range(DEPTH):
                    gathered[each, k] = (g[each].transpose(1, 0, 2).reshape(rows, N_CHIPS * cols) if k in COLUMN_SHARDED
                                         else g[each].reshape(N_CHIPS * rows, cols))
        return gathered[l, name]

    big(0, "w_in", None)

    pending, started, reduced = {}, {}, {}

    def emit(l, name, g):
        pending[l, name] = g
        for gi, group in enumerate(REDUCE_GROUPS):
            if name in group and all((l, k) in pending for k in group):
                started[l, gi], token = _reduce_start([pending[l, k] for k in group], g, f"l{l}_reduce{gi}_")
                return token
        return None

    def finish(l, after):
        for gi, group in enumerate(REDUCE_GROUPS):
            for k, r in zip(group, _reduce_finish(started[l, gi], place, after, f"l{l}_reduce{gi}_")):
                reduced[l, k] = r
            after = reduced[l, group[0]][1]

    def mid_hook(l, a):
        if l + 1 < DEPTH:
            finish(l + 1, a)

    small = {k: w[k] for k in REPLICATED}
    small["conv_w"] = conv_w_full
    loss, dx, small_grads = _local_step(x.reshape(t, D_MODEL), loss_target.reshape(t, D_MODEL), small, big, emit, mid_hook, bsz,
                                        prefetch)
    loss = lax.psum(loss[0, 0], ("x", "y", "c"))

    names = REPLICATED + ("conv_w",)
    shapes = [small_grads[k].shape for k in names]
    packed_grads = _pack([small_grads[k] for k in names])
    eighths = packed_grads.reshape(N_CHIPS, packed_grads.shape[0] // N_CHIPS, LANES)
    small_reduce, _ = _reduce_start([eighths], dx, "small_grads_")
    finish(0, dx)

    grads, deltas, new_m, new_v = {}, {}, {}, {}
    for k in SHARDED:
        grads[k], deltas[k], new_m[k], new_v[k] = _adamw_shard(w[k], m[k], v[k], [reduced[l, k] for l in range(DEPTH)], c1, "adamw_" + k)

    landed = _chips_wait(small_reduce[0], True, new_v[SHARDED[-1]], "small_grads_wait")[0]
    mine = _add_own_seven(eighths, landed, place, "small_grads_sum")
    own, others = _everyone_wait(_everyone_start(mine, mine, "small_grads_spread"), mine, "small_grads_spread_wait")
    summed = lax.dynamic_update_slice_in_dim(others, own[None], 4 * xi + 2 * yi + ci, axis=0).reshape(packed_grads.shape)
    summed_parts = dict(zip(names, _unpack(summed, shapes)))
    rep_shapes = [w[k].shape for k in REPLICATED]
    packed = [_pack([src[k] for k in REPLICATED]) for src in (w, {k: summed_parts[k] for k in REPLICATED}, m, v)]
    d_p, m_p, v_p = _adamw(*packed, "adamw_replicated")
    for k, gk, dk, mk, vk in zip(REPLICATED, _unpack(packed[1], rep_shapes), _unpack(d_p, rep_shapes), _unpack(m_p, rep_shapes),
                                 _unpack(v_p, rep_shapes)):
        grads[k], deltas[k], new_m[k], new_v[k] = gk, dk, mk, vk
    g_conv = lax.dynamic_slice_in_dim(summed_parts["conv_w"], chip * conv_cols, conv_cols, axis=2)
    packed = [_pack([a]) for a in (conv_w, g_conv, m["conv_w"], v["conv_w"])]
    d_p, m_p, v_p = _adamw(*packed, "adamw_conv_w")
    grads["conv_w"] = g_conv
    deltas["conv_w"], new_m["conv_w"], new_v["conv_w"] = (_unpack(a, [conv_w.shape])[0] for a in (d_p, m_p, v_p))

    return (loss, dx.reshape(x.shape), *[grads[k] for k in WEIGHTS], *[deltas[k] for k in WEIGHTS],
            *[new_m[k] for k in WEIGHTS], *[new_v[k] for k in WEIGHTS])
```

```python
import functools
import math

import numpy as np
import jax
import jax.numpy as jnp
from jax import lax
from jax.experimental import pallas as pl
from jax.experimental.pallas import tpu as pltpu

F32 = jnp.float32
BF16 = jnp.bfloat16

D_MODEL = 2048
SEQ = 2048
DEPTH = 2
HEAD_DIM = 64
GROUP_WIDTH = 512
N_HEADS = 8
SGU_CHUNK = 128
DIL_PATTERNS = ((128, 1), (512, 4), (2048, 16))
DIL_HALF = 64
CONV_WIDTH = 31
KV_WIDTH = 128
GRID_W = 64
ROPE_THETA = 10000.0
REL_BUCKETS = 32
REL_MAX_DIST = 1024
FFN_HIDDEN = 5632
IN_WIDTH = 4352
RMS_EPS = 1e-6
LN_EPS = 1e-5
ADAM_LR = 0.001
ADAM_B1 = 0.9
ADAM_B2 = 0.999
ADAM_EPS = 1e-08
ADAM_WD = 0.01
ADAM_STEP = 10
N_CHIPS = 4

V7X_VMEM_LIMIT = 56 * 1024 * 1024
MATMUL_VMEM_BUDGET = 48 * 1024 * 1024
LANES = 128
HI = lax.Precision.HIGHEST
SPLIT3 = lax.Precision.HIGH
MESH = pl.DeviceIdType.MESH


def _cparams(sem=None):
    return pltpu.CompilerParams(dimension_semantics=sem, vmem_limit_bytes=V7X_VMEM_LIMIT)


def _pick(n, cands):
    for c in cands:
        if n % c == 0:
            return c
    raise ValueError(f"no tile for {n}")


_DIMS = {"nn": (((1,), (0,)), ((), ())), "nt": (((1,), (1,)), ((), ())), "tn": (((0,), (0,)), ((), ()))}


def _matmul(pairs, mode, out_dtype, name, residual=None, slabs=1):
    a0, b0 = pairs[0]
    b3 = b0.ndim == 3
    if mode == "nn":
        (M, K), N = a0.shape, b0.shape[1]
    elif mode == "nt":
        (M, K), N = a0.shape, b0.shape[0]
    else:
        (K, M) = a0.shape
        N = b0.shape[-1] if b3 else b0.shape[1] // slabs
    npairs = len(pairs)
    a_bytes, b_bytes, o_bytes = a0.dtype.itemsize, b0.dtype.itemsize, jnp.dtype(out_dtype).itemsize
    per_out = 4 + 2 * o_bytes + (8 if residual is not None else 0)
    tn_cands = [c for c in ((1024, 512) if K <= 2048 else (512,)) + (1408, 2176, 256) if N % c == 0] + [N]
    tm, tn = next((tm, tn) for tn in tn_cands for tm in (1024, 1408, 512, 256)
                  if M % tm == 0 and 2 * npairs * K * (tm * a_bytes + tn * b_bytes) + tm * tn * per_out <= MATMUL_VMEM_BUDGET)
    tk = K
    ni, nj = M // tm, N // tn
    j_outer = nj * M * a_bytes + N * b_bytes < M * a_bytes + ni * N * b_bytes
    grid = (slabs, nj, ni) if j_outer else (slabs, ni, nj)
    at = lambda f: (lambda s, g1, g2: f(s, g2, g1)) if j_outer else f

    if mode in ("nn", "nt"):
        a_spec = pl.BlockSpec((tm, tk), at(lambda s, i, j: (i, 0)))
    else:
        a_spec = pl.BlockSpec((tk, tm), at(lambda s, i, j: (0, i)))
    if mode == "nt":
        b_spec = pl.BlockSpec((tn, tk), at(lambda s, i, j: (j, 0)))
    elif b3:
        b_spec = pl.BlockSpec((None, tk, tn), at(lambda s, i, j: (s, 0, j)))
    else:
        b_spec = pl.BlockSpec((tk, tn), at(lambda s, i, j: (0, s * nj + j)))
    if slabs > 1:
        o_spec = pl.BlockSpec((None, tm, tn), at(lambda s, i, j: (s, i, j)))
        o_shape = jax.ShapeDtypeStruct((slabs, M, N), out_dtype)
    else:
        o_spec = pl.BlockSpec((tm, tn), at(lambda s, i, j: (i, j)))
        o_shape = jax.ShapeDtypeStruct((M, N), out_dtype)
    in_specs = [a_spec] * npairs + [b_spec] * npairs
    args = [a for a, _ in pairs] + [b for _, b in pairs]
    if residual is not None:
        in_specs.append(pl.BlockSpec((tm, tn), at(lambda s, i, j: (i, j))))
        args.append(residual)
    dims = _DIMS[mode]

    def body(*refs):
        a_refs, b_refs = refs[:npairs], refs[npairs:2 * npairs]
        res_ref = refs[2 * npairs] if residual is not None else None
        o_ref = refs[-1]
        r = None
        for a_ref, b_ref in zip(a_refs, b_refs):
            d = lax.dot_general(a_ref[...].astype(BF16), b_ref[...].astype(BF16), dims, preferred_element_type=F32)
            r = d if r is None else r + d
        if res_ref is not None:
            r = r + res_ref[...]
        o_ref[...] = r.astype(out_dtype)

    return pl.pallas_call(
        body, name=name, grid=grid, in_specs=in_specs, out_specs=o_spec, out_shape=o_shape,
        compiler_params=_cparams(("parallel", "parallel", "parallel")),
    )(*args)


class Strided:
    def __init__(self, r):
        self.r = r


def _rowmap(fn, rows, fulls, row_outs, acc_outs, name, tm, n_rows):
    nr, nf, nro = len(rows), len(fulls), len(row_outs)
    rows = [r if len(r) == 4 else (*r, n_rows // tm) for r in rows]
    row_outs = [o if len(o) == 3 else (*o, None) for o in row_outs]
    in_specs = [pl.BlockSpec((tm // per.r, per.r * w), lambda i: (i, 0)) if isinstance(per, Strided) else
                pl.BlockSpec((tm, w), functools.partial(lambda i, cb, per: (i % per, cb), cb=cb, per=per)) for _, w, cb, per in rows]
    in_specs += [pl.BlockSpec(f.shape, lambda i: (0,) * f.ndim) for f in fulls]
    out_specs = [pl.BlockSpec((tm, w) if st is None else (tm // st.r, st.r * w), lambda i: (i, 0)) for w, _, st in row_outs]
    out_specs += [pl.BlockSpec(s, functools.partial(lambda i, n: (0,) * n, n=len(s))) for s in acc_outs]
    out_shape = [jax.ShapeDtypeStruct((n_rows, w) if st is None else (n_rows // st.r, st.r * w), dt) for w, dt, st in row_outs]
    out_shape += [jax.ShapeDtypeStruct(s, F32) for s in acc_outs]
    strided = [(k, w, per.r) for k, (_, w, _, per) in enumerate(rows) if isinstance(per, Strided)]
    strided += [(nr + nf + k, w, st.r) for k, (w, _, st) in enumerate(row_outs) if st is not None]
    n_scratch = len(strided)

    def body(*refs):
        refs, scratch = refs[:len(refs) - n_scratch], dict(zip([k for k, _, _ in strided], refs[len(refs) - n_scratch:]))
        ins = []
        for k, ref in enumerate(refs[:nr + nf]):
            if k in scratch:
                w, r, scr = rows[k][1], rows[k][3].r, scratch[k]
                for rho in range(r):
                    for j in range(w // LANES):
                        scr.at[j][pl.ds(rho, tm // r, stride=r), :] = ref[:, pl.ds(rho * w + j * LANES, LANES)]
                ins.append(jnp.concatenate([scr[j] for j in range(w // LANES)], axis=1))
            else:
                ins.append(ref[...])
        outs = fn(*ins)
        o_refs = refs[nr + nf:]
        for k, (o_ref, val) in enumerate(zip(o_refs[:nro], outs[:nro])):
            if nr + nf + k in scratch:
                w, r, scr = row_outs[k][0], row_outs[k][2].r, scratch[nr + nf + k]
                val = val.astype(F32)
                for j in range(w // LANES):
                    scr[j] = val[:, j * LANES:(j + 1) * LANES]
                for rho in range(r):
                    for j in range(w // LANES):
                        o_ref[:, pl.ds(rho * w + j * LANES, LANES)] = scr.at[j][pl.ds(rho, tm // r, stride=r), :].astype(o_ref.dtype)
            else:
                o_ref[...] = val.astype(o_ref.dtype)
        if acc_outs:
            first = pl.program_id(0) == 0
            for o_ref, val in zip(o_refs[nro:], outs[nro:]):
                @pl.when(first)
                def _(o_ref=o_ref, val=val):
                    o_ref[...] = val

                @pl.when(jnp.logical_not(first))
                def _(o_ref=o_ref, val=val):
                    o_ref[...] += val

    res = pl.pallas_call(
        body, name=name, grid=(n_rows // tm,), in_specs=in_specs, out_specs=out_specs, out_shape=out_shape,
        scratch_shapes=[pltpu.VMEM((w // LANES, tm, LANES), F32) for _, w, _ in strided],
        compiler_params=_cparams(("arbitrary",) if acc_outs else ("parallel",)),
    )(*[r[0] for r in rows], *fulls)
    return res


def _rms(x, g):
    return x * lax.rsqrt(jnp.mean(x * x, axis=-1, keepdims=True) + RMS_EPS) * g


def _rmsnorm_fwd(x, g, name):
    t = x.shape[0]
    return _rowmap(lambda xv, gv: (_rms(xv, gv),), [(x, D_MODEL, 0)], [g], [(D_MODEL, BF16)], [], name, 512, t)[0]


def _rmsnorm_bwd(dh, x, g, dres, name, follow=None):
    t = x.shape[0]

    def fn(dhv, xv, drv, gv, *_):
        _, vjp = jax.vjp(_rms, xv, gv)
        dx, dg = vjp(dhv)
        return dx + drv, dx + drv, dg

    fulls = [g] if follow is None else [g, follow]
    return _rowmap(fn, [(dh, D_MODEL, 0), (x, D_MODEL, 0), (dres, D_MODEL, 0)], fulls, [(D_MODEL, F32), (D_MODEL, BF16)],
                   [(1, D_MODEL)], name, 256, t)


def _loss_fwd_bwd(y, target, name):
    t = y.shape[0]

    def fn(yv, tv):
        e = yv - tv
        return e * (1.0 / D_MODEL), e * (1.0 / D_MODEL), (0.5 / D_MODEL) * jnp.sum(e * e, keepdims=True)

    return _rowmap(fn, [(y, D_MODEL, 0), (target, D_MODEL, 0)], [], [(D_MODEL, F32), (D_MODEL, BF16)], [(1, 1)], name, 512, t)


FFN_TILE = (1024, 512)


def _ffn_up(h, wg, wu, name):
    t, n = h.shape[0], wg.shape[1]
    tm, tn = FFN_TILE

    def body(h_ref, wg_ref, wu_ref, act_ref, du_ref, dg_ref):
        hv = h_ref[...]
        g = jnp.dot(hv, wg_ref[...], preferred_element_type=F32)
        u = jnp.dot(hv, wu_ref[...], preferred_element_type=F32)
        sg = jax.nn.sigmoid(g)
        silu = g * sg
        act_ref[...] = (silu * u).astype(BF16)
        du_ref[...] = silu.astype(BF16)
        dg_ref[...] = (u * (sg + silu * (1.0 - sg))).astype(BF16)

    o_spec = pl.BlockSpec((tm, tn), lambda i, j: (i, j))
    o_shape = jax.ShapeDtypeStruct((t, n), BF16)
    return pl.pallas_call(
        body, name=name, grid=(t // tm, n // tn),
        in_specs=[pl.BlockSpec((tm, D_MODEL), lambda i, j: (i, 0)), pl.BlockSpec((D_MODEL, tn), lambda i, j: (0, j)),
                  pl.BlockSpec((D_MODEL, tn), lambda i, j: (0, j))],
        out_specs=[o_spec] * 3, out_shape=[o_shape] * 3, compiler_params=_cparams(("parallel", "parallel")),
    )(h, wg, wu)


def _ffn_down_bwd(dy, wd, act_du, act_dg, name):
    t, n = dy.shape[0], wd.shape[0]
    tm, tn = FFN_TILE

    def body(dy_ref, wd_ref, adu_ref, adg_ref, dg_ref, du_ref):
        dact = lax.dot_general(dy_ref[...].astype(BF16), wd_ref[...], _DIMS["nt"], preferred_element_type=F32)
        du_ref[...] = (dact * adu_ref[...].astype(F32)).astype(BF16)
        dg_ref[...] = (dact * adg_ref[...].astype(F32)).astype(BF16)

    o_spec = pl.BlockSpec((tm, tn), lambda i, j: (i, j))
    o_shape = jax.ShapeDtypeStruct((t, n), BF16)
    return pl.pallas_call(
        body, name=name, grid=(t // tm, n // tn),
        in_specs=[pl.BlockSpec((tm, D_MODEL), lambda i, j: (i, 0)), pl.BlockSpec((tn, D_MODEL), lambda i, j: (j, 0)),
                  o_spec, o_spec],
        out_specs=[o_spec] * 2, out_shape=[o_shape] * 2, compiler_params=_cparams(("parallel", "parallel")),
    )(dy, wd, act_du, act_dg)


def _np_group_avg(width, group=HEAD_DIM):
    i = np.arange(width)
    return ((i[:, None] // group) == (i[None, :] // group)).astype(np.float32) / group


def _np_tile_fold(width, group=HEAD_DIM):
    return ((np.arange(width)[:, None] % group) == np.arange(group)[None, :]).astype(np.float32)


def _np_group_fold(width, group=HEAD_DIM, pad=128):
    return ((np.arange(width)[:, None] // group) == np.arange(pad)[None, :]).astype(np.float32)


def _np_rope_partner(width):
    i = np.arange(width)
    partner = np.where(i % 32 < 16, i + 16, i - 16)
    return (partner[:, None] == i[None, :]).astype(np.float32)


def _np_kv_expand():
    src = np.arange(KV_WIDTH)
    dst = np.arange(GROUP_WIDTH)
    return ((src[:, None] // HEAD_DIM == dst[None, :] // (4 * HEAD_DIM)) & (src[:, None] % HEAD_DIM == dst[None, :] % HEAD_DIM)).astype(np.float32)


def _np_rope_tables(n_heads):
    t = np.arange(SEQ)
    pos = {0: (t // GRID_W).astype(np.float32), 1: (t % GRID_W).astype(np.float32)}
    freqs = (ROPE_THETA ** (-np.arange(16, dtype=np.float32) / 16)).astype(np.float32)
    cos_parts, sin_parts = [], []
    for axis in (0, 1):
        ang = pos[axis][:, None] * freqs[None, :]
        c, s = np.cos(ang).astype(np.float32), np.sin(ang).astype(np.float32)
        cos_parts += [c, c]
        sin_parts += [-s, s]
    cos = np.concatenate(cos_parts, axis=1)
    sin = np.concatenate(sin_parts, axis=1)
    return np.tile(cos, (1, n_heads)), np.tile(sin, (1, n_heads))


def _np_t5_buckets(rel):
    nb = REL_BUCKETS // 2
    max_exact = nb // 2
    ret = np.where(rel > 0, nb, 0)
    n = np.abs(rel)
    nf = np.maximum(n, 1).astype(np.float32)
    large = max_exact + (np.log(nf / max_exact) / math.log(REL_MAX_DIST / max_exact) * (nb - max_exact)).astype(np.int32)
    large = np.minimum(large, nb - 1)
    return (ret + np.where(n < max_exact, n, large)).astype(np.int32)


DIL_QB = 128
DIL_WIN = DIL_QB + 2 * DIL_HALF


def _np_dil_buckets(dil):
    off = np.arange(DIL_WIN)[None, :] - DIL_HALF - np.arange(DIL_QB)[:, None]
    return _np_t5_buckets(off * dil)


def _dil_live_buckets(dil):
    off = np.arange(-DIL_HALF, DIL_HALF + 1)
    return sorted(set(_np_t5_buckets(off * dil).tolist()))


def _head_stat(x, mavg):
    return jnp.dot(x, mavg, precision=SPLIT3, preferred_element_type=F32)


def _gelu(x):
    return 0.5 * x * (1.0 + jnp.tanh(math.sqrt(2.0 / math.pi) * (x + 0.044715 * (x * x * x))))


def _sgu_pre(u_pre, v_pre, mavg):
    v = _gelu(v_pre)
    xc = v - _head_stat(v, mavg)
    vn = xc * lax.rsqrt(_head_stat(xc * xc, mavg) + LN_EPS)
    return _gelu(u_pre), vn


def _sgu_mix(w_ref, vnb, bm):
    lane_group = lax.broadcasted_iota(jnp.int32, (1, GROUP_WIDTH), 1) // HEAD_DIM
    mixed = bm
    for g in range(N_HEADS):
        r = jnp.dot(w_ref[g], vnb, preferred_element_type=F32)
        mixed = mixed + jnp.where(lane_group == g, r, 0.0)
    return mixed


SGU_TM = 512


def _sgu_fwd(z, w_s, bm, name):
    t = z.shape[0]
    mavg = jnp.asarray(_np_group_avg(GROUP_WIDTH))

    def body(u_ref, v_ref, w_ref, bm_ref, mavg_ref, y_ref):
        for c in range(SGU_TM // SGU_CHUNK):
            rows = pl.ds(c * SGU_CHUNK, SGU_CHUNK)
            u, vn = _sgu_pre(u_ref[rows, :], v_ref[rows, :], mavg_ref[...])
            y_ref[rows, :] = u * _sgu_mix(w_ref, vn.astype(BF16), bm_ref[...])

    full = lambda a: pl.BlockSpec(a.shape, lambda i: (0,) * a.ndim)
    return pl.pallas_call(
        body, name=name, grid=(t // SGU_TM,),
        in_specs=[pl.BlockSpec((SGU_TM, GROUP_WIDTH), lambda i: (i, 0)), pl.BlockSpec((SGU_TM, GROUP_WIDTH), lambda i: (i, 1)),
                  full(w_s), full(bm), full(mavg)],
        out_specs=pl.BlockSpec((SGU_TM, GROUP_WIDTH), lambda i: (i, 0)),
        out_shape=jax.ShapeDtypeStruct((t, GROUP_WIDTH), F32), compiler_params=_cparams(("parallel",)),
    )(z, z, w_s, bm, mavg)


def _sgu_bwd(z, dy, w_s, w_s_t, bm, name):
    t = z.shape[0]
    mavg = jnp.asarray(_np_group_avg(GROUP_WIDTH))
    gfold = jnp.asarray(_np_group_fold(GROUP_WIDTH))

    def body(u_ref, v_ref, dy_ref, w_ref, wt_ref, bm_ref, mavg_ref, gfold_ref, du_ref, dv_ref, dw_ref, dbs_ref, dbm_ref):
        @pl.when(pl.program_id(0) == 0)
        def _():
            dw_ref[...] = jnp.zeros_like(dw_ref)
            dbm_ref[...] = jnp.zeros_like(dbm_ref)

        lane_group = lax.broadcasted_iota(jnp.int32, (1, GROUP_WIDTH), 1) // HEAD_DIM
        for c in range(SGU_TM // SGU_CHUNK):
            rows = pl.ds(c * SGU_CHUNK, SGU_CHUNK)
            (u, vn), pre_vjp = jax.vjp(functools.partial(_sgu_pre, mavg=mavg_ref[...]), u_ref[rows, :], v_ref[rows, :])
            vnb = vn.astype(BF16)
            mixed = _sgu_mix(w_ref, vnb, bm_ref[...])
            dyv = dy_ref[rows, :]
            dmixed = dyv * u
            dbm_ref[...] += dmixed
            dvn = jnp.zeros_like(vn)
            for g in range(N_HEADS):
                dm_g = jnp.where(lane_group == g, dmixed, 0.0).astype(BF16)
                dw_ref[g] += lax.dot_general(dm_g, vnb, _DIMS["nt"], preferred_element_type=F32)
                dvn = dvn + jnp.dot(wt_ref[g], dm_g, preferred_element_type=F32)
            du_pre, dv_pre = pre_vjp((dyv * mixed, dvn))
            du_ref[rows, :] = du_pre
            dv_ref[rows, :] = dv_pre

        @pl.when(pl.program_id(0) == t // SGU_TM - 1)
        def _():
            dbs_ref[...] = jnp.dot(dbm_ref[...], gfold_ref[...], precision=HI, preferred_element_type=F32)

    full = lambda a: pl.BlockSpec(a.shape, lambda i: (0,) * a.ndim)
    row = pl.BlockSpec((SGU_TM, GROUP_WIDTH), lambda i: (i, 0))
    return pl.pallas_call(
        body, name=name, grid=(t // SGU_TM,),
        in_specs=[row, pl.BlockSpec((SGU_TM, GROUP_WIDTH), lambda i: (i, 1)), row, full(w_s), full(w_s_t), full(bm), full(mavg),
                  full(gfold)],
        out_specs=[row, row, pl.BlockSpec((N_HEADS, SGU_CHUNK, SGU_CHUNK), lambda i: (0, 0, 0)),
                   pl.BlockSpec((SGU_CHUNK, 128), lambda i: (0, 0))],
        out_shape=[jax.ShapeDtypeStruct((t, GROUP_WIDTH), F32)] * 2 + [jax.ShapeDtypeStruct((N_HEADS, SGU_CHUNK, SGU_CHUNK), F32),
                                                                      jax.ShapeDtypeStruct((SGU_CHUNK, 128), F32)],
        scratch_shapes=[pltpu.VMEM((SGU_CHUNK, GROUP_WIDTH), F32)],
        compiler_params=_cparams(("arbitrary",)),
    )(z, z, dy, w_s, w_s_t, bm, mavg, gfold)


def _head_lanes():
    lane_head = lax.broadcasted_iota(jnp.int32, (1, 2 * HEAD_DIM), 1) // HEAD_DIM
    return lane_head == 0, lane_head == 1


def _stack_heads(x2):
    h0, h1 = _head_lanes()
    zero = jnp.zeros_like(x2)
    return jnp.concatenate([jnp.where(h0, x2, zero), jnp.where(h1, x2, zero)], axis=0)


def _unstack_heads(y):
    r = y.shape[0] // 2
    h0, _ = _head_lanes()
    return jnp.where(h0, y[:r], y[r:])


def _pair_softmax(qs, k2, biases, valid):
    s = lax.dot_general(qs, k2, _DIMS["nt"], preferred_element_type=F32)
    if biases is not None:
        s = s + jnp.concatenate(biases, axis=0)
    if valid is not None:
        s = jnp.where(jnp.concatenate([valid, valid], axis=0), s, -1e30)
    m = jnp.max(s, axis=-1, keepdims=True)
    e = jnp.exp(s - m)
    l = jnp.sum(e, axis=-1, keepdims=True)
    return e / l, m + jnp.log(l)


def _attn_pair_fwd(q2, k2, v2, biases, valid):
    p, lse = _pair_softmax(_stack_heads(q2), k2, biases, valid)
    o = jnp.dot(p.astype(BF16), v2, preferred_element_type=F32)
    return _unstack_heads(o), _unstack_heads(jnp.broadcast_to(lse, o.shape))


def _attn_pair_bwd(q2, k2, v2, biases, valid, do2, dlse2):
    r = q2.shape[0]
    qs = _stack_heads(q2)
    p, _ = _pair_softmax(qs, k2, biases, valid)
    dos = _stack_heads(do2).astype(BF16)
    dp = lax.dot_general(dos, v2, _DIMS["nt"], preferred_element_type=F32)
    delta = jnp.sum(dp * p, axis=-1, keepdims=True)
    if dlse2 is not None:
        delta = delta - jnp.sum(_stack_heads(dlse2), axis=-1, keepdims=True)
    ds = p * (dp - delta)
    dsb = ds.astype(BF16)
    dq2 = _unstack_heads(jnp.dot(dsb, k2, preferred_element_type=F32))
    dk2 = lax.dot_general(dsb, qs, _DIMS["tn"], preferred_element_type=F32)
    dv2 = lax.dot_general(p.astype(BF16), dos, _DIMS["tn"], preferred_element_type=F32)
    return dq2, dk2, dv2, [ds[:r], ds[r:]]


def _dil_valid(r0, length):
    row = lax.broadcasted_iota(jnp.int32, (DIL_QB, DIL_WIN), 0)
    col = lax.broadcasted_iota(jnp.int32, (DIL_QB, DIL_WIN), 1)
    off = col - DIL_HALF - row
    kpos = r0 - DIL_HALF + col
    return (jnp.abs(off) <= DIL_HALF) & (kpos >= 0) & (kpos < length)


def _dil_build_bias(tab_ref, bkt_ref, bias_ref, dil):
    bkt = bkt_ref[...]
    for h in range(N_HEADS):
        acc = jnp.zeros((DIL_QB, DIL_WIN), F32)
        for b in _dil_live_buckets(dil):
            acc = jnp.where(bkt == b, tab_ref[b, h], acc)
        bias_ref[h] = acc


def _dil_fill_pad(pad_ref, src_ref, length):
    zeros = jnp.zeros((DIL_HALF, GROUP_WIDTH), pad_ref.dtype)
    pad_ref[pl.ds(0, DIL_HALF), :] = zeros
    pad_ref[pl.ds(DIL_HALF + length, DIL_HALF), :] = zeros
    pad_ref[pl.ds(DIL_HALF, length), :] = src_ref[...]


def _dil_specs(bsz, length, dil):
    view = lambda a: a.reshape(bsz, length, dil * GROUP_WIDTH)
    blk = pl.BlockSpec((None, DIL_QB, GROUP_WIDTH), lambda b, rho, i: (b, i, rho))
    seq = pl.BlockSpec((None, length, GROUP_WIDTH), lambda b, rho, i: (b, 0, rho))
    return view, blk, seq


def _dil_fwd(qb, kb, vb, table, dil, name):
    length = SEQ // dil
    bsz = qb.shape[0] // length
    bkt = jnp.asarray(_np_dil_buckets(dil))
    view, blk, seq = _dil_specs(bsz, length, dil)

    def body(tab_ref, bkt_ref, q_ref, k_ref, v_ref, o_ref, lse_ref, kpad, vpad, bias_ref):
        i = pl.program_id(2)

        @pl.when((pl.program_id(0) == 0) & (pl.program_id(1) == 0) & (i == 0))
        def _():
            _dil_build_bias(tab_ref, bkt_ref, bias_ref, dil)

        @pl.when(i == 0)
        def _():
            _dil_fill_pad(kpad, k_ref, length)
            _dil_fill_pad(vpad, v_ref, length)

        r0 = pl.multiple_of(i * DIL_QB, DIL_QB)
        valid = _dil_valid(r0, length)
        for m in range(N_HEADS // 2):
            lanes = pl.ds(m * 128, 128)
            o2, lse2 = _attn_pair_fwd(q_ref[:, lanes], kpad[pl.ds(r0, DIL_WIN), lanes], vpad[pl.ds(r0, DIL_WIN), lanes],
                                      (bias_ref[2 * m], bias_ref[2 * m + 1]), valid)
            o_ref[:, lanes] = o2
            lse_ref[:, lanes] = lse2

    out = jax.ShapeDtypeStruct((bsz, length, dil * GROUP_WIDTH), F32)
    o, lse = pl.pallas_call(
        body, name=name, grid=(bsz, dil, length // DIL_QB),
        in_specs=[pl.BlockSpec(memory_space=pltpu.SMEM), pl.BlockSpec(bkt.shape, lambda b, rho, i: (0, 0)), blk, seq, seq],
        out_specs=[blk, blk], out_shape=[out, out],
        scratch_shapes=[pltpu.VMEM((length + 2 * DIL_HALF, GROUP_WIDTH), BF16), pltpu.VMEM((length + 2 * DIL_HALF, GROUP_WIDTH), BF16),
                        pltpu.VMEM((N_HEADS, DIL_QB, DIL_WIN), F32)],
        compiler_params=_cparams(("arbitrary", "arbitrary", "arbitrary")),
    )(table, bkt, view(qb), view(kb), view(vb))
    return o.reshape(qb.shape), lse.reshape(qb.shape)


def _dil_bwd(qb, kb, vb, do, dlse, table, dil, name):
    length = SEQ // dil
    bsz = qb.shape[0] // length
    nqb = length // DIL_QB
    bkt = jnp.asarray(_np_dil_buckets(dil))
    view, blk, seq = _dil_specs(bsz, length, dil)

    def body(tab_ref, bkt_ref, q_ref, k_ref, v_ref, do_ref, dlse_ref, dq_ref, dk_ref, dv_ref, dsc_ref, kpad, vpad, bias_ref):
        i = pl.program_id(2)

        @pl.when((pl.program_id(0) == 0) & (pl.program_id(1) == 0) & (i == 0))
        def _():
            _dil_build_bias(tab_ref, bkt_ref, bias_ref, dil)
            dsc_ref[...] = jnp.zeros_like(dsc_ref)

        @pl.when(i == 0)
        def _():
            _dil_fill_pad(kpad, k_ref, length)
            _dil_fill_pad(vpad, v_ref, length)
            dk_ref[...] = jnp.zeros_like(dk_ref)
            dv_ref[...] = jnp.zeros_like(dv_ref)

        r0 = pl.multiple_of(i * DIL_QB, DIL_QB)
        valid = _dil_valid(r0, length)
        for m in range(N_HEADS // 2):
            lanes = pl.ds(m * 128, 128)
            dq2, dk2, dv2, ds_heads = _attn_pair_bwd(
                q_ref[:, lanes], kpad[pl.ds(r0, DIL_WIN), lanes], vpad[pl.ds(r0, DIL_WIN), lanes],
                (bias_ref[2 * m], bias_ref[2 * m + 1]), valid, do_ref[:, lanes], dlse_ref[:, lanes])
            dq_ref[:, lanes] = dq2
            dsc_ref[2 * m] += ds_heads[0]
            dsc_ref[2 * m + 1] += ds_heads[1]
            for first, size, live in ((0, DIL_HALF, i >= 1), (DIL_HALF, DIL_QB, None), (DIL_HALF + DIL_QB, DIL_HALF, i <= nqb - 2)):
                def add(first=first, size=size, dk2=dk2, dv2=dv2, lanes=lanes):
                    rows = pl.ds(pl.multiple_of(r0 - DIL_HALF + first, DIL_HALF), size)
                    dk_ref[rows, lanes] += dk2[first:first + size]
                    dv_ref[rows, lanes] += dv2[first:first + size]
                if live is None:
                    add()
                else:
                    pl.when(live)(add)

    out = jax.ShapeDtypeStruct((bsz, length, dil * GROUP_WIDTH), F32)
    dsc_shape = (N_HEADS, DIL_QB, DIL_WIN)
    dq, dk, dv, dsc = pl.pallas_call(
        body, name=name, grid=(bsz, dil, nqb),
        in_specs=[pl.BlockSpec(memory_space=pltpu.SMEM), pl.BlockSpec(bkt.shape, lambda b, rho, i: (0, 0)), blk, seq, seq, blk, blk],
        out_specs=[blk, seq, seq, pl.BlockSpec(dsc_shape, lambda b, rho, i: (0, 0, 0))],
        out_shape=[out, out, out, jax.ShapeDtypeStruct(dsc_shape, F32)],
        scratch_shapes=[pltpu.VMEM((length + 2 * DIL_HALF, GROUP_WIDTH), BF16), pltpu.VMEM((length + 2 * DIL_HALF, GROUP_WIDTH), BF16),
                        pltpu.VMEM(dsc_shape, F32)],
        compiler_params=_cparams(("arbitrary", "arbitrary", "arbitrary")),
    )(table, bkt, view(qb), view(kb), view(vb), view(do), view(dlse))
    return dq.reshape(qb.shape), dk.reshape(qb.shape), dv.reshape(qb.shape), dsc


def _headnorm(x, g, mavg):
    return x * lax.rsqrt(_head_stat(x * x, mavg) + RMS_EPS) * g


def _fold_gain(dg_full, fold):
    return jnp.dot(jnp.broadcast_to(dg_full, (8, dg_full.shape[1])), fold, precision=HI, preferred_element_type=F32)


def _bprep_fn(qp, kp, gq, gk, mavg):
    return _headnorm(qp, gq, mavg) * (HEAD_DIM ** -0.5), _headnorm(kp, gk, mavg)


def _dil_layout(dil):
    return None if dil == 1 else Strided(dil)


def _bprep_fwd(z, gq, gk, name):
    mavg = jnp.asarray(_np_group_avg(GROUP_WIDTH))

    def fn(qp, kp, vp, gqv, gkv, mv):
        qb, kb = _bprep_fn(qp, kp, gqv, gkv, mv)
        return (qb, kb, vp) * len(DIL_PATTERNS)

    w = GROUP_WIDTH
    outs = [(w, BF16, _dil_layout(dil)) for _, dil in DIL_PATTERNS for _ in range(3)]
    res = _rowmap(fn, [(z, w, 2), (z, w, 3), (z, w, 4)], [gq, gk, mavg], outs, [], name, 512, z.shape[0])
    return [res[3 * i:3 * i + 3] for i in range(len(DIL_PATTERNS))]


def _bprep_bwd(z, dqs, dks, dvs, gq, gk, name):
    mavg = jnp.asarray(_np_group_avg(GROUP_WIDTH))
    fold = jnp.asarray(_np_tile_fold(GROUP_WIDTH))

    def fn(qp, kp, dq0, dq1, dq2, dk0, dk1, dk2, dv0, dv1, dv2, gqv, gkv, mv, fv):
        _, vjp = jax.vjp(functools.partial(_bprep_fn, mavg=mv), qp, kp, gqv, gkv)
        dqp, dkp, dgq, dgk = vjp((dq0 + dq1 + dq2, dk0 + dk1 + dk2))
        return dqp, dkp, dv0 + dv1 + dv2, _fold_gain(dgq, fv), _fold_gain(dgk, fv)

    w = GROUP_WIDTH
    rows = [(z, w, 2), (z, w, 3)] + _pattern_rows(dqs) + _pattern_rows(dks) + _pattern_rows(dvs)
    return _rowmap(fn, rows, [gq, gk, mavg, fold], [(w, F32)] * 3, [(8, HEAD_DIM)] * 2, name, 256, z.shape[0])


def _mixture_fn(o0, o1, o2, l0, l1, l2):
    m = lax.stop_gradient(jnp.maximum(jnp.maximum(l0, l1), l2))
    e0, e1, e2 = jnp.exp(l0 - m), jnp.exp(l1 - m), jnp.exp(l2 - m)
    return (e0 * o0 + e1 * o1 + e2 * o2) / (e0 + e1 + e2)


def _pattern_rows(arrs):
    return [(a, GROUP_WIDTH, 0) if dil == 1 else (a, GROUP_WIDTH, 0, Strided(dil)) for a, (_, dil) in zip(arrs, DIL_PATTERNS)]


def _mixture_fwd(os_, ls_, name):
    n = os_[0].shape[0]
    return _rowmap(lambda *v: (_mixture_fn(*v),), _pattern_rows(os_) + _pattern_rows(ls_), [], [(GROUP_WIDTH, F32)], [], name, 512, n)[0]


def _mixture_bwd(os_, ls_, dy, name):
    w = GROUP_WIDTH

    def fn(*v):
        _, vjp = jax.vjp(_mixture_fn, *v[:6])
        return vjp(v[6])

    outs = [(w, F32, _dil_layout(dil)) for _ in range(2) for _, dil in DIL_PATTERNS]
    return _rowmap(fn, _pattern_rows(os_) + _pattern_rows(ls_) + [(dy, w, 0)], [], outs, [], name, 512, dy.shape[0])


def _relbias_fold(dscs, name):
    bkts = [jnp.asarray(_np_dil_buckets(dil)) for _, dil in DIL_PATTERNS]
    npat = len(DIL_PATTERNS)

    def body(*refs):
        bkt_refs, d_refs, o_ref = refs[:npat], refs[npat:-1], refs[-1]
        row = lax.broadcasted_iota(jnp.int32, (REL_BUCKETS, 128), 0)
        lane = lax.broadcasted_iota(jnp.int32, (REL_BUCKETS, 128), 1)
        out = jnp.zeros((REL_BUCKETS, 128), F32)
        for p, (_, dil) in enumerate(DIL_PATTERNS):
            bkt = bkt_refs[p][...]
            for h in range(N_HEADS):
                d = d_refs[2 * p][h] + d_refs[2 * p + 1][h]
                for b in _dil_live_buckets(dil):
                    val = jnp.sum(jnp.where(bkt == b, d, 0.0), keepdims=True)
                    out = out + jnp.where((row == b) & (lane == h), val, 0.0)
        o_ref[...] = out

    return pl.pallas_call(
        body, name=name, out_shape=jax.ShapeDtypeStruct((REL_BUCKETS, 128), F32), compiler_params=_cparams(),
    )(*bkts, *dscs)


DPREP_TM = 512


def _dprep_fn(qp, kp, vp, gq, gk, cq, sq, ck, sk, mavg_q, mavg_k, perm_q, perm_k, expand):
    rot = lambda x, perm: jnp.dot(x, perm, precision=SPLIT3, preferred_element_type=F32)
    qn = _headnorm(qp, gq, mavg_q)
    kn = _headnorm(kp, gk, mavg_k)
    qr = (qn * cq + rot(qn, perm_q) * sq) * (HEAD_DIM ** -0.5)
    kr = kn * ck + rot(kn, perm_k) * sk
    return qr, rot(kr, expand), rot(vp, expand)


def _dprep_consts():
    cq, sq = _np_rope_tables(N_HEADS)
    ck, sk = _np_rope_tables(KV_WIDTH // HEAD_DIM)
    tables = [jnp.asarray(a) for a in (cq, sq, ck, sk)]
    mats = [jnp.asarray(a) for a in (_np_group_avg(GROUP_WIDTH), _np_group_avg(KV_WIDTH), _np_rope_partner(GROUP_WIDTH),
                                      _np_rope_partner(KV_WIDTH), _np_kv_expand())]
    per = SEQ // DPREP_TM
    w, kw = GROUP_WIDTH, KV_WIDTH
    table_rows = [(tables[0], w, 0, per), (tables[1], w, 0, per), (tables[2], kw, 0, per), (tables[3], kw, 0, per)]
    return table_rows, mats


def _dprep_fwd(z, gq, gk, name):
    table_rows, mats = _dprep_consts()
    w, kw = GROUP_WIDTH, KV_WIDTH

    def fn(qp, kp, vp, cq, sq, ck, sk, gqv, gkv, *m):
        return _dprep_fn(qp, kp, vp, gqv, gkv, cq, sq, ck, sk, *m)

    return _rowmap(fn, [(z, w, 7), (z, kw, 32), (z, kw, 33)] + table_rows, [gq, gk] + mats, [(w, BF16)] * 3, [], name,
                   DPREP_TM, z.shape[0])


def _dprep_bwd(z, dq, dkx, dvx, gq, gk, name):
    table_rows, mats = _dprep_consts()
    fold_q = jnp.asarray(_np_tile_fold(GROUP_WIDTH))
    fold_k = jnp.asarray(_np_tile_fold(KV_WIDTH))
    w, kw = GROUP_WIDTH, KV_WIDTH

    def fn(qp, kp, vp, dqv, dkv, dvv, cq, sq, ck, sk, gqv, gkv, fq, fk, *m):
        f = lambda a, b, c, d, e: _dprep_fn(a, b, c, d, e, cq, sq, ck, sk, *m)
        _, vjp = jax.vjp(f, qp, kp, vp, gqv, gkv)
        dqp, dkp, dvp, dgq, dgk = vjp((dqv, dkv, dvv))
        return dqp, dkp, dvp, _fold_gain(dgq, fq), _fold_gain(dgk, fk)

    return _rowmap(fn, [(z, w, 7), (z, kw, 32), (z, kw, 33), (dq, w, 0), (dkx, w, 0), (dvx, w, 0)] + table_rows,
                   [gq, gk, fold_q, fold_k] + mats, [(w, F32), (kw, F32), (kw, F32)], [(8, HEAD_DIM)] * 2, name,
                   DPREP_TM, z.shape[0])


GQA_QB = 512


def _gqa_fwd(q, kx, vx, name):
    bsz = q.shape[0]
    blk = pl.BlockSpec((None, GQA_QB, GROUP_WIDTH), lambda b, i: (b, i, 0))
    seq = pl.BlockSpec((None, SEQ, GROUP_WIDTH), lambda b, i: (b, 0, 0))

    def body(q_ref, k_ref, v_ref, o_ref):
        for m in range(N_HEADS // 2):
            lanes = pl.ds(m * 128, 128)
            o_ref[:, lanes] = _attn_pair_fwd(q_ref[:, lanes], k_ref[:, lanes], v_ref[:, lanes], None, None)[0]

    return pl.pallas_call(
        body, name=name, grid=(bsz, SEQ // GQA_QB), in_specs=[blk, seq, seq], out_specs=blk,
        out_shape=jax.ShapeDtypeStruct((bsz, SEQ, GROUP_WIDTH), F32), compiler_params=_cparams(("parallel", "parallel")),
    )(q, kx, vx)


def _gqa_bwd(q, kx, vx, do, name):
    bsz = q.shape[0]
    blk = pl.BlockSpec((None, GQA_QB, GROUP_WIDTH), lambda b, i: (b, i, 0))
    seq = pl.BlockSpec((None, SEQ, GROUP_WIDTH), lambda b, i: (b, 0, 0))

    def body(q_ref, k_ref, v_ref, do_ref, dq_ref, dk_ref, dv_ref):
        @pl.when(pl.program_id(1) == 0)
        def _():
            dk_ref[...] = jnp.zeros_like(dk_ref)
            dv_ref[...] = jnp.zeros_like(dv_ref)

        for m in range(N_HEADS // 2):
            lanes = pl.ds(m * 128, 128)
            dq2, dk2, dv2, _ = _attn_pair_bwd(q_ref[:, lanes], k_ref[:, lanes], v_ref[:, lanes], None, None, do_ref[:, lanes], None)
            dq_ref[:, lanes] = dq2
            dk_ref[:, lanes] += dk2
            dv_ref[:, lanes] += dv2

    out = jax.ShapeDtypeStruct((bsz, SEQ, GROUP_WIDTH), F32)
    return pl.pallas_call(
        body, name=name, grid=(bsz, SEQ // GQA_QB), in_specs=[blk, seq, seq, blk], out_specs=[blk, seq, seq],
        out_shape=[out, out, out], compiler_params=_cparams(("parallel", "arbitrary")),
    )(q, kx, vx, do)


CONV_TILE = 64
CONV_LEAD = 16
CONV_WINDOW = CONV_TILE + 32


def _glu(a, g):
    return a * jax.nn.sigmoid(g)


def _conv_post(c, b, ln_g, ln_b):
    x = c + b
    xc = x - jnp.mean(x, axis=-1, keepdims=True)
    y = xc * lax.rsqrt(jnp.mean(xc * xc, axis=-1, keepdims=True) + LN_EPS) * ln_g + ln_b
    return y * jax.nn.sigmoid(y)


def _conv_shifts(win):
    out = []
    for phase in range(8):
        rolled = win if phase == 0 else pltpu.roll(win, CONV_WINDOW - phase, 0)
        for base in range(0, CONV_WINDOW - CONV_TILE + 1, 8):
            if 1 <= base + phase <= CONV_WIDTH:
                out.append((base + phase, rolled[base:base + CONV_TILE]))
    return out


def _conv_fill(pad_ref, value_of_tile):
    zeros = jnp.zeros((CONV_LEAD, GROUP_WIDTH), F32)
    pad_ref[pl.ds(0, CONV_LEAD), :] = zeros
    pad_ref[pl.ds(CONV_LEAD + SEQ, CONV_LEAD), :] = zeros

    def step(t, carry):
        r0 = pl.multiple_of(t * CONV_TILE, CONV_TILE)
        pad_ref[pl.ds(CONV_LEAD + r0, CONV_TILE), :] = value_of_tile(r0)
        return carry

    lax.fori_loop(0, SEQ // CONV_TILE, step, 0)


def _conv_tile(pad_ref, w_ref, r0, flip):
    acc = jnp.zeros((CONV_TILE, GROUP_WIDTH), F32)
    for offset, rows in _conv_shifts(pad_ref[pl.ds(r0, CONV_WINDOW), :]):
        k = (CONV_WIDTH - offset) if flip else (offset - 1)
        acc = acc + w_ref[pl.ds(k, 1), :] * rows
    return acc


def _conv_fwd(z3, w, b, ln_g, ln_b, name):
    bsz = z3.shape[0]
    seq = lambda cb: pl.BlockSpec((None, SEQ, GROUP_WIDTH), functools.partial(lambda i, cb: (i, 0, cb), cb=cb))
    full = lambda a: pl.BlockSpec(a.shape, lambda i: (0,) * a.ndim)

    def body(a_ref, g_ref, w_ref, b_ref, lg_ref, lb_ref, y_ref, pad_ref):
        _conv_fill(pad_ref, lambda r0: _glu(a_ref[pl.ds(r0, CONV_TILE), :], g_ref[pl.ds(r0, CONV_TILE), :]))

        def step(t, carry):
            r0 = pl.multiple_of(t * CONV_TILE, CONV_TILE)
            y_ref[pl.ds(r0, CONV_TILE), :] = _conv_post(_conv_tile(pad_ref, w_ref, r0, False), b_ref[...], lg_ref[...], lb_ref[...])
            return carry

        lax.fori_loop(0, SEQ // CONV_TILE, step, 0)

    return pl.pallas_call(
        body, name=name, grid=(bsz,), in_specs=[seq(5), seq(6), full(w), full(b), full(ln_g), full(ln_b)], out_specs=seq(0),
        out_shape=jax.ShapeDtypeStruct((bsz, SEQ, GROUP_WIDTH), F32),
        scratch_shapes=[pltpu.VMEM((SEQ + 2 * CONV_LEAD, GROUP_WIDTH), F32)], compiler_params=_cparams(("parallel",)),
    )(z3, z3, w, b, ln_g, ln_b)


def _conv_bwd(z3, dy, w, b, ln_g, ln_b, name):
    bsz = z3.shape[0]
    seq = lambda cb: pl.BlockSpec((None, SEQ, GROUP_WIDTH), functools.partial(lambda i, cb: (i, 0, cb), cb=cb))
    full = lambda a: pl.BlockSpec(a.shape, lambda i: (0,) * a.ndim)
    vec = pl.BlockSpec((1, GROUP_WIDTH), lambda i: (0, 0))

    def body(a_ref, g_ref, dy_ref, w_ref, b_ref, lg_ref, lb_ref, da_ref, dg_ref, dw_ref, db_ref, dlg_ref, dlb_ref, hpad, dpad, dw8):
        @pl.when(pl.program_id(0) == 0)
        def _():
            dw8[...] = jnp.zeros_like(dw8)
            db_ref[...] = jnp.zeros_like(db_ref)
            dlg_ref[...] = jnp.zeros_like(dlg_ref)
            dlb_ref[...] = jnp.zeros_like(dlb_ref)

        _conv_fill(hpad, lambda r0: _glu(a_ref[pl.ds(r0, CONV_TILE), :], g_ref[pl.ds(r0, CONV_TILE), :]))
        zeros = jnp.zeros((CONV_LEAD, GROUP_WIDTH), F32)
        dpad[pl.ds(0, CONV_LEAD), :] = zeros
        dpad[pl.ds(CONV_LEAD + SEQ, CONV_LEAD), :] = zeros

        def through_post(t, carry):
            r0 = pl.multiple_of(t * CONV_TILE, CONV_TILE)
            conv = _conv_tile(hpad, w_ref, r0, False)
            _, vjp = jax.vjp(_conv_post, conv, b_ref[...], lg_ref[...], lb_ref[...])
            dconv, db, dlg, dlb = vjp(dy_ref[pl.ds(r0, CONV_TILE), :])
            db_ref[...] += db
            dlg_ref[...] += dlg
            dlb_ref[...] += dlb
            dpad[pl.ds(CONV_LEAD + r0, CONV_TILE), :] = dconv
            for offset, rows in _conv_shifts(hpad[pl.ds(r0, CONV_WINDOW), :]):
                prod = dconv * rows
                part = prod[0:8]
                for j in range(1, CONV_TILE // 8):
                    part = part + prod[8 * j:8 * j + 8]
                dw8[offset - 1] += part
            return carry

        lax.fori_loop(0, SEQ // CONV_TILE, through_post, 0)

        def through_glu(t, carry):
            r0 = pl.multiple_of(t * CONV_TILE, CONV_TILE)
            dh = _conv_tile(dpad, w_ref, r0, True)
            rows = pl.ds(r0, CONV_TILE)
            _, vjp = jax.vjp(_glu, a_ref[rows, :], g_ref[rows, :])
            da, dg = vjp(dh)
            da_ref[rows, :] = da
            dg_ref[rows, :] = dg
            return carry

        lax.fori_loop(0, SEQ // CONV_TILE, through_glu, 0)
        dw_ref[...] = jnp.sum(dw8[...], axis=1)

    out = jax.ShapeDtypeStruct((bsz, SEQ, GROUP_WIDTH), F32)
    v = jax.ShapeDtypeStruct((1, GROUP_WIDTH), F32)
    return pl.pallas_call(
        body, name=name, grid=(bsz,), in_specs=[seq(5), seq(6), seq(0), full(w), full(b), full(ln_g), full(ln_b)],
        out_specs=[seq(0), seq(0), pl.BlockSpec((CONV_WIDTH, GROUP_WIDTH), lambda i: (0, 0)), vec, vec, vec],
        out_shape=[out, out, jax.ShapeDtypeStruct((CONV_WIDTH, GROUP_WIDTH), F32), v, v, v],
        scratch_shapes=[pltpu.VMEM((SEQ + 2 * CONV_LEAD, GROUP_WIDTH), F32), pltpu.VMEM((SEQ + 2 * CONV_LEAD, GROUP_WIDTH), F32),
                        pltpu.VMEM((CONV_WIDTH, 8, GROUP_WIDTH), F32)],
        compiler_params=_cparams(("arbitrary",)),
    )(z3, z3, dy, w, b, ln_g, ln_b)


def _mixnorm_fwd(ys, gains, name):
    w = GROUP_WIDTH

    def fn(*v):
        return (jnp.concatenate([_rms(v[i], v[4 + i]) for i in range(4)], axis=-1),)

    return _rowmap(fn, [(y, w, 0) for y in ys], list(gains), [(4 * w, BF16)], [], name, 512, ys[0].shape[0])[0]


def _mixnorm_bwd(dyn, ys, gains, name, follow=None):
    w = GROUP_WIDTH
    gains = list(gains) if follow is None else [*gains, follow]

    def fn(*v):
        dys, dgs = [], []
        for i in range(4):
            _, vjp = jax.vjp(_rms, v[4 + i], v[8 + i])
            dy, dg = vjp(v[i])
            dys.append(dy)
            dgs.append(dg)
        return (*dys, *dgs)

    rows = [(dyn, w, i) for i in range(4)] + [(y, w, 0) for y in ys]
    return _rowmap(fn, rows, list(gains), [(w, F32)] * 4, [(1, w)] * 4, name, 512, dyn.shape[0])


def _adamw_fn(w, g, m, v):
    m = ADAM_B1 * m + (1.0 - ADAM_B1) * g
    v = ADAM_B2 * v + (1.0 - ADAM_B2) * (g * g)
    m_hat = m / (1.0 - ADAM_B1 ** ADAM_STEP)
    v_hat = v / (1.0 - ADAM_B2 ** ADAM_STEP)
    delta = -ADAM_LR * (m_hat / (jnp.sqrt(v_hat) + ADAM_EPS) + ADAM_WD * w)
    return delta, m, v


def _adamw(w, g, m, v, name):
    r, c = w.shape
    tm = _pick(r, (256, 128, 64, 32, 16, 8))
    return _rowmap(_adamw_fn, [(a, c, 0) for a in (w, g, m, v)], [], [(c, F32)] * 3, [], name, tm, r)


def _layer_params(l, small, big, prefetch):
    tile_row = lambda g, n: jnp.tile(g, n)[None, :]
    row = lambda g: g[None, :]
    w_s = small["sgu_w"][l].astype(BF16)
    return dict(
        norm1_g=row(small["norm1_g"][l]), norm2_g=row(small["norm2_g"][l]),
        w_s=w_s, w_s_t=jnp.swapaxes(w_s, 1, 2), bm=jnp.repeat(small["sgu_b"][l].T, HEAD_DIM, axis=1),
        gq_dil=tile_row(small["dil_qn_g"][l], N_HEADS), gk_dil=tile_row(small["dil_kn_g"][l], N_HEADS),
        conv_w=small["conv_w"][l], conv_b=row(small["conv_b"][l]), conv_ln_g=row(small["conv_ln_g"][l]),
        conv_ln_b=row(small["conv_ln_b"][l]),
        gq_gqa=tile_row(small["gqa_qn_g"][l], N_HEADS), gk_gqa=tile_row(small["gqa_kn_g"][l], KV_WIDTH // HEAD_DIM),
        mix_g=[row(small["mix_norm_g"][l][i * GROUP_WIDTH:(i + 1) * GROUP_WIDTH]) for i in range(4)],
        big=big, prefetch=prefetch,
    )


def _layer_fwd(x, p, table, bsz, tag):
    t = x.shape[0]
    seq3 = lambda a: a.reshape(bsz, SEQ, a.shape[-1])
    flat = lambda a: a.reshape(t, a.shape[-1])
    h1 = _rmsnorm_fwd(x, p["norm1_g"], tag + "rms1")
    z = _matmul([(h1, p["big"]("w_in", h1))], "nn", F32, tag + "mm_z")
    y_a = _sgu_fwd(z, p["w_s"], p["bm"], tag + "sgu_fwd")
    p["prefetch"](0, "w_out", y_a)
    dil_qkv = _bprep_fwd(z, p["gq_dil"], p["gk_dil"], tag + "dil_prep")
    outs, lses = [], []
    for (_, dil), (qb, kb, vb) in zip(DIL_PATTERNS, dil_qkv):
        o, lse = _dil_fwd(qb, kb, vb, table, dil, f"{tag}dil{dil}_fwd")
        outs.append(o)
        lses.append(lse)
    y_b = _mixture_fwd(outs, lses, tag + "dil_mix")
    y_c = flat(_conv_fwd(seq3(z), p["conv_w"], p["conv_b"], p["conv_ln_g"], p["conv_ln_b"], tag + "conv_fwd"))
    qd, kx, vx = _dprep_fwd(z, p["gq_gqa"], p["gk_gqa"], tag + "gqa_prep")
    y_d = flat(_gqa_fwd(seq3(qd), seq3(kx), seq3(vx), tag + "gqa_fwd"))
    ys = [y_a, y_b, y_c, y_d]
    p["prefetch"](0, "w_gate", y_d)
    yn = _mixnorm_fwd(ys, p["mix_g"], tag + "mixnorm")
    x_mid = _matmul([(yn, p["big"]("w_out", yn))], "nn", F32, tag + "mm_out", residual=x)
    h2 = _rmsnorm_fwd(x_mid, p["norm2_g"], tag + "rms2")
    act, act_du, act_dg = _ffn_up(h2, p["big"]("w_gate", h2), p["big"]("w_up", h2), tag + "ffn_up")
    p["prefetch"](1, "w_in", act)
    x_out = _matmul([(act, p["big"]("w_down", act))], "nn", F32, tag + "mm_down", residual=x_mid)
    saved = dict(x=x, h1=h1, z=z, dil_qkv=dil_qkv, outs=outs, lses=lses, qd=qd, kx=kx, vx=vx, ys=ys, yn=yn, x_mid=x_mid,
                 h2=h2, act=act, act_du=act_du, act_dg=act_dg)
    return x_out, saved


def _layer_bwd(dx_out, dx_out_b, s, p, table, bsz, tag, emit, mid_hook):
    t = dx_out.shape[0]
    seq3 = lambda a: a.reshape(bsz, SEQ, a.shape[-1])
    flat = lambda a: a.reshape(t, a.shape[-1])
    z = s["z"]
    small = {}
    weight = lambda name: p["big"](name, None)
    emit("w_down", _matmul([(s["act"], dx_out_b)], "tn", BF16, tag + "mm_dwdown").reshape(N_CHIPS, FFN_HIDDEN // N_CHIPS, D_MODEL))
    dgate, dup = _ffn_down_bwd(dx_out_b, weight("w_down"), s["act_du"], s["act_dg"], tag + "ffn_dact")
    emit("w_gate", _matmul([(s["h2"], dgate)], "tn", BF16, tag + "mm_dwgate", slabs=N_CHIPS))
    started = emit("w_up", _matmul([(s["h2"], dup)], "tn", BF16, tag + "mm_dwup", slabs=N_CHIPS))
    dh2 = _matmul([(dgate, weight("w_gate")), (dup, weight("w_up"))], "nt", F32, tag + "mm_dh2")
    dx_mid, dx_mid_b, dg2 = _rmsnorm_bwd(dh2, s["x_mid"], p["norm2_g"], dx_out, tag + "rms2_bwd", follow=started)
    small["norm2_g"] = dg2[0]
    mid_hook(dx_mid)
    dyn = _matmul([(dx_mid_b, weight("w_out"))], "nt", F32, tag + "mm_dyn")
    started = emit("w_out", _matmul([(s["yn"], dx_mid_b)], "tn", BF16, tag + "mm_dwout").reshape(N_CHIPS, D_MODEL // N_CHIPS, D_MODEL))
    *dys, dga, dgb, dgc, dgd = _mixnorm_bwd(dyn, s["ys"], p["mix_g"], tag + "mixnorm_bwd", follow=started)
    small["mix_norm_g"] = jnp.concatenate([dga[0], dgb[0], dgc[0], dgd[0]])
    du, dv, dws, dbs = _sgu_bwd(z, dys[0], p["w_s"], p["w_s_t"], p["bm"], tag + "sgu_bwd")
    small["sgu_w"] = dws
    small["sgu_b"] = dbs[:, :N_HEADS].T
    *douts, dl0, dl1, dl2 = _mixture_bwd(s["outs"], s["lses"], dys[1], tag + "dil_mix_bwd")
    dlses = [dl0, dl1, dl2]
    dqs, dks, dvs, dscs = [], [], [], []
    for i, (_, dil) in enumerate(DIL_PATTERNS):
        dq, dk, dvv, dsc = _dil_bwd(*s["dil_qkv"][i], douts[i], dlses[i], table, dil, f"{tag}dil{dil}_bwd")
        dqs.append(dq)
        dks.append(dk)
        dvs.append(dvv)
        dscs.append(dsc)
    dbq, dbk, dbv, dgq, dgk = _bprep_bwd(z, dqs, dks, dvs, p["gq_dil"], p["gk_dil"], tag + "dil_prep_bwd")
    small["dil_qn_g"], small["dil_kn_g"] = dgq[0], dgk[0]
    dca, dcg, dcw, dcb, dclg, dclb = _conv_bwd(seq3(z), seq3(dys[2]), p["conv_w"], p["conv_b"], p["conv_ln_g"], p["conv_ln_b"],
                                               tag + "conv_bwd")
    small["conv_w"], small["conv_b"], small["conv_ln_g"], small["conv_ln_b"] = dcw, dcb[0], dclg[0], dclb[0]
    dqd, dkx, dvx = _gqa_bwd(seq3(s["qd"]), seq3(s["kx"]), seq3(s["vx"]), seq3(dys[3]), tag + "gqa_bwd")
    ddq, ddk, ddv, dgq, dgk = _dprep_bwd(z, flat(dqd), flat(dkx), flat(dvx), p["gq_gqa"], p["gk_gqa"], tag + "gqa_prep_bwd")
    small["gqa_qn_g"], small["gqa_kn_g"] = dgq[0], dgk[0]
    dz = jnp.concatenate([a.astype(BF16) for a in (du, dv, dbq, dbk, dbv, flat(dca), flat(dcg), ddq, ddk, ddv)], axis=1)
    dz4 = dz.reshape(t, N_CHIPS, IN_WIDTH // N_CHIPS).transpose(1, 0, 2)
    started = emit("w_in", _matmul([(s["h1"], dz4)], "tn", BF16, tag + "mm_dwin", slabs=N_CHIPS))
    dh1 = _matmul([(dz, weight("w_in"))], "nt", F32, tag + "mm_dh1")
    dx, dx_b, dg1 = _rmsnorm_bwd(dh1, s["x"], p["norm1_g"], dx_mid, tag + "rms1_bwd", follow=started)
    small["norm1_g"] = dg1[0]
    return dx, dx_b, small, dscs


def _local_step(x, target, small, big, emit, mid_hook, bsz, prefetch=lambda l, name, after: None):
    table = small["rel_bias"]
    ahead = lambda l: (lambda more, name, after: prefetch(l + more, name, after))
    params = [_layer_params(l, small, functools.partial(big, l), ahead(l)) for l in range(DEPTH)]
    saved = []
    h = x
    for l in range(DEPTH):
        h, sv = _layer_fwd(h, params[l], table, bsz, f"l{l}_")
        saved.append(sv)
    dh, dh_b, loss = _loss_fwd_bwd(h, target, "loss")
    small_grads, dscs = [None] * DEPTH, [None] * DEPTH
    for l in reversed(range(DEPTH)):
        dh, dh_b, small_grads[l], dscs[l] = _layer_bwd(dh, dh_b, saved[l], params[l], table, bsz, f"l{l}_",
                                                       functools.partial(emit, l), functools.partial(mid_hook, l))
    fold_in = [dscs[l][i] for i in range(len(DIL_PATTERNS)) for l in range(DEPTH)]
    stacked = {k: jnp.stack([small_grads[l][k] for l in range(DEPTH)]) for k in small_grads[0]}
    stacked["rel_bias"] = _relbias_fold(fold_in, "relbias_fold")[:, :N_HEADS]
    return loss, dh, stacked


def _mesh_pos():
    return lax.axis_index("x"), lax.axis_index("y"), lax.axis_index("c")


def _other_chips(x, y):
    return [(1 - x, y), (x, 1 - y), (1 - x, 1 - y)]


_ANY = pl.BlockSpec(memory_space=pl.ANY)


def _swap_sibling(arrs, name):
    n = len(arrs)

    def body(*refs):
        in_refs, out_refs, send_sems, recv_sems = refs[:n], refs[n:2 * n], refs[2 * n], refs[2 * n + 1]
        x, y, c = _mesh_pos()
        copies = [pltpu.make_async_remote_copy(src_ref=in_refs[k], dst_ref=out_refs[k], send_sem=send_sems.at[k],
                                               recv_sem=recv_sems.at[k], device_id=(x, y, 1 - c), device_id_type=MESH)
                  for k in range(n)]
        for cp in copies:
            cp.start()
        for cp in copies:
            cp.wait()

    return pl.pallas_call(
        body, name=name, in_specs=[_ANY] * n, out_specs=[_ANY] * n,
        out_shape=[jax.ShapeDtypeStruct(a.shape, a.dtype) for a in arrs],
        scratch_shapes=[pltpu.SemaphoreType.DMA((n,)), pltpu.SemaphoreType.DMA((n,))],
    )(*arrs)


def _complete_pairs(arrs, name):
    n = len(arrs)

    def body(*refs):
        in_refs, out_refs, send_sems, recv_sems = refs[:n], refs[n:2 * n], refs[2 * n], refs[2 * n + 1]
        x, y, c = _mesh_pos()
        copies = [pltpu.make_async_remote_copy(src_ref=in_refs[k].at[c], dst_ref=out_refs[k].at[c], send_sem=send_sems.at[k],
                                               recv_sem=recv_sems.at[k], device_id=(x, y, 1 - c), device_id_type=MESH)
                  for k in range(n)]
        for cp in copies:
            cp.start()
        for k, cp in enumerate(copies):
            cp.wait_send()
            pltpu.make_async_remote_copy(src_ref=in_refs[k].at[1 - c], dst_ref=out_refs[k].at[1 - c], send_sem=send_sems.at[k],
                                         recv_sem=recv_sems.at[k], device_id=(x, y, 1 - c), device_id_type=MESH).wait_recv()

    return pl.pallas_call(
        body, name=name, in_specs=[_ANY] * n, out_specs=[_ANY] * n,
        out_shape=[jax.ShapeDtypeStruct(a.shape, a.dtype) for a in arrs], input_output_aliases={k: k for k in range(n)},
        scratch_shapes=[pltpu.SemaphoreType.DMA((n,)), pltpu.SemaphoreType.DMA((n,))],
    )(*arrs)


_HBM = pl.BlockSpec(memory_space=pltpu.HBM)
_SEM = pl.BlockSpec(memory_space=pltpu.SEMAPHORE)
_DATAFLOW = pltpu.SideEffectType.DATAFLOW_SIDE_EFFECTING


def _chip_copies(src_refs, land_refs, send_sems, recv_sems, scatter):
    x, y, c = _mesh_pos()
    me = 2 * x + y
    out = []
    for k, (src_ref, land_ref) in enumerate(zip(src_refs, land_refs)):
        if scatter:
            h = src_ref.shape[1] // 2
            for q in range(N_DEV - 1):
                fx, fy, fc = ((q + 1) >> 2) & 1, ((q + 1) >> 1) & 1, (q + 1) & 1
                px, py, pc = (1 - x if fx else x), (1 - y if fy else y), (1 - c if fc else c)
                src = src_ref.at[2 * px + py, pl.ds(pc * h, h)]
                sems = dict(send_sem=send_sems.at[7 * k + q], recv_sem=recv_sems.at[7 * k + q], device_id=(px, py, pc), device_id_type=MESH)
                out.append((pltpu.make_async_remote_copy(src_ref=src, dst_ref=land_ref.at[4 * x + 2 * y + c], **sems),
                            pltpu.make_async_remote_copy(src_ref=src, dst_ref=land_ref.at[4 * px + 2 * py + pc], **sems)))
            continue
        slot = (lambda chip: land_ref.at[c, chip]) if len(land_ref.shape) == 4 else (lambda chip: land_ref.at[chip])
        for j, (cx, cy) in enumerate(_other_chips(x, y)):
            sems = dict(send_sem=send_sems.at[3 * k + j], recv_sem=recv_sems.at[3 * k + j], device_id=(cx, cy, c), device_id_type=MESH)
            out.append((pltpu.make_async_remote_copy(src_ref=src_ref, dst_ref=slot(me), **sems),
                        pltpu.make_async_remote_copy(src_ref=src_ref, dst_ref=slot(2 * cx + cy), **sems)))
    return out


def _chips_start(srcs, scatter, after, name, per_core=False):
    n = len(srcs)
    n_sems = (N_DEV - 1 if scatter else N_CHIPS - 1) * n
    if scatter:
        lands = [lax.empty((N_DEV, s.shape[1] // 2, s.shape[2]), s.dtype) for s in srcs]
    else:
        lands = [lax.empty((*((2, N_CHIPS) if per_core else (N_CHIPS,)), *s.shape), s.dtype) for s in srcs]

    def body(*refs):
        src_refs, land_refs = refs[:n], refs[n:2 * n]
        send_sems, recv_sems, token = refs[2 * n + 1], refs[2 * n + 2], refs[-1]
        for sent, _ in _chip_copies(src_refs, land_refs, send_sems, recv_sems, scatter):
            sent.start()
        token[...] = jnp.zeros_like(token)

    hbm = lambda a: pltpu.HBM(a.shape, a.dtype)
    res = pl.pallas_call(
        body, name=name,
        in_specs=[_HBM] * (2 * n) + [_ANY],
        out_specs=[_SEM, _SEM] + [_HBM] * (2 * n) + [pl.BlockSpec(memory_space=pltpu.VMEM)],
        out_shape=[pltpu.SemaphoreType.DMA((n_sems,)), pltpu.SemaphoreType.DMA((n_sems,))] + [hbm(a) for a in srcs] + [hbm(a) for a in lands]
        + [jax.ShapeDtypeStruct((8, 128), F32)],
        input_output_aliases={i: 2 + i for i in range(2 * n)},
        compiler_params=pltpu.CompilerParams(has_side_effects=_DATAFLOW),
    )(*[pltpu.with_memory_space_constraint(a, pltpu.HBM) for a in (*srcs, *lands)], after)
    return (res[0], res[1], res[2:2 + n], res[2 + n:2 + 2 * n]), res[-1]


def _chips_wait(handle, scatter, after, name):
    send_sems, recv_sems, srcs, lands = handle
    n = len(srcs)

    def body(*refs):
        src_refs, land_refs = refs[:n], refs[n:2 * n]
        send_sems, recv_sems = refs[2 * n], refs[2 * n + 1]
        for sent, landed in _chip_copies(src_refs, land_refs, send_sems, recv_sems, scatter):
            sent.wait_send()
            landed.wait_recv()

    hbm = lambda a: pltpu.HBM(a.shape, a.dtype)
    res = pl.pallas_call(
        body, name=name,
        in_specs=[_HBM] * (2 * n) + [_SEM, _SEM, _ANY], out_specs=[_HBM] * (2 * n),
        out_shape=[hbm(a) for a in srcs] + [hbm(a) for a in lands],
        input_output_aliases={i: i for i in range(2 * n)},
        compiler_params=pltpu.CompilerParams(has_side_effects=_DATAFLOW),
    )(*srcs, *lands, send_sems, recv_sems, after)
    return res[n:]


N_DEV = 8


def _everyone_copies(src_ref, land_ref, send_sems, recv_sems):
    x, y, c = _mesh_pos()
    out = []
    for q in range(N_DEV - 1):
        fx, fy, fc = ((q + 1) >> 2) & 1, ((q + 1) >> 1) & 1, (q + 1) & 1
        px, py, pc = (1 - x if fx else x), (1 - y if fy else y), (1 - c if fc else c)
        sems = dict(send_sem=send_sems.at[q], recv_sem=recv_sems.at[q], device_id=(px, py, pc), device_id_type=MESH)
        out.append((pltpu.make_async_remote_copy(src_ref=src_ref, dst_ref=land_ref.at[4 * x + 2 * y + c], **sems),
                    pltpu.make_async_remote_copy(src_ref=src_ref, dst_ref=land_ref.at[4 * px + 2 * py + pc], **sems)))
    return out


def _everyone_start(block, after, name):
    land = lax.empty((N_DEV, *block.shape), block.dtype)

    def body(src_ref, land_ref, after_ref, send_sems, recv_sems, src_thru, land_thru, token):
        for sent, _ in _everyone_copies(src_ref, land_ref, send_sems, recv_sems):
            sent.start()
        token[...] = jnp.zeros_like(token)

    hbm = lambda a: pltpu.HBM(a.shape, a.dtype)
    n_sem = N_DEV - 1
    res = pl.pallas_call(
        body, name=name, in_specs=[_HBM, _HBM, _ANY],
        out_specs=[_SEM, _SEM, _HBM, _HBM, pl.BlockSpec(memory_space=pltpu.VMEM)],
        out_shape=[pltpu.SemaphoreType.DMA((n_sem,)), pltpu.SemaphoreType.DMA((n_sem,)), hbm(block), hbm(land),
                   jax.ShapeDtypeStruct((8, 128), F32)],
        input_output_aliases={0: 2, 1: 3}, compiler_params=pltpu.CompilerParams(has_side_effects=_DATAFLOW),
    )(pltpu.with_memory_space_constraint(block, pltpu.HBM), pltpu.with_memory_space_constraint(land, pltpu.HBM), after)
    return res[:4]


def _everyone_wait(handle, after, name):
    send_sems, recv_sems, block, land = handle

    def body(src_ref, land_ref, send_sems, recv_sems, after_ref, src_thru, land_thru):
        for sent, landed in _everyone_copies(src_ref, land_ref, send_sems, recv_sems):
            sent.wait_send()
            landed.wait_recv()

    hbm = lambda a: pltpu.HBM(a.shape, a.dtype)
    return pl.pallas_call(
        body, name=name, in_specs=[_HBM, _HBM, _SEM, _SEM, _ANY], out_specs=[_HBM, _HBM], out_shape=[hbm(block), hbm(land)],
        input_output_aliases={0: 0, 1: 1}, compiler_params=pltpu.CompilerParams(has_side_effects=_DATAFLOW),
    )(block, land, send_sems, recv_sems, after)


def _pair_copies(arr_refs, send_sems, recv_sems):
    x, y, c = _mesh_pos()
    out = []
    for k, ref in enumerate(arr_refs):
        sems = dict(send_sem=send_sems.at[k], recv_sem=recv_sems.at[k], device_id=(x, y, 1 - c), device_id_type=MESH)
        out.append((pltpu.make_async_remote_copy(src_ref=ref.at[c], dst_ref=ref.at[c], **sems),
                    pltpu.make_async_remote_copy(src_ref=ref.at[1 - c], dst_ref=ref.at[1 - c], **sems)))
    return out


def _pairs_start(arrs, after, name):
    n = len(arrs)

    def body(*refs):
        send_sems, recv_sems = refs[n + 1], refs[n + 2]
        for sent, _ in _pair_copies(refs[:n], send_sems, recv_sems):
            sent.start()
        refs[-1][...] = jnp.zeros_like(refs[-1])

    hbm = lambda a: pltpu.HBM(a.shape, a.dtype)
    res = pl.pallas_call(
        body, name=name, in_specs=[_HBM] * n + [_ANY],
        out_specs=[_SEM, _SEM] + [_HBM] * n + [pl.BlockSpec(memory_space=pltpu.VMEM)],
        out_shape=[pltpu.SemaphoreType.DMA((n,)), pltpu.SemaphoreType.DMA((n,))] + [hbm(a) for a in arrs] + [jax.ShapeDtypeStruct((8, 128), F32)],
        input_output_aliases={i: 2 + i for i in range(n)}, compiler_params=pltpu.CompilerParams(has_side_effects=_DATAFLOW),
    )(*[pltpu.with_memory_space_constraint(a, pltpu.HBM) for a in arrs], after)
    return res[0], res[1], res[2:2 + n]


def _pairs_wait(handle, after, name):
    send_sems, recv_sems, arrs = handle
    n = len(arrs)

    def body(*refs):
        for sent, landed in _pair_copies(refs[:n], refs[n], refs[n + 1]):
            sent.wait_send()
            landed.wait_recv()

    hbm = lambda a: pltpu.HBM(a.shape, a.dtype)
    return pl.pallas_call(
        body, name=name, in_specs=[_HBM] * n + [_SEM, _SEM, _ANY], out_specs=[_HBM] * n, out_shape=[hbm(a) for a in arrs],
        input_output_aliases={i: i for i in range(n)}, compiler_params=pltpu.CompilerParams(has_side_effects=_DATAFLOW),
    )(*arrs, send_sems, recv_sems, after)


def _allgather_sum_small(block, name):
    m_per, n = block.shape

    def body(x_ref, out_ref, sum_ref, send_sems, recv_sems, local_sem):
        x, y, c = _mesh_pos()
        me, sibling = (x, y, c), (x, y, 1 - c)
        chips = _other_chips(x, y)

        def rows(px, py, pc):
            return out_ref.at[pl.ds((4 * px + 2 * py + pc) * m_per, m_per), :]

        def copy(k, blk, to, src=None):
            return pltpu.make_async_remote_copy(src_ref=rows(*blk) if src is None else src, dst_ref=rows(*blk),
                                                send_sem=send_sems.at[k], recv_sem=recv_sems.at[k], device_id=to, device_id_type=MESH)

        mine = pltpu.make_async_copy(x_ref, rows(*me), local_sem)
        mine.start()
        first = [copy(0, me, sibling, src=x_ref)]
        first += [copy(1 + j, me, (*chip, c), src=x_ref) for j, chip in enumerate(chips)]
        for cp in first:
            cp.start()
        passed = [copy(4 + j, (*chip, c), sibling) for j, chip in enumerate(chips)]
        for j, chip in enumerate(chips):
            copy(1 + j, (*chip, c), me).wait_recv()
            passed[j].start()
        copy(0, sibling, me).wait_recv()
        for j, chip in enumerate(chips):
            copy(4 + j, (*chip, 1 - c), me).wait_recv()
        for cp in first + passed:
            cp.wait_send()
        mine.wait()
        total = out_ref[pl.ds(0, m_per), :]
        for d in range(1, N_DEV):
            total = total + out_ref[pl.ds(d * m_per, m_per), :]
        sum_ref[...] = total

    vmem = pl.BlockSpec(memory_space=pltpu.VMEM)
    return pl.pallas_call(
        body, name=name, in_specs=[vmem], out_specs=[vmem, vmem],
        out_shape=[jax.ShapeDtypeStruct((N_DEV * m_per, n), F32), jax.ShapeDtypeStruct((m_per, n), F32)],
        scratch_shapes=[pltpu.SemaphoreType.DMA((7,)), pltpu.SemaphoreType.DMA((7,)), pltpu.SemaphoreType.DMA],
        compiler_params=pltpu.CompilerParams(vmem_limit_bytes=V7X_VMEM_LIMIT),
    )(block)


WEIGHTS = ("rel_bias", "norm1_g", "w_in", "sgu_w", "sgu_b", "dil_qn_g", "dil_kn_g", "conv_w", "conv_b", "conv_ln_g", "conv_ln_b",
           "gqa_qn_g", "gqa_kn_g", "mix_norm_g", "w_out", "norm2_g", "w_gate", "w_up", "w_down")
SHARDED = ("w_in", "w_out", "w_gate", "w_up", "w_down")
COLUMN_SHARDED = ("w_in", "w_gate", "w_up")
REPLICATED = tuple(k for k in WEIGHTS if k not in SHARDED and k != "conv_w")


PACK_ROWS = 256


def _pack(parts):
    flat = jnp.concatenate([p.reshape(-1) for p in parts])
    pad = (-flat.shape[0]) % (PACK_ROWS * LANES)
    return jnp.pad(flat, (0, pad)).reshape(-1, LANES)


def _unpack(buf, shapes):
    flat = buf.reshape(-1)
    out, at = [], 0
    for s in shapes:
        size = math.prod(s)
        out.append(flat[at:at + size].reshape(s))
        at += size
    return out


RS_TM = (256, 128, 64, 32, 16)


def _add_own_seven(own, land, place, name):
    _, h, cols = land.shape
    tm = _pick(h, RS_TM)
    nb = h // tm

    def body(place_ref, own_ref, *refs):
        total = own_ref[...].astype(F32)
        for l_ref in refs[:-1]:
            total = total + l_ref[...].astype(F32)
        refs[-1][...] = total

    slot = lambda r: pl.BlockSpec((None, tm, cols), functools.partial(lambda i, p, r: (jnp.bitwise_xor(p[2], r), i, 0), r=r))
    grid_spec = pltpu.PrefetchScalarGridSpec(
        num_scalar_prefetch=1, grid=(nb,),
        in_specs=[pl.BlockSpec((None, tm, cols), lambda i, p: (p[0], p[1] * nb + i, 0))] + [slot(r) for r in range(1, N_DEV)],
        out_specs=pl.BlockSpec((tm, cols), lambda i, p: (i, 0)))
    return pl.pallas_call(body, name=name, grid_spec=grid_spec, out_shape=jax.ShapeDtypeStruct((h, cols), F32),
                          compiler_params=_cparams(("parallel",)))(place, own, *[land] * (N_DEV - 1))


def _reduce_start(grads, after, tag):
    handle, token = _chips_start(grads, True, after, tag + "start")
    return (handle, grads), token


def _reduce_finish(started, place, after, tag):
    handle, grads = started
    lands = _chips_wait(handle, True, after, tag + "wait")
    totals = [_add_own_seven(g, land, place, f"{tag}sum_{k}") for k, (g, land) in enumerate(zip(grads, lands))]
    return list(zip(totals, _swap_sibling(totals, tag + "share")))


def _adamw_shard(w, m, v, halves, c1, name):
    depth, rows, cols = w.shape
    h = rows // 2
    tm = _pick(h, RS_TM)
    nb = h // tm
    sources = [a for pair in halves for a in pair]

    def body(c_ref, w_ref, m_ref, v_ref, *refs):
        g_refs, (g_out, d_out, m_out, v_out) = refs[:2 * depth], refs[2 * depth:]
        layer, mine = pl.program_id(0), pl.program_id(1) == c_ref[0]
        g = None
        for l in range(depth):
            g_l = jnp.where(mine, g_refs[2 * l][...], g_refs[2 * l + 1][...])
            g = g_l if g is None else jnp.where(layer == l, g_l, g)
        delta, m_new, v_new = _adamw_fn(w_ref[...], g, m_ref[...], v_ref[...])
        g_out[...], d_out[...], m_out[...], v_out[...] = g, delta, m_new, v_new

    def source_spec(l, own):
        def index(layer, half, i, c_ref):
            return (jnp.where((layer == l) & ((half == c_ref[0]) == own), i, 0), 0)
        return pl.BlockSpec((tm, cols), index)

    blk = pl.BlockSpec((None, tm, cols), lambda layer, half, i, c_ref: (layer, half * nb + i, 0))
    grid_spec = pltpu.PrefetchScalarGridSpec(
        num_scalar_prefetch=1, grid=(depth, 2, nb),
        in_specs=[blk, blk, blk] + [source_spec(l, own) for l in range(depth) for own in (True, False)], out_specs=[blk] * 4)
    return pl.pallas_call(body, name=name, grid_spec=grid_spec, out_shape=[jax.ShapeDtypeStruct(w.shape, F32)] * 4,
                          compiler_params=_cparams(("arbitrary", "arbitrary", "arbitrary")))(c1, w, m, v, *sources)


GATHER_GROUPS = ((("w_in",), 0), (("w_out",), None), (("w_gate", "w_up"), None), (("w_down",), None), (("w_in",), 1))
REDUCE_GROUPS = (("w_down", "w_gate", "w_up"), ("w_out",), ("w_in",))


def kernel(x, rel_bias, norm1_g, w_in, sgu_w, sgu_b, dil_qn_g, dil_kn_g, conv_w, conv_b, conv_ln_g, conv_ln_b, gqa_qn_g, gqa_kn_g, mix_norm_g, w_out, norm2_g, w_gate, w_up, w_down, loss_target, m_rel_bias, m_norm1_g, m_w_in, m_sgu_w, m_sgu_b, m_dil_qn_g, m_dil_kn_g, m_conv_w, m_conv_b, m_conv_ln_g, m_conv_ln_b, m_gqa_qn_g, m_gqa_kn_g, m_mix_norm_g, m_w_out, m_norm2_g, m_w_gate, m_w_up, m_w_down, v_rel_bias, v_norm1_g, v_w_in, v_sgu_w, v_sgu_b, v_dil_qn_g, v_dil_kn_g, v_conv_w, v_conv_b, v_conv_ln_g, v_conv_ln_b, v_gqa_qn_g, v_gqa_kn_g, v_mix_norm_g, v_w_out, v_norm2_g, v_w_gate, v_w_up, v_w_down):
    w = dict(rel_bias=rel_bias, norm1_g=norm1_g, w_in=w_in, sgu_w=sgu_w, sgu_b=sgu_b, dil_qn_g=dil_qn_g, dil_kn_g=dil_kn_g,
             conv_w=conv_w, conv_b=conv_b, conv_ln_g=conv_ln_g, conv_ln_b=conv_ln_b, gqa_qn_g=gqa_qn_g, gqa_kn_g=gqa_kn_g,
             mix_norm_g=mix_norm_g, w_out=w_out, norm2_g=norm2_g, w_gate=w_gate, w_up=w_up, w_down=w_down)
    m = dict(rel_bias=m_rel_bias, norm1_g=m_norm1_g, w_in=m_w_in, sgu_w=m_sgu_w, sgu_b=m_sgu_b, dil_qn_g=m_dil_qn_g,
             dil_kn_g=m_dil_kn_g, conv_w=m_conv_w, conv_b=m_conv_b, conv_ln_g=m_conv_ln_g, conv_ln_b=m_conv_ln_b,
             gqa_qn_g=m_gqa_qn_g, gqa_kn_g=m_gqa_kn_g, mix_norm_g=m_mix_norm_g, w_out=m_w_out, norm2_g=m_norm2_g,
             w_gate=m_w_gate, w_up=m_w_up, w_down=m_w_down)
    v = dict(rel_bias=v_rel_bias, norm1_g=v_norm1_g, w_in=v_w_in, sgu_w=v_sgu_w, sgu_b=v_sgu_b, dil_qn_g=v_dil_qn_g,
             dil_kn_g=v_dil_kn_g, conv_w=v_conv_w, conv_b=v_conv_b, conv_ln_g=v_conv_ln_g, conv_ln_b=v_conv_ln_b,
             gqa_qn_g=v_gqa_qn_g, gqa_kn_g=v_gqa_kn_g, mix_norm_g=v_mix_norm_g, w_out=v_w_out, norm2_g=v_norm2_g,
             w_gate=v_w_gate, w_up=v_w_up, w_down=v_w_down)
    bsz = x.shape[0]
    t = bsz * SEQ
    xi, yi, ci = _mesh_pos()
    chip = 2 * xi + yi
    conv_cols = conv_w.shape[-1]

    conv_rows = DEPTH * CONV_WIDTH
    conv_block = jnp.pad(conv_w.reshape(conv_rows, conv_cols), ((0, (-conv_rows) % 8), (0, 0)))
    every, _ = _allgather_sum_small(conv_block, "conv_w_gather")
    every = every.reshape(N_DEV, conv_block.shape[0], conv_cols)
    conv_w_full = jnp.concatenate([every[2 * j, :conv_rows].reshape(DEPTH, CONV_WIDTH, conv_cols) for j in range(N_CHIPS)], axis=-1)

    c1 = jnp.reshape(ci, (1,)).astype(jnp.int32)
    place = jnp.stack([chip, ci, 4 * xi + 2 * yi + ci]).astype(jnp.int32)

    def own_part(k, layer):
        if layer is None:
            return lax.dynamic_index_in_dim(w[k], ci, axis=0, keepdims=False).astype(BF16)
        half = w[k].shape[1] // 2
        return lax.dynamic_slice_in_dim(w[k][layer], ci * half, half, axis=0).astype(BF16)

    fetches, token = [], every
    for gi, (group, layer) in enumerate(GATHER_GROUPS):
        parts = [own_part(k, layer) for k in group]
        handle, token = _chips_start(parts, False, token, f"gather{gi}_start", per_core=True)
        fetches.append((handle, parts))
    all_started = token
    gathered, handed = {}, {}

    def group_of(l, name):
        return [name in group and layer in (None, l) for group, layer in GATHER_GROUPS].index(True)

    def landed(gi, after):
        handle, parts = fetches[gi]
        lands = _chips_wait(handle, False, all_started if after is None else after, f"gather{gi}_wait")
        return [lax.dynamic_update_slice(land, own[None, None], (ci, chip, 0, 0)) for land, own in zip(lands, parts)]

    def prefetch(l, name, after):
        if l < DEPTH and group_of(l, name) not in handed:
            gi = group_of(l, name)
            handed[gi] = _pairs_start(landed(gi, after), after, f"gather{gi}_share_start")

    def big(l, name, after):
        if (l, name) not in gathered:
            gi = group_of(l, name)
            group, layer = GATHER_GROUPS[gi]
            if gi in handed:
                whole = _pairs_wait(handed[gi], after, f"gather{gi}_share_wait")
            else:
                whole = _complete_pairs(landed(gi, after), f"gather{gi}_share")
            for k, g in zip(group, whole):
                rows, cols = g.shape[2:]
                if layer is not None:
                    gathered[layer, k] = g.transpose(0, 2, 1, 3).reshape(2 * rows, N_CHIPS * cols)
                    continue
                for each in range(DEPTH):
                    gathered[each, k] = (g[each].transpose(1, 0, 2).reshape(rows, N_CHIPS * cols) if k in COLUMN_SHARDED
                                         else g[each].reshape(N_CHIPS * rows, cols))
        return gathered[l, name]

    big(0, "w_in", None)

    pending, started, reduced = {}, {}, {}

    def emit(l, name, g):
        pending[l, name] = g
        for gi, group in enumerate(REDUCE_GROUPS):
            if name in group and all((l, k) in pending for k in group):
                started[l, gi], token = _reduce_start([pending[l, k] for k in group], g, f"l{l}_reduce{gi}_")
                return token
        return None

    def finish(l, after):
        for gi, group in enumerate(REDUCE_GROUPS):
            for k, r in zip(group, _reduce_finish(started[l, gi], place, after, f"l{l}_reduce{gi}_")):
                reduced[l, k] = r
            after = reduced[l, group[0]][1]

    def mid_hook(l, a):
        if l + 1 < DEPTH:
            finish(l + 1, a)

    small = {k: w[k] for k in REPLICATED}
    small["conv_w"] = conv_w_full
    loss, dx, small_grads = _local_step(x.reshape(t, D_MODEL), loss_target.reshape(t, D_MODEL), small, big, emit, mid_hook, bsz,
                                        prefetch)
    loss = lax.psum(loss[0, 0], ("x", "y", "c"))

    names = REPLICATED + ("conv_w",)
    shapes = [small_grads[k].shape for k in names]
    packed_grads = _pack([small_grads[k] for k in names])
    eighths = packed_grads.reshape(N_CHIPS, packed_grads.shape[0] // N_CHIPS, LANES)
    small_reduce, _ = _reduce_start([eighths], dx, "small_grads_")
    finish(0, dx)

    grads, deltas, new_m, new_v = {}, {}, {}, {}
    for k in SHARDED:
        grads[k], deltas[k], new_m[k], new_v[k] = _adamw_shard(w[k], m[k], v[k], [reduced[l, k] for l in range(DEPTH)], c1, "adamw_" + k)

    landed = _chips_wait(small_reduce[0], True, new_v[SHARDED[-1]], "small_grads_wait")[0]
    mine = _add_own_seven(eighths, landed, place, "small_grads_sum")
    own, others = _everyone_wait(_everyone_start(mine, mine, "small_grads_spread"), mine, "small_grads_spread_wait")
    summed = lax.dynamic_update_slice_in_dim(others, own[None], 4 * xi + 2 * yi + ci, axis=0).reshape(packed_grads.shape)
    summed_parts = dict(zip(names, _unpack(summed, shapes)))
    rep_shapes = [w[k].shape for k in REPLICATED]
    packed = [_pack([src[k] for k in REPLICATED]) for src in (w, {k: summed_parts[k] for k in REPLICATED}, m, v)]
    d_p, m_p, v_p = _adamw(*packed, "adamw_replicated")
    for k, gk, dk, mk, vk in zip(REPLICATED, _unpack(packed[1], rep_shapes), _unpack(d_p, rep_shapes), _unpack(m_p, rep_shapes),
                                 _unpack(v_p, rep_shapes)):
        grads[k], deltas[k], new_m[k], new_v[k] = gk, dk, mk, vk
    g_conv = lax.dynamic_slice_in_dim(summed_parts["conv_w"], chip * conv_cols, conv_cols, axis=2)
    packed = [_pack([a]) for a in (conv_w, g_conv, m["conv_w"], v["conv_w"])]
    d_p, m_p, v_p = _adamw(*packed, "adamw_conv_w")
    grads["conv_w"] = g_conv
    deltas["conv_w"], new_m["conv_w"], new_v["conv_w"] = (_unpack(a, [conv_w.shape])[0] for a in (d_p, m_p, v_p))

    return (loss, dx.reshape(x.shape), *[grads[k] for k in WEIGHTS], *[deltas[k] for k in WEIGHTS],
            *[new_m[k] for k in WEIGHTS], *[new_v[k] for k in WEIGHTS])
```

```python
import functools
import math

import numpy as np
import jax
import jax.numpy as jnp
from jax import lax
from jax.experimental import pallas as pl
from jax.experimental.pallas import tpu as pltpu

F32 = jnp.float32
BF16 = jnp.bfloat16

D_MODEL = 2048
SEQ = 2048
DEPTH = 2
HEAD_DIM = 64
GROUP_WIDTH = 512
N_HEADS = 8
SGU_CHUNK = 128
DIL_PATTERNS = ((128, 1), (512, 4), (2048, 16))
DIL_HALF = 64
CONV_WIDTH = 31
KV_WIDTH = 128
GRID_W = 64
ROPE_THETA = 10000.0
REL_BUCKETS = 32
REL_MAX_DIST = 1024
FFN_HIDDEN = 5632
IN_WIDTH = 4352
RMS_EPS = 1e-6
LN_EPS = 1e-5
ADAM_LR = 0.001
ADAM_B1 = 0.9
ADAM_B2 = 0.999
ADAM_EPS = 1e-08
ADAM_WD = 0.01
ADAM_STEP = 10
N_CHIPS = 4

V7X_VMEM_LIMIT = 56 * 1024 * 1024
MATMUL_VMEM_BUDGET = 48 * 1024 * 1024
LANES = 128
HI = lax.Precision.HIGHEST
SPLIT3 = lax.Precision.HIGH
MESH = pl.DeviceIdType.MESH


def _cparams(sem=None):
    return pltpu.CompilerParams(dimension_semantics=sem, vmem_limit_bytes=V7X_VMEM_LIMIT)


def _pick(n, cands):
    for c in cands:
        if n % c == 0:
            return c
    raise ValueError(f"no tile for {n}")


_DIMS = {"nn": (((1,), (0,)), ((), ())), "nt": (((1,), (1,)), ((), ())), "tn": (((0,), (0,)), ((), ()))}


def _matmul(pairs, mode, out_dtype, name, residual=None, slabs=1):
    a0, b0 = pairs[0]
    b3 = b0.ndim == 3
    if mode == "nn":
        (M, K), N = a0.shape, b0.shape[1]
    elif mode == "nt":
        (M, K), N = a0.shape, b0.shape[0]
    else:
        (K, M) = a0.shape
        N = b0.shape[-1] if b3 else b0.shape[1] // slabs
    npairs = len(pairs)
    a_bytes, b_bytes, o_bytes = a0.dtype.itemsize, b0.dtype.itemsize, jnp.dtype(out_dtype).itemsize
    per_out = 4 + 2 * o_bytes + (8 if residual is not None else 0)
    tn_cands = [c for c in ((1024, 512) if K <= 2048 else (512,)) + (1408, 2176, 256) if N % c == 0] + [N]
    tm, tn = next((tm, tn) for tn in tn_cands for tm in (1024, 1408, 512, 256)
                  if M % tm == 0 and 2 * npairs * K * (tm * a_bytes + tn * b_bytes) + tm * tn * per_out <= MATMUL_VMEM_BUDGET)
    tk = K
    ni, nj = M // tm, N // tn
    j_outer = nj * M * a_bytes + N * b_bytes < M * a_bytes + ni * N * b_bytes
    grid = (slabs, nj, ni) if j_outer else (slabs, ni, nj)
    at = lambda f: (lambda s, g1, g2: f(s, g2, g1)) if j_outer else f

    if mode in ("nn", "nt"):
        a_spec = pl.BlockSpec((tm, tk), at(lambda s, i, j: (i, 0)))
    else:
        a_spec = pl.BlockSpec((tk, tm), at(lambda s, i, j: (0, i)))
    if mode == "nt":
        b_spec = pl.BlockSpec((tn, tk), at(lambda s, i, j: (j, 0)))
    elif b3:
        b_spec = pl.BlockSpec((None, tk, tn), at(lambda s, i, j: (s, 0, j)))
    else:
        b_spec = pl.BlockSpec((tk, tn), at(lambda s, i, j: (0, s * nj + j)))
    if slabs > 1:
        o_spec = pl.BlockSpec((None, tm, tn), at(lambda s, i, j: (s, i, j)))
        o_shape = jax.ShapeDtypeStruct((slabs, M, N), out_dtype)
    else:
        o_spec = pl.BlockSpec((tm, tn), at(lambda s, i, j: (i, j)))
        o_shape = jax.ShapeDtypeStruct((M, N), out_dtype)
    in_specs = [a_spec] * npairs + [b_spec] * npairs
    args = [a for a, _ in pairs] + [b for _, b in pairs]
    if residual is not None:
        in_specs.append(pl.BlockSpec((tm, tn), at(lambda s, i, j: (i, j))))
        args.append(residual)
    dims = _DIMS[mode]

    def body(*refs):
        a_refs, b_refs = refs[:npairs], refs[npairs:2 * npairs]
        res_ref = refs[2 * npairs] if residual is not None else None
        o_ref = refs[-1]
        r = None
        for a_ref, b_ref in zip(a_refs, b_refs):
            d = lax.dot_general(a_ref[...].astype(BF16), b_ref[...].astype(BF16), dims, preferred_element_type=F32)
            r = d if r is None else r + d
        if res_ref is not None:
            r = r + res_ref[...]
        o_ref[...] = r.astype(out_dtype)

    return pl.pallas_call(
        body, name=name, grid=grid, in_specs=in_specs, out_specs=o_spec, out_shape=o_shape,
        compiler_params=_cparams(("parallel", "parallel", "parallel")),
    )(*args)


class Strided:
    def __init__(self, r):
        self.r = r


def _rowmap(fn, rows, fulls, row_outs, acc_outs, name, tm, n_rows):
    nr, nf, nro = len(rows), len(fulls), len(row_outs)
    rows = [r if len(r) == 4 else (*r, n_rows // tm) for r in rows]
    row_outs = [o if len(o) == 3 else (*o, None) for o in row_outs]
    in_specs = [pl.BlockSpec((tm // per.r, per.r * w), lambda i: (i, 0)) if isinstance(per, Strided) else
                pl.BlockSpec((tm, w), functools.partial(lambda i, cb, per: (i % per, cb), cb=cb, per=per)) for _, w, cb, per in rows]
    in_specs += [pl.BlockSpec(f.shape, lambda i: (0,) * f.ndim) for f in fulls]
    out_specs = [pl.BlockSpec((tm, w) if st is None else (tm // st.r, st.r * w), lambda i: (i, 0)) for w, _, st in row_outs]
    out_specs += [pl.BlockSpec(s, functools.partial(lambda i, n: (0,) * n, n=len(s))) for s in acc_outs]
    out_shape = [jax.ShapeDtypeStruct((n_rows, w) if st is None else (n_rows // st.r, st.r * w), dt) for w, dt, st in row_outs]
    out_shape += [jax.ShapeDtypeStruct(s, F32) for s in acc_outs]
    strided = [(k, w, per.r) for k, (_, w, _, per) in enumerate(rows) if isinstance(per, Strided)]
    strided += [(nr + nf + k, w, st.r) for k, (w, _, st) in enumerate(row_outs) if st is not None]
    n_scratch = len(strided)

    def body(*refs):
        refs, scratch = refs[:len(refs) - n_scratch], dict(zip([k for k, _, _ in strided], refs[len(refs) - n_scratch:]))
        ins = []
        for k, ref in enumerate(refs[:nr + nf]):
            if k in scratch:
                w, r, scr = rows[k][1], rows[k][3].r, scratch[k]
                for rho in range(r):
                    for j in range(w // LANES):
                        scr.at[j][pl.ds(rho, tm // r, stride=r), :] = ref[:, pl.ds(rho * w + j * LANES, LANES)]
                ins.append(jnp.concatenate([scr[j] for j in range(w // LANES)], axis=1))
            else:
                ins.append(ref[...])
        outs = fn(*ins)
        o_refs = refs[nr + nf:]
        for k, (o_ref, val) in enumerate(zip(o_refs[:nro], outs[:nro])):
            if nr + nf + k in scratch:
                w, r, scr = row_outs[k][0], row_outs[k][2].r, scratch[nr + nf + k]
                val = val.astype(F32)
                for j in range(w // LANES):
                    scr[j] = val[:, j * LANES:(j + 1) * LANES]
                for rho in range(r):
                    for j in range(w // LANES):
                        o_ref[:, pl.ds(rho * w + j * LANES, LANES)] = scr.at[j][pl.ds(rho, tm // r, stride=r), :].astype(o_ref.dtype)
            else:
                o_ref[...] = val.astype(o_ref.dtype)
        if acc_outs:
            first = pl.program_id(0) == 0
            for o_ref, val in zip(o_refs[nro:], outs[nro:]):
                @pl.when(first)
                def _(o_ref=o_ref, val=val):
                    o_ref[...] = val

                @pl.when(jnp.logical_not(first))
                def _(o_ref=o_ref, val=val):
                    o_ref[...] += val

    res = pl.pallas_call(
        body, name=name, grid=(n_rows // tm,), in_specs=in_specs, out_specs=out_specs, out_shape=out_shape,
        scratch_shapes=[pltpu.VMEM((w // LANES, tm, LANES), F32) for _, w, _ in strided],
        compiler_params=_cparams(("arbitrary",) if acc_outs else ("parallel",)),
    )(*[r[0] for r in rows], *fulls)
    return res


def _rms(x, g):
    return x * lax.rsqrt(jnp.mean(x * x, axis=-1, keepdims=True) + RMS_EPS) * g


def _rmsnorm_fwd(x, g, name):
    t = x.shape[0]
    return _rowmap(lambda xv, gv: (_rms(xv, gv),), [(x, D_MODEL, 0)], [g], [(D_MODEL, BF16)], [], name, 512, t)[0]


def _matmul_residual_rms(a, b, residual, g, name):
    (m, k), n = a.shape, b.shape[1]
    tm = 512

    def body(a_ref, b_ref, res_ref, g_ref, x_ref, h_ref):
        x = jnp.dot(a_ref[...], b_ref[...], preferred_element_type=F32) + res_ref[...]
        x_ref[...] = x
        h_ref[...] = _rms(x, g_ref[...]).astype(BF16)

    row = lambda w: pl.BlockSpec((tm, w), lambda i: (i, 0))
    return pl.pallas_call(
        body, name=name, grid=(m // tm,),
        in_specs=[row(k), pl.BlockSpec((k, n), lambda i: (0, 0)), row(n), pl.BlockSpec((1, n), lambda i: (0, 0))],
        out_specs=[row(n), row(n)], out_shape=[jax.ShapeDtypeStruct((m, n), F32), jax.ShapeDtypeStruct((m, n), BF16)],
        compiler_params=_cparams(("parallel",)),
    )(a, b, residual, g)


def _rmsnorm_bwd(dh, x, g, dres, name, follow=None):
    t = x.shape[0]

    def fn(dhv, xv, drv, gv, *_):
        _, vjp = jax.vjp(_rms, xv, gv)
        dx, dg = vjp(dhv)
        return dx + drv, dx + drv, dg

    fulls = [g] if follow is None else [g, follow]
    return _rowmap(fn, [(dh, D_MODEL, 0), (x, D_MODEL, 0), (dres, D_MODEL, 0)], fulls, [(D_MODEL, F32), (D_MODEL, BF16)],
                   [(1, D_MODEL)], name, 256, t)


def _loss_fwd_bwd(y, target, name):
    t = y.shape[0]

    def fn(yv, tv):
        e = yv - tv
        return e * (1.0 / D_MODEL), e * (1.0 / D_MODEL), (0.5 / D_MODEL) * jnp.sum(e * e, keepdims=True)

    return _rowmap(fn, [(y, D_MODEL, 0), (target, D_MODEL, 0)], [], [(D_MODEL, F32), (D_MODEL, BF16)], [(1, 1)], name, 512, t)


FFN_TILE = (1024, 512)


def _ffn_up(h, wg, wu, name):
    t, n = h.shape[0], wg.shape[1]
    tm, tn = FFN_TILE

    def body(h_ref, wg_ref, wu_ref, act_ref, du_ref, dg_ref):
        hv = h_ref[...]
        g = jnp.dot(hv, wg_ref[...], preferred_element_type=F32)
        u = jnp.dot(hv, wu_ref[...], preferred_element_type=F32)
        sg = jax.nn.sigmoid(g)
        silu = g * sg
        act_ref[...] = (silu * u).astype(BF16)
        du_ref[...] = silu.astype(BF16)
        dg_ref[...] = (u * (sg + silu * (1.0 - sg))).astype(BF16)

    o_spec = pl.BlockSpec((tm, tn), lambda i, j: (i, j))
    o_shape = jax.ShapeDtypeStruct((t, n), BF16)
    return pl.pallas_call(
        body, name=name, grid=(t // tm, n // tn),
        in_specs=[pl.BlockSpec((tm, D_MODEL), lambda i, j: (i, 0)), pl.BlockSpec((D_MODEL, tn), lambda i, j: (0, j)),
                  pl.BlockSpec((D_MODEL, tn), lambda i, j: (0, j))],
        out_specs=[o_spec] * 3, out_shape=[o_shape] * 3, compiler_params=_cparams(("parallel", "parallel")),
    )(h, wg, wu)


def _ffn_down_bwd(dy, wd, act_du, act_dg, name):
    t, n = dy.shape[0], wd.shape[0]
    tm, tn = FFN_TILE

    def body(dy_ref, wd_ref, adu_ref, adg_ref, dg_ref, du_ref):
        dact = lax.dot_general(dy_ref[...].astype(BF16), wd_ref[...], _DIMS["nt"], preferred_element_type=F32)
        du_ref[...] = (dact * adu_ref[...].astype(F32)).astype(BF16)
        dg_ref[...] = (dact * adg_ref[...].astype(F32)).astype(BF16)

    o_spec = pl.BlockSpec((tm, tn), lambda i, j: (i, j))
    o_shape = jax.ShapeDtypeStruct((t, n), BF16)
    return pl.pallas_call(
        body, name=name, grid=(t // tm, n // tn),
        in_specs=[pl.BlockSpec((tm, D_MODEL), lambda i, j: (i, 0)), pl.BlockSpec((tn, D_MODEL), lambda i, j: (j, 0)),
                  o_spec, o_spec],
        out_specs=[o_spec] * 2, out_shape=[o_shape] * 2, compiler_params=_cparams(("parallel", "parallel")),
    )(dy, wd, act_du, act_dg)


def _np_group_avg(width, group=HEAD_DIM):
    i = np.arange(width)
    return ((i[:, None] // group) == (i[None, :] // group)).astype(np.float32) / group


def _np_tile_fold(width, group=HEAD_DIM):
    return ((np.arange(width)[:, None] % group) == np.arange(group)[None, :]).astype(np.float32)


def _np_group_fold(width, group=HEAD_DIM, pad=128):
    return ((np.arange(width)[:, None] // group) == np.arange(pad)[None, :]).astype(np.float32)


def _np_rope_partner(width):
    i = np.arange(width)
    partner = np.where(i % 32 < 16, i + 16, i - 16)
    return (partner[:, None] == i[None, :]).astype(np.float32)


def _np_kv_expand():
    src = np.arange(KV_WIDTH)
    dst = np.arange(GROUP_WIDTH)
    return ((src[:, None] // HEAD_DIM == dst[None, :] // (4 * HEAD_DIM)) & (src[:, None] % HEAD_DIM == dst[None, :] % HEAD_DIM)).astype(np.float32)


def _np_rope_tables(n_heads):
    t = np.arange(SEQ)
    pos = {0: (t // GRID_W).astype(np.float32), 1: (t % GRID_W).astype(np.float32)}
    freqs = (ROPE_THETA ** (-np.arange(16, dtype=np.float32) / 16)).astype(np.float32)
    cos_parts, sin_parts = [], []
    for axis in (0, 1):
        ang = pos[axis][:, None] * freqs[None, :]
        c, s = np.cos(ang).astype(np.float32), np.sin(ang).astype(np.float32)
        cos_parts += [c, c]
        sin_parts += [-s, s]
    cos = np.concatenate(cos_parts, axis=1)
    sin = np.concatenate(sin_parts, axis=1)
    return np.tile(cos, (1, n_heads)), np.tile(sin, (1, n_heads))


def _np_t5_buckets(rel):
    nb = REL_BUCKETS // 2
    max_exact = nb // 2
    ret = np.where(rel > 0, nb, 0)
    n = np.abs(rel)
    nf = np.maximum(n, 1).astype(np.float32)
    large = max_exact + (np.log(nf / max_exact) / math.log(REL_MAX_DIST / max_exact) * (nb - max_exact)).astype(np.int32)
    large = np.minimum(large, nb - 1)
    return (ret + np.where(n < max_exact, n, large)).astype(np.int32)


DIL_QB = 128
DIL_WIN = DIL_QB + 2 * DIL_HALF


def _np_dil_buckets(dil):
    off = np.arange(DIL_WIN)[None, :] - DIL_HALF - np.arange(DIL_QB)[:, None]
    return _np_t5_buckets(off * dil)


def _dil_live_buckets(dil):
    off = np.arange(-DIL_HALF, DIL_HALF + 1)
    return sorted(set(_np_t5_buckets(off * dil).tolist()))


def _head_stat(x, mavg):
    return jnp.dot(x, mavg, precision=SPLIT3, preferred_element_type=F32)


def _gelu(x):
    return 0.5 * x * (1.0 + jnp.tanh(math.sqrt(2.0 / math.pi) * (x + 0.044715 * (x * x * x))))


def _sgu_pre(u_pre, v_pre, mavg):
    v = _gelu(v_pre)
    xc = v - _head_stat(v, mavg)
    vn = xc * lax.rsqrt(_head_stat(xc * xc, mavg) + LN_EPS)
    return _gelu(u_pre), vn


def _sgu_mix(w_ref, vnb, bm):
    lane_group = lax.broadcasted_iota(jnp.int32, (1, GROUP_WIDTH), 1) // HEAD_DIM
    mixed = bm
    for g in range(N_HEADS):
        r = jnp.dot(w_ref[g], vnb, preferred_element_type=F32)
        mixed = mixed + jnp.where(lane_group == g, r, 0.0)
    return mixed


SGU_TM = 512


def _sgu_fwd(z, w_s, bm, name):
    t = z.shape[0]
    mavg = jnp.asarray(_np_group_avg(GROUP_WIDTH))

    def body(u_ref, v_ref, w_ref, bm_ref, mavg_ref, y_ref):
        for c in range(SGU_TM // SGU_CHUNK):
            rows = pl.ds(c * SGU_CHUNK, SGU_CHUNK)
            u, vn = _sgu_pre(u_ref[rows, :], v_ref[rows, :], mavg_ref[...])
            y_ref[rows, :] = u * _sgu_mix(w_ref, vn.astype(BF16), bm_ref[...])

    full = lambda a: pl.BlockSpec(a.shape, lambda i: (0,) * a.ndim)
    return pl.pallas_call(
        body, name=name, grid=(t // SGU_TM,),
        in_specs=[pl.BlockSpec((SGU_TM, GROUP_WIDTH), lambda i: (i, 0)), pl.BlockSpec((SGU_TM, GROUP_WIDTH), lambda i: (i, 1)),
                  full(w_s), full(bm), full(mavg)],
        out_specs=pl.BlockSpec((SGU_TM, GROUP_WIDTH), lambda i: (i, 0)),
        out_shape=jax.ShapeDtypeStruct((t, GROUP_WIDTH), F32), compiler_params=_cparams(("parallel",)),
    )(z, z, w_s, bm, mavg)


def _sgu_bwd(z, dy, w_s, w_s_t, bm, name):
    t = z.shape[0]
    mavg = jnp.asarray(_np_group_avg(GROUP_WIDTH))
    gfold = jnp.asarray(_np_group_fold(GROUP_WIDTH))

    def body(u_ref, v_ref, dy_ref, w_ref, wt_ref, bm_ref, mavg_ref, gfold_ref, du_ref, dv_ref, dw_ref, dbs_ref, dbm_ref):
        @pl.when(pl.program_id(0) == 0)
        def _():
            dw_ref[...] = jnp.zeros_like(dw_ref)
            dbm_ref[...] = jnp.zeros_like(dbm_ref)

        lane_group = lax.broadcasted_iota(jnp.int32, (1, GROUP_WIDTH), 1) // HEAD_DIM
        for c in range(SGU_TM // SGU_CHUNK):
            rows = pl.ds(c * SGU_CHUNK, SGU_CHUNK)
            (u, vn), pre_vjp = jax.vjp(functools.partial(_sgu_pre, mavg=mavg_ref[...]), u_ref[rows, :], v_ref[rows, :])
            vnb = vn.astype(BF16)
            mixed = _sgu_mix(w_ref, vnb, bm_ref[...])
            dyv = dy_ref[rows, :]
            dmixed = dyv * u
            dbm_ref[...] += dmixed
            dvn = jnp.zeros_like(vn)
            for g in range(N_HEADS):
                dm_g = jnp.where(lane_group == g, dmixed, 0.0).astype(BF16)
                dw_ref[g] += lax.dot_general(dm_g, vnb, _DIMS["nt"], preferred_element_type=F32)
                dvn = dvn + jnp.dot(wt_ref[g], dm_g, preferred_element_type=F32)
            du_pre, dv_pre = pre_vjp((dyv * mixed, dvn))
            du_ref[rows, :] = du_pre
            dv_ref[rows, :] = dv_pre

        @pl.when(pl.program_id(0) == t // SGU_TM - 1)
        def _():
            dbs_ref[...] = jnp.dot(dbm_ref[...], gfold_ref[...], precision=HI, preferred_element_type=F32)

    full = lambda a: pl.BlockSpec(a.shape, lambda i: (0,) * a.ndim)
    row = pl.BlockSpec((SGU_TM, GROUP_WIDTH), lambda i: (i, 0))
    return pl.pallas_call(
        body, name=name, grid=(t // SGU_TM,),
        in_specs=[row, pl.BlockSpec((SGU_TM, GROUP_WIDTH), lambda i: (i, 1)), row, full(w_s), full(w_s_t), full(bm), full(mavg),
                  full(gfold)],
        out_specs=[row, row, pl.BlockSpec((N_HEADS, SGU_CHUNK, SGU_CHUNK), lambda i: (0, 0, 0)),
                   pl.BlockSpec((SGU_CHUNK, 128), lambda i: (0, 0))],
        out_shape=[jax.ShapeDtypeStruct((t, GROUP_WIDTH), F32)] * 2 + [jax.ShapeDtypeStruct((N_HEADS, SGU_CHUNK, SGU_CHUNK), F32),
                                                                      jax.ShapeDtypeStruct((SGU_CHUNK, 128), F32)],
        scratch_shapes=[pltpu.VMEM((SGU_CHUNK, GROUP_WIDTH), F32)],
        compiler_params=_cparams(("arbitrary",)),
    )(z, z, dy, w_s, w_s_t, bm, mavg, gfold)


def _head_lanes():
    lane_head = lax.broadcasted_iota(jnp.int32, (1, 2 * HEAD_DIM), 1) // HEAD_DIM
    return lane_head == 0, lane_head == 1


def _stack_heads(x2):
    h0, h1 = _head_lanes()
    zero = jnp.zeros_like(x2)
    return jnp.concatenate([jnp.where(h0, x2, zero), jnp.where(h1, x2, zero)], axis=0)


def _unstack_heads(y):
    r = y.shape[0] // 2
    h0, _ = _head_lanes()
    return jnp.where(h0, y[:r], y[r:])


def _pair_softmax(qs, k2, biases, valid):
    s = lax.dot_general(qs, k2, _DIMS["nt"], preferred_element_type=F32)
    if biases is not None:
        s = s + jnp.concatenate(biases, axis=0)
    if valid is not None:
        s = jnp.where(jnp.concatenate([valid, valid], axis=0), s, -1e30)
    m = jnp.max(s, axis=-1, keepdims=True)
    e = jnp.exp(s - m)
    l = jnp.sum(e, axis=-1, keepdims=True)
    return e / l, m + jnp.log(l)


def _attn_pair_fwd(q2, k2, v2, biases, valid):
    p, lse = _pair_softmax(_stack_heads(q2), k2, biases, valid)
    o = jnp.dot(p.astype(BF16), v2, preferred_element_type=F32)
    return _unstack_heads(o), _unstack_heads(jnp.broadcast_to(lse, o.shape))


def _attn_pair_bwd(q2, k2, v2, biases, valid, do2, dlse2):
    r = q2.shape[0]
    qs = _stack_heads(q2)
    p, _ = _pair_softmax(qs, k2, biases, valid)
    dos = _stack_heads(do2).astype(BF16)
    dp = lax.dot_general(dos, v2, _DIMS["nt"], preferred_element_type=F32)
    delta = jnp.sum(dp * p, axis=-1, keepdims=True)
    if dlse2 is not None:
        delta = delta - jnp.sum(_stack_heads(dlse2), axis=-1, keepdims=True)
    ds = p * (dp - delta)
    dsb = ds.astype(BF16)
    dq2 = _unstack_heads(jnp.dot(dsb, k2, preferred_element_type=F32))
    dk2 = lax.dot_general(dsb, qs, _DIMS["tn"], preferred_element_type=F32)
    dv2 = lax.dot_general(p.astype(BF16), dos, _DIMS["tn"], preferred_element_type=F32)
    return dq2, dk2, dv2, [ds[:r], ds[r:]]


def _dil_valid(r0, length):
    row = lax.broadcasted_iota(jnp.int32, (DIL_QB, DIL_WIN), 0)
    col = lax.broadcasted_iota(jnp.int32, (DIL_QB, DIL_WIN), 1)
    off = col - DIL_HALF - row
    kpos = r0 - DIL_HALF + col
    return (jnp.abs(off) <= DIL_HALF) & (kpos >= 0) & (kpos < length)


def _dil_build_bias(tab_ref, bkt_ref, bias_ref, dil):
    bkt = bkt_ref[...]
    for h in range(N_HEADS):
        acc = jnp.zeros((DIL_QB, DIL_WIN), F32)
        for b in _dil_live_buckets(dil):
            acc = jnp.where(bkt == b, tab_ref[b, h], acc)
        bias_ref[h] = acc


def _dil_fill_pad(pad_ref, src_ref, length):
    zeros = jnp.zeros((DIL_HALF, GROUP_WIDTH), pad_ref.dtype)
    pad_ref[pl.ds(0, DIL_HALF), :] = zeros
    pad_ref[pl.ds(DIL_HALF + length, DIL_HALF), :] = zeros
    pad_ref[pl.ds(DIL_HALF, length), :] = src_ref[...]


def _dil_specs(bsz, length, dil):
    view = lambda a: a.reshape(bsz, length, dil * GROUP_WIDTH)
    blk = pl.BlockSpec((None, DIL_QB, GROUP_WIDTH), lambda b, rho, i: (b, i, rho))
    seq = pl.BlockSpec((None, length, GROUP_WIDTH), lambda b, rho, i: (b, 0, rho))
    return view, blk, seq


def _dil_fwd(qb, kb, vb, table, dil, name):
    length = SEQ // dil
    bsz = qb.shape[0] // length
    bkt = jnp.asarray(_np_dil_buckets(dil))
    view, blk, seq = _dil_specs(bsz, length, dil)

    def body(tab_ref, bkt_ref, q_ref, k_ref, v_ref, o_ref, lse_ref, kpad, vpad, bias_ref):
        i = pl.program_id(2)

        @pl.when((pl.program_id(0) == 0) & (pl.program_id(1) == 0) & (i == 0))
        def _():
            _dil_build_bias(tab_ref, bkt_ref, bias_ref, dil)

        @pl.when(i == 0)
        def _():
            _dil_fill_pad(kpad, k_ref, length)
            _dil_fill_pad(vpad, v_ref, length)

        r0 = pl.multiple_of(i * DIL_QB, DIL_QB)
        valid = _dil_valid(r0, length)
        for m in range(N_HEADS // 2):
            lanes = pl.ds(m * 128, 128)
            o2, lse2 = _attn_pair_fwd(q_ref[:, lanes], kpad[pl.ds(r0, DIL_WIN), lanes], vpad[pl.ds(r0, DIL_WIN), lanes],
                                      (bias_ref[2 * m], bias_ref[2 * m + 1]), valid)
            o_ref[:, lanes] = o2
            lse_ref[:, lanes] = lse2

    out = jax.ShapeDtypeStruct((bsz, length, dil * GROUP_WIDTH), F32)
    o, lse = pl.pallas_call(
        body, name=name, grid=(bsz, dil, length // DIL_QB),
        in_specs=[pl.BlockSpec(memory_space=pltpu.SMEM), pl.BlockSpec(bkt.shape, lambda b, rho, i: (0, 0)), blk, seq, seq],
        out_specs=[blk, blk], out_shape=[out, out],
        scratch_shapes=[pltpu.VMEM((length + 2 * DIL_HALF, GROUP_WIDTH), BF16), pltpu.VMEM((length + 2 * DIL_HALF, GROUP_WIDTH), BF16),
                        pltpu.VMEM((N_HEADS, DIL_QB, DIL_WIN), F32)],
        compiler_params=_cparams(("arbitrary", "arbitrary", "arbitrary")),
    )(table, bkt, view(qb), view(kb), view(vb))
    return o.reshape(qb.shape), lse.reshape(qb.shape)


def _dil_bwd(qb, kb, vb, do, dlse, table, dil, name):
    length = SEQ // dil
    bsz = qb.shape[0] // length
    nqb = length // DIL_QB
    bkt = jnp.asarray(_np_dil_buckets(dil))
    view, blk, seq = _dil_specs(bsz, length, dil)

    def body(tab_ref, bkt_ref, q_ref, k_ref, v_ref, do_ref, dlse_ref, dq_ref, dk_ref, dv_ref, dsc_ref, kpad, vpad, bias_ref):
        i = pl.program_id(2)

        @pl.when((pl.program_id(0) == 0) & (pl.program_id(1) == 0) & (i == 0))
        def _():
            _dil_build_bias(tab_ref, bkt_ref, bias_ref, dil)
            dsc_ref[...] = jnp.zeros_like(dsc_ref)

        @pl.when(i == 0)
        def _():
            _dil_fill_pad(kpad, k_ref, length)
            _dil_fill_pad(vpad, v_ref, length)
            dk_ref[...] = jnp.zeros_like(dk_ref)
            dv_ref[...] = jnp.zeros_like(dv_ref)

        r0 = pl.multiple_of(i * DIL_QB, DIL_QB)
        valid = _dil_valid(r0, length)
        for m in range(N_HEADS // 2):
            lanes = pl.ds(m * 128, 128)
            dq2, dk2, dv2, ds_heads = _attn_pair_bwd(
                q_ref[:, lanes], kpad[pl.ds(r0, DIL_WIN), lanes], vpad[pl.ds(r0, DIL_WIN), lanes],
                (bias_ref[2 * m], bias_ref[2 * m + 1]), valid, do_ref[:, lanes], dlse_ref[:, lanes])
            dq_ref[:, lanes] = dq2
            dsc_ref[2 * m] += ds_heads[0]
            dsc_ref[2 * m + 1] += ds_heads[1]
            for first, size, live in ((0, DIL_HALF, i >= 1), (DIL_HALF, DIL_QB, None), (DIL_HALF + DIL_QB, DIL_HALF, i <= nqb - 2)):
                def add(first=first, size=size, dk2=dk2, dv2=dv2, lanes=lanes):
                    rows = pl.ds(pl.multiple_of(r0 - DIL_HALF + first, DIL_HALF), size)
                    dk_ref[rows, lanes] += dk2[first:first + size]
                    dv_ref[rows, lanes] += dv2[first:first + size]
                if live is None:
                    add()
                else:
                    pl.when(live)(add)

    out = jax.ShapeDtypeStruct((bsz, length, dil * GROUP_WIDTH), F32)
    dsc_shape = (N_HEADS, DIL_QB, DIL_WIN)
    dq, dk, dv, dsc = pl.pallas_call(
        body, name=name, grid=(bsz, dil, nqb),
        in_specs=[pl.BlockSpec(memory_space=pltpu.SMEM), pl.BlockSpec(bkt.shape, lambda b, rho, i: (0, 0)), blk, seq, seq, blk, blk],
        out_specs=[blk, seq, seq, pl.BlockSpec(dsc_shape, lambda b, rho, i: (0, 0, 0))],
        out_shape=[out, out, out, jax.ShapeDtypeStruct(dsc_shape, F32)],
        scratch_shapes=[pltpu.VMEM((length + 2 * DIL_HALF, GROUP_WIDTH), BF16), pltpu.VMEM((length + 2 * DIL_HALF, GROUP_WIDTH), BF16),
                        pltpu.VMEM(dsc_shape, F32)],
        compiler_params=_cparams(("arbitrary", "arbitrary", "arbitrary")),
    )(table, bkt, view(qb), view(kb), view(vb), view(do), view(dlse))
    return dq.reshape(qb.shape), dk.reshape(qb.shape), dv.reshape(qb.shape), dsc


def _headnorm(x, g, mavg):
    return x * lax.rsqrt(_head_stat(x * x, mavg) + RMS_EPS) * g


def _fold_gain(dg_full, fold):
    return jnp.dot(jnp.broadcast_to(dg_full, (8, dg_full.shape[1])), fold, precision=HI, preferred_element_type=F32)


def _bprep_fn(qp, kp, gq, gk, mavg):
    return _headnorm(qp, gq, mavg) * (HEAD_DIM ** -0.5), _headnorm(kp, gk, mavg)


def _dil_layout(dil):
    return None if dil == 1 else Strided(dil)


def _bprep_fwd(z, gq, gk, name):
    mavg = jnp.asarray(_np_group_avg(GROUP_WIDTH))

    def fn(qp, kp, vp, gqv, gkv, mv):
        qb, kb = _bprep_fn(qp, kp, gqv, gkv, mv)
        return (qb, kb, vp) * len(DIL_PATTERNS)

    w = GROUP_WIDTH
    outs = [(w, BF16, _dil_layout(dil)) for _, dil in DIL_PATTERNS for _ in range(3)]
    res = _rowmap(fn, [(z, w, 2), (z, w, 3), (z, w, 4)], [gq, gk, mavg], outs, [], name, 512, z.shape[0])
    return [res[3 * i:3 * i + 3] for i in range(len(DIL_PATTERNS))]


def _bprep_bwd(z, dqs, dks, dvs, gq, gk, name):
    mavg = jnp.asarray(_np_group_avg(GROUP_WIDTH))
    fold = jnp.asarray(_np_tile_fold(GROUP_WIDTH))

    def fn(qp, kp, dq0, dq1, dq2, dk0, dk1, dk2, dv0, dv1, dv2, gqv, gkv, mv, fv):
        _, vjp = jax.vjp(functools.partial(_bprep_fn, mavg=mv), qp, kp, gqv, gkv)
        dqp, dkp, dgq, dgk = vjp((dq0 + dq1 + dq2, dk0 + dk1 + dk2))
        return dqp, dkp, dv0 + dv1 + dv2, _fold_gain(dgq, fv), _fold_gain(dgk, fv)

    w = GROUP_WIDTH
    rows = [(z, w, 2), (z, w, 3)] + _pattern_rows(dqs) + _pattern_rows(dks) + _pattern_rows(dvs)
    return _rowmap(fn, rows, [gq, gk, mavg, fold], [(w, F32)] * 3, [(8, HEAD_DIM)] * 2, name, 256, z.shape[0])


def _mixture_fn(o0, o1, o2, l0, l1, l2):
    m = lax.stop_gradient(jnp.maximum(jnp.maximum(l0, l1), l2))
    e0, e1, e2 = jnp.exp(l0 - m), jnp.exp(l1 - m), jnp.exp(l2 - m)
    return (e0 * o0 + e1 * o1 + e2 * o2) / (e0 + e1 + e2)


def _pattern_rows(arrs):
    return [(a, GROUP_WIDTH, 0) if dil == 1 else (a, GROUP_WIDTH, 0, Strided(dil)) for a, (_, dil) in zip(arrs, DIL_PATTERNS)]


def _mixture_fwd(os_, ls_, name):
    n = os_[0].shape[0]
    return _rowmap(lambda *v: (_mixture_fn(*v),), _pattern_rows(os_) + _pattern_rows(ls_), [], [(GROUP_WIDTH, F32)], [], name, 512, n)[0]


def _mixture_bwd(os_, ls_, dy, name):
    w = GROUP_WIDTH

    def fn(*v):
        _, vjp = jax.vjp(_mixture_fn, *v[:6])
        return vjp(v[6])

    outs = [(w, F32, _dil_layout(dil)) for _ in range(2) for _, dil in DIL_PATTERNS]
    return _rowmap(fn, _pattern_rows(os_) + _pattern_rows(ls_) + [(dy, w, 0)], [], outs, [], name, 512, dy.shape[0])


def _relbias_fold(dscs, name):
    bkts = [jnp.asarray(_np_dil_buckets(dil)) for _, dil in DIL_PATTERNS]
    npat = len(DIL_PATTERNS)

    def body(*refs):
        bkt_refs, d_refs, o_ref = refs[:npat], refs[npat:-1], refs[-1]
        row = lax.broadcasted_iota(jnp.int32, (REL_BUCKETS, 128), 0)
        lane = lax.broadcasted_iota(jnp.int32, (REL_BUCKETS, 128), 1)
        out = jnp.zeros((REL_BUCKETS, 128), F32)
        for p, (_, dil) in enumerate(DIL_PATTERNS):
            bkt = bkt_refs[p][...]
            for h in range(N_HEADS):
                d = d_refs[2 * p][h] + d_refs[2 * p + 1][h]
                for b in _dil_live_buckets(dil):
                    val = jnp.sum(jnp.where(bkt == b, d, 0.0), keepdims=True)
                    out = out + jnp.where((row == b) & (lane == h), val, 0.0)
        o_ref[...] = out

    return pl.pallas_call(
        body, name=name, out_shape=jax.ShapeDtypeStruct((REL_BUCKETS, 128), F32), compiler_params=_cparams(),
    )(*bkts, *dscs)


DPREP_TM = 512


def _dprep_fn(qp, kp, vp, gq, gk, cq, sq, ck, sk, mavg_q, mavg_k, perm_q, perm_k, expand):
    rot = lambda x, perm: jnp.dot(x, perm, precision=SPLIT3, preferred_element_type=F32)
    qn = _headnorm(qp, gq, mavg_q)
    kn = _headnorm(kp, gk, mavg_k)
    qr = (qn * cq + rot(qn, perm_q) * sq) * (HEAD_DIM ** -0.5)
    kr = kn * ck + rot(kn, perm_k) * sk
    return qr, rot(kr, expand), rot(vp, expand)


def _dprep_consts():
    cq, sq = _np_rope_tables(N_HEADS)
    ck, sk = _np_rope_tables(KV_WIDTH // HEAD_DIM)
    tables = [jnp.asarray(a) for a in (cq, sq, ck, sk)]
    mats = [jnp.asarray(a) for a in (_np_group_avg(GROUP_WIDTH), _np_group_avg(KV_WIDTH), _np_rope_partner(GROUP_WIDTH),
                                      _np_rope_partner(KV_WIDTH), _np_kv_expand())]
    per = SEQ // DPREP_TM
    w, kw = GROUP_WIDTH, KV_WIDTH
    table_rows = [(tables[0], w, 0, per), (tables[1], w, 0, per), (tables[2], kw, 0, per), (tables[3], kw, 0, per)]
    return table_rows, mats


def _dprep_fwd(z, gq, gk, name):
    table_rows, mats = _dprep_consts()
    w, kw = GROUP_WIDTH, KV_WIDTH

    def fn(qp, kp, vp, cq, sq, ck, sk, gqv, gkv, *m):
        return _dprep_fn(qp, kp, vp, gqv, gkv, cq, sq, ck, sk, *m)

    return _rowmap(fn, [(z, w, 7), (z, kw, 32), (z, kw, 33)] + table_rows, [gq, gk] + mats, [(w, BF16)] * 3, [], name,
                   DPREP_TM, z.shape[0])


def _dprep_bwd(z, dq, dkx, dvx, gq, gk, name):
    table_rows, mats = _dprep_consts()
    fold_q = jnp.asarray(_np_tile_fold(GROUP_WIDTH))
    fold_k = jnp.asarray(_np_tile_fold(KV_WIDTH))
    w, kw = GROUP_WIDTH, KV_WIDTH

    def fn(qp, kp, vp, dqv, dkv, dvv, cq, sq, ck, sk, gqv, gkv, fq, fk, *m):
        f = lambda a, b, c, d, e: _dprep_fn(a, b, c, d, e, cq, sq, ck, sk, *m)
        _, vjp = jax.vjp(f, qp, kp, vp, gqv, gkv)
        dqp, dkp, dvp, dgq, dgk = vjp((dqv, dkv, dvv))
        return dqp, dkp, dvp, _fold_gain(dgq, fq), _fold_gain(dgk, fk)

    return _rowmap(fn, [(z, w, 7), (z, kw, 32), (z, kw, 33), (dq, w, 0), (dkx, w, 0), (dvx, w, 0)] + table_rows,
                   [gq, gk, fold_q, fold_k] + mats, [(w, F32), (kw, F32), (kw, F32)], [(8, HEAD_DIM)] * 2, name,
                   DPREP_TM, z.shape[0])


GQA_QB = 256


def _gqa_fwd(q, kx, vx, name):
    bsz = q.shape[0]
    blk = pl.BlockSpec((None, GQA_QB, GROUP_WIDTH), lambda b, i: (b, i, 0))
    seq = pl.BlockSpec((None, SEQ, GROUP_WIDTH), lambda b, i: (b, 0, 0))

    def body(q_ref, k_ref, v_ref, o_ref):
        for m in range(N_HEADS // 2):
            lanes = pl.ds(m * 128, 128)
            o_ref[:, lanes] = _attn_pair_fwd(q_ref[:, lanes], k_ref[:, lanes], v_ref[:, lanes], None, None)[0]

    return pl.pallas_call(
        body, name=name, grid=(bsz, SEQ // GQA_QB), in_specs=[blk, seq, seq], out_specs=blk,
        out_shape=jax.ShapeDtypeStruct((bsz, SEQ, GROUP_WIDTH), F32), compiler_params=_cparams(("parallel", "parallel")),
    )(q, kx, vx)


def _gqa_bwd(q, kx, vx, do, name):
    bsz = q.shape[0]
    blk = pl.BlockSpec((None, GQA_QB, GROUP_WIDTH), lambda b, i: (b, i, 0))
    seq = pl.BlockSpec((None, SEQ, GROUP_WIDTH), lambda b, i: (b, 0, 0))

    def body(q_ref, k_ref, v_ref, do_ref, dq_ref, dk_ref, dv_ref):
        @pl.when(pl.program_id(1) == 0)
        def _():
            dk_ref[...] = jnp.zeros_like(dk_ref)
            dv_ref[...] = jnp.zeros_like(dv_ref)

        for m in range(N_HEADS // 2):
            lanes = pl.ds(m * 128, 128)
            dq2, dk2, dv2, _ = _attn_pair_bwd(q_ref[:, lanes], k_ref[:, lanes], v_ref[:, lanes], None, None, do_ref[:, lanes], None)
            dq_ref[:, lanes] = dq2
            dk_ref[:, lanes] += dk2
            dv_ref[:, lanes] += dv2

    out = jax.ShapeDtypeStruct((bsz, SEQ, GROUP_WIDTH), F32)
    return pl.pallas_call(
        body, name=name, grid=(bsz, SEQ // GQA_QB), in_specs=[blk, seq, seq, blk], out_specs=[blk, seq, seq],
        out_shape=[out, out, out], compiler_params=_cparams(("parallel", "arbitrary")),
    )(q, kx, vx, do)


CONV_TILE = 64
CONV_LEAD = 16
CONV_WINDOW = CONV_TILE + 32


def _glu(a, g):
    return a * jax.nn.sigmoid(g)


def _conv_post(c, b, ln_g, ln_b):
    x = c + b
    xc = x - jnp.mean(x, axis=-1, keepdims=True)
    y = xc * lax.rsqrt(jnp.mean(xc * xc, axis=-1, keepdims=True) + LN_EPS) * ln_g + ln_b
    return y * jax.nn.sigmoid(y)


def _conv_shifts(win):
    out = []
    for phase in range(8):
        rolled = win if phase == 0 else pltpu.roll(win, CONV_WINDOW - phase, 0)
        for base in range(0, CONV_WINDOW - CONV_TILE + 1, 8):
            if 1 <= base + phase <= CONV_WIDTH:
                out.append((base + phase, rolled[base:base + CONV_TILE]))
    return out


def _conv_fill(pad_ref, value_of_tile):
    zeros = jnp.zeros((CONV_LEAD, GROUP_WIDTH), F32)
    pad_ref[pl.ds(0, CONV_LEAD), :] = zeros
    pad_ref[pl.ds(CONV_LEAD + SEQ, CONV_LEAD), :] = zeros

    def step(t, carry):
        r0 = pl.multiple_of(t * CONV_TILE, CONV_TILE)
        pad_ref[pl.ds(CONV_LEAD + r0, CONV_TILE), :] = value_of_tile(r0)
        return carry

    lax.fori_loop(0, SEQ // CONV_TILE, step, 0)


def _conv_tile(pad_ref, w_ref, r0, flip):
    acc = jnp.zeros((CONV_TILE, GROUP_WIDTH), F32)
    for offset, rows in _conv_shifts(pad_ref[pl.ds(r0, CONV_WINDOW), :]):
        k = (CONV_WIDTH - offset) if flip else (offset - 1)
        acc = acc + w_ref[pl.ds(k, 1), :] * rows
    return acc


def _conv_fwd(z3, w, b, ln_g, ln_b, name):
    bsz = z3.shape[0]
    seq = lambda cb: pl.BlockSpec((None, SEQ, GROUP_WIDTH), functools.partial(lambda i, cb: (i, 0, cb), cb=cb))
    full = lambda a: pl.BlockSpec(a.shape, lambda i: (0,) * a.ndim)

    def body(a_ref, g_ref, w_ref, b_ref, lg_ref, lb_ref, y_ref, pad_ref):
        _conv_fill(pad_ref, lambda r0: _glu(a_ref[pl.ds(r0, CONV_TILE), :], g_ref[pl.ds(r0, CONV_TILE), :]))

        def step(t, carry):
            r0 = pl.multiple_of(t * CONV_TILE, CONV_TILE)
            y_ref[pl.ds(r0, CONV_TILE), :] = _conv_post(_conv_tile(pad_ref, w_ref, r0, False), b_ref[...], lg_ref[...], lb_ref[...])
            return carry

        lax.fori_loop(0, SEQ // CONV_TILE, step, 0)

    return pl.pallas_call(
        body, name=name, grid=(bsz,), in_specs=[seq(5), seq(6), full(w), full(b), full(ln_g), full(ln_b)], out_specs=seq(0),
        out_shape=jax.ShapeDtypeStruct((bsz, SEQ, GROUP_WIDTH), F32),
        scratch_shapes=[pltpu.VMEM((SEQ + 2 * CONV_LEAD, GROUP_WIDTH), F32)], compiler_params=_cparams(("parallel",)),
    )(z3, z3, w, b, ln_g, ln_b)


def _conv_bwd(z3, dy, w, b, ln_g, ln_b, name):
    bsz = z3.shape[0]
    seq = lambda cb: pl.BlockSpec((None, SEQ, GROUP_WIDTH), functools.partial(lambda i, cb: (i, 0, cb), cb=cb))
    full = lambda a: pl.BlockSpec(a.shape, lambda i: (0,) * a.ndim)
    vec = pl.BlockSpec((1, GROUP_WIDTH), lambda i: (0, 0))

    def body(a_ref, g_ref, dy_ref, w_ref, b_ref, lg_ref, lb_ref, da_ref, dg_ref, dw_ref, db_ref, dlg_ref, dlb_ref, hpad, dpad, dw8):
        @pl.when(pl.program_id(0) == 0)
        def _():
            dw8[...] = jnp.zeros_like(dw8)
            db_ref[...] = jnp.zeros_like(db_ref)
            dlg_ref[...] = jnp.zeros_like(dlg_ref)
            dlb_ref[...] = jnp.zeros_like(dlb_ref)

        _conv_fill(hpad, lambda r0: _glu(a_ref[pl.ds(r0, CONV_TILE), :], g_ref[pl.ds(r0, CONV_TILE), :]))
        zeros = jnp.zeros((CONV_LEAD, GROUP_WIDTH), F32)
        dpad[pl.ds(0, CONV_LEAD), :] = zeros
        dpad[pl.ds(CONV_LEAD + SEQ, CONV_LEAD), :] = zeros

        def through_post(t, carry):
            r0 = pl.multiple_of(t * CONV_TILE, CONV_TILE)
            conv = _conv_tile(hpad, w_ref, r0, False)
            _, vjp = jax.vjp(_conv_post, conv, b_ref[...], lg_ref[...], lb_ref[...])
            dconv, db, dlg, dlb = vjp(dy_ref[pl.ds(r0, CONV_TILE), :])
            db_ref[...] += db
            dlg_ref[...] += dlg
            dlb_ref[...] += dlb
            dpad[pl.ds(CONV_LEAD + r0, CONV_TILE), :] = dconv
            for offset, rows in _conv_shifts(hpad[pl.ds(r0, CONV_WINDOW), :]):
                prod = dconv * rows
                part = prod[0:8]
                for j in range(1, CONV_TILE // 8):
                    part = part + prod[8 * j:8 * j + 8]
                dw8[offset - 1] += part
            return carry

        lax.fori_loop(0, SEQ // CONV_TILE, through_post, 0)

        def through_glu(t, carry):
            r0 = pl.multiple_of(t * CONV_TILE, CONV_TILE)
            dh = _conv_tile(dpad, w_ref, r0, True)
            rows = pl.ds(r0, CONV_TILE)
            _, vjp = jax.vjp(_glu, a_ref[rows, :], g_ref[rows, :])
            da, dg = vjp(dh)
            da_ref[rows, :] = da
            dg_ref[rows, :] = dg
            return carry

        lax.fori_loop(0, SEQ // CONV_TILE, through_glu, 0)
        dw_ref[...] = jnp.sum(dw8[...], axis=1)

    out = jax.ShapeDtypeStruct((bsz, SEQ, GROUP_WIDTH), F32)
    v = jax.ShapeDtypeStruct((1, GROUP_WIDTH), F32)
    return pl.pallas_call(
        body, name=name, grid=(bsz,), in_specs=[seq(5), seq(6), seq(0), full(w), full(b), full(ln_g), full(ln_b)],
        out_specs=[seq(0), seq(0), pl.BlockSpec((CONV_WIDTH, GROUP_WIDTH), lambda i: (0, 0)), vec, vec, vec],
        out_shape=[out, out, jax.ShapeDtypeStruct((CONV_WIDTH, GROUP_WIDTH), F32), v, v, v],
        scratch_shapes=[pltpu.VMEM((SEQ + 2 * CONV_LEAD, GROUP_WIDTH), F32), pltpu.VMEM((SEQ + 2 * CONV_LEAD, GROUP_WIDTH), F32),
                        pltpu.VMEM((CONV_WIDTH, 8, GROUP_WIDTH), F32)],
        compiler_params=_cparams(("arbitrary",)),
    )(z3, z3, dy, w, b, ln_g, ln_b)


def _mixnorm_fwd(ys, gains, name):
    w = GROUP_WIDTH

    def fn(*v):
        return (jnp.concatenate([_rms(v[i], v[4 + i]) for i in range(4)], axis=-1),)

    return _rowmap(fn, [(y, w, 0) for y in ys], list(gains), [(4 * w, BF16)], [], name, 512, ys[0].shape[0])[0]


def _mixnorm_bwd(dyn, ys, gains, name, follow=None):
    w = GROUP_WIDTH
    gains = list(gains) if follow is None else [*gains, follow]

    def fn(*v):
        dys, dgs = [], []
        for i in range(4):
            _, vjp = jax.vjp(_rms, v[4 + i], v[8 + i])
            dy, dg = vjp(v[i])
            dys.append(dy)
            dgs.append(dg)
        return (*dys, *dgs)

    rows = [(dyn, w, i) for i in range(4)] + [(y, w, 0) for y in ys]
    return _rowmap(fn, rows, list(gains), [(w, F32)] * 4, [(1, w)] * 4, name, 512, dyn.shape[0])


def _adamw_fn(w, g, m, v):
    m = ADAM_B1 * m + (1.0 - ADAM_B1) * g
    v = ADAM_B2 * v + (1.0 - ADAM_B2) * (g * g)
    m_hat = m / (1.0 - ADAM_B1 ** ADAM_STEP)
    v_hat = v / (1.0 - ADAM_B2 ** ADAM_STEP)
    delta = -ADAM_LR * (m_hat / (jnp.sqrt(v_hat) + ADAM_EPS) + ADAM_WD * w)
    return delta, m, v


def _adamw(w, g, m, v, name):
    r, c = w.shape
    tm = _pick(r, (256, 128, 64, 32, 16, 8))
    return _rowmap(_adamw_fn, [(a, c, 0) for a in (w, g, m, v)], [], [(c, F32)] * 3, [], name, tm, r)


def _layer_params(l, small, big, prefetch):
    tile_row = lambda g, n: jnp.tile(g, n)[None, :]
    row = lambda g: g[None, :]
    w_s = small["sgu_w"][l].astype(BF16)
    return dict(
        norm1_g=row(small["norm1_g"][l]), norm2_g=row(small["norm2_g"][l]),
        w_s=w_s, w_s_t=jnp.swapaxes(w_s, 1, 2), bm=jnp.repeat(small["sgu_b"][l].T, HEAD_DIM, axis=1),
        gq_dil=tile_row(small["dil_qn_g"][l], N_HEADS), gk_dil=tile_row(small["dil_kn_g"][l], N_HEADS),
        conv_w=small["conv_w"][l], conv_b=row(small["conv_b"][l]), conv_ln_g=row(small["conv_ln_g"][l]),
        conv_ln_b=row(small["conv_ln_b"][l]),
        gq_gqa=tile_row(small["gqa_qn_g"][l], N_HEADS), gk_gqa=tile_row(small["gqa_kn_g"][l], KV_WIDTH // HEAD_DIM),
        mix_g=[row(small["mix_norm_g"][l][i * GROUP_WIDTH:(i + 1) * GROUP_WIDTH]) for i in range(4)],
        big=big, prefetch=prefetch,
    )


def _layer_fwd(x, p, table, bsz, tag):
    t = x.shape[0]
    seq3 = lambda a: a.reshape(bsz, SEQ, a.shape[-1])
    flat = lambda a: a.reshape(t, a.shape[-1])
    h1 = _rmsnorm_fwd(x, p["norm1_g"], tag + "rms1")
    z = _matmul([(h1, p["big"]("w_in", h1))], "nn", F32, tag + "mm_z")
    y_a = _sgu_fwd(z, p["w_s"], p["bm"], tag + "sgu_fwd")
    p["prefetch"](0, "w_out", y_a)
    dil_qkv = _bprep_fwd(z, p["gq_dil"], p["gk_dil"], tag + "dil_prep")
    outs, lses = [], []
    for (_, dil), (qb, kb, vb) in zip(DIL_PATTERNS, dil_qkv):
        o, lse = _dil_fwd(qb, kb, vb, table, dil, f"{tag}dil{dil}_fwd")
        outs.append(o)
        lses.append(lse)
    y_b = _mixture_fwd(outs, lses, tag + "dil_mix")
    y_c = flat(_conv_fwd(seq3(z), p["conv_w"], p["conv_b"], p["conv_ln_g"], p["conv_ln_b"], tag + "conv_fwd"))
    qd, kx, vx = _dprep_fwd(z, p["gq_gqa"], p["gk_gqa"], tag + "gqa_prep")
    y_d = flat(_gqa_fwd(seq3(qd), seq3(kx), seq3(vx), tag + "gqa_fwd"))
    ys = [y_a, y_b, y_c, y_d]
    p["prefetch"](0, "w_gate", y_d)
    yn = _mixnorm_fwd(ys, p["mix_g"], tag + "mixnorm")
    x_mid, h2 = _matmul_residual_rms(yn, p["big"]("w_out", yn), x, p["norm2_g"], tag + "mm_out_rms2")
    act, act_du, act_dg = _ffn_up(h2, p["big"]("w_gate", h2), p["big"]("w_up", h2), tag + "ffn_up")
    p["prefetch"](1, "w_in", act)
    x_out = _matmul([(act, p["big"]("w_down", act))], "nn", F32, tag + "mm_down", residual=x_mid)
    saved = dict(x=x, h1=h1, z=z, dil_qkv=dil_qkv, outs=outs, lses=lses, qd=qd, kx=kx, vx=vx, ys=ys, yn=yn, x_mid=x_mid,
                 h2=h2, act=act, act_du=act_du, act_dg=act_dg)
    return x_out, saved


def _layer_bwd(dx_out, dx_out_b, s, p, table, bsz, tag, emit, mid_hook):
    t = dx_out.shape[0]
    seq3 = lambda a: a.reshape(bsz, SEQ, a.shape[-1])
    flat = lambda a: a.reshape(t, a.shape[-1])
    z = s["z"]
    small = {}
    weight = lambda name: p["big"](name, None)
    emit("w_down", _matmul([(s["act"], dx_out_b)], "tn", BF16, tag + "mm_dwdown").reshape(N_CHIPS, FFN_HIDDEN // N_CHIPS, D_MODEL))
    dgate, dup = _ffn_down_bwd(dx_out_b, weight("w_down"), s["act_du"], s["act_dg"], tag + "ffn_dact")
    emit("w_gate", _matmul([(s["h2"], dgate)], "tn", BF16, tag + "mm_dwgate", slabs=N_CHIPS))
    started = emit("w_up", _matmul([(s["h2"], dup)], "tn", BF16, tag + "mm_dwup", slabs=N_CHIPS))
    dh2 = _matmul([(dgate, weight("w_gate")), (dup, weight("w_up"))], "nt", F32, tag + "mm_dh2")
    dx_mid, dx_mid_b, dg2 = _rmsnorm_bwd(dh2, s["x_mid"], p["norm2_g"], dx_out, tag + "rms2_bwd", follow=started)
    small["norm2_g"] = dg2[0]
    mid_hook(dx_mid)
    dyn = _matmul([(dx_mid_b, weight("w_out"))], "nt", F32, tag + "mm_dyn")
    started = emit("w_out", _matmul([(s["yn"], dx_mid_b)], "tn", BF16, tag + "mm_dwout").reshape(N_CHIPS, D_MODEL // N_CHIPS, D_MODEL))
    *dys, dga, dgb, dgc, dgd = _mixnorm_bwd(dyn, s["ys"], p["mix_g"], tag + "mixnorm_bwd", follow=started)
    small["mix_norm_g"] = jnp.concatenate([dga[0], dgb[0], dgc[0], dgd[0]])
    du, dv, dws, dbs = _sgu_bwd(z, dys[0], p["w_s"], p["w_s_t"], p["bm"], tag + "sgu_bwd")
    small["sgu_w"] = dws
    small["sgu_b"] = dbs[:, :N_HEADS].T
    *douts, dl0, dl1, dl2 = _mixture_bwd(s["outs"], s["lses"], dys[1], tag + "dil_mix_bwd")
    dlses = [dl0, dl1, dl2]
    dqs, dks, dvs, dscs = [], [], [], []
    for i, (_, dil) in enumerate(DIL_PATTERNS):
        dq, dk, dvv, dsc = _dil_bwd(*s["dil_qkv"][i], douts[i], dlses[i], table, dil, f"{tag}dil{dil}_bwd")
        dqs.append(dq)
        dks.append(dk)
        dvs.append(dvv)
        dscs.append(dsc)
    dbq, dbk, dbv, dgq, dgk = _bprep_bwd(z, dqs, dks, dvs, p["gq_dil"], p["gk_dil"], tag + "dil_prep_bwd")
    small["dil_qn_g"], small["dil_kn_g"] = dgq[0], dgk[0]
    dca, dcg, dcw, dcb, dclg, dclb = _conv_bwd(seq3(z), seq3(dys[2]), p["conv_w"], p["conv_b"], p["conv_ln_g"], p["conv_ln_b"],
                                               tag + "conv_bwd")
    small["conv_w"], small["conv_b"], small["conv_ln_g"], small["conv_ln_b"] = dcw, dcb[0], dclg[0], dclb[0]
    dqd, dkx, dvx = _gqa_bwd(seq3(s["qd"]), seq3(s["kx"]), seq3(s["vx"]), seq3(dys[3]), tag + "gqa_bwd")
    ddq, ddk, ddv, dgq, dgk = _dprep_bwd(z, flat(dqd), flat(dkx), flat(dvx), p["gq_gqa"], p["gk_gqa"], tag + "gqa_prep_bwd")
    small["gqa_qn_g"], small["gqa_kn_g"] = dgq[0], dgk[0]
    dz = jnp.concatenate([a.astype(BF16) for a in (du, dv, dbq, dbk, dbv, flat(dca), flat(dcg), ddq, ddk, ddv)], axis=1)
    dz4 = dz.reshape(t, N_CHIPS, IN_WIDTH // N_CHIPS).transpose(1, 0, 2)
    started = emit("w_in", _matmul([(s["h1"], dz4)], "tn", BF16, tag + "mm_dwin", slabs=N_CHIPS))
    dh1 = _matmul([(dz, weight("w_in"))], "nt", F32, tag + "mm_dh1")
    dx, dx_b, dg1 = _rmsnorm_bwd(dh1, s["x"], p["norm1_g"], dx_mid, tag + "rms1_bwd", follow=started)
    small["norm1_g"] = dg1[0]
    return dx, dx_b, small, dscs


def _local_step(x, target, small, big, emit, mid_hook, bsz, prefetch=lambda l, name, after: None):
    table = small["rel_bias"]
    ahead = lambda l: (lambda more, name, after: prefetch(l + more, name, after))
    params = [_layer_params(l, small, functools.partial(big, l), ahead(l)) for l in range(DEPTH)]
    saved = []
    h = x
    for l in range(DEPTH):
        h, sv = _layer_fwd(h, params[l], table, bsz, f"l{l}_")
        saved.append(sv)
    dh, dh_b, loss = _loss_fwd_bwd(h, target, "loss")
    small_grads, dscs = [None] * DEPTH, [None] * DEPTH
    for l in reversed(range(DEPTH)):
        dh, dh_b, small_grads[l], dscs[l] = _layer_bwd(dh, dh_b, saved[l], params[l], table, bsz, f"l{l}_",
                                                       functools.partial(emit, l), functools.partial(mid_hook, l))
    fold_in = [dscs[l][i] for i in range(len(DIL_PATTERNS)) for l in range(DEPTH)]
    stacked = {k: jnp.stack([small_grads[l][k] for l in range(DEPTH)]) for k in small_grads[0]}
    stacked["rel_bias"] = _relbias_fold(fold_in, "relbias_fold")[:, :N_HEADS]
    return loss, dh, stacked


def _mesh_pos():
    return lax.axis_index("x"), lax.axis_index("y"), lax.axis_index("c")


def _other_chips(x, y):
    return [(1 - x, y), (x, 1 - y), (1 - x, 1 - y)]


_ANY = pl.BlockSpec(memory_space=pl.ANY)


def _swap_sibling(arrs, name):
    n = len(arrs)

    def body(*refs):
        in_refs, out_refs, send_sems, recv_sems = refs[:n], refs[n:2 * n], refs[2 * n], refs[2 * n + 1]
        x, y, c = _mesh_pos()
        copies = [pltpu.make_async_remote_copy(src_ref=in_refs[k], dst_ref=out_refs[k], send_sem=send_sems.at[k],
                                               recv_sem=recv_sems.at[k], device_id=(x, y, 1 - c), device_id_type=MESH)
                  for k in range(n)]
        for cp in copies:
            cp.start()
        for cp in copies:
            cp.wait()

    return pl.pallas_call(
        body, name=name, in_specs=[_ANY] * n, out_specs=[_ANY] * n,
        out_shape=[jax.ShapeDtypeStruct(a.shape, a.dtype) for a in arrs],
        scratch_shapes=[pltpu.SemaphoreType.DMA((n,)), pltpu.SemaphoreType.DMA((n,))],
    )(*arrs)


def _complete_pairs(arrs, name):
    n = len(arrs)

    def body(*refs):
        in_refs, out_refs, send_sems, recv_sems = refs[:n], refs[n:2 * n], refs[2 * n], refs[2 * n + 1]
        x, y, c = _mesh_pos()
        copies = [pltpu.make_async_remote_copy(src_ref=in_refs[k].at[c], dst_ref=out_refs[k].at[c], send_sem=send_sems.at[k],
                                               recv_sem=recv_sems.at[k], device_id=(x, y, 1 - c), device_id_type=MESH)
                  for k in range(n)]
        for cp in copies:
            cp.start()
        for k, cp in enumerate(copies):
            cp.wait_send()
            pltpu.make_async_remote_copy(src_ref=in_refs[k].at[1 - c], dst_ref=out_refs[k].at[1 - c], send_sem=send_sems.at[k],
                                         recv_sem=recv_sems.at[k], device_id=(x, y, 1 - c), device_id_type=MESH).wait_recv()

    return pl.pallas_call(
        body, name=name, in_specs=[_ANY] * n, out_specs=[_ANY] * n,
        out_shape=[jax.ShapeDtypeStruct(a.shape, a.dtype) for a in arrs], input_output_aliases={k: k for k in range(n)},
        scratch_shapes=[pltpu.SemaphoreType.DMA((n,)), pltpu.SemaphoreType.DMA((n,))],
    )(*arrs)


_HBM = pl.BlockSpec(memory_space=pltpu.HBM)
_SEM = pl.BlockSpec(memory_space=pltpu.SEMAPHORE)
_DATAFLOW = pltpu.SideEffectType.DATAFLOW_SIDE_EFFECTING


def _chip_copies(src_refs, land_refs, send_sems, recv_sems, scatter):
    x, y, c = _mesh_pos()
    me = 2 * x + y
    out = []
    for k, (src_ref, land_ref) in enumerate(zip(src_refs, land_refs)):
        if scatter:
            h = src_ref.shape[1] // 2
            for q in range(N_DEV - 1):
                fx, fy, fc = ((q + 1) >> 2) & 1, ((q + 1) >> 1) & 1, (q + 1) & 1
                px, py, pc = (1 - x if fx else x), (1 - y if fy else y), (1 - c if fc else c)
                src = src_ref.at[2 * px + py, pl.ds(pc * h, h)]
                sems = dict(send_sem=send_sems.at[7 * k + q], recv_sem=recv_sems.at[7 * k + q], device_id=(px, py, pc), device_id_type=MESH)
                out.append((pltpu.make_async_remote_copy(src_ref=src, dst_ref=land_ref.at[4 * x + 2 * y + c], **sems),
                            pltpu.make_async_remote_copy(src_ref=src, dst_ref=land_ref.at[4 * px + 2 * py + pc], **sems)))
            continue
        slot = (lambda chip: land_ref.at[c, chip]) if len(land_ref.shape) == 4 else (lambda chip: land_ref.at[chip])
        for j, (cx, cy) in enumerate(_other_chips(x, y)):
            sems = dict(send_sem=send_sems.at[3 * k + j], recv_sem=recv_sems.at[3 * k + j], device_id=(cx, cy, c), device_id_type=MESH)
            out.append((pltpu.make_async_remote_copy(src_ref=src_ref, dst_ref=slot(me), **sems),
                        pltpu.make_async_remote_copy(src_ref=src_ref, dst_ref=slot(2 * cx + cy), **sems)))
    return out


def _chips_start(srcs, scatter, after, name, per_core=False):
    n = len(srcs)
    n_sems = (N_DEV - 1 if scatter else N_CHIPS - 1) * n
    if scatter:
        lands = [lax.empty((N_DEV, s.shape[1] // 2, s.shape[2]), s.dtype) for s in srcs]
    else:
        lands = [lax.empty((*((2, N_CHIPS) if per_core else (N_CHIPS,)), *s.shape), s.dtype) for s in srcs]

    def body(*refs):
        src_refs, land_refs = refs[:n], refs[n:2 * n]
        send_sems, recv_sems, token = refs[2 * n + 1], refs[2 * n + 2], refs[-1]
        for sent, _ in _chip_copies(src_refs, land_refs, send_sems, recv_sems, scatter):
            sent.start()
        token[...] = jnp.zeros_like(token)

    hbm = lambda a: pltpu.HBM(a.shape, a.dtype)
    res = pl.pallas_call(
        body, name=name,
        in_specs=[_HBM] * (2 * n) + [_ANY],
        out_specs=[_SEM, _SEM] + [_HBM] * (2 * n) + [pl.BlockSpec(memory_space=pltpu.VMEM)],
        out_shape=[pltpu.SemaphoreType.DMA((n_sems,)), pltpu.SemaphoreType.DMA((n_sems,))] + [hbm(a) for a in srcs] + [hbm(a) for a in lands]
        + [jax.ShapeDtypeStruct((8, 128), F32)],
        input_output_aliases={i: 2 + i for i in range(2 * n)},
        compiler_params=pltpu.CompilerParams(has_side_effects=_DATAFLOW),
    )(*[pltpu.with_memory_space_constraint(a, pltpu.HBM) for a in (*srcs, *lands)], after)
    return (res[0], res[1], res[2:2 + n], res[2 + n:2 + 2 * n]), res[-1]


def _chips_wait(handle, scatter, after, name):
    send_sems, recv_sems, srcs, lands = handle
    n = len(srcs)

    def body(*refs):
        src_refs, land_refs = refs[:n], refs[n:2 * n]
        send_sems, recv_sems = refs[2 * n], refs[2 * n + 1]
        for sent, landed in _chip_copies(src_refs, land_refs, send_sems, recv_sems, scatter):
            sent.wait_send()
            landed.wait_recv()

    hbm = lambda a: pltpu.HBM(a.shape, a.dtype)
    res = pl.pallas_call(
        body, name=name,
        in_specs=[_HBM] * (2 * n) + [_SEM, _SEM, _ANY], out_specs=[_HBM] * (2 * n),
        out_shape=[hbm(a) for a in srcs] + [hbm(a) for a in lands],
        input_output_aliases={i: i for i in range(2 * n)},
        compiler_params=pltpu.CompilerParams(has_side_effects=_DATAFLOW),
    )(*srcs, *lands, send_sems, recv_sems, after)
    return res[n:]


N_DEV = 8


def _everyone_copies(src_ref, land_ref, send_sems, recv_sems):
    x, y, c = _mesh_pos()
    out = []
    for q in range(N_DEV - 1):
        fx, fy, fc = ((q + 1) >> 2) & 1, ((q + 1) >> 1) & 1, (q + 1) & 1
        px, py, pc = (1 - x if fx else x), (1 - y if fy else y), (1 - c if fc else c)
        sems = dict(send_sem=send_sems.at[q], recv_sem=recv_sems.at[q], device_id=(px, py, pc), device_id_type=MESH)
        out.append((pltpu.make_async_remote_copy(src_ref=src_ref, dst_ref=land_ref.at[4 * x + 2 * y + c], **sems),
                    pltpu.make_async_remote_copy(src_ref=src_ref, dst_ref=land_ref.at[4 * px + 2 * py + pc], **sems)))
    return out


def _everyone_start(block, after, name):
    land = lax.empty((N_DEV, *block.shape), block.dtype)

    def body(src_ref, land_ref, after_ref, send_sems, recv_sems, src_thru, land_thru, token):
        for sent, _ in _everyone_copies(src_ref, land_ref, send_sems, recv_sems):
            sent.start()
        token[...] = jnp.zeros_like(token)

    hbm = lambda a: pltpu.HBM(a.shape, a.dtype)
    n_sem = N_DEV - 1
    res = pl.pallas_call(
        body, name=name, in_specs=[_HBM, _HBM, _ANY],
        out_specs=[_SEM, _SEM, _HBM, _HBM, pl.BlockSpec(memory_space=pltpu.VMEM)],
        out_shape=[pltpu.SemaphoreType.DMA((n_sem,)), pltpu.SemaphoreType.DMA((n_sem,)), hbm(block), hbm(land),
                   jax.ShapeDtypeStruct((8, 128), F32)],
        input_output_aliases={0: 2, 1: 3}, compiler_params=pltpu.CompilerParams(has_side_effects=_DATAFLOW),
    )(pltpu.with_memory_space_constraint(block, pltpu.HBM), pltpu.with_memory_space_constraint(land, pltpu.HBM), after)
    return res[:4]


def _everyone_wait(handle, after, name):
    send_sems, recv_sems, block, land = handle

    def body(src_ref, land_ref, send_sems, recv_sems, after_ref, src_thru, land_thru):
        for sent, landed in _everyone_copies(src_ref, land_ref, send_sems, recv_sems):
            sent.wait_send()
            landed.wait_recv()

    hbm = lambda a: pltpu.HBM(a.shape, a.dtype)
    return pl.pallas_call(
        body, name=name, in_specs=[_HBM, _HBM, _SEM, _SEM, _ANY], out_specs=[_HBM, _HBM], out_shape=[hbm(block), hbm(land)],
        input_output_aliases={0: 0, 1: 1}, compiler_params=pltpu.CompilerParams(has_side_effects=_DATAFLOW),
    )(block, land, send_sems, recv_sems, after)


def _pair_copies(arr_refs, send_sems, recv_sems):
    x, y, c = _mesh_pos()
    out = []
    for k, ref in enumerate(arr_refs):
        sems = dict(send_sem=send_sems.at[k], recv_sem=recv_sems.at[k], device_id=(x, y, 1 - c), device_id_type=MESH)
        out.append((pltpu.make_async_remote_copy(src_ref=ref.at[c], dst_ref=ref.at[c], **sems),
                    pltpu.make_async_remote_copy(src_ref=ref.at[1 - c], dst_ref=ref.at[1 - c], **sems)))
    return out


def _pairs_start(arrs, after, name):
    n = len(arrs)

    def body(*refs):
        send_sems, recv_sems = refs[n + 1], refs[n + 2]
        for sent, _ in _pair_copies(refs[:n], send_sems, recv_sems):
            sent.start()
        refs[-1][...] = jnp.zeros_like(refs[-1])

    hbm = lambda a: pltpu.HBM(a.shape, a.dtype)
    res = pl.pallas_call(
        body, name=name, in_specs=[_HBM] * n + [_ANY],
        out_specs=[_SEM, _SEM] + [_HBM] * n + [pl.BlockSpec(memory_space=pltpu.VMEM)],
        out_shape=[pltpu.SemaphoreType.DMA((n,)), pltpu.SemaphoreType.DMA((n,))] + [hbm(a) for a in arrs] + [jax.ShapeDtypeStruct((8, 128), F32)],
        input_output_aliases={i: 2 + i for i in range(n)}, compiler_params=pltpu.CompilerParams(has_side_effects=_DATAFLOW),
    )(*[pltpu.with_memory_space_constraint(a, pltpu.HBM) for a in arrs], after)
    return res[0], res[1], res[2:2 + n]


def _pairs_wait(handle, after, name):
    send_sems, recv_sems, arrs = handle
    n = len(arrs)

    def body(*refs):
        for sent, landed in _pair_copies(refs[:n], refs[n], refs[n + 1]):
            sent.wait_send()
            landed.wait_recv()

    hbm = lambda a: pltpu.HBM(a.shape, a.dtype)
    return pl.pallas_call(
        body, name=name, in_specs=[_HBM] * n + [_SEM, _SEM, _ANY], out_specs=[_HBM] * n, out_shape=[hbm(a) for a in arrs],
        input_output_aliases={i: i for i in range(n)}, compiler_params=pltpu.CompilerParams(has_side_effects=_DATAFLOW),
    )(*arrs, send_sems, recv_sems, after)


def _allgather_sum_small(block, name):
    m_per, n = block.shape

    def body(x_ref, out_ref, sum_ref, send_sems, recv_sems, local_sem):
        x, y, c = _mesh_pos()
        me, sibling = (x, y, c), (x, y, 1 - c)
        chips = _other_chips(x, y)

        def rows(px, py, pc):
            return out_ref.at[pl.ds((4 * px + 2 * py + pc) * m_per, m_per), :]

        def copy(k, blk, to, src=None):
            return pltpu.make_async_remote_copy(src_ref=rows(*blk) if src is None else src, dst_ref=rows(*blk),
                                                send_sem=send_sems.at[k], recv_sem=recv_sems.at[k], device_id=to, device_id_type=MESH)

        mine = pltpu.make_async_copy(x_ref, rows(*me), local_sem)
        mine.start()
        first = [copy(0, me, sibling, src=x_ref)]
        first += [copy(1 + j, me, (*chip, c), src=x_ref) for j, chip in enumerate(chips)]
        for cp in first:
            cp.start()
        passed = [copy(4 + j, (*chip, c), sibling) for j, chip in enumerate(chips)]
        for j, chip in enumerate(chips):
            copy(1 + j, (*chip, c), me).wait_recv()
            passed[j].start()
        copy(0, sibling, me).wait_recv()
        for j, chip in enumerate(chips):
            copy(4 + j, (*chip, 1 - c), me).wait_recv()
        for cp in first + passed:
            cp.wait_send()
        mine.wait()
        total = out_ref[pl.ds(0, m_per), :]
        for d in range(1, N_DEV):
            total = total + out_ref[pl.ds(d * m_per, m_per), :]
        sum_ref[...] = total

    vmem = pl.BlockSpec(memory_space=pltpu.VMEM)
    return pl.pallas_call(
        body, name=name, in_specs=[vmem], out_specs=[vmem, vmem],
        out_shape=[jax.ShapeDtypeStruct((N_DEV * m_per, n), F32), jax.ShapeDtypeStruct((m_per, n), F32)],
        scratch_shapes=[pltpu.SemaphoreType.DMA((7,)), pltpu.SemaphoreType.DMA((7,)), pltpu.SemaphoreType.DMA],
        compiler_params=pltpu.CompilerParams(vmem_limit_bytes=V7X_VMEM_LIMIT),
    )(block)


WEIGHTS = ("rel_bias", "norm1_g", "w_in", "sgu_w", "sgu_b", "dil_qn_g", "dil_kn_g", "conv_w", "conv_b", "conv_ln_g", "conv_ln_b",
           "gqa_qn_g", "gqa_kn_g", "mix_norm_g", "w_out", "norm2_g", "w_gate", "w_up", "w_down")
SHARDED = ("w_in", "w_out", "w_gate", "w_up", "w_down")
COLUMN_SHARDED = ("w_in", "w_gate", "w_up")
REPLICATED = tuple(k for k in WEIGHTS if k not in SHARDED and k != "conv_w")


PACK_ROWS = 256


def _pack(parts):
    flat = jnp.concatenate([p.reshape(-1) for p in parts])
    pad = (-flat.shape[0]) % (PACK_ROWS * LANES)
    return jnp.pad(flat, (0, pad)).reshape(-1, LANES)


def _unpack(buf, shapes):
    flat = buf.reshape(-1)
    out, at = [], 0
    for s in shapes:
        size = math.prod(s)
        out.append(flat[at:at + size].reshape(s))
        at += size
    return out


RS_TM = (256, 128, 64, 32, 16)


def _add_own_seven(own, land, place, name):
    _, h, cols = land.shape
    tm = _pick(h, RS_TM)
    nb = h // tm

    def body(place_ref, own_ref, *refs):
        total = own_ref[...].astype(F32)
        for l_ref in refs[:-1]:
            total = total + l_ref[...].astype(F32)
        refs[-1][...] = total

    slot = lambda r: pl.BlockSpec((None, tm, cols), functools.partial(lambda i, p, r: (jnp.bitwise_xor(p[2], r), i, 0), r=r))
    grid_spec = pltpu.PrefetchScalarGridSpec(
        num_scalar_prefetch=1, grid=(nb,),
        in_specs=[pl.BlockSpec((None, tm, cols), lambda i, p: (p[0], p[1] * nb + i, 0))] + [slot(r) for r in range(1, N_DEV)],
        out_specs=pl.BlockSpec((tm, cols), lambda i, p: (i, 0)))
    return pl.pallas_call(body, name=name, grid_spec=grid_spec, out_shape=jax.ShapeDtypeStruct((h, cols), F32),
                          compiler_params=_cparams(("parallel",)))(place, own, *[land] * (N_DEV - 1))


def _reduce_start(grads, after, tag):
    handle, token = _chips_start(grads, True, after, tag + "start")
    return (handle, grads), token


def _reduce_finish(started, place, after, tag):
    handle, grads = started
    lands = _chips_wait(handle, True, after, tag + "wait")
    totals = [_add_own_seven(g, land, place, f"{tag}sum_{k}") for k, (g, land) in enumerate(zip(grads, lands))]
    return list(zip(totals, _swap_sibling(totals, tag + "share")))


def _adamw_shard(w, m, v, halves, c1, name):
    depth, rows, cols = w.shape
    h = rows // 2
    tm = _pick(h, RS_TM)
    nb = h // tm
    sources = [a for pair in halves for a in pair]

    def body(c_ref, w_ref, m_ref, v_ref, *refs):
        g_refs, (g_out, d_out, m_out, v_out) = refs[:2 * depth], refs[2 * depth:]
        layer, mine = pl.program_id(0), pl.program_id(1) == c_ref[0]
        g = None
        for l in range(depth):
            g_l = jnp.where(mine, g_refs[2 * l][...], g_refs[2 * l + 1][...])
            g = g_l if g is None else jnp.where(layer == l, g_l, g)
        delta, m_new, v_new = _adamw_fn(w_ref[...], g, m_ref[...], v_ref[...])
        g_out[...], d_out[...], m_out[...], v_out[...] = g, delta, m_new, v_new

    def source_spec(l, own):
        def index(layer, half, i, c_ref):
            return (jnp.where((layer == l) & ((half == c_ref[0]) == own), i, 0), 0)
        return pl.BlockSpec((tm, cols), index)

    blk = pl.BlockSpec((None, tm, cols), lambda layer, half, i, c_ref: (layer, half * nb + i, 0))
    grid_spec = pltpu.PrefetchScalarGridSpec(
        num_scalar_prefetch=1, grid=(depth, 2, nb),
        in_specs=[blk, blk, blk] + [source_spec(l, own) for l in range(depth) for own in (True, False)], out_specs=[blk] * 4)
    return pl.pallas_call(body, name=name, grid_spec=grid_spec, out_shape=[jax.ShapeDtypeStruct(w.shape, F32)] * 4,
                          compiler_params=_cparams(("arbitrary", "arbitrary", "arbitrary")))(c1, w, m, v, *sources)


GATHER_GROUPS = ((("w_in",), 0), (("w_out",), None), (("w_gate", "w_up"), None), (("w_down",), None), (("w_in",), 1))
REDUCE_GROUPS = (("w_down", "w_gate", "w_up"), ("w_out",), ("w_in",))


def kernel(x, rel_bias, norm1_g, w_in, sgu_w, sgu_b, dil_qn_g, dil_kn_g, conv_w, conv_b, conv_ln_g, conv_ln_b, gqa_qn_g, gqa_kn_g, mix_norm_g, w_out, norm2_g, w_gate, w_up, w_down, loss_target, m_rel_bias, m_norm1_g, m_w_in, m_sgu_w, m_sgu_b, m_dil_qn_g, m_dil_kn_g, m_conv_w, m_conv_b, m_conv_ln_g, m_conv_ln_b, m_gqa_qn_g, m_gqa_kn_g, m_mix_norm_g, m_w_out, m_norm2_g, m_w_gate, m_w_up, m_w_down, v_rel_bias, v_norm1_g, v_w_in, v_sgu_w, v_sgu_b, v_dil_qn_g, v_dil_kn_g, v_conv_w, v_conv_b, v_conv_ln_g, v_conv_ln_b, v_gqa_qn_g, v_gqa_kn_g, v_mix_norm_g, v_w_out, v_norm2_g, v_w_gate, v_w_up, v_w_down):
    w = dict(rel_bias=rel_bias, norm1_g=norm1_g, w_in=w_in, sgu_w=sgu_w, sgu_b=sgu_b, dil_qn_g=dil_qn_g, dil_kn_g=dil_kn_g,
             conv_w=conv_w, conv_b=conv_b, conv_ln_g=conv_ln_g, conv_ln_b=conv_ln_b, gqa_qn_g=gqa_qn_g, gqa_kn_g=gqa_kn_g,
             mix_norm_g=mix_norm_g, w_out=w_out, norm2_g=norm2_g, w_gate=w_gate, w_up=w_up, w_down=w_down)
    m = dict(rel_bias=m_rel_bias, norm1_g=m_norm1_g, w_in=m_w_in, sgu_w=m_sgu_w, sgu_b=m_sgu_b, dil_qn_g=m_dil_qn_g,
             dil_kn_g=m_dil_kn_g, conv_w=m_conv_w, conv_b=m_conv_b, conv_ln_g=m_conv_ln_g, conv_ln_b=m_conv_ln_b,
             gqa_qn_g=m_gqa_qn_g, gqa_kn_g=m_gqa_kn_g, mix_norm_g=m_mix_norm_g, w_out=m_w_out, norm2_g=m_norm2_g,
             w_gate=m_w_gate, w_up=m_w_up, w_down=m_w_down)
    v = dict(rel_bias=v_rel_bias, norm1_g=v_norm1_g, w_in=v_w_in, sgu_w=v_sgu_w, sgu_b=v_sgu_b, dil_qn_g=v_dil_qn_g,
             dil_kn_g=v_dil_kn_g, conv_w=v_conv_w, conv_b=v_conv_b, conv_ln_g=v_conv_ln_g, conv_ln_b=v_conv_ln_b,
             gqa_qn_g=v_gqa_qn_g, gqa_kn_g=v_gqa_kn_g, mix_norm_g=v_mix_norm_g, w_out=v_w_out, norm2_g=v_norm2_g,
             w_gate=v_w_gate, w_up=v_w_up, w_down=v_w_down)
    bsz = x.shape[0]
    t = bsz * SEQ
    xi, yi, ci = _mesh_pos()
    chip = 2 * xi + yi
    conv_cols = conv_w.shape[-1]

    conv_rows = DEPTH * CONV_WIDTH
    conv_block = jnp.pad(conv_w.reshape(conv_rows, conv_cols), ((0, (-conv_rows) % 8), (0, 0)))
    every, _ = _allgather_sum_small(conv_block, "conv_w_gather")
    every = every.reshape(N_DEV, conv_block.shape[0], conv_cols)
    conv_w_full = jnp.concatenate([every[2 * j, :conv_rows].reshape(DEPTH, CONV_WIDTH, conv_cols) for j in range(N_CHIPS)], axis=-1)

    c1 = jnp.reshape(ci, (1,)).astype(jnp.int32)
    place = jnp.stack([chip, ci, 4 * xi + 2 * yi + ci]).astype(jnp.int32)

    def own_part(k, layer):
        if layer is None:
            return lax.dynamic_index_in_dim(w[k], ci, axis=0, keepdims=False).astype(BF16)
        half = w[k].shape[1] // 2
        return lax.dynamic_slice_in_dim(w[k][layer], ci * half, half, axis=0).astype(BF16)

    fetches, token = [], every
    for gi, (group, layer) in enumerate(GATHER_GROUPS):
        parts = [own_part(k, layer) for k in group]
        handle, token = _chips_start(parts, False, token, f"gather{gi}_start", per_core=True)
        fetches.append((handle, parts))
    all_started = token
    gathered, handed = {}, {}

    def group_of(l, name):
        return [name in group and layer in (None, l) for group, layer in GATHER_GROUPS].index(True)

    def landed(gi, after):
        handle, parts = fetches[gi]
        lands = _chips_wait(handle, False, all_started if after is None else after, f"gather{gi}_wait")
        return [lax.dynamic_update_slice(land, own[None, None], (ci, chip, 0, 0)) for land, own in zip(lands, parts)]

    def prefetch(l, name, after):
        if l < DEPTH and group_of(l, name) not in handed:
            gi = group_of(l, name)
            handed[gi] = _pairs_start(landed(gi, after), after, f"gather{gi}_share_start")

    def big(l, name, after):
        if (l, name) not in gathered:
            gi = group_of(l, name)
            group, layer = GATHER_GROUPS[gi]
            if gi in handed:
                whole = _pairs_wait(handed[gi], after, f"gather{gi}_share_wait")
            else:
                whole = _complete_pairs(landed(gi, after), f"gather{gi}_share")
            for k, g in zip(group, whole):
                rows, cols = g.shape[2:]
                if layer is not None:
                    gathered[layer, k] = g.transpose(0, 2, 1, 3).reshape(2 * rows, N_CHIPS * cols)
                    continue
                for each in range(DEPTH):
                    gathered[each, k] = (g[each].transpose(1, 0, 2).reshape(rows, N_CHIPS * cols) if k in COLUMN_SHARDED
                                         else g[each].reshape(N_CHIPS * rows, cols))
        return gathered[l, name]

    big(0, "w_in", None)

    pending, started, reduced = {}, {}, {}

    def emit(l, name, g):
        pending[l, name] = g
        for gi, group in enumerate(REDUCE_GROUPS):
            if name in group and all((l, k) in pending for k in group):
                started[l, gi], token = _reduce_start([pending[l, k] for k in group], g, f"l{l}_reduce{gi}_")
                return token
        return None

    def finish(l, after):
        for gi, group in enumerate(REDUCE_GROUPS):
            for k, r in zip(group, _reduce_finish(started[l, gi], place, after, f"l{l}_reduce{gi}_")):
                reduced[l, k] = r
            after = reduced[l, group[0]][1]

    def mid_hook(l, a):
        if l + 1 < DEPTH:
            finish(l + 1, a)

    small = {k: w[k] for k in REPLICATED}
    small["conv_w"] = conv_w_full
    loss, dx, small_grads = _local_step(x.reshape(t, D_MODEL), loss_target.reshape(t, D_MODEL), small, big, emit, mid_hook, bsz,
                                        prefetch)
    loss = lax.psum(loss[0, 0], ("x", "y", "c"))

    names = REPLICATED + ("conv_w",)
    shapes = [small_grads[k].shape for k in names]
    packed_grads = _pack([small_grads[k] for k in names])
    eighths = packed_grads.reshape(N_CHIPS, packed_grads.shape[0] // N_CHIPS, LANES)
    small_reduce, _ = _reduce_start([eighths], dx, "small_grads_")
    finish(0, dx)

    grads, deltas, new_m, new_v = {}, {}, {}, {}
    for k in SHARDED:
        grads[k], deltas[k], new_m[k], new_v[k] = _adamw_shard(w[k], m[k], v[k], [reduced[l, k] for l in range(DEPTH)], c1, "adamw_" + k)

    landed = _chips_wait(small_reduce[0], True, new_v[SHARDED[-1]], "small_grads_wait")[0]
    mine = _add_own_seven(eighths, landed, place, "small_grads_sum")
    own, others = _everyone_wait(_everyone_start(mine, mine, "small_grads_spread"), mine, "small_grads_spread_wait")
    summed = lax.dynamic_update_slice_in_dim(others, own[None], 4 * xi + 2 * yi + ci, axis=0).reshape(packed_grads.shape)
    summed_parts = dict(zip(names, _unpack(summed, shapes)))
    rep_shapes = [w[k].shape for k in REPLICATED]
    packed = [_pack([src[k] for k in REPLICATED]) for src in (w, {k: summed_parts[k] for k in REPLICATED}, m, v)]
    d_p, m_p, v_p = _adamw(*packed, "adamw_replicated")
    for k, gk, dk, mk, vk in zip(REPLICATED, _unpack(packed[1], rep_shapes), _unpack(d_p, rep_shapes), _unpack(m_p, rep_shapes),
                                 _unpack(v_p, rep_shapes)):
        grads[k], deltas[k], new_m[k], new_v[k] = gk, dk, mk, vk
    g_conv = lax.dynamic_slice_in_dim(summed_parts["conv_w"], chip * conv_cols, conv_cols, axis=2)
    packed = [_pack([a]) for a in (conv_w, g_conv, m["conv_w"], v["conv_w"])]
    d_p, m_p, v_p = _adamw(*packed, "adamw_conv_w")
    grads["conv_w"] = g_conv
    deltas["conv_w"], new_m["conv_w"], new_v["conv_w"] = (_unpack(a, [conv_w.shape])[0] for a in (d_p, m_p, v_p))

    return (loss, dx.reshape(x.shape), *[grads[k] for k in WEIGHTS], *[deltas[k] for k in WEIGHTS],
            *[new_m[k] for k in WEIGHTS], *[new_v[k] for k in WEIGHTS])
```
